```python
import math
import jax, jax.numpy as jnp
from jax import lax
import numpy as np

D_MODEL = 1024
BATCH = 16
SEQ = 2048
DEPTH = 2

CHUNK = 64
N_MIXERS = 2
D_FF = 2816
PLE_DIM = 256
S5_GROUP = 16
S5_GROUPS = D_MODEL // S5_GROUP
S5_STATE = 64
SB_HEAD_DIM = 64
SB_HEADS = D_MODEL // SB_HEAD_DIM
Q_BLOCK = 128
N_A = (DEPTH + 1) // 2
N_B = DEPTH // 2
EPS = 1e-6
DT_MIN = 1e-3
DT_MAX = 1e-1

kernel_name = 'hybrid_s5_stickbreaking_macaron'


def rmsnorm(x, g):
    xf = x.astype(jnp.float32)
    y = xf * lax.rsqrt(jnp.mean(xf * xf, axis=-1, keepdims=True) + EPS)
    return (y * g.astype(jnp.float32)).astype(x.dtype)


def swiglu(x, w1, w3, w2):
    return (jax.nn.silu(x @ w1) * (x @ w3)) @ w2


def _cmul(ar, ai, br, bi):
    return ar * br - ai * bi, ar * bi + ai * br


def _s5_combine(e1, e2):
    a1r, a1i, b1r, b1i = e1
    a2r, a2i, b2r, b2i = e2
    ar, ai = _cmul(a2r, a2i, a1r, a1i)
    cr, ci = _cmul(a2r, a2i, b1r, b1i)
    return ar, ai, cr + b2r, ci + b2i


def s5_mixer(h, w_in, a_re, a_im, log_dt, b_re, b_im, c_re, c_im, d_skip, w_glu):
    bsz, seq_len, _ = h.shape
    f32 = jnp.float32
    u = h @ w_in
    ug = u.astype(f32).reshape(bsz, seq_len, S5_GROUPS, S5_GROUP)
    lam_re = jnp.minimum(a_re.astype(f32), -1e-4)
    lam_im = a_im.astype(f32)
    dt = jnp.exp(log_dt.astype(f32))[:, None]
    mag = jnp.exp(lam_re * dt)
    abar_re = mag * jnp.cos(lam_im * dt)
    abar_im = mag * jnp.sin(lam_im * dt)
    den = lam_re * lam_re + lam_im * lam_im
    nr = abar_re - 1.0
    ni = abar_im
    fr = (nr * lam_re + ni * lam_im) / den
    fi = (ni * lam_re - nr * lam_im) / den
    bre = b_re.astype(f32)
    bim = b_im.astype(f32)
    bbar_re = fr[..., None] * bre - fi[..., None] * bim
    bbar_im = fr[..., None] * bim + fi[..., None] * bre
    bu_re = jnp.einsum('blgh,gph->blgp', ug, bbar_re)
    bu_im = jnp.einsum('blgh,gph->blgp', ug, bbar_im)
    a_seq_re = jnp.broadcast_to(abar_re, (1, seq_len, S5_GROUPS, S5_STATE))
    a_seq_im = jnp.broadcast_to(abar_im, (1, seq_len, S5_GROUPS, S5_STATE))
    _, _, s_re, s_im = lax.associative_scan(
        _s5_combine, (a_seq_re, a_seq_im, bu_re, bu_im), axis=1)
    y = (jnp.einsum('blgp,ghp->blgh', s_re, c_re.astype(f32))
         - jnp.einsum('blgp,ghp->blgh', s_im, c_im.astype(f32)))
    y = y.reshape(bsz, seq_len, D_MODEL) + d_skip.astype(f32) * u.astype(f32)
    z = jax.nn.gelu(y).astype(h.dtype)
    z_out, z_gate = jnp.split(z @ w_glu, 2, axis=-1)
    return z_out * jax.nn.sigmoid(z_gate)


def stick_breaking_mixer(h, w_qkv, w_o):
    bsz, seq_len, _ = h.shape
    qkv = (h @ w_qkv).reshape(bsz, seq_len, 3, SB_HEADS, SB_HEAD_DIM)
    q = qkv[:, :, 0].transpose(0, 2, 1, 3)
    k = qkv[:, :, 1].transpose(0, 2, 1, 3)
    v = qkv[:, :, 2].transpose(0, 2, 1, 3)
    scale = SB_HEAD_DIM ** -0.5
    outs = []
    for blk in range(seq_len // Q_BLOCK):
        start = blk * Q_BLOCK
        end = start + Q_BLOCK
        qb = q[:, :, start:end]
        kb = k[:, :, :end]
        vb = v[:, :, :end]
        z = jnp.einsum('bhqd,bhkd->bhqk', qb, kb).astype(jnp.float32) * scale
        t_pos = start + jnp.arange(Q_BLOCK)[:, None]
        s_pos = jnp.arange(end)[None, :]
        strict = s_pos < t_pos
        log_keep = jnp.where(strict, jax.nn.log_sigmoid(-z), 0.0)
        later = lax.cumsum(log_keep, axis=3, reverse=True) - log_keep
        att = jnp.where(strict, jnp.exp(jax.nn.log_sigmoid(z) + later), 0.0)
        outs.append(jnp.einsum('bhqk,bhkd->bhqd', att.astype(vb.dtype), vb))
    o = jnp.concatenate(outs, axis=2).transpose(0, 2, 1, 3).reshape(bsz, seq_len, D_MODEL)
    return o @ w_o


def _fwd_setup_inputs(seed: int = 0) -> dict:
    key = jax.random.key(seed)
    ks = jax.random.split(key, 32)
    f32 = jnp.float32
    nrm = lambda k, shape, s: jax.random.normal(k, shape, f32) * s
    gain = lambda k, shape: 1.0 + 0.02 * jax.random.normal(k, shape, f32)
    d_in = D_MODEL ** -0.5
    a_im0 = math.pi * jnp.arange(S5_STATE, dtype=f32)
    return {
        'x': jax.random.normal(ks[0], (BATCH, SEQ, D_MODEL), f32),
        'p': jax.random.normal(ks[1], (DEPTH, BATCH, SEQ, PLE_DIM), f32),
        'ffn1_norm': gain(ks[2], (DEPTH, D_MODEL)),
        'ffn1_w1': nrm(ks[3], (DEPTH, D_MODEL, D_FF), d_in),
        'ffn1_w3': nrm(ks[4], (DEPTH, D_MODEL, D_FF), d_in),
        'ffn1_w2': nrm(ks[5], (DEPTH, D_FF, D_MODEL), D_FF ** -0.5),
        'mix_norm': gain(ks[6], (DEPTH, D_MODEL)),
        'ffn2_norm': gain(ks[7], (DEPTH, D_MODEL)),
        'ffn2_w1': nrm(ks[8], (DEPTH, D_MODEL, D_FF), d_in),
        'ffn2_w3': nrm(ks[9], (DEPTH, D_MODEL, D_FF), d_in),
        'ffn2_w2': nrm(ks[10], (DEPTH, D_FF, D_MODEL), D_FF ** -0.5),
        'ple_norm': gain(ks[11], (DEPTH, D_MODEL)),
        'ple_proj': nrm(ks[12], (DEPTH, PLE_DIM, D_MODEL), PLE_DIM ** -0.5),
        'ple_gate': nrm(ks[13], (DEPTH, D_MODEL, D_MODEL), d_in),
        's5_w_in': nrm(ks[14], (N_A, D_MODEL, D_MODEL), d_in),
        's5_a_re': -0.5 + 0.01 * jax.random.normal(ks[15], (N_A, S5_GROUPS, S5_STATE), f32),
        's5_a_im': a_im0 + 0.01 * jax.random.normal(ks[16], (N_A, S5_GROUPS, S5_STATE), f32),
        's5_log_dt': jax.random.uniform(ks[17], (N_A, S5_GROUPS), f32,
                                        math.log(DT_MIN), math.log(DT_MAX)),
        's5_b_re': nrm(ks[18], (N_A, S5_GROUPS, S5_STATE, S5_GROUP), (2 * S5_GROUP) ** -0.5),
        's5_b_im': nrm(ks[19], (N_A, S5_GROUPS, S5_STATE, S5_GROUP), (2 * S5_GROUP) ** -0.5),
        's5_c_re': nrm(ks[20], (N_A, S5_GROUPS, S5_GROUP, S5_STATE), S5_STATE ** -0.5),
        's5_c_im': nrm(ks[21], (N_A, S5_GROUPS, S5_GROUP, S5_STATE), S5_STATE ** -0.5),
        's5_d': jax.random.normal(ks[22], (N_A, D_MODEL), f32),
        's5_w_glu': nrm(ks[23], (N_A, D_MODEL, 2 * D_MODEL), d_in),
        'sb_w_qkv': nrm(ks[24], (N_B, D_MODEL, 3 * D_MODEL), d_in),
        'sb_w_o': nrm(ks[25], (N_B, D_MODEL, D_MODEL), d_in),
        'final_norm': gain(ks[26], (D_MODEL,)),
    }


def _fwd_reference(x, p, ffn1_norm, ffn1_w1, ffn1_w3, ffn1_w2, mix_norm, ffn2_norm,
              ffn2_w1, ffn2_w3, ffn2_w2, ple_norm, ple_proj, ple_gate,
              s5_w_in, s5_a_re, s5_a_im, s5_log_dt, s5_b_re, s5_b_im,
              s5_c_re, s5_c_im, s5_d, s5_w_glu, sb_w_qkv, sb_w_o, final_norm):
    h = x
    for i in range(DEPTH):
        h = h + 0.5 * swiglu(rmsnorm(h, ffn1_norm[i]), ffn1_w1[i], ffn1_w3[i], ffn1_w2[i])
        hn = rmsnorm(h, mix_norm[i])
        j = i // N_MIXERS
        if i % N_MIXERS == 0:
            m = s5_mixer(hn, s5_w_in[j], s5_a_re[j], s5_a_im[j], s5_log_dt[j],
                         s5_b_re[j], s5_b_im[j], s5_c_re[j], s5_c_im[j],
                         s5_d[j], s5_w_glu[j])
        else:
            m = stick_breaking_mixer(hn, sb_w_qkv[j], sb_w_o[j])
        h = h + m
        h = h + 0.5 * swiglu(rmsnorm(h, ffn2_norm[i]), ffn2_w1[i], ffn2_w3[i], ffn2_w2[i])
        gate = jax.nn.sigmoid(rmsnorm(h, ple_norm[i]) @ ple_gate[i])
        h = h + (p[i].astype(h.dtype) @ ple_proj[i]) * gate
    return rmsnorm(h, final_norm)


import jax as _jax
import jax.numpy as _jnp

TWIN_FORMAT = 'train_step'
FWD_PARAMS = ['x', 'p', 'ffn1_norm', 'ffn1_w1', 'ffn1_w3', 'ffn1_w2', 'mix_norm', 'ffn2_norm', 'ffn2_w1', 'ffn2_w3', 'ffn2_w2', 'ple_norm', 'ple_proj', 'ple_gate', 's5_w_in', 's5_a_re', 's5_a_im', 's5_log_dt', 's5_b_re', 's5_b_im', 's5_c_re', 's5_c_im', 's5_d', 's5_w_glu', 'sb_w_qkv', 'sb_w_o', 'final_norm']
TWIN_WEIGHTS = ['ffn1_norm', 'ffn1_w1', 'ffn1_w3', 'ffn1_w2', 'mix_norm', 'ffn2_norm', 'ffn2_w1', 'ffn2_w3', 'ffn2_w2', 'ple_norm', 'ple_proj', 'ple_gate', 's5_w_in', 's5_a_re', 's5_a_im', 's5_log_dt', 's5_b_re', 's5_b_im', 's5_c_re', 's5_c_im', 's5_d', 's5_w_glu', 'sb_w_qkv', 'sb_w_o', 'final_norm']
TWIN_DIFF_INPUT = 'x'
TWIN_INPUTS = ['x', 'p', 'ffn1_norm', 'ffn1_w1', 'ffn1_w3', 'ffn1_w2', 'mix_norm', 'ffn2_norm', 'ffn2_w1', 'ffn2_w3', 'ffn2_w2', 'ple_norm', 'ple_proj', 'ple_gate', 's5_w_in', 's5_a_re', 's5_a_im', 's5_log_dt', 's5_b_re', 's5_b_im', 's5_c_re', 's5_c_im', 's5_d', 's5_w_glu', 'sb_w_qkv', 'sb_w_o', 'final_norm', 'loss_target', 'm_ffn1_norm', 'm_ffn1_w1', 'm_ffn1_w3', 'm_ffn1_w2', 'm_mix_norm', 'm_ffn2_norm', 'm_ffn2_w1', 'm_ffn2_w3', 'm_ffn2_w2', 'm_ple_norm', 'm_ple_proj', 'm_ple_gate', 'm_s5_w_in', 'm_s5_a_re', 'm_s5_a_im', 'm_s5_log_dt', 'm_s5_b_re', 'm_s5_b_im', 'm_s5_c_re', 'm_s5_c_im', 'm_s5_d', 'm_s5_w_glu', 'm_sb_w_qkv', 'm_sb_w_o', 'm_final_norm', 'v_ffn1_norm', 'v_ffn1_w1', 'v_ffn1_w3', 'v_ffn1_w2', 'v_mix_norm', 'v_ffn2_norm', 'v_ffn2_w1', 'v_ffn2_w3', 'v_ffn2_w2', 'v_ple_norm', 'v_ple_proj', 'v_ple_gate', 'v_s5_w_in', 'v_s5_a_re', 'v_s5_a_im', 'v_s5_log_dt', 'v_s5_b_re', 'v_s5_b_im', 'v_s5_c_re', 'v_s5_c_im', 'v_s5_d', 'v_s5_w_glu', 'v_sb_w_qkv', 'v_sb_w_o', 'v_final_norm']
TWIN_OUTPUTS = ['loss', 'grad_x', 'grad_ffn1_norm', 'grad_ffn1_w1', 'grad_ffn1_w3', 'grad_ffn1_w2', 'grad_mix_norm', 'grad_ffn2_norm', 'grad_ffn2_w1', 'grad_ffn2_w3', 'grad_ffn2_w2', 'grad_ple_norm', 'grad_ple_proj', 'grad_ple_gate', 'grad_s5_w_in', 'grad_s5_a_re', 'grad_s5_a_im', 'grad_s5_log_dt', 'grad_s5_b_re', 'grad_s5_b_im', 'grad_s5_c_re', 'grad_s5_c_im', 'grad_s5_d', 'grad_s5_w_glu', 'grad_sb_w_qkv', 'grad_sb_w_o', 'grad_final_norm', 'delta_ffn1_norm', 'delta_ffn1_w1', 'delta_ffn1_w3', 'delta_ffn1_w2', 'delta_mix_norm', 'delta_ffn2_norm', 'delta_ffn2_w1', 'delta_ffn2_w3', 'delta_ffn2_w2', 'delta_ple_norm', 'delta_ple_proj', 'delta_ple_gate', 'delta_s5_w_in', 'delta_s5_a_re', 'delta_s5_a_im', 'delta_s5_log_dt', 'delta_s5_b_re', 'delta_s5_b_im', 'delta_s5_c_re', 'delta_s5_c_im', 'delta_s5_d', 'delta_s5_w_glu', 'delta_sb_w_qkv', 'delta_sb_w_o', 'delta_final_norm', 'new_m_ffn1_norm', 'new_m_ffn1_w1', 'new_m_ffn1_w3', 'new_m_ffn1_w2', 'new_m_mix_norm', 'new_m_ffn2_norm', 'new_m_ffn2_w1', 'new_m_ffn2_w3', 'new_m_ffn2_w2', 'new_m_ple_norm', 'new_m_ple_proj', 'new_m_ple_gate', 'new_m_s5_w_in', 'new_m_s5_a_re', 'new_m_s5_a_im', 'new_m_s5_log_dt', 'new_m_s5_b_re', 'new_m_s5_b_im', 'new_m_s5_c_re', 'new_m_s5_c_im', 'new_m_s5_d', 'new_m_s5_w_glu', 'new_m_sb_w_qkv', 'new_m_sb_w_o', 'new_m_final_norm', 'new_v_ffn1_norm', 'new_v_ffn1_w1', 'new_v_ffn1_w3', 'new_v_ffn1_w2', 'new_v_mix_norm', 'new_v_ffn2_norm', 'new_v_ffn2_w1', 'new_v_ffn2_w3', 'new_v_ffn2_w2', 'new_v_ple_norm', 'new_v_ple_proj', 'new_v_ple_gate', 'new_v_s5_w_in', 'new_v_s5_a_re', 'new_v_s5_a_im', 'new_v_s5_log_dt', 'new_v_s5_b_re', 'new_v_s5_b_im', 'new_v_s5_c_re', 'new_v_s5_c_im', 'new_v_s5_d', 'new_v_s5_w_glu', 'new_v_sb_w_qkv', 'new_v_sb_w_o', 'new_v_final_norm']
TWIN_LEAF_KINDS = {'loss': 'loss', 'grad_x': 'grad_x', 'grad_ffn1_norm': 'grad_w', 'grad_ffn1_w1': 'grad_w', 'grad_ffn1_w3': 'grad_w', 'grad_ffn1_w2': 'grad_w', 'grad_mix_norm': 'grad_w', 'grad_ffn2_norm': 'grad_w', 'grad_ffn2_w1': 'grad_w', 'grad_ffn2_w3': 'grad_w', 'grad_ffn2_w2': 'grad_w', 'grad_ple_norm': 'grad_w', 'grad_ple_proj': 'grad_w', 'grad_ple_gate': 'grad_w', 'grad_s5_w_in': 'grad_w', 'grad_s5_a_re': 'grad_w', 'grad_s5_a_im': 'grad_w', 'grad_s5_log_dt': 'grad_w', 'grad_s5_b_re': 'grad_w', 'grad_s5_b_im': 'grad_w', 'grad_s5_c_re': 'grad_w', 'grad_s5_c_im': 'grad_w', 'grad_s5_d': 'grad_w', 'grad_s5_w_glu': 'grad_w', 'grad_sb_w_qkv': 'grad_w', 'grad_sb_w_o': 'grad_w', 'grad_final_norm': 'grad_w', 'delta_ffn1_norm': 'delta_w', 'delta_ffn1_w1': 'delta_w', 'delta_ffn1_w3': 'delta_w', 'delta_ffn1_w2': 'delta_w', 'delta_mix_norm': 'delta_w', 'delta_ffn2_norm': 'delta_w', 'delta_ffn2_w1': 'delta_w', 'delta_ffn2_w3': 'delta_w', 'delta_ffn2_w2': 'delta_w', 'delta_ple_norm': 'delta_w', 'delta_ple_proj': 'delta_w', 'delta_ple_gate': 'delta_w', 'delta_s5_w_in': 'delta_w', 'delta_s5_a_re': 'delta_w', 'delta_s5_a_im': 'delta_w', 'delta_s5_log_dt': 'delta_w', 'delta_s5_b_re': 'delta_w', 'delta_s5_b_im': 'delta_w', 'delta_s5_c_re': 'delta_w', 'delta_s5_c_im': 'delta_w', 'delta_s5_d': 'delta_w', 'delta_s5_w_glu': 'delta_w', 'delta_sb_w_qkv': 'delta_w', 'delta_sb_w_o': 'delta_w', 'delta_final_norm': 'delta_w', 'new_m_ffn1_norm': 'new_m', 'new_m_ffn1_w1': 'new_m', 'new_m_ffn1_w3': 'new_m', 'new_m_ffn1_w2': 'new_m', 'new_m_mix_norm': 'new_m', 'new_m_ffn2_norm': 'new_m', 'new_m_ffn2_w1': 'new_m', 'new_m_ffn2_w3': 'new_m', 'new_m_ffn2_w2': 'new_m', 'new_m_ple_norm': 'new_m', 'new_m_ple_proj': 'new_m', 'new_m_ple_gate': 'new_m', 'new_m_s5_w_in': 'new_m', 'new_m_s5_a_re': 'new_m', 'new_m_s5_a_im': 'new_m', 'new_m_s5_log_dt': 'new_m', 'new_m_s5_b_re': 'new_m', 'new_m_s5_b_im': 'new_m', 'new_m_s5_c_re': 'new_m', 'new_m_s5_c_im': 'new_m', 'new_m_s5_d': 'new_m', 'new_m_s5_w_glu': 'new_m', 'new_m_sb_w_qkv': 'new_m', 'new_m_sb_w_o': 'new_m', 'new_m_final_norm': 'new_m', 'new_v_ffn1_norm': 'new_v', 'new_v_ffn1_w1': 'new_v', 'new_v_ffn1_w3': 'new_v', 'new_v_ffn1_w2': 'new_v', 'new_v_mix_norm': 'new_v', 'new_v_ffn2_norm': 'new_v', 'new_v_ffn2_w1': 'new_v', 'new_v_ffn2_w3': 'new_v', 'new_v_ffn2_w2': 'new_v', 'new_v_ple_norm': 'new_v', 'new_v_ple_proj': 'new_v', 'new_v_ple_gate': 'new_v', 'new_v_s5_w_in': 'new_v', 'new_v_s5_a_re': 'new_v', 'new_v_s5_a_im': 'new_v', 'new_v_s5_log_dt': 'new_v', 'new_v_s5_b_re': 'new_v', 'new_v_s5_b_im': 'new_v', 'new_v_s5_c_re': 'new_v', 'new_v_s5_c_im': 'new_v', 'new_v_s5_d': 'new_v', 'new_v_s5_w_glu': 'new_v', 'new_v_sb_w_qkv': 'new_v', 'new_v_sb_w_o': 'new_v', 'new_v_final_norm': 'new_v'}


def _forward(args):
    return _fwd_reference(*[args[k] for k in FWD_PARAMS])


def _output_shape():
    out = _jax.eval_shape(lambda: _forward(_fwd_setup_inputs(0)))
    return out.shape, out.dtype

N_MICROBATCH = 1
ADAM_LR = 0.001
ADAM_B1 = 0.9
ADAM_B2 = 0.999
ADAM_EPS = 1e-08
ADAM_WD = 0.01
ADAM_STEP = 10
PER_EXAMPLE_BATCH_AXIS = {'x': 0, 'p': 1, 'loss_target': 0}
SHARED_INPUTS = []
_WEIGHT_DTYPES = {'ffn1_norm': _jnp.float32, 'ffn1_w1': _jnp.float32, 'ffn1_w3': _jnp.float32, 'ffn1_w2': _jnp.float32, 'mix_norm': _jnp.float32, 'ffn2_norm': _jnp.float32, 'ffn2_w1': _jnp.float32, 'ffn2_w3': _jnp.float32, 'ffn2_w2': _jnp.float32, 'ple_norm': _jnp.float32, 'ple_proj': _jnp.float32, 'ple_gate': _jnp.float32, 's5_w_in': _jnp.float32, 's5_a_re': _jnp.float32, 's5_a_im': _jnp.float32, 's5_log_dt': _jnp.float32, 's5_b_re': _jnp.float32, 's5_b_im': _jnp.float32, 's5_c_re': _jnp.float32, 's5_c_im': _jnp.float32, 's5_d': _jnp.float32, 's5_w_glu': _jnp.float32, 'sb_w_qkv': _jnp.float32, 'sb_w_o': _jnp.float32, 'final_norm': _jnp.float32}
MOMENT_SCALE = {'ffn1_norm': 6.638060e-02, 'ffn1_w1': 2.856737e-02, 'ffn1_w3': 2.764563e-02, 'ffn1_w2': 4.577595e-02, 'mix_norm': 7.747178e-02, 'ffn2_norm': 5.739851e-02, 'ffn2_w1': 2.443562e-02, 'ffn2_w3': 2.364571e-02, 'ffn2_w2': 3.922102e-02, 'ple_norm': 2.776171e-02, 'ple_proj': 7.044114e-02, 'ple_gate': 2.763294e-02, 's5_w_in': 5.974369e-02, 's5_a_re': 4.081517e-03, 's5_a_im': 4.564438e-03, 's5_log_dt': 2.049246e+00, 's5_b_re': 2.776796e-03, 's5_b_im': 2.801044e-03, 's5_c_re': 3.975006e-03, 's5_c_im': 3.974279e-03, 's5_d': 6.383632e-02, 's5_w_glu': 4.238852e-02, 'sb_w_qkv': 5.349031e-02, 'sb_w_o': 7.793581e-02, 'final_norm': 3.195667e+01}


def _to_microbatches(a, axis):
    t = _jnp.moveaxis(a, axis, 0)
    t = t.reshape((N_MICROBATCH, t.shape[0] // N_MICROBATCH) + t.shape[1:])
    return _jnp.moveaxis(t, 1, axis + 1)


def setup_inputs(seed: int = 0) -> dict:
    inp = _fwd_setup_inputs(seed)
    key = _jax.random.fold_in(_jax.random.key(seed), 7919)
    shape, _ = _output_shape()
    out = dict(inp)
    out["loss_target"] = _jax.random.normal(_jax.random.fold_in(key, 0), shape, _jnp.float32)
    for i, name in enumerate(TWIN_WEIGHTS):
        w = inp[name].astype(_jnp.float32)
        if MOMENT_SCALE is None:
            s = _jnp.sqrt(_jnp.mean(_jnp.square(w)) + 1e-30)
        else:
            s = MOMENT_SCALE[name]
        km, kv = _jax.random.split(_jax.random.fold_in(key, i + 1))
        out[name] = w
        out["m_" + name] = s * _jax.random.normal(km, w.shape, _jnp.float32)
        out["v_" + name] = (s * s) * _jax.random.uniform(kv, w.shape, _jnp.float32, 0.5, 1.5)
    if N_MICROBATCH > 1:
        for name, axis in PER_EXAMPLE_BATCH_AXIS.items():
            out[name] = _to_microbatches(out[name], axis)
    return {'x': out['x'], 'p': out['p'], 'ffn1_norm': out['ffn1_norm'], 'ffn1_w1': out['ffn1_w1'], 'ffn1_w3': out['ffn1_w3'], 'ffn1_w2': out['ffn1_w2'], 'mix_norm': out['mix_norm'], 'ffn2_norm': out['ffn2_norm'], 'ffn2_w1': out['ffn2_w1'], 'ffn2_w3': out['ffn2_w3'], 'ffn2_w2': out['ffn2_w2'], 'ple_norm': out['ple_norm'], 'ple_proj': out['ple_proj'], 'ple_gate': out['ple_gate'], 's5_w_in': out['s5_w_in'], 's5_a_re': out['s5_a_re'], 's5_a_im': out['s5_a_im'], 's5_log_dt': out['s5_log_dt'], 's5_b_re': out['s5_b_re'], 's5_b_im': out['s5_b_im'], 's5_c_re': out['s5_c_re'], 's5_c_im': out['s5_c_im'], 's5_d': out['s5_d'], 's5_w_glu': out['s5_w_glu'], 'sb_w_qkv': out['sb_w_qkv'], 'sb_w_o': out['sb_w_o'], 'final_norm': out['final_norm'], 'loss_target': out['loss_target'], 'm_ffn1_norm': out['m_ffn1_norm'], 'm_ffn1_w1': out['m_ffn1_w1'], 'm_ffn1_w3': out['m_ffn1_w3'], 'm_ffn1_w2': out['m_ffn1_w2'], 'm_mix_norm': out['m_mix_norm'], 'm_ffn2_norm': out['m_ffn2_norm'], 'm_ffn2_w1': out['m_ffn2_w1'], 'm_ffn2_w3': out['m_ffn2_w3'], 'm_ffn2_w2': out['m_ffn2_w2'], 'm_ple_norm': out['m_ple_norm'], 'm_ple_proj': out['m_ple_proj'], 'm_ple_gate': out['m_ple_gate'], 'm_s5_w_in': out['m_s5_w_in'], 'm_s5_a_re': out['m_s5_a_re'], 'm_s5_a_im': out['m_s5_a_im'], 'm_s5_log_dt': out['m_s5_log_dt'], 'm_s5_b_re': out['m_s5_b_re'], 'm_s5_b_im': out['m_s5_b_im'], 'm_s5_c_re': out['m_s5_c_re'], 'm_s5_c_im': out['m_s5_c_im'], 'm_s5_d': out['m_s5_d'], 'm_s5_w_glu': out['m_s5_w_glu'], 'm_sb_w_qkv': out['m_sb_w_qkv'], 'm_sb_w_o': out['m_sb_w_o'], 'm_final_norm': out['m_final_norm'], 'v_ffn1_norm': out['v_ffn1_norm'], 'v_ffn1_w1': out['v_ffn1_w1'], 'v_ffn1_w3': out['v_ffn1_w3'], 'v_ffn1_w2': out['v_ffn1_w2'], 'v_mix_norm': out['v_mix_norm'], 'v_ffn2_norm': out['v_ffn2_norm'], 'v_ffn2_w1': out['v_ffn2_w1'], 'v_ffn2_w3': out['v_ffn2_w3'], 'v_ffn2_w2': out['v_ffn2_w2'], 'v_ple_norm': out['v_ple_norm'], 'v_ple_proj': out['v_ple_proj'], 'v_ple_gate': out['v_ple_gate'], 'v_s5_w_in': out['v_s5_w_in'], 'v_s5_a_re': out['v_s5_a_re'], 'v_s5_a_im': out['v_s5_a_im'], 'v_s5_log_dt': out['v_s5_log_dt'], 'v_s5_b_re': out['v_s5_b_re'], 'v_s5_b_im': out['v_s5_b_im'], 'v_s5_c_re': out['v_s5_c_re'], 'v_s5_c_im': out['v_s5_c_im'], 'v_s5_d': out['v_s5_d'], 'v_s5_w_glu': out['v_s5_w_glu'], 'v_sb_w_qkv': out['v_sb_w_qkv'], 'v_sb_w_o': out['v_sb_w_o'], 'v_final_norm': out['v_final_norm']}


def _loss(weights, diff, rest, loss_target):
    with _jax.named_scope("forward"):
        args = {**rest, TWIN_DIFF_INPUT: diff, **{k: w.astype(_WEIGHT_DTYPES[k]) for k, w in weights.items()}}
        y = _forward(args)
    with _jax.named_scope("loss_head"):
        err = _jnp.square(y.astype(_jnp.float32) - loss_target)
        return 0.5 * _jnp.sum(_jnp.mean(err, axis=-1)) if err.ndim else 0.5 * err


def _adamw(w, g, m, v):
    m = ADAM_B1 * m + (1.0 - ADAM_B1) * g
    v = ADAM_B2 * v + (1.0 - ADAM_B2) * _jnp.square(g)
    m_hat = m / (1.0 - ADAM_B1 ** ADAM_STEP)
    v_hat = v / (1.0 - ADAM_B2 ** ADAM_STEP)
    delta = -ADAM_LR * (m_hat / (_jnp.sqrt(v_hat) + ADAM_EPS) + ADAM_WD * w)
    return delta, m, v


def reference(x, p, ffn1_norm, ffn1_w1, ffn1_w3, ffn1_w2, mix_norm, ffn2_norm, ffn2_w1, ffn2_w3, ffn2_w2, ple_norm, ple_proj, ple_gate, s5_w_in, s5_a_re, s5_a_im, s5_log_dt, s5_b_re, s5_b_im, s5_c_re, s5_c_im, s5_d, s5_w_glu, sb_w_qkv, sb_w_o, final_norm, loss_target, m_ffn1_norm, m_ffn1_w1, m_ffn1_w3, m_ffn1_w2, m_mix_norm, m_ffn2_norm, m_ffn2_w1, m_ffn2_w3, m_ffn2_w2, m_ple_norm, m_ple_proj, m_ple_gate, m_s5_w_in, m_s5_a_re, m_s5_a_im, m_s5_log_dt, m_s5_b_re, m_s5_b_im, m_s5_c_re, m_s5_c_im, m_s5_d, m_s5_w_glu, m_sb_w_qkv, m_sb_w_o, m_final_norm, v_ffn1_norm, v_ffn1_w1, v_ffn1_w3, v_ffn1_w2, v_mix_norm, v_ffn2_norm, v_ffn2_w1, v_ffn2_w3, v_ffn2_w2, v_ple_norm, v_ple_proj, v_ple_gate, v_s5_w_in, v_s5_a_re, v_s5_a_im, v_s5_log_dt, v_s5_b_re, v_s5_b_im, v_s5_c_re, v_s5_c_im, v_s5_d, v_s5_w_glu, v_sb_w_qkv, v_sb_w_o, v_final_norm):
    given = dict(x=x, p=p, ffn1_norm=ffn1_norm, ffn1_w1=ffn1_w1, ffn1_w3=ffn1_w3, ffn1_w2=ffn1_w2, mix_norm=mix_norm, ffn2_norm=ffn2_norm, ffn2_w1=ffn2_w1, ffn2_w3=ffn2_w3, ffn2_w2=ffn2_w2, ple_norm=ple_norm, ple_proj=ple_proj, ple_gate=ple_gate, s5_w_in=s5_w_in, s5_a_re=s5_a_re, s5_a_im=s5_a_im, s5_log_dt=s5_log_dt, s5_b_re=s5_b_re, s5_b_im=s5_b_im, s5_c_re=s5_c_re, s5_c_im=s5_c_im, s5_d=s5_d, s5_w_glu=s5_w_glu, sb_w_qkv=sb_w_qkv, sb_w_o=sb_w_o, final_norm=final_norm, loss_target=loss_target, m_ffn1_norm=m_ffn1_norm, m_ffn1_w1=m_ffn1_w1, m_ffn1_w3=m_ffn1_w3, m_ffn1_w2=m_ffn1_w2, m_mix_norm=m_mix_norm, m_ffn2_norm=m_ffn2_norm, m_ffn2_w1=m_ffn2_w1, m_ffn2_w3=m_ffn2_w3, m_ffn2_w2=m_ffn2_w2, m_ple_norm=m_ple_norm, m_ple_proj=m_ple_proj, m_ple_gate=m_ple_gate, m_s5_w_in=m_s5_w_in, m_s5_a_re=m_s5_a_re, m_s5_a_im=m_s5_a_im, m_s5_log_dt=m_s5_log_dt, m_s5_b_re=m_s5_b_re, m_s5_b_im=m_s5_b_im, m_s5_c_re=m_s5_c_re, m_s5_c_im=m_s5_c_im, m_s5_d=m_s5_d, m_s5_w_glu=m_s5_w_glu, m_sb_w_qkv=m_sb_w_qkv, m_sb_w_o=m_sb_w_o, m_final_norm=m_final_norm, v_ffn1_norm=v_ffn1_norm, v_ffn1_w1=v_ffn1_w1, v_ffn1_w3=v_ffn1_w3, v_ffn1_w2=v_ffn1_w2, v_mix_norm=v_mix_norm, v_ffn2_norm=v_ffn2_norm, v_ffn2_w1=v_ffn2_w1, v_ffn2_w3=v_ffn2_w3, v_ffn2_w2=v_ffn2_w2, v_ple_norm=v_ple_norm, v_ple_proj=v_ple_proj, v_ple_gate=v_ple_gate, v_s5_w_in=v_s5_w_in, v_s5_a_re=v_s5_a_re, v_s5_a_im=v_s5_a_im, v_s5_log_dt=v_s5_log_dt, v_s5_b_re=v_s5_b_re, v_s5_b_im=v_s5_b_im, v_s5_c_re=v_s5_c_re, v_s5_c_im=v_s5_c_im, v_s5_d=v_s5_d, v_s5_w_glu=v_s5_w_glu, v_sb_w_qkv=v_sb_w_qkv, v_sb_w_o=v_sb_w_o, v_final_norm=v_final_norm)
    weights = {n: given[n] for n in TWIN_WEIGHTS}
    shared = {n: given[n] for n in SHARED_INPUTS}
    per_example = {n: given[n] for n in ['x', 'p']}
    grad_fn = _jax.value_and_grad(_loss, argnums=(0, 1))

    def one_microbatch(ex, loss_target):
        ex = dict(ex)
        diff = ex.pop(TWIN_DIFF_INPUT)
        return grad_fn(weights, diff, {**shared, **ex}, loss_target)

    if N_MICROBATCH == 1:
        loss, (grad_w, grad_x) = one_microbatch(per_example, given["loss_target"])
    else:
        def body(carry, xs):
            loss_sum, grad_sum = carry
            l_k, (gw_k, gx_k) = one_microbatch(xs[0], xs[1])
            with _jax.named_scope("update"):
                return (loss_sum + l_k, _jax.tree.map(_jnp.add, grad_sum, gw_k)), gx_k

        init = (_jnp.zeros((), _jnp.float32), _jax.tree.map(_jnp.zeros_like, weights))
        (loss, grad_w), grad_x = _jax.lax.scan(body, init, (per_example, given["loss_target"]))
    with _jax.named_scope("update"):
        delta_w, new_m, new_v = {}, {}, {}
        for n in TWIN_WEIGHTS:
            delta_w[n], new_m[n], new_v[n] = _adamw(weights[n], grad_w[n], given["m_" + n], given["v_" + n])
    return (loss, grad_x, *[grad_w[n] for n in TWIN_WEIGHTS], *[delta_w[n] for n in TWIN_WEIGHTS],
            *[new_m[n] for n in TWIN_WEIGHTS], *[new_v[n] for n in TWIN_WEIGHTS])
```

```python
import functools
import math

import jax
import jax.numpy as jnp
from jax import lax
from jax.experimental import pallas as pl
from jax.experimental.pallas import tpu as pltpu

F32 = jnp.float32
BF16 = jnp.bfloat16
MESH = pl.DeviceIdType.MESH

N_CHIPS = 4
N_DEV = 8
RMS_EPS = 1e-6
S5_GROUP = 16
S5_STATE = 64
S5_CHUNK = 16
SB_HEAD_DIM = 64
SB_BLOCK = 128
ADAM_LR, ADAM_B1, ADAM_B2, ADAM_EPS, ADAM_WD, ADAM_STEP = 0.001, 0.9, 0.999, 1e-08, 0.01, 10
VMEM_LIMIT = 48 * 1024 * 1024

NN = (((1,), (0,)), ((), ()))
NT = (((1,), (1,)), ((), ()))
TN = (((0,), (0,)), ((), ()))

ANY = pl.BlockSpec(memory_space=pl.ANY)


def _tile(n, target):
    if n <= target:
        return n
    for t in range(target - target % 8, 7, -8):
        if n % t == 0:
            return t
    raise ValueError(f"no row tile for {n}")


def _params(*semantics):
    return pltpu.CompilerParams(dimension_semantics=semantics, vmem_limit_bytes=VMEM_LIMIT)


def _sigmoid(v):
    return 1.0 / (1.0 + jnp.exp(-v))


def _gemm(name, grid, operands, in_specs, groups, acc_shapes, out_shapes, out_specs, epilogue, reduce_axis=None):
    n_in, n_out = len(operands), len(out_shapes)
    n_red = None if reduce_axis is None else grid[reduce_axis]

    def body(*refs):
        ins, outs, accs = refs[:n_in], refs[n_in:n_in + n_out], refs[n_in + n_out:]

        def products():
            res = []
            for terms in groups:
                tot = None
                for ia, ib, dims in terms:
                    d = lax.dot_general(ins[ia][...], ins[ib][...], dims, preferred_element_type=F32)
                    tot = d if tot is None else tot + d
                res.append(tot)
            return res

        def finish(vals):
            for o, v in zip(outs, epilogue(vals, ins)):
                o[...] = v.astype(o.dtype)

        if reduce_axis is None:
            finish(products())
        else:
            k = pl.program_id(reduce_axis)

            @pl.when(k == 0)
            def _():
                for a in accs:
                    a[...] = jnp.zeros_like(a)

            for a, d in zip(accs, products()):
                a[...] += d

            @pl.when(k == n_red - 1)
            def _():
                finish([a[...] for a in accs])

    scratch = [] if reduce_axis is None else [pltpu.VMEM(s, F32) for s in acc_shapes]
    sem = tuple("arbitrary" if i == reduce_axis else "parallel" for i in range(len(grid)))
    return pl.pallas_call(
        body, name=name, grid=grid, in_specs=in_specs, out_specs=out_specs, out_shape=out_shapes,
        scratch_shapes=scratch, compiler_params=_params(*sem))(*operands)


def _ident(vals, ins):
    return vals


def _act_spec(layout, tm, cs, pos):
    if layout == "sm":
        return pl.BlockSpec((None, tm, cs), lambda *g: (pos(*g)[1], pos(*g)[0], 0))
    return pl.BlockSpec((tm, cs), lambda *g: pos(*g))


def _act_shape(layout, t, cs, dtype):
    return jax.ShapeDtypeStruct((N_CHIPS, t, cs) if layout == "sm" else (t, N_CHIPS * cs), dtype)


def _w_spec(w, layer, pos_k):
    _, _, r, c = w.shape
    return pl.BlockSpec((None, None, r, c), lambda *g: (pos_k(*g), layer, 0, 0))


def _mm_cs(name, x, w, layer, out_layout, out_dtype, tm=512):
    t, kd = x.shape
    cs = w.shape[3]
    tm = _tile(t, tm)
    return _gemm(
        name, (N_CHIPS, t // tm), [x, w],
        [pl.BlockSpec((tm, kd), lambda k, i: (i, 0)), _w_spec(w, layer, lambda k, i: k)],
        [[(0, 1, NN)]], None, [_act_shape(out_layout, t, cs, out_dtype)],
        [_act_spec(out_layout, tm, cs, lambda k, i: (i, k))], _ident)[0]


def _mm_rs(name, xs, layout, w, layer, res=None, alpha=1.0, out_dtype=F32, tm=512):
    ks, n = w.shape[2], w.shape[3]
    t = xs.shape[1] if layout == "sm" else xs.shape[0]
    tm = _tile(t, tm)
    operands = [xs, w] + ([] if res is None else [res])
    specs = [_act_spec(layout, tm, ks, lambda i, k: (i, k)), _w_spec(w, layer, lambda i, k: k)]
    if res is not None:
        specs.append(pl.BlockSpec((tm, n), lambda i, k: (i, 0)))

    def epilogue(vals, ins):
        y = alpha * vals[0]
        return [y if res is None else ins[2][...] + y]

    return _gemm(
        name, (t // tm, N_CHIPS), operands, specs, [[(0, 1, NN)]], [(tm, n)],
        [jax.ShapeDtypeStruct((t, n), out_dtype)], [pl.BlockSpec((tm, n), lambda i, k: (i, 0))],
        epilogue, reduce_axis=1)[0]


def _mm_cs_dx(name, pairs, layout, layer, tm=512):
    w0 = pairs[0][1]
    kd, cs = w0.shape[2], w0.shape[3]
    dy0 = pairs[0][0]
    t = dy0.shape[1] if layout == "sm" else dy0.shape[0]
    tm = _tile(t, tm)
    operands, specs, terms = [], [], []
    for dy, w in pairs:
        terms.append((len(operands), len(operands) + 1, NT))
        operands += [dy, w]
        specs += [_act_spec(layout, tm, cs, lambda i, k: (i, k)), _w_spec(w, layer, lambda i, k: k)]
    return _gemm(
        name, (t // tm, N_CHIPS), operands, specs, [terms], [(tm, kd)],
        [jax.ShapeDtypeStruct((t, kd), F32)], [pl.BlockSpec((tm, kd), lambda i, k: (i, 0))],
        _ident, reduce_axis=1)[0]


def _mm_rs_dx(name, dy, w, layer, out_layout, out_dtype, tm=512):
    t, n = dy.shape
    ks = w.shape[2]
    tm = _tile(t, tm)
    return _gemm(
        name, (N_CHIPS, t // tm), [dy, w],
        [pl.BlockSpec((tm, n), lambda k, i: (i, 0)), _w_spec(w, layer, lambda k, i: k)],
        [[(0, 1, NT)]], None, [_act_shape(out_layout, t, ks, out_dtype)],
        [_act_spec(out_layout, tm, ks, lambda k, i: (i, k))], _ident)[0]


def _mm_dw(name, x, x_layout, dy, dy_layout, alpha=1.0, tk=512):
    if x_layout is None:
        t, rows = x.shape
        cols = dy.shape[2] if dy_layout == "sm" else dy.shape[1] // N_CHIPS
        tk = _tile(t, tk)
        xspec = pl.BlockSpec((tk, rows), lambda k, j: (j, 0))
        yspec = _act_spec(dy_layout, tk, cols, lambda k, j: (j, k))
    else:
        t, cols = dy.shape
        rows = x.shape[2] if x_layout == "sm" else x.shape[1] // N_CHIPS
        tk = _tile(t, tk)
        xspec = _act_spec(x_layout, tk, rows, lambda k, j: (j, k))
        yspec = pl.BlockSpec((tk, cols), lambda k, j: (j, 0))
    return _gemm(
        name, (N_CHIPS, t // tk), [x, dy], [xspec, yspec], [[(0, 1, TN)]], [(rows, cols)],
        [jax.ShapeDtypeStruct((N_CHIPS, rows, cols), F32)],
        [pl.BlockSpec((None, rows, cols), lambda k, j: (k, 0, 0))],
        lambda vals, ins: [alpha * vals[0]], reduce_axis=1)[0]


def _rows(name, fn, ins, outs, accs=(), tm=256):
    t = ins[0].shape[0]
    tm = _tile(t, tm)
    n_in, n_out, n_acc = len(ins), len(outs), len(accs)
    in_specs = []
    for a in ins:
        if a.shape[0] == t:
            in_specs.append(pl.BlockSpec((tm, a.shape[1]), lambda i: (i, 0)))
        else:
            in_specs.append(pl.BlockSpec(a.shape, lambda i: (0, 0)))
    out_shape = [jax.ShapeDtypeStruct((t, c), d) for c, d in outs] + [jax.ShapeDtypeStruct(s, F32) for s in accs]
    out_specs = [pl.BlockSpec((tm, c), lambda i: (i, 0)) for c, _ in outs] + [pl.BlockSpec(s, lambda i: (0, 0)) for s in accs]

    def body(*refs):
        i = pl.program_id(0)
        row_vals, acc_vals = fn(*[r[...] for r in refs[:n_in]])
        for o, v in zip(refs[n_in:n_in + n_out], row_vals):
            o[...] = v.astype(o.dtype)
        acc_refs = refs[n_in + n_out:]
        if n_acc:
            @pl.when(i == 0)
            def _():
                for a in acc_refs:
                    a[...] = jnp.zeros_like(a)

            for a, v in zip(acc_refs, acc_vals):
                a[...] += v

    res = pl.pallas_call(
        body, name=name, grid=(t // tm,), in_specs=in_specs, out_specs=out_specs, out_shape=out_shape,
        compiler_params=_params("arbitrary" if n_acc else "parallel"))(*ins)
    return res[:n_out], res[n_out:]


def _rms_stats(x):
    return lax.rsqrt(jnp.mean(x * x, axis=-1, keepdims=True) + RMS_EPS)


def _rmsnorm(name, h, g):
    def fn(x, gg):
        return [x * _rms_stats(x) * gg], []
    return _rows(name, fn, [h, g], [(h.shape[1], BF16)])[0][0]


def _rms_bwd_math(dn, x, g):
    r = _rms_stats(x)
    xhat = x * r
    dxh = dn * g
    dx = r * (dxh - xhat * jnp.mean(dxh * xhat, axis=-1, keepdims=True))
    return dx, jnp.sum(dn * xhat, axis=0, keepdims=True)


def _rmsnorm_bwd(name, dres, dn, h, g):
    def fn(dr, d, x, gg):
        dx, dg = _rms_bwd_math(d, x, gg)
        return [dr + dx], [dg]
    (dh,), (dg,) = _rows(name, fn, [dres, dn, h, g], [(h.shape[1], F32)], [(1, h.shape[1])])
    return dh, dg


def _ffn_fwd(h, g, w1, w3, w2, layer, tm=512):
    t, d = h.shape
    fs = w1.shape[3]
    n = _rmsnorm("ffn_norm", h, g)
    tm = _tile(t, tm)

    def up(vals, ins):
        a, b = vals
        return [a, b, a * _sigmoid(a) * b]

    sm = _act_shape("sm", t, fs, BF16)
    osp = _act_spec("sm", tm, fs, lambda k, i: (i, k))
    a, b, s = _gemm(
        "ffn_up", (N_CHIPS, t // tm), [n, w1, w3],
        [pl.BlockSpec((tm, d), lambda k, i: (i, 0)), _w_spec(w1, layer, lambda k, i: k), _w_spec(w3, layer, lambda k, i: k)],
        [[(0, 1, NN)], [(0, 2, NN)]], None, [sm, sm, sm], [osp, osp, osp], up)
    out = _mm_rs("ffn_down", s, "sm", w2, layer, res=h, alpha=0.5)
    return out, (h, n, a, b, s)


def _ffn_bwd(dout, saved, g, w1, w3, w2, layer, tm=512):
    h, n, a, b, s = saved
    t, d = h.shape
    fs = w1.shape[3]
    tm = _tile(t, tm)
    dob = dout.astype(BF16)

    def down(vals, ins):
        ds = 0.5 * vals[0]
        av, bv = ins[2][...].astype(F32), ins[3][...].astype(F32)
        sg = _sigmoid(av)
        return [ds * bv * sg * (1.0 + av * (1.0 - sg)), ds * av * sg]

    sm = _act_shape("sm", t, fs, BF16)
    asp = _act_spec("sm", tm, fs, lambda k, i: (i, k))
    da, db = _gemm(
        "ffn_down_dx", (N_CHIPS, t // tm), [dob, w2, a, b],
        [pl.BlockSpec((tm, d), lambda k, i: (i, 0)), _w_spec(w2, layer, lambda k, i: k), asp, asp],
        [[(0, 1, NT)]], None, [sm, sm], [asp, asp], down)
    dw2 = _mm_dw("ffn_dw2", s, "sm", dob, None, alpha=0.5)
    dw1 = _mm_dw("ffn_dw1", n, None, da, "sm")
    dw3 = _mm_dw("ffn_dw3", n, None, db, "sm")
    dn = _mm_cs_dx("ffn_up_dx", [(da, w1), (db, w3)], "sm", layer)
    dh, dg = _rmsnorm_bwd("ffn_norm_bwd", dout, dn, h, g)
    return dh, dg, dw1, dw3, dw2


def _ple_fwd(h, g, p2, wproj, wgate, layer):
    n = _rmsnorm("ple_norm", h, g)
    gl = _mm_rs("ple_gate", n, "flat", wgate, layer)
    pp = _mm_cs("ple_proj", p2, wproj, layer, "flat", F32)

    def fn(hh, gg, q):
        return [hh + q * _sigmoid(gg)], []
    out = _rows("ple_mix", fn, [h, gl, pp], [(h.shape[1], F32)])[0][0]
    return out, (h, n, gl, pp)


def _ple_bwd(dout, saved, g, p2, wproj, wgate, layer):
    h, n, gl, pp = saved
    d = h.shape[1]

    def fn(do, gg, q):
        sg = _sigmoid(gg)
        return [do * sg, do * q * sg * (1.0 - sg)], []
    (dpp, dgl), _ = _rows("ple_mix_bwd", fn, [dout, gl, pp], [(d, BF16), (d, BF16)])
    dwproj = _mm_dw("ple_dwproj", p2, None, dpp, "flat")
    dwgate = _mm_dw("ple_dwgate", n, "flat", dgl, None)
    dn = _mm_rs_dx("ple_gate_dx", dgl, wgate, layer, "flat", F32)
    dh, dg = _rmsnorm_bwd("ple_norm_bwd", dout, dn, h, g)
    return dh, dg, dwproj, dwgate


def _head(h, g, target):
    d = h.shape[1]

    def fn(x, gg, tg):
        y = x * _rms_stats(x) * gg
        err = y - tg
        dy = err * (1.0 / d)
        dx, dg = _rms_bwd_math(dy, x, gg)
        loss = 0.5 * jnp.sum(jnp.sum(err * err, axis=-1, keepdims=True) * (1.0 / d), axis=0, keepdims=True)
        return [dx], [dg, jnp.broadcast_to(loss, (1, 128))]
    (dh,), (dg, loss) = _rows("loss_head", fn, [h, g, target], [(d, F32)], [(1, d), (1, 128)])
    return loss[0, 0], dh, dg


def _s5_prep(a_re, a_im, log_dt, b_re, b_im, c_re, c_im):
    c = S5_CHUNK
    lam_re = jnp.minimum(a_re, -1e-4)
    lam_im = a_im
    dt = jnp.exp(log_dt)[:, None]
    ks = jnp.arange(c + 1, dtype=F32)[:, None, None]
    mag = jnp.exp(lam_re[None] * dt[None] * ks)
    ph = lam_im[None] * dt[None] * ks
    pw_re, pw_im = mag * jnp.cos(ph), mag * jnp.sin(ph)
    den = lam_re * lam_re + lam_im * lam_im
    nr, ni = pw_re[1] - 1.0, pw_im[1]
    fr = (nr * lam_re + ni * lam_im) / den
    fi = (ni * lam_re - nr * lam_im) / den
    bb_re = fr[..., None] * b_re - fi[..., None] * b_im
    bb_im = fr[..., None] * b_im + fi[..., None] * b_re
    ca_re = c_re[None] * pw_re[:, :, None, :] - c_im[None] * pw_im[:, :, None, :]
    ca_im = c_re[None] * pw_im[:, :, None, :] + c_im[None] * pw_re[:, :, None, :]
    hp = lax.Precision.HIGHEST
    kern = (jnp.einsum("kghp,gpj->kghj", ca_re[:c], bb_re, precision=hp)
            - jnp.einsum("kghp,gpj->kghj", ca_im[:c], bb_im, precision=hp))
    lag = jnp.arange(c)[None, :] - jnp.arange(c)[:, None]
    toep = jnp.where((lag >= 0)[:, :, None, None, None], kern[jnp.clip(lag, 0, c - 1)], 0.0)
    g = a_re.shape[0]
    wi = toep.transpose(2, 0, 4, 1, 3).reshape(g, c * S5_GROUP, c * S5_GROUP)
    rev_re, rev_im = pw_re[c - 1::-1][:c], pw_im[c - 1::-1][:c]
    wn_re = rev_re[..., None] * bb_re[None] - rev_im[..., None] * bb_im[None]
    wn_im = rev_re[..., None] * bb_im[None] + rev_im[..., None] * bb_re[None]
    wn = jnp.stack([wn_re, wn_im], axis=0).transpose(2, 1, 4, 0, 3).reshape(g, c * S5_GROUP, 2 * S5_STATE)
    wo = jnp.stack([ca_re[1:], -ca_im[1:]], axis=0).transpose(2, 0, 4, 1, 3).reshape(g, 2 * S5_STATE, c * S5_GROUP)
    ar, ai = pw_re[c], pw_im[c]
    m1 = jnp.concatenate([ar, ar], axis=1)
    m2 = jnp.concatenate([-ai, ai], axis=1)
    return jnp.concatenate([wi, wn], axis=2), wo, m1, m2


def _bmm(name, a, b, dims, out_dtype, gb=8):
    g = a.shape[0]
    gb = min(gb, g)
    m = a.shape[2] if dims == TN else a.shape[1]
    n = b.shape[1] if dims == NT else b.shape[2]

    def body(a_ref, b_ref, o_ref):
        for j in range(gb):
            o_ref[j] = lax.dot_general(a_ref[j], b_ref[j], dims, preferred_element_type=F32).astype(o_ref.dtype)

    return pl.pallas_call(
        body, name=name, grid=(g // gb,),
        in_specs=[pl.BlockSpec((gb,) + a.shape[1:], lambda i: (i, 0, 0)), pl.BlockSpec((gb,) + b.shape[1:], lambda i: (i, 0, 0))],
        out_specs=pl.BlockSpec((gb, m, n), lambda i: (i, 0, 0)),
        out_shape=jax.ShapeDtypeStruct((g, m, n), out_dtype), compiler_params=_params("parallel"))(a, b)


def _s5_scan_fwd(sloc, m1, m2):
    nc, r, w = sloc.shape

    def body(s_ref, m1_ref, m2_ref, o_ref):
        a1, a2 = m1_ref[...], m2_ref[...]

        def step(c, s):
            o_ref[c] = s
            return a1 * s + a2 * pltpu.roll(s, S5_STATE, 1) + s_ref[c]
        lax.fori_loop(0, nc, step, jnp.zeros((r, w), F32))

    vm = pl.BlockSpec(memory_space=pltpu.VMEM)
    return pl.pallas_call(
        body, name="s5_scan", in_specs=[vm, vm, vm], out_specs=vm,
        out_shape=jax.ShapeDtypeStruct(sloc.shape, F32),
        compiler_params=pltpu.CompilerParams(vmem_limit_bytes=VMEM_LIMIT))(sloc, m1, m2)


def _s5_scan_bwd(dsprev, sprev, m1, m2):
    nc, r, w = dsprev.shape

    def body(d_ref, s_ref, m1_ref, m2_ref, g_ref, p1_ref, p2_ref):
        a1, a2 = m1_ref[...], m2_ref[...]
        zero = jnp.zeros((r, w), F32)

        def step(i, carry):
            gp, p1, p2 = carry
            c = nc - 2 - i
            g_ref[c] = gp
            sp = s_ref[c]
            p1 = p1 + gp * sp
            p2 = p2 + gp * pltpu.roll(sp, S5_STATE, 1)
            return d_ref[c] + a1 * gp - a2 * pltpu.roll(gp, S5_STATE, 1), p1, p2

        g_ref[nc - 1] = zero
        _, p1, p2 = lax.fori_loop(0, nc - 1, step, (d_ref[nc - 1], zero, zero))
        p1_ref[...] = p1
        p2_ref[...] = p2

    vm = pl.BlockSpec(memory_space=pltpu.VMEM)
    sd = jax.ShapeDtypeStruct
    return pl.pallas_call(
        body, name="s5_scan_bwd", in_specs=[vm, vm, vm, vm], out_specs=[vm, vm, vm],
        out_shape=[sd(dsprev.shape, F32), sd((r, w), F32), sd((r, w), F32)],
        compiler_params=pltpu.CompilerParams(vmem_limit_bytes=VMEM_LIMIT))(dsprev, sprev, m1, m2)


def _to_groups(u, bl):
    t, d = u.shape
    g = d // S5_GROUP
    return u.reshape(t // S5_CHUNK, S5_CHUNK, g, S5_GROUP).transpose(2, 0, 1, 3).reshape(g, t // S5_CHUNK, S5_CHUNK * S5_GROUP)


def _from_groups(y):
    g, nct, _ = y.shape
    return y.reshape(g, nct, S5_CHUNK, S5_GROUP).transpose(1, 2, 0, 3).reshape(nct * S5_CHUNK, g * S5_GROUP)


def _to_scan(s, bl):
    g, nct, w = s.shape
    return s.reshape(g, bl, nct // bl, w).transpose(2, 1, 0, 3).reshape(nct // bl, bl * g, w)


def _from_scan(s, bl):
    nc, r, w = s.shape
    g = r // bl
    return s.reshape(nc, bl, g, w).transpose(2, 1, 0, 3).reshape(g, bl * nc, w)


def _gelu_tanh_parts(y):
    c0 = math.sqrt(2.0 / math.pi)
    inner = c0 * (y + 0.044715 * y * y * y)
    th = jnp.tanh(inner)
    return th, c0 * (1.0 + 3 * 0.044715 * y * y)


def _s5_fwd(h, g, ops, d_skip, w_in, w_glu, bl):
    wcat, wo, m1, m2 = ops
    t, d = h.shape
    ch = S5_CHUNK * S5_GROUP
    hn = _rmsnorm("mix_norm", h, g)
    u = _mm_rs("s5_in", hn, "flat", w_in, 0)
    ug = _to_groups(u.astype(BF16), bl)
    x = _bmm("s5_chunk_in", ug, wcat.astype(BF16), NN, F32)
    sprev = _s5_scan_fwd(_to_scan(x[:, :, ch:], bl), jnp.tile(m1, (bl, 1)), jnp.tile(m2, (bl, 1)))
    sprev_g = _from_scan(sprev, bl).astype(BF16)
    y_state = _bmm("s5_chunk_out", sprev_g, wo.astype(BF16), NN, F32)
    y = _from_groups(x[:, :, :ch] + y_state)

    def fn(yy, uu, dd):
        y2 = yy + dd * uu
        th, _ = _gelu_tanh_parts(y2)
        return [0.5 * y2 * (1.0 + th)], []
    z = _rows("s5_gelu", fn, [y, u, d_skip], [(d, BF16)])[0][0]
    zz = _mm_cs("s5_glu", z, w_glu, 0, "sm", F32)
    half = d // 2

    def glu(hh, zo0, zo1, zg0, zg1):
        m = jnp.concatenate([zo0 * _sigmoid(zg0), zo1 * _sigmoid(zg1)], axis=1)
        return [hh + m], []
    out = _rows("s5_glu_mix", glu, [h, zz[0], zz[1], zz[2], zz[3]], [(d, F32)])[0][0]
    return out, (h, hn, u, ug, sprev, sprev_g, y, z, zz)


def _s5_bwd(dout, saved, g, ops, d_skip, w_in, w_glu, bl):
    h, hn, u, ug, sprev, sprev_g, y, z, zz = saved
    wcat, wo, m1, m2 = ops
    t, d = h.shape
    half = d // 2
    ch = S5_CHUNK * S5_GROUP

    def glu_bwd(do, zo0, zo1, zg0, zg1):
        outs = []
        for j, (zo, zg) in enumerate(((zo0, zg0), (zo1, zg1))):
            dm = do[:, j * half:(j + 1) * half]
            sg = _sigmoid(zg)
            outs.append((dm * sg, dm * zo * sg * (1.0 - sg)))
        return [outs[0][0], outs[1][0], outs[0][1], outs[1][1]], []
    dz4, _ = _rows("s5_glu_bwd", glu_bwd, [dout, zz[0], zz[1], zz[2], zz[3]], [(half, BF16)] * 4)
    dzz = jnp.stack(dz4, axis=0)
    dwglu = _mm_dw("s5_dwglu", z, None, dzz, "sm")
    dz = _mm_cs_dx("s5_glu_dx", [(dzz, w_glu)], "sm", 0)

    def gelu_bwd(dzv, yy, uu, dd):
        y2 = yy + dd * uu
        th, dinner = _gelu_tanh_parts(y2)
        dy2 = dzv * (0.5 * (1.0 + th) + 0.5 * y2 * (1.0 - th * th) * dinner)
        return [dy2, dy2 * dd], [jnp.sum(dy2 * uu, axis=0, keepdims=True)]
    (dy_b, du_skip), (dd,) = _rows("s5_gelu_bwd", gelu_bwd, [dz, y, u, d_skip], [(d, BF16), (d, F32)], [(1, d)])
    dyg = _to_groups(dy_b, bl)
    wo_b, wcat_b = wo.astype(BF16), wcat.astype(BF16)
    dsprev = _bmm("s5_chunk_out_dx", dyg, wo_b, NT, F32)
    m1t, m2t = jnp.tile(m1, (bl, 1)), jnp.tile(m2, (bl, 1))
    dsloc, p1, p2 = _s5_scan_bwd(_to_scan(dsprev, bl), sprev, m1t, m2t)
    dcat = jnp.concatenate([dyg, _from_scan(dsloc, bl).astype(BF16)], axis=2)
    dug = _bmm("s5_chunk_in_dx", dcat, wcat_b, NT, F32)
    dwcat = _bmm("s5_chunk_in_dw", ug, dcat, TN, F32)
    dwo = _bmm("s5_chunk_out_dw", sprev_g, dyg, TN, F32)
    gcount = d // S5_GROUP
    dm1 = p1.reshape(bl, gcount, 2 * S5_STATE).sum(axis=0)
    dm2 = p2.reshape(bl, gcount, 2 * S5_STATE).sum(axis=0)
    du = (_from_groups(dug) + du_skip).astype(BF16)
    dwin = _mm_dw("s5_dwin", hn, "flat", du, None)
    dhn = _mm_rs_dx("s5_in_dx", du, w_in, 0, "flat", F32)
    dh, dg = _rmsnorm_bwd("mix_norm_bwd", dout, dhn, h, g)
    return dh, dg, dwin, dwglu, dd, (dwcat, dwo, dm1, dm2)


def _sb_scores(q, kblk, diag, row, col):
    z = lax.dot_general(q, kblk, NT, preferred_element_type=F32) * (SB_HEAD_DIM ** -0.5)
    l1 = jnp.log(1.0 + jnp.exp(-jnp.abs(z)))
    ls = jnp.minimum(z, 0.0) - l1
    mask = jnp.logical_or(col < row, jnp.logical_not(diag))
    lk = jnp.where(mask, ls - z, 0.0)
    return ls, lk, mask


def _split_dot(v, tri):
    hi = v.astype(BF16)
    lo = (v - hi.astype(F32)).astype(BF16)
    return (jnp.dot(hi, tri, preferred_element_type=F32) + jnp.dot(lo, tri, preferred_element_type=F32))


def _sb_attn_fwd(q, k, v):
    bh, l, dh = q.shape
    tb = min(SB_BLOCK, l)
    nq = l // tb

    def body(q_ref, k_ref, v_ref, o_ref):
        qi = pl.program_id(1)
        qv = q_ref[...]
        row = lax.broadcasted_iota(jnp.int32, (tb, tb), 0)
        col = lax.broadcasted_iota(jnp.int32, (tb, tb), 1)
        tri = (row > col).astype(BF16)

        def step(j, carry):
            acc, cr = carry
            ks = pl.multiple_of((qi - j) * tb, tb)
            kblk, vblk = k_ref[pl.ds(ks, tb), :], v_ref[pl.ds(ks, tb), :]
            ls, lk, mask = _sb_scores(qv, kblk, j == 0, row, col)
            later = _split_dot(lk, tri)
            att = jnp.where(mask, jnp.exp(ls + later + cr), 0.0)
            acc = acc + jnp.dot(att.astype(BF16), vblk, preferred_element_type=F32)
            return acc, cr + jnp.sum(lk, axis=1, keepdims=True)

        acc, _ = lax.fori_loop(0, qi + 1, step, (jnp.zeros((tb, dh), F32), jnp.zeros((tb, 1), F32)))
        o_ref[...] = acc

    blk = pl.BlockSpec((None, tb, dh), lambda b, i: (b, i, 0))
    full = pl.BlockSpec((None, l, dh), lambda b, i: (b, 0, 0))
    return pl.pallas_call(
        body, name="sb_attn", grid=(bh, nq), in_specs=[blk, full, full], out_specs=blk,
        out_shape=jax.ShapeDtypeStruct((bh, l, dh), F32), compiler_params=_params("parallel", "parallel"))(q, k, v)


def _sb_attn_bwd(q, k, v, o, do):
    bh, l, dh = q.shape
    tb = min(SB_BLOCK, l)
    nq = l // tb
    scale = SB_HEAD_DIM ** -0.5

    def body(q_ref, k_ref, v_ref, o_ref, do_ref, dq_ref, dk_ref, dv_ref):
        qi = pl.program_id(1)

        @pl.when(qi == 0)
        def _():
            dk_ref[...] = jnp.zeros_like(dk_ref)
            dv_ref[...] = jnp.zeros_like(dv_ref)

        qv = q_ref[...]
        dob = do_ref[...].astype(BF16)
        dsum = jnp.sum(dob.astype(F32) * o_ref[...], axis=1, keepdims=True)
        row = lax.broadcasted_iota(jnp.int32, (tb, tb), 0)
        col = lax.broadcasted_iota(jnp.int32, (tb, tb), 1)
        tri = (row > col).astype(BF16)
        tri_inc = (row >= col).astype(BF16)

        def step(j, carry):
            dq, cr, ce = carry
            ks = pl.multiple_of((qi - j) * tb, tb)
            kblk, vblk = k_ref[pl.ds(ks, tb), :], v_ref[pl.ds(ks, tb), :]
            ls, lk, mask = _sb_scores(qv, kblk, j == 0, row, col)
            later = _split_dot(lk, tri)
            att = jnp.where(mask, jnp.exp(ls + later + cr), 0.0).astype(BF16)
            datt = lax.dot_general(dob, vblk, NT, preferred_element_type=F32)
            e = att.astype(F32) * datt
            pre = dsum - ce - _split_dot(e, tri_inc)
            sg = jnp.exp(ls)
            dz = (jnp.where(mask, e * (1.0 - sg) - pre * sg, 0.0) * scale).astype(BF16)
            dq = dq + jnp.dot(dz, kblk, preferred_element_type=F32)
            dk_ref[pl.ds(ks, tb), :] += lax.dot_general(dz, qv, TN, preferred_element_type=F32)
            dv_ref[pl.ds(ks, tb), :] += lax.dot_general(att, dob, TN, preferred_element_type=F32)
            return dq, cr + jnp.sum(lk, axis=1, keepdims=True), ce + jnp.sum(e, axis=1, keepdims=True)

        zc = jnp.zeros((tb, 1), F32)
        dq, _, _ = lax.fori_loop(0, qi + 1, step, (jnp.zeros((tb, dh), F32), zc, zc))
        dq_ref[...] = dq

    blk = pl.BlockSpec((None, tb, dh), lambda b, i: (b, i, 0))
    full = pl.BlockSpec((None, l, dh), lambda b, i: (b, 0, 0))
    sd = jax.ShapeDtypeStruct((bh, l, dh), F32)
    return pl.pallas_call(
        body, name="sb_attn_bwd", grid=(bh, nq), in_specs=[blk, full, full, blk, blk], out_specs=[blk, full, full],
        out_shape=[sd, sd, sd], compiler_params=_params("parallel", "arbitrary"))(q, k, v, o, do)


def _to_heads(x, bl):
    t, w = x.shape
    heads = w // SB_HEAD_DIM
    l = t // bl
    return x.reshape(bl, l, heads, SB_HEAD_DIM).transpose(0, 2, 1, 3).reshape(bl * heads, l, SB_HEAD_DIM)


def _from_heads(x, bl):
    bh, l, dh = x.shape
    heads = bh // bl
    return x.reshape(bl, heads, l, dh).transpose(0, 2, 1, 3).reshape(bl * l, heads * dh)


def _sb_fwd(h, g, w_qkv, w_o, bl):
    t, d = h.shape
    hn = _rmsnorm("mix_norm", h, g)
    qkv = _mm_cs("sb_qkv", hn, w_qkv, 0, "flat", BF16)
    q, k, v = (_to_heads(qkv[:, i * d:(i + 1) * d], bl) for i in range(3))
    o = _sb_attn_fwd(q, k, v)
    ob = _from_heads(o, bl).astype(BF16)
    out = _mm_rs("sb_out", ob, "flat", w_o, 0, res=h)
    return out, (h, hn, q, k, v, o, ob)


def _sb_bwd(dout, saved, g, w_qkv, w_o, bl):
    h, hn, q, k, v, o, ob = saved
    dob = dout.astype(BF16)
    dwo = _mm_dw("sb_dwo", ob, "flat", dob, None)
    do = _mm_rs_dx("sb_out_dx", dob, w_o, 0, "flat", F32)
    dq, dk, dv = _sb_attn_bwd(q, k, v, o, _to_heads(do, bl))
    dqkv = jnp.concatenate([_from_heads(a, bl).astype(BF16) for a in (dq, dk, dv)], axis=1)
    dwqkv = _mm_dw("sb_dwqkv", hn, None, dqkv, "flat")
    dhn = _mm_cs_dx("sb_qkv_dx", [(dqkv, w_qkv)], "flat", 0)
    dh, dg = _rmsnorm_bwd("mix_norm_bwd", dout, dhn, h, g)
    return dh, dg, dwqkv, dwo


def _adamw(name, w, parts, m, v):
    n_parts = len(parts)
    c1 = 1.0 / (1.0 - ADAM_B1 ** ADAM_STEP)
    c2 = 1.0 / (1.0 - ADAM_B2 ** ADAM_STEP)

    def fn(wv, mv, vv, *gs):
        gr = gs[0]
        for extra in gs[1:]:
            gr = gr + extra
        mn = ADAM_B1 * mv + (1.0 - ADAM_B1) * gr
        vn = ADAM_B2 * vv + (1.0 - ADAM_B2) * gr * gr
        delta = -ADAM_LR * ((mn * c1) / (jnp.sqrt(vn * c2) + ADAM_EPS) + ADAM_WD * wv)
        return [gr, delta, mn, vn], []
    cols = w.shape[1]
    return _rows(name, fn, [w, m, v] + list(parts), [(cols, F32)] * 4, tm=_tile(w.shape[0], 256))[0]


def _place():
    x, y, c = lax.axis_index("x"), lax.axis_index("y"), lax.axis_index("c")
    chips = [(1 - x, y), (x, 1 - y), (1 - x, 1 - y)]
    return x, y, c, chips


def _remote(src, dst, send_sem, recv_sem, to):
    return pltpu.make_async_remote_copy(src_ref=src, dst_ref=dst, send_sem=send_sem, recv_sem=recv_sem,
                                        device_id=to, device_id_type=MESH)


def _half(ref, c, rh, lead):
    return ref.at[(slice(None),) * lead + (pl.ds(c * rh, rh),)]


def _allgather_weights(ws):
    n = len(ws)

    def body(*refs):
        ins, outs = refs[:n], refs[n:2 * n]
        send, recv, local = refs[2 * n:]
        x, y, c, chips = _place()
        own = 2 * x + y
        sibling = (x, y, 1 - c)
        kept, sent = [], []
        for t in range(n):
            rh = ws[t].shape[1] // 2
            cp = pltpu.make_async_copy(ins[t], outs[t].at[own], local.at[t])
            cp.start()
            kept.append(cp)
            for j, chip in enumerate(chips):
                cp = _remote(_half(ins[t], c, rh, 1), _half(outs[t].at[own], c, rh, 1), send.at[t, j], recv.at[t, j], (*chip, c))
                cp.start()
                sent.append(cp)
        for t in range(n):
            rh = ws[t].shape[1] // 2
            for j, chip in enumerate(chips):
                landed = _half(outs[t].at[2 * chip[0] + chip[1]], c, rh, 1)
                _remote(landed, landed, send.at[t, j], recv.at[t, j], (*chip, c)).wait_recv()
                cp = _remote(landed, landed, send.at[t, 3 + j], recv.at[t, 3 + j], sibling)
                cp.start()
                sent.append(cp)
        for t in range(n):
            rh = ws[t].shape[1] // 2
            for j, chip in enumerate(chips):
                passed = _half(outs[t].at[2 * chip[0] + chip[1]], 1 - c, rh, 1)
                _remote(passed, passed, send.at[t, 3 + j], recv.at[t, 3 + j], sibling).wait_recv()
        for cp in sent:
            cp.wait_send()
        for cp in kept:
            cp.wait()

    return pl.pallas_call(
        body, name="allgather_weights", in_specs=[ANY] * n, out_specs=[ANY] * n,
        out_shape=[jax.ShapeDtypeStruct((N_CHIPS,) + w.shape, w.dtype) for w in ws],
        scratch_shapes=[pltpu.SemaphoreType.DMA((n, 6)), pltpu.SemaphoreType.DMA((n, 6)), pltpu.SemaphoreType.DMA((n,))],
    )(*ws)


def _pair_exchange(gs):
    n = len(gs)

    def body(*refs):
        ins, mine, theirs = refs[:n], refs[n:2 * n], refs[2 * n:3 * n]
        send, recv, local = refs[3 * n:]
        x, y, c, _ = _place()
        sibling = (x, y, 1 - c)
        started = []
        for t in range(n):
            rh = gs[t].shape[1] // 2
            out = _remote(_half(ins[t], 1 - c, rh, 1), theirs[t], send.at[t], recv.at[t], sibling)
            out.start()
            keep = pltpu.make_async_copy(_half(ins[t], c, rh, 1), mine[t], local.at[t])
            keep.start()
            started.append((out, keep))
        for t in range(n):
            out, keep = started[t]
            out.wait()
            keep.wait()

    shapes = [jax.ShapeDtypeStruct((N_CHIPS, g.shape[1] // 2, g.shape[2]), F32) for g in gs]
    res = pl.pallas_call(
        body, name="grad_pair_exchange", in_specs=[ANY] * n, out_specs=[ANY] * (2 * n), out_shape=shapes + shapes,
        scratch_shapes=[pltpu.SemaphoreType.DMA((n,)), pltpu.SemaphoreType.DMA((n,)), pltpu.SemaphoreType.DMA((n,))],
    )(*gs)
    return res[:n], res[n:]


def _chip_exchange(ps):
    n = len(ps)

    def body(*refs):
        ins, outs = refs[:n], refs[n:2 * n]
        send, recv, local = refs[2 * n:]
        x, y, c, chips = _place()
        own = 2 * x + y
        started = []
        for t in range(n):
            keep = pltpu.make_async_copy(ins[t].at[own], outs[t].at[own], local.at[t])
            keep.start()
            started.append(keep)
            for j, chip in enumerate(chips):
                cp = _remote(ins[t].at[2 * chip[0] + chip[1]], outs[t].at[own], send.at[t, j], recv.at[t, j], (*chip, c))
                cp.start()
                started.append(cp)
        for cp in started:
            cp.wait()

    return pl.pallas_call(
        body, name="grad_chip_exchange", in_specs=[ANY] * n, out_specs=[ANY] * n,
        out_shape=[jax.ShapeDtypeStruct(p.shape, F32) for p in ps],
        scratch_shapes=[pltpu.SemaphoreType.DMA((n, 3)), pltpu.SemaphoreType.DMA((n, 3)), pltpu.SemaphoreType.DMA((n,))],
    )(*ps)


def _pair_gather(halves, groups):
    n = len(halves)
    where = {}
    for gi, idxs in enumerate(groups):
        for li, t in enumerate(idxs):
            where[t] = (gi, li)
    n_out = len(groups)

    def body(*refs):
        ins, outs = refs[:n], refs[n:n + n_out]
        send, recv, local = refs[n + n_out:]
        x, y, c, _ = _place()
        sibling = (x, y, 1 - c)
        started = []
        for t in range(n):
            rh = halves[t].shape[0]
            gi, li = where[t]
            dst = outs[gi].at[li, pl.ds(c * rh, rh)]
            keep = pltpu.make_async_copy(ins[t], dst, local.at[t])
            keep.start()
            cp = _remote(ins[t], dst, send.at[t], recv.at[t], sibling)
            cp.start()
            started.append((keep, cp))
        for keep, cp in started:
            keep.wait()
            cp.wait()

    shapes = [jax.ShapeDtypeStruct((len(idxs), 2 * halves[idxs[0]].shape[0], halves[idxs[0]].shape[1]), F32) for idxs in groups]
    return pl.pallas_call(
        body, name="grad_pair_gather", in_specs=[ANY] * n, out_specs=[ANY] * n_out, out_shape=shapes,
        scratch_shapes=[pltpu.SemaphoreType.DMA((n,)), pltpu.SemaphoreType.DMA((n,)), pltpu.SemaphoreType.DMA((n,))],
    )(*halves)


def _allreduce_small(v):
    rows, w = v.shape

    def body(x_ref, sum_ref, all_ref, send, recv, local):
        x, y, c, chips = _place()
        me, sibling = (x, y, c), (x, y, 1 - c)

        def slot(px, py, pc):
            return all_ref.at[4 * px + 2 * py + pc]

        def copy(k, block, to, src=None):
            return _remote(slot(*block) if src is None else src, slot(*block), send.at[k], recv.at[k], to)

        mine = pltpu.make_async_copy(x_ref, slot(*me), local)
        mine.start()
        first = [copy(0, me, sibling, src=x_ref)]
        first += [copy(1 + j, me, (*chip, c), src=x_ref) for j, chip in enumerate(chips)]
        for cp in first:
            cp.start()
        passed = [copy(4 + j, (*chip, c), sibling) for j, chip in enumerate(chips)]
        for j, chip in enumerate(chips):
            copy(1 + j, (*chip, c), me).wait_recv()
            passed[j].start()
        copy(0, sibling, me).wait_recv()
        for j, chip in enumerate(chips):
            copy(4 + j, (*chip, 1 - c), me).wait_recv()
        for cp in first + passed:
            cp.wait_send()
        mine.wait()
        tot = all_ref[0]
        for k in range(1, N_DEV):
            tot = tot + all_ref[k]
        sum_ref[...] = tot

    vm = pl.BlockSpec(memory_space=pltpu.VMEM)
    return pl.pallas_call(
        body, name="allreduce_small", in_specs=[vm], out_specs=[vm, vm],
        out_shape=[jax.ShapeDtypeStruct((rows, w), F32), jax.ShapeDtypeStruct((N_DEV, rows, w), F32)],
        scratch_shapes=[pltpu.SemaphoreType.DMA((7,)), pltpu.SemaphoreType.DMA((7,)), pltpu.SemaphoreType.DMA],
        compiler_params=pltpu.CompilerParams(vmem_limit_bytes=VMEM_LIMIT),
    )(v)[0]


def _sum4(name, a):
    def fn(a0, a1, a2, a3):
        return [((a0 + a1) + a2) + a3], []
    return _rows(name, fn, [a[0], a[1], a[2], a[3]], [(a.shape[2], F32)], tm=_tile(a.shape[1], 256))[0][0]


def _add(name, a, b):
    def fn(u, v):
        return [u + v], []
    return _rows(name, fn, [a, b], [(a.shape[1], F32)], tm=_tile(a.shape[0], 256))[0][0]


BIG = ["ffn1_w1", "ffn1_w3", "ffn1_w2", "ffn2_w1", "ffn2_w3", "ffn2_w2", "ple_proj", "ple_gate",
       "s5_w_in", "s5_w_glu", "sb_w_qkv", "sb_w_o"]
SMALL = ["ffn1_norm", "mix_norm", "ffn2_norm", "ple_norm", "s5_a_re", "s5_a_im", "s5_log_dt", "s5_b_re", "s5_b_im",
         "s5_c_re", "s5_c_im", "s5_d", "final_norm"]
ORDER = ["ffn1_norm", "ffn1_w1", "ffn1_w3", "ffn1_w2", "mix_norm", "ffn2_norm", "ffn2_w1", "ffn2_w3", "ffn2_w2",
         "ple_norm", "ple_proj", "ple_gate", "s5_w_in", "s5_a_re", "s5_a_im", "s5_log_dt", "s5_b_re", "s5_b_im",
         "s5_c_re", "s5_c_im", "s5_d", "s5_w_glu", "sb_w_qkv", "sb_w_o", "final_norm"]


def _pack(arrays):
    flat = jnp.concatenate([a.reshape(-1) for a in arrays])
    pad = (-flat.shape[0]) % 1024
    return jnp.pad(flat, (0, pad)).reshape(-1, 128)


def _unpack(packed, like):
    flat = packed.reshape(-1)
    out, off = [], 0
    for a in like:
        out.append(flat[off:off + a.size].reshape(a.shape))
        off += a.size
    return out


def _fwd_bwd(x, p, target, w, gathered):
    bl, l, d = x.shape
    t = bl * l
    depth = w["ffn1_norm"].shape[0]
    s5_ops, s5_vjp = jax.vjp(_s5_prep, w["s5_a_re"][0], w["s5_a_im"][0], w["s5_log_dt"][0], w["s5_b_re"][0],
                             w["s5_b_im"][0], w["s5_c_re"][0], w["s5_c_im"][0])

    h = x.reshape(t, d)
    p2 = [p[i].reshape(t, p.shape[-1]).astype(BF16) for i in range(depth)]
    saved = []
    for i in range(depth):
        norm = lambda name: w[name][i:i + 1]
        h, s1 = _ffn_fwd(h, norm("ffn1_norm"), gathered["ffn1_w1"], gathered["ffn1_w3"], gathered["ffn1_w2"], i)
        if i % 2 == 0:
            h, s2 = _s5_fwd(h, norm("mix_norm"), s5_ops, w["s5_d"][i // 2:i // 2 + 1], gathered["s5_w_in"], gathered["s5_w_glu"], bl)
        else:
            h, s2 = _sb_fwd(h, norm("mix_norm"), gathered["sb_w_qkv"], gathered["sb_w_o"], bl)
        h, s3 = _ffn_fwd(h, norm("ffn2_norm"), gathered["ffn2_w1"], gathered["ffn2_w3"], gathered["ffn2_w2"], i)
        h, s4 = _ple_fwd(h, norm("ple_norm"), p2[i], gathered["ple_proj"], gathered["ple_gate"], i)
        saved.append((s1, s2, s3, s4))

    loss, dh, dfinal = _head(h, w["final_norm"].reshape(1, d), target.reshape(t, d))

    big = {k: [None] * depth for k in BIG}
    small = {k: [None] * w[k].shape[0] if w[k].ndim > 1 else None for k in SMALL}
    small["final_norm"] = dfinal.reshape(d)
    for i in reversed(range(depth)):
        norm = lambda name: w[name][i:i + 1]
        s1, s2, s3, s4 = saved[i]
        dh, dg, big["ple_proj"][i], big["ple_gate"][i] = _ple_bwd(dh, s4, norm("ple_norm"), p2[i], gathered["ple_proj"], gathered["ple_gate"], i)
        small["ple_norm"][i] = dg[0]
        dh, dg, big["ffn2_w1"][i], big["ffn2_w3"][i], big["ffn2_w2"][i] = _ffn_bwd(
            dh, s3, norm("ffn2_norm"), gathered["ffn2_w1"], gathered["ffn2_w3"], gathered["ffn2_w2"], i)
        small["ffn2_norm"][i] = dg[0]
        if i % 2 == 0:
            dh, dg, big["s5_w_in"][0], big["s5_w_glu"][0], dd, dops = _s5_bwd(
                dh, s2, norm("mix_norm"), s5_ops, w["s5_d"][i // 2:i // 2 + 1], gathered["s5_w_in"], gathered["s5_w_glu"], bl)
            small["s5_d"][0] = dd[0]
            raw = s5_vjp(dops)
            for name, gr in zip(["s5_a_re", "s5_a_im", "s5_log_dt", "s5_b_re", "s5_b_im", "s5_c_re", "s5_c_im"], raw):
                small[name][0] = gr
        else:
            dh, dg, big["sb_w_qkv"][0], big["sb_w_o"][0] = _sb_bwd(dh, s2, norm("mix_norm"), gathered["sb_w_qkv"], gathered["sb_w_o"], bl)
        small["mix_norm"][i] = dg[0]
        dh, dg, big["ffn1_w1"][i], big["ffn1_w3"][i], big["ffn1_w2"][i] = _ffn_bwd(
            dh, s1, norm("ffn1_norm"), gathered["ffn1_w1"], gathered["ffn1_w3"], gathered["ffn1_w2"], i)
        small["ffn1_norm"][i] = dg[0]
    small_list = [jnp.stack(small[k]) if isinstance(small[k], list) else small[k] for k in SMALL]
    return loss, dh.reshape(bl, l, d), big, small_list


def _step(x, p, target, w, m, v):
    gathered = dict(zip(BIG, _allgather_weights([w[k].astype(BF16) for k in BIG])))
    loss, grad_x, big, small_list = _fwd_bwd(x, p, target, w, gathered)

    flat, groups = [], []
    for k in BIG:
        layers = [gr for gr in big[k] if gr is not None]
        groups.append(list(range(len(flat), len(flat) + len(layers))))
        flat += layers
    mine, theirs = _pair_exchange(flat)
    pair = [_add("grad_pair_sum", a.reshape(-1, a.shape[2]), b.reshape(-1, b.shape[2])).reshape(a.shape) for a, b in zip(mine, theirs)]
    landed = _chip_exchange(pair)
    halves = [_sum4("grad_chip_sum", a) for a in landed]
    grads = dict(zip(BIG, _pair_gather(halves, groups)))

    small_sum = _unpack(_allreduce_small(_pack(small_list)), [w[k] for k in SMALL])
    grads.update(dict(zip(SMALL, small_sum)))

    out_g, out_d, out_m, out_v = {}, {}, {}, {}
    for k in BIG:
        shape = w[k].shape
        two = lambda a: a.reshape(-1, shape[-1])
        res = _adamw("adamw", two(w[k]), [two(grads[k])], two(m[k]), two(v[k]))
        out_g[k], out_d[k], out_m[k], out_v[k] = (r.reshape(shape) for r in res)
    packed = _adamw("adamw_small", _pack([w[k] for k in SMALL]), [_pack([grads[k] for k in SMALL])],
                    _pack([m[k] for k in SMALL]), _pack([v[k] for k in SMALL]))
    like = [w[k] for k in SMALL]
    for dst, pk in zip((out_g, out_d, out_m, out_v), packed):
        dst.update(dict(zip(SMALL, _unpack(pk, like))))

    loss = lax.psum(loss, ("x", "y", "c"))
    return (loss, grad_x, *[out_g[k] for k in ORDER], *[out_d[k] for k in ORDER],
            *[out_m[k] for k in ORDER], *[out_v[k] for k in ORDER])


def kernel(x, p, ffn1_norm, ffn1_w1, ffn1_w3, ffn1_w2, mix_norm, ffn2_norm, ffn2_w1, ffn2_w3, ffn2_w2, ple_norm, ple_proj, ple_gate, s5_w_in, s5_a_re, s5_a_im, s5_log_dt, s5_b_re, s5_b_im, s5_c_re, s5_c_im, s5_d, s5_w_glu, sb_w_qkv, sb_w_o, final_norm, loss_target, m_ffn1_norm, m_ffn1_w1, m_ffn1_w3, m_ffn1_w2, m_mix_norm, m_ffn2_norm, m_ffn2_w1, m_ffn2_w3, m_ffn2_w2, m_ple_norm, m_ple_proj, m_ple_gate, m_s5_w_in, m_s5_a_re, m_s5_a_im, m_s5_log_dt, m_s5_b_re, m_s5_b_im, m_s5_c_re, m_s5_c_im, m_s5_d, m_s5_w_glu, m_sb_w_qkv, m_sb_w_o, m_final_norm, v_ffn1_norm, v_ffn1_w1, v_ffn1_w3, v_ffn1_w2, v_mix_norm, v_ffn2_norm, v_ffn2_w1, v_ffn2_w3, v_ffn2_w2, v_ple_norm, v_ple_proj, v_ple_gate, v_s5_w_in, v_s5_a_re, v_s5_a_im, v_s5_log_dt, v_s5_b_re, v_s5_b_im, v_s5_c_re, v_s5_c_im, v_s5_d, v_s5_w_glu, v_sb_w_qkv, v_sb_w_o, v_final_norm):
    args = dict(locals())
    w = {k: args[k] for k in ORDER}
    m = {k: args["m_" + k] for k in ORDER}
    v = {k: args["v_" + k] for k in ORDER}
    return _step(x, p, loss_target, w, m, v)
```

```python
import functools
import math

import jax
import jax.numpy as jnp
from jax import lax
from jax.experimental import pallas as pl
from jax.experimental.pallas import tpu as pltpu

F32 = jnp.float32
BF16 = jnp.bfloat16
MESH = pl.DeviceIdType.MESH

N_CHIPS = 4
N_DEV = 8
RMS_EPS = 1e-6
S5_GROUP = 16
S5_STATE = 64
S5_CHUNK = 16
SB_HEAD_DIM = 64
SB_BLOCK = 128
SB_CUT = -104.0
ADAM_LR, ADAM_B1, ADAM_B2, ADAM_EPS, ADAM_WD, ADAM_STEP = 0.001, 0.9, 0.999, 1e-08, 0.01, 10
VMEM_LIMIT = 48 * 1024 * 1024

NN = (((1,), (0,)), ((), ()))
NT = (((1,), (1,)), ((), ()))
TN = (((0,), (0,)), ((), ()))

ANY = pl.BlockSpec(memory_space=pl.ANY)


def _tile(n, target):
    if n <= target:
        return n
    for t in range(target - target % 8, 7, -8):
        if n % t == 0:
            return t
    raise ValueError(f"no row tile for {n}")


def _params(*semantics):
    return pltpu.CompilerParams(dimension_semantics=semantics, vmem_limit_bytes=VMEM_LIMIT)


def _sigmoid(v):
    return 1.0 / (1.0 + jnp.exp(-v))


def _gemm(name, grid, operands, in_specs, groups, acc_shapes, out_shapes, out_specs, epilogue, reduce_axis=None, aliases=None):
    n_in, n_out = len(operands), len(out_shapes)
    n_red = None if reduce_axis is None else grid[reduce_axis]

    def body(*refs):
        ins, outs, accs = refs[:n_in], refs[n_in:n_in + n_out], refs[n_in + n_out:]

        def products():
            res = []
            for terms in groups:
                tot = None
                for ia, ib, dims in terms:
                    d = lax.dot_general(ins[ia][...], ins[ib][...], dims, preferred_element_type=F32)
                    tot = d if tot is None else tot + d
                res.append(tot)
            return res

        def finish(vals):
            for o, v in zip(outs, epilogue(vals, ins)):
                o[...] = v.astype(o.dtype)

        if reduce_axis is None:
            finish(products())
        else:
            k = pl.program_id(reduce_axis)

            @pl.when(k == 0)
            def _():
                for a in accs:
                    a[...] = jnp.zeros_like(a)

            for a, d in zip(accs, products()):
                a[...] += d

            @pl.when(k == n_red - 1)
            def _():
                finish([a[...] for a in accs])

    scratch = [] if reduce_axis is None else [pltpu.VMEM(s, F32) for s in acc_shapes]
    sem = tuple("arbitrary" if i == reduce_axis else "parallel" for i in range(len(grid)))
    return pl.pallas_call(
        body, name=name, grid=grid, in_specs=in_specs, out_specs=out_specs, out_shape=out_shapes,
        scratch_shapes=scratch, input_output_aliases=aliases or {}, compiler_params=_params(*sem))(*operands)


def _ident(vals, ins):
    return vals


def _act_spec(layout, tm, cs, pos):
    if layout == "sm":
        return pl.BlockSpec((None, tm, cs), lambda *g: (pos(*g)[1], pos(*g)[0], 0))
    return pl.BlockSpec((tm, cs), lambda *g: pos(*g))


def _act_shape(layout, t, cs, dtype):
    return jax.ShapeDtypeStruct((N_CHIPS, t, cs) if layout == "sm" else (t, N_CHIPS * cs), dtype)


def _w_spec(w, layer, pos_k):
    _, _, r, c = w.shape
    return pl.BlockSpec((None, None, r, c), lambda *g: (pos_k(*g), layer, 0, 0))


def _mm_cs(name, x, w, layer, out_layout, out_dtype, tm=512):
    t, kd = x.shape
    cs = w.shape[3]
    tm = _tile(t, tm)
    return _gemm(
        name, (N_CHIPS, t // tm), [x, w],
        [pl.BlockSpec((tm, kd), lambda k, i: (i, 0)), _w_spec(w, layer, lambda k, i: k)],
        [[(0, 1, NN)]], None, [_act_shape(out_layout, t, cs, out_dtype)],
        [_act_spec(out_layout, tm, cs, lambda k, i: (i, k))], _ident)[0]


def _mm_rs(name, xs, layout, w, layer, res=None, alpha=1.0, out_dtype=F32, tm=512):
    ks, n = w.shape[2], w.shape[3]
    t = xs.shape[1] if layout == "sm" else xs.shape[0]
    tm = _tile(t, tm)
    operands = [xs, w] + ([] if res is None else [res])
    specs = [_act_spec(layout, tm, ks, lambda i, k: (i, k)), _w_spec(w, layer, lambda i, k: k)]
    if res is not None:
        specs.append(pl.BlockSpec((tm, n), lambda i, k: (i, 0)))

    def epilogue(vals, ins):
        y = alpha * vals[0]
        return [y if res is None else ins[2][...] + y]

    return _gemm(
        name, (t // tm, N_CHIPS), operands, specs, [[(0, 1, NN)]], [(tm, n)],
        [jax.ShapeDtypeStruct((t, n), out_dtype)], [pl.BlockSpec((tm, n), lambda i, k: (i, 0))],
        epilogue, reduce_axis=1)[0]


def _mm_cs_dx(name, pairs, layout, layer, tm=512):
    w0 = pairs[0][1]
    kd, cs = w0.shape[2], w0.shape[3]
    dy0 = pairs[0][0]
    t = dy0.shape[1] if layout == "sm" else dy0.shape[0]
    tm = _tile(t, tm)
    operands, specs, terms = [], [], []
    for dy, w in pairs:
        terms.append((len(operands), len(operands) + 1, NT))
        operands += [dy, w]
        specs += [_act_spec(layout, tm, cs, lambda i, k: (i, k)), _w_spec(w, layer, lambda i, k: k)]
    return _gemm(
        name, (t // tm, N_CHIPS), operands, specs, [terms], [(tm, kd)],
        [jax.ShapeDtypeStruct((t, kd), F32)], [pl.BlockSpec((tm, kd), lambda i, k: (i, 0))],
        _ident, reduce_axis=1)[0]


def _mm_rs_dx(name, dy, w, layer, out_layout, out_dtype, tm=512):
    t, n = dy.shape
    ks = w.shape[2]
    tm = _tile(t, tm)
    return _gemm(
        name, (N_CHIPS, t // tm), [dy, w],
        [pl.BlockSpec((tm, n), lambda k, i: (i, 0)), _w_spec(w, layer, lambda k, i: k)],
        [[(0, 1, NT)]], None, [_act_shape(out_layout, t, ks, out_dtype)],
        [_act_spec(out_layout, tm, ks, lambda k, i: (i, k))], _ident)[0]


def _mm_dw(name, x, x_layout, dy, dy_layout, slot, alpha=1.0, tk=512):
    stack, layer, layers = slot
    if x_layout is None:
        t, rows = x.shape
        cols = dy.shape[2] if dy_layout == "sm" else dy.shape[1] // N_CHIPS
        tk = _tile(t, tk)
        xspec = pl.BlockSpec((tk, rows), lambda k, j: (j, 0))
        yspec = _act_spec(dy_layout, tk, cols, lambda k, j: (j, k))
    else:
        t, cols = dy.shape
        rows = x.shape[2] if x_layout == "sm" else x.shape[1] // N_CHIPS
        tk = _tile(t, tk)
        xspec = _act_spec(x_layout, tk, rows, lambda k, j: (j, k))
        yspec = pl.BlockSpec((tk, cols), lambda k, j: (j, 0))
    operands, specs = [x, dy], [xspec, yspec]
    if stack is not None:
        operands.append(stack)
        specs.append(ANY)
    return _gemm(
        name, (N_CHIPS, t // tk), operands, specs, [[(0, 1, TN)]], [(rows, cols)],
        [jax.ShapeDtypeStruct((N_CHIPS, layers, rows, cols), F32)],
        [pl.BlockSpec((None, None, rows, cols), lambda k, j: (k, layer, 0, 0))],
        lambda vals, ins: [alpha * vals[0]], reduce_axis=1, aliases=None if stack is None else {2: 0})[0]


def _rows(name, fn, ins, outs, accs=(), tm=256):
    t = ins[0].shape[0]
    tm = _tile(t, tm)
    n_in, n_out, n_acc = len(ins), len(outs), len(accs)
    in_specs = []
    for a in ins:
        if a.shape[0] == t:
            in_specs.append(pl.BlockSpec((tm, a.shape[1]), lambda i: (i, 0)))
        else:
            in_specs.append(pl.BlockSpec(a.shape, lambda i: (0, 0)))
    out_shape = [jax.ShapeDtypeStruct((t, c), d) for c, d in outs] + [jax.ShapeDtypeStruct(s, F32) for s in accs]
    out_specs = [pl.BlockSpec((tm, c), lambda i: (i, 0)) for c, _ in outs] + [pl.BlockSpec(s, lambda i: (0, 0)) for s in accs]

    def body(*refs):
        i = pl.program_id(0)
        row_vals, acc_vals = fn(*[r[...] for r in refs[:n_in]])
        for o, v in zip(refs[n_in:n_in + n_out], row_vals):
            o[...] = v.astype(o.dtype)
        acc_refs = refs[n_in + n_out:]
        if n_acc:
            @pl.when(i == 0)
            def _():
                for a in acc_refs:
                    a[...] = jnp.zeros_like(a)

            for a, v in zip(acc_refs, acc_vals):
                a[...] += v

    res = pl.pallas_call(
        body, name=name, grid=(t // tm,), in_specs=in_specs, out_specs=out_specs, out_shape=out_shape,
        compiler_params=_params("arbitrary" if n_acc else "parallel"))(*ins)
    return res[:n_out], res[n_out:]


def _rms_stats(x):
    return lax.rsqrt(jnp.mean(x * x, axis=-1, keepdims=True) + RMS_EPS)


def _rmsnorm(name, h, g):
    def fn(x, gg):
        return [x * _rms_stats(x) * gg], []
    return _rows(name, fn, [h, g], [(h.shape[1], BF16)])[0][0]


def _rms_bwd_math(dn, x, g):
    r = _rms_stats(x)
    xhat = x * r
    dxh = dn * g
    dx = r * (dxh - xhat * jnp.mean(dxh * xhat, axis=-1, keepdims=True))
    return dx, jnp.sum(dn * xhat, axis=0, keepdims=True)


def _rmsnorm_bwd(name, dres, dn, h, g):
    def fn(dr, d, x, gg):
        dx, dg = _rms_bwd_math(d, x, gg)
        return [dr + dx], [dg]
    (dh,), (dg,) = _rows(name, fn, [dres, dn, h, g], [(h.shape[1], F32)], [(1, h.shape[1])])
    return dh, dg


def _ffn_fwd(h, g, w1, w3, w2, layer, tm=512):
    t, d = h.shape
    fs = w1.shape[3]
    n = _rmsnorm("ffn_norm", h, g)
    tm = _tile(t, tm)

    def up(vals, ins):
        a, b = vals
        return [a, b, a * _sigmoid(a) * b]

    sm = _act_shape("sm", t, fs, BF16)
    osp = _act_spec("sm", tm, fs, lambda k, i: (i, k))
    a, b, s = _gemm(
        "ffn_up", (N_CHIPS, t // tm), [n, w1, w3],
        [pl.BlockSpec((tm, d), lambda k, i: (i, 0)), _w_spec(w1, layer, lambda k, i: k), _w_spec(w3, layer, lambda k, i: k)],
        [[(0, 1, NN)], [(0, 2, NN)]], None, [sm, sm, sm], [osp, osp, osp], up)
    out = _mm_rs("ffn_down", s, "sm", w2, layer, res=h, alpha=0.5)
    return out, (h, n, a, b, s)


def _ffn_bwd(dout, saved, g, w1, w3, w2, layer, slots, tm=512):
    h, n, a, b, s = saved
    t, d = h.shape
    fs = w1.shape[3]
    tm = _tile(t, tm)
    dob = dout.astype(BF16)

    def down(vals, ins):
        ds = 0.5 * vals[0]
        av, bv = ins[2][...].astype(F32), ins[3][...].astype(F32)
        sg = _sigmoid(av)
        return [ds * bv * sg * (1.0 + av * (1.0 - sg)), ds * av * sg]

    sm = _act_shape("sm", t, fs, BF16)
    asp = _act_spec("sm", tm, fs, lambda k, i: (i, k))
    da, db = _gemm(
        "ffn_down_dx", (N_CHIPS, t // tm), [dob, w2, a, b],
        [pl.BlockSpec((tm, d), lambda k, i: (i, 0)), _w_spec(w2, layer, lambda k, i: k), asp, asp],
        [[(0, 1, NT)]], None, [sm, sm], [asp, asp], down)
    dw2 = _mm_dw("ffn_dw2", s, "sm", dob, None, slots[2], alpha=0.5)
    dw1 = _mm_dw("ffn_dw1", n, None, da, "sm", slots[0])
    dw3 = _mm_dw("ffn_dw3", n, None, db, "sm", slots[1])
    dn = _mm_cs_dx("ffn_up_dx", [(da, w1), (db, w3)], "sm", layer)
    dh, dg = _rmsnorm_bwd("ffn_norm_bwd", dout, dn, h, g)
    return dh, dg, dw1, dw3, dw2


def _ple_fwd(h, g, p2, wproj, wgate, layer):
    n = _rmsnorm("ple_norm", h, g)
    gl = _mm_rs("ple_gate", n, "flat", wgate, layer)
    pp = _mm_cs("ple_proj", p2, wproj, layer, "flat", F32)

    def fn(hh, gg, q):
        return [hh + q * _sigmoid(gg)], []
    out = _rows("ple_mix", fn, [h, gl, pp], [(h.shape[1], F32)])[0][0]
    return out, (h, n, gl, pp)


def _ple_bwd(dout, saved, g, p2, wproj, wgate, layer, slots):
    h, n, gl, pp = saved
    d = h.shape[1]

    def fn(do, gg, q):
        sg = _sigmoid(gg)
        return [do * sg, do * q * sg * (1.0 - sg)], []
    (dpp, dgl), _ = _rows("ple_mix_bwd", fn, [dout, gl, pp], [(d, BF16), (d, BF16)])
    dwproj = _mm_dw("ple_dwproj", p2, None, dpp, "flat", slots[0])
    dwgate = _mm_dw("ple_dwgate", n, "flat", dgl, None, slots[1])
    dn = _mm_rs_dx("ple_gate_dx", dgl, wgate, layer, "flat", F32)
    dh, dg = _rmsnorm_bwd("ple_norm_bwd", dout, dn, h, g)
    return dh, dg, dwproj, dwgate


def _head(h, g, target):
    d = h.shape[1]

    def fn(x, gg, tg):
        y = x * _rms_stats(x) * gg
        err = y - tg
        dy = err * (1.0 / d)
        dx, dg = _rms_bwd_math(dy, x, gg)
        loss = 0.5 * jnp.sum(jnp.sum(err * err, axis=-1, keepdims=True) * (1.0 / d), axis=0, keepdims=True)
        return [dx], [dg, jnp.broadcast_to(loss, (1, 128))]
    (dh,), (dg, loss) = _rows("loss_head", fn, [h, g, target], [(d, F32)], [(1, d), (1, 128)])
    return loss[0, 0], dh, dg


def _s5_prep(a_re, a_im, log_dt, b_re, b_im, c_re, c_im):
    c = S5_CHUNK
    lam_re = jnp.minimum(a_re, -1e-4)
    lam_im = a_im
    dt = jnp.exp(log_dt)[:, None]
    ks = jnp.arange(c + 1, dtype=F32)[:, None, None]
    mag = jnp.exp(lam_re[None] * dt[None] * ks)
    ph = lam_im[None] * dt[None] * ks
    pw_re, pw_im = mag * jnp.cos(ph), mag * jnp.sin(ph)
    den = lam_re * lam_re + lam_im * lam_im
    nr, ni = pw_re[1] - 1.0, pw_im[1]
    fr = (nr * lam_re + ni * lam_im) / den
    fi = (ni * lam_re - nr * lam_im) / den
    bb_re = fr[..., None] * b_re - fi[..., None] * b_im
    bb_im = fr[..., None] * b_im + fi[..., None] * b_re
    ca_re = c_re[None] * pw_re[:, :, None, :] - c_im[None] * pw_im[:, :, None, :]
    ca_im = c_re[None] * pw_im[:, :, None, :] + c_im[None] * pw_re[:, :, None, :]
    hp = lax.Precision.HIGHEST
    kern = (jnp.einsum("kghp,gpj->kghj", ca_re[:c], bb_re, precision=hp)
            - jnp.einsum("kghp,gpj->kghj", ca_im[:c], bb_im, precision=hp))
    lag = jnp.arange(c)[None, :] - jnp.arange(c)[:, None]
    toep = jnp.where((lag >= 0)[:, :, None, None, None], kern[jnp.clip(lag, 0, c - 1)], 0.0)
    g = a_re.shape[0]
    wi = toep.transpose(2, 0, 4, 1, 3).reshape(g, c * S5_GROUP, c * S5_GROUP)
    rev_re, rev_im = pw_re[c - 1::-1][:c], pw_im[c - 1::-1][:c]
    wn_re = rev_re[..., None] * bb_re[None] - rev_im[..., None] * bb_im[None]
    wn_im = rev_re[..., None] * bb_im[None] + rev_im[..., None] * bb_re[None]
    wn = jnp.stack([wn_re, wn_im], axis=0).transpose(2, 1, 4, 0, 3).reshape(g, c * S5_GROUP, 2 * S5_STATE)
    wo = jnp.stack([ca_re[1:], -ca_im[1:]], axis=0).transpose(2, 0, 4, 1, 3).reshape(g, 2 * S5_STATE, c * S5_GROUP)
    ar, ai = pw_re[c], pw_im[c]
    m1 = jnp.concatenate([ar, ar], axis=1)
    m2 = jnp.concatenate([-ai, ai], axis=1)
    return jnp.concatenate([wi, wn], axis=2), wo, m1, m2


def _bmm(name, a, b, dims, out_dtype, gb=8):
    g = a.shape[0]
    gb = min(gb, g)
    m = a.shape[2] if dims == TN else a.shape[1]
    n = b.shape[1] if dims == NT else b.shape[2]

    def body(a_ref, b_ref, o_ref):
        for j in range(gb):
            o_ref[j] = lax.dot_general(a_ref[j], b_ref[j], dims, preferred_element_type=F32).astype(o_ref.dtype)

    return pl.pallas_call(
        body, name=name, grid=(g // gb,),
        in_specs=[pl.BlockSpec((gb,) + a.shape[1:], lambda i: (i, 0, 0)), pl.BlockSpec((gb,) + b.shape[1:], lambda i: (i, 0, 0))],
        out_specs=pl.BlockSpec((gb, m, n), lambda i: (i, 0, 0)),
        out_shape=jax.ShapeDtypeStruct((g, m, n), out_dtype), compiler_params=_params("parallel"))(a, b)


def _s5_scan_fwd(sloc, m1, m2):
    nc, r, w = sloc.shape

    def body(s_ref, m1_ref, m2_ref, o_ref):
        a1, a2 = m1_ref[...], m2_ref[...]

        def step(c, s):
            o_ref[c] = s
            return a1 * s + a2 * pltpu.roll(s, S5_STATE, 1) + s_ref[c]
        lax.fori_loop(0, nc, step, jnp.zeros((r, w), F32))

    vm = pl.BlockSpec(memory_space=pltpu.VMEM)
    return pl.pallas_call(
        body, name="s5_scan", in_specs=[vm, vm, vm], out_specs=vm,
        out_shape=jax.ShapeDtypeStruct(sloc.shape, F32),
        compiler_params=pltpu.CompilerParams(vmem_limit_bytes=VMEM_LIMIT))(sloc, m1, m2)


def _s5_scan_bwd(dsprev, sprev, m1, m2):
    nc, r, w = dsprev.shape

    def body(d_ref, s_ref, m1_ref, m2_ref, g_ref, p1_ref, p2_ref):
        a1, a2 = m1_ref[...], m2_ref[...]
        zero = jnp.zeros((r, w), F32)

        def step(i, carry):
            gp, p1, p2 = carry
            c = nc - 2 - i
            g_ref[c] = gp
            sp = s_ref[c]
            p1 = p1 + gp * sp
            p2 = p2 + gp * pltpu.roll(sp, S5_STATE, 1)
            return d_ref[c] + a1 * gp - a2 * pltpu.roll(gp, S5_STATE, 1), p1, p2

        g_ref[nc - 1] = zero
        _, p1, p2 = lax.fori_loop(0, nc - 1, step, (d_ref[nc - 1], zero, zero))
        p1_ref[...] = p1
        p2_ref[...] = p2

    vm = pl.BlockSpec(memory_space=pltpu.VMEM)
    sd = jax.ShapeDtypeStruct
    return pl.pallas_call(
        body, name="s5_scan_bwd", in_specs=[vm, vm, vm, vm], out_specs=[vm, vm, vm],
        out_shape=[sd(dsprev.shape, F32), sd((r, w), F32), sd((r, w), F32)],
        compiler_params=pltpu.CompilerParams(vmem_limit_bytes=VMEM_LIMIT))(dsprev, sprev, m1, m2)


def _to_groups(u, bl):
    t, d = u.shape
    g = d // S5_GROUP
    return u.reshape(t // S5_CHUNK, S5_CHUNK, g, S5_GROUP).transpose(2, 0, 1, 3).reshape(g, t // S5_CHUNK, S5_CHUNK * S5_GROUP)


def _from_groups(y):
    g, nct, _ = y.shape
    return y.reshape(g, nct, S5_CHUNK, S5_GROUP).transpose(1, 2, 0, 3).reshape(nct * S5_CHUNK, g * S5_GROUP)


def _to_scan(s, bl):
    g, nct, w = s.shape
    return s.reshape(g, bl, nct // bl, w).transpose(2, 1, 0, 3).reshape(nct // bl, bl * g, w)


def _from_scan(s, bl):
    nc, r, w = s.shape
    g = r // bl
    return s.reshape(nc, bl, g, w).transpose(2, 1, 0, 3).reshape(g, bl * nc, w)


def _gelu_tanh_parts(y):
    c0 = math.sqrt(2.0 / math.pi)
    inner = c0 * (y + 0.044715 * y * y * y)
    th = jnp.tanh(inner)
    return th, c0 * (1.0 + 3 * 0.044715 * y * y)


def _s5_fwd(h, g, ops, d_skip, w_in, w_glu, bl):
    wcat, wo, m1, m2 = ops
    t, d = h.shape
    ch = S5_CHUNK * S5_GROUP
    hn = _rmsnorm("mix_norm", h, g)
    u = _mm_rs("s5_in", hn, "flat", w_in, 0)
    ug = _to_groups(u.astype(BF16), bl)
    x = _bmm("s5_chunk_in", ug, wcat.astype(BF16), NN, F32)
    sprev = _s5_scan_fwd(_to_scan(x[:, :, ch:], bl), jnp.tile(m1, (bl, 1)), jnp.tile(m2, (bl, 1)))
    sprev_g = _from_scan(sprev, bl).astype(BF16)
    y_state = _bmm("s5_chunk_out", sprev_g, wo.astype(BF16), NN, F32)
    y = _from_groups(x[:, :, :ch] + y_state)

    def fn(yy, uu, dd):
        y2 = yy + dd * uu
        th, _ = _gelu_tanh_parts(y2)
        return [0.5 * y2 * (1.0 + th)], []
    z = _rows("s5_gelu", fn, [y, u, d_skip], [(d, BF16)])[0][0]
    zz = _mm_cs("s5_glu", z, w_glu, 0, "sm", F32)
    half = d // 2

    def glu(hh, zo0, zo1, zg0, zg1):
        m = jnp.concatenate([zo0 * _sigmoid(zg0), zo1 * _sigmoid(zg1)], axis=1)
        return [hh + m], []
    out = _rows("s5_glu_mix", glu, [h, zz[0], zz[1], zz[2], zz[3]], [(d, F32)])[0][0]
    return out, (h, hn, u, ug, sprev, sprev_g, y, z, zz)


def _s5_bwd(dout, saved, g, ops, d_skip, w_in, w_glu, bl):
    h, hn, u, ug, sprev, sprev_g, y, z, zz = saved
    wcat, wo, m1, m2 = ops
    t, d = h.shape
    half = d // 2
    ch = S5_CHUNK * S5_GROUP

    def glu_bwd(do, zo0, zo1, zg0, zg1):
        outs = []
        for j, (zo, zg) in enumerate(((zo0, zg0), (zo1, zg1))):
            dm = do[:, j * half:(j + 1) * half]
            sg = _sigmoid(zg)
            outs.append((dm * sg, dm * zo * sg * (1.0 - sg)))
        return [outs[0][0], outs[1][0], outs[0][1], outs[1][1]], []
    dz4, _ = _rows("s5_glu_bwd", glu_bwd, [dout, zz[0], zz[1], zz[2], zz[3]], [(half, BF16)] * 4)
    dzz = jnp.stack(dz4, axis=0)
    dwglu = _mm_dw("s5_dwglu", z, None, dzz, "sm", (None, 0, 1))
    dz = _mm_cs_dx("s5_glu_dx", [(dzz, w_glu)], "sm", 0)

    def gelu_bwd(dzv, yy, uu, dd):
        y2 = yy + dd * uu
        th, dinner = _gelu_tanh_parts(y2)
        dy2 = dzv * (0.5 * (1.0 + th) + 0.5 * y2 * (1.0 - th * th) * dinner)
        return [dy2, dy2 * dd], [jnp.sum(dy2 * uu, axis=0, keepdims=True)]
    (dy_b, du_skip), (dd,) = _rows("s5_gelu_bwd", gelu_bwd, [dz, y, u, d_skip], [(d, BF16), (d, F32)], [(1, d)])
    dyg = _to_groups(dy_b, bl)
    wo_b, wcat_b = wo.astype(BF16), wcat.astype(BF16)
    dsprev = _bmm("s5_chunk_out_dx", dyg, wo_b, NT, F32)
    m1t, m2t = jnp.tile(m1, (bl, 1)), jnp.tile(m2, (bl, 1))
    dsloc, p1, p2 = _s5_scan_bwd(_to_scan(dsprev, bl), sprev, m1t, m2t)
    dcat = jnp.concatenate([dyg, _from_scan(dsloc, bl).astype(BF16)], axis=2)
    dug = _bmm("s5_chunk_in_dx", dcat, wcat_b, NT, F32)
    dwcat = _bmm("s5_chunk_in_dw", ug, dcat, TN, F32)
    dwo = _bmm("s5_chunk_out_dw", sprev_g, dyg, TN, F32)
    gcount = d // S5_GROUP
    dm1 = p1.reshape(bl, gcount, 2 * S5_STATE).sum(axis=0)
    dm2 = p2.reshape(bl, gcount, 2 * S5_STATE).sum(axis=0)
    du = (_from_groups(dug) + du_skip).astype(BF16)
    dwin = _mm_dw("s5_dwin", hn, "flat", du, None, (None, 0, 1))
    dhn = _mm_rs_dx("s5_in_dx", du, w_in, 0, "flat", F32)
    dh, dg = _rmsnorm_bwd("mix_norm_bwd", dout, dhn, h, g)
    return dh, dg, dwin, dwglu, dd, (dwcat, dwo, dm1, dm2)


def _sb_scores(q, kblk, diag, row, col):
    z = lax.dot_general(q, kblk, NT, preferred_element_type=F32) * (SB_HEAD_DIM ** -0.5)
    l1 = jnp.log(1.0 + jnp.exp(-jnp.abs(z)))
    ls = jnp.minimum(z, 0.0) - l1
    mask = jnp.logical_or(col < row, jnp.logical_not(diag))
    lk = jnp.where(mask, ls - z, 0.0)
    return ls, lk, mask


def _sb_more(qi, carry):
    j, cr = carry[0], carry[2]
    return jnp.logical_and(j <= qi, jnp.max(cr) > SB_CUT)


def _split_dot(v, tri):
    hi = v.astype(BF16)
    lo = (v - hi.astype(F32)).astype(BF16)
    return (jnp.dot(hi, tri, preferred_element_type=F32) + jnp.dot(lo, tri, preferred_element_type=F32))


def _sb_attn_fwd(q, k, v):
    bh, l, dh = q.shape
    tb = min(SB_BLOCK, l)
    nq = l // tb

    def body(q_ref, k_ref, v_ref, o_ref):
        qi = pl.program_id(1)
        qv = q_ref[...]
        row = lax.broadcasted_iota(jnp.int32, (tb, tb), 0)
        col = lax.broadcasted_iota(jnp.int32, (tb, tb), 1)
        tri = (row > col).astype(BF16)

        def step(carry):
            j, acc, cr = carry
            ks = pl.multiple_of((qi - j) * tb, tb)
            kblk, vblk = k_ref[pl.ds(ks, tb), :], v_ref[pl.ds(ks, tb), :]
            ls, lk, mask = _sb_scores(qv, kblk, j == 0, row, col)
            later = _split_dot(lk, tri)
            att = jnp.where(mask, jnp.exp(ls + later + cr), 0.0)
            acc = acc + jnp.dot(att.astype(BF16), vblk, preferred_element_type=F32)
            return j + 1, acc, cr + jnp.sum(lk, axis=1, keepdims=True)

        _, acc, _ = lax.while_loop(functools.partial(_sb_more, qi), step,
                                   (jnp.int32(0), jnp.zeros((tb, dh), F32), jnp.zeros((tb, 1), F32)))
        o_ref[...] = acc

    blk = pl.BlockSpec((None, tb, dh), lambda b, i: (b, i, 0))
    full = pl.BlockSpec((None, l, dh), lambda b, i: (b, 0, 0))
    return pl.pallas_call(
        body, name="sb_attn", grid=(bh, nq), in_specs=[blk, full, full], out_specs=blk,
        out_shape=jax.ShapeDtypeStruct((bh, l, dh), F32), compiler_params=_params("parallel", "parallel"))(q, k, v)


def _sb_attn_bwd(q, k, v, o, do):
    bh, l, dh = q.shape
    tb = min(SB_BLOCK, l)
    nq = l // tb
    scale = SB_HEAD_DIM ** -0.5

    def body(q_ref, k_ref, v_ref, o_ref, do_ref, dq_ref, dk_ref, dv_ref):
        qi = pl.program_id(1)

        @pl.when(qi == 0)
        def _():
            dk_ref[...] = jnp.zeros_like(dk_ref)
            dv_ref[...] = jnp.zeros_like(dv_ref)

        qv = q_ref[...]
        dob = do_ref[...].astype(BF16)
        dsum = jnp.sum(dob.astype(F32) * o_ref[...], axis=1, keepdims=True)
        row = lax.broadcasted_iota(jnp.int32, (tb, tb), 0)
        col = lax.broadcasted_iota(jnp.int32, (tb, tb), 1)
        tri = (row > col).astype(BF16)
        tri_inc = (row >= col).astype(BF16)

        def step(carry):
            j, dq, cr, ce = carry
            ks = pl.multiple_of((qi - j) * tb, tb)
            kblk, vblk = k_ref[pl.ds(ks, tb), :], v_ref[pl.ds(ks, tb), :]
            ls, lk, mask = _sb_scores(qv, kblk, j == 0, row, col)
            later = _split_dot(lk, tri)
            att = jnp.where(mask, jnp.exp(ls + later + cr), 0.0).astype(BF16)
            datt = lax.dot_general(dob, vblk, NT, preferred_element_type=F32)
            e = att.astype(F32) * datt
            pre = dsum - ce - _split_dot(e, tri_inc)
            sg = jnp.exp(ls)
            dz = (jnp.where(mask, e * (1.0 - sg) - pre * sg, 0.0) * scale).astype(BF16)
            dq = dq + jnp.dot(dz, kblk, preferred_element_type=F32)
            dk_ref[pl.ds(ks, tb), :] += lax.dot_general(dz, qv, TN, preferred_element_type=F32)
            dv_ref[pl.ds(ks, tb), :] += lax.dot_general(att, dob, TN, preferred_element_type=F32)
            return j + 1, dq, cr + jnp.sum(lk, axis=1, keepdims=True), ce + jnp.sum(e, axis=1, keepdims=True)

        zc = jnp.zeros((tb, 1), F32)
        _, dq, _, _ = lax.while_loop(functools.partial(_sb_more, qi), step,
                                     (jnp.int32(0), jnp.zeros((tb, dh), F32), zc, zc))
        dq_ref[...] = dq

    blk = pl.BlockSpec((None, tb, dh), lambda b, i: (b, i, 0))
    full = pl.BlockSpec((None, l, dh), lambda b, i: (b, 0, 0))
    sd = jax.ShapeDtypeStruct((bh, l, dh), F32)
    return pl.pallas_call(
        body, name="sb_attn_bwd", grid=(bh, nq), in_specs=[blk, full, full, blk, blk], out_specs=[blk, full, full],
        out_shape=[sd, sd, sd], compiler_params=_params("parallel", "arbitrary"))(q, k, v, o, do)


def _to_heads(x, bl):
    t, w = x.shape
    heads = w // SB_HEAD_DIM
    l = t // bl
    return x.reshape(bl, l, heads, SB_HEAD_DIM).transpose(0, 2, 1, 3).reshape(bl * heads, l, SB_HEAD_DIM)


def _from_heads(x, bl):
    bh, l, dh = x.shape
    heads = bh // bl
    return x.reshape(bl, heads, l, dh).transpose(0, 2, 1, 3).reshape(bl * l, heads * dh)


def _sb_fwd(h, g, w_qkv, w_o, bl):
    t, d = h.shape
    hn = _rmsnorm("mix_norm", h, g)
    qkv = _mm_cs("sb_qkv", hn, w_qkv, 0, "flat", BF16)
    q, k, v = (_to_heads(qkv[:, i * d:(i + 1) * d], bl) for i in range(3))
    o = _sb_attn_fwd(q, k, v)
    ob = _from_heads(o, bl).astype(BF16)
    out = _mm_rs("sb_out", ob, "flat", w_o, 0, res=h)
    return out, (h, hn, q, k, v, o, ob)


def _sb_bwd(dout, saved, g, w_qkv, w_o, bl):
    h, hn, q, k, v, o, ob = saved
    dob = dout.astype(BF16)
    dwo = _mm_dw("sb_dwo", ob, "flat", dob, None, (None, 0, 1))
    do = _mm_rs_dx("sb_out_dx", dob, w_o, 0, "flat", F32)
    dq, dk, dv = _sb_attn_bwd(q, k, v, o, _to_heads(do, bl))
    dqkv = jnp.concatenate([_from_heads(a, bl).astype(BF16) for a in (dq, dk, dv)], axis=1)
    dwqkv = _mm_dw("sb_dwqkv", hn, None, dqkv, "flat", (None, 0, 1))
    dhn = _mm_cs_dx("sb_qkv_dx", [(dqkv, w_qkv)], "flat", 0)
    dh, dg = _rmsnorm_bwd("mix_norm_bwd", dout, dhn, h, g)
    return dh, dg, dwqkv, dwo


def _adamw_update(wv, gr, mv, vv):
    c1 = 1.0 / (1.0 - ADAM_B1 ** ADAM_STEP)
    c2 = 1.0 / (1.0 - ADAM_B2 ** ADAM_STEP)
    mn = ADAM_B1 * mv + (1.0 - ADAM_B1) * gr
    vn = ADAM_B2 * vv + (1.0 - ADAM_B2) * gr * gr
    delta = -ADAM_LR * ((mn * c1) / (jnp.sqrt(vn * c2) + ADAM_EPS) + ADAM_WD * wv)
    return delta, mn, vn


def _adamw_small(w, gr, m, v):
    def fn(wv, gv, mv, vv):
        return list(_adamw_update(wv, gv, mv, vv)), []
    return _rows("adamw_small", fn, [w, gr, m, v], [(w.shape[1], F32)] * 3)[0]


def _place():
    x, y, c = lax.axis_index("x"), lax.axis_index("y"), lax.axis_index("c")
    chips = [(1 - x, y), (x, 1 - y), (1 - x, 1 - y)]
    return x, y, c, chips


def _remote(src, dst, send_sem, recv_sem, to):
    return pltpu.make_async_remote_copy(src_ref=src, dst_ref=dst, send_sem=send_sem, recv_sem=recv_sem,
                                        device_id=to, device_id_type=MESH)


def _half(ref, c, rh, lead):
    return ref.at[(slice(None),) * lead + (pl.ds(c * rh, rh),)]


def _allgather_weights(ws):
    n = len(ws)

    def body(*refs):
        ins, outs = refs[:n], refs[n:2 * n]
        send, recv = refs[2 * n:]
        x, y, c, chips = _place()
        own = 2 * x + y
        sibling = (x, y, 1 - c)
        sent = []
        for t in range(n):
            rh = ws[t].shape[1] // 2
            for j, chip in enumerate(chips):
                cp = _remote(_half(ins[t], c, rh, 1), _half(outs[t].at[own], c, rh, 1), send.at[t, j], recv.at[t, j], (*chip, c))
                cp.start()
                sent.append(cp)
        for t in range(n):
            rh = ws[t].shape[1] // 2
            for j, chip in enumerate(chips):
                landed = _half(outs[t].at[2 * chip[0] + chip[1]], c, rh, 1)
                _remote(landed, landed, send.at[t, j], recv.at[t, j], (*chip, c)).wait_recv()
                cp = _remote(landed, landed, send.at[t, 3 + j], recv.at[t, 3 + j], sibling)
                cp.start()
                sent.append(cp)
        for t in range(n):
            rh = ws[t].shape[1] // 2
            for j, chip in enumerate(chips):
                passed = _half(outs[t].at[2 * chip[0] + chip[1]], 1 - c, rh, 1)
                _remote(passed, passed, send.at[t, 3 + j], recv.at[t, 3 + j], sibling).wait_recv()
        for cp in sent:
            cp.wait_send()

    res = pl.pallas_call(
        body, name="allgather_weights", in_specs=[ANY] * n, out_specs=[ANY] * n,
        out_shape=[jax.ShapeDtypeStruct((N_CHIPS,) + w.shape, w.dtype) for w in ws],
        scratch_shapes=[pltpu.SemaphoreType.DMA((n, 6)), pltpu.SemaphoreType.DMA((n, 6))],
    )(*ws)
    own = 2 * lax.axis_index("x") + lax.axis_index("y")
    return [lax.dynamic_update_slice(g, w[None], (own, 0, 0, 0)) for g, w in zip(res, ws)]


def _pair_exchange(gs):
    n = len(gs)

    def body(*refs):
        ins, outs = refs[:n], refs[n:2 * n]
        send, recv = refs[2 * n:]
        x, y, c, _ = _place()
        copies = [_remote(_half(ins[t], 1 - c, gs[t].shape[2] // 2, 2), outs[t], send.at[t], recv.at[t], (x, y, 1 - c))
                  for t in range(n)]
        for cp in copies:
            cp.start()
        for cp in copies:
            cp.wait()

    return pl.pallas_call(
        body, name="grad_pair_exchange", in_specs=[ANY] * n, out_specs=[ANY] * n,
        out_shape=[jax.ShapeDtypeStruct(g.shape[:2] + (g.shape[2] // 2, g.shape[3]), F32) for g in gs],
        scratch_shapes=[pltpu.SemaphoreType.DMA((n,)), pltpu.SemaphoreType.DMA((n,))],
    )(*gs)


def _pair_sum(g, theirs, c_idx):
    n4, ly, r, cc = g.shape
    rh = r // 2
    tm = _tile(rh, 256)
    nt = rh // tm

    def body(c_ref, g_ref, t_ref, o_ref):
        o_ref[...] = g_ref[...] + t_ref[...]

    blk = (None, tm, cc)
    grid_spec = pltpu.PrefetchScalarGridSpec(
        num_scalar_prefetch=1, grid=(n4 * ly, nt),
        in_specs=[pl.BlockSpec(blk, lambda a, i, cr: (a, cr[0] * nt + i, 0)), pl.BlockSpec(blk, lambda a, i, cr: (a, i, 0))],
        out_specs=pl.BlockSpec(blk, lambda a, i, cr: (a, i, 0)))
    out = pl.pallas_call(
        body, name="grad_pair_sum", grid_spec=grid_spec, out_shape=jax.ShapeDtypeStruct((n4 * ly, rh, cc), F32),
        compiler_params=_params("parallel", "parallel"))(c_idx, g.reshape(n4 * ly, r, cc), theirs.reshape(n4 * ly, rh, cc))
    return out.reshape(n4, ly, rh, cc)


def _chip_exchange(ps):
    n = len(ps)

    def body(*refs):
        ins, outs = refs[:n], refs[n:2 * n]
        send, recv = refs[2 * n:]
        x, y, c, chips = _place()
        copies = []
        for t in range(n):
            for j, chip in enumerate(chips):
                copies.append(_remote(ins[t].at[2 * chip[0] + chip[1]], outs[t].at[j], send.at[t, j], recv.at[t, j], (*chip, c)))
        for cp in copies:
            cp.start()
        for cp in copies:
            cp.wait()

    return pl.pallas_call(
        body, name="grad_chip_exchange", in_specs=[ANY] * n, out_specs=[ANY] * n,
        out_shape=[jax.ShapeDtypeStruct((3,) + p.shape[1:], F32) for p in ps],
        scratch_shapes=[pltpu.SemaphoreType.DMA((n, 3)), pltpu.SemaphoreType.DMA((n, 3))],
    )(*ps)


def _chip_sum(p, landed, own_idx):
    _, ly, rh, cc = p.shape
    tm = _tile(rh, 256)

    def body(o_ref, p_ref, a_ref, b_ref, c_ref, out_ref):
        out_ref[...] = ((p_ref[...] + a_ref[...]) + b_ref[...]) + c_ref[...]

    blk = (None, None, tm, cc)
    slot = lambda j: pl.BlockSpec(blk, lambda l, i, o: (j, l, i, 0))
    grid_spec = pltpu.PrefetchScalarGridSpec(
        num_scalar_prefetch=1, grid=(ly, rh // tm),
        in_specs=[pl.BlockSpec(blk, lambda l, i, o: (o[0], l, i, 0)), slot(0), slot(1), slot(2)],
        out_specs=pl.BlockSpec((None, tm, cc), lambda l, i, o: (l, i, 0)))
    return pl.pallas_call(
        body, name="grad_chip_sum", grid_spec=grid_spec, out_shape=jax.ShapeDtypeStruct((ly, rh, cc), F32),
        compiler_params=_params("parallel", "parallel"))(own_idx, p, landed, landed, landed)


def _pair_swap(halves):
    n = len(halves)

    def body(*refs):
        ins, outs = refs[:n], refs[n:2 * n]
        send, recv = refs[2 * n:]
        x, y, c, _ = _place()
        copies = [_remote(ins[t], outs[t], send.at[t], recv.at[t], (x, y, 1 - c)) for t in range(n)]
        for cp in copies:
            cp.start()
        for cp in copies:
            cp.wait()

    return pl.pallas_call(
        body, name="grad_pair_swap", in_specs=[ANY] * n, out_specs=[ANY] * n,
        out_shape=[jax.ShapeDtypeStruct(h.shape, F32) for h in halves],
        scratch_shapes=[pltpu.SemaphoreType.DMA((n,)), pltpu.SemaphoreType.DMA((n,))],
    )(*halves)


def _adamw_big(w, m, v, mine, theirs, c_idx):
    ly, r, cc = w.shape
    rh = r // 2
    tm = _tile(rh, 256)
    nt = rh // tm

    def body(c_ref, w_ref, m_ref, v_ref, a_ref, b_ref, g_out, d_out, m_out, v_out):
        gr = jnp.where(pl.program_id(1) == c_ref[0], a_ref[...], b_ref[...])
        delta, mn, vn = _adamw_update(w_ref[...], gr, m_ref[...], v_ref[...])
        g_out[...] = gr
        d_out[...] = delta
        m_out[...] = mn
        v_out[...] = vn

    blk = (None, tm, cc)
    full = pl.BlockSpec(blk, lambda l, hc, i, cr: (l, hc * nt + i, 0))
    half = pl.BlockSpec(blk, lambda l, hc, i, cr: (l, i, 0))
    grid_spec = pltpu.PrefetchScalarGridSpec(
        num_scalar_prefetch=1, grid=(ly, 2, nt), in_specs=[full, full, full, half, half], out_specs=[full] * 4)
    sd = jax.ShapeDtypeStruct(w.shape, F32)
    return pl.pallas_call(
        body, name="adamw", grid_spec=grid_spec, out_shape=[sd] * 4,
        compiler_params=_params("parallel", "parallel", "parallel"))(c_idx, w, m, v, mine, theirs)


def _allreduce_small(v):
    rows, w = v.shape

    def body(x_ref, sum_ref, all_ref, send, recv, local):
        x, y, c, chips = _place()
        me, sibling = (x, y, c), (x, y, 1 - c)

        def slot(px, py, pc):
            return all_ref.at[4 * px + 2 * py + pc]

        def copy(k, block, to, src=None):
            return _remote(slot(*block) if src is None else src, slot(*block), send.at[k], recv.at[k], to)

        mine = pltpu.make_async_copy(x_ref, slot(*me), local)
        mine.start()
        first = [copy(0, me, sibling, src=x_ref)]
        first += [copy(1 + j, me, (*chip, c), src=x_ref) for j, chip in enumerate(chips)]
        for cp in first:
            cp.start()
        passed = [copy(4 + j, (*chip, c), sibling) for j, chip in enumerate(chips)]
        for j, chip in enumerate(chips):
            copy(1 + j, (*chip, c), me).wait_recv()
            passed[j].start()
        copy(0, sibling, me).wait_recv()
        for j, chip in enumerate(chips):
            copy(4 + j, (*chip, 1 - c), me).wait_recv()
        for cp in first + passed:
            cp.wait_send()
        mine.wait()
        tot = all_ref[0]
        for k in range(1, N_DEV):
            tot = tot + all_ref[k]
        sum_ref[...] = tot

    vm = pl.BlockSpec(memory_space=pltpu.VMEM)
    return pl.pallas_call(
        body, name="allreduce_small", in_specs=[vm], out_specs=[vm, vm],
        out_shape=[jax.ShapeDtypeStruct((rows, w), F32), jax.ShapeDtypeStruct((N_DEV, rows, w), F32)],
        scratch_shapes=[pltpu.SemaphoreType.DMA((7,)), pltpu.SemaphoreType.DMA((7,)), pltpu.SemaphoreType.DMA],
        compiler_params=pltpu.CompilerParams(vmem_limit_bytes=VMEM_LIMIT),
    )(v)[0]


BIG = ["ffn1_w1", "ffn1_w3", "ffn1_w2", "ffn2_w1", "ffn2_w3", "ffn2_w2", "ple_proj", "ple_gate",
       "s5_w_in", "s5_w_glu", "sb_w_qkv", "sb_w_o"]
SMALL = ["ffn1_norm", "mix_norm", "ffn2_norm", "ple_norm", "s5_a_re", "s5_a_im", "s5_log_dt", "s5_b_re", "s5_b_im",
         "s5_c_re", "s5_c_im", "s5_d", "final_norm"]
ORDER = ["ffn1_norm", "ffn1_w1", "ffn1_w3", "ffn1_w2", "mix_norm", "ffn2_norm", "ffn2_w1", "ffn2_w3", "ffn2_w2",
         "ple_norm", "ple_proj", "ple_gate", "s5_w_in", "s5_a_re", "s5_a_im", "s5_log_dt", "s5_b_re", "s5_b_im",
         "s5_c_re", "s5_c_im", "s5_d", "s5_w_glu", "sb_w_qkv", "sb_w_o", "final_norm"]


def _pack(arrays):
    flat = jnp.concatenate([a.reshape(-1) for a in arrays])
    pad = (-flat.shape[0]) % 1024
    return jnp.pad(flat, (0, pad)).reshape(-1, 128)


def _unpack(packed, like):
    flat = packed.reshape(-1)
    out, off = [], 0
    for a in like:
        out.append(flat[off:off + a.size].reshape(a.shape))
        off += a.size
    return out


def _fwd_bwd(x, p, target, w, gathered):
    bl, l, d = x.shape
    t = bl * l
    depth = w["ffn1_norm"].shape[0]
    s5_ops, s5_vjp = jax.vjp(_s5_prep, w["s5_a_re"][0], w["s5_a_im"][0], w["s5_log_dt"][0], w["s5_b_re"][0],
                             w["s5_b_im"][0], w["s5_c_re"][0], w["s5_c_im"][0])

    h = x.reshape(t, d)
    p2 = [p[i].reshape(t, p.shape[-1]).astype(BF16) for i in range(depth)]
    saved = []
    for i in range(depth):
        norm = lambda name: w[name][i:i + 1]
        h, s1 = _ffn_fwd(h, norm("ffn1_norm"), gathered["ffn1_w1"], gathered["ffn1_w3"], gathered["ffn1_w2"], i)
        if i % 2 == 0:
            h, s2 = _s5_fwd(h, norm("mix_norm"), s5_ops, w["s5_d"][i // 2:i // 2 + 1], gathered["s5_w_in"], gathered["s5_w_glu"], bl)
        else:
            h, s2 = _sb_fwd(h, norm("mix_norm"), gathered["sb_w_qkv"], gathered["sb_w_o"], bl)
        h, s3 = _ffn_fwd(h, norm("ffn2_norm"), gathered["ffn2_w1"], gathered["ffn2_w3"], gathered["ffn2_w2"], i)
        h, s4 = _ple_fwd(h, norm("ple_norm"), p2[i], gathered["ple_proj"], gathered["ple_gate"], i)
        saved.append((s1, s2, s3, s4))

    loss, dh, dfinal = _head(h, w["final_norm"].reshape(1, d), target.reshape(t, d))

    big = {k: None for k in BIG}
    small = {k: [None] * w[k].shape[0] if w[k].ndim > 1 else None for k in SMALL}
    small["final_norm"] = dfinal.reshape(d)
    for i in reversed(range(depth)):
        norm = lambda name: w[name][i:i + 1]
        slots = lambda *names: [(big[k], i, depth) for k in names]
        s1, s2, s3, s4 = saved[i]
        dh, dg, big["ple_proj"], big["ple_gate"] = _ple_bwd(
            dh, s4, norm("ple_norm"), p2[i], gathered["ple_proj"], gathered["ple_gate"], i, slots("ple_proj", "ple_gate"))
        small["ple_norm"][i] = dg[0]
        dh, dg, big["ffn2_w1"], big["ffn2_w3"], big["ffn2_w2"] = _ffn_bwd(
            dh, s3, norm("ffn2_norm"), gathered["ffn2_w1"], gathered["ffn2_w3"], gathered["ffn2_w2"], i,
            slots("ffn2_w1", "ffn2_w3", "ffn2_w2"))
        small["ffn2_norm"][i] = dg[0]
        if i % 2 == 0:
            dh, dg, big["s5_w_in"], big["s5_w_glu"], dd, dops = _s5_bwd(
                dh, s2, norm("mix_norm"), s5_ops, w["s5_d"][i // 2:i // 2 + 1], gathered["s5_w_in"], gathered["s5_w_glu"], bl)
            small["s5_d"][0] = dd[0]
            raw = s5_vjp(dops)
            for name, gr in zip(["s5_a_re", "s5_a_im", "s5_log_dt", "s5_b_re", "s5_b_im", "s5_c_re", "s5_c_im"], raw):
                small[name][0] = gr
        else:
            dh, dg, big["sb_w_qkv"], big["sb_w_o"] = _sb_bwd(dh, s2, norm("mix_norm"), gathered["sb_w_qkv"], gathered["sb_w_o"], bl)
        small["mix_norm"][i] = dg[0]
        dh, dg, big["ffn1_w1"], big["ffn1_w3"], big["ffn1_w2"] = _ffn_bwd(
            dh, s1, norm("ffn1_norm"), gathered["ffn1_w1"], gathered["ffn1_w3"], gathered["ffn1_w2"], i,
            slots("ffn1_w1", "ffn1_w3", "ffn1_w2"))
        small["ffn1_norm"][i] = dg[0]
    small_list = [jnp.stack(small[k]) if isinstance(small[k], list) else small[k] for k in SMALL]
    return loss, dh.reshape(bl, l, d), big, small_list


def _step(x, p, target, w, m, v):
    gathered = dict(zip(BIG, _allgather_weights([w[k].astype(BF16) for k in BIG])))
    loss, grad_x, big, small_list = _fwd_bwd(x, p, target, w, gathered)

    c_idx = lax.axis_index("c").astype(jnp.int32).reshape(1)
    own_idx = (2 * lax.axis_index("x") + lax.axis_index("y")).astype(jnp.int32).reshape(1)
    partial = [big[k] for k in BIG]
    pair = [_pair_sum(g, t, c_idx) for g, t in zip(partial, _pair_exchange(partial))]
    mine = [_chip_sum(pr, ld, own_idx) for pr, ld in zip(pair, _chip_exchange(pair))]
    theirs = _pair_swap(mine)
    out_g, out_d, out_m, out_v = {}, {}, {}, {}
    for k, a, b in zip(BIG, mine, theirs):
        out_g[k], out_d[k], out_m[k], out_v[k] = _adamw_big(w[k], m[k], v[k], a, b, c_idx)

    like = [w[k] for k in SMALL]
    g_small = _allreduce_small(_pack(small_list))
    packed = (g_small,) + tuple(_adamw_small(_pack(like), g_small, _pack([m[k] for k in SMALL]), _pack([v[k] for k in SMALL])))
    for dst, pk in zip((out_g, out_d, out_m, out_v), packed):
        dst.update(dict(zip(SMALL, _unpack(pk, like))))

    loss = lax.psum(loss, ("x", "y", "c"))
    return (loss, grad_x, *[out_g[k] for k in ORDER], *[out_d[k] for k in ORDER],
            *[out_m[k] for k in ORDER], *[out_v[k] for k in ORDER])


def kernel(x, p, ffn1_norm, ffn1_w1, ffn1_w3, ffn1_w2, mix_norm, ffn2_norm, ffn2_w1, ffn2_w3, ffn2_w2, ple_norm, ple_proj, ple_gate, s5_w_in, s5_a_re, s5_a_im, s5_log_dt, s5_b_re, s5_b_im, s5_c_re, s5_c_im, s5_d, s5_w_glu, sb_w_qkv, sb_w_o, final_norm, loss_target, m_ffn1_norm, m_ffn1_w1, m_ffn1_w3, m_ffn1_w2, m_mix_norm, m_ffn2_norm, m_ffn2_w1, m_ffn2_w3, m_ffn2_w2, m_ple_norm, m_ple_proj, m_ple_gate, m_s5_w_in, m_s5_a_re, m_s5_a_im, m_s5_log_dt, m_s5_b_re, m_s5_b_im, m_s5_c_re, m_s5_c_im, m_s5_d, m_s5_w_glu, m_sb_w_qkv, m_sb_w_o, m_final_norm, v_ffn1_norm, v_ffn1_w1, v_ffn1_w3, v_ffn1_w2, v_mix_norm, v_ffn2_norm, v_ffn2_w1, v_ffn2_w3, v_ffn2_w2, v_ple_norm, v_ple_proj, v_ple_gate, v_s5_w_in, v_s5_a_re, v_s5_a_im, v_s5_log_dt, v_s5_b_re, v_s5_b_im, v_s5_c_re, v_s5_c_im, v_s5_d, v_s5_w_glu, v_sb_w_qkv, v_sb_w_o, v_final_norm):
    args = dict(locals())
    w = {k: args[k] for k in ORDER}
    m = {k: args["m_" + k] for k in ORDER}
    v = {k: args["v_" + k] for k in ORDER}
    return _step(x, p, loss_target, w, m, v)
```

```python
import functools
import math

import jax
import jax.numpy as jnp
from jax import lax
from jax.experimental import pallas as pl
from jax.experimental.pallas import tpu as pltpu

F32 = jnp.float32
BF16 = jnp.bfloat16
MESH = pl.DeviceIdType.MESH

N_CHIPS = 4
N_DEV = 8
RMS_EPS = 1e-6
S5_GROUP = 16
S5_STATE = 64
S5_CHUNK = 16
SB_HEAD_DIM = 64
SB_BLOCK = 128
ADAM_LR, ADAM_B1, ADAM_B2, ADAM_EPS, ADAM_WD, ADAM_STEP = 0.001, 0.9, 0.999, 1e-08, 0.01, 10
VMEM_LIMIT = 48 * 1024 * 1024

NN = (((1,), (0,)), ((), ()))
NT = (((1,), (1,)), ((), ()))
TN = (((0,), (0,)), ((), ()))

ANY = pl.BlockSpec(memory_space=pl.ANY)


def _tile(n, target):
    if n <= target:
        return n
    for t in range(target - target % 8, 7, -8):
        if n % t == 0:
            return t
    raise ValueError(f"no row tile for {n}")


def _params(*semantics):
    return pltpu.CompilerParams(dimension_semantics=semantics, vmem_limit_bytes=VMEM_LIMIT)


def _sigmoid(v):
    return 1.0 / (1.0 + jnp.exp(-v))


def _gemm(name, grid, operands, in_specs, groups, acc_shapes, out_shapes, out_specs, epilogue, reduce_axis=None, aliases=None):
    n_in, n_out = len(operands), len(out_shapes)
    n_red = None if reduce_axis is None else grid[reduce_axis]

    def body(*refs):
        ins, outs, accs = refs[:n_in], refs[n_in:n_in + n_out], refs[n_in + n_out:]

        def products():
            res = []
            for terms in groups:
                tot = None
                for ia, ib, dims in terms:
                    d = lax.dot_general(ins[ia][...], ins[ib][...], dims, preferred_element_type=F32)
                    tot = d if tot is None else tot + d
                res.append(tot)
            return res

        def finish(vals):
            for o, v in zip(outs, epilogue(vals, ins)):
                o[...] = v.astype(o.dtype)

        if reduce_axis is None:
            finish(products())
        else:
            k = pl.program_id(reduce_axis)

            @pl.when(k == 0)
            def _():
                for a in accs:
                    a[...] = jnp.zeros_like(a)

            for a, d in zip(accs, products()):
                a[...] += d

            @pl.when(k == n_red - 1)
            def _():
                finish([a[...] for a in accs])

    scratch = [] if reduce_axis is None else [pltpu.VMEM(s, F32) for s in acc_shapes]
    sem = tuple("arbitrary" if i == reduce_axis else "parallel" for i in range(len(grid)))
    return pl.pallas_call(
        body, name=name, grid=grid, in_specs=in_specs, out_specs=out_specs, out_shape=out_shapes,
        scratch_shapes=scratch, input_output_aliases=aliases or {}, compiler_params=_params(*sem))(*operands)


def _ident(vals, ins):
    return vals


def _act_spec(layout, tm, cs, pos):
    if layout == "sm":
        return pl.BlockSpec((None, tm, cs), lambda *g: (pos(*g)[1], pos(*g)[0], 0))
    return pl.BlockSpec((tm, cs), lambda *g: pos(*g))


def _act_shape(layout, t, cs, dtype):
    return jax.ShapeDtypeStruct((N_CHIPS, t, cs) if layout == "sm" else (t, N_CHIPS * cs), dtype)


def _w_spec(w, layer, pos_k):
    _, _, r, c = w.shape
    return pl.BlockSpec((None, None, r, c), lambda *g: (pos_k(*g), layer, 0, 0))


def _mm_cs(name, x, w, layer, out_layout, out_dtype, tm=512):
    t, kd = x.shape
    cs = w.shape[3]
    tm = _tile(t, tm)
    return _gemm(
        name, (N_CHIPS, t // tm), [x, w],
        [pl.BlockSpec((tm, kd), lambda k, i: (i, 0)), _w_spec(w, layer, lambda k, i: k)],
        [[(0, 1, NN)]], None, [_act_shape(out_layout, t, cs, out_dtype)],
        [_act_spec(out_layout, tm, cs, lambda k, i: (i, k))], _ident)[0]


def _mm_rs(name, xs, layout, w, layer, res=None, alpha=1.0, out_dtype=F32, tm=512):
    ks, n = w.shape[2], w.shape[3]
    t = xs.shape[1] if layout == "sm" else xs.shape[0]
    tm = _tile(t, tm)
    operands = [xs, w] + ([] if res is None else [res])
    specs = [_act_spec(layout, tm, ks, lambda i, k: (i, k)), _w_spec(w, layer, lambda i, k: k)]
    if res is not None:
        specs.append(pl.BlockSpec((tm, n), lambda i, k: (i, 0)))

    def epilogue(vals, ins):
        y = alpha * vals[0]
        return [y if res is None else ins[2][...] + y]

    return _gemm(
        name, (t // tm, N_CHIPS), operands, specs, [[(0, 1, NN)]], [(tm, n)],
        [jax.ShapeDtypeStruct((t, n), out_dtype)], [pl.BlockSpec((tm, n), lambda i, k: (i, 0))],
        epilogue, reduce_axis=1)[0]


def _mm_cs_dx(name, pairs, layout, layer, tm=512):
    w0 = pairs[0][1]
    kd, cs = w0.shape[2], w0.shape[3]
    dy0 = pairs[0][0]
    t = dy0.shape[1] if layout == "sm" else dy0.shape[0]
    tm = _tile(t, tm)
    operands, specs, terms = [], [], []
    for dy, w in pairs:
        terms.append((len(operands), len(operands) + 1, NT))
        operands += [dy, w]
        specs += [_act_spec(layout, tm, cs, lambda i, k: (i, k)), _w_spec(w, layer, lambda i, k: k)]
    return _gemm(
        name, (t // tm, N_CHIPS), operands, specs, [terms], [(tm, kd)],
        [jax.ShapeDtypeStruct((t, kd), F32)], [pl.BlockSpec((tm, kd), lambda i, k: (i, 0))],
        _ident, reduce_axis=1)[0]


def _mm_rs_dx(name, dy, w, layer, out_layout, out_dtype, tm=512):
    t, n = dy.shape
    ks = w.shape[2]
    tm = _tile(t, tm)
    return _gemm(
        name, (N_CHIPS, t // tm), [dy, w],
        [pl.BlockSpec((tm, n), lambda k, i: (i, 0)), _w_spec(w, layer, lambda k, i: k)],
        [[(0, 1, NT)]], None, [_act_shape(out_layout, t, ks, out_dtype)],
        [_act_spec(out_layout, tm, ks, lambda k, i: (i, k))], _ident)[0]


def _mm_dw(name, x, x_layout, dy, dy_layout, slot, alpha=1.0, tk=512):
    stack, layer, layers = slot
    if x_layout is None:
        t, rows = x.shape
        cols = dy.shape[2] if dy_layout == "sm" else dy.shape[1] // N_CHIPS
        tk = _tile(t, tk)
        xspec = pl.BlockSpec((tk, rows), lambda k, j: (j, 0))
        yspec = _act_spec(dy_layout, tk, cols, lambda k, j: (j, k))
    else:
        t, cols = dy.shape
        rows = x.shape[2] if x_layout == "sm" else x.shape[1] // N_CHIPS
        tk = _tile(t, tk)
        xspec = _act_spec(x_layout, tk, rows, lambda k, j: (j, k))
        yspec = pl.BlockSpec((tk, cols), lambda k, j: (j, 0))
    operands, specs = [x, dy], [xspec, yspec]
    if stack is not None:
        operands.append(stack)
        specs.append(ANY)
    return _gemm(
        name, (N_CHIPS, t // tk), operands, specs, [[(0, 1, TN)]], [(rows, cols)],
        [jax.ShapeDtypeStruct((N_CHIPS, layers, rows, cols), F32)],
        [pl.BlockSpec((None, None, rows, cols), lambda k, j: (k, layer, 0, 0))],
        lambda vals, ins: [alpha * vals[0]], reduce_axis=1, aliases=None if stack is None else {2: 0})[0]


def _rows(name, fn, ins, outs, accs=(), tm=256):
    t = ins[0].shape[0]
    tm = _tile(t, tm)
    n_in, n_out, n_acc = len(ins), len(outs), len(accs)
    in_specs = []
    for a in ins:
        if a.shape[0] == t:
            in_specs.append(pl.BlockSpec((tm, a.shape[1]), lambda i: (i, 0)))
        else:
            in_specs.append(pl.BlockSpec(a.shape, lambda i: (0, 0)))
    out_shape = [jax.ShapeDtypeStruct((t, c), d) for c, d in outs] + [jax.ShapeDtypeStruct(s, F32) for s in accs]
    out_specs = [pl.BlockSpec((tm, c), lambda i: (i, 0)) for c, _ in outs] + [pl.BlockSpec(s, lambda i: (0, 0)) for s in accs]

    def body(*refs):
        i = pl.program_id(0)
        row_vals, acc_vals = fn(*[r[...] for r in refs[:n_in]])
        for o, v in zip(refs[n_in:n_in + n_out], row_vals):
            o[...] = v.astype(o.dtype)
        acc_refs = refs[n_in + n_out:]
        if n_acc:
            @pl.when(i == 0)
            def _():
                for a in acc_refs:
                    a[...] = jnp.zeros_like(a)

            for a, v in zip(acc_refs, acc_vals):
                a[...] += v

    res = pl.pallas_call(
        body, name=name, grid=(t // tm,), in_specs=in_specs, out_specs=out_specs, out_shape=out_shape,
        compiler_params=_params("arbitrary" if n_acc else "parallel"))(*ins)
    return res[:n_out], res[n_out:]


def _rms_stats(x):
    return lax.rsqrt(jnp.mean(x * x, axis=-1, keepdims=True) + RMS_EPS)


def _rmsnorm(name, h, g):
    def fn(x, gg):
        return [x * _rms_stats(x) * gg], []
    return _rows(name, fn, [h, g], [(h.shape[1], BF16)])[0][0]


def _rms_bwd_math(dn, x, g):
    r = _rms_stats(x)
    xhat = x * r
    dxh = dn * g
    dx = r * (dxh - xhat * jnp.mean(dxh * xhat, axis=-1, keepdims=True))
    return dx, jnp.sum(dn * xhat, axis=0, keepdims=True)


def _rmsnorm_bwd(name, dres, dn, h, g):
    def fn(dr, d, x, gg):
        dx, dg = _rms_bwd_math(d, x, gg)
        return [dr + dx], [dg]
    (dh,), (dg,) = _rows(name, fn, [dres, dn, h, g], [(h.shape[1], F32)], [(1, h.shape[1])])
    return dh, dg


def _ffn_fwd(h, g, w1, w3, w2, layer, tm=512):
    t, d = h.shape
    fs = w1.shape[3]
    n = _rmsnorm("ffn_norm", h, g)
    tm = _tile(t, tm)

    def up(vals, ins):
        a, b = vals
        return [a, b, a * _sigmoid(a) * b]

    sm = _act_shape("sm", t, fs, BF16)
    osp = _act_spec("sm", tm, fs, lambda k, i: (i, k))
    a, b, s = _gemm(
        "ffn_up", (N_CHIPS, t // tm), [n, w1, w3],
        [pl.BlockSpec((tm, d), lambda k, i: (i, 0)), _w_spec(w1, layer, lambda k, i: k), _w_spec(w3, layer, lambda k, i: k)],
        [[(0, 1, NN)], [(0, 2, NN)]], None, [sm, sm, sm], [osp, osp, osp], up)
    out = _mm_rs("ffn_down", s, "sm", w2, layer, res=h, alpha=0.5)
    return out, (h, n, a, b, s)


def _ffn_bwd(dout, saved, g, w1, w3, w2, layer, slots, tm=512):
    h, n, a, b, s = saved
    t, d = h.shape
    fs = w1.shape[3]
    tm = _tile(t, tm)
    dob = dout.astype(BF16)

    def down(vals, ins):
        ds = 0.5 * vals[0]
        av, bv = ins[2][...].astype(F32), ins[3][...].astype(F32)
        sg = _sigmoid(av)
        return [ds * bv * sg * (1.0 + av * (1.0 - sg)), ds * av * sg]

    sm = _act_shape("sm", t, fs, BF16)
    asp = _act_spec("sm", tm, fs, lambda k, i: (i, k))
    da, db = _gemm(
        "ffn_down_dx", (N_CHIPS, t // tm), [dob, w2, a, b],
        [pl.BlockSpec((tm, d), lambda k, i: (i, 0)), _w_spec(w2, layer, lambda k, i: k), asp, asp],
        [[(0, 1, NT)]], None, [sm, sm], [asp, asp], down)
    dw2 = _mm_dw("ffn_dw2", s, "sm", dob, None, slots[2], alpha=0.5)
    dw1 = _mm_dw("ffn_dw1", n, None, da, "sm", slots[0])
    dw3 = _mm_dw("ffn_dw3", n, None, db, "sm", slots[1])
    dn = _mm_cs_dx("ffn_up_dx", [(da, w1), (db, w3)], "sm", layer)
    dh, dg = _rmsnorm_bwd("ffn_norm_bwd", dout, dn, h, g)
    return dh, dg, dw1, dw3, dw2


def _ple_fwd(h, g, p2, wproj, wgate, layer):
    n = _rmsnorm("ple_norm", h, g)
    gl = _mm_rs("ple_gate", n, "flat", wgate, layer)
    pp = _mm_cs("ple_proj", p2, wproj, layer, "flat", F32)

    def fn(hh, gg, q):
        return [hh + q * _sigmoid(gg)], []
    out = _rows("ple_mix", fn, [h, gl, pp], [(h.shape[1], F32)])[0][0]
    return out, (h, n, gl, pp)


def _ple_bwd(dout, saved, g, p2, wproj, wgate, layer, slots):
    h, n, gl, pp = saved
    d = h.shape[1]

    def fn(do, gg, q):
        sg = _sigmoid(gg)
        return [do * sg, do * q * sg * (1.0 - sg)], []
    (dpp, dgl), _ = _rows("ple_mix_bwd", fn, [dout, gl, pp], [(d, BF16), (d, BF16)])
    dwproj = _mm_dw("ple_dwproj", p2, None, dpp, "flat", slots[0])
    dwgate = _mm_dw("ple_dwgate", n, "flat", dgl, None, slots[1])
    dn = _mm_rs_dx("ple_gate_dx", dgl, wgate, layer, "flat", F32)
    dh, dg = _rmsnorm_bwd("ple_norm_bwd", dout, dn, h, g)
    return dh, dg, dwproj, dwgate


def _head(h, g, target):
    d = h.shape[1]

    def fn(x, gg, tg):
        y = x * _rms_stats(x) * gg
        err = y - tg
        dy = err * (1.0 / d)
        dx, dg = _rms_bwd_math(dy, x, gg)
        loss = 0.5 * jnp.sum(jnp.sum(err * err, axis=-1, keepdims=True) * (1.0 / d), axis=0, keepdims=True)
        return [dx], [dg, jnp.broadcast_to(loss, (1, 128))]
    (dh,), (dg, loss) = _rows("loss_head", fn, [h, g, target], [(d, F32)], [(1, d), (1, 128)])
    return loss[0, 0], dh, dg


def _s5_prep(a_re, a_im, log_dt, b_re, b_im, c_re, c_im):
    c = S5_CHUNK
    lam_re = jnp.minimum(a_re, -1e-4)
    lam_im = a_im
    dt = jnp.exp(log_dt)[:, None]
    ks = jnp.arange(c + 1, dtype=F32)[:, None, None]
    mag = jnp.exp(lam_re[None] * dt[None] * ks)
    ph = lam_im[None] * dt[None] * ks
    pw_re, pw_im = mag * jnp.cos(ph), mag * jnp.sin(ph)
    den = lam_re * lam_re + lam_im * lam_im
    nr, ni = pw_re[1] - 1.0, pw_im[1]
    fr = (nr * lam_re + ni * lam_im) / den
    fi = (ni * lam_re - nr * lam_im) / den
    bb_re = fr[..., None] * b_re - fi[..., None] * b_im
    bb_im = fr[..., None] * b_im + fi[..., None] * b_re
    ca_re = c_re[None] * pw_re[:, :, None, :] - c_im[None] * pw_im[:, :, None, :]
    ca_im = c_re[None] * pw_im[:, :, None, :] + c_im[None] * pw_re[:, :, None, :]
    hp = lax.Precision.HIGHEST
    kern = (jnp.einsum("kghp,gpj->kghj", ca_re[:c], bb_re, precision=hp)
            - jnp.einsum("kghp,gpj->kghj", ca_im[:c], bb_im, precision=hp))
    lag = jnp.arange(c)[None, :] - jnp.arange(c)[:, None]
    toep = jnp.where((lag >= 0)[:, :, None, None, None], kern[jnp.clip(lag, 0, c - 1)], 0.0)
    g = a_re.shape[0]
    wi = toep.transpose(2, 0, 4, 1, 3).reshape(g, c * S5_GROUP, c * S5_GROUP)
    rev_re, rev_im = pw_re[c - 1::-1][:c], pw_im[c - 1::-1][:c]
    wn_re = rev_re[..., None] * bb_re[None] - rev_im[..., None] * bb_im[None]
    wn_im = rev_re[..., None] * bb_im[None] + rev_im[..., None] * bb_re[None]
    wn = jnp.stack([wn_re, wn_im], axis=0).transpose(2, 1, 4, 0, 3).reshape(g, c * S5_GROUP, 2 * S5_STATE)
    wo = jnp.stack([ca_re[1:], -ca_im[1:]], axis=0).transpose(2, 0, 4, 1, 3).reshape(g, 2 * S5_STATE, c * S5_GROUP)
    ar, ai = pw_re[c], pw_im[c]
    m1 = jnp.concatenate([ar, ar], axis=1)
    m2 = jnp.concatenate([-ai, ai], axis=1)
    return jnp.concatenate([wi, wn], axis=2), wo, m1, m2


def _bmm(name, a, b, dims, out_dtype, gb=8):
    g = a.shape[0]
    gb = min(gb, g)
    m = a.shape[2] if dims == TN else a.shape[1]
    n = b.shape[1] if dims == NT else b.shape[2]

    def body(a_ref, b_ref, o_ref):
        for j in range(gb):
            o_ref[j] = lax.dot_general(a_ref[j], b_ref[j], dims, preferred_element_type=F32).astype(o_ref.dtype)

    return pl.pallas_call(
        body, name=name, grid=(g // gb,),
        in_specs=[pl.BlockSpec((gb,) + a.shape[1:], lambda i: (i, 0, 0)), pl.BlockSpec((gb,) + b.shape[1:], lambda i: (i, 0, 0))],
        out_specs=pl.BlockSpec((gb, m, n), lambda i: (i, 0, 0)),
        out_shape=jax.ShapeDtypeStruct((g, m, n), out_dtype), compiler_params=_params("parallel"))(a, b)


def _s5_scan_fwd(sloc, m1, m2):
    nc, r, w = sloc.shape

    def body(s_ref, m1_ref, m2_ref, o_ref):
        a1, a2 = m1_ref[...], m2_ref[...]

        def step(c, s):
            o_ref[c] = s
            return a1 * s + a2 * pltpu.roll(s, S5_STATE, 1) + s_ref[c]
        lax.fori_loop(0, nc, step, jnp.zeros((r, w), F32))

    vm = pl.BlockSpec(memory_space=pltpu.VMEM)
    return pl.pallas_call(
        body, name="s5_scan", in_specs=[vm, vm, vm], out_specs=vm,
        out_shape=jax.ShapeDtypeStruct(sloc.shape, F32),
        compiler_params=pltpu.CompilerParams(vmem_limit_bytes=VMEM_LIMIT))(sloc, m1, m2)


def _s5_scan_bwd(dsprev, sprev, m1, m2):
    nc, r, w = dsprev.shape

    def body(d_ref, s_ref, m1_ref, m2_ref, g_ref, p1_ref, p2_ref):
        a1, a2 = m1_ref[...], m2_ref[...]
        zero = jnp.zeros((r, w), F32)

        def step(i, carry):
            gp, p1, p2 = carry
            c = nc - 2 - i
            g_ref[c] = gp
            sp = s_ref[c]
            p1 = p1 + gp * sp
            p2 = p2 + gp * pltpu.roll(sp, S5_STATE, 1)
            return d_ref[c] + a1 * gp - a2 * pltpu.roll(gp, S5_STATE, 1), p1, p2

        g_ref[nc - 1] = zero
        _, p1, p2 = lax.fori_loop(0, nc - 1, step, (d_ref[nc - 1], zero, zero))
        p1_ref[...] = p1
        p2_ref[...] = p2

    vm = pl.BlockSpec(memory_space=pltpu.VMEM)
    sd = jax.ShapeDtypeStruct
    return pl.pallas_call(
        body, name="s5_scan_bwd", in_specs=[vm, vm, vm, vm], out_specs=[vm, vm, vm],
        out_shape=[sd(dsprev.shape, F32), sd((r, w), F32), sd((r, w), F32)],
        compiler_params=pltpu.CompilerParams(vmem_limit_bytes=VMEM_LIMIT))(dsprev, sprev, m1, m2)


def _to_groups(u, bl):
    t, d = u.shape
    g = d // S5_GROUP
    return u.reshape(t // S5_CHUNK, S5_CHUNK, g, S5_GROUP).transpose(2, 0, 1, 3).reshape(g, t // S5_CHUNK, S5_CHUNK * S5_GROUP)


def _from_groups(y):
    g, nct, _ = y.shape
    return y.reshape(g, nct, S5_CHUNK, S5_GROUP).transpose(1, 2, 0, 3).reshape(nct * S5_CHUNK, g * S5_GROUP)


def _to_scan(s, bl):
    g, nct, w = s.shape
    return s.reshape(g, bl, nct // bl, w).transpose(2, 1, 0, 3).reshape(nct // bl, bl * g, w)


def _from_scan(s, bl):
    nc, r, w = s.shape
    g = r // bl
    return s.reshape(nc, bl, g, w).transpose(2, 1, 0, 3).reshape(g, bl * nc, w)


def _gelu_tanh_parts(y):
    c0 = math.sqrt(2.0 / math.pi)
    inner = c0 * (y + 0.044715 * y * y * y)
    th = jnp.tanh(inner)
    return th, c0 * (1.0 + 3 * 0.044715 * y * y)


def _s5_fwd(h, g, ops, d_skip, w_in, w_glu, bl):
    wcat, wo, m1, m2 = ops
    t, d = h.shape
    ch = S5_CHUNK * S5_GROUP
    hn = _rmsnorm("mix_norm", h, g)
    u = _mm_rs("s5_in", hn, "flat", w_in, 0)
    ug = _to_groups(u.astype(BF16), bl)
    x = _bmm("s5_chunk_in", ug, wcat.astype(BF16), NN, F32)
    sprev = _s5_scan_fwd(_to_scan(x[:, :, ch:], bl), jnp.tile(m1, (bl, 1)), jnp.tile(m2, (bl, 1)))
    sprev_g = _from_scan(sprev, bl).astype(BF16)
    y_state = _bmm("s5_chunk_out", sprev_g, wo.astype(BF16), NN, F32)
    y = _from_groups(x[:, :, :ch] + y_state)

    def fn(yy, uu, dd):
        y2 = yy + dd * uu
        th, _ = _gelu_tanh_parts(y2)
        return [0.5 * y2 * (1.0 + th)], []
    z = _rows("s5_gelu", fn, [y, u, d_skip], [(d, BF16)])[0][0]
    zz = _mm_cs("s5_glu", z, w_glu, 0, "sm", F32)
    half = d // 2

    def glu(hh, zo0, zo1, zg0, zg1):
        m = jnp.concatenate([zo0 * _sigmoid(zg0), zo1 * _sigmoid(zg1)], axis=1)
        return [hh + m], []
    out = _rows("s5_glu_mix", glu, [h, zz[0], zz[1], zz[2], zz[3]], [(d, F32)])[0][0]
    return out, (h, hn, u, ug, sprev, sprev_g, y, z, zz)


def _s5_bwd(dout, saved, g, ops, d_skip, w_in, w_glu, bl):
    h, hn, u, ug, sprev, sprev_g, y, z, zz = saved
    wcat, wo, m1, m2 = ops
    t, d = h.shape
    half = d // 2
    ch = S5_CHUNK * S5_GROUP

    def glu_bwd(do, zo0, zo1, zg0, zg1):
        outs = []
        for j, (zo, zg) in enumerate(((zo0, zg0), (zo1, zg1))):
            dm = do[:, j * half:(j + 1) * half]
            sg = _sigmoid(zg)
            outs.append((dm * sg, dm * zo * sg * (1.0 - sg)))
        return [outs[0][0], outs[1][0], outs[0][1], outs[1][1]], []
    dz4, _ = _rows("s5_glu_bwd", glu_bwd, [dout, zz[0], zz[1], zz[2], zz[3]], [(half, BF16)] * 4)
    dzz = jnp.stack(dz4, axis=0)
    dwglu = _mm_dw("s5_dwglu", z, None, dzz, "sm", (None, 0, 1))
    dz = _mm_cs_dx("s5_glu_dx", [(dzz, w_glu)], "sm", 0)

    def gelu_bwd(dzv, yy, uu, dd):
        y2 = yy + dd * uu
        th, dinner = _gelu_tanh_parts(y2)
        dy2 = dzv * (0.5 * (1.0 + th) + 0.5 * y2 * (1.0 - th * th) * dinner)
        return [dy2, dy2 * dd], [jnp.sum(dy2 * uu, axis=0, keepdims=True)]
    (dy_b, du_skip), (dd,) = _rows("s5_gelu_bwd", gelu_bwd, [dz, y, u, d_skip], [(d, BF16), (d, F32)], [(1, d)])
    dyg = _to_groups(dy_b, bl)
    wo_b, wcat_b = wo.astype(BF16), wcat.astype(BF16)
    dsprev = _bmm("s5_chunk_out_dx", dyg, wo_b, NT, F32)
    m1t, m2t = jnp.tile(m1, (bl, 1)), jnp.tile(m2, (bl, 1))
    dsloc, p1, p2 = _s5_scan_bwd(_to_scan(dsprev, bl), sprev, m1t, m2t)
    dcat = jnp.concatenate([dyg, _from_scan(dsloc, bl).astype(BF16)], axis=2)
    dug = _bmm("s5_chunk_in_dx", dcat, wcat_b, NT, F32)
    dwcat = _bmm("s5_chunk_in_dw", ug, dcat, TN, F32)
    dwo = _bmm("s5_chunk_out_dw", sprev_g, dyg, TN, F32)
    gcount = d // S5_GROUP
    dm1 = p1.reshape(bl, gcount, 2 * S5_STATE).sum(axis=0)
    dm2 = p2.reshape(bl, gcount, 2 * S5_STATE).sum(axis=0)
    du = (_from_groups(dug) + du_skip).astype(BF16)
    dwin = _mm_dw("s5_dwin", hn, "flat", du, None, (None, 0, 1))
    dhn = _mm_rs_dx("s5_in_dx", du, w_in, 0, "flat", F32)
    dh, dg = _rmsnorm_bwd("mix_norm_bwd", dout, dhn, h, g)
    return dh, dg, dwin, dwglu, dd, (dwcat, dwo, dm1, dm2)


def _sb_scores(q, kblk, diag, row, col):
    z = lax.dot_general(q, kblk, NT, preferred_element_type=F32) * (SB_HEAD_DIM ** -0.5)
    l1 = jnp.log(1.0 + jnp.exp(-jnp.abs(z)))
    ls = jnp.minimum(z, 0.0) - l1
    mask = jnp.logical_or(col < row, jnp.logical_not(diag))
    lk = jnp.where(mask, ls - z, 0.0)
    return ls, lk, mask


def _split_dot(v, tri):
    hi = v.astype(BF16)
    lo = (v - hi.astype(F32)).astype(BF16)
    return (jnp.dot(hi, tri, preferred_element_type=F32) + jnp.dot(lo, tri, preferred_element_type=F32))


def _sb_attn_fwd(q, k, v):
    bh, l, dh = q.shape
    tb = min(SB_BLOCK, l)
    nq = l // tb

    def body(q_ref, k_ref, v_ref, o_ref):
        qi = pl.program_id(1)
        qv = q_ref[...]
        row = lax.broadcasted_iota(jnp.int32, (tb, tb), 0)
        col = lax.broadcasted_iota(jnp.int32, (tb, tb), 1)
        tri = (row > col).astype(BF16)

        def step(carry):
            j, acc, cr = carry
            ks = pl.multiple_of((qi - j) * tb, tb)
            kblk, vblk = k_ref[pl.ds(ks, tb), :], v_ref[pl.ds(ks, tb), :]
            ls, lk, mask = _sb_scores(qv, kblk, j == 0, row, col)
            later = _split_dot(lk, tri)
            att = jnp.where(mask, jnp.exp(ls + later + cr), 0.0)
            acc = acc + jnp.dot(att.astype(BF16), vblk, preferred_element_type=F32)
            return j + 1, acc, cr + jnp.sum(lk, axis=1, keepdims=True)

        acc, _ = lax.fori_loop(0, qi + 1, lambda j, cy: step((j,) + cy)[1:],
                               (jnp.zeros((tb, dh), F32), jnp.zeros((tb, 1), F32)))
        o_ref[...] = acc

    blk = pl.BlockSpec((None, tb, dh), lambda b, i: (b, i, 0))
    full = pl.BlockSpec((None, l, dh), lambda b, i: (b, 0, 0))
    return pl.pallas_call(
        body, name="sb_attn", grid=(bh, nq), in_specs=[blk, full, full], out_specs=blk,
        out_shape=jax.ShapeDtypeStruct((bh, l, dh), F32), compiler_params=_params("parallel", "parallel"))(q, k, v)


def _sb_attn_bwd(q, k, v, o, do):
    bh, l, dh = q.shape
    tb = min(SB_BLOCK, l)
    nq = l // tb
    scale = SB_HEAD_DIM ** -0.5

    def body(q_ref, k_ref, v_ref, o_ref, do_ref, dq_ref, dk_ref, dv_ref):
        qi = pl.program_id(1)

        @pl.when(qi == 0)
        def _():
            dk_ref[...] = jnp.zeros_like(dk_ref)
            dv_ref[...] = jnp.zeros_like(dv_ref)

        qv = q_ref[...]
        dob = do_ref[...].astype(BF16)
        dsum = jnp.sum(dob.astype(F32) * o_ref[...], axis=1, keepdims=True)
        row = lax.broadcasted_iota(jnp.int32, (tb, tb), 0)
        col = lax.broadcasted_iota(jnp.int32, (tb, tb), 1)
        tri = (row > col).astype(BF16)
        tri_inc = (row >= col).astype(BF16)

        def step(carry):
            j, dq, cr, ce = carry
            ks = pl.multiple_of((qi - j) * tb, tb)
            kblk, vblk = k_ref[pl.ds(ks, tb), :], v_ref[pl.ds(ks, tb), :]
            ls, lk, mask = _sb_scores(qv, kblk, j == 0, row, col)
            later = _split_dot(lk, tri)
            att = jnp.where(mask, jnp.exp(ls + later + cr), 0.0).astype(BF16)
            datt = lax.dot_general(dob, vblk, NT, preferred_element_type=F32)
            e = att.astype(F32) * datt
            pre = dsum - ce - _split_dot(e, tri_inc)
            sg = jnp.exp(ls)
            dz = (jnp.where(mask, e * (1.0 - sg) - pre * sg, 0.0) * scale).astype(BF16)
            dq = dq + jnp.dot(dz, kblk, preferred_element_type=F32)
            dk_ref[pl.ds(ks, tb), :] += lax.dot_general(dz, qv, TN, preferred_element_type=F32)
            dv_ref[pl.ds(ks, tb), :] += lax.dot_general(att, dob, TN, preferred_element_type=F32)
            return j + 1, dq, cr + jnp.sum(lk, axis=1, keepdims=True), ce + jnp.sum(e, axis=1, keepdims=True)

        zc = jnp.zeros((tb, 1), F32)
        dq, _, _ = lax.fori_loop(0, qi + 1, lambda j, cy: step((j,) + cy)[1:], (jnp.zeros((tb, dh), F32), zc, zc))
        dq_ref[...] = dq

    blk = pl.BlockSpec((None, tb, dh), lambda b, i: (b, i, 0))
    full = pl.BlockSpec((None, l, dh), lambda b, i: (b, 0, 0))
    sd = jax.ShapeDtypeStruct((bh, l, dh), F32)
    return pl.pallas_call(
        body, name="sb_attn_bwd", grid=(bh, nq), in_specs=[blk, full, full, blk, blk], out_specs=[blk, full, full],
        out_shape=[sd, sd, sd], compiler_params=_params("parallel", "arbitrary"))(q, k, v, o, do)


def _to_heads(x, bl):
    t, w = x.shape
    heads = w // SB_HEAD_DIM
    l = t // bl
    return x.reshape(bl, l, heads, SB_HEAD_DIM).transpose(0, 2, 1, 3).reshape(bl * heads, l, SB_HEAD_DIM)


def _from_heads(x, bl):
    bh, l, dh = x.shape
    heads = bh // bl
    return x.reshape(bl, heads, l, dh).transpose(0, 2, 1, 3).reshape(bl * l, heads * dh)


def _sb_fwd(h, g, w_qkv, w_o, bl):
    t, d = h.shape
    hn = _rmsnorm("mix_norm", h, g)
    qkv = _mm_cs("sb_qkv", hn, w_qkv, 0, "flat", BF16)
    q, k, v = (_to_heads(qkv[:, i * d:(i + 1) * d], bl) for i in range(3))
    o = _sb_attn_fwd(q, k, v)
    ob = _from_heads(o, bl).astype(BF16)
    out = _mm_rs("sb_out", ob, "flat", w_o, 0, res=h)
    return out, (h, hn, q, k, v, o, ob)


def _sb_bwd(dout, saved, g, w_qkv, w_o, bl):
    h, hn, q, k, v, o, ob = saved
    dob = dout.astype(BF16)
    dwo = _mm_dw("sb_dwo", ob, "flat", dob, None, (None, 0, 1))
    do = _mm_rs_dx("sb_out_dx", dob, w_o, 0, "flat", F32)
    dq, dk, dv = _sb_attn_bwd(q, k, v, o, _to_heads(do, bl))
    dqkv = jnp.concatenate([_from_heads(a, bl).astype(BF16) for a in (dq, dk, dv)], axis=1)
    dwqkv = _mm_dw("sb_dwqkv", hn, None, dqkv, "flat", (None, 0, 1))
    dhn = _mm_cs_dx("sb_qkv_dx", [(dqkv, w_qkv)], "flat", 0)
    dh, dg = _rmsnorm_bwd("mix_norm_bwd", dout, dhn, h, g)
    return dh, dg, dwqkv, dwo


def _adamw_update(wv, gr, mv, vv):
    c1 = 1.0 / (1.0 - ADAM_B1 ** ADAM_STEP)
    c2 = 1.0 / (1.0 - ADAM_B2 ** ADAM_STEP)
    mn = ADAM_B1 * mv + (1.0 - ADAM_B1) * gr
    vn = ADAM_B2 * vv + (1.0 - ADAM_B2) * gr * gr
    delta = -ADAM_LR * ((mn * c1) / (jnp.sqrt(vn * c2) + ADAM_EPS) + ADAM_WD * wv)
    return delta, mn, vn


def _adamw_small(w, gr, m, v):
    def fn(wv, gv, mv, vv):
        return list(_adamw_update(wv, gv, mv, vv)), []
    return _rows("adamw_small", fn, [w, gr, m, v], [(w.shape[1], F32)] * 3)[0]


def _place():
    x, y, c = lax.axis_index("x"), lax.axis_index("y"), lax.axis_index("c")
    chips = [(1 - x, y), (x, 1 - y), (1 - x, 1 - y)]
    return x, y, c, chips


def _remote(src, dst, send_sem, recv_sem, to):
    return pltpu.make_async_remote_copy(src_ref=src, dst_ref=dst, send_sem=send_sem, recv_sem=recv_sem,
                                        device_id=to, device_id_type=MESH)


def _half(ref, c, rh, lead):
    return ref.at[(slice(None),) * lead + (pl.ds(c * rh, rh),)]


def _allgather_weights(ws):
    n = len(ws)

    def body(*refs):
        ins, outs = refs[:n], refs[n:2 * n]
        send, recv = refs[2 * n:]
        x, y, c, chips = _place()
        own = 2 * x + y
        sibling = (x, y, 1 - c)
        sent = []
        for t in range(n):
            rh = ws[t].shape[1] // 2
            for j, chip in enumerate(chips):
                cp = _remote(_half(ins[t], c, rh, 1), _half(outs[t].at[own], c, rh, 1), send.at[t, j], recv.at[t, j], (*chip, c))
                cp.start()
                sent.append(cp)
        for t in range(n):
            rh = ws[t].shape[1] // 2
            for j, chip in enumerate(chips):
                landed = _half(outs[t].at[2 * chip[0] + chip[1]], c, rh, 1)
                _remote(landed, landed, send.at[t, j], recv.at[t, j], (*chip, c)).wait_recv()
                cp = _remote(landed, landed, send.at[t, 3 + j], recv.at[t, 3 + j], sibling)
                cp.start()
                sent.append(cp)
        for t in range(n):
            rh = ws[t].shape[1] // 2
            for j, chip in enumerate(chips):
                passed = _half(outs[t].at[2 * chip[0] + chip[1]], 1 - c, rh, 1)
                _remote(passed, passed, send.at[t, 3 + j], recv.at[t, 3 + j], sibling).wait_recv()
        for cp in sent:
            cp.wait_send()

    res = pl.pallas_call(
        body, name="allgather_weights", in_specs=[ANY] * n, out_specs=[ANY] * n,
        out_shape=[jax.ShapeDtypeStruct((N_CHIPS,) + w.shape, w.dtype) for w in ws],
        scratch_shapes=[pltpu.SemaphoreType.DMA((n, 6)), pltpu.SemaphoreType.DMA((n, 6))],
    )(*ws)
    own = 2 * lax.axis_index("x") + lax.axis_index("y")
    return [lax.dynamic_update_slice(g, w[None], (own, 0, 0, 0)) for g, w in zip(res, ws)]


def _pair_exchange(gs):
    n = len(gs)

    def body(*refs):
        ins, outs = refs[:n], refs[n:2 * n]
        send, recv = refs[2 * n:]
        x, y, c, _ = _place()
        copies = [_remote(_half(ins[t], 1 - c, gs[t].shape[2] // 2, 2), outs[t], send.at[t], recv.at[t], (x, y, 1 - c))
                  for t in range(n)]
        for cp in copies:
            cp.start()
        for cp in copies:
            cp.wait()

    return pl.pallas_call(
        body, name="grad_pair_exchange", in_specs=[ANY] * n, out_specs=[ANY] * n,
        out_shape=[jax.ShapeDtypeStruct(g.shape[:2] + (g.shape[2] // 2, g.shape[3]), F32) for g in gs],
        scratch_shapes=[pltpu.SemaphoreType.DMA((n,)), pltpu.SemaphoreType.DMA((n,))],
    )(*gs)


def _pair_sum(g, theirs, c_idx):
    n4, ly, r, cc = g.shape
    rh = r // 2
    tm = _tile(rh, 256)
    nt = rh // tm

    def body(c_ref, g_ref, t_ref, o_ref):
        o_ref[...] = g_ref[...] + t_ref[...]

    blk = (None, tm, cc)
    grid_spec = pltpu.PrefetchScalarGridSpec(
        num_scalar_prefetch=1, grid=(n4 * ly, nt),
        in_specs=[pl.BlockSpec(blk, lambda a, i, cr: (a, cr[0] * nt + i, 0)), pl.BlockSpec(blk, lambda a, i, cr: (a, i, 0))],
        out_specs=pl.BlockSpec(blk, lambda a, i, cr: (a, i, 0)))
    out = pl.pallas_call(
        body, name="grad_pair_sum", grid_spec=grid_spec, out_shape=jax.ShapeDtypeStruct((n4 * ly, rh, cc), F32),
        compiler_params=_params("parallel", "parallel"))(c_idx, g.reshape(n4 * ly, r, cc), theirs.reshape(n4 * ly, rh, cc))
    return out.reshape(n4, ly, rh, cc)


def _chip_exchange(ps):
    n = len(ps)

    def body(*refs):
        ins, outs = refs[:n], refs[n:2 * n]
        send, recv = refs[2 * n:]
        x, y, c, chips = _place()
        copies = []
        for t in range(n):
            for j, chip in enumerate(chips):
                copies.append(_remote(ins[t].at[2 * chip[0] + chip[1]], outs[t].at[j], send.at[t, j], recv.at[t, j], (*chip, c)))
        for cp in copies:
            cp.start()
        for cp in copies:
            cp.wait()

    return pl.pallas_call(
        body, name="grad_chip_exchange", in_specs=[ANY] * n, out_specs=[ANY] * n,
        out_shape=[jax.ShapeDtypeStruct((3,) + p.shape[1:], F32) for p in ps],
        scratch_shapes=[pltpu.SemaphoreType.DMA((n, 3)), pltpu.SemaphoreType.DMA((n, 3))],
    )(*ps)


def _chip_sum(p, landed, own_idx):
    _, ly, rh, cc = p.shape
    tm = _tile(rh, 256)

    def body(o_ref, p_ref, a_ref, b_ref, c_ref, out_ref):
        out_ref[...] = ((p_ref[...] + a_ref[...]) + b_ref[...]) + c_ref[...]

    blk = (None, None, tm, cc)
    slot = lambda j: pl.BlockSpec(blk, lambda l, i, o: (j, l, i, 0))
    grid_spec = pltpu.PrefetchScalarGridSpec(
        num_scalar_prefetch=1, grid=(ly, rh // tm),
        in_specs=[pl.BlockSpec(blk, lambda l, i, o: (o[0], l, i, 0)), slot(0), slot(1), slot(2)],
        out_specs=pl.BlockSpec((None, tm, cc), lambda l, i, o: (l, i, 0)))
    return pl.pallas_call(
        body, name="grad_chip_sum", grid_spec=grid_spec, out_shape=jax.ShapeDtypeStruct((ly, rh, cc), F32),
        compiler_params=_params("parallel", "parallel"))(own_idx, p, landed, landed, landed)


def _pair_swap(halves):
    n = len(halves)

    def body(*refs):
        ins, outs = refs[:n], refs[n:2 * n]
        send, recv = refs[2 * n:]
        x, y, c, _ = _place()
        copies = [_remote(ins[t], outs[t], send.at[t], recv.at[t], (x, y, 1 - c)) for t in range(n)]
        for cp in copies:
            cp.start()
        for cp in copies:
            cp.wait()

    return pl.pallas_call(
        body, name="grad_pair_swap", in_specs=[ANY] * n, out_specs=[ANY] * n,
        out_shape=[jax.ShapeDtypeStruct(h.shape, F32) for h in halves],
        scratch_shapes=[pltpu.SemaphoreType.DMA((n,)), pltpu.SemaphoreType.DMA((n,))],
    )(*halves)


def _adamw_big(w, m, v, mine, theirs, c_idx):
    ly, r, cc = w.shape
    rh = r // 2
    tm = _tile(rh, 256)
    nt = rh // tm

    def body(c_ref, w_ref, m_ref, v_ref, a_ref, b_ref, g_out, d_out, m_out, v_out):
        gr = jnp.where(pl.program_id(1) == c_ref[0], a_ref[...], b_ref[...])
        delta, mn, vn = _adamw_update(w_ref[...], gr, m_ref[...], v_ref[...])
        g_out[...] = gr
        d_out[...] = delta
        m_out[...] = mn
        v_out[...] = vn

    blk = (None, tm, cc)
    full = pl.BlockSpec(blk, lambda l, hc, i, cr: (l, hc * nt + i, 0))
    half = pl.BlockSpec(blk, lambda l, hc, i, cr: (l, i, 0))
    grid_spec = pltpu.PrefetchScalarGridSpec(
        num_scalar_prefetch=1, grid=(ly, 2, nt), in_specs=[full, full, full, half, half], out_specs=[full] * 4)
    sd = jax.ShapeDtypeStruct(w.shape, F32)
    return pl.pallas_call(
        body, name="adamw", grid_spec=grid_spec, out_shape=[sd] * 4,
        compiler_params=_params("parallel", "parallel", "parallel"))(c_idx, w, m, v, mine, theirs)


def _allreduce_small(v):
    rows, w = v.shape

    def body(x_ref, sum_ref, all_ref, send, recv, local):
        x, y, c, chips = _place()
        me, sibling = (x, y, c), (x, y, 1 - c)

        def slot(px, py, pc):
            return all_ref.at[4 * px + 2 * py + pc]

        def copy(k, block, to, src=None):
            return _remote(slot(*block) if src is None else src, slot(*block), send.at[k], recv.at[k], to)

        mine = pltpu.make_async_copy(x_ref, slot(*me), local)
        mine.start()
        first = [copy(0, me, sibling, src=x_ref)]
        first += [copy(1 + j, me, (*chip, c), src=x_ref) for j, chip in enumerate(chips)]
        for cp in first:
            cp.start()
        passed = [copy(4 + j, (*chip, c), sibling) for j, chip in enumerate(chips)]
        for j, chip in enumerate(chips):
            copy(1 + j, (*chip, c), me).wait_recv()
            passed[j].start()
        copy(0, sibling, me).wait_recv()
        for j, chip in enumerate(chips):
            copy(4 + j, (*chip, 1 - c), me).wait_recv()
        for cp in first + passed:
            cp.wait_send()
        mine.wait()
        tot = all_ref[0]
        for k in range(1, N_DEV):
            tot = tot + all_ref[k]
        sum_ref[...] = tot

    vm = pl.BlockSpec(memory_space=pltpu.VMEM)
    return pl.pallas_call(
        body, name="allreduce_small", in_specs=[vm], out_specs=[vm, vm],
        out_shape=[jax.ShapeDtypeStruct((rows, w), F32), jax.ShapeDtypeStruct((N_DEV, rows, w), F32)],
        scratch_shapes=[pltpu.SemaphoreType.DMA((7,)), pltpu.SemaphoreType.DMA((7,)), pltpu.SemaphoreType.DMA],
        compiler_params=pltpu.CompilerParams(vmem_limit_bytes=VMEM_LIMIT),
    )(v)[0]


BIG = ["ffn1_w1", "ffn1_w3", "ffn1_w2", "ffn2_w1", "ffn2_w3", "ffn2_w2", "ple_proj", "ple_gate",
       "s5_w_in", "s5_w_glu", "sb_w_qkv", "sb_w_o"]
SMALL = ["ffn1_norm", "mix_norm", "ffn2_norm", "ple_norm", "s5_a_re", "s5_a_im", "s5_log_dt", "s5_b_re", "s5_b_im",
         "s5_c_re", "s5_c_im", "s5_d", "final_norm"]
ORDER = ["ffn1_norm", "ffn1_w1", "ffn1_w3", "ffn1_w2", "mix_norm", "ffn2_norm", "ffn2_w1", "ffn2_w3", "ffn2_w2",
         "ple_norm", "ple_proj", "ple_gate", "s5_w_in", "s5_a_re", "s5_a_im", "s5_log_dt", "s5_b_re", "s5_b_im",
         "s5_c_re", "s5_c_im", "s5_d", "s5_w_glu", "sb_w_qkv", "sb_w_o", "final_norm"]


def _pack(arrays):
    flat = jnp.concatenate([a.reshape(-1) for a in arrays])
    pad = (-flat.shape[0]) % 1024
    return jnp.pad(flat, (0, pad)).reshape(-1, 128)


def _unpack(packed, like):
    flat = packed.reshape(-1)
    out, off = [], 0
    for a in like:
        out.append(flat[off:off + a.size].reshape(a.shape))
        off += a.size
    return out


def _fwd_bwd(x, p, target, w, gathered):
    bl, l, d = x.shape
    t = bl * l
    depth = w["ffn1_norm"].shape[0]
    s5_ops, s5_vjp = jax.vjp(_s5_prep, w["s5_a_re"][0], w["s5_a_im"][0], w["s5_log_dt"][0], w["s5_b_re"][0],
                             w["s5_b_im"][0], w["s5_c_re"][0], w["s5_c_im"][0])

    h = x.reshape(t, d)
    p2 = [p[i].reshape(t, p.shape[-1]).astype(BF16) for i in range(depth)]
    saved = []
    for i in range(depth):
        norm = lambda name: w[name][i:i + 1]
        h, s1 = _ffn_fwd(h, norm("ffn1_norm"), gathered["ffn1_w1"], gathered["ffn1_w3"], gathered["ffn1_w2"], i)
        if i % 2 == 0:
            h, s2 = _s5_fwd(h, norm("mix_norm"), s5_ops, w["s5_d"][i // 2:i // 2 + 1], gathered["s5_w_in"], gathered["s5_w_glu"], bl)
        else:
            h, s2 = _sb_fwd(h, norm("mix_norm"), gathered["sb_w_qkv"], gathered["sb_w_o"], bl)
        h, s3 = _ffn_fwd(h, norm("ffn2_norm"), gathered["ffn2_w1"], gathered["ffn2_w3"], gathered["ffn2_w2"], i)
        h, s4 = _ple_fwd(h, norm("ple_norm"), p2[i], gathered["ple_proj"], gathered["ple_gate"], i)
        saved.append((s1, s2, s3, s4))

    loss, dh, dfinal = _head(h, w["final_norm"].reshape(1, d), target.reshape(t, d))

    big = {k: None for k in BIG}
    small = {k: [None] * w[k].shape[0] if w[k].ndim > 1 else None for k in SMALL}
    small["final_norm"] = dfinal.reshape(d)
    for i in reversed(range(depth)):
        norm = lambda name: w[name][i:i + 1]
        slots = lambda *names: [(big[k], i, depth) for k in names]
        s1, s2, s3, s4 = saved[i]
        dh, dg, big["ple_proj"], big["ple_gate"] = _ple_bwd(
            dh, s4, norm("ple_norm"), p2[i], gathered["ple_proj"], gathered["ple_gate"], i, slots("ple_proj", "ple_gate"))
        small["ple_norm"][i] = dg[0]
        dh, dg, big["ffn2_w1"], big["ffn2_w3"], big["ffn2_w2"] = _ffn_bwd(
            dh, s3, norm("ffn2_norm"), gathered["ffn2_w1"], gathered["ffn2_w3"], gathered["ffn2_w2"], i,
            slots("ffn2_w1", "ffn2_w3", "ffn2_w2"))
        small["ffn2_norm"][i] = dg[0]
        if i % 2 == 0:
            dh, dg, big["s5_w_in"], big["s5_w_glu"], dd, dops = _s5_bwd(
                dh, s2, norm("mix_norm"), s5_ops, w["s5_d"][i // 2:i // 2 + 1], gathered["s5_w_in"], gathered["s5_w_glu"], bl)
            small["s5_d"][0] = dd[0]
            raw = s5_vjp(dops)
            for name, gr in zip(["s5_a_re", "s5_a_im", "s5_log_dt", "s5_b_re", "s5_b_im", "s5_c_re", "s5_c_im"], raw):
                small[name][0] = gr
        else:
            dh, dg, big["sb_w_qkv"], big["sb_w_o"] = _sb_bwd(dh, s2, norm("mix_norm"), gathered["sb_w_qkv"], gathered["sb_w_o"], bl)
        small["mix_norm"][i] = dg[0]
        dh, dg, big["ffn1_w1"], big["ffn1_w3"], big["ffn1_w2"] = _ffn_bwd(
            dh, s1, norm("ffn1_norm"), gathered["ffn1_w1"], gathered["ffn1_w3"], gathered["ffn1_w2"], i,
            slots("ffn1_w1", "ffn1_w3", "ffn1_w2"))
        small["ffn1_norm"][i] = dg[0]
    small_list = [jnp.stack(small[k]) if isinstance(small[k], list) else small[k] for k in SMALL]
    return loss, dh.reshape(bl, l, d), big, small_list


def _step(x, p, target, w, m, v):
    gathered = dict(zip(BIG, _allgather_weights([w[k].astype(BF16) for k in BIG])))
    loss, grad_x, big, small_list = _fwd_bwd(x, p, target, w, gathered)

    c_idx = lax.axis_index("c").astype(jnp.int32).reshape(1)
    own_idx = (2 * lax.axis_index("x") + lax.axis_index("y")).astype(jnp.int32).reshape(1)
    partial = [big[k] for k in BIG]
    pair = [_pair_sum(g, t, c_idx) for g, t in zip(partial, _pair_exchange(partial))]
    mine = [_chip_sum(pr, ld, own_idx) for pr, ld in zip(pair, _chip_exchange(pair))]
    theirs = _pair_swap(mine)
    out_g, out_d, out_m, out_v = {}, {}, {}, {}
    for k, a, b in zip(BIG, mine, theirs):
        out_g[k], out_d[k], out_m[k], out_v[k] = _adamw_big(w[k], m[k], v[k], a, b, c_idx)

    like = [w[k] for k in SMALL]
    g_small = _allreduce_small(_pack(small_list))
    packed = (g_small,) + tuple(_adamw_small(_pack(like), g_small, _pack([m[k] for k in SMALL]), _pack([v[k] for k in SMALL])))
    for dst, pk in zip((out_g, out_d, out_m, out_v), packed):
        dst.update(dict(zip(SMALL, _unpack(pk, like))))

    loss = lax.psum(loss, ("x", "y", "c"))
    return (loss, grad_x, *[out_g[k] for k in ORDER], *[out_d[k] for k in ORDER],
            *[out_m[k] for k in ORDER], *[out_v[k] for k in ORDER])


def kernel(x, p, ffn1_norm, ffn1_w1, ffn1_w3, ffn1_w2, mix_norm, ffn2_norm, ffn2_w1, ffn2_w3, ffn2_w2, ple_norm, ple_proj, ple_gate, s5_w_in, s5_a_re, s5_a_im, s5_log_dt, s5_b_re, s5_b_im, s5_c_re, s5_c_im, s5_d, s5_w_glu, sb_w_qkv, sb_w_o, final_norm, loss_target, m_ffn1_norm, m_ffn1_w1, m_ffn1_w3, m_ffn1_w2, m_mix_norm, m_ffn2_norm, m_ffn2_w1, m_ffn2_w3, m_ffn2_w2, m_ple_norm, m_ple_proj, m_ple_gate, m_s5_w_in, m_s5_a_re, m_s5_a_im, m_s5_log_dt, m_s5_b_re, m_s5_b_im, m_s5_c_re, m_s5_c_im, m_s5_d, m_s5_w_glu, m_sb_w_qkv, m_sb_w_o, m_final_norm, v_ffn1_norm, v_ffn1_w1, v_ffn1_w3, v_ffn1_w2, v_mix_norm, v_ffn2_norm, v_ffn2_w1, v_ffn2_w3, v_ffn2_w2, v_ple_norm, v_ple_proj, v_ple_gate, v_s5_w_in, v_s5_a_re, v_s5_a_im, v_s5_log_dt, v_s5_b_re, v_s5_b_im, v_s5_c_re, v_s5_c_im, v_s5_d, v_s5_w_glu, v_sb_w_qkv, v_sb_w_o, v_final_norm):
    args = dict(locals())
    w = {k: args[k] for k in ORDER}
    m = {k: args["m_" + k] for k in ORDER}
    v = {k: args["v_" + k] for k in ORDER}
    return _step(x, p, loss_target, w, m, v)
```

```python
import functools
import math

import jax
import jax.numpy as jnp
from jax import lax
from jax.experimental import pallas as pl
from jax.experimental.pallas import tpu as pltpu

F32 = jnp.float32
BF16 = jnp.bfloat16
MESH = pl.DeviceIdType.MESH

N_CHIPS = 4
N_DEV = 8
RMS_EPS = 1e-6
S5_GROUP = 16
S5_STATE = 64
S5_CHUNK = 16
SB_HEAD_DIM = 64
SB_BLOCK = 128
SB_CUT = -104.0
SB_UNROLL = 3
ADAM_LR, ADAM_B1, ADAM_B2, ADAM_EPS, ADAM_WD, ADAM_STEP = 0.001, 0.9, 0.999, 1e-08, 0.01, 10
VMEM_LIMIT = 48 * 1024 * 1024

NN = (((1,), (0,)), ((), ()))
NT = (((1,), (1,)), ((), ()))
TN = (((0,), (0,)), ((), ()))

ANY = pl.BlockSpec(memory_space=pl.ANY)


def _tile(n, target):
    if n <= target:
        return n
    for t in range(target - target % 8, 7, -8):
        if n % t == 0:
            return t
    raise ValueError(f"no row tile for {n}")


def _params(*semantics):
    return pltpu.CompilerParams(dimension_semantics=semantics, vmem_limit_bytes=VMEM_LIMIT)


def _sigmoid(v):
    return 1.0 / (1.0 + jnp.exp(-v))


def _gemm(name, grid, operands, in_specs, groups, acc_shapes, out_shapes, out_specs, epilogue, reduce_axis=None, aliases=None):
    n_in, n_out = len(operands), len(out_shapes)
    n_red = None if reduce_axis is None else grid[reduce_axis]

    def body(*refs):
        ins, outs, accs = refs[:n_in], refs[n_in:n_in + n_out], refs[n_in + n_out:]

        def products():
            res = []
            for terms in groups:
                tot = None
                for ia, ib, dims in terms:
                    d = lax.dot_general(ins[ia][...], ins[ib][...], dims, preferred_element_type=F32)
                    tot = d if tot is None else tot + d
                res.append(tot)
            return res

        def finish(vals):
            for o, v in zip(outs, epilogue(vals, ins)):
                o[...] = v.astype(o.dtype)

        if reduce_axis is None:
            finish(products())
        else:
            k = pl.program_id(reduce_axis)

            @pl.when(k == 0)
            def _():
                for a in accs:
                    a[...] = jnp.zeros_like(a)

            for a, d in zip(accs, products()):
                a[...] += d

            @pl.when(k == n_red - 1)
            def _():
                finish([a[...] for a in accs])

    scratch = [] if reduce_axis is None else [pltpu.VMEM(s, F32) for s in acc_shapes]
    sem = tuple("arbitrary" if i == reduce_axis else "parallel" for i in range(len(grid)))
    return pl.pallas_call(
        body, name=name, grid=grid, in_specs=in_specs, out_specs=out_specs, out_shape=out_shapes,
        scratch_shapes=scratch, input_output_aliases=aliases or {}, compiler_params=_params(*sem))(*operands)


def _ident(vals, ins):
    return vals


def _act_spec(layout, tm, cs, pos):
    if layout == "sm":
        return pl.BlockSpec((None, tm, cs), lambda *g: (pos(*g)[1], pos(*g)[0], 0))
    return pl.BlockSpec((tm, cs), lambda *g: pos(*g))


def _act_shape(layout, t, cs, dtype):
    return jax.ShapeDtypeStruct((N_CHIPS, t, cs) if layout == "sm" else (t, N_CHIPS * cs), dtype)


def _w_spec(w, layer, pos_k):
    _, _, r, c = w.shape
    return pl.BlockSpec((None, None, r, c), lambda *g: (pos_k(*g), layer, 0, 0))


def _mm_cs(name, x, w, layer, out_layout, out_dtype, tm=512):
    t, kd = x.shape
    cs = w.shape[3]
    tm = _tile(t, tm)
    return _gemm(
        name, (N_CHIPS, t // tm), [x, w],
        [pl.BlockSpec((tm, kd), lambda k, i: (i, 0)), _w_spec(w, layer, lambda k, i: k)],
        [[(0, 1, NN)]], None, [_act_shape(out_layout, t, cs, out_dtype)],
        [_act_spec(out_layout, tm, cs, lambda k, i: (i, k))], _ident)[0]


def _mm_rs(name, xs, layout, w, layer, res=None, alpha=1.0, out_dtype=F32, tm=512):
    ks, n = w.shape[2], w.shape[3]
    t = xs.shape[1] if layout == "sm" else xs.shape[0]
    tm = _tile(t, tm)
    operands = [xs, w] + ([] if res is None else [res])
    specs = [_act_spec(layout, tm, ks, lambda i, k: (i, k)), _w_spec(w, layer, lambda i, k: k)]
    if res is not None:
        specs.append(pl.BlockSpec((tm, n), lambda i, k: (i, 0)))

    def epilogue(vals, ins):
        y = alpha * vals[0]
        return [y if res is None else ins[2][...] + y]

    return _gemm(
        name, (t // tm, N_CHIPS), operands, specs, [[(0, 1, NN)]], [(tm, n)],
        [jax.ShapeDtypeStruct((t, n), out_dtype)], [pl.BlockSpec((tm, n), lambda i, k: (i, 0))],
        epilogue, reduce_axis=1)[0]


def _mm_cs_dx(name, pairs, layout, layer, tm=512):
    w0 = pairs[0][1]
    kd, cs = w0.shape[2], w0.shape[3]
    dy0 = pairs[0][0]
    t = dy0.shape[1] if layout == "sm" else dy0.shape[0]
    tm = _tile(t, tm)
    operands, specs, terms = [], [], []
    for dy, w in pairs:
        terms.append((len(operands), len(operands) + 1, NT))
        operands += [dy, w]
        specs += [_act_spec(layout, tm, cs, lambda i, k: (i, k)), _w_spec(w, layer, lambda i, k: k)]
    return _gemm(
        name, (t // tm, N_CHIPS), operands, specs, [terms], [(tm, kd)],
        [jax.ShapeDtypeStruct((t, kd), F32)], [pl.BlockSpec((tm, kd), lambda i, k: (i, 0))],
        _ident, reduce_axis=1)[0]


def _mm_rs_dx(name, dy, w, layer, out_layout, out_dtype, tm=512):
    t, n = dy.shape
    ks = w.shape[2]
    tm = _tile(t, tm)
    return _gemm(
        name, (N_CHIPS, t // tm), [dy, w],
        [pl.BlockSpec((tm, n), lambda k, i: (i, 0)), _w_spec(w, layer, lambda k, i: k)],
        [[(0, 1, NT)]], None, [_act_shape(out_layout, t, ks, out_dtype)],
        [_act_spec(out_layout, tm, ks, lambda k, i: (i, k))], _ident)[0]


def _mm_dw(name, x, x_layout, dy, dy_layout, slot, alpha=1.0, tk=512):
    stack, layer, layers = slot
    if x_layout is None:
        t, rows = x.shape
        cols = dy.shape[2] if dy_layout == "sm" else dy.shape[1] // N_CHIPS
        tk = _tile(t, tk)
        xspec = pl.BlockSpec((tk, rows), lambda k, j: (j, 0))
        yspec = _act_spec(dy_layout, tk, cols, lambda k, j: (j, k))
    else:
        t, cols = dy.shape
        rows = x.shape[2] if x_layout == "sm" else x.shape[1] // N_CHIPS
        tk = _tile(t, tk)
        xspec = _act_spec(x_layout, tk, rows, lambda k, j: (j, k))
        yspec = pl.BlockSpec((tk, cols), lambda k, j: (j, 0))
    operands, specs = [x, dy], [xspec, yspec]
    if stack is not None:
        operands.append(stack)
        specs.append(ANY)
    return _gemm(
        name, (N_CHIPS, t // tk), operands, specs, [[(0, 1, TN)]], [(rows, cols)],
        [jax.ShapeDtypeStruct((N_CHIPS, layers, rows, cols), F32)],
        [pl.BlockSpec((None, None, rows, cols), lambda k, j: (k, layer, 0, 0))],
        lambda vals, ins: [alpha * vals[0]], reduce_axis=1, aliases=None if stack is None else {2: 0})[0]


def _rows(name, fn, ins, outs, accs=(), tm=256):
    t = ins[0].shape[0]
    tm = _tile(t, tm)
    n_in, n_out, n_acc = len(ins), len(outs), len(accs)
    in_specs = []
    for a in ins:
        if a.shape[0] == t:
            in_specs.append(pl.BlockSpec((tm, a.shape[1]), lambda i: (i, 0)))
        else:
            in_specs.append(pl.BlockSpec(a.shape, lambda i: (0, 0)))
    out_shape = [jax.ShapeDtypeStruct((t, c), d) for c, d in outs] + [jax.ShapeDtypeStruct(s, F32) for s in accs]
    out_specs = [pl.BlockSpec((tm, c), lambda i: (i, 0)) for c, _ in outs] + [pl.BlockSpec(s, lambda i: (0, 0)) for s in accs]

    def body(*refs):
        i = pl.program_id(0)
        row_vals, acc_vals = fn(*[r[...] for r in refs[:n_in]])
        for o, v in zip(refs[n_in:n_in + n_out], row_vals):
            o[...] = v.astype(o.dtype)
        acc_refs = refs[n_in + n_out:]
        if n_acc:
            @pl.when(i == 0)
            def _():
                for a in acc_refs:
                    a[...] = jnp.zeros_like(a)

            for a, v in zip(acc_refs, acc_vals):
                a[...] += v

    res = pl.pallas_call(
        body, name=name, grid=(t // tm,), in_specs=in_specs, out_specs=out_specs, out_shape=out_shape,
        compiler_params=_params("arbitrary" if n_acc else "parallel"))(*ins)
    return res[:n_out], res[n_out:]


def _rms_stats(x):
    return lax.rsqrt(jnp.mean(x * x, axis=-1, keepdims=True) + RMS_EPS)


def _rmsnorm(name, h, g):
    def fn(x, gg):
        return [x * _rms_stats(x) * gg], []
    return _rows(name, fn, [h, g], [(h.shape[1], BF16)])[0][0]


def _rms_bwd_math(dn, x, g):
    r = _rms_stats(x)
    xhat = x * r
    dxh = dn * g
    dx = r * (dxh - xhat * jnp.mean(dxh * xhat, axis=-1, keepdims=True))
    return dx, jnp.sum(dn * xhat, axis=0, keepdims=True)


def _rmsnorm_bwd(name, dres, dn, h, g):
    def fn(dr, d, x, gg):
        dx, dg = _rms_bwd_math(d, x, gg)
        return [dr + dx], [dg]
    (dh,), (dg,) = _rows(name, fn, [dres, dn, h, g], [(h.shape[1], F32)], [(1, h.shape[1])])
    return dh, dg


def _ffn_fwd(h, g, w1, w3, w2, layer, tm=512):
    t, d = h.shape
    fs = w1.shape[3]
    n = _rmsnorm("ffn_norm", h, g)
    tm = _tile(t, tm)

    def up(vals, ins):
        a, b = vals
        return [a, b, a * _sigmoid(a) * b]

    sm = _act_shape("sm", t, fs, BF16)
    osp = _act_spec("sm", tm, fs, lambda k, i: (i, k))
    a, b, s = _gemm(
        "ffn_up", (N_CHIPS, t // tm), [n, w1, w3],
        [pl.BlockSpec((tm, d), lambda k, i: (i, 0)), _w_spec(w1, layer, lambda k, i: k), _w_spec(w3, layer, lambda k, i: k)],
        [[(0, 1, NN)], [(0, 2, NN)]], None, [sm, sm, sm], [osp, osp, osp], up)
    out = _mm_rs("ffn_down", s, "sm", w2, layer, res=h, alpha=0.5)
    return out, (h, n, a, b, s)


def _ffn_bwd(dout, saved, g, w1, w3, w2, layer, slots, tm=512):
    h, n, a, b, s = saved
    t, d = h.shape
    fs = w1.shape[3]
    tm = _tile(t, tm)
    dob = dout.astype(BF16)

    def down(vals, ins):
        ds = 0.5 * vals[0]
        av, bv = ins[2][...].astype(F32), ins[3][...].astype(F32)
        sg = _sigmoid(av)
        return [ds * bv * sg * (1.0 + av * (1.0 - sg)), ds * av * sg]

    sm = _act_shape("sm", t, fs, BF16)
    asp = _act_spec("sm", tm, fs, lambda k, i: (i, k))
    da, db = _gemm(
        "ffn_down_dx", (N_CHIPS, t // tm), [dob, w2, a, b],
        [pl.BlockSpec((tm, d), lambda k, i: (i, 0)), _w_spec(w2, layer, lambda k, i: k), asp, asp],
        [[(0, 1, NT)]], None, [sm, sm], [asp, asp], down)
    dw2 = _mm_dw("ffn_dw2", s, "sm", dob, None, slots[2], alpha=0.5)
    dw1 = _mm_dw("ffn_dw1", n, None, da, "sm", slots[0])
    dw3 = _mm_dw("ffn_dw3", n, None, db, "sm", slots[1])
    dn = _mm_cs_dx("ffn_up_dx", [(da, w1), (db, w3)], "sm", layer)
    dh, dg = _rmsnorm_bwd("ffn_norm_bwd", dout, dn, h, g)
    return dh, dg, dw1, dw3, dw2


def _ple_fwd(h, g, p2, wproj, wgate, layer):
    n = _rmsnorm("ple_norm", h, g)
    gl = _mm_rs("ple_gate", n, "flat", wgate, layer)
    pp = _mm_cs("ple_proj", p2, wproj, layer, "flat", F32)

    def fn(hh, gg, q):
        return [hh + q * _sigmoid(gg)], []
    out = _rows("ple_mix", fn, [h, gl, pp], [(h.shape[1], F32)])[0][0]
    return out, (h, n, gl, pp)


def _ple_bwd(dout, saved, g, p2, wproj, wgate, layer, slots):
    h, n, gl, pp = saved
    d = h.shape[1]

    def fn(do, gg, q):
        sg = _sigmoid(gg)
        return [do * sg, do * q * sg * (1.0 - sg)], []
    (dpp, dgl), _ = _rows("ple_mix_bwd", fn, [dout, gl, pp], [(d, BF16), (d, BF16)])
    dwproj = _mm_dw("ple_dwproj", p2, None, dpp, "flat", slots[0])
    dwgate = _mm_dw("ple_dwgate", n, "flat", dgl, None, slots[1])
    dn = _mm_rs_dx("ple_gate_dx", dgl, wgate, layer, "flat", F32)
    dh, dg = _rmsnorm_bwd("ple_norm_bwd", dout, dn, h, g)
    return dh, dg, dwproj, dwgate


def _head(h, g, target):
    d = h.shape[1]

    def fn(x, gg, tg):
        y = x * _rms_stats(x) * gg
        err = y - tg
        dy = err * (1.0 / d)
        dx, dg = _rms_bwd_math(dy, x, gg)
        loss = 0.5 * jnp.sum(jnp.sum(err * err, axis=-1, keepdims=True) * (1.0 / d), axis=0, keepdims=True)
        return [dx], [dg, jnp.broadcast_to(loss, (1, 128))]
    (dh,), (dg, loss) = _rows("loss_head", fn, [h, g, target], [(d, F32)], [(1, d), (1, 128)])
    return loss[0, 0], dh, dg


def _s5_prep(a_re, a_im, log_dt, b_re, b_im, c_re, c_im):
    c = S5_CHUNK
    lam_re = jnp.minimum(a_re, -1e-4)
    lam_im = a_im
    dt = jnp.exp(log_dt)[:, None]
    ks = jnp.arange(c + 1, dtype=F32)[:, None, None]
    mag = jnp.exp(lam_re[None] * dt[None] * ks)
    ph = lam_im[None] * dt[None] * ks
    pw_re, pw_im = mag * jnp.cos(ph), mag * jnp.sin(ph)
    den = lam_re * lam_re + lam_im * lam_im
    nr, ni = pw_re[1] - 1.0, pw_im[1]
    fr = (nr * lam_re + ni * lam_im) / den
    fi = (ni * lam_re - nr * lam_im) / den
    bb_re = fr[..., None] * b_re - fi[..., None] * b_im
    bb_im = fr[..., None] * b_im + fi[..., None] * b_re
    ca_re = c_re[None] * pw_re[:, :, None, :] - c_im[None] * pw_im[:, :, None, :]
    ca_im = c_re[None] * pw_im[:, :, None, :] + c_im[None] * pw_re[:, :, None, :]
    hp = lax.Precision.HIGHEST
    kern = (jnp.einsum("kghp,gpj->kghj", ca_re[:c], bb_re, precision=hp)
            - jnp.einsum("kghp,gpj->kghj", ca_im[:c], bb_im, precision=hp))
    lag = jnp.arange(c)[None, :] - jnp.arange(c)[:, None]
    toep = jnp.where((lag >= 0)[:, :, None, None, None], kern[jnp.clip(lag, 0, c - 1)], 0.0)
    g = a_re.shape[0]
    wi = toep.transpose(2, 0, 4, 1, 3).reshape(g, c * S5_GROUP, c * S5_GROUP)
    rev_re, rev_im = pw_re[c - 1::-1][:c], pw_im[c - 1::-1][:c]
    wn_re = rev_re[..., None] * bb_re[None] - rev_im[..., None] * bb_im[None]
    wn_im = rev_re[..., None] * bb_im[None] + rev_im[..., None] * bb_re[None]
    wn = jnp.stack([wn_re, wn_im], axis=0).transpose(2, 1, 4, 0, 3).reshape(g, c * S5_GROUP, 2 * S5_STATE)
    wo = jnp.stack([ca_re[1:], -ca_im[1:]], axis=0).transpose(2, 0, 4, 1, 3).reshape(g, 2 * S5_STATE, c * S5_GROUP)
    ar, ai = pw_re[c], pw_im[c]
    m1 = jnp.concatenate([ar, ar], axis=1)
    m2 = jnp.concatenate([-ai, ai], axis=1)
    return jnp.concatenate([wi, wn], axis=2), wo, m1, m2


def _bmm(name, a, b, dims, out_dtype, gb=8):
    g = a.shape[0]
    gb = min(gb, g)
    m = a.shape[2] if dims == TN else a.shape[1]
    n = b.shape[1] if dims == NT else b.shape[2]

    def body(a_ref, b_ref, o_ref):
        for j in range(gb):
            o_ref[j] = lax.dot_general(a_ref[j], b_ref[j], dims, preferred_element_type=F32).astype(o_ref.dtype)

    return pl.pallas_call(
        body, name=name, grid=(g // gb,),
        in_specs=[pl.BlockSpec((gb,) + a.shape[1:], lambda i: (i, 0, 0)), pl.BlockSpec((gb,) + b.shape[1:], lambda i: (i, 0, 0))],
        out_specs=pl.BlockSpec((gb, m, n), lambda i: (i, 0, 0)),
        out_shape=jax.ShapeDtypeStruct((g, m, n), out_dtype), compiler_params=_params("parallel"))(a, b)


def _s5_scan_fwd(sloc, m1, m2):
    nc, r, w = sloc.shape

    def body(s_ref, m1_ref, m2_ref, o_ref):
        a1, a2 = m1_ref[...], m2_ref[...]

        def step(c, s):
            o_ref[c] = s
            return a1 * s + a2 * pltpu.roll(s, S5_STATE, 1) + s_ref[c]
        lax.fori_loop(0, nc, step, jnp.zeros((r, w), F32))

    vm = pl.BlockSpec(memory_space=pltpu.VMEM)
    return pl.pallas_call(
        body, name="s5_scan", in_specs=[vm, vm, vm], out_specs=vm,
        out_shape=jax.ShapeDtypeStruct(sloc.shape, F32),
        compiler_params=pltpu.CompilerParams(vmem_limit_bytes=VMEM_LIMIT))(sloc, m1, m2)


def _s5_scan_bwd(dsprev, sprev, m1, m2):
    nc, r, w = dsprev.shape

    def body(d_ref, s_ref, m1_ref, m2_ref, g_ref, p1_ref, p2_ref):
        a1, a2 = m1_ref[...], m2_ref[...]
        zero = jnp.zeros((r, w), F32)

        def step(i, carry):
            gp, p1, p2 = carry
            c = nc - 2 - i
            g_ref[c] = gp
            sp = s_ref[c]
            p1 = p1 + gp * sp
            p2 = p2 + gp * pltpu.roll(sp, S5_STATE, 1)
            return d_ref[c] + a1 * gp - a2 * pltpu.roll(gp, S5_STATE, 1), p1, p2

        g_ref[nc - 1] = zero
        _, p1, p2 = lax.fori_loop(0, nc - 1, step, (d_ref[nc - 1], zero, zero))
        p1_ref[...] = p1
        p2_ref[...] = p2

    vm = pl.BlockSpec(memory_space=pltpu.VMEM)
    sd = jax.ShapeDtypeStruct
    return pl.pallas_call(
        body, name="s5_scan_bwd", in_specs=[vm, vm, vm, vm], out_specs=[vm, vm, vm],
        out_shape=[sd(dsprev.shape, F32), sd((r, w), F32), sd((r, w), F32)],
        compiler_params=pltpu.CompilerParams(vmem_limit_bytes=VMEM_LIMIT))(dsprev, sprev, m1, m2)


def _to_groups(u, bl):
    t, d = u.shape
    g = d // S5_GROUP
    return u.reshape(t // S5_CHUNK, S5_CHUNK, g, S5_GROUP).transpose(2, 0, 1, 3).reshape(g, t // S5_CHUNK, S5_CHUNK * S5_GROUP)


def _from_groups(y):
    g, nct, _ = y.shape
    return y.reshape(g, nct, S5_CHUNK, S5_GROUP).transpose(1, 2, 0, 3).reshape(nct * S5_CHUNK, g * S5_GROUP)


def _to_scan(s, bl):
    g, nct, w = s.shape
    return s.reshape(g, bl, nct // bl, w).transpose(2, 1, 0, 3).reshape(nct // bl, bl * g, w)


def _from_scan(s, bl):
    nc, r, w = s.shape
    g = r // bl
    return s.reshape(nc, bl, g, w).transpose(2, 1, 0, 3).reshape(g, bl * nc, w)


def _gelu_tanh_parts(y):
    c0 = math.sqrt(2.0 / math.pi)
    inner = c0 * (y + 0.044715 * y * y * y)
    th = jnp.tanh(inner)
    return th, c0 * (1.0 + 3 * 0.044715 * y * y)


def _s5_fwd(h, g, ops, d_skip, w_in, w_glu, bl):
    wcat, wo, m1, m2 = ops
    t, d = h.shape
    ch = S5_CHUNK * S5_GROUP
    hn = _rmsnorm("mix_norm", h, g)
    u = _mm_rs("s5_in", hn, "flat", w_in, 0)
    ug = _to_groups(u.astype(BF16), bl)
    x = _bmm("s5_chunk_in", ug, wcat.astype(BF16), NN, F32)
    sprev = _s5_scan_fwd(_to_scan(x[:, :, ch:], bl), jnp.tile(m1, (bl, 1)), jnp.tile(m2, (bl, 1)))
    sprev_g = _from_scan(sprev, bl).astype(BF16)
    y_state = _bmm("s5_chunk_out", sprev_g, wo.astype(BF16), NN, F32)
    y = _from_groups(x[:, :, :ch] + y_state)

    def fn(yy, uu, dd):
        y2 = yy + dd * uu
        th, _ = _gelu_tanh_parts(y2)
        return [0.5 * y2 * (1.0 + th)], []
    z = _rows("s5_gelu", fn, [y, u, d_skip], [(d, BF16)])[0][0]
    zz = _mm_cs("s5_glu", z, w_glu, 0, "flat", F32)

    def glu(hh, zv):
        return [hh + zv[:, :d] * _sigmoid(zv[:, d:])], []
    out = _rows("s5_glu_mix", glu, [h, zz], [(d, F32)])[0][0]
    return out, (h, hn, u, ug, sprev, sprev_g, y, z, zz)


def _s5_bwd(dout, saved, g, ops, d_skip, w_in, w_glu, bl):
    h, hn, u, ug, sprev, sprev_g, y, z, zz = saved
    wcat, wo, m1, m2 = ops
    t, d = h.shape
    ch = S5_CHUNK * S5_GROUP

    def glu_bwd(do, zv):
        sg = _sigmoid(zv[:, d:])
        return [jnp.concatenate([do * sg, do * zv[:, :d] * sg * (1.0 - sg)], axis=1)], []
    dzz = _rows("s5_glu_bwd", glu_bwd, [dout, zz], [(2 * d, BF16)])[0][0]
    dwglu = _mm_dw("s5_dwglu", z, None, dzz, "flat", (None, 0, 1))
    dz = _mm_cs_dx("s5_glu_dx", [(dzz, w_glu)], "flat", 0)

    def gelu_bwd(dzv, yy, uu, dd):
        y2 = yy + dd * uu
        th, dinner = _gelu_tanh_parts(y2)
        dy2 = dzv * (0.5 * (1.0 + th) + 0.5 * y2 * (1.0 - th * th) * dinner)
        return [dy2, dy2 * dd], [jnp.sum(dy2 * uu, axis=0, keepdims=True)]
    (dy_b, du_skip), (dd,) = _rows("s5_gelu_bwd", gelu_bwd, [dz, y, u, d_skip], [(d, BF16), (d, F32)], [(1, d)])
    dyg = _to_groups(dy_b, bl)
    wo_b, wcat_b = wo.astype(BF16), wcat.astype(BF16)
    dsprev = _bmm("s5_chunk_out_dx", dyg, wo_b, NT, F32)
    m1t, m2t = jnp.tile(m1, (bl, 1)), jnp.tile(m2, (bl, 1))
    dsloc, p1, p2 = _s5_scan_bwd(_to_scan(dsprev, bl), sprev, m1t, m2t)
    dcat = jnp.concatenate([dyg, _from_scan(dsloc, bl).astype(BF16)], axis=2)
    dug = _bmm("s5_chunk_in_dx", dcat, wcat_b, NT, F32)
    dwcat = _bmm("s5_chunk_in_dw", ug, dcat, TN, F32)
    dwo = _bmm("s5_chunk_out_dw", sprev_g, dyg, TN, F32)
    gcount = d // S5_GROUP
    dm1 = p1.reshape(bl, gcount, 2 * S5_STATE).sum(axis=0)
    dm2 = p2.reshape(bl, gcount, 2 * S5_STATE).sum(axis=0)
    du = (_from_groups(dug) + du_skip).astype(BF16)
    dwin = _mm_dw("s5_dwin", hn, "flat", du, None, (None, 0, 1))
    dhn = _mm_rs_dx("s5_in_dx", du, w_in, 0, "flat", F32)
    dh, dg = _rmsnorm_bwd("mix_norm_bwd", dout, dhn, h, g)
    return dh, dg, dwin, dwglu, dd, (dwcat, dwo, dm1, dm2)


def _sb_block(qi, idx, tb):
    kb = qi - idx
    return pl.multiple_of(jnp.maximum(kb, 0) * tb, tb), idx == 0, kb >= 0


def _sb_scores(q, kblk, diag, exists, row, col):
    z = lax.dot_general(q, kblk, NT, preferred_element_type=F32) * (SB_HEAD_DIM ** -0.5)
    l1 = jnp.log(1.0 + jnp.exp(-jnp.abs(z)))
    ls = jnp.minimum(z, 0.0) - l1
    mask = jnp.logical_and(jnp.logical_or(col < row, jnp.logical_not(diag)), exists)
    lk = jnp.where(mask, ls - z, 0.0)
    return ls, lk, mask


def _sb_more(qi, carry):
    j, cr = carry[0], carry[2]
    return jnp.logical_and(j <= qi, jnp.max(cr) > SB_CUT)


def _split_dot(v, tri):
    hi = v.astype(BF16)
    lo = (v - hi.astype(F32)).astype(BF16)
    return (jnp.dot(hi, tri, preferred_element_type=F32) + jnp.dot(lo, tri, preferred_element_type=F32))


def _sb_attn_fwd(q, k, v):
    bh, l, dh = q.shape
    tb = min(SB_BLOCK, l)
    nq = l // tb

    def body(q_ref, k_ref, v_ref, o_ref):
        qi = pl.program_id(1)
        qv = q_ref[...]
        row = lax.broadcasted_iota(jnp.int32, (tb, tb), 0)
        col = lax.broadcasted_iota(jnp.int32, (tb, tb), 1)
        tri = (row > col).astype(BF16)

        def step(carry):
            j, acc, cr = carry
            where = [_sb_block(qi, j + u, tb) for u in range(SB_UNROLL)]
            scores = [_sb_scores(qv, k_ref[pl.ds(ks, tb), :], diag, exists, row, col) for ks, diag, exists in where]
            laters = [_split_dot(lk, tri) for _, lk, _ in scores]
            for (ks, _, _), (ls, lk, mask), later in zip(where, scores, laters):
                att = jnp.where(mask, jnp.exp(ls + later + cr), 0.0)
                acc = acc + jnp.dot(att.astype(BF16), v_ref[pl.ds(ks, tb), :], preferred_element_type=F32)
                cr = cr + jnp.sum(lk, axis=1, keepdims=True)
            return j + SB_UNROLL, acc, cr

        _, acc, _ = lax.while_loop(functools.partial(_sb_more, qi), step,
                                   (jnp.int32(0), jnp.zeros((tb, dh), F32), jnp.zeros((tb, 1), F32)))
        o_ref[...] = acc

    blk = pl.BlockSpec((None, tb, dh), lambda b, i: (b, i, 0))
    full = pl.BlockSpec((None, l, dh), lambda b, i: (b, 0, 0))
    return pl.pallas_call(
        body, name="sb_attn", grid=(bh, nq), in_specs=[blk, full, full], out_specs=blk,
        out_shape=jax.ShapeDtypeStruct((bh, l, dh), F32), compiler_params=_params("parallel", "parallel"))(q, k, v)


def _sb_attn_bwd(q, k, v, o, do):
    bh, l, dh = q.shape
    tb = min(SB_BLOCK, l)
    nq = l // tb
    scale = SB_HEAD_DIM ** -0.5

    def body(q_ref, k_ref, v_ref, o_ref, do_ref, dq_ref, dk_ref, dv_ref):
        qi = pl.program_id(1)

        @pl.when(qi == 0)
        def _():
            dk_ref[...] = jnp.zeros_like(dk_ref)
            dv_ref[...] = jnp.zeros_like(dv_ref)

        qv = q_ref[...]
        dob = do_ref[...].astype(BF16)
        dsum = jnp.sum(dob.astype(F32) * o_ref[...], axis=1, keepdims=True)
        row = lax.broadcasted_iota(jnp.int32, (tb, tb), 0)
        col = lax.broadcasted_iota(jnp.int32, (tb, tb), 1)
        tri = (row > col).astype(BF16)
        tri_inc = (row >= col).astype(BF16)

        def step(carry):
            j, dq, cr, ce = carry
            n = range(SB_UNROLL)
            where = [_sb_block(qi, j + u, tb) for u in n]
            rows = [pl.ds(ks, tb) for ks, _, _ in where]
            scores = [_sb_scores(qv, k_ref[rows[u], :], where[u][1], where[u][2], row, col) for u in n]
            laters = [_split_dot(lk, tri) for _, lk, _ in scores]
            datts = [lax.dot_general(dob, v_ref[rows[u], :], NT, preferred_element_type=F32) for u in n]
            atts = []
            for (ls, lk, mask), later in zip(scores, laters):
                atts.append(jnp.where(mask, jnp.exp(ls + later + cr), 0.0).astype(BF16))
                cr = cr + jnp.sum(lk, axis=1, keepdims=True)
            es = [atts[u].astype(F32) * datts[u] for u in n]
            sufs = [_split_dot(e, tri_inc) for e in es]
            dzs = []
            for (ls, _, mask), e, suf in zip(scores, es, sufs):
                pre = dsum - ce - suf
                sg = jnp.exp(ls)
                dzs.append((jnp.where(mask, e * (1.0 - sg) - pre * sg, 0.0) * scale).astype(BF16))
                ce = ce + jnp.sum(e, axis=1, keepdims=True)
            for u in n:
                dq = dq + jnp.dot(dzs[u], k_ref[rows[u], :], preferred_element_type=F32)
                dk_ref[rows[u], :] += lax.dot_general(dzs[u], qv, TN, preferred_element_type=F32)
                dv_ref[rows[u], :] += lax.dot_general(atts[u], dob, TN, preferred_element_type=F32)
            return j + SB_UNROLL, dq, cr, ce

        zc = jnp.zeros((tb, 1), F32)
        _, dq, _, _ = lax.while_loop(functools.partial(_sb_more, qi), step,
                                     (jnp.int32(0), jnp.zeros((tb, dh), F32), zc, zc))
        dq_ref[...] = dq

    blk = pl.BlockSpec((None, tb, dh), lambda b, i: (b, i, 0))
    full = pl.BlockSpec((None, l, dh), lambda b, i: (b, 0, 0))
    sd = jax.ShapeDtypeStruct((bh, l, dh), F32)
    return pl.pallas_call(
        body, name="sb_attn_bwd", grid=(bh, nq), in_specs=[blk, full, full, blk, blk], out_specs=[blk, full, full],
        out_shape=[sd, sd, sd], compiler_params=_params("parallel", "arbitrary"))(q, k, v, o, do)


def _to_heads(x, bl):
    t, w = x.shape
    heads = w // SB_HEAD_DIM
    l = t // bl
    return x.reshape(bl, l, heads, SB_HEAD_DIM).transpose(0, 2, 1, 3).reshape(bl * heads, l, SB_HEAD_DIM)


def _from_heads(x, bl):
    bh, l, dh = x.shape
    heads = bh // bl
    return x.reshape(bl, heads, l, dh).transpose(0, 2, 1, 3).reshape(bl * l, heads * dh)


def _sb_fwd(h, g, w_qkv, w_o, bl):
    t, d = h.shape
    hn = _rmsnorm("mix_norm", h, g)
    qkv = _mm_cs("sb_qkv", hn, w_qkv, 0, "flat", BF16)
    q, k, v = (_to_heads(qkv[:, i * d:(i + 1) * d], bl) for i in range(3))
    o = _sb_attn_fwd(q, k, v)
    ob = _from_heads(o, bl).astype(BF16)
    out = _mm_rs("sb_out", ob, "flat", w_o, 0, res=h)
    return out, (h, hn, q, k, v, o, ob)


def _sb_bwd(dout, saved, g, w_qkv, w_o, bl):
    h, hn, q, k, v, o, ob = saved
    dob = dout.astype(BF16)
    dwo = _mm_dw("sb_dwo", ob, "flat", dob, None, (None, 0, 1))
    do = _mm_rs_dx("sb_out_dx", dob, w_o, 0, "flat", F32)
    dq, dk, dv = _sb_attn_bwd(q, k, v, o, _to_heads(do, bl))
    dqkv = jnp.concatenate([_from_heads(a, bl).astype(BF16) for a in (dq, dk, dv)], axis=1)
    dwqkv = _mm_dw("sb_dwqkv", hn, None, dqkv, "flat", (None, 0, 1))
    dhn = _mm_cs_dx("sb_qkv_dx", [(dqkv, w_qkv)], "flat", 0)
    dh, dg = _rmsnorm_bwd("mix_norm_bwd", dout, dhn, h, g)
    return dh, dg, dwqkv, dwo


def _adamw_update(wv, gr, mv, vv):
    c1 = 1.0 / (1.0 - ADAM_B1 ** ADAM_STEP)
    c2 = 1.0 / (1.0 - ADAM_B2 ** ADAM_STEP)
    mn = ADAM_B1 * mv + (1.0 - ADAM_B1) * gr
    vn = ADAM_B2 * vv + (1.0 - ADAM_B2) * gr * gr
    delta = -ADAM_LR * ((mn * c1) / (jnp.sqrt(vn * c2) + ADAM_EPS) + ADAM_WD * wv)
    return delta, mn, vn


def _adamw_small(w, gr, m, v):
    def fn(wv, gv, mv, vv):
        return list(_adamw_update(wv, gv, mv, vv)), []
    return _rows("adamw_small", fn, [w, gr, m, v], [(w.shape[1], F32)] * 3)[0]


def _place():
    x, y, c = lax.axis_index("x"), lax.axis_index("y"), lax.axis_index("c")
    chips = [(1 - x, y), (x, 1 - y), (1 - x, 1 - y)]
    return x, y, c, chips


def _remote(src, dst, send_sem, recv_sem, to):
    return pltpu.make_async_remote_copy(src_ref=src, dst_ref=dst, send_sem=send_sem, recv_sem=recv_sem,
                                        device_id=to, device_id_type=MESH)


def _half(ref, c, rh, lead):
    return ref.at[(slice(None),) * lead + (pl.ds(c * rh, rh),)]


def _allgather_weights(ws):
    n = len(ws)

    def body(*refs):
        ins, outs = refs[:n], refs[n:2 * n]
        send, recv = refs[2 * n:]
        x, y, c, chips = _place()
        own = 2 * x + y
        sibling = (x, y, 1 - c)
        sent = []
        for t in range(n):
            rh = ws[t].shape[1] // 2
            for j, chip in enumerate(chips):
                cp = _remote(_half(ins[t], c, rh, 1), _half(outs[t].at[own], c, rh, 1), send.at[t, j], recv.at[t, j], (*chip, c))
                cp.start()
                sent.append(cp)
        for t in range(n):
            rh = ws[t].shape[1] // 2
            for j, chip in enumerate(chips):
                landed = _half(outs[t].at[2 * chip[0] + chip[1]], c, rh, 1)
                _remote(landed, landed, send.at[t, j], recv.at[t, j], (*chip, c)).wait_recv()
                cp = _remote(landed, landed, send.at[t, 3 + j], recv.at[t, 3 + j], sibling)
                cp.start()
                sent.append(cp)
        for t in range(n):
            rh = ws[t].shape[1] // 2
            for j, chip in enumerate(chips):
                passed = _half(outs[t].at[2 * chip[0] + chip[1]], 1 - c, rh, 1)
                _remote(passed, passed, send.at[t, 3 + j], recv.at[t, 3 + j], sibling).wait_recv()
        for cp in sent:
            cp.wait_send()

    res = pl.pallas_call(
        body, name="allgather_weights", in_specs=[ANY] * n, out_specs=[ANY] * n,
        out_shape=[jax.ShapeDtypeStruct((N_CHIPS,) + w.shape, w.dtype) for w in ws],
        scratch_shapes=[pltpu.SemaphoreType.DMA((n, 6)), pltpu.SemaphoreType.DMA((n, 6))],
    )(*ws)
    own = 2 * lax.axis_index("x") + lax.axis_index("y")
    return [lax.dynamic_update_slice(g, w[None], (own, 0, 0, 0)) for g, w in zip(res, ws)]


def _pair_exchange(gs):
    n = len(gs)

    def body(*refs):
        ins, outs = refs[:n], refs[n:2 * n]
        send, recv = refs[2 * n:]
        x, y, c, _ = _place()
        copies = [_remote(_half(ins[t], 1 - c, gs[t].shape[2] // 2, 2), outs[t], send.at[t], recv.at[t], (x, y, 1 - c))
                  for t in range(n)]
        for cp in copies:
            cp.start()
        for cp in copies:
            cp.wait()

    return pl.pallas_call(
        body, name="grad_pair_exchange", in_specs=[ANY] * n, out_specs=[ANY] * n,
        out_shape=[jax.ShapeDtypeStruct(g.shape[:2] + (g.shape[2] // 2, g.shape[3]), F32) for g in gs],
        scratch_shapes=[pltpu.SemaphoreType.DMA((n,)), pltpu.SemaphoreType.DMA((n,))],
    )(*gs)


def _pair_sum(g, theirs, c_idx):
    n4, ly, r, cc = g.shape
    rh = r // 2
    tm = _tile(rh, 256)
    nt = rh // tm

    def body(c_ref, g_ref, t_ref, o_ref):
        o_ref[...] = (g_ref[...] + t_ref[...]).astype(o_ref.dtype)

    blk = (None, tm, cc)
    grid_spec = pltpu.PrefetchScalarGridSpec(
        num_scalar_prefetch=1, grid=(n4 * ly, nt),
        in_specs=[pl.BlockSpec(blk, lambda a, i, cr: (a, cr[0] * nt + i, 0)), pl.BlockSpec(blk, lambda a, i, cr: (a, i, 0))],
        out_specs=pl.BlockSpec(blk, lambda a, i, cr: (a, i, 0)))
    out = pl.pallas_call(
        body, name="grad_pair_sum", grid_spec=grid_spec, out_shape=jax.ShapeDtypeStruct((n4 * ly, rh, cc), BF16),
        compiler_params=_params("parallel", "parallel"))(c_idx, g.reshape(n4 * ly, r, cc), theirs.reshape(n4 * ly, rh, cc))
    return out.reshape(n4, ly, rh, cc)


def _chip_exchange(ps):
    n = len(ps)

    def body(*refs):
        ins, outs = refs[:n], refs[n:2 * n]
        send, recv = refs[2 * n:]
        x, y, c, chips = _place()
        copies = []
        for t in range(n):
            for j, chip in enumerate(chips):
                copies.append(_remote(ins[t].at[2 * chip[0] + chip[1]], outs[t].at[j], send.at[t, j], recv.at[t, j], (*chip, c)))
        for cp in copies:
            cp.start()
        for cp in copies:
            cp.wait()

    return pl.pallas_call(
        body, name="grad_chip_exchange", in_specs=[ANY] * n, out_specs=[ANY] * n,
        out_shape=[jax.ShapeDtypeStruct((3,) + p.shape[1:], p.dtype) for p in ps],
        scratch_shapes=[pltpu.SemaphoreType.DMA((n, 3)), pltpu.SemaphoreType.DMA((n, 3))],
    )(*ps)


def _chip_sum(p, landed, own_idx):
    _, ly, rh, cc = p.shape
    tm = _tile(rh, 256)

    def body(o_ref, p_ref, a_ref, b_ref, c_ref, out_ref):
        up = lambda r: r[...].astype(F32)
        out_ref[...] = ((up(p_ref) + up(a_ref)) + up(b_ref)) + up(c_ref)

    blk = (None, None, tm, cc)
    slot = lambda j: pl.BlockSpec(blk, lambda l, i, o: (j, l, i, 0))
    grid_spec = pltpu.PrefetchScalarGridSpec(
        num_scalar_prefetch=1, grid=(ly, rh // tm),
        in_specs=[pl.BlockSpec(blk, lambda l, i, o: (o[0], l, i, 0)), slot(0), slot(1), slot(2)],
        out_specs=pl.BlockSpec((None, tm, cc), lambda l, i, o: (l, i, 0)))
    return pl.pallas_call(
        body, name="grad_chip_sum", grid_spec=grid_spec, out_shape=jax.ShapeDtypeStruct((ly, rh, cc), F32),
        compiler_params=_params("parallel", "parallel"))(own_idx, p, landed, landed, landed)


def _pair_swap(halves):
    n = len(halves)

    def body(*refs):
        ins, outs = refs[:n], refs[n:2 * n]
        send, recv = refs[2 * n:]
        x, y, c, _ = _place()
        copies = [_remote(ins[t], outs[t], send.at[t], recv.at[t], (x, y, 1 - c)) for t in range(n)]
        for cp in copies:
            cp.start()
        for cp in copies:
            cp.wait()

    return pl.pallas_call(
        body, name="grad_pair_swap", in_specs=[ANY] * n, out_specs=[ANY] * n,
        out_shape=[jax.ShapeDtypeStruct(h.shape, F32) for h in halves],
        scratch_shapes=[pltpu.SemaphoreType.DMA((n,)), pltpu.SemaphoreType.DMA((n,))],
    )(*halves)


def _adamw_big(w, m, v, mine, theirs, c_idx):
    ly, r, cc = w.shape
    rh = r // 2
    tm = _tile(rh, 256)
    nt = rh // tm

    def body(c_ref, w_ref, m_ref, v_ref, a_ref, b_ref, g_out, d_out, m_out, v_out):
        gr = jnp.where(pl.program_id(1) == c_ref[0], a_ref[...], b_ref[...])
        delta, mn, vn = _adamw_update(w_ref[...], gr, m_ref[...], v_ref[...])
        g_out[...] = gr
        d_out[...] = delta
        m_out[...] = mn
        v_out[...] = vn

    blk = (None, tm, cc)
    full = pl.BlockSpec(blk, lambda l, hc, i, cr: (l, hc * nt + i, 0))
    half = pl.BlockSpec(blk, lambda l, hc, i, cr: (l, i, 0))
    grid_spec = pltpu.PrefetchScalarGridSpec(
        num_scalar_prefetch=1, grid=(ly, 2, nt), in_specs=[full, full, full, half, half], out_specs=[full] * 4)
    sd = jax.ShapeDtypeStruct(w.shape, F32)
    return pl.pallas_call(
        body, name="adamw", grid_spec=grid_spec, out_shape=[sd] * 4,
        compiler_params=_params("parallel", "parallel", "parallel"))(c_idx, w, m, v, mine, theirs)


def _allreduce_small(v):
    rows, w = v.shape

    def body(x_ref, sum_ref, all_ref, send, recv, local):
        x, y, c, chips = _place()
        me, sibling = (x, y, c), (x, y, 1 - c)

        def slot(px, py, pc):
            return all_ref.at[4 * px + 2 * py + pc]

        def copy(k, block, to, src=None):
            return _remote(slot(*block) if src is None else src, slot(*block), send.at[k], recv.at[k], to)

        mine = pltpu.make_async_copy(x_ref, slot(*me), local)
        mine.start()
        first = [copy(0, me, sibling, src=x_ref)]
        first += [copy(1 + j, me, (*chip, c), src=x_ref) for j, chip in enumerate(chips)]
        for cp in first:
            cp.start()
        passed = [copy(4 + j, (*chip, c), sibling) for j, chip in enumerate(chips)]
        for j, chip in enumerate(chips):
            copy(1 + j, (*chip, c), me).wait_recv()
            passed[j].start()
        copy(0, sibling, me).wait_recv()
        for j, chip in enumerate(chips):
            copy(4 + j, (*chip, 1 - c), me).wait_recv()
        for cp in first + passed:
            cp.wait_send()
        mine.wait()
        tot = all_ref[0]
        for k in range(1, N_DEV):
            tot = tot + all_ref[k]
        sum_ref[...] = tot

    vm = pl.BlockSpec(memory_space=pltpu.VMEM)
    return pl.pallas_call(
        body, name="allreduce_small", in_specs=[vm], out_specs=[vm, vm],
        out_shape=[jax.ShapeDtypeStruct((rows, w), F32), jax.ShapeDtypeStruct((N_DEV, rows, w), F32)],
        scratch_shapes=[pltpu.SemaphoreType.DMA((7,)), pltpu.SemaphoreType.DMA((7,)), pltpu.SemaphoreType.DMA],
        compiler_params=pltpu.CompilerParams(vmem_limit_bytes=VMEM_LIMIT),
    )(v)[0]


BIG = ["ffn1_w1", "ffn1_w3", "ffn1_w2", "ffn2_w1", "ffn2_w3", "ffn2_w2", "ple_proj", "ple_gate",
       "s5_w_in", "s5_w_glu", "sb_w_qkv", "sb_w_o"]
SMALL = ["ffn1_norm", "mix_norm", "ffn2_norm", "ple_norm", "s5_a_re", "s5_a_im", "s5_log_dt", "s5_b_re", "s5_b_im",
         "s5_c_re", "s5_c_im", "s5_d", "final_norm"]
ORDER = ["ffn1_norm", "ffn1_w1", "ffn1_w3", "ffn1_w2", "mix_norm", "ffn2_norm", "ffn2_w1", "ffn2_w3", "ffn2_w2",
         "ple_norm", "ple_proj", "ple_gate", "s5_w_in", "s5_a_re", "s5_a_im", "s5_log_dt", "s5_b_re", "s5_b_im",
         "s5_c_re", "s5_c_im", "s5_d", "s5_w_glu", "sb_w_qkv", "sb_w_o", "final_norm"]


def _pack(arrays):
    flat = jnp.concatenate([a.reshape(-1) for a in arrays])
    pad = (-flat.shape[0]) % 1024
    return jnp.pad(flat, (0, pad)).reshape(-1, 128)


def _unpack(packed, like):
    flat = packed.reshape(-1)
    out, off = [], 0
    for a in like:
        out.append(flat[off:off + a.size].reshape(a.shape))
        off += a.size
    return out


def _fwd_bwd(x, p, target, w, gathered):
    bl, l, d = x.shape
    t = bl * l
    depth = w["ffn1_norm"].shape[0]
    s5_ops, s5_vjp = jax.vjp(_s5_prep, w["s5_a_re"][0], w["s5_a_im"][0], w["s5_log_dt"][0], w["s5_b_re"][0],
                             w["s5_b_im"][0], w["s5_c_re"][0], w["s5_c_im"][0])

    h = x.reshape(t, d)
    p2 = [p[i].reshape(t, p.shape[-1]).astype(BF16) for i in range(depth)]
    saved = []
    for i in range(depth):
        norm = lambda name: w[name][i:i + 1]
        h, s1 = _ffn_fwd(h, norm("ffn1_norm"), gathered["ffn1_w1"], gathered["ffn1_w3"], gathered["ffn1_w2"], i)
        if i % 2 == 0:
            h, s2 = _s5_fwd(h, norm("mix_norm"), s5_ops, w["s5_d"][i // 2:i // 2 + 1], gathered["s5_w_in"], gathered["s5_w_glu"], bl)
        else:
            h, s2 = _sb_fwd(h, norm("mix_norm"), gathered["sb_w_qkv"], gathered["sb_w_o"], bl)
        h, s3 = _ffn_fwd(h, norm("ffn2_norm"), gathered["ffn2_w1"], gathered["ffn2_w3"], gathered["ffn2_w2"], i)
        h, s4 = _ple_fwd(h, norm("ple_norm"), p2[i], gathered["ple_proj"], gathered["ple_gate"], i)
        saved.append((s1, s2, s3, s4))

    loss, dh, dfinal = _head(h, w["final_norm"].reshape(1, d), target.reshape(t, d))

    big = {k: None for k in BIG}
    small = {k: [None] * w[k].shape[0] if w[k].ndim > 1 else None for k in SMALL}
    small["final_norm"] = dfinal.reshape(d)
    for i in reversed(range(depth)):
        norm = lambda name: w[name][i:i + 1]
        slots = lambda *names: [(big[k], i, depth) for k in names]
        s1, s2, s3, s4 = saved[i]
        dh, dg, big["ple_proj"], big["ple_gate"] = _ple_bwd(
            dh, s4, norm("ple_norm"), p2[i], gathered["ple_proj"], gathered["ple_gate"], i, slots("ple_proj", "ple_gate"))
        small["ple_norm"][i] = dg[0]
        dh, dg, big["ffn2_w1"], big["ffn2_w3"], big["ffn2_w2"] = _ffn_bwd(
            dh, s3, norm("ffn2_norm"), gathered["ffn2_w1"], gathered["ffn2_w3"], gathered["ffn2_w2"], i,
            slots("ffn2_w1", "ffn2_w3", "ffn2_w2"))
        small["ffn2_norm"][i] = dg[0]
        if i % 2 == 0:
            dh, dg, big["s5_w_in"], big["s5_w_glu"], dd, dops = _s5_bwd(
                dh, s2, norm("mix_norm"), s5_ops, w["s5_d"][i // 2:i // 2 + 1], gathered["s5_w_in"], gathered["s5_w_glu"], bl)
            small["s5_d"][0] = dd[0]
            raw = s5_vjp(dops)
            for name, gr in zip(["s5_a_re", "s5_a_im", "s5_log_dt", "s5_b_re", "s5_b_im", "s5_c_re", "s5_c_im"], raw):
                small[name][0] = gr
        else:
            dh, dg, big["sb_w_qkv"], big["sb_w_o"] = _sb_bwd(dh, s2, norm("mix_norm"), gathered["sb_w_qkv"], gathered["sb_w_o"], bl)
        small["mix_norm"][i] = dg[0]
        dh, dg, big["ffn1_w1"], big["ffn1_w3"], big["ffn1_w2"] = _ffn_bwd(
            dh, s1, norm("ffn1_norm"), gathered["ffn1_w1"], gathered["ffn1_w3"], gathered["ffn1_w2"], i,
            slots("ffn1_w1", "ffn1_w3", "ffn1_w2"))
        small["ffn1_norm"][i] = dg[0]
    small_list = [jnp.stack(small[k]) if isinstance(small[k], list) else small[k] for k in SMALL]
    return loss, dh.reshape(bl, l, d), big, small_list


def _step(x, p, target, w, m, v):
    gathered = dict(zip(BIG, _allgather_weights([w[k].astype(BF16) for k in BIG])))
    loss, grad_x, big, small_list = _fwd_bwd(x, p, target, w, gathered)

    c_idx = lax.axis_index("c").astype(jnp.int32).reshape(1)
    own_idx = (2 * lax.axis_index("x") + lax.axis_index("y")).astype(jnp.int32).reshape(1)
    partial = [big[k] for k in BIG]
    pair = [_pair_sum(g, t, c_idx) for g, t in zip(partial, _pair_exchange(partial))]
    mine = [_chip_sum(pr, ld, own_idx) for pr, ld in zip(pair, _chip_exchange(pair))]
    theirs = _pair_swap(mine)
    out_g, out_d, out_m, out_v = {}, {}, {}, {}
    for k, a, b in zip(BIG, mine, theirs):
        out_g[k], out_d[k], out_m[k], out_v[k] = _adamw_big(w[k], m[k], v[k], a, b, c_idx)

    like = [w[k] for k in SMALL]
    pad = [jnp.zeros((1,), F32)]
    g_small = _allreduce_small(_pack(small_list + [loss.reshape(1)]))
    packed = (g_small,) + tuple(_adamw_small(_pack(like + pad), g_small, _pack([m[k] for k in SMALL] + pad),
                                             _pack([v[k] for k in SMALL] + pad)))
    for dst, pk in zip((out_g, out_d, out_m, out_v), packed):
        dst.update(dict(zip(SMALL, _unpack(pk, like))))
    loss = g_small.reshape(-1)[sum(a.size for a in like)]
    return (loss, grad_x, *[out_g[k] for k in ORDER], *[out_d[k] for k in ORDER],
            *[out_m[k] for k in ORDER], *[out_v[k] for k in ORDER])


def kernel(x, p, ffn1_norm, ffn1_w1, ffn1_w3, ffn1_w2, mix_norm, ffn2_norm, ffn2_w1, ffn2_w3, ffn2_w2, ple_norm, ple_proj, ple_gate, s5_w_in, s5_a_re, s5_a_im, s5_log_dt, s5_b_re, s5_b_im, s5_c_re, s5_c_im, s5_d, s5_w_glu, sb_w_qkv, sb_w_o, final_norm, loss_target, m_ffn1_norm, m_ffn1_w1, m_ffn1_w3, m_ffn1_w2, m_mix_norm, m_ffn2_norm, m_ffn2_w1, m_ffn2_w3, m_ffn2_w2, m_ple_norm, m_ple_proj, m_ple_gate, m_s5_w_in, m_s5_a_re, m_s5_a_im, m_s5_log_dt, m_s5_b_re, m_s5_b_im, m_s5_c_re, m_s5_c_im, m_s5_d, m_s5_w_glu, m_sb_w_qkv, m_sb_w_o, m_final_norm, v_ffn1_norm, v_ffn1_w1, v_ffn1_w3, v_ffn1_w2, v_mix_norm, v_ffn2_norm, v_ffn2_w1, v_ffn2_w3, v_ffn2_w2, v_ple_norm, v_ple_proj, v_ple_gate, v_s5_w_in, v_s5_a_re, v_s5_a_im, v_s5_log_dt, v_s5_b_re, v_s5_b_im, v_s5_c_re, v_s5_c_im, v_s5_d, v_s5_w_glu, v_sb_w_qkv, v_sb_w_o, v_final_norm):
    args = dict(locals())
    w = {k: args[k] for k in ORDER}
    m = {k: args["m_" + k] for k in ORDER}
    v = {k: args["v_" + k] for k in ORDER}
    return _step(x, p, loss_target, w, m, v)
```

```python
import functools
import math

import jax
import jax.numpy as jnp
from jax import lax
from jax.experimental import pallas as pl
from jax.experimental.pallas import tpu as pltpu

F32 = jnp.float32
BF16 = jnp.bfloat16
MESH = pl.DeviceIdType.MESH

N_CHIPS = 4
N_DEV = 8
RMS_EPS = 1e-6
S5_GROUP = 16
S5_STATE = 64
S5_CHUNK = 16
SB_HEAD_DIM = 64
SB_BLOCK = 128
SB_CUT = -104.0
SB_UNROLL = 3
ADAM_LR, ADAM_B1, ADAM_B2, ADAM_EPS, ADAM_WD, ADAM_STEP = 0.001, 0.9, 0.999, 1e-08, 0.01, 10
VMEM_LIMIT = 48 * 1024 * 1024

NN = (((1,), (0,)), ((), ()))
NT = (((1,), (1,)), ((), ()))
TN = (((0,), (0,)), ((), ()))

ANY = pl.BlockSpec(memory_space=pl.ANY)


def _tile(n, target):
    if n <= target:
        return n
    for t in range(target - target % 8, 7, -8):
        if n % t == 0:
            return t
    raise ValueError(f"no row tile for {n}")


def _params(*semantics):
    return pltpu.CompilerParams(dimension_semantics=semantics, vmem_limit_bytes=VMEM_LIMIT)


def _sigmoid(v):
    return 1.0 / (1.0 + jnp.exp(-v))


def _gemm(name, grid, operands, in_specs, groups, acc_shapes, out_shapes, out_specs, epilogue, reduce_axis=None, aliases=None):
    n_in, n_out = len(operands), len(out_shapes)
    n_red = None if reduce_axis is None else grid[reduce_axis]

    def body(*refs):
        ins, outs, accs = refs[:n_in], refs[n_in:n_in + n_out], refs[n_in + n_out:]

        def products():
            res = []
            for terms in groups:
                tot = None
                for ia, ib, dims in terms:
                    d = lax.dot_general(ins[ia][...], ins[ib][...], dims, preferred_element_type=F32)
                    tot = d if tot is None else tot + d
                res.append(tot)
            return res

        def finish(vals):
            for o, v in zip(outs, epilogue(vals, ins)):
                o[...] = v.astype(o.dtype)

        if reduce_axis is None:
            finish(products())
        else:
            k = pl.program_id(reduce_axis)

            @pl.when(k == 0)
            def _():
                for a in accs:
                    a[...] = jnp.zeros_like(a)

            for a, d in zip(accs, products()):
                a[...] += d

            @pl.when(k == n_red - 1)
            def _():
                finish([a[...] for a in accs])

    scratch = [] if reduce_axis is None else [pltpu.VMEM(s, F32) for s in acc_shapes]
    sem = tuple("arbitrary" if i == reduce_axis else "parallel" for i in range(len(grid)))
    return pl.pallas_call(
        body, name=name, grid=grid, in_specs=in_specs, out_specs=out_specs, out_shape=out_shapes,
        scratch_shapes=scratch, input_output_aliases=aliases or {}, compiler_params=_params(*sem))(*operands)


def _ident(vals, ins):
    return vals


def _act_spec(layout, tm, cs, pos):
    if layout == "sm":
        return pl.BlockSpec((None, tm, cs), lambda *g: (pos(*g)[1], pos(*g)[0], 0))
    return pl.BlockSpec((tm, cs), lambda *g: pos(*g))


def _act_shape(layout, t, cs, dtype):
    return jax.ShapeDtypeStruct((N_CHIPS, t, cs) if layout == "sm" else (t, N_CHIPS * cs), dtype)


def _w_spec(w, layer, pos_k):
    _, _, r, c = w.shape
    return pl.BlockSpec((None, None, r, c), lambda *g: (pos_k(*g), layer, 0, 0))


def _mm_cs(name, x, w, layer, out_layout, out_dtype, tm=512):
    t, kd = x.shape
    cs = w.shape[3]
    tm = _tile(t, tm)
    return _gemm(
        name, (N_CHIPS, t // tm), [x, w],
        [pl.BlockSpec((tm, kd), lambda k, i: (i, 0)), _w_spec(w, layer, lambda k, i: k)],
        [[(0, 1, NN)]], None, [_act_shape(out_layout, t, cs, out_dtype)],
        [_act_spec(out_layout, tm, cs, lambda k, i: (i, k))], _ident)[0]


def _mm_rs(name, xs, layout, w, layer, res=None, alpha=1.0, out_dtype=F32, tm=1024):
    ks, n = w.shape[2], w.shape[3]
    t = xs.shape[1] if layout == "sm" else xs.shape[0]
    tm = _tile(t, tm)
    operands = [xs, w] + ([] if res is None else [res])
    specs = [_act_spec(layout, tm, ks, lambda i, k: (i, k)), _w_spec(w, layer, lambda i, k: k)]
    if res is not None:
        specs.append(pl.BlockSpec((tm, n), lambda i, k: (i, 0)))

    def epilogue(vals, ins):
        y = alpha * vals[0]
        return [y if res is None else ins[2][...] + y]

    return _gemm(
        name, (t // tm, N_CHIPS), operands, specs, [[(0, 1, NN)]], [(tm, n)],
        [jax.ShapeDtypeStruct((t, n), out_dtype)], [pl.BlockSpec((tm, n), lambda i, k: (i, 0))],
        epilogue, reduce_axis=1)[0]


def _mm_cs_dx(name, pairs, layout, layer, tm=1024):
    w0 = pairs[0][1]
    kd, cs = w0.shape[2], w0.shape[3]
    dy0 = pairs[0][0]
    t = dy0.shape[1] if layout == "sm" else dy0.shape[0]
    tm = _tile(t, tm)
    operands, specs, terms = [], [], []
    for dy, w in pairs:
        terms.append((len(operands), len(operands) + 1, NT))
        operands += [dy, w]
        specs += [_act_spec(layout, tm, cs, lambda i, k: (i, k)), _w_spec(w, layer, lambda i, k: k)]
    return _gemm(
        name, (t // tm, N_CHIPS), operands, specs, [terms], [(tm, kd)],
        [jax.ShapeDtypeStruct((t, kd), F32)], [pl.BlockSpec((tm, kd), lambda i, k: (i, 0))],
        _ident, reduce_axis=1)[0]


def _mm_rs_dx(name, dy, w, layer, out_layout, out_dtype, tm=512):
    t, n = dy.shape
    ks = w.shape[2]
    tm = _tile(t, tm)
    return _gemm(
        name, (N_CHIPS, t // tm), [dy, w],
        [pl.BlockSpec((tm, n), lambda k, i: (i, 0)), _w_spec(w, layer, lambda k, i: k)],
        [[(0, 1, NT)]], None, [_act_shape(out_layout, t, ks, out_dtype)],
        [_act_spec(out_layout, tm, ks, lambda k, i: (i, k))], _ident)[0]


def _mm_dw(name, x, x_layout, dy, dy_layout, slot, alpha=1.0, tk=2048):
    stack, layer, layers = slot
    if x_layout is None:
        t, rows = x.shape
        cols = dy.shape[2] if dy_layout == "sm" else dy.shape[1] // N_CHIPS
        tk = _tile(t, tk)
        xspec = pl.BlockSpec((tk, rows), lambda k, j: (j, 0))
        yspec = _act_spec(dy_layout, tk, cols, lambda k, j: (j, k))
    else:
        t, cols = dy.shape
        rows = x.shape[2] if x_layout == "sm" else x.shape[1] // N_CHIPS
        tk = _tile(t, tk)
        xspec = _act_spec(x_layout, tk, rows, lambda k, j: (j, k))
        yspec = pl.BlockSpec((tk, cols), lambda k, j: (j, 0))
    operands, specs = [x, dy], [xspec, yspec]
    if stack is not None:
        operands.append(stack)
        specs.append(ANY)
    return _gemm(
        name, (N_CHIPS, t // tk), operands, specs, [[(0, 1, TN)]], [(rows, cols)],
        [jax.ShapeDtypeStruct((N_CHIPS, layers, rows, cols), F32)],
        [pl.BlockSpec((None, None, rows, cols), lambda k, j: (k, layer, 0, 0))],
        lambda vals, ins: [alpha * vals[0]], reduce_axis=1, aliases=None if stack is None else {2: 0})[0]


def _rows(name, fn, ins, outs, accs=(), tm=256):
    t = ins[0].shape[0]
    tm = _tile(t, tm)
    n_in, n_out, n_acc = len(ins), len(outs), len(accs)
    in_specs = []
    for a in ins:
        if a.shape[0] == t:
            in_specs.append(pl.BlockSpec((tm, a.shape[1]), lambda i: (i, 0)))
        else:
            in_specs.append(pl.BlockSpec(a.shape, lambda i: (0, 0)))
    out_shape = [jax.ShapeDtypeStruct((t, c), d) for c, d in outs] + [jax.ShapeDtypeStruct(s, F32) for s in accs]
    out_specs = [pl.BlockSpec((tm, c), lambda i: (i, 0)) for c, _ in outs] + [pl.BlockSpec(s, lambda i: (0, 0)) for s in accs]

    def body(*refs):
        i = pl.program_id(0)
        row_vals, acc_vals = fn(*[r[...] for r in refs[:n_in]])
        for o, v in zip(refs[n_in:n_in + n_out], row_vals):
            o[...] = v.astype(o.dtype)
        acc_refs = refs[n_in + n_out:]
        if n_acc:
            @pl.when(i == 0)
            def _():
                for a in acc_refs:
                    a[...] = jnp.zeros_like(a)

            for a, v in zip(acc_refs, acc_vals):
                a[...] += v

    res = pl.pallas_call(
        body, name=name, grid=(t // tm,), in_specs=in_specs, out_specs=out_specs, out_shape=out_shape,
        compiler_params=_params("arbitrary" if n_acc else "parallel"))(*ins)
    return res[:n_out], res[n_out:]


def _to_bf16(a):
    def fn(x):
        return [x], []
    return _rows("weights_bf16", fn, [a.reshape(-1, a.shape[-1])], [(a.shape[-1], BF16)], tm=512)[0][0].reshape(a.shape)


def _rms_stats(x):
    return lax.rsqrt(jnp.mean(x * x, axis=-1, keepdims=True) + RMS_EPS)


def _rmsnorm(name, h, g):
    def fn(x, gg):
        return [x * _rms_stats(x) * gg], []
    return _rows(name, fn, [h, g], [(h.shape[1], BF16)])[0][0]


def _rms_bwd_math(dn, x, g):
    r = _rms_stats(x)
    xhat = x * r
    dxh = dn * g
    dx = r * (dxh - xhat * jnp.mean(dxh * xhat, axis=-1, keepdims=True))
    return dx, jnp.sum(dn * xhat, axis=0, keepdims=True)


def _rmsnorm_bwd(name, dres, dn, h, g):
    def fn(dr, d, x, gg):
        dx, dg = _rms_bwd_math(d, x, gg)
        return [dr + dx], [dg]
    (dh,), (dg,) = _rows(name, fn, [dres, dn, h, g], [(h.shape[1], F32)], [(1, h.shape[1])])
    return dh, dg


def _ffn_fwd(h, g, w1, w3, w2, layer, tm=512):
    t, d = h.shape
    fs = w1.shape[3]
    n = _rmsnorm("ffn_norm", h, g)
    tm = _tile(t, tm)

    def up(vals, ins):
        a, b = vals
        return [a, b, a * _sigmoid(a) * b]

    sm = _act_shape("sm", t, fs, BF16)
    osp = _act_spec("sm", tm, fs, lambda k, i: (i, k))
    a, b, s = _gemm(
        "ffn_up", (N_CHIPS, t // tm), [n, w1, w3],
        [pl.BlockSpec((tm, d), lambda k, i: (i, 0)), _w_spec(w1, layer, lambda k, i: k), _w_spec(w3, layer, lambda k, i: k)],
        [[(0, 1, NN)], [(0, 2, NN)]], None, [sm, sm, sm], [osp, osp, osp], up)
    out = _mm_rs("ffn_down", s, "sm", w2, layer, res=h, alpha=0.5)
    return out, (h, n, a, b, s)


def _ffn_bwd(dout, saved, g, w1, w3, w2, layer, slots, tm=512):
    h, n, a, b, s = saved
    t, d = h.shape
    fs = w1.shape[3]
    tm = _tile(t, tm)
    dob = dout.astype(BF16)

    def down(vals, ins):
        ds = 0.5 * vals[0]
        av, bv = ins[2][...].astype(F32), ins[3][...].astype(F32)
        sg = _sigmoid(av)
        return [ds * bv * sg * (1.0 + av * (1.0 - sg)), ds * av * sg]

    sm = _act_shape("sm", t, fs, BF16)
    asp = _act_spec("sm", tm, fs, lambda k, i: (i, k))
    da, db = _gemm(
        "ffn_down_dx", (N_CHIPS, t // tm), [dob, w2, a, b],
        [pl.BlockSpec((tm, d), lambda k, i: (i, 0)), _w_spec(w2, layer, lambda k, i: k), asp, asp],
        [[(0, 1, NT)]], None, [sm, sm], [asp, asp], down)
    dw2 = _mm_dw("ffn_dw2", s, "sm", dob, None, slots[2], alpha=0.5)
    dw1 = _mm_dw("ffn_dw1", n, None, da, "sm", slots[0])
    dw3 = _mm_dw("ffn_dw3", n, None, db, "sm", slots[1])
    dn = _mm_cs_dx("ffn_up_dx", [(da, w1), (db, w3)], "sm", layer)
    dh, dg = _rmsnorm_bwd("ffn_norm_bwd", dout, dn, h, g)
    return dh, dg, dw1, dw3, dw2


def _ple_fwd(h, g, p2, wproj, wgate, layer):
    n = _rmsnorm("ple_norm", h, g)
    gl = _mm_rs("ple_gate", n, "flat", wgate, layer)
    pp = _mm_cs("ple_proj", p2, wproj, layer, "flat", F32)

    def fn(hh, gg, q):
        return [hh + q * _sigmoid(gg)], []
    out = _rows("ple_mix", fn, [h, gl, pp], [(h.shape[1], F32)])[0][0]
    return out, (h, n, gl, pp)


def _ple_bwd(dout, saved, g, p2, wproj, wgate, layer, slots):
    h, n, gl, pp = saved
    d = h.shape[1]

    def fn(do, gg, q):
        sg = _sigmoid(gg)
        return [do * sg, do * q * sg * (1.0 - sg)], []
    (dpp, dgl), _ = _rows("ple_mix_bwd", fn, [dout, gl, pp], [(d, BF16), (d, BF16)])
    dwproj = _mm_dw("ple_dwproj", p2, None, dpp, "flat", slots[0])
    dwgate = _mm_dw("ple_dwgate", n, "flat", dgl, None, slots[1])
    dn = _mm_rs_dx("ple_gate_dx", dgl, wgate, layer, "flat", F32)
    dh, dg = _rmsnorm_bwd("ple_norm_bwd", dout, dn, h, g)
    return dh, dg, dwproj, dwgate


def _head(h, g, target):
    d = h.shape[1]

    def fn(x, gg, tg):
        y = x * _rms_stats(x) * gg
        err = y - tg
        dy = err * (1.0 / d)
        dx, dg = _rms_bwd_math(dy, x, gg)
        loss = 0.5 * jnp.sum(jnp.sum(err * err, axis=-1, keepdims=True) * (1.0 / d), axis=0, keepdims=True)
        return [dx], [dg, jnp.broadcast_to(loss, (1, 128))]
    (dh,), (dg, loss) = _rows("loss_head", fn, [h, g, target], [(d, F32)], [(1, d), (1, 128)])
    return loss[0, 0], dh, dg


def _s5_prep(a_re, a_im, log_dt, b_re, b_im, c_re, c_im):
    c = S5_CHUNK
    lam_re = jnp.minimum(a_re, -1e-4)
    lam_im = a_im
    dt = jnp.exp(log_dt)[:, None]
    ks = jnp.arange(c + 1, dtype=F32)[:, None, None]
    mag = jnp.exp(lam_re[None] * dt[None] * ks)
    ph = lam_im[None] * dt[None] * ks
    pw_re, pw_im = mag * jnp.cos(ph), mag * jnp.sin(ph)
    den = lam_re * lam_re + lam_im * lam_im
    nr, ni = pw_re[1] - 1.0, pw_im[1]
    fr = (nr * lam_re + ni * lam_im) / den
    fi = (ni * lam_re - nr * lam_im) / den
    bb_re = fr[..., None] * b_re - fi[..., None] * b_im
    bb_im = fr[..., None] * b_im + fi[..., None] * b_re
    ca_re = c_re[None] * pw_re[:, :, None, :] - c_im[None] * pw_im[:, :, None, :]
    ca_im = c_re[None] * pw_im[:, :, None, :] + c_im[None] * pw_re[:, :, None, :]
    hp = lax.Precision.HIGHEST
    kern = (jnp.einsum("kghp,gpj->kghj", ca_re[:c], bb_re, precision=hp)
            - jnp.einsum("kghp,gpj->kghj", ca_im[:c], bb_im, precision=hp))
    lag = jnp.arange(c)[None, :] - jnp.arange(c)[:, None]
    toep = jnp.where((lag >= 0)[:, :, None, None, None], kern[jnp.clip(lag, 0, c - 1)], 0.0)
    g = a_re.shape[0]
    wi = toep.transpose(2, 0, 4, 1, 3).reshape(g, c * S5_GROUP, c * S5_GROUP)
    rev_re, rev_im = pw_re[c - 1::-1][:c], pw_im[c - 1::-1][:c]
    wn_re = rev_re[..., None] * bb_re[None] - rev_im[..., None] * bb_im[None]
    wn_im = rev_re[..., None] * bb_im[None] + rev_im[..., None] * bb_re[None]
    wn = jnp.stack([wn_re, wn_im], axis=0).transpose(2, 1, 4, 0, 3).reshape(g, c * S5_GROUP, 2 * S5_STATE)
    wo = jnp.stack([ca_re[1:], -ca_im[1:]], axis=0).transpose(2, 0, 4, 1, 3).reshape(g, 2 * S5_STATE, c * S5_GROUP)
    ar, ai = pw_re[c], pw_im[c]
    m1 = jnp.concatenate([ar, ar], axis=1)
    m2 = jnp.concatenate([-ai, ai], axis=1)
    return jnp.concatenate([wi, wn], axis=2), wo, m1, m2


def _bmm(name, a, b, dims, out_dtype, gb=8):
    g = a.shape[0]
    gb = min(gb, g)
    m = a.shape[2] if dims == TN else a.shape[1]
    n = b.shape[1] if dims == NT else b.shape[2]

    def body(a_ref, b_ref, o_ref):
        for j in range(gb):
            o_ref[j] = lax.dot_general(a_ref[j], b_ref[j], dims, preferred_element_type=F32).astype(o_ref.dtype)

    return pl.pallas_call(
        body, name=name, grid=(g // gb,),
        in_specs=[pl.BlockSpec((gb,) + a.shape[1:], lambda i: (i, 0, 0)), pl.BlockSpec((gb,) + b.shape[1:], lambda i: (i, 0, 0))],
        out_specs=pl.BlockSpec((gb, m, n), lambda i: (i, 0, 0)),
        out_shape=jax.ShapeDtypeStruct((g, m, n), out_dtype), compiler_params=_params("parallel"))(a, b)


def _s5_scan_fwd(sloc, m1, m2):
    nc, r, w = sloc.shape

    def body(s_ref, m1_ref, m2_ref, o_ref):
        a1, a2 = m1_ref[...], m2_ref[...]

        def step(c, s):
            o_ref[c] = s
            return a1 * s + a2 * pltpu.roll(s, S5_STATE, 1) + s_ref[c]
        lax.fori_loop(0, nc, step, jnp.zeros((r, w), F32))

    vm = pl.BlockSpec(memory_space=pltpu.VMEM)
    return pl.pallas_call(
        body, name="s5_scan", in_specs=[vm, vm, vm], out_specs=vm,
        out_shape=jax.ShapeDtypeStruct(sloc.shape, F32),
        compiler_params=pltpu.CompilerParams(vmem_limit_bytes=VMEM_LIMIT))(sloc, m1, m2)


def _s5_scan_bwd(dsprev, sprev, m1, m2):
    nc, r, w = dsprev.shape

    def body(d_ref, s_ref, m1_ref, m2_ref, g_ref, p1_ref, p2_ref):
        a1, a2 = m1_ref[...], m2_ref[...]
        zero = jnp.zeros((r, w), F32)

        def step(i, carry):
            gp, p1, p2 = carry
            c = nc - 2 - i
            g_ref[c] = gp
            sp = s_ref[c]
            p1 = p1 + gp * sp
            p2 = p2 + gp * pltpu.roll(sp, S5_STATE, 1)
            return d_ref[c] + a1 * gp - a2 * pltpu.roll(gp, S5_STATE, 1), p1, p2

        g_ref[nc - 1] = zero
        _, p1, p2 = lax.fori_loop(0, nc - 1, step, (d_ref[nc - 1], zero, zero))
        p1_ref[...] = p1
        p2_ref[...] = p2

    vm = pl.BlockSpec(memory_space=pltpu.VMEM)
    sd = jax.ShapeDtypeStruct
    return pl.pallas_call(
        body, name="s5_scan_bwd", in_specs=[vm, vm, vm, vm], out_specs=[vm, vm, vm],
        out_shape=[sd(dsprev.shape, F32), sd((r, w), F32), sd((r, w), F32)],
        compiler_params=pltpu.CompilerParams(vmem_limit_bytes=VMEM_LIMIT))(dsprev, sprev, m1, m2)


def _to_groups(u, bl):
    t, d = u.shape
    g = d // S5_GROUP
    return u.reshape(t // S5_CHUNK, S5_CHUNK, g, S5_GROUP).transpose(2, 0, 1, 3).reshape(g, t // S5_CHUNK, S5_CHUNK * S5_GROUP)


def _from_groups(y):
    g, nct, _ = y.shape
    return y.reshape(g, nct, S5_CHUNK, S5_GROUP).transpose(1, 2, 0, 3).reshape(nct * S5_CHUNK, g * S5_GROUP)


def _to_scan(s, bl):
    g, nct, w = s.shape
    return s.reshape(g, bl, nct // bl, w).transpose(2, 1, 0, 3).reshape(nct // bl, bl * g, w)


def _from_scan(s, bl):
    nc, r, w = s.shape
    g = r // bl
    return s.reshape(nc, bl, g, w).transpose(2, 1, 0, 3).reshape(g, bl * nc, w)


def _gelu_tanh_parts(y):
    c0 = math.sqrt(2.0 / math.pi)
    inner = c0 * (y + 0.044715 * y * y * y)
    th = jnp.tanh(inner)
    return th, c0 * (1.0 + 3 * 0.044715 * y * y)


def _s5_fwd(h, g, ops, d_skip, w_in, w_glu, bl):
    wcat, wo, m1, m2 = ops
    t, d = h.shape
    ch = S5_CHUNK * S5_GROUP
    hn = _rmsnorm("mix_norm", h, g)
    u = _mm_rs("s5_in", hn, "flat", w_in, 0)
    ug = _to_groups(u.astype(BF16), bl)
    x = _bmm("s5_chunk_in", ug, wcat.astype(BF16), NN, F32)
    sprev = _s5_scan_fwd(_to_scan(x[:, :, ch:], bl), jnp.tile(m1, (bl, 1)), jnp.tile(m2, (bl, 1)))
    sprev_g = _from_scan(sprev, bl).astype(BF16)
    y_state = _bmm("s5_chunk_out", sprev_g, wo.astype(BF16), NN, F32)
    y = _from_groups(x[:, :, :ch] + y_state)

    def fn(yy, uu, dd):
        y2 = yy + dd * uu
        th, _ = _gelu_tanh_parts(y2)
        return [0.5 * y2 * (1.0 + th)], []
    z = _rows("s5_gelu", fn, [y, u, d_skip], [(d, BF16)])[0][0]
    zz = _mm_cs("s5_glu", z, w_glu, 0, "flat", F32)

    def glu(hh, zv):
        return [hh + zv[:, :d] * _sigmoid(zv[:, d:])], []
    out = _rows("s5_glu_mix", glu, [h, zz], [(d, F32)])[0][0]
    return out, (h, hn, u, ug, sprev, sprev_g, y, z, zz)


def _s5_bwd(dout, saved, g, ops, d_skip, w_in, w_glu, bl):
    h, hn, u, ug, sprev, sprev_g, y, z, zz = saved
    wcat, wo, m1, m2 = ops
    t, d = h.shape
    ch = S5_CHUNK * S5_GROUP

    def glu_bwd(do, zv):
        sg = _sigmoid(zv[:, d:])
        return [jnp.concatenate([do * sg, do * zv[:, :d] * sg * (1.0 - sg)], axis=1)], []
    dzz = _rows("s5_glu_bwd", glu_bwd, [dout, zz], [(2 * d, BF16)])[0][0]
    dwglu = _mm_dw("s5_dwglu", z, None, dzz, "flat", (None, 0, 1))
    dz = _mm_cs_dx("s5_glu_dx", [(dzz, w_glu)], "flat", 0)

    def gelu_bwd(dzv, yy, uu, dd):
        y2 = yy + dd * uu
        th, dinner = _gelu_tanh_parts(y2)
        dy2 = dzv * (0.5 * (1.0 + th) + 0.5 * y2 * (1.0 - th * th) * dinner)
        return [dy2, dy2 * dd], [jnp.sum(dy2 * uu, axis=0, keepdims=True)]
    (dy_b, du_skip), (dd,) = _rows("s5_gelu_bwd", gelu_bwd, [dz, y, u, d_skip], [(d, BF16), (d, F32)], [(1, d)])
    dyg = _to_groups(dy_b, bl)
    wo_b, wcat_b = wo.astype(BF16), wcat.astype(BF16)
    dsprev = _bmm("s5_chunk_out_dx", dyg, wo_b, NT, F32)
    m1t, m2t = jnp.tile(m1, (bl, 1)), jnp.tile(m2, (bl, 1))
    dsloc, p1, p2 = _s5_scan_bwd(_to_scan(dsprev, bl), sprev, m1t, m2t)
    dcat = jnp.concatenate([dyg, _from_scan(dsloc, bl).astype(BF16)], axis=2)
    dug = _bmm("s5_chunk_in_dx", dcat, wcat_b, NT, F32)
    dwcat = _bmm("s5_chunk_in_dw", ug, dcat, TN, F32)
    dwo = _bmm("s5_chunk_out_dw", sprev_g, dyg, TN, F32)
    gcount = d // S5_GROUP
    dm1 = p1.reshape(bl, gcount, 2 * S5_STATE).sum(axis=0)
    dm2 = p2.reshape(bl, gcount, 2 * S5_STATE).sum(axis=0)
    du = (_from_groups(dug) + du_skip).astype(BF16)
    dwin = _mm_dw("s5_dwin", hn, "flat", du, None, (None, 0, 1))
    dhn = _mm_rs_dx("s5_in_dx", du, w_in, 0, "flat", F32)
    dh, dg = _rmsnorm_bwd("mix_norm_bwd", dout, dhn, h, g)
    return dh, dg, dwin, dwglu, dd, (dwcat, dwo, dm1, dm2)


def _sb_block(qi, idx, tb):
    kb = qi - idx
    return pl.multiple_of(jnp.maximum(kb, 0) * tb, tb), idx == 0, kb >= 0


def _sb_scores(q, kblk, diag, exists, row, col):
    z = lax.dot_general(q, kblk, NT, preferred_element_type=F32) * (SB_HEAD_DIM ** -0.5)
    l1 = jnp.log(1.0 + jnp.exp(-jnp.abs(z)))
    ls = jnp.minimum(z, 0.0) - l1
    mask = jnp.logical_and(jnp.logical_or(col < row, jnp.logical_not(diag)), exists)
    lk = jnp.where(mask, ls - z, 0.0)
    return ls, lk, mask


def _sb_more(qi, carry):
    j, cr = carry[0], carry[2]
    return jnp.logical_and(j <= qi, jnp.max(cr) > SB_CUT)


def _split_dot(v, tri):
    hi = v.astype(BF16)
    lo = (v - hi.astype(F32)).astype(BF16)
    return (jnp.dot(hi, tri, preferred_element_type=F32) + jnp.dot(lo, tri, preferred_element_type=F32))


def _sb_attn_fwd(q, k, v):
    bh, l, dh = q.shape
    tb = min(SB_BLOCK, l)
    nq = l // tb

    def body(q_ref, k_ref, v_ref, o_ref):
        qi = pl.program_id(1)
        qv = q_ref[...]
        row = lax.broadcasted_iota(jnp.int32, (tb, tb), 0)
        col = lax.broadcasted_iota(jnp.int32, (tb, tb), 1)
        tri = (row > col).astype(BF16)

        def step(carry):
            j, acc, cr = carry
            where = [_sb_block(qi, j + u, tb) for u in range(SB_UNROLL)]
            scores = [_sb_scores(qv, k_ref[pl.ds(ks, tb), :], diag, exists, row, col) for ks, diag, exists in where]
            laters = [_split_dot(lk, tri) for _, lk, _ in scores]
            for (ks, _, _), (ls, lk, mask), later in zip(where, scores, laters):
                att = jnp.where(mask, jnp.exp(ls + later + cr), 0.0)
                acc = acc + jnp.dot(att.astype(BF16), v_ref[pl.ds(ks, tb), :], preferred_element_type=F32)
                cr = cr + jnp.sum(lk, axis=1, keepdims=True)
            return j + SB_UNROLL, acc, cr

        _, acc, _ = lax.while_loop(functools.partial(_sb_more, qi), step,
                                   (jnp.int32(0), jnp.zeros((tb, dh), F32), jnp.zeros((tb, 1), F32)))
        o_ref[...] = acc

    blk = pl.BlockSpec((None, tb, dh), lambda b, i: (b, i, 0))
    full = pl.BlockSpec((None, l, dh), lambda b, i: (b, 0, 0))
    return pl.pallas_call(
        body, name="sb_attn", grid=(bh, nq), in_specs=[blk, full, full], out_specs=blk,
        out_shape=jax.ShapeDtypeStruct((bh, l, dh), F32), compiler_params=_params("parallel", "parallel"))(q, k, v)


def _sb_attn_bwd(q, k, v, o, do):
    bh, l, dh = q.shape
    tb = min(SB_BLOCK, l)
    nq = l // tb
    scale = SB_HEAD_DIM ** -0.5

    def body(q_ref, k_ref, v_ref, o_ref, do_ref, dq_ref, dk_ref, dv_ref):
        qi = pl.program_id(1)

        @pl.when(qi == 0)
        def _():
            dk_ref[...] = jnp.zeros_like(dk_ref)
            dv_ref[...] = jnp.zeros_like(dv_ref)

        qv = q_ref[...]
        dob = do_ref[...].astype(BF16)
        dsum = jnp.sum(dob.astype(F32) * o_ref[...], axis=1, keepdims=True)
        row = lax.broadcasted_iota(jnp.int32, (tb, tb), 0)
        col = lax.broadcasted_iota(jnp.int32, (tb, tb), 1)
        tri = (row > col).astype(BF16)
        tri_inc = (row >= col).astype(BF16)

        def step(carry):
            j, dq, cr, ce = carry
            n = range(SB_UNROLL)
            where = [_sb_block(qi, j + u, tb) for u in n]
            rows = [pl.ds(ks, tb) for ks, _, _ in where]
            scores = [_sb_scores(qv, k_ref[rows[u], :], where[u][1], where[u][2], row, col) for u in n]
            laters = [_split_dot(lk, tri) for _, lk, _ in scores]
            datts = [lax.dot_general(dob, v_ref[rows[u], :], NT, preferred_element_type=F32) for u in n]
            atts = []
            for (ls, lk, mask), later in zip(scores, laters):
                atts.append(jnp.where(mask, jnp.exp(ls + later + cr), 0.0).astype(BF16))
                cr = cr + jnp.sum(lk, axis=1, keepdims=True)
            es = [atts[u].astype(F32) * datts[u] for u in n]
            sufs = [_split_dot(e, tri_inc) for e in es]
            dzs = []
            for (ls, _, mask), e, suf in zip(scores, es, sufs):
                pre = dsum - ce - suf
                sg = jnp.exp(ls)
                dzs.append((jnp.where(mask, e * (1.0 - sg) - pre * sg, 0.0) * scale).astype(BF16))
                ce = ce + jnp.sum(e, axis=1, keepdims=True)
            for u in n:
                dq = dq + jnp.dot(dzs[u], k_ref[rows[u], :], preferred_element_type=F32)
                dk_ref[rows[u], :] += lax.dot_general(dzs[u], qv, TN, preferred_element_type=F32)
                dv_ref[rows[u], :] += lax.dot_general(atts[u], dob, TN, preferred_element_type=F32)
            return j + SB_UNROLL, dq, cr, ce

        zc = jnp.zeros((tb, 1), F32)
        _, dq, _, _ = lax.while_loop(functools.partial(_sb_more, qi), step,
                                     (jnp.int32(0), jnp.zeros((tb, dh), F32), zc, zc))
        dq_ref[...] = dq

    blk = pl.BlockSpec((None, tb, dh), lambda b, i: (b, i, 0))
    full = pl.BlockSpec((None, l, dh), lambda b, i: (b, 0, 0))
    sd = jax.ShapeDtypeStruct((bh, l, dh), F32)
    return pl.pallas_call(
        body, name="sb_attn_bwd", grid=(bh, nq), in_specs=[blk, full, full, blk, blk], out_specs=[blk, full, full],
        out_shape=[sd, sd, sd], compiler_params=_params("parallel", "arbitrary"))(q, k, v, o, do)


SB_PAIR = 2 * SB_HEAD_DIM


def _pair_masks(tb):
    lane = lax.broadcasted_iota(jnp.int32, (1, SB_PAIR), 1)
    row = lax.broadcasted_iota(jnp.int32, (tb, tb), 0)
    col = lax.broadcasted_iota(jnp.int32, (tb, tb), 1)
    return [lane < SB_HEAD_DIM, lane >= SB_HEAD_DIM], row, col


def _pair_more(qi, carry):
    j, crs = carry[0], carry[2]
    return jnp.logical_and(j <= qi, jnp.maximum(jnp.max(crs[0]), jnp.max(crs[1])) > SB_CUT)


def _pair_specs(bl, l, d, tb):
    nq, off = l // tb, d // SB_PAIR
    qspec = pl.BlockSpec((tb, SB_PAIR), lambda b, p, i: (b * nq + i, p))
    kspec = pl.BlockSpec((l, SB_PAIR), lambda b, p, i: (b, off + p))
    vspec = pl.BlockSpec((l, SB_PAIR), lambda b, p, i: (b, 2 * off + p))
    return qspec, kspec, vspec


def _sb_attn_fwd2(qkv, bl):
    t, d3 = qkv.shape
    d, l = d3 // 3, t // bl
    tb = min(SB_BLOCK, l)
    nq = l // tb

    def body(q_ref, k_ref, v_ref, o_ref, ob_ref):
        qi = pl.program_id(2)
        heads, row, col = _pair_masks(tb)
        qv = q_ref[...]
        qh = [jnp.where(m, qv, jnp.zeros_like(qv)) for m in heads]
        tri = (row > col).astype(BF16)

        def step(carry):
            j, acc, crs = carry
            crs = list(crs)
            where = [_sb_block(qi, j + u, tb) for u in range(SB_UNROLL)]
            kblks = [k_ref[pl.ds(ks, tb), :] for ks, _, _ in where]
            scores = [[_sb_scores(qh[hd], kblks[u], where[u][1], where[u][2], row, col) for hd in range(2)]
                      for u in range(SB_UNROLL)]
            laters = [[_split_dot(sc[1], tri) for sc in su] for su in scores]
            for u in range(SB_UNROLL):
                vblk = v_ref[pl.ds(where[u][0], tb), :]
                outs = []
                for hd in range(2):
                    ls, lk, mask = scores[u][hd]
                    att = jnp.where(mask, jnp.exp(ls + laters[u][hd] + crs[hd]), 0.0)
                    outs.append(jnp.dot(att.astype(BF16), vblk, preferred_element_type=F32))
                    crs[hd] = crs[hd] + jnp.sum(lk, axis=1, keepdims=True)
                acc = acc + jnp.where(heads[0], outs[0], outs[1])
            return j + SB_UNROLL, acc, tuple(crs)

        zc = jnp.zeros((tb, 1), F32)
        _, acc, _ = lax.while_loop(functools.partial(_pair_more, qi), step,
                                   (jnp.int32(0), jnp.zeros((tb, SB_PAIR), F32), (zc, zc)))
        o_ref[...] = acc
        ob_ref[...] = acc.astype(BF16)

    qspec, kspec, vspec = _pair_specs(bl, l, d, tb)
    return pl.pallas_call(
        body, name="sb_attn", grid=(bl, d // SB_PAIR, nq), in_specs=[qspec, kspec, vspec], out_specs=[qspec, qspec],
        out_shape=[jax.ShapeDtypeStruct((t, d), F32), jax.ShapeDtypeStruct((t, d), BF16)],
        compiler_params=_params("parallel", "parallel", "parallel"))(qkv, qkv, qkv)


def _sb_attn_bwd2(qkv, o, do, bl):
    t, d3 = qkv.shape
    d, l = d3 // 3, t // bl
    tb = min(SB_BLOCK, l)
    nq = l // tb
    scale = SB_HEAD_DIM ** -0.5

    def body(q_ref, k_ref, v_ref, o_ref, do_ref, dq_ref, dk_ref, dv_ref, dk_acc, dv_acc):
        qi = pl.program_id(2)

        @pl.when(qi == 0)
        def _():
            dk_acc[...] = jnp.zeros_like(dk_acc)
            dv_acc[...] = jnp.zeros_like(dv_acc)

        heads, row, col = _pair_masks(tb)
        qv = q_ref[...]
        dov = do_ref[...].astype(BF16)
        qh = [jnp.where(m, qv, jnp.zeros_like(qv)) for m in heads]
        doh = [jnp.where(m, dov, jnp.zeros_like(dov)) for m in heads]
        ov = o_ref[...]
        dsum = [jnp.sum(dh.astype(F32) * ov, axis=1, keepdims=True) for dh in doh]
        tri = (row > col).astype(BF16)
        tri_inc = (row >= col).astype(BF16)

        def step(carry):
            j, dq, crs, ces = carry
            crs, ces = list(crs), list(ces)
            n = range(SB_UNROLL)
            where = [_sb_block(qi, j + u, tb) for u in n]
            rows = [pl.ds(ks, tb) for ks, _, _ in where]
            kblks = [k_ref[rows[u], :] for u in n]
            vblks = [v_ref[rows[u], :] for u in n]
            scores = [[_sb_scores(qh[hd], kblks[u], where[u][1], where[u][2], row, col) for hd in range(2)] for u in n]
            laters = [[_split_dot(sc[1], tri) for sc in su] for su in scores]
            datts = [[lax.dot_general(doh[hd], vblks[u], NT, preferred_element_type=F32) for hd in range(2)] for u in n]
            atts = [[None, None] for _ in n]
            for u in n:
                for hd in range(2):
                    ls, lk, mask = scores[u][hd]
                    atts[u][hd] = jnp.where(mask, jnp.exp(ls + laters[u][hd] + crs[hd]), 0.0).astype(BF16)
                    crs[hd] = crs[hd] + jnp.sum(lk, axis=1, keepdims=True)
            es = [[atts[u][hd].astype(F32) * datts[u][hd] for hd in range(2)] for u in n]
            sufs = [[_split_dot(e, tri_inc) for e in eu] for eu in es]
            dzs = [[None, None] for _ in n]
            for u in n:
                for hd in range(2):
                    ls, _, mask = scores[u][hd]
                    pre = dsum[hd] - ces[hd] - sufs[u][hd]
                    sg = jnp.exp(ls)
                    dzs[u][hd] = (jnp.where(mask, es[u][hd] * (1.0 - sg) - pre * sg, 0.0) * scale).astype(BF16)
                    ces[hd] = ces[hd] + jnp.sum(es[u][hd], axis=1, keepdims=True)
            for u in n:
                dq = dq + jnp.where(heads[0], jnp.dot(dzs[u][0], kblks[u], preferred_element_type=F32),
                                    jnp.dot(dzs[u][1], kblks[u], preferred_element_type=F32))
                dk_acc[rows[u], :] += (lax.dot_general(dzs[u][0], qh[0], TN, preferred_element_type=F32)
                                       + lax.dot_general(dzs[u][1], qh[1], TN, preferred_element_type=F32))
                dv_acc[rows[u], :] += (lax.dot_general(atts[u][0], doh[0], TN, preferred_element_type=F32)
                                       + lax.dot_general(atts[u][1], doh[1], TN, preferred_element_type=F32))
            return j + SB_UNROLL, dq, tuple(crs), tuple(ces)

        zc = jnp.zeros((tb, 1), F32)
        _, dq, _, _ = lax.while_loop(functools.partial(_pair_more, qi), step,
                                     (jnp.int32(0), jnp.zeros((tb, SB_PAIR), F32), (zc, zc), (zc, zc)))
        dq_ref[...] = dq.astype(BF16)

        @pl.when(qi == nq - 1)
        def _():
            dk_ref[...] = dk_acc[...].astype(BF16)
            dv_ref[...] = dv_acc[...].astype(BF16)

    qspec, kspec, vspec = _pair_specs(bl, l, d, tb)
    blk = pl.BlockSpec((tb, SB_PAIR), lambda b, p, i: (b * nq + i, p))
    full = pl.BlockSpec((l, SB_PAIR), lambda b, p, i: (b, p))
    sd = jax.ShapeDtypeStruct((t, d), BF16)
    dq, dk, dv = pl.pallas_call(
        body, name="sb_attn_bwd", grid=(bl, d // SB_PAIR, nq), in_specs=[qspec, kspec, vspec, blk, blk],
        out_specs=[blk, full, full], out_shape=[sd, sd, sd],
        scratch_shapes=[pltpu.VMEM((l, SB_PAIR), F32), pltpu.VMEM((l, SB_PAIR), F32)],
        compiler_params=_params("parallel", "parallel", "arbitrary"))(qkv, qkv, qkv, o, do)
    return jnp.concatenate([dq, dk, dv], axis=1)


def _to_heads(x, bl):
    t, w = x.shape
    heads = w // SB_HEAD_DIM
    l = t // bl
    return x.reshape(bl, l, heads, SB_HEAD_DIM).transpose(0, 2, 1, 3).reshape(bl * heads, l, SB_HEAD_DIM)


def _from_heads(x, bl):
    bh, l, dh = x.shape
    heads = bh // bl
    return x.reshape(bl, heads, l, dh).transpose(0, 2, 1, 3).reshape(bl * l, heads * dh)


def _sb_fwd(h, g, w_qkv, w_o, bl):
    t, d = h.shape
    hn = _rmsnorm("mix_norm", h, g)
    qkv = _mm_cs("sb_qkv", hn, w_qkv, 0, "flat", BF16)
    o, ob = _sb_attn_fwd2(qkv, bl)
    out = _mm_rs("sb_out", ob, "flat", w_o, 0, res=h)
    return out, (h, hn, qkv, o, ob)


def _sb_bwd(dout, saved, g, w_qkv, w_o, bl):
    h, hn, qkv, o, ob = saved
    dob = dout.astype(BF16)
    dwo = _mm_dw("sb_dwo", ob, "flat", dob, None, (None, 0, 1))
    do = _mm_rs_dx("sb_out_dx", dob, w_o, 0, "flat", F32)
    dqkv = _sb_attn_bwd2(qkv, o, do, bl)
    dwqkv = _mm_dw("sb_dwqkv", hn, None, dqkv, "flat", (None, 0, 1))
    dhn = _mm_cs_dx("sb_qkv_dx", [(dqkv, w_qkv)], "flat", 0)
    dh, dg = _rmsnorm_bwd("mix_norm_bwd", dout, dhn, h, g)
    return dh, dg, dwqkv, dwo


def _adamw_update(wv, gr, mv, vv):
    c1 = 1.0 / (1.0 - ADAM_B1 ** ADAM_STEP)
    c2 = 1.0 / (1.0 - ADAM_B2 ** ADAM_STEP)
    mn = ADAM_B1 * mv + (1.0 - ADAM_B1) * gr
    vn = ADAM_B2 * vv + (1.0 - ADAM_B2) * gr * gr
    delta = -ADAM_LR * ((mn * c1) / (jnp.sqrt(vn * c2) + ADAM_EPS) + ADAM_WD * wv)
    return delta, mn, vn


def _adamw_small(w, gr, m, v):
    def fn(wv, gv, mv, vv):
        return list(_adamw_update(wv, gv, mv, vv)), []
    return _rows("adamw_small", fn, [w, gr, m, v], [(w.shape[1], F32)] * 3)[0]


def _place():
    x, y, c = lax.axis_index("x"), lax.axis_index("y"), lax.axis_index("c")
    chips = [(1 - x, y), (x, 1 - y), (1 - x, 1 - y)]
    return x, y, c, chips


def _remote(src, dst, send_sem, recv_sem, to):
    return pltpu.make_async_remote_copy(src_ref=src, dst_ref=dst, send_sem=send_sem, recv_sem=recv_sem,
                                        device_id=to, device_id_type=MESH)


def _half(ref, c, rh, lead):
    return ref.at[(slice(None),) * lead + (pl.ds(c * rh, rh),)]


def _allgather_weights(ws):
    n = len(ws)

    def body(*refs):
        ins, outs = refs[:n], refs[n:2 * n]
        send, recv = refs[2 * n:]
        x, y, c, chips = _place()
        own = 2 * x + y
        sibling = (x, y, 1 - c)
        sent = []
        for t in range(n):
            rh = ws[t].shape[1] // 2
            for j, chip in enumerate(chips):
                cp = _remote(_half(ins[t], c, rh, 1), _half(outs[t].at[own], c, rh, 1), send.at[t, j], recv.at[t, j], (*chip, c))
                cp.start()
                sent.append(cp)
        for t in range(n):
            rh = ws[t].shape[1] // 2
            for j, chip in enumerate(chips):
                landed = _half(outs[t].at[2 * chip[0] + chip[1]], c, rh, 1)
                _remote(landed, landed, send.at[t, j], recv.at[t, j], (*chip, c)).wait_recv()
                cp = _remote(landed, landed, send.at[t, 3 + j], recv.at[t, 3 + j], sibling)
                cp.start()
                sent.append(cp)
        for t in range(n):
            rh = ws[t].shape[1] // 2
            for j, chip in enumerate(chips):
                passed = _half(outs[t].at[2 * chip[0] + chip[1]], 1 - c, rh, 1)
                _remote(passed, passed, send.at[t, 3 + j], recv.at[t, 3 + j], sibling).wait_recv()
        for cp in sent:
            cp.wait_send()

    res = pl.pallas_call(
        body, name="allgather_weights", in_specs=[ANY] * n, out_specs=[ANY] * n,
        out_shape=[jax.ShapeDtypeStruct((N_CHIPS,) + w.shape, w.dtype) for w in ws],
        scratch_shapes=[pltpu.SemaphoreType.DMA((n, 6)), pltpu.SemaphoreType.DMA((n, 6))],
    )(*ws)
    own = 2 * lax.axis_index("x") + lax.axis_index("y")
    return [lax.dynamic_update_slice(g, w[None], (own, 0, 0, 0)) for g, w in zip(res, ws)]


def _pair_exchange(gs):
    n = len(gs)

    def body(*refs):
        ins, outs = refs[:n], refs[n:2 * n]
        send, recv = refs[2 * n:]
        x, y, c, _ = _place()
        copies = [_remote(_half(ins[t], 1 - c, gs[t].shape[2] // 2, 2), outs[t], send.at[t], recv.at[t], (x, y, 1 - c))
                  for t in range(n)]
        for cp in copies:
            cp.start()
        for cp in copies:
            cp.wait()

    return pl.pallas_call(
        body, name="grad_pair_exchange", in_specs=[ANY] * n, out_specs=[ANY] * n,
        out_shape=[jax.ShapeDtypeStruct(g.shape[:2] + (g.shape[2] // 2, g.shape[3]), F32) for g in gs],
        scratch_shapes=[pltpu.SemaphoreType.DMA((n,)), pltpu.SemaphoreType.DMA((n,))],
    )(*gs)


def _pair_sum(g, theirs, c_idx):
    n4, ly, r, cc = g.shape
    rh = r // 2
    tm = _tile(rh, 256)
    nt = rh // tm

    def body(c_ref, g_ref, t_ref, o_ref):
        o_ref[...] = (g_ref[...] + t_ref[...]).astype(o_ref.dtype)

    blk = (None, tm, cc)
    grid_spec = pltpu.PrefetchScalarGridSpec(
        num_scalar_prefetch=1, grid=(n4 * ly, nt),
        in_specs=[pl.BlockSpec(blk, lambda a, i, cr: (a, cr[0] * nt + i, 0)), pl.BlockSpec(blk, lambda a, i, cr: (a, i, 0))],
        out_specs=pl.BlockSpec(blk, lambda a, i, cr: (a, i, 0)))
    out = pl.pallas_call(
        body, name="grad_pair_sum", grid_spec=grid_spec, out_shape=jax.ShapeDtypeStruct((n4 * ly, rh, cc), BF16),
        compiler_params=_params("parallel", "parallel"))(c_idx, g.reshape(n4 * ly, r, cc), theirs.reshape(n4 * ly, rh, cc))
    return out.reshape(n4, ly, rh, cc)


def _chip_exchange(ps):
    n = len(ps)

    def body(*refs):
        ins, outs = refs[:n], refs[n:2 * n]
        send, recv = refs[2 * n:]
        x, y, c, chips = _place()
        copies = []
        for t in range(n):
            for j, chip in enumerate(chips):
                copies.append(_remote(ins[t].at[2 * chip[0] + chip[1]], outs[t].at[j], send.at[t, j], recv.at[t, j], (*chip, c)))
        for cp in copies:
            cp.start()
        for cp in copies:
            cp.wait()

    return pl.pallas_call(
        body, name="grad_chip_exchange", in_specs=[ANY] * n, out_specs=[ANY] * n,
        out_shape=[jax.ShapeDtypeStruct((3,) + p.shape[1:], p.dtype) for p in ps],
        scratch_shapes=[pltpu.SemaphoreType.DMA((n, 3)), pltpu.SemaphoreType.DMA((n, 3))],
    )(*ps)


def _chip_sum(p, landed, own_idx):
    _, ly, rh, cc = p.shape
    tm = _tile(rh, 256)

    def body(o_ref, p_ref, a_ref, b_ref, c_ref, out_ref):
        up = lambda r: r[...].astype(F32)
        out_ref[...] = ((up(p_ref) + up(a_ref)) + up(b_ref)) + up(c_ref)

    blk = (None, None, tm, cc)
    slot = lambda j: pl.BlockSpec(blk, lambda l, i, o: (j, l, i, 0))
    grid_spec = pltpu.PrefetchScalarGridSpec(
        num_scalar_prefetch=1, grid=(ly, rh // tm),
        in_specs=[pl.BlockSpec(blk, lambda l, i, o: (o[0], l, i, 0)), slot(0), slot(1), slot(2)],
        out_specs=pl.BlockSpec((None, tm, cc), lambda l, i, o: (l, i, 0)))
    return pl.pallas_call(
        body, name="grad_chip_sum", grid_spec=grid_spec, out_shape=jax.ShapeDtypeStruct((ly, rh, cc), F32),
        compiler_params=_params("parallel", "parallel"))(own_idx, p, landed, landed, landed)


def _pair_swap(halves):
    n = len(halves)

    def body(*refs):
        ins, outs = refs[:n], refs[n:2 * n]
        send, recv = refs[2 * n:]
        x, y, c, _ = _place()
        copies = [_remote(ins[t], outs[t], send.at[t], recv.at[t], (x, y, 1 - c)) for t in range(n)]
        for cp in copies:
            cp.start()
        for cp in copies:
            cp.wait()

    return pl.pallas_call(
        body, name="grad_pair_swap", in_specs=[ANY] * n, out_specs=[ANY] * n,
        out_shape=[jax.ShapeDtypeStruct(h.shape, F32) for h in halves],
        scratch_shapes=[pltpu.SemaphoreType.DMA((n,)), pltpu.SemaphoreType.DMA((n,))],
    )(*halves)


def _adamw_big(w, m, v, mine, theirs, c_idx):
    ly, r, cc = w.shape
    rh = r // 2
    tm = _tile(rh, 256)
    nt = rh // tm

    def body(c_ref, w_ref, m_ref, v_ref, a_ref, b_ref, g_out, d_out, m_out, v_out):
        gr = jnp.where(pl.program_id(1) == c_ref[0], a_ref[...], b_ref[...])
        delta, mn, vn = _adamw_update(w_ref[...], gr, m_ref[...], v_ref[...])
        g_out[...] = gr
        d_out[...] = delta
        m_out[...] = mn
        v_out[...] = vn

    blk = (None, tm, cc)
    full = pl.BlockSpec(blk, lambda l, hc, i, cr: (l, hc * nt + i, 0))
    half = pl.BlockSpec(blk, lambda l, hc, i, cr: (l, i, 0))
    grid_spec = pltpu.PrefetchScalarGridSpec(
        num_scalar_prefetch=1, grid=(ly, 2, nt), in_specs=[full, full, full, half, half], out_specs=[full] * 4)
    sd = jax.ShapeDtypeStruct(w.shape, F32)
    return pl.pallas_call(
        body, name="adamw", grid_spec=grid_spec, out_shape=[sd] * 4,
        compiler_params=_params("parallel", "parallel", "parallel"))(c_idx, w, m, v, mine, theirs)


def _allreduce_small(v):
    rows, w = v.shape

    def body(x_ref, sum_ref, all_ref, send, recv, local):
        x, y, c, chips = _place()
        me, sibling = (x, y, c), (x, y, 1 - c)

        def slot(px, py, pc):
            return all_ref.at[4 * px + 2 * py + pc]

        def copy(k, block, to, src=None):
            return _remote(slot(*block) if src is None else src, slot(*block), send.at[k], recv.at[k], to)

        mine = pltpu.make_async_copy(x_ref, slot(*me), local)
        mine.start()
        first = [copy(0, me, sibling, src=x_ref)]
        first += [copy(1 + j, me, (*chip, c), src=x_ref) for j, chip in enumerate(chips)]
        for cp in first:
            cp.start()
        passed = [copy(4 + j, (*chip, c), sibling) for j, chip in enumerate(chips)]
        for j, chip in enumerate(chips):
            copy(1 + j, (*chip, c), me).wait_recv()
            passed[j].start()
        copy(0, sibling, me).wait_recv()
        for j, chip in enumerate(chips):
            copy(4 + j, (*chip, 1 - c), me).wait_recv()
        for cp in first + passed:
            cp.wait_send()
        mine.wait()
        tot = all_ref[0]
        for k in range(1, N_DEV):
            tot = tot + all_ref[k]
        sum_ref[...] = tot

    vm = pl.BlockSpec(memory_space=pltpu.VMEM)
    return pl.pallas_call(
        body, name="allreduce_small", in_specs=[vm], out_specs=[vm, vm],
        out_shape=[jax.ShapeDtypeStruct((rows, w), F32), jax.ShapeDtypeStruct((N_DEV, rows, w), F32)],
        scratch_shapes=[pltpu.SemaphoreType.DMA((7,)), pltpu.SemaphoreType.DMA((7,)), pltpu.SemaphoreType.DMA],
        compiler_params=pltpu.CompilerParams(vmem_limit_bytes=VMEM_LIMIT),
    )(v)[0]


BIG = ["ffn1_w1", "ffn1_w3", "ffn1_w2", "ffn2_w1", "ffn2_w3", "ffn2_w2", "ple_proj", "ple_gate",
       "s5_w_in", "s5_w_glu", "sb_w_qkv", "sb_w_o"]
SMALL = ["ffn1_norm", "mix_norm", "ffn2_norm", "ple_norm", "s5_a_re", "s5_a_im", "s5_log_dt", "s5_b_re", "s5_b_im",
         "s5_c_re", "s5_c_im", "s5_d", "final_norm"]
ORDER = ["ffn1_norm", "ffn1_w1", "ffn1_w3", "ffn1_w2", "mix_norm", "ffn2_norm", "ffn2_w1", "ffn2_w3", "ffn2_w2",
         "ple_norm", "ple_proj", "ple_gate", "s5_w_in", "s5_a_re", "s5_a_im", "s5_log_dt", "s5_b_re", "s5_b_im",
         "s5_c_re", "s5_c_im", "s5_d", "s5_w_glu", "sb_w_qkv", "sb_w_o", "final_norm"]


def _pack(arrays):
    flat = jnp.concatenate([a.reshape(-1) for a in arrays])
    pad = (-flat.shape[0]) % 1024
    return jnp.pad(flat, (0, pad)).reshape(-1, 128)


def _unpack(packed, like):
    flat = packed.reshape(-1)
    out, off = [], 0
    for a in like:
        out.append(flat[off:off + a.size].reshape(a.shape))
        off += a.size
    return out


def _fwd_bwd(x, p, target, w, gathered):
    bl, l, d = x.shape
    t = bl * l
    depth = w["ffn1_norm"].shape[0]
    s5_ops, s5_vjp = jax.vjp(_s5_prep, w["s5_a_re"][0], w["s5_a_im"][0], w["s5_log_dt"][0], w["s5_b_re"][0],
                             w["s5_b_im"][0], w["s5_c_re"][0], w["s5_c_im"][0])

    h = x.reshape(t, d)
    p2 = [p[i].reshape(t, p.shape[-1]).astype(BF16) for i in range(depth)]
    saved = []
    for i in range(depth):
        norm = lambda name: w[name][i:i + 1]
        h, s1 = _ffn_fwd(h, norm("ffn1_norm"), gathered["ffn1_w1"], gathered["ffn1_w3"], gathered["ffn1_w2"], i)
        if i % 2 == 0:
            h, s2 = _s5_fwd(h, norm("mix_norm"), s5_ops, w["s5_d"][i // 2:i // 2 + 1], gathered["s5_w_in"], gathered["s5_w_glu"], bl)
        else:
            h, s2 = _sb_fwd(h, norm("mix_norm"), gathered["sb_w_qkv"], gathered["sb_w_o"], bl)
        h, s3 = _ffn_fwd(h, norm("ffn2_norm"), gathered["ffn2_w1"], gathered["ffn2_w3"], gathered["ffn2_w2"], i)
        h, s4 = _ple_fwd(h, norm("ple_norm"), p2[i], gathered["ple_proj"], gathered["ple_gate"], i)
        saved.append((s1, s2, s3, s4))

    loss, dh, dfinal = _head(h, w["final_norm"].reshape(1, d), target.reshape(t, d))

    big = {k: None for k in BIG}
    small = {k: [None] * w[k].shape[0] if w[k].ndim > 1 else None for k in SMALL}
    small["final_norm"] = dfinal.reshape(d)
    for i in reversed(range(depth)):
        norm = lambda name: w[name][i:i + 1]
        slots = lambda *names: [(big[k], i, depth) for k in names]
        s1, s2, s3, s4 = saved[i]
        dh, dg, big["ple_proj"], big["ple_gate"] = _ple_bwd(
            dh, s4, norm("ple_norm"), p2[i], gathered["ple_proj"], gathered["ple_gate"], i, slots("ple_proj", "ple_gate"))
        small["ple_norm"][i] = dg[0]
        dh, dg, big["ffn2_w1"], big["ffn2_w3"], big["ffn2_w2"] = _ffn_bwd(
            dh, s3, norm("ffn2_norm"), gathered["ffn2_w1"], gathered["ffn2_w3"], gathered["ffn2_w2"], i,
            slots("ffn2_w1", "ffn2_w3", "ffn2_w2"))
        small["ffn2_norm"][i] = dg[0]
        if i % 2 == 0:
            dh, dg, big["s5_w_in"], big["s5_w_glu"], dd, dops = _s5_bwd(
                dh, s2, norm("mix_norm"), s5_ops, w["s5_d"][i // 2:i // 2 + 1], gathered["s5_w_in"], gathered["s5_w_glu"], bl)
            small["s5_d"][0] = dd[0]
            raw = s5_vjp(dops)
            for name, gr in zip(["s5_a_re", "s5_a_im", "s5_log_dt", "s5_b_re", "s5_b_im", "s5_c_re", "s5_c_im"], raw):
                small[name][0] = gr
        else:
            dh, dg, big["sb_w_qkv"], big["sb_w_o"] = _sb_bwd(dh, s2, norm("mix_norm"), gathered["sb_w_qkv"], gathered["sb_w_o"], bl)
        small["mix_norm"][i] = dg[0]
        dh, dg, big["ffn1_w1"], big["ffn1_w3"], big["ffn1_w2"] = _ffn_bwd(
            dh, s1, norm("ffn1_norm"), gathered["ffn1_w1"], gathered["ffn1_w3"], gathered["ffn1_w2"], i,
            slots("ffn1_w1", "ffn1_w3", "ffn1_w2"))
        small["ffn1_norm"][i] = dg[0]
    small_list = [jnp.stack(small[k]) if isinstance(small[k], list) else small[k] for k in SMALL]
    return loss, dh.reshape(bl, l, d), big, small_list


def _step(x, p, target, w, m, v):
    gathered = dict(zip(BIG, _allgather_weights([_to_bf16(w[k]) for k in BIG])))
    loss, grad_x, big, small_list = _fwd_bwd(x, p, target, w, gathered)

    c_idx = lax.axis_index("c").astype(jnp.int32).reshape(1)
    own_idx = (2 * lax.axis_index("x") + lax.axis_index("y")).astype(jnp.int32).reshape(1)
    partial = [big[k] for k in BIG]
    pair = [_pair_sum(g, t, c_idx) for g, t in zip(partial, _pair_exchange(partial))]
    mine = [_chip_sum(pr, ld, own_idx) for pr, ld in zip(pair, _chip_exchange(pair))]
    theirs = _pair_swap(mine)
    out_g, out_d, out_m, out_v = {}, {}, {}, {}
    for k, a, b in zip(BIG, mine, theirs):
        out_g[k], out_d[k], out_m[k], out_v[k] = _adamw_big(w[k], m[k], v[k], a, b, c_idx)

    like = [w[k] for k in SMALL]
    pad = [jnp.zeros((1,), F32)]
    g_small = _allreduce_small(_pack(small_list + [loss.reshape(1)]))
    packed = (g_small,) + tuple(_adamw_small(_pack(like + pad), g_small, _pack([m[k] for k in SMALL] + pad),
                                             _pack([v[k] for k in SMALL] + pad)))
    for dst, pk in zip((out_g, out_d, out_m, out_v), packed):
        dst.update(dict(zip(SMALL, _unpack(pk, like))))
    loss = g_small.reshape(-1)[sum(a.size for a in like)]
    return (loss, grad_x, *[out_g[k] for k in ORDER], *[out_d[k] for k in ORDER],
            *[out_m[k] for k in ORDER], *[out_v[k] for k in ORDER])


def kernel(x, p, ffn1_norm, ffn1_w1, ffn1_w3, ffn1_w2, mix_norm, ffn2_norm, ffn2_w1, ffn2_w3, ffn2_w2, ple_norm, ple_proj, ple_gate, s5_w_in, s5_a_re, s5_a_im, s5_log_dt, s5_b_re, s5_b_im, s5_c_re, s5_c_im, s5_d, s5_w_glu, sb_w_qkv, sb_w_o, final_norm, loss_target, m_ffn1_norm, m_ffn1_w1, m_ffn1_w3, m_ffn1_w2, m_mix_norm, m_ffn2_norm, m_ffn2_w1, m_ffn2_w3, m_ffn2_w2, m_ple_norm, m_ple_proj, m_ple_gate, m_s5_w_in, m_s5_a_re, m_s5_a_im, m_s5_log_dt, m_s5_b_re, m_s5_b_im, m_s5_c_re, m_s5_c_im, m_s5_d, m_s5_w_glu, m_sb_w_qkv, m_sb_w_o, m_final_norm, v_ffn1_norm, v_ffn1_w1, v_ffn1_w3, v_ffn1_w2, v_mix_norm, v_ffn2_norm, v_ffn2_w1, v_ffn2_w3, v_ffn2_w2, v_ple_norm, v_ple_proj, v_ple_gate, v_s5_w_in, v_s5_a_re, v_s5_a_im, v_s5_log_dt, v_s5_b_re, v_s5_b_im, v_s5_c_re, v_s5_c_im, v_s5_d, v_s5_w_glu, v_sb_w_qkv, v_sb_w_o, v_final_norm):
    args = dict(locals())
    w = {k: args[k] for k in ORDER}
    m = {k: args["m_" + k] for k in ORDER}
    v = {k: args["v_" + k] for k in ORDER}
    return _step(x, p, loss_target, w, m, v)
```

```python
import functools
import math

import jax
import jax.numpy as jnp
from jax import lax
from jax.experimental import pallas as pl
from jax.experimental.pallas import tpu as pltpu

F32 = jnp.float32
BF16 = jnp.bfloat16
MESH = pl.DeviceIdType.MESH

N_CHIPS = 4
N_DEV = 8
RMS_EPS = 1e-6
S5_GROUP = 16
S5_STATE = 64
S5_CHUNK = 16
SB_HEAD_DIM = 64
SB_BLOCK = 128
SB_CUT = -104.0
SB_UNROLL = 3
ADAM_LR, ADAM_B1, ADAM_B2, ADAM_EPS, ADAM_WD, ADAM_STEP = 0.001, 0.9, 0.999, 1e-08, 0.01, 10
VMEM_LIMIT = 48 * 1024 * 1024

NN = (((1,), (0,)), ((), ()))
NT = (((1,), (1,)), ((), ()))
TN = (((0,), (0,)), ((), ()))

ANY = pl.BlockSpec(memory_space=pl.ANY)


def _tile(n, target):
    if n <= target:
        return n
    for t in range(target - target % 8, 7, -8):
        if n % t == 0:
            return t
    raise ValueError(f"no row tile for {n}")


def _params(*semantics):
    return pltpu.CompilerParams(dimension_semantics=semantics, vmem_limit_bytes=VMEM_LIMIT)


def _sigmoid(v):
    return 1.0 / (1.0 + jnp.exp(-v))


def _gemm(name, grid, operands, in_specs, groups, acc_shapes, out_shapes, out_specs, epilogue, reduce_axis=None, aliases=None):
    n_in, n_out = len(operands), len(out_shapes)
    n_red = None if reduce_axis is None else grid[reduce_axis]

    def body(*refs):
        ins, outs, accs = refs[:n_in], refs[n_in:n_in + n_out], refs[n_in + n_out:]

        def products():
            res = []
            for terms in groups:
                tot = None
                for ia, ib, dims in terms:
                    d = lax.dot_general(ins[ia][...], ins[ib][...], dims, preferred_element_type=F32)
                    tot = d if tot is None else tot + d
                res.append(tot)
            return res

        def finish(vals):
            for o, v in zip(outs, epilogue(vals, ins)):
                o[...] = v.astype(o.dtype)

        if reduce_axis is None:
            finish(products())
        else:
            k = pl.program_id(reduce_axis)

            @pl.when(k == 0)
            def _():
                for a in accs:
                    a[...] = jnp.zeros_like(a)

            for a, d in zip(accs, products()):
                a[...] += d

            @pl.when(k == n_red - 1)
            def _():
                finish([a[...] for a in accs])

    scratch = [] if reduce_axis is None else [pltpu.VMEM(s, F32) for s in acc_shapes]
    sem = tuple("arbitrary" if i == reduce_axis else "parallel" for i in range(len(grid)))
    return pl.pallas_call(
        body, name=name, grid=grid, in_specs=in_specs, out_specs=out_specs, out_shape=out_shapes,
        scratch_shapes=scratch, input_output_aliases=aliases or {}, compiler_params=_params(*sem))(*operands)


def _ident(vals, ins):
    return vals


def _act_spec(layout, tm, cs, pos):
    if layout == "sm":
        return pl.BlockSpec((None, tm, cs), lambda *g: (pos(*g)[1], pos(*g)[0], 0))
    return pl.BlockSpec((tm, cs), lambda *g: pos(*g))


def _act_shape(layout, t, cs, dtype):
    return jax.ShapeDtypeStruct((N_CHIPS, t, cs) if layout == "sm" else (t, N_CHIPS * cs), dtype)


def _w_spec(w, layer, pos_k):
    _, _, r, c = w.shape
    return pl.BlockSpec((None, None, r, c), lambda *g: (pos_k(*g), layer, 0, 0))


def _mm_cs(name, x, w, layer, out_layout, out_dtype, tm=512):
    t, kd = x.shape
    cs = w.shape[3]
    tm = _tile(t, tm)
    return _gemm(
        name, (N_CHIPS, t // tm), [x, w],
        [pl.BlockSpec((tm, kd), lambda k, i: (i, 0)), _w_spec(w, layer, lambda k, i: k)],
        [[(0, 1, NN)]], None, [_act_shape(out_layout, t, cs, out_dtype)],
        [_act_spec(out_layout, tm, cs, lambda k, i: (i, k))], _ident)[0]


def _mm_rs(name, xs, layout, w, layer, res=None, alpha=1.0, out_dtype=F32, tm=1024):
    ks, n = w.shape[2], w.shape[3]
    t = xs.shape[1] if layout == "sm" else xs.shape[0]
    tm = _tile(t, tm)
    operands = [xs, w] + ([] if res is None else [res])
    specs = [_act_spec(layout, tm, ks, lambda i, k: (i, k)), _w_spec(w, layer, lambda i, k: k)]
    if res is not None:
        specs.append(pl.BlockSpec((tm, n), lambda i, k: (i, 0)))

    def epilogue(vals, ins):
        y = alpha * vals[0]
        return [y if res is None else ins[2][...] + y]

    return _gemm(
        name, (t // tm, N_CHIPS), operands, specs, [[(0, 1, NN)]], [(tm, n)],
        [jax.ShapeDtypeStruct((t, n), out_dtype)], [pl.BlockSpec((tm, n), lambda i, k: (i, 0))],
        epilogue, reduce_axis=1)[0]


def _mm_cs_dx(name, pairs, layout, layer, tm=1024):
    w0 = pairs[0][1]
    kd, cs = w0.shape[2], w0.shape[3]
    dy0 = pairs[0][0]
    t = dy0.shape[1] if layout == "sm" else dy0.shape[0]
    tm = _tile(t, tm)
    operands, specs, terms = [], [], []
    for dy, w in pairs:
        terms.append((len(operands), len(operands) + 1, NT))
        operands += [dy, w]
        specs += [_act_spec(layout, tm, cs, lambda i, k: (i, k)), _w_spec(w, layer, lambda i, k: k)]
    return _gemm(
        name, (t // tm, N_CHIPS), operands, specs, [terms], [(tm, kd)],
        [jax.ShapeDtypeStruct((t, kd), F32)], [pl.BlockSpec((tm, kd), lambda i, k: (i, 0))],
        _ident, reduce_axis=1)[0]


def _mm_rs_dx(name, dy, w, layer, out_layout, out_dtype, tm=512):
    t, n = dy.shape
    ks = w.shape[2]
    tm = _tile(t, tm)
    return _gemm(
        name, (N_CHIPS, t // tm), [dy, w],
        [pl.BlockSpec((tm, n), lambda k, i: (i, 0)), _w_spec(w, layer, lambda k, i: k)],
        [[(0, 1, NT)]], None, [_act_shape(out_layout, t, ks, out_dtype)],
        [_act_spec(out_layout, tm, ks, lambda k, i: (i, k))], _ident)[0]


def _mm_dw(name, x, x_layout, dy, dy_layout, slot, alpha=1.0, tk=2048):
    stack, layer, layers = slot
    if x_layout is None:
        t, rows = x.shape
        cols = dy.shape[2] if dy_layout == "sm" else dy.shape[1] // N_CHIPS
        tk = _tile(t, tk)
        xspec = pl.BlockSpec((tk, rows), lambda k, j: (j, 0))
        yspec = _act_spec(dy_layout, tk, cols, lambda k, j: (j, k))
    else:
        t, cols = dy.shape
        rows = x.shape[2] if x_layout == "sm" else x.shape[1] // N_CHIPS
        tk = _tile(t, tk)
        xspec = _act_spec(x_layout, tk, rows, lambda k, j: (j, k))
        yspec = pl.BlockSpec((tk, cols), lambda k, j: (j, 0))
    operands, specs = [x, dy], [xspec, yspec]
    if stack is not None:
        operands.append(stack)
        specs.append(ANY)
    return _gemm(
        name, (N_CHIPS, t // tk), operands, specs, [[(0, 1, TN)]], [(rows, cols)],
        [jax.ShapeDtypeStruct((N_CHIPS, layers, rows, cols), F32)],
        [pl.BlockSpec((None, None, rows, cols), lambda k, j: (k, layer, 0, 0))],
        lambda vals, ins: [alpha * vals[0]], reduce_axis=1, aliases=None if stack is None else {2: 0})[0]


def _rows(name, fn, ins, outs, accs=(), tm=256):
    t = ins[0].shape[0]
    tm = _tile(t, tm)
    n_in, n_out, n_acc = len(ins), len(outs), len(accs)
    in_specs = []
    for a in ins:
        if a.shape[0] == t:
            in_specs.append(pl.BlockSpec((tm, a.shape[1]), lambda i: (i, 0)))
        else:
            in_specs.append(pl.BlockSpec(a.shape, lambda i: (0, 0)))
    out_shape = [jax.ShapeDtypeStruct((t, c), d) for c, d in outs] + [jax.ShapeDtypeStruct(s, F32) for s in accs]
    out_specs = [pl.BlockSpec((tm, c), lambda i: (i, 0)) for c, _ in outs] + [pl.BlockSpec(s, lambda i: (0, 0)) for s in accs]

    def body(*refs):
        i = pl.program_id(0)
        row_vals, acc_vals = fn(*[r[...] for r in refs[:n_in]])
        for o, v in zip(refs[n_in:n_in + n_out], row_vals):
            o[...] = v.astype(o.dtype)
        acc_refs = refs[n_in + n_out:]
        if n_acc:
            @pl.when(i == 0)
            def _():
                for a in acc_refs:
                    a[...] = jnp.zeros_like(a)

            for a, v in zip(acc_refs, acc_vals):
                a[...] += v

    res = pl.pallas_call(
        body, name=name, grid=(t // tm,), in_specs=in_specs, out_specs=out_specs, out_shape=out_shape,
        compiler_params=_params("arbitrary" if n_acc else "parallel"))(*ins)
    return res[:n_out], res[n_out:]


def _to_bf16(a):
    def fn(x):
        return [x], []
    return _rows("weights_bf16", fn, [a.reshape(-1, a.shape[-1])], [(a.shape[-1], BF16)], tm=512)[0][0].reshape(a.shape)


def _rms_stats(x):
    return lax.rsqrt(jnp.mean(x * x, axis=-1, keepdims=True) + RMS_EPS)


def _rmsnorm(name, h, g):
    def fn(x, gg):
        return [x * _rms_stats(x) * gg], []
    return _rows(name, fn, [h, g], [(h.shape[1], BF16)])[0][0]


def _rms_bwd_math(dn, x, g):
    r = _rms_stats(x)
    xhat = x * r
    dxh = dn * g
    dx = r * (dxh - xhat * jnp.mean(dxh * xhat, axis=-1, keepdims=True))
    return dx, jnp.sum(dn * xhat, axis=0, keepdims=True)


def _rmsnorm_bwd(name, dres, dn, h, g):
    def fn(dr, d, x, gg):
        dx, dg = _rms_bwd_math(d, x, gg)
        return [dr + dx], [dg]
    (dh,), (dg,) = _rows(name, fn, [dres, dn, h, g], [(h.shape[1], F32)], [(1, h.shape[1])])
    return dh, dg


def _ffn_fwd(h, g, w1, w3, w2, layer, tm=512):
    t, d = h.shape
    fs = w1.shape[3]
    n = _rmsnorm("ffn_norm", h, g)
    tm = _tile(t, tm)

    def up(vals, ins):
        a, b = vals
        return [a, b, a * _sigmoid(a) * b]

    sm = _act_shape("sm", t, fs, BF16)
    osp = _act_spec("sm", tm, fs, lambda k, i: (i, k))
    a, b, s = _gemm(
        "ffn_up", (N_CHIPS, t // tm), [n, w1, w3],
        [pl.BlockSpec((tm, d), lambda k, i: (i, 0)), _w_spec(w1, layer, lambda k, i: k), _w_spec(w3, layer, lambda k, i: k)],
        [[(0, 1, NN)], [(0, 2, NN)]], None, [sm, sm, sm], [osp, osp, osp], up)
    out = _mm_rs("ffn_down", s, "sm", w2, layer, res=h, alpha=0.5)
    return out, (h, n, a, b, s)


def _ffn_bwd(dout, saved, g, w1, w3, w2, layer, slots, tm=512):
    h, n, a, b, s = saved
    t, d = h.shape
    fs = w1.shape[3]
    tm = _tile(t, tm)
    dob = dout.astype(BF16)

    def down(vals, ins):
        ds = 0.5 * vals[0]
        av, bv = ins[2][...].astype(F32), ins[3][...].astype(F32)
        sg = _sigmoid(av)
        return [ds * bv * sg * (1.0 + av * (1.0 - sg)), ds * av * sg]

    sm = _act_shape("sm", t, fs, BF16)
    asp = _act_spec("sm", tm, fs, lambda k, i: (i, k))
    da, db = _gemm(
        "ffn_down_dx", (N_CHIPS, t // tm), [dob, w2, a, b],
        [pl.BlockSpec((tm, d), lambda k, i: (i, 0)), _w_spec(w2, layer, lambda k, i: k), asp, asp],
        [[(0, 1, NT)]], None, [sm, sm], [asp, asp], down)
    dw2 = _mm_dw("ffn_dw2", s, "sm", dob, None, slots[2], alpha=0.5)
    dw1 = _mm_dw("ffn_dw1", n, None, da, "sm", slots[0])
    dw3 = _mm_dw("ffn_dw3", n, None, db, "sm", slots[1])
    dn = _mm_cs_dx("ffn_up_dx", [(da, w1), (db, w3)], "sm", layer)
    dh, dg = _rmsnorm_bwd("ffn_norm_bwd", dout, dn, h, g)
    return dh, dg, dw1, dw3, dw2


def _ple_fwd(h, g, p2, wproj, wgate, layer):
    n = _rmsnorm("ple_norm", h, g)
    gl = _mm_rs("ple_gate", n, "flat", wgate, layer)
    pp = _mm_cs("ple_proj", p2, wproj, layer, "flat", F32)

    def fn(hh, gg, q):
        return [hh + q * _sigmoid(gg)], []
    out = _rows("ple_mix", fn, [h, gl, pp], [(h.shape[1], F32)])[0][0]
    return out, (h, n, gl, pp)


def _ple_bwd(dout, saved, g, p2, wproj, wgate, layer, slots):
    h, n, gl, pp = saved
    d = h.shape[1]

    def fn(do, gg, q):
        sg = _sigmoid(gg)
        return [do * sg, do * q * sg * (1.0 - sg)], []
    (dpp, dgl), _ = _rows("ple_mix_bwd", fn, [dout, gl, pp], [(d, BF16), (d, BF16)])
    dwproj = _mm_dw("ple_dwproj", p2, None, dpp, "flat", slots[0])
    dwgate = _mm_dw("ple_dwgate", n, "flat", dgl, None, slots[1])
    dn = _mm_rs_dx("ple_gate_dx", dgl, wgate, layer, "flat", F32)
    dh, dg = _rmsnorm_bwd("ple_norm_bwd", dout, dn, h, g)
    return dh, dg, dwproj, dwgate


def _head(h, g, target):
    d = h.shape[1]

    def fn(x, gg, tg):
        y = x * _rms_stats(x) * gg
        err = y - tg
        dy = err * (1.0 / d)
        dx, dg = _rms_bwd_math(dy, x, gg)
        loss = 0.5 * jnp.sum(jnp.sum(err * err, axis=-1, keepdims=True) * (1.0 / d), axis=0, keepdims=True)
        return [dx], [dg, jnp.broadcast_to(loss, (1, 128))]
    (dh,), (dg, loss) = _rows("loss_head", fn, [h, g, target], [(d, F32)], [(1, d), (1, 128)])
    return loss[0, 0], dh, dg


S5_LANES = 2 * S5_STATE
S5_GB = 128 // S5_GROUP


def _s5_prep(a_re, a_im, log_dt, b_re, b_im, c_re, c_im):
    c, gb = S5_CHUNK, S5_GB
    g = a_re.shape[0]
    nb = g // gb
    lam_re = jnp.minimum(a_re, -1e-4)
    lam_im = a_im
    dt = jnp.exp(log_dt)[:, None, None]
    ks = jnp.arange(c + 1, dtype=F32)
    mag = jnp.exp(lam_re[..., None] * dt * ks)
    ph = lam_im[..., None] * dt * ks
    pw_re, pw_im = mag * jnp.cos(ph), mag * jnp.sin(ph)
    den = lam_re * lam_re + lam_im * lam_im
    nr, ni = pw_re[..., 1] - 1.0, pw_im[..., 1]
    fr = (nr * lam_re + ni * lam_im) / den
    fi = (ni * lam_re - nr * lam_im) / den
    bb_re = fr[..., None] * b_re - fi[..., None] * b_im
    bb_im = fr[..., None] * b_im + fi[..., None] * b_re
    ct_re, ct_im = c_re.transpose(0, 2, 1), c_im.transpose(0, 2, 1)
    ca_re = ct_re[:, :, None, :] * pw_re[..., None] - ct_im[:, :, None, :] * pw_im[..., None]
    ca_im = ct_re[:, :, None, :] * pw_im[..., None] + ct_im[:, :, None, :] * pw_re[..., None]
    hp = lax.Precision.HIGHEST
    kern = (jnp.einsum("gpj,gpkh->gkjh", bb_re, ca_re[:, :, :c], precision=hp)
            - jnp.einsum("gpj,gpkh->gkjh", bb_im, ca_im[:, :, :c], precision=hp))
    rev_re = pw_re[:, :, :c][:, :, ::-1].transpose(0, 2, 1)
    rev_im = pw_im[:, :, :c][:, :, ::-1].transpose(0, 2, 1)
    bt_re, bt_im = bb_re.transpose(0, 2, 1), bb_im.transpose(0, 2, 1)
    wn_re = rev_re[:, :, None, :] * bt_re[:, None] - rev_im[:, :, None, :] * bt_im[:, None]
    wn_im = rev_re[:, :, None, :] * bt_im[:, None] + rev_im[:, :, None, :] * bt_re[:, None]
    wn = jnp.concatenate([wn_re, wn_im], axis=-1)
    wo = jnp.concatenate([ca_re[:, :, 1:], -ca_im[:, :, 1:]], axis=1)
    eye = jnp.eye(gb, dtype=F32)[None, None, :, None, :, None]

    def embed(x):
        r, w = x.shape[2], x.shape[3]
        x = x.reshape(nb, gb, c, r, w).transpose(0, 2, 1, 3, 4)
        return (x[:, :, :, :, None, :] * eye).reshape(nb, c, gb * r, gb * w)

    bd = embed(kern)
    bn = embed(wn)
    co = embed(wo.transpose(0, 2, 1, 3))
    ar, ai = pw_re[..., c], pw_im[..., c]
    return bd, bn, co, jnp.concatenate([ar, ar], axis=1), jnp.concatenate([-ai, ai], axis=1)


def _step_rows(ref, tau, n):
    return ref[pl.ds(tau, n, stride=S5_CHUNK), :].astype(BF16)


def _cat_groups(ref, dtype):
    return jnp.concatenate([ref[:, j, :] for j in range(S5_GB)], axis=1).astype(dtype)


def _cat_steps(ref, n):
    return jnp.concatenate([_step_rows(ref, tau, n) for tau in range(S5_CHUNK)], axis=1)


def _stack_steps(ref, n):
    return jnp.concatenate([_step_rows(ref, tau, n) for tau in range(S5_CHUNK)], axis=0)


def _cat_ops(ref, axis, reverse=False):
    order = range(S5_CHUNK - 1, -1, -1) if reverse else range(S5_CHUNK)
    return jnp.concatenate([ref[k] for k in order], axis=axis)


def _s5_specs(t, d):
    nct, g = t // S5_CHUNK, d // S5_GROUP
    tok = pl.BlockSpec((t, 128), lambda i: (0, i))
    st = pl.BlockSpec((nct, S5_GB, S5_LANES), lambda i: (0, i, 0))
    op = lambda w: pl.BlockSpec((None,) + w.shape[1:], lambda i: (i, 0, 0, 0))
    return nct, g, tok, st, op


def _s5_chunk_fwd(u, bd, bn):
    t, d = u.shape
    nct, g, tok, st, op = _s5_specs(t, d)
    c = S5_CHUNK

    def body(u_ref, bd_ref, bn_ref, y_ref, s_ref):
        ucat = _cat_steps(u_ref, nct)
        sloc = jnp.dot(ucat, _cat_ops(bn_ref, 0), preferred_element_type=F32)
        for j in range(S5_GB):
            s_ref[:, j, :] = sloc[:, j * S5_LANES:(j + 1) * S5_LANES]
        lags = _cat_ops(bd_ref, 0, reverse=True)
        for tt in range(c):
            y_ref[pl.ds(tt, nct, stride=c), :] = jnp.dot(ucat[:, :(tt + 1) * 128], lags[(c - 1 - tt) * 128:, :],
                                                         preferred_element_type=F32)

    return pl.pallas_call(
        body, name="s5_chunk", grid=(d // 128,), in_specs=[tok, op(bd), op(bn)], out_specs=[tok, st],
        out_shape=[jax.ShapeDtypeStruct((t, d), F32), jax.ShapeDtypeStruct((nct, g, S5_LANES), F32)],
        compiler_params=_params("parallel"))(u, bd, bn)


def _s5_state_out(sprev, co, yin):
    t, d = yin.shape
    nct, g, tok, st, op = _s5_specs(t, d)
    c = S5_CHUNK

    def body(s_ref, co_ref, yi_ref, y_ref):
        ys = jnp.dot(_cat_groups(s_ref, BF16), _cat_ops(co_ref, 1), preferred_element_type=F32)
        for tt in range(c):
            rows = pl.ds(tt, nct, stride=c)
            y_ref[rows, :] = yi_ref[rows, :] + ys[:, tt * 128:(tt + 1) * 128]

    return pl.pallas_call(
        body, name="s5_state_out", grid=(d // 128,), in_specs=[st, op(co), tok], out_specs=tok,
        out_shape=jax.ShapeDtypeStruct((t, d), F32), compiler_params=_params("parallel"))(sprev, co, yin)


def _s5_state_out_dx(dyb, co):
    t, d = dyb.shape
    nct, g, tok, st, op = _s5_specs(t, d)
    c = S5_CHUNK

    def body(dy_ref, co_ref, ds_ref):
        acc = lax.dot_general(_cat_steps(dy_ref, nct), _cat_ops(co_ref, 1), NT, preferred_element_type=F32)
        for j in range(S5_GB):
            ds_ref[:, j, :] = acc[:, j * S5_LANES:(j + 1) * S5_LANES]

    return pl.pallas_call(
        body, name="s5_state_out_dx", grid=(d // 128,), in_specs=[tok, op(co)], out_specs=st,
        out_shape=jax.ShapeDtypeStruct((nct, g, S5_LANES), F32), compiler_params=_params("parallel"))(dyb, co)


def _s5_chunk_dx(dyb, dsloc, bd, bn, skip):
    t, d = dyb.shape
    nct, g, tok, st, op = _s5_specs(t, d)
    c = S5_CHUNK

    def body(dy_ref, ds_ref, bd_ref, bn_ref, sk_ref, du_ref):
        dus = lax.dot_general(_cat_groups(ds_ref, BF16), _cat_ops(bn_ref, 0), NT, preferred_element_type=F32)
        dycat = _cat_steps(dy_ref, nct)
        lags = _cat_ops(bd_ref, 1)
        for tau in range(c):
            rows = pl.ds(tau, nct, stride=c)
            du_ref[rows, :] = (sk_ref[rows, :] + dus[:, tau * 128:(tau + 1) * 128]
                               + lax.dot_general(dycat[:, tau * 128:], lags[:, :(c - tau) * 128], NT,
                                                 preferred_element_type=F32))

    return pl.pallas_call(
        body, name="s5_chunk_dx", grid=(d // 128,), in_specs=[tok, st, op(bd), op(bn), tok], out_specs=tok,
        out_shape=jax.ShapeDtypeStruct((t, d), F32), compiler_params=_params("parallel"))(dyb, dsloc, bd, bn, skip)


def _s5_chunk_dw(u, dyb, dsloc, bd, bn):
    t, d = u.shape
    nct, g, tok, st, op = _s5_specs(t, d)
    c = S5_CHUNK

    def body(u_ref, dy_ref, ds_ref, dbd_ref, dbn_ref):
        dbn = lax.dot_general(_cat_steps(u_ref, nct), _cat_groups(ds_ref, BF16), TN, preferred_element_type=F32)
        for tau in range(c):
            dbn_ref[tau] = dbn[tau * 128:(tau + 1) * 128, :]
        ustk, dystk = _stack_steps(u_ref, nct), _stack_steps(dy_ref, nct)
        for k in range(c):
            dbd_ref[k] = lax.dot_general(ustk[:(c - k) * nct], dystk[k * nct:], TN, preferred_element_type=F32)

    return pl.pallas_call(
        body, name="s5_chunk_dw", grid=(d // 128,), in_specs=[tok, tok, st], out_specs=[op(bd), op(bn)],
        out_shape=[jax.ShapeDtypeStruct(bd.shape, F32), jax.ShapeDtypeStruct(bn.shape, F32)],
        compiler_params=_params("parallel"))(u, dyb, dsloc)


def _s5_state_out_dw(sprev, dyb, co):
    t, d = dyb.shape
    nct, g, tok, st, op = _s5_specs(t, d)
    c = S5_CHUNK

    def body(s_ref, dy_ref, dco_ref):
        dco = lax.dot_general(_cat_groups(s_ref, BF16), _cat_steps(dy_ref, nct), TN, preferred_element_type=F32)
        for tt in range(c):
            dco_ref[tt] = dco[:, tt * 128:(tt + 1) * 128]

    return pl.pallas_call(
        body, name="s5_state_out_dw", grid=(d // 128,), in_specs=[st, tok], out_specs=op(co),
        out_shape=jax.ShapeDtypeStruct(co.shape, F32), compiler_params=_params("parallel"))(sprev, dyb)


def _s5_scan_fwd(sloc, m1, m2):
    bl, nc, g, w = sloc.shape

    def body(s_ref, m1_ref, m2_ref, o_ref):
        a1, a2 = m1_ref[...], m2_ref[...]

        def step(c, states):
            new = []
            for b, s in enumerate(states):
                o_ref[b, c] = s
                new.append(a1 * s + a2 * pltpu.roll(s, S5_STATE, 1) + s_ref[b, c])
            return tuple(new)
        lax.fori_loop(0, nc, step, tuple(jnp.zeros((g, w), F32) for _ in range(bl)))

    vm = pl.BlockSpec(memory_space=pltpu.VMEM)
    return pl.pallas_call(
        body, name="s5_scan", in_specs=[vm, vm, vm], out_specs=vm,
        out_shape=jax.ShapeDtypeStruct(sloc.shape, F32),
        compiler_params=pltpu.CompilerParams(vmem_limit_bytes=VMEM_LIMIT))(sloc, m1, m2)


def _s5_scan_bwd(dsprev, sprev, m1, m2):
    bl, nc, g, w = dsprev.shape

    def body(d_ref, s_ref, m1_ref, m2_ref, g_ref, p1_ref, p2_ref):
        a1, a2 = m1_ref[...], m2_ref[...]
        zero = jnp.zeros((g, w), F32)

        def step(i, carry):
            gps, p1, p2 = carry
            c = nc - 2 - i
            new = []
            for b, gp in enumerate(gps):
                g_ref[b, c] = gp
                sp = s_ref[b, c]
                p1 = p1 + gp * sp
                p2 = p2 + gp * pltpu.roll(sp, S5_STATE, 1)
                new.append(d_ref[b, c] + a1 * gp - a2 * pltpu.roll(gp, S5_STATE, 1))
            return tuple(new), p1, p2

        for b in range(bl):
            g_ref[b, nc - 1] = zero
        _, p1, p2 = lax.fori_loop(0, nc - 1, step, (tuple(d_ref[b, nc - 1] for b in range(bl)), zero, zero))
        p1_ref[...] = p1
        p2_ref[...] = p2

    vm = pl.BlockSpec(memory_space=pltpu.VMEM)
    sd = jax.ShapeDtypeStruct
    return pl.pallas_call(
        body, name="s5_scan_bwd", in_specs=[vm, vm, vm, vm], out_specs=[vm, vm, vm],
        out_shape=[sd(dsprev.shape, F32), sd((g, w), F32), sd((g, w), F32)],
        compiler_params=pltpu.CompilerParams(vmem_limit_bytes=VMEM_LIMIT))(dsprev, sprev, m1, m2)


def _gelu_tanh_parts(y):
    c0 = math.sqrt(2.0 / math.pi)
    inner = c0 * (y + 0.044715 * y * y * y)
    th = jnp.tanh(inner)
    return th, c0 * (1.0 + 3 * 0.044715 * y * y)


def _s5_fwd(h, g, ops, d_skip, w_in, w_glu, bl):
    bd, bn, co, m1, m2 = ops
    t, d = h.shape
    nct, groups = t // S5_CHUNK, d // S5_GROUP
    hn = _rmsnorm("mix_norm", h, g)
    u = _mm_rs("s5_in", hn, "flat", w_in, 0)
    yin, sloc = _s5_chunk_fwd(u, bd.astype(BF16), bn.astype(BF16))
    sprev = _s5_scan_fwd(sloc.reshape(bl, nct // bl, groups, S5_LANES), m1, m2).reshape(nct, groups, S5_LANES)
    y = _s5_state_out(sprev, co.astype(BF16), yin)

    def fn(yy, uu, dd):
        y2 = yy + dd * uu
        th, _ = _gelu_tanh_parts(y2)
        return [0.5 * y2 * (1.0 + th)], []
    z = _rows("s5_gelu", fn, [y, u, d_skip], [(d, BF16)])[0][0]
    zz = _mm_cs("s5_glu", z, w_glu, 0, "flat", F32)

    def glu(hh, zv):
        return [hh + zv[:, :d] * _sigmoid(zv[:, d:])], []
    out = _rows("s5_glu_mix", glu, [h, zz], [(d, F32)])[0][0]
    return out, (h, hn, u, sprev, y, z, zz)


def _s5_bwd(dout, saved, g, ops, d_skip, w_in, w_glu, bl):
    h, hn, u, sprev, y, z, zz = saved
    bd, bn, co, m1, m2 = ops
    t, d = h.shape
    nct, groups = t // S5_CHUNK, d // S5_GROUP

    def glu_bwd(do, zv):
        sg = _sigmoid(zv[:, d:])
        return [jnp.concatenate([do * sg, do * zv[:, :d] * sg * (1.0 - sg)], axis=1)], []
    dzz = _rows("s5_glu_bwd", glu_bwd, [dout, zz], [(2 * d, BF16)])[0][0]
    dwglu = _mm_dw("s5_dwglu", z, None, dzz, "flat", (None, 0, 1))
    dz = _mm_cs_dx("s5_glu_dx", [(dzz, w_glu)], "flat", 0)

    def gelu_bwd(dzv, yy, uu, dd):
        y2 = yy + dd * uu
        th, dinner = _gelu_tanh_parts(y2)
        dy2 = dzv * (0.5 * (1.0 + th) + 0.5 * y2 * (1.0 - th * th) * dinner)
        return [dy2, dy2 * dd], [jnp.sum(dy2 * uu, axis=0, keepdims=True)]
    (dyb, du_skip), (dd,) = _rows("s5_gelu_bwd", gelu_bwd, [dz, y, u, d_skip], [(d, F32), (d, F32)], [(1, d)])
    bd_b, bn_b, co_b = bd.astype(BF16), bn.astype(BF16), co.astype(BF16)
    dsprev = _s5_state_out_dx(dyb, co_b)
    shape4 = (bl, nct // bl, groups, S5_LANES)
    dsloc, dm1, dm2 = _s5_scan_bwd(dsprev.reshape(shape4), sprev.reshape(shape4), m1, m2)
    dsloc = dsloc.reshape(nct, groups, S5_LANES)
    du = _s5_chunk_dx(dyb, dsloc, bd_b, bn_b, du_skip).astype(BF16)
    dbd, dbn = _s5_chunk_dw(u, dyb, dsloc, bd, bn)
    dco = _s5_state_out_dw(sprev, dyb, co)
    dwin = _mm_dw("s5_dwin", hn, "flat", du, None, (None, 0, 1))
    dhn = _mm_rs_dx("s5_in_dx", du, w_in, 0, "flat", F32)
    dh, dg = _rmsnorm_bwd("mix_norm_bwd", dout, dhn, h, g)
    return dh, dg, dwin, dwglu, dd, (dbd, dbn, dco, dm1, dm2)


def _sb_block(qi, idx, tb):
    kb = qi - idx
    return pl.multiple_of(jnp.maximum(kb, 0) * tb, tb), idx == 0, kb >= 0


def _sb_scores(q, kblk, diag, exists, row, col):
    z = lax.dot_general(q, kblk, NT, preferred_element_type=F32) * (SB_HEAD_DIM ** -0.5)
    l1 = jnp.log(1.0 + jnp.exp(-jnp.abs(z)))
    ls = jnp.minimum(z, 0.0) - l1
    mask = jnp.logical_and(jnp.logical_or(col < row, jnp.logical_not(diag)), exists)
    lk = jnp.where(mask, ls - z, 0.0)
    return ls, lk, mask


def _sb_more(qi, carry):
    j, cr = carry[0], carry[2]
    return jnp.logical_and(j <= qi, jnp.max(cr) > SB_CUT)


def _split_dot(v, tri):
    hi = v.astype(BF16)
    lo = (v - hi.astype(F32)).astype(BF16)
    return (jnp.dot(hi, tri, preferred_element_type=F32) + jnp.dot(lo, tri, preferred_element_type=F32))


def _sb_attn_fwd(q, k, v):
    bh, l, dh = q.shape
    tb = min(SB_BLOCK, l)
    nq = l // tb

    def body(q_ref, k_ref, v_ref, o_ref):
        qi = pl.program_id(1)
        qv = q_ref[...]
        row = lax.broadcasted_iota(jnp.int32, (tb, tb), 0)
        col = lax.broadcasted_iota(jnp.int32, (tb, tb), 1)
        tri = (row > col).astype(BF16)

        def step(carry):
            j, acc, cr = carry
            where = [_sb_block(qi, j + u, tb) for u in range(SB_UNROLL)]
            scores = [_sb_scores(qv, k_ref[pl.ds(ks, tb), :], diag, exists, row, col) for ks, diag, exists in where]
            laters = [_split_dot(lk, tri) for _, lk, _ in scores]
            for (ks, _, _), (ls, lk, mask), later in zip(where, scores, laters):
                att = jnp.where(mask, jnp.exp(ls + later + cr), 0.0)
                acc = acc + jnp.dot(att.astype(BF16), v_ref[pl.ds(ks, tb), :], preferred_element_type=F32)
                cr = cr + jnp.sum(lk, axis=1, keepdims=True)
            return j + SB_UNROLL, acc, cr

        _, acc, _ = lax.while_loop(functools.partial(_sb_more, qi), step,
                                   (jnp.int32(0), jnp.zeros((tb, dh), F32), jnp.zeros((tb, 1), F32)))
        o_ref[...] = acc

    blk = pl.BlockSpec((None, tb, dh), lambda b, i: (b, i, 0))
    full = pl.BlockSpec((None, l, dh), lambda b, i: (b, 0, 0))
    return pl.pallas_call(
        body, name="sb_attn", grid=(bh, nq), in_specs=[blk, full, full], out_specs=blk,
        out_shape=jax.ShapeDtypeStruct((bh, l, dh), F32), compiler_params=_params("parallel", "parallel"))(q, k, v)


def _sb_attn_bwd(q, k, v, o, do):
    bh, l, dh = q.shape
    tb = min(SB_BLOCK, l)
    nq = l // tb
    scale = SB_HEAD_DIM ** -0.5

    def body(q_ref, k_ref, v_ref, o_ref, do_ref, dq_ref, dk_ref, dv_ref):
        qi = pl.program_id(1)

        @pl.when(qi == 0)
        def _():
            dk_ref[...] = jnp.zeros_like(dk_ref)
            dv_ref[...] = jnp.zeros_like(dv_ref)

        qv = q_ref[...]
        dob = do_ref[...].astype(BF16)
        dsum = jnp.sum(dob.astype(F32) * o_ref[...], axis=1, keepdims=True)
        row = lax.broadcasted_iota(jnp.int32, (tb, tb), 0)
        col = lax.broadcasted_iota(jnp.int32, (tb, tb), 1)
        tri = (row > col).astype(BF16)
        tri_inc = (row >= col).astype(BF16)

        def step(carry):
            j, dq, cr, ce = carry
            n = range(SB_UNROLL)
            where = [_sb_block(qi, j + u, tb) for u in n]
            rows = [pl.ds(ks, tb) for ks, _, _ in where]
            scores = [_sb_scores(qv, k_ref[rows[u], :], where[u][1], where[u][2], row, col) for u in n]
            laters = [_split_dot(lk, tri) for _, lk, _ in scores]
            datts = [lax.dot_general(dob, v_ref[rows[u], :], NT, preferred_element_type=F32) for u in n]
            atts = []
            for (ls, lk, mask), later in zip(scores, laters):
                atts.append(jnp.where(mask, jnp.exp(ls + later + cr), 0.0).astype(BF16))
                cr = cr + jnp.sum(lk, axis=1, keepdims=True)
            es = [atts[u].astype(F32) * datts[u] for u in n]
            sufs = [_split_dot(e, tri_inc) for e in es]
            dzs = []
            for (ls, _, mask), e, suf in zip(scores, es, sufs):
                pre = dsum - ce - suf
                sg = jnp.exp(ls)
                dzs.append((jnp.where(mask, e * (1.0 - sg) - pre * sg, 0.0) * scale).astype(BF16))
                ce = ce + jnp.sum(e, axis=1, keepdims=True)
            for u in n:
                dq = dq + jnp.dot(dzs[u], k_ref[rows[u], :], preferred_element_type=F32)
                dk_ref[rows[u], :] += lax.dot_general(dzs[u], qv, TN, preferred_element_type=F32)
                dv_ref[rows[u], :] += lax.dot_general(atts[u], dob, TN, preferred_element_type=F32)
            return j + SB_UNROLL, dq, cr, ce

        zc = jnp.zeros((tb, 1), F32)
        _, dq, _, _ = lax.while_loop(functools.partial(_sb_more, qi), step,
                                     (jnp.int32(0), jnp.zeros((tb, dh), F32), zc, zc))
        dq_ref[...] = dq

    blk = pl.BlockSpec((None, tb, dh), lambda b, i: (b, i, 0))
    full = pl.BlockSpec((None, l, dh), lambda b, i: (b, 0, 0))
    sd = jax.ShapeDtypeStruct((bh, l, dh), F32)
    return pl.pallas_call(
        body, name="sb_attn_bwd", grid=(bh, nq), in_specs=[blk, full, full, blk, blk], out_specs=[blk, full, full],
        out_shape=[sd, sd, sd], compiler_params=_params("parallel", "arbitrary"))(q, k, v, o, do)


SB_PAIR = 2 * SB_HEAD_DIM


def _pair_masks(tb):
    lane = lax.broadcasted_iota(jnp.int32, (1, SB_PAIR), 1)
    row = lax.broadcasted_iota(jnp.int32, (tb, tb), 0)
    col = lax.broadcasted_iota(jnp.int32, (tb, tb), 1)
    return [lane < SB_HEAD_DIM, lane >= SB_HEAD_DIM], row, col


def _pair_more(qi, carry):
    j, crs = carry[0], carry[2]
    return jnp.logical_and(j <= qi, jnp.maximum(jnp.max(crs[0]), jnp.max(crs[1])) > SB_CUT)


def _pair_specs(bl, l, d, tb):
    nq, off = l // tb, d // SB_PAIR
    qspec = pl.BlockSpec((tb, SB_PAIR), lambda b, p, i: (b * nq + i, p))
    kspec = pl.BlockSpec((l, SB_PAIR), lambda b, p, i: (b, off + p))
    vspec = pl.BlockSpec((l, SB_PAIR), lambda b, p, i: (b, 2 * off + p))
    return qspec, kspec, vspec


def _sb_attn_fwd2(qkv, bl):
    t, d3 = qkv.shape
    d, l = d3 // 3, t // bl
    tb = min(SB_BLOCK, l)
    nq = l // tb

    def body(q_ref, k_ref, v_ref, o_ref, ob_ref):
        qi = pl.program_id(2)
        heads, row, col = _pair_masks(tb)
        qv = q_ref[...]
        qh = [jnp.where(m, qv, jnp.zeros_like(qv)) for m in heads]
        tri = (row > col).astype(BF16)

        def step(carry):
            j, acc, crs = carry
            crs = list(crs)
            where = [_sb_block(qi, j + u, tb) for u in range(SB_UNROLL)]
            kblks = [k_ref[pl.ds(ks, tb), :] for ks, _, _ in where]
            scores = [[_sb_scores(qh[hd], kblks[u], where[u][1], where[u][2], row, col) for hd in range(2)]
                      for u in range(SB_UNROLL)]
            laters = [[_split_dot(sc[1], tri) for sc in su] for su in scores]
            for u in range(SB_UNROLL):
                vblk = v_ref[pl.ds(where[u][0], tb), :]
                outs = []
                for hd in range(2):
                    ls, lk, mask = scores[u][hd]
                    att = jnp.where(mask, jnp.exp(ls + laters[u][hd] + crs[hd]), 0.0)
                    outs.append(jnp.dot(att.astype(BF16), vblk, preferred_element_type=F32))
                    crs[hd] = crs[hd] + jnp.sum(lk, axis=1, keepdims=True)
                acc = acc + jnp.where(heads[0], outs[0], outs[1])
            return j + SB_UNROLL, acc, tuple(crs)

        zc = jnp.zeros((tb, 1), F32)
        _, acc, _ = lax.while_loop(functools.partial(_pair_more, qi), step,
                                   (jnp.int32(0), jnp.zeros((tb, SB_PAIR), F32), (zc, zc)))
        o_ref[...] = acc
        ob_ref[...] = acc.astype(BF16)

    qspec, kspec, vspec = _pair_specs(bl, l, d, tb)
    return pl.pallas_call(
        body, name="sb_attn", grid=(bl, d // SB_PAIR, nq), in_specs=[qspec, kspec, vspec], out_specs=[qspec, qspec],
        out_shape=[jax.ShapeDtypeStruct((t, d), F32), jax.ShapeDtypeStruct((t, d), BF16)],
        compiler_params=_params("parallel", "parallel", "parallel"))(qkv, qkv, qkv)


def _sb_attn_bwd2(qkv, o, do, bl):
    t, d3 = qkv.shape
    d, l = d3 // 3, t // bl
    tb = min(SB_BLOCK, l)
    nq = l // tb
    scale = SB_HEAD_DIM ** -0.5

    def body(q_ref, k_ref, v_ref, o_ref, do_ref, dq_ref, dk_ref, dv_ref, dk_acc, dv_acc):
        qi = pl.program_id(2)

        @pl.when(qi == 0)
        def _():
            dk_acc[...] = jnp.zeros_like(dk_acc)
            dv_acc[...] = jnp.zeros_like(dv_acc)

        heads, row, col = _pair_masks(tb)
        qv = q_ref[...]
        dov = do_ref[...].astype(BF16)
        qh = [jnp.where(m, qv, jnp.zeros_like(qv)) for m in heads]
        doh = [jnp.where(m, dov, jnp.zeros_like(dov)) for m in heads]
        ov = o_ref[...]
        dsum = [jnp.sum(dh.astype(F32) * ov, axis=1, keepdims=True) for dh in doh]
        tri = (row > col).astype(BF16)
        tri_inc = (row >= col).astype(BF16)

        def step(carry):
            j, dq, crs, ces = carry
            crs, ces = list(crs), list(ces)
            n = range(SB_UNROLL)
            where = [_sb_block(qi, j + u, tb) for u in n]
            rows = [pl.ds(ks, tb) for ks, _, _ in where]
            kblks = [k_ref[rows[u], :] for u in n]
            vblks = [v_ref[rows[u], :] for u in n]
            scores = [[_sb_scores(qh[hd], kblks[u], where[u][1], where[u][2], row, col) for hd in range(2)] for u in n]
            laters = [[_split_dot(sc[1], tri) for sc in su] for su in scores]
            datts = [[lax.dot_general(doh[hd], vblks[u], NT, preferred_element_type=F32) for hd in range(2)] for u in n]
            atts = [[None, None] for _ in n]
            for u in n:
                for hd in range(2):
                    ls, lk, mask = scores[u][hd]
                    atts[u][hd] = jnp.where(mask, jnp.exp(ls + laters[u][hd] + crs[hd]), 0.0).astype(BF16)
                    crs[hd] = crs[hd] + jnp.sum(lk, axis=1, keepdims=True)
            es = [[atts[u][hd].astype(F32) * datts[u][hd] for hd in range(2)] for u in n]
            sufs = [[_split_dot(e, tri_inc) for e in eu] for eu in es]
            dzs = [[None, None] for _ in n]
            for u in n:
                for hd in range(2):
                    ls, _, mask = scores[u][hd]
                    pre = dsum[hd] - ces[hd] - sufs[u][hd]
                    sg = jnp.exp(ls)
                    dzs[u][hd] = (jnp.where(mask, es[u][hd] * (1.0 - sg) - pre * sg, 0.0) * scale).astype(BF16)
                    ces[hd] = ces[hd] + jnp.sum(es[u][hd], axis=1, keepdims=True)
            for u in n:
                dq = dq + jnp.where(heads[0], jnp.dot(dzs[u][0], kblks[u], preferred_element_type=F32),
                                    jnp.dot(dzs[u][1], kblks[u], preferred_element_type=F32))
                dk_acc[rows[u], :] += (lax.dot_general(dzs[u][0], qh[0], TN, preferred_element_type=F32)
                                       + lax.dot_general(dzs[u][1], qh[1], TN, preferred_element_type=F32))
                dv_acc[rows[u], :] += (lax.dot_general(atts[u][0], doh[0], TN, preferred_element_type=F32)
                                       + lax.dot_general(atts[u][1], doh[1], TN, preferred_element_type=F32))
            return j + SB_UNROLL, dq, tuple(crs), tuple(ces)

        zc = jnp.zeros((tb, 1), F32)
        _, dq, _, _ = lax.while_loop(functools.partial(_pair_more, qi), step,
                                     (jnp.int32(0), jnp.zeros((tb, SB_PAIR), F32), (zc, zc), (zc, zc)))
        dq_ref[...] = dq.astype(BF16)

        @pl.when(qi == nq - 1)
        def _():
            dk_ref[...] = dk_acc[...].astype(BF16)
            dv_ref[...] = dv_acc[...].astype(BF16)

    qspec, kspec, vspec = _pair_specs(bl, l, d, tb)
    blk = pl.BlockSpec((tb, SB_PAIR), lambda b, p, i: (b * nq + i, p))
    full = pl.BlockSpec((l, SB_PAIR), lambda b, p, i: (b, p))
    sd = jax.ShapeDtypeStruct((t, d), BF16)
    dq, dk, dv = pl.pallas_call(
        body, name="sb_attn_bwd", grid=(bl, d // SB_PAIR, nq), in_specs=[qspec, kspec, vspec, blk, blk],
        out_specs=[blk, full, full], out_shape=[sd, sd, sd],
        scratch_shapes=[pltpu.VMEM((l, SB_PAIR), F32), pltpu.VMEM((l, SB_PAIR), F32)],
        compiler_params=_params("parallel", "parallel", "arbitrary"))(qkv, qkv, qkv, o, do)
    return jnp.concatenate([dq, dk, dv], axis=1)


def _to_heads(x, bl):
    t, w = x.shape
    heads = w // SB_HEAD_DIM
    l = t // bl
    return x.reshape(bl, l, heads, SB_HEAD_DIM).transpose(0, 2, 1, 3).reshape(bl * heads, l, SB_HEAD_DIM)


def _from_heads(x, bl):
    bh, l, dh = x.shape
    heads = bh // bl
    return x.reshape(bl, heads, l, dh).transpose(0, 2, 1, 3).reshape(bl * l, heads * dh)


def _sb_fwd(h, g, w_qkv, w_o, bl):
    t, d = h.shape
    hn = _rmsnorm("mix_norm", h, g)
    qkv = _mm_cs("sb_qkv", hn, w_qkv, 0, "flat", BF16)
    o, ob = _sb_attn_fwd2(qkv, bl)
    out = _mm_rs("sb_out", ob, "flat", w_o, 0, res=h)
    return out, (h, hn, qkv, o, ob)


def _sb_bwd(dout, saved, g, w_qkv, w_o, bl):
    h, hn, qkv, o, ob = saved
    dob = dout.astype(BF16)
    dwo = _mm_dw("sb_dwo", ob, "flat", dob, None, (None, 0, 1))
    do = _mm_rs_dx("sb_out_dx", dob, w_o, 0, "flat", F32)
    dqkv = _sb_attn_bwd2(qkv, o, do, bl)
    dwqkv = _mm_dw("sb_dwqkv", hn, None, dqkv, "flat", (None, 0, 1))
    dhn = _mm_cs_dx("sb_qkv_dx", [(dqkv, w_qkv)], "flat", 0)
    dh, dg = _rmsnorm_bwd("mix_norm_bwd", dout, dhn, h, g)
    return dh, dg, dwqkv, dwo


def _adamw_update(wv, gr, mv, vv):
    c1 = 1.0 / (1.0 - ADAM_B1 ** ADAM_STEP)
    c2 = 1.0 / (1.0 - ADAM_B2 ** ADAM_STEP)
    mn = ADAM_B1 * mv + (1.0 - ADAM_B1) * gr
    vn = ADAM_B2 * vv + (1.0 - ADAM_B2) * gr * gr
    delta = -ADAM_LR * ((mn * c1) / (jnp.sqrt(vn * c2) + ADAM_EPS) + ADAM_WD * wv)
    return delta, mn, vn


def _adamw_small(w, gr, m, v):
    def fn(wv, gv, mv, vv):
        return list(_adamw_update(wv, gv, mv, vv)), []
    return _rows("adamw_small", fn, [w, gr, m, v], [(w.shape[1], F32)] * 3)[0]


def _place():
    x, y, c = lax.axis_index("x"), lax.axis_index("y"), lax.axis_index("c")
    chips = [(1 - x, y), (x, 1 - y), (1 - x, 1 - y)]
    return x, y, c, chips


def _remote(src, dst, send_sem, recv_sem, to):
    return pltpu.make_async_remote_copy(src_ref=src, dst_ref=dst, send_sem=send_sem, recv_sem=recv_sem,
                                        device_id=to, device_id_type=MESH)


def _half(ref, c, rh, lead):
    return ref.at[(slice(None),) * lead + (pl.ds(c * rh, rh),)]


def _allgather_weights(ws):
    n = len(ws)

    def body(*refs):
        ins, outs = refs[:n], refs[n:2 * n]
        send, recv = refs[2 * n:]
        x, y, c, chips = _place()
        own = 2 * x + y
        sibling = (x, y, 1 - c)
        sent = []
        for t in range(n):
            rh = ws[t].shape[1] // 2
            for j, chip in enumerate(chips):
                cp = _remote(_half(ins[t], c, rh, 1), _half(outs[t].at[own], c, rh, 1), send.at[t, j], recv.at[t, j], (*chip, c))
                cp.start()
                sent.append(cp)
        for t in range(n):
            rh = ws[t].shape[1] // 2
            for j, chip in enumerate(chips):
                landed = _half(outs[t].at[2 * chip[0] + chip[1]], c, rh, 1)
                _remote(landed, landed, send.at[t, j], recv.at[t, j], (*chip, c)).wait_recv()
                cp = _remote(landed, landed, send.at[t, 3 + j], recv.at[t, 3 + j], sibling)
                cp.start()
                sent.append(cp)
        for t in range(n):
            rh = ws[t].shape[1] // 2
            for j, chip in enumerate(chips):
                passed = _half(outs[t].at[2 * chip[0] + chip[1]], 1 - c, rh, 1)
                _remote(passed, passed, send.at[t, 3 + j], recv.at[t, 3 + j], sibling).wait_recv()
        for cp in sent:
            cp.wait_send()

    res = pl.pallas_call(
        body, name="allgather_weights", in_specs=[ANY] * n, out_specs=[ANY] * n,
        out_shape=[jax.ShapeDtypeStruct((N_CHIPS,) + w.shape, w.dtype) for w in ws],
        scratch_shapes=[pltpu.SemaphoreType.DMA((n, 6)), pltpu.SemaphoreType.DMA((n, 6))],
    )(*ws)
    own = 2 * lax.axis_index("x") + lax.axis_index("y")
    return [lax.dynamic_update_slice(g, w[None], (own, 0, 0, 0)) for g, w in zip(res, ws)]


def _pair_exchange(gs):
    n = len(gs)

    def body(*refs):
        ins, outs = refs[:n], refs[n:2 * n]
        send, recv = refs[2 * n:]
        x, y, c, _ = _place()
        copies = [_remote(_half(ins[t], 1 - c, gs[t].shape[2] // 2, 2), outs[t], send.at[t], recv.at[t], (x, y, 1 - c))
                  for t in range(n)]
        for cp in copies:
            cp.start()
        for cp in copies:
            cp.wait()

    return pl.pallas_call(
        body, name="grad_pair_exchange", in_specs=[ANY] * n, out_specs=[ANY] * n,
        out_shape=[jax.ShapeDtypeStruct(g.shape[:2] + (g.shape[2] // 2, g.shape[3]), F32) for g in gs],
        scratch_shapes=[pltpu.SemaphoreType.DMA((n,)), pltpu.SemaphoreType.DMA((n,))],
    )(*gs)


def _pair_sum(g, theirs, c_idx):
    n4, ly, r, cc = g.shape
    rh = r // 2
    tm = _tile(rh, 256)
    nt = rh // tm

    def body(c_ref, g_ref, t_ref, o_ref):
        o_ref[...] = (g_ref[...] + t_ref[...]).astype(o_ref.dtype)

    blk = (None, tm, cc)
    grid_spec = pltpu.PrefetchScalarGridSpec(
        num_scalar_prefetch=1, grid=(n4 * ly, nt),
        in_specs=[pl.BlockSpec(blk, lambda a, i, cr: (a, cr[0] * nt + i, 0)), pl.BlockSpec(blk, lambda a, i, cr: (a, i, 0))],
        out_specs=pl.BlockSpec(blk, lambda a, i, cr: (a, i, 0)))
    out = pl.pallas_call(
        body, name="grad_pair_sum", grid_spec=grid_spec, out_shape=jax.ShapeDtypeStruct((n4 * ly, rh, cc), BF16),
        compiler_params=_params("parallel", "parallel"))(c_idx, g.reshape(n4 * ly, r, cc), theirs.reshape(n4 * ly, rh, cc))
    return out.reshape(n4, ly, rh, cc)


def _chip_exchange(ps):
    n = len(ps)

    def body(*refs):
        ins, outs = refs[:n], refs[n:2 * n]
        send, recv = refs[2 * n:]
        x, y, c, chips = _place()
        copies = []
        for t in range(n):
            for j, chip in enumerate(chips):
                copies.append(_remote(ins[t].at[2 * chip[0] + chip[1]], outs[t].at[j], send.at[t, j], recv.at[t, j], (*chip, c)))
        for cp in copies:
            cp.start()
        for cp in copies:
            cp.wait()

    return pl.pallas_call(
        body, name="grad_chip_exchange", in_specs=[ANY] * n, out_specs=[ANY] * n,
        out_shape=[jax.ShapeDtypeStruct((3,) + p.shape[1:], p.dtype) for p in ps],
        scratch_shapes=[pltpu.SemaphoreType.DMA((n, 3)), pltpu.SemaphoreType.DMA((n, 3))],
    )(*ps)


def _chip_sum(p, landed, own_idx):
    _, ly, rh, cc = p.shape
    tm = _tile(rh, 256)

    def body(o_ref, p_ref, a_ref, b_ref, c_ref, out_ref):
        up = lambda r: r[...].astype(F32)
        out_ref[...] = ((up(p_ref) + up(a_ref)) + up(b_ref)) + up(c_ref)

    blk = (None, None, tm, cc)
    slot = lambda j: pl.BlockSpec(blk, lambda l, i, o: (j, l, i, 0))
    grid_spec = pltpu.PrefetchScalarGridSpec(
        num_scalar_prefetch=1, grid=(ly, rh // tm),
        in_specs=[pl.BlockSpec(blk, lambda l, i, o: (o[0], l, i, 0)), slot(0), slot(1), slot(2)],
        out_specs=pl.BlockSpec((None, tm, cc), lambda l, i, o: (l, i, 0)))
    return pl.pallas_call(
        body, name="grad_chip_sum", grid_spec=grid_spec, out_shape=jax.ShapeDtypeStruct((ly, rh, cc), F32),
        compiler_params=_params("parallel", "parallel"))(own_idx, p, landed, landed, landed)


def _pair_swap(halves):
    n = len(halves)

    def body(*refs):
        ins, outs = refs[:n], refs[n:2 * n]
        send, recv = refs[2 * n:]
        x, y, c, _ = _place()
        copies = [_remote(ins[t], outs[t], send.at[t], recv.at[t], (x, y, 1 - c)) for t in range(n)]
        for cp in copies:
            cp.start()
        for cp in copies:
            cp.wait()

    return pl.pallas_call(
        body, name="grad_pair_swap", in_specs=[ANY] * n, out_specs=[ANY] * n,
        out_shape=[jax.ShapeDtypeStruct(h.shape, F32) for h in halves],
        scratch_shapes=[pltpu.SemaphoreType.DMA((n,)), pltpu.SemaphoreType.DMA((n,))],
    )(*halves)


def _adamw_big(w, m, v, mine, theirs, c_idx):
    ly, r, cc = w.shape
    rh = r // 2
    tm = _tile(rh, 256)
    nt = rh // tm

    def body(c_ref, w_ref, m_ref, v_ref, a_ref, b_ref, g_out, d_out, m_out, v_out):
        gr = jnp.where(pl.program_id(1) == c_ref[0], a_ref[...], b_ref[...])
        delta, mn, vn = _adamw_update(w_ref[...], gr, m_ref[...], v_ref[...])
        g_out[...] = gr
        d_out[...] = delta
        m_out[...] = mn
        v_out[...] = vn

    blk = (None, tm, cc)
    full = pl.BlockSpec(blk, lambda l, hc, i, cr: (l, hc * nt + i, 0))
    half = pl.BlockSpec(blk, lambda l, hc, i, cr: (l, i, 0))
    grid_spec = pltpu.PrefetchScalarGridSpec(
        num_scalar_prefetch=1, grid=(ly, 2, nt), in_specs=[full, full, full, half, half], out_specs=[full] * 4)
    sd = jax.ShapeDtypeStruct(w.shape, F32)
    return pl.pallas_call(
        body, name="adamw", grid_spec=grid_spec, out_shape=[sd] * 4,
        compiler_params=_params("parallel", "parallel", "parallel"))(c_idx, w, m, v, mine, theirs)


def _allreduce_small(v):
    rows, w = v.shape

    def body(x_ref, sum_ref, all_ref, send, recv, local):
        x, y, c, chips = _place()
        me, sibling = (x, y, c), (x, y, 1 - c)

        def slot(px, py, pc):
            return all_ref.at[4 * px + 2 * py + pc]

        def copy(k, block, to, src=None):
            return _remote(slot(*block) if src is None else src, slot(*block), send.at[k], recv.at[k], to)

        mine = pltpu.make_async_copy(x_ref, slot(*me), local)
        mine.start()
        first = [copy(0, me, sibling, src=x_ref)]
        first += [copy(1 + j, me, (*chip, c), src=x_ref) for j, chip in enumerate(chips)]
        for cp in first:
            cp.start()
        passed = [copy(4 + j, (*chip, c), sibling) for j, chip in enumerate(chips)]
        for j, chip in enumerate(chips):
            copy(1 + j, (*chip, c), me).wait_recv()
            passed[j].start()
        copy(0, sibling, me).wait_recv()
        for j, chip in enumerate(chips):
            copy(4 + j, (*chip, 1 - c), me).wait_recv()
        for cp in first + passed:
            cp.wait_send()
        mine.wait()
        tot = all_ref[0]
        for k in range(1, N_DEV):
            tot = tot + all_ref[k]
        sum_ref[...] = tot

    vm = pl.BlockSpec(memory_space=pltpu.VMEM)
    return pl.pallas_call(
        body, name="allreduce_small", in_specs=[vm], out_specs=[vm, vm],
        out_shape=[jax.ShapeDtypeStruct((rows, w), F32), jax.ShapeDtypeStruct((N_DEV, rows, w), F32)],
        scratch_shapes=[pltpu.SemaphoreType.DMA((7,)), pltpu.SemaphoreType.DMA((7,)), pltpu.SemaphoreType.DMA],
        compiler_params=pltpu.CompilerParams(vmem_limit_bytes=VMEM_LIMIT),
    )(v)[0]


BIG = ["ffn1_w1", "ffn1_w3", "ffn1_w2", "ffn2_w1", "ffn2_w3", "ffn2_w2", "ple_proj", "ple_gate",
       "s5_w_in", "s5_w_glu", "sb_w_qkv", "sb_w_o"]
SMALL = ["ffn1_norm", "mix_norm", "ffn2_norm", "ple_norm", "s5_a_re", "s5_a_im", "s5_log_dt", "s5_b_re", "s5_b_im",
         "s5_c_re", "s5_c_im", "s5_d", "final_norm"]
ORDER = ["ffn1_norm", "ffn1_w1", "ffn1_w3", "ffn1_w2", "mix_norm", "ffn2_norm", "ffn2_w1", "ffn2_w3", "ffn2_w2",
         "ple_norm", "ple_proj", "ple_gate", "s5_w_in", "s5_a_re", "s5_a_im", "s5_log_dt", "s5_b_re", "s5_b_im",
         "s5_c_re", "s5_c_im", "s5_d", "s5_w_glu", "sb_w_qkv", "sb_w_o", "final_norm"]


def _pack(arrays):
    flat = jnp.concatenate([a.reshape(-1) for a in arrays])
    pad = (-flat.shape[0]) % 1024
    return jnp.pad(flat, (0, pad)).reshape(-1, 128)


def _unpack(packed, like):
    flat = packed.reshape(-1)
    out, off = [], 0
    for a in like:
        out.append(flat[off:off + a.size].reshape(a.shape))
        off += a.size
    return out


def _fwd_bwd(x, p, target, w, gathered):
    bl, l, d = x.shape
    t = bl * l
    depth = w["ffn1_norm"].shape[0]
    s5_ops, s5_vjp = jax.vjp(_s5_prep, w["s5_a_re"][0], w["s5_a_im"][0], w["s5_log_dt"][0], w["s5_b_re"][0],
                             w["s5_b_im"][0], w["s5_c_re"][0], w["s5_c_im"][0])

    h = x.reshape(t, d)
    p2 = [p[i].reshape(t, p.shape[-1]).astype(BF16) for i in range(depth)]
    saved = []
    for i in range(depth):
        norm = lambda name: w[name][i:i + 1]
        h, s1 = _ffn_fwd(h, norm("ffn1_norm"), gathered["ffn1_w1"], gathered["ffn1_w3"], gathered["ffn1_w2"], i)
        if i % 2 == 0:
            h, s2 = _s5_fwd(h, norm("mix_norm"), s5_ops, w["s5_d"][i // 2:i // 2 + 1], gathered["s5_w_in"], gathered["s5_w_glu"], bl)
        else:
            h, s2 = _sb_fwd(h, norm("mix_norm"), gathered["sb_w_qkv"], gathered["sb_w_o"], bl)
        h, s3 = _ffn_fwd(h, norm("ffn2_norm"), gathered["ffn2_w1"], gathered["ffn2_w3"], gathered["ffn2_w2"], i)
        h, s4 = _ple_fwd(h, norm("ple_norm"), p2[i], gathered["ple_proj"], gathered["ple_gate"], i)
        saved.append((s1, s2, s3, s4))

    loss, dh, dfinal = _head(h, w["final_norm"].reshape(1, d), target.reshape(t, d))

    big = {k: None for k in BIG}
    small = {k: [None] * w[k].shape[0] if w[k].ndim > 1 else None for k in SMALL}
    small["final_norm"] = dfinal.reshape(d)
    for i in reversed(range(depth)):
        norm = lambda name: w[name][i:i + 1]
        slots = lambda *names: [(big[k], i, depth) for k in names]
        s1, s2, s3, s4 = saved[i]
        dh, dg, big["ple_proj"], big["ple_gate"] = _ple_bwd(
            dh, s4, norm("ple_norm"), p2[i], gathered["ple_proj"], gathered["ple_gate"], i, slots("ple_proj", "ple_gate"))
        small["ple_norm"][i] = dg[0]
        dh, dg, big["ffn2_w1"], big["ffn2_w3"], big["ffn2_w2"] = _ffn_bwd(
            dh, s3, norm("ffn2_norm"), gathered["ffn2_w1"], gathered["ffn2_w3"], gathered["ffn2_w2"], i,
            slots("ffn2_w1", "ffn2_w3", "ffn2_w2"))
        small["ffn2_norm"][i] = dg[0]
        if i % 2 == 0:
            dh, dg, big["s5_w_in"], big["s5_w_glu"], dd, dops = _s5_bwd(
                dh, s2, norm("mix_norm"), s5_ops, w["s5_d"][i // 2:i // 2 + 1], gathered["s5_w_in"], gathered["s5_w_glu"], bl)
            small["s5_d"][0] = dd[0]
            raw = s5_vjp(dops)
            for name, gr in zip(["s5_a_re", "s5_a_im", "s5_log_dt", "s5_b_re", "s5_b_im", "s5_c_re", "s5_c_im"], raw):
                small[name][0] = gr
        else:
            dh, dg, big["sb_w_qkv"], big["sb_w_o"] = _sb_bwd(dh, s2, norm("mix_norm"), gathered["sb_w_qkv"], gathered["sb_w_o"], bl)
        small["mix_norm"][i] = dg[0]
        dh, dg, big["ffn1_w1"], big["ffn1_w3"], big["ffn1_w2"] = _ffn_bwd(
            dh, s1, norm("ffn1_norm"), gathered["ffn1_w1"], gathered["ffn1_w3"], gathered["ffn1_w2"], i,
            slots("ffn1_w1", "ffn1_w3", "ffn1_w2"))
        small["ffn1_norm"][i] = dg[0]
    small_list = [jnp.stack(small[k]) if isinstance(small[k], list) else small[k] for k in SMALL]
    return loss, dh.reshape(bl, l, d), big, small_list


def _step(x, p, target, w, m, v):
    gathered = dict(zip(BIG, _allgather_weights([_to_bf16(w[k]) for k in BIG])))
    loss, grad_x, big, small_list = _fwd_bwd(x, p, target, w, gathered)

    c_idx = lax.axis_index("c").astype(jnp.int32).reshape(1)
    own_idx = (2 * lax.axis_index("x") + lax.axis_index("y")).astype(jnp.int32).reshape(1)
    partial = [big[k] for k in BIG]
    pair = [_pair_sum(g, t, c_idx) for g, t in zip(partial, _pair_exchange(partial))]
    mine = [_chip_sum(pr, ld, own_idx) for pr, ld in zip(pair, _chip_exchange(pair))]
    theirs = _pair_swap(mine)
    out_g, out_d, out_m, out_v = {}, {}, {}, {}
    for k, a, b in zip(BIG, mine, theirs):
        out_g[k], out_d[k], out_m[k], out_v[k] = _adamw_big(w[k], m[k], v[k], a, b, c_idx)

    like = [w[k] for k in SMALL]
    pad = [jnp.zeros((1,), F32)]
    g_small = _allreduce_small(_pack(small_list + [loss.reshape(1)]))
    packed = (g_small,) + tuple(_adamw_small(_pack(like + pad), g_small, _pack([m[k] for k in SMALL] + pad),
                                             _pack([v[k] for k in SMALL] + pad)))
    for dst, pk in zip((out_g, out_d, out_m, out_v), packed):
        dst.update(dict(zip(SMALL, _unpack(pk, like))))
    loss = g_small.reshape(-1)[sum(a.size for a in like)]
    return (loss, grad_x, *[out_g[k] for k in ORDER], *[out_d[k] for k in ORDER],
            *[out_m[k] for k in ORDER], *[out_v[k] for k in ORDER])


def kernel(x, p, ffn1_norm, ffn1_w1, ffn1_w3, ffn1_w2, mix_norm, ffn2_norm, ffn2_w1, ffn2_w3, ffn2_w2, ple_norm, ple_proj, ple_gate, s5_w_in, s5_a_re, s5_a_im, s5_log_dt, s5_b_re, s5_b_im, s5_c_re, s5_c_im, s5_d, s5_w_glu, sb_w_qkv, sb_w_o, final_norm, loss_target, m_ffn1_norm, m_ffn1_w1, m_ffn1_w3, m_ffn1_w2, m_mix_norm, m_ffn2_norm, m_ffn2_w1, m_ffn2_w3, m_ffn2_w2, m_ple_norm, m_ple_proj, m_ple_gate, m_s5_w_in, m_s5_a_re, m_s5_a_im, m_s5_log_dt, m_s5_b_re, m_s5_b_im, m_s5_c_re, m_s5_c_im, m_s5_d, m_s5_w_glu, m_sb_w_qkv, m_sb_w_o, m_final_norm, v_ffn1_norm, v_ffn1_w1, v_ffn1_w3, v_ffn1_w2, v_mix_norm, v_ffn2_norm, v_ffn2_w1, v_ffn2_w3, v_ffn2_w2, v_ple_norm, v_ple_proj, v_ple_gate, v_s5_w_in, v_s5_a_re, v_s5_a_im, v_s5_log_dt, v_s5_b_re, v_s5_b_im, v_s5_c_re, v_s5_c_im, v_s5_d, v_s5_w_glu, v_sb_w_qkv, v_sb_w_o, v_final_norm):
    args = dict(locals())
    w = {k: args[k] for k in ORDER}
    m = {k: args["m_" + k] for k in ORDER}
    v = {k: args["v_" + k] for k in ORDER}
    return _step(x, p, loss_target, w, m, v)
```

```python
import functools
import math

import jax
import jax.numpy as jnp
from jax import lax
from jax.experimental import pallas as pl
from jax.experimental.pallas import tpu as pltpu

F32 = jnp.float32
BF16 = jnp.bfloat16
MESH = pl.DeviceIdType.MESH

N_CHIPS = 4
N_DEV = 8
RMS_EPS = 1e-6
S5_GROUP = 16
S5_STATE = 64
S5_CHUNK = 16
SB_HEAD_DIM = 64
SB_BLOCK = 128
SB_CUT = -104.0
SB_UNROLL = 3
ADAM_LR, ADAM_B1, ADAM_B2, ADAM_EPS, ADAM_WD, ADAM_STEP = 0.001, 0.9, 0.999, 1e-08, 0.01, 10
VMEM_LIMIT = 48 * 1024 * 1024

NN = (((1,), (0,)), ((), ()))
NT = (((1,), (1,)), ((), ()))
TN = (((0,), (0,)), ((), ()))

ANY = pl.BlockSpec(memory_space=pl.ANY)


def _tile(n, target):
    if n <= target:
        return n
    for t in range(target - target % 8, 7, -8):
        if n % t == 0:
            return t
    raise ValueError(f"no row tile for {n}")


def _params(*semantics):
    return pltpu.CompilerParams(dimension_semantics=semantics, vmem_limit_bytes=VMEM_LIMIT)


def _sigmoid(v):
    return 1.0 / (1.0 + jnp.exp(-v))


def _gemm(name, grid, operands, in_specs, groups, acc_shapes, out_shapes, out_specs, epilogue, reduce_axis=None, aliases=None):
    n_in, n_out = len(operands), len(out_shapes)
    n_red = None if reduce_axis is None else grid[reduce_axis]

    def body(*refs):
        ins, outs, accs = refs[:n_in], refs[n_in:n_in + n_out], refs[n_in + n_out:]

        def products():
            res = []
            for terms in groups:
                tot = None
                for ia, ib, dims in terms:
                    d = lax.dot_general(ins[ia][...], ins[ib][...], dims, preferred_element_type=F32)
                    tot = d if tot is None else tot + d
                res.append(tot)
            return res

        def finish(vals):
            for o, v in zip(outs, epilogue(vals, ins)):
                o[...] = v.astype(o.dtype)

        if reduce_axis is None:
            finish(products())
        else:
            k = pl.program_id(reduce_axis)

            @pl.when(k == 0)
            def _():
                for a in accs:
                    a[...] = jnp.zeros_like(a)

            for a, d in zip(accs, products()):
                a[...] += d

            @pl.when(k == n_red - 1)
            def _():
                finish([a[...] for a in accs])

    scratch = [] if reduce_axis is None else [pltpu.VMEM(s, F32) for s in acc_shapes]
    sem = tuple("arbitrary" if i == reduce_axis else "parallel" for i in range(len(grid)))
    return pl.pallas_call(
        body, name=name, grid=grid, in_specs=in_specs, out_specs=out_specs, out_shape=out_shapes,
        scratch_shapes=scratch, input_output_aliases=aliases or {}, compiler_params=_params(*sem))(*operands)


def _ident(vals, ins):
    return vals


def _act_spec(layout, tm, cs, pos):
    if layout == "sm":
        return pl.BlockSpec((None, tm, cs), lambda *g: (pos(*g)[1], pos(*g)[0], 0))
    return pl.BlockSpec((tm, cs), lambda *g: pos(*g))


def _act_shape(layout, t, cs, dtype):
    return jax.ShapeDtypeStruct((N_CHIPS, t, cs) if layout == "sm" else (t, N_CHIPS * cs), dtype)


def _w_spec(w, layer, pos_k):
    _, _, r, c = w.shape
    return pl.BlockSpec((None, None, r, c), lambda *g: (pos_k(*g), layer, 0, 0))


def _mm_cs(name, x, w, layer, out_layout, out_dtype, tm=512):
    t, kd = x.shape
    cs = w.shape[3]
    tm = _tile(t, tm)
    return _gemm(
        name, (N_CHIPS, t // tm), [x, w],
        [pl.BlockSpec((tm, kd), lambda k, i: (i, 0)), _w_spec(w, layer, lambda k, i: k)],
        [[(0, 1, NN)]], None, [_act_shape(out_layout, t, cs, out_dtype)],
        [_act_spec(out_layout, tm, cs, lambda k, i: (i, k))], _ident)[0]


def _mm_rs(name, xs, layout, w, layer, res=None, alpha=1.0, out_dtype=F32, tm=1024):
    ks, n = w.shape[2], w.shape[3]
    t = xs.shape[1] if layout == "sm" else xs.shape[0]
    tm = _tile(t, tm)
    operands = [xs, w] + ([] if res is None else [res])
    specs = [_act_spec(layout, tm, ks, lambda i, k: (i, k)), _w_spec(w, layer, lambda i, k: k)]
    if res is not None:
        specs.append(pl.BlockSpec((tm, n), lambda i, k: (i, 0)))

    def epilogue(vals, ins):
        y = alpha * vals[0]
        return [y if res is None else ins[2][...] + y]

    return _gemm(
        name, (t // tm, N_CHIPS), operands, specs, [[(0, 1, NN)]], [(tm, n)],
        [jax.ShapeDtypeStruct((t, n), out_dtype)], [pl.BlockSpec((tm, n), lambda i, k: (i, 0))],
        epilogue, reduce_axis=1)[0]


def _mm_cs_dx(name, pairs, layout, layer, tm=1024):
    w0 = pairs[0][1]
    kd, cs = w0.shape[2], w0.shape[3]
    dy0 = pairs[0][0]
    t = dy0.shape[1] if layout == "sm" else dy0.shape[0]
    tm = _tile(t, tm)
    operands, specs, terms = [], [], []
    for dy, w in pairs:
        terms.append((len(operands), len(operands) + 1, NT))
        operands += [dy, w]
        specs += [_act_spec(layout, tm, cs, lambda i, k: (i, k)), _w_spec(w, layer, lambda i, k: k)]
    return _gemm(
        name, (t // tm, N_CHIPS), operands, specs, [terms], [(tm, kd)],
        [jax.ShapeDtypeStruct((t, kd), F32)], [pl.BlockSpec((tm, kd), lambda i, k: (i, 0))],
        _ident, reduce_axis=1)[0]


def _mm_rs_dx(name, dy, w, layer, out_layout, out_dtype, tm=512):
    t, n = dy.shape
    ks = w.shape[2]
    tm = _tile(t, tm)
    return _gemm(
        name, (N_CHIPS, t // tm), [dy, w],
        [pl.BlockSpec((tm, n), lambda k, i: (i, 0)), _w_spec(w, layer, lambda k, i: k)],
        [[(0, 1, NT)]], None, [_act_shape(out_layout, t, ks, out_dtype)],
        [_act_spec(out_layout, tm, ks, lambda k, i: (i, k))], _ident)[0]


def _mm_dw(name, x, x_layout, dy, dy_layout, slot, alpha=1.0, tk=2048):
    stack, layer, layers = slot
    if x_layout is None:
        t, rows = x.shape
        cols = dy.shape[2] if dy_layout == "sm" else dy.shape[1] // N_CHIPS
        tk = _tile(t, tk)
        xspec = pl.BlockSpec((tk, rows), lambda k, j: (j, 0))
        yspec = _act_spec(dy_layout, tk, cols, lambda k, j: (j, k))
    else:
        t, cols = dy.shape
        rows = x.shape[2] if x_layout == "sm" else x.shape[1] // N_CHIPS
        tk = _tile(t, tk)
        xspec = _act_spec(x_layout, tk, rows, lambda k, j: (j, k))
        yspec = pl.BlockSpec((tk, cols), lambda k, j: (j, 0))
    operands, specs = [x, dy], [xspec, yspec]
    if stack is not None:
        operands.append(stack)
        specs.append(ANY)
    return _gemm(
        name, (N_CHIPS, t // tk), operands, specs, [[(0, 1, TN)]], [(rows, cols)],
        [jax.ShapeDtypeStruct((N_CHIPS, layers, rows, cols), F32)],
        [pl.BlockSpec((None, None, rows, cols), lambda k, j: (k, layer, 0, 0))],
        lambda vals, ins: [alpha * vals[0]], reduce_axis=1, aliases=None if stack is None else {2: 0})[0]


def _rows(name, fn, ins, outs, accs=(), tm=256):
    t = ins[0].shape[0]
    tm = _tile(t, tm)
    n_in, n_out, n_acc = len(ins), len(outs), len(accs)
    in_specs = []
    for a in ins:
        if a.shape[0] == t:
            in_specs.append(pl.BlockSpec((tm, a.shape[1]), lambda i: (i, 0)))
        else:
            in_specs.append(pl.BlockSpec(a.shape, lambda i: (0, 0)))
    out_shape = [jax.ShapeDtypeStruct((t, c), d) for c, d in outs] + [jax.ShapeDtypeStruct(s, F32) for s in accs]
    out_specs = [pl.BlockSpec((tm, c), lambda i: (i, 0)) for c, _ in outs] + [pl.BlockSpec(s, lambda i: (0, 0)) for s in accs]

    def body(*refs):
        i = pl.program_id(0)
        row_vals, acc_vals = fn(*[r[...] for r in refs[:n_in]])
        for o, v in zip(refs[n_in:n_in + n_out], row_vals):
            o[...] = v.astype(o.dtype)
        acc_refs = refs[n_in + n_out:]
        if n_acc:
            @pl.when(i == 0)
            def _():
                for a in acc_refs:
                    a[...] = jnp.zeros_like(a)

            for a, v in zip(acc_refs, acc_vals):
                a[...] += v

    res = pl.pallas_call(
        body, name=name, grid=(t // tm,), in_specs=in_specs, out_specs=out_specs, out_shape=out_shape,
        compiler_params=_params("arbitrary" if n_acc else "parallel"))(*ins)
    return res[:n_out], res[n_out:]


def _to_bf16(a):
    def fn(x):
        return [x], []
    return _rows("weights_bf16", fn, [a.reshape(-1, a.shape[-1])], [(a.shape[-1], BF16)], tm=512)[0][0].reshape(a.shape)


def _rms_stats(x):
    return lax.rsqrt(jnp.mean(x * x, axis=-1, keepdims=True) + RMS_EPS)


def _rmsnorm(name, h, g):
    def fn(x, gg):
        return [x * _rms_stats(x) * gg], []
    return _rows(name, fn, [h, g], [(h.shape[1], BF16)])[0][0]


def _rms_bwd_math(dn, x, g):
    r = _rms_stats(x)
    xhat = x * r
    dxh = dn * g
    dx = r * (dxh - xhat * jnp.mean(dxh * xhat, axis=-1, keepdims=True))
    return dx, jnp.sum(dn * xhat, axis=0, keepdims=True)


def _rmsnorm_bwd(name, dres, dn, h, g):
    def fn(dr, d, x, gg):
        dx, dg = _rms_bwd_math(d, x, gg)
        return [dr + dx], [dg]
    (dh,), (dg,) = _rows(name, fn, [dres, dn, h, g], [(h.shape[1], F32)], [(1, h.shape[1])])
    return dh, dg


def _ffn_fwd(h, g, w1, w3, w2, layer, tm=512):
    t, d = h.shape
    fs = w1.shape[3]
    n = _rmsnorm("ffn_norm", h, g)
    tm = _tile(t, tm)

    def up(vals, ins):
        a, b = vals
        return [a, b, a * _sigmoid(a) * b]

    sm = _act_shape("sm", t, fs, BF16)
    osp = _act_spec("sm", tm, fs, lambda k, i: (i, k))
    a, b, s = _gemm(
        "ffn_up", (N_CHIPS, t // tm), [n, w1, w3],
        [pl.BlockSpec((tm, d), lambda k, i: (i, 0)), _w_spec(w1, layer, lambda k, i: k), _w_spec(w3, layer, lambda k, i: k)],
        [[(0, 1, NN)], [(0, 2, NN)]], None, [sm, sm, sm], [osp, osp, osp], up)
    out = _mm_rs("ffn_down", s, "sm", w2, layer, res=h, alpha=0.5)
    return out, (h, n, a, b, s)


def _ffn_bwd(dout, saved, g, w1, w3, w2, layer, slots, tm=512):
    h, n, a, b, s = saved
    t, d = h.shape
    fs = w1.shape[3]
    tm = _tile(t, tm)
    dob = dout.astype(BF16)

    def down(vals, ins):
        ds = 0.5 * vals[0]
        av, bv = ins[2][...].astype(F32), ins[3][...].astype(F32)
        sg = _sigmoid(av)
        return [ds * bv * sg * (1.0 + av * (1.0 - sg)), ds * av * sg]

    sm = _act_shape("sm", t, fs, BF16)
    asp = _act_spec("sm", tm, fs, lambda k, i: (i, k))
    da, db = _gemm(
        "ffn_down_dx", (N_CHIPS, t // tm), [dob, w2, a, b],
        [pl.BlockSpec((tm, d), lambda k, i: (i, 0)), _w_spec(w2, layer, lambda k, i: k), asp, asp],
        [[(0, 1, NT)]], None, [sm, sm], [asp, asp], down)
    dw2 = _mm_dw("ffn_dw2", s, "sm", dob, None, slots[2], alpha=0.5)
    dw1 = _mm_dw("ffn_dw1", n, None, da, "sm", slots[0])
    dw3 = _mm_dw("ffn_dw3", n, None, db, "sm", slots[1])
    dn = _mm_cs_dx("ffn_up_dx", [(da, w1), (db, w3)], "sm", layer)
    dh, dg = _rmsnorm_bwd("ffn_norm_bwd", dout, dn, h, g)
    return dh, dg, dw1, dw3, dw2


def _ple_fwd(h, g, p2, wproj, wgate, layer):
    n = _rmsnorm("ple_norm", h, g)
    gl = _mm_rs("ple_gate", n, "flat", wgate, layer)
    pp = _mm_cs("ple_proj", p2, wproj, layer, "flat", F32)

    def fn(hh, gg, q):
        return [hh + q * _sigmoid(gg)], []
    out = _rows("ple_mix", fn, [h, gl, pp], [(h.shape[1], F32)])[0][0]
    return out, (h, n, gl, pp)


def _ple_bwd(dout, saved, g, p2, wproj, wgate, layer, slots):
    h, n, gl, pp = saved
    d = h.shape[1]

    def fn(do, gg, q):
        sg = _sigmoid(gg)
        return [do * sg, do * q * sg * (1.0 - sg)], []
    (dpp, dgl), _ = _rows("ple_mix_bwd", fn, [dout, gl, pp], [(d, BF16), (d, BF16)])
    dwproj = _mm_dw("ple_dwproj", p2, None, dpp, "flat", slots[0])
    dwgate = _mm_dw("ple_dwgate", n, "flat", dgl, None, slots[1])
    dn = _mm_rs_dx("ple_gate_dx", dgl, wgate, layer, "flat", F32)
    dh, dg = _rmsnorm_bwd("ple_norm_bwd", dout, dn, h, g)
    return dh, dg, dwproj, dwgate


def _head(h, g, target):
    d = h.shape[1]

    def fn(x, gg, tg):
        y = x * _rms_stats(x) * gg
        err = y - tg
        dy = err * (1.0 / d)
        dx, dg = _rms_bwd_math(dy, x, gg)
        loss = 0.5 * jnp.sum(jnp.sum(err * err, axis=-1, keepdims=True) * (1.0 / d), axis=0, keepdims=True)
        return [dx], [dg, jnp.broadcast_to(loss, (1, 128))]
    (dh,), (dg, loss) = _rows("loss_head", fn, [h, g, target], [(d, F32)], [(1, d), (1, 128)])
    return loss[0, 0], dh, dg


S5_LANES = 2 * S5_STATE
S5_GB = 128 // S5_GROUP


def _s5_prep(a_re, a_im, log_dt, b_re, b_im, c_re, c_im):
    c, gb = S5_CHUNK, S5_GB
    g = a_re.shape[0]
    nb = g // gb
    lam_re = jnp.minimum(a_re, -1e-4)
    lam_im = a_im
    dt = jnp.exp(log_dt)[:, None, None]
    ks = jnp.arange(c + 1, dtype=F32)
    mag = jnp.exp(lam_re[..., None] * dt * ks)
    ph = lam_im[..., None] * dt * ks
    pw_re, pw_im = mag * jnp.cos(ph), mag * jnp.sin(ph)
    den = lam_re * lam_re + lam_im * lam_im
    nr, ni = pw_re[..., 1] - 1.0, pw_im[..., 1]
    fr = (nr * lam_re + ni * lam_im) / den
    fi = (ni * lam_re - nr * lam_im) / den
    bb_re = fr[..., None] * b_re - fi[..., None] * b_im
    bb_im = fr[..., None] * b_im + fi[..., None] * b_re
    ct_re, ct_im = c_re.transpose(0, 2, 1), c_im.transpose(0, 2, 1)
    ca_re = ct_re[:, :, None, :] * pw_re[..., None] - ct_im[:, :, None, :] * pw_im[..., None]
    ca_im = ct_re[:, :, None, :] * pw_im[..., None] + ct_im[:, :, None, :] * pw_re[..., None]
    hp = lax.Precision.HIGHEST
    kern = (jnp.einsum("gpj,gpkh->gkjh", bb_re, ca_re[:, :, :c], precision=hp)
            - jnp.einsum("gpj,gpkh->gkjh", bb_im, ca_im[:, :, :c], precision=hp))
    rev_re = pw_re[:, :, :c][:, :, ::-1].transpose(0, 2, 1)
    rev_im = pw_im[:, :, :c][:, :, ::-1].transpose(0, 2, 1)
    bt_re, bt_im = bb_re.transpose(0, 2, 1), bb_im.transpose(0, 2, 1)
    wn_re = rev_re[:, :, None, :] * bt_re[:, None] - rev_im[:, :, None, :] * bt_im[:, None]
    wn_im = rev_re[:, :, None, :] * bt_im[:, None] + rev_im[:, :, None, :] * bt_re[:, None]
    wn = jnp.concatenate([wn_re, wn_im], axis=-1)
    wo = jnp.concatenate([ca_re[:, :, 1:].transpose(0, 2, 3, 1), -ca_im[:, :, 1:].transpose(0, 2, 3, 1)], axis=-1)

    def blocks(x):
        return x.reshape(nb, gb, c, S5_GROUP, x.shape[3]).transpose(0, 2, 1, 3, 4).reshape(nb, c, gb * S5_GROUP, x.shape[3])

    ar, ai = pw_re[..., c], pw_im[..., c]
    return (jnp.tile(blocks(kern), (1, 1, 1, gb)), blocks(wn), blocks(wo),
            jnp.concatenate([ar, ar], axis=1), jnp.concatenate([-ai, ai], axis=1))


def _step_rows(ref, tau, n):
    return ref[pl.ds(tau, n, stride=S5_CHUNK), :].astype(BF16)


def _cat_groups(ref, dtype):
    return jnp.concatenate([ref[:, j, :] for j in range(S5_GB)], axis=1).astype(dtype)


def _cat_steps(ref, n):
    return jnp.concatenate([_step_rows(ref, tau, n) for tau in range(S5_CHUNK)], axis=1)


def _stack_steps(ref, n):
    return jnp.concatenate([_step_rows(ref, tau, n) for tau in range(S5_CHUNK)], axis=0)


def _cat_ops(ref, axis, reverse=False):
    order = range(S5_CHUNK - 1, -1, -1) if reverse else range(S5_CHUNK)
    return jnp.concatenate([ref[k] for k in order], axis=axis)


def _row_group(rows, lanes):
    row = (lax.broadcasted_iota(jnp.int32, (rows, lanes), 0) // S5_GROUP) % S5_GB
    lane = (lax.broadcasted_iota(jnp.int32, (rows, lanes), 1) // S5_GROUP) % S5_GB
    return row, lane


def _own_group(x):
    row, lane = _row_group(*x.shape)
    return jnp.where(row == lane, x, jnp.zeros_like(x))


def _spread(x):
    row, _ = _row_group(*x.shape)
    return jnp.concatenate([jnp.where(row == j, x, jnp.zeros_like(x)) for j in range(S5_GB)], axis=1)


def _gather_own(x):
    row, _ = _row_group(x.shape[0], S5_LANES)
    out = jnp.zeros((x.shape[0], S5_LANES), x.dtype)
    for j in range(S5_GB):
        out = out + jnp.where(row == j, x[:, j * S5_LANES:(j + 1) * S5_LANES], 0.0)
    return out


def _s5_specs(t, d):
    nct, g = t // S5_CHUNK, d // S5_GROUP
    tok = pl.BlockSpec((t, 128), lambda i: (0, i))
    st = pl.BlockSpec((nct, S5_GB, S5_LANES), lambda i: (0, i, 0))
    op = lambda w: pl.BlockSpec((None,) + w.shape[1:], lambda i: (i, 0, 0, 0))
    return nct, g, tok, st, op


def _s5_chunk_fwd(u, bd, bn):
    t, d = u.shape
    nct, g, tok, st, op = _s5_specs(t, d)
    c = S5_CHUNK

    def body(u_ref, bd_ref, bn_ref, y_ref, s_ref):
        ucat = _cat_steps(u_ref, nct)
        sloc = jnp.dot(ucat, _spread(_cat_ops(bn_ref, 0)), preferred_element_type=F32)
        for j in range(S5_GB):
            s_ref[:, j, :] = sloc[:, j * S5_LANES:(j + 1) * S5_LANES]
        lags = _own_group(_cat_ops(bd_ref, 0, reverse=True))
        for tt in range(c):
            y_ref[pl.ds(tt, nct, stride=c), :] = jnp.dot(ucat[:, :(tt + 1) * 128], lags[(c - 1 - tt) * 128:, :],
                                                         preferred_element_type=F32)

    return pl.pallas_call(
        body, name="s5_chunk", grid=(d // 128,), in_specs=[tok, op(bd), op(bn)], out_specs=[tok, st],
        out_shape=[jax.ShapeDtypeStruct((t, d), F32), jax.ShapeDtypeStruct((nct, g, S5_LANES), F32)],
        compiler_params=_params("parallel"))(u, bd, bn)


def _s5_state_out(sprev, co, yin):
    t, d = yin.shape
    nct, g, tok, st, op = _s5_specs(t, d)
    c = S5_CHUNK

    def body(s_ref, co_ref, yi_ref, y_ref):
        ys = lax.dot_general(_cat_groups(s_ref, BF16), _spread(_cat_ops(co_ref, 0)), NT,
                             preferred_element_type=F32)
        for tt in range(c):
            rows = pl.ds(tt, nct, stride=c)
            y_ref[rows, :] = yi_ref[rows, :] + ys[:, tt * 128:(tt + 1) * 128]

    return pl.pallas_call(
        body, name="s5_state_out", grid=(d // 128,), in_specs=[st, op(co), tok], out_specs=tok,
        out_shape=jax.ShapeDtypeStruct((t, d), F32), compiler_params=_params("parallel"))(sprev, co, yin)


def _s5_state_out_dx(dyb, co):
    t, d = dyb.shape
    nct, g, tok, st, op = _s5_specs(t, d)
    c = S5_CHUNK

    def body(dy_ref, co_ref, ds_ref):
        acc = jnp.dot(_cat_steps(dy_ref, nct), _spread(_cat_ops(co_ref, 0)), preferred_element_type=F32)
        for j in range(S5_GB):
            ds_ref[:, j, :] = acc[:, j * S5_LANES:(j + 1) * S5_LANES]

    return pl.pallas_call(
        body, name="s5_state_out_dx", grid=(d // 128,), in_specs=[tok, op(co)], out_specs=st,
        out_shape=jax.ShapeDtypeStruct((nct, g, S5_LANES), F32), compiler_params=_params("parallel"))(dyb, co)


def _s5_chunk_dx(dyb, dsloc, bd, bn, skip):
    t, d = dyb.shape
    nct, g, tok, st, op = _s5_specs(t, d)
    c = S5_CHUNK

    def body(dy_ref, ds_ref, bd_ref, bn_ref, sk_ref, du_ref):
        dus = lax.dot_general(_cat_groups(ds_ref, BF16), _spread(_cat_ops(bn_ref, 0)), NT, preferred_element_type=F32)
        dycat = _cat_steps(dy_ref, nct)
        lags = _own_group(_cat_ops(bd_ref, 1))
        for tau in range(c):
            rows = pl.ds(tau, nct, stride=c)
            du_ref[rows, :] = (sk_ref[rows, :] + dus[:, tau * 128:(tau + 1) * 128]
                               + lax.dot_general(dycat[:, tau * 128:], lags[:, :(c - tau) * 128], NT,
                                                 preferred_element_type=F32))

    return pl.pallas_call(
        body, name="s5_chunk_dx", grid=(d // 128,), in_specs=[tok, st, op(bd), op(bn), tok], out_specs=tok,
        out_shape=jax.ShapeDtypeStruct((t, d), F32), compiler_params=_params("parallel"))(dyb, dsloc, bd, bn, skip)


def _s5_chunk_dw(u, dyb, dsloc, bd, bn):
    t, d = u.shape
    nct, g, tok, st, op = _s5_specs(t, d)
    c = S5_CHUNK

    def body(u_ref, dy_ref, ds_ref, dbd_ref, dbn_ref):
        dbn = _gather_own(lax.dot_general(_cat_steps(u_ref, nct), _cat_groups(ds_ref, BF16), TN,
                                          preferred_element_type=F32))
        for tau in range(c):
            dbn_ref[tau] = dbn[tau * 128:(tau + 1) * 128, :]
        ustk, dystk = _stack_steps(u_ref, nct), _stack_steps(dy_ref, nct)
        for k in range(c):
            dbd_ref[k] = _own_group(lax.dot_general(ustk[:(c - k) * nct], dystk[k * nct:], TN,
                                                    preferred_element_type=F32))

    return pl.pallas_call(
        body, name="s5_chunk_dw", grid=(d // 128,), in_specs=[tok, tok, st], out_specs=[op(bd), op(bn)],
        out_shape=[jax.ShapeDtypeStruct(bd.shape, F32), jax.ShapeDtypeStruct(bn.shape, F32)],
        compiler_params=_params("parallel"))(u, dyb, dsloc)


def _s5_state_out_dw(sprev, dyb, co):
    t, d = dyb.shape
    nct, g, tok, st, op = _s5_specs(t, d)
    c = S5_CHUNK

    def body(s_ref, dy_ref, dco_ref):
        dco = _gather_own(lax.dot_general(_cat_steps(dy_ref, nct), _cat_groups(s_ref, BF16), TN,
                                          preferred_element_type=F32))
        for tt in range(c):
            dco_ref[tt] = dco[tt * 128:(tt + 1) * 128, :]

    return pl.pallas_call(
        body, name="s5_state_out_dw", grid=(d // 128,), in_specs=[st, tok], out_specs=op(co),
        out_shape=jax.ShapeDtypeStruct(co.shape, F32), compiler_params=_params("parallel"))(sprev, dyb)


def _s5_scan_fwd(sloc, m1, m2):
    bl, nc, g, w = sloc.shape

    def body(s_ref, m1_ref, m2_ref, o_ref):
        a1, a2 = m1_ref[...], m2_ref[...]

        def step(c, states):
            new = []
            for b, s in enumerate(states):
                o_ref[b, c] = s
                new.append(a1 * s + a2 * pltpu.roll(s, S5_STATE, 1) + s_ref[b, c])
            return tuple(new)
        lax.fori_loop(0, nc, step, tuple(jnp.zeros((g, w), F32) for _ in range(bl)))

    vm = pl.BlockSpec(memory_space=pltpu.VMEM)
    return pl.pallas_call(
        body, name="s5_scan", in_specs=[vm, vm, vm], out_specs=vm,
        out_shape=jax.ShapeDtypeStruct(sloc.shape, F32),
        compiler_params=pltpu.CompilerParams(vmem_limit_bytes=VMEM_LIMIT))(sloc, m1, m2)


def _s5_scan_bwd(dsprev, sprev, m1, m2):
    bl, nc, g, w = dsprev.shape

    def body(d_ref, s_ref, m1_ref, m2_ref, g_ref, p1_ref, p2_ref):
        a1, a2 = m1_ref[...], m2_ref[...]
        zero = jnp.zeros((g, w), F32)

        def step(i, carry):
            gps, p1, p2 = carry
            c = nc - 2 - i
            new = []
            for b, gp in enumerate(gps):
                g_ref[b, c] = gp
                sp = s_ref[b, c]
                p1 = p1 + gp * sp
                p2 = p2 + gp * pltpu.roll(sp, S5_STATE, 1)
                new.append(d_ref[b, c] + a1 * gp - a2 * pltpu.roll(gp, S5_STATE, 1))
            return tuple(new), p1, p2

        for b in range(bl):
            g_ref[b, nc - 1] = zero
        _, p1, p2 = lax.fori_loop(0, nc - 1, step, (tuple(d_ref[b, nc - 1] for b in range(bl)), zero, zero))
        p1_ref[...] = p1
        p2_ref[...] = p2

    vm = pl.BlockSpec(memory_space=pltpu.VMEM)
    sd = jax.ShapeDtypeStruct
    return pl.pallas_call(
        body, name="s5_scan_bwd", in_specs=[vm, vm, vm, vm], out_specs=[vm, vm, vm],
        out_shape=[sd(dsprev.shape, F32), sd((g, w), F32), sd((g, w), F32)],
        compiler_params=pltpu.CompilerParams(vmem_limit_bytes=VMEM_LIMIT))(dsprev, sprev, m1, m2)


def _gelu_tanh_parts(y):
    c0 = math.sqrt(2.0 / math.pi)
    inner = c0 * (y + 0.044715 * y * y * y)
    th = jnp.tanh(inner)
    return th, c0 * (1.0 + 3 * 0.044715 * y * y)


def _s5_fwd(h, g, ops, d_skip, w_in, w_glu, bl):
    bd, bn, co, m1, m2 = ops
    t, d = h.shape
    nct, groups = t // S5_CHUNK, d // S5_GROUP
    hn = _rmsnorm("mix_norm", h, g)
    u = _mm_rs("s5_in", hn, "flat", w_in, 0)
    yin, sloc = _s5_chunk_fwd(u, bd.astype(BF16), bn.astype(BF16))
    sprev = _s5_scan_fwd(sloc.reshape(bl, nct // bl, groups, S5_LANES), m1, m2).reshape(nct, groups, S5_LANES)
    y = _s5_state_out(sprev, co.astype(BF16), yin)

    def fn(yy, uu, dd):
        y2 = yy + dd * uu
        th, _ = _gelu_tanh_parts(y2)
        return [0.5 * y2 * (1.0 + th)], []
    z = _rows("s5_gelu", fn, [y, u, d_skip], [(d, BF16)])[0][0]
    zz = _mm_cs("s5_glu", z, w_glu, 0, "flat", F32)

    def glu(hh, zv):
        return [hh + zv[:, :d] * _sigmoid(zv[:, d:])], []
    out = _rows("s5_glu_mix", glu, [h, zz], [(d, F32)])[0][0]
    return out, (h, hn, u, sprev, y, z, zz)


def _s5_bwd(dout, saved, g, ops, d_skip, w_in, w_glu, bl):
    h, hn, u, sprev, y, z, zz = saved
    bd, bn, co, m1, m2 = ops
    t, d = h.shape
    nct, groups = t // S5_CHUNK, d // S5_GROUP

    def glu_bwd(do, zv):
        sg = _sigmoid(zv[:, d:])
        return [jnp.concatenate([do * sg, do * zv[:, :d] * sg * (1.0 - sg)], axis=1)], []
    dzz = _rows("s5_glu_bwd", glu_bwd, [dout, zz], [(2 * d, BF16)])[0][0]
    dwglu = _mm_dw("s5_dwglu", z, None, dzz, "flat", (None, 0, 1))
    dz = _mm_cs_dx("s5_glu_dx", [(dzz, w_glu)], "flat", 0)

    def gelu_bwd(dzv, yy, uu, dd):
        y2 = yy + dd * uu
        th, dinner = _gelu_tanh_parts(y2)
        dy2 = dzv * (0.5 * (1.0 + th) + 0.5 * y2 * (1.0 - th * th) * dinner)
        return [dy2, dy2 * dd], [jnp.sum(dy2 * uu, axis=0, keepdims=True)]
    (dyb, du_skip), (dd,) = _rows("s5_gelu_bwd", gelu_bwd, [dz, y, u, d_skip], [(d, F32), (d, F32)], [(1, d)])
    bd_b, bn_b, co_b = bd.astype(BF16), bn.astype(BF16), co.astype(BF16)
    dsprev = _s5_state_out_dx(dyb, co_b)
    shape4 = (bl, nct // bl, groups, S5_LANES)
    dsloc, dm1, dm2 = _s5_scan_bwd(dsprev.reshape(shape4), sprev.reshape(shape4), m1, m2)
    dsloc = dsloc.reshape(nct, groups, S5_LANES)
    du = _s5_chunk_dx(dyb, dsloc, bd_b, bn_b, du_skip).astype(BF16)
    dbd, dbn = _s5_chunk_dw(u, dyb, dsloc, bd, bn)
    dco = _s5_state_out_dw(sprev, dyb, co)
    dwin = _mm_dw("s5_dwin", hn, "flat", du, None, (None, 0, 1))
    dhn = _mm_rs_dx("s5_in_dx", du, w_in, 0, "flat", F32)
    dh, dg = _rmsnorm_bwd("mix_norm_bwd", dout, dhn, h, g)
    return dh, dg, dwin, dwglu, dd, (dbd, dbn, dco, dm1, dm2)


def _sb_block(qi, idx, tb):
    kb = qi - idx
    return pl.multiple_of(jnp.maximum(kb, 0) * tb, tb), idx == 0, kb >= 0


def _sb_scores(q, kblk, diag, exists, row, col):
    z = lax.dot_general(q, kblk, NT, preferred_element_type=F32) * (SB_HEAD_DIM ** -0.5)
    l1 = jnp.log(1.0 + jnp.exp(-jnp.abs(z)))
    ls = jnp.minimum(z, 0.0) - l1
    mask = jnp.logical_and(jnp.logical_or(col < row, jnp.logical_not(diag)), exists)
    lk = jnp.where(mask, ls - z, 0.0)
    return ls, lk, mask


def _sb_more(qi, carry):
    j, cr = carry[0], carry[2]
    return jnp.logical_and(j <= qi, jnp.max(cr) > SB_CUT)


def _split_dot(v, tri):
    hi = v.astype(BF16)
    lo = (v - hi.astype(F32)).astype(BF16)
    return (jnp.dot(hi, tri, preferred_element_type=F32) + jnp.dot(lo, tri, preferred_element_type=F32))


def _sb_attn_fwd(q, k, v):
    bh, l, dh = q.shape
    tb = min(SB_BLOCK, l)
    nq = l // tb

    def body(q_ref, k_ref, v_ref, o_ref):
        qi = pl.program_id(1)
        qv = q_ref[...]
        row = lax.broadcasted_iota(jnp.int32, (tb, tb), 0)
        col = lax.broadcasted_iota(jnp.int32, (tb, tb), 1)
        tri = (row > col).astype(BF16)

        def step(carry):
            j, acc, cr = carry
            where = [_sb_block(qi, j + u, tb) for u in range(SB_UNROLL)]
            scores = [_sb_scores(qv, k_ref[pl.ds(ks, tb), :], diag, exists, row, col) for ks, diag, exists in where]
            laters = [_split_dot(lk, tri) for _, lk, _ in scores]
            for (ks, _, _), (ls, lk, mask), later in zip(where, scores, laters):
                att = jnp.where(mask, jnp.exp(ls + later + cr), 0.0)
                acc = acc + jnp.dot(att.astype(BF16), v_ref[pl.ds(ks, tb), :], preferred_element_type=F32)
                cr = cr + jnp.sum(lk, axis=1, keepdims=True)
            return j + SB_UNROLL, acc, cr

        _, acc, _ = lax.while_loop(functools.partial(_sb_more, qi), step,
                                   (jnp.int32(0), jnp.zeros((tb, dh), F32), jnp.zeros((tb, 1), F32)))
        o_ref[...] = acc

    blk = pl.BlockSpec((None, tb, dh), lambda b, i: (b, i, 0))
    full = pl.BlockSpec((None, l, dh), lambda b, i: (b, 0, 0))
    return pl.pallas_call(
        body, name="sb_attn", grid=(bh, nq), in_specs=[blk, full, full], out_specs=blk,
        out_shape=jax.ShapeDtypeStruct((bh, l, dh), F32), compiler_params=_params("parallel", "parallel"))(q, k, v)


def _sb_attn_bwd(q, k, v, o, do):
    bh, l, dh = q.shape
    tb = min(SB_BLOCK, l)
    nq = l // tb
    scale = SB_HEAD_DIM ** -0.5

    def body(q_ref, k_ref, v_ref, o_ref, do_ref, dq_ref, dk_ref, dv_ref):
        qi = pl.program_id(1)

        @pl.when(qi == 0)
        def _():
            dk_ref[...] = jnp.zeros_like(dk_ref)
            dv_ref[...] = jnp.zeros_like(dv_ref)

        qv = q_ref[...]
        dob = do_ref[...].astype(BF16)
        dsum = jnp.sum(dob.astype(F32) * o_ref[...], axis=1, keepdims=True)
        row = lax.broadcasted_iota(jnp.int32, (tb, tb), 0)
        col = lax.broadcasted_iota(jnp.int32, (tb, tb), 1)
        tri = (row > col).astype(BF16)
        tri_inc = (row >= col).astype(BF16)

        def step(carry):
            j, dq, cr, ce = carry
            n = range(SB_UNROLL)
            where = [_sb_block(qi, j + u, tb) for u in n]
            rows = [pl.ds(ks, tb) for ks, _, _ in where]
            scores = [_sb_scores(qv, k_ref[rows[u], :], where[u][1], where[u][2], row, col) for u in n]
            laters = [_split_dot(lk, tri) for _, lk, _ in scores]
            datts = [lax.dot_general(dob, v_ref[rows[u], :], NT, preferred_element_type=F32) for u in n]
            atts = []
            for (ls, lk, mask), later in zip(scores, laters):
                atts.append(jnp.where(mask, jnp.exp(ls + later + cr), 0.0).astype(BF16))
                cr = cr + jnp.sum(lk, axis=1, keepdims=True)
            es = [atts[u].astype(F32) * datts[u] for u in n]
            sufs = [_split_dot(e, tri_inc) for e in es]
            dzs = []
            for (ls, _, mask), e, suf in zip(scores, es, sufs):
                pre = dsum - ce - suf
                sg = jnp.exp(ls)
                dzs.append((jnp.where(mask, e * (1.0 - sg) - pre * sg, 0.0) * scale).astype(BF16))
                ce = ce + jnp.sum(e, axis=1, keepdims=True)
            for u in n:
                dq = dq + jnp.dot(dzs[u], k_ref[rows[u], :], preferred_element_type=F32)
                dk_ref[rows[u], :] += lax.dot_general(dzs[u], qv, TN, preferred_element_type=F32)
                dv_ref[rows[u], :] += lax.dot_general(atts[u], dob, TN, preferred_element_type=F32)
            return j + SB_UNROLL, dq, cr, ce

        zc = jnp.zeros((tb, 1), F32)
        _, dq, _, _ = lax.while_loop(functools.partial(_sb_more, qi), step,
                                     (jnp.int32(0), jnp.zeros((tb, dh), F32), zc, zc))
        dq_ref[...] = dq

    blk = pl.BlockSpec((None, tb, dh), lambda b, i: (b, i, 0))
    full = pl.BlockSpec((None, l, dh), lambda b, i: (b, 0, 0))
    sd = jax.ShapeDtypeStruct((bh, l, dh), F32)
    return pl.pallas_call(
        body, name="sb_attn_bwd", grid=(bh, nq), in_specs=[blk, full, full, blk, blk], out_specs=[blk, full, full],
        out_shape=[sd, sd, sd], compiler_params=_params("parallel", "arbitrary"))(q, k, v, o, do)


SB_PAIR = 2 * SB_HEAD_DIM


def _pair_masks(tb):
    lane = lax.broadcasted_iota(jnp.int32, (1, SB_PAIR), 1)
    row = lax.broadcasted_iota(jnp.int32, (tb, tb), 0)
    col = lax.broadcasted_iota(jnp.int32, (tb, tb), 1)
    return [lane < SB_HEAD_DIM, lane >= SB_HEAD_DIM], row, col


def _pair_more(qi, carry):
    j, crs = carry[0], carry[2]
    return jnp.logical_and(j <= qi, jnp.maximum(jnp.max(crs[0]), jnp.max(crs[1])) > SB_CUT)


def _pair_specs(bl, l, d, tb):
    nq, off = l // tb, d // SB_PAIR
    qspec = pl.BlockSpec((tb, SB_PAIR), lambda b, p, i: (b * nq + i, p))
    kspec = pl.BlockSpec((l, SB_PAIR), lambda b, p, i: (b, off + p))
    vspec = pl.BlockSpec((l, SB_PAIR), lambda b, p, i: (b, 2 * off + p))
    return qspec, kspec, vspec


def _sb_attn_fwd2(qkv, bl):
    t, d3 = qkv.shape
    d, l = d3 // 3, t // bl
    tb = min(SB_BLOCK, l)
    nq = l // tb

    def body(q_ref, k_ref, v_ref, o_ref, ob_ref):
        qi = pl.program_id(2)
        heads, row, col = _pair_masks(tb)
        qv = q_ref[...]
        qh = [jnp.where(m, qv, jnp.zeros_like(qv)) for m in heads]
        tri = (row > col).astype(BF16)

        def step(carry):
            j, acc, crs = carry
            crs = list(crs)
            where = [_sb_block(qi, j + u, tb) for u in range(SB_UNROLL)]
            kblks = [k_ref[pl.ds(ks, tb), :] for ks, _, _ in where]
            scores = [[_sb_scores(qh[hd], kblks[u], where[u][1], where[u][2], row, col) for hd in range(2)]
                      for u in range(SB_UNROLL)]
            laters = [[_split_dot(sc[1], tri) for sc in su] for su in scores]
            for u in range(SB_UNROLL):
                vblk = v_ref[pl.ds(where[u][0], tb), :]
                outs = []
                for hd in range(2):
                    ls, lk, mask = scores[u][hd]
                    att = jnp.where(mask, jnp.exp(ls + laters[u][hd] + crs[hd]), 0.0)
                    outs.append(jnp.dot(att.astype(BF16), vblk, preferred_element_type=F32))
                    crs[hd] = crs[hd] + jnp.sum(lk, axis=1, keepdims=True)
                acc = acc + jnp.where(heads[0], outs[0], outs[1])
            return j + SB_UNROLL, acc, tuple(crs)

        zc = jnp.zeros((tb, 1), F32)
        _, acc, _ = lax.while_loop(functools.partial(_pair_more, qi), step,
                                   (jnp.int32(0), jnp.zeros((tb, SB_PAIR), F32), (zc, zc)))
        o_ref[...] = acc
        ob_ref[...] = acc.astype(BF16)

    qspec, kspec, vspec = _pair_specs(bl, l, d, tb)
    return pl.pallas_call(
        body, name="sb_attn", grid=(bl, d // SB_PAIR, nq), in_specs=[qspec, kspec, vspec], out_specs=[qspec, qspec],
        out_shape=[jax.ShapeDtypeStruct((t, d), F32), jax.ShapeDtypeStruct((t, d), BF16)],
        compiler_params=_params("parallel", "parallel", "parallel"))(qkv, qkv, qkv)


def _sb_attn_bwd2(qkv, o, do, bl):
    t, d3 = qkv.shape
    d, l = d3 // 3, t // bl
    tb = min(SB_BLOCK, l)
    nq = l // tb
    scale = SB_HEAD_DIM ** -0.5

    def body(q_ref, k_ref, v_ref, o_ref, do_ref, dq_ref, dk_ref, dv_ref, dk_acc, dv_acc):
        qi = pl.program_id(2)

        @pl.when(qi == 0)
        def _():
            dk_acc[...] = jnp.zeros_like(dk_acc)
            dv_acc[...] = jnp.zeros_like(dv_acc)

        heads, row, col = _pair_masks(tb)
        qv = q_ref[...]
        dov = do_ref[...].astype(BF16)
        qh = [jnp.where(m, qv, jnp.zeros_like(qv)) for m in heads]
        doh = [jnp.where(m, dov, jnp.zeros_like(dov)) for m in heads]
        ov = o_ref[...]
        dsum = [jnp.sum(dh.astype(F32) * ov, axis=1, keepdims=True) for dh in doh]
        tri = (row > col).astype(BF16)
        tri_inc = (row >= col).astype(BF16)

        def step(carry):
            j, dq, crs, ces = carry
            crs, ces = list(crs), list(ces)
            n = range(SB_UNROLL)
            where = [_sb_block(qi, j + u, tb) for u in n]
            rows = [pl.ds(ks, tb) for ks, _, _ in where]
            kblks = [k_ref[rows[u], :] for u in n]
            vblks = [v_ref[rows[u], :] for u in n]
            scores = [[_sb_scores(qh[hd], kblks[u], where[u][1], where[u][2], row, col) for hd in range(2)] for u in n]
            laters = [[_split_dot(sc[1], tri) for sc in su] for su in scores]
            datts = [[lax.dot_general(doh[hd], vblks[u], NT, preferred_element_type=F32) for hd in range(2)] for u in n]
            atts = [[None, None] for _ in n]
            for u in n:
                for hd in range(2):
                    ls, lk, mask = scores[u][hd]
                    atts[u][hd] = jnp.where(mask, jnp.exp(ls + laters[u][hd] + crs[hd]), 0.0).astype(BF16)
                    crs[hd] = crs[hd] + jnp.sum(lk, axis=1, keepdims=True)
            es = [[atts[u][hd].astype(F32) * datts[u][hd] for hd in range(2)] for u in n]
            sufs = [[_split_dot(e, tri_inc) for e in eu] for eu in es]
            dzs = [[None, None] for _ in n]
            for u in n:
                for hd in range(2):
                    ls, _, mask = scores[u][hd]
                    pre = dsum[hd] - ces[hd] - sufs[u][hd]
                    sg = jnp.exp(ls)
                    dzs[u][hd] = (jnp.where(mask, es[u][hd] * (1.0 - sg) - pre * sg, 0.0) * scale).astype(BF16)
                    ces[hd] = ces[hd] + jnp.sum(es[u][hd], axis=1, keepdims=True)
            for u in n:
                dq = dq + jnp.where(heads[0], jnp.dot(dzs[u][0], kblks[u], preferred_element_type=F32),
                                    jnp.dot(dzs[u][1], kblks[u], preferred_element_type=F32))
                dk_acc[rows[u], :] += (lax.dot_general(dzs[u][0], qh[0], TN, preferred_element_type=F32)
                                       + lax.dot_general(dzs[u][1], qh[1], TN, preferred_element_type=F32))
                dv_acc[rows[u], :] += (lax.dot_general(atts[u][0], doh[0], TN, preferred_element_type=F32)
                                       + lax.dot_general(atts[u][1], doh[1], TN, preferred_element_type=F32))
            return j + SB_UNROLL, dq, tuple(crs), tuple(ces)

        zc = jnp.zeros((tb, 1), F32)
        _, dq, _, _ = lax.while_loop(functools.partial(_pair_more, qi), step,
                                     (jnp.int32(0), jnp.zeros((tb, SB_PAIR), F32), (zc, zc), (zc, zc)))
        dq_ref[...] = dq.astype(BF16)

        @pl.when(qi == nq - 1)
        def _():
            dk_ref[...] = dk_acc[...].astype(BF16)
            dv_ref[...] = dv_acc[...].astype(BF16)

    qspec, kspec, vspec = _pair_specs(bl, l, d, tb)
    blk = pl.BlockSpec((tb, SB_PAIR), lambda b, p, i: (b * nq + i, p))
    full = pl.BlockSpec((l, SB_PAIR), lambda b, p, i: (b, p))
    sd = jax.ShapeDtypeStruct((t, d), BF16)
    dq, dk, dv = pl.pallas_call(
        body, name="sb_attn_bwd", grid=(bl, d // SB_PAIR, nq), in_specs=[qspec, kspec, vspec, blk, blk],
        out_specs=[blk, full, full], out_shape=[sd, sd, sd],
        scratch_shapes=[pltpu.VMEM((l, SB_PAIR), F32), pltpu.VMEM((l, SB_PAIR), F32)],
        compiler_params=_params("parallel", "parallel", "arbitrary"))(qkv, qkv, qkv, o, do)
    return jnp.concatenate([dq, dk, dv], axis=1)


def _to_heads(x, bl):
    t, w = x.shape
    heads = w // SB_HEAD_DIM
    l = t // bl
    return x.reshape(bl, l, heads, SB_HEAD_DIM).transpose(0, 2, 1, 3).reshape(bl * heads, l, SB_HEAD_DIM)


def _from_heads(x, bl):
    bh, l, dh = x.shape
    heads = bh // bl
    return x.reshape(bl, heads, l, dh).transpose(0, 2, 1, 3).reshape(bl * l, heads * dh)


def _sb_fwd(h, g, w_qkv, w_o, bl):
    t, d = h.shape
    hn = _rmsnorm("mix_norm", h, g)
    qkv = _mm_cs("sb_qkv", hn, w_qkv, 0, "flat", BF16)
    o, ob = _sb_attn_fwd2(qkv, bl)
    out = _mm_rs("sb_out", ob, "flat", w_o, 0, res=h)
    return out, (h, hn, qkv, o, ob)


def _sb_bwd(dout, saved, g, w_qkv, w_o, bl):
    h, hn, qkv, o, ob = saved
    dob = dout.astype(BF16)
    dwo = _mm_dw("sb_dwo", ob, "flat", dob, None, (None, 0, 1))
    do = _mm_rs_dx("sb_out_dx", dob, w_o, 0, "flat", F32)
    dqkv = _sb_attn_bwd2(qkv, o, do, bl)
    dwqkv = _mm_dw("sb_dwqkv", hn, None, dqkv, "flat", (None, 0, 1))
    dhn = _mm_cs_dx("sb_qkv_dx", [(dqkv, w_qkv)], "flat", 0)
    dh, dg = _rmsnorm_bwd("mix_norm_bwd", dout, dhn, h, g)
    return dh, dg, dwqkv, dwo


def _adamw_update(wv, gr, mv, vv):
    c1 = 1.0 / (1.0 - ADAM_B1 ** ADAM_STEP)
    c2 = 1.0 / (1.0 - ADAM_B2 ** ADAM_STEP)
    mn = ADAM_B1 * mv + (1.0 - ADAM_B1) * gr
    vn = ADAM_B2 * vv + (1.0 - ADAM_B2) * gr * gr
    delta = -ADAM_LR * ((mn * c1) / (jnp.sqrt(vn * c2) + ADAM_EPS) + ADAM_WD * wv)
    return delta, mn, vn


def _adamw_small(w, gr, m, v):
    def fn(wv, gv, mv, vv):
        return list(_adamw_update(wv, gv, mv, vv)), []
    return _rows("adamw_small", fn, [w, gr, m, v], [(w.shape[1], F32)] * 3)[0]


def _place():
    x, y, c = lax.axis_index("x"), lax.axis_index("y"), lax.axis_index("c")
    chips = [(1 - x, y), (x, 1 - y), (1 - x, 1 - y)]
    return x, y, c, chips


def _remote(src, dst, send_sem, recv_sem, to):
    return pltpu.make_async_remote_copy(src_ref=src, dst_ref=dst, send_sem=send_sem, recv_sem=recv_sem,
                                        device_id=to, device_id_type=MESH)


def _half(ref, c, rh, lead):
    return ref.at[(slice(None),) * lead + (pl.ds(c * rh, rh),)]


def _allgather_weights(ws):
    n = len(ws)

    def body(*refs):
        ins, outs = refs[:n], refs[n:2 * n]
        send, recv = refs[2 * n:]
        x, y, c, chips = _place()
        own = 2 * x + y
        sibling = (x, y, 1 - c)
        sent = []
        for t in range(n):
            rh = ws[t].shape[1] // 2
            for j, chip in enumerate(chips):
                cp = _remote(_half(ins[t], c, rh, 1), _half(outs[t].at[own], c, rh, 1), send.at[t, j], recv.at[t, j], (*chip, c))
                cp.start()
                sent.append(cp)
        for t in range(n):
            rh = ws[t].shape[1] // 2
            for j, chip in enumerate(chips):
                landed = _half(outs[t].at[2 * chip[0] + chip[1]], c, rh, 1)
                _remote(landed, landed, send.at[t, j], recv.at[t, j], (*chip, c)).wait_recv()
                cp = _remote(landed, landed, send.at[t, 3 + j], recv.at[t, 3 + j], sibling)
                cp.start()
                sent.append(cp)
        for t in range(n):
            rh = ws[t].shape[1] // 2
            for j, chip in enumerate(chips):
                passed = _half(outs[t].at[2 * chip[0] + chip[1]], 1 - c, rh, 1)
                _remote(passed, passed, send.at[t, 3 + j], recv.at[t, 3 + j], sibling).wait_recv()
        for cp in sent:
            cp.wait_send()

    res = pl.pallas_call(
        body, name="allgather_weights", in_specs=[ANY] * n, out_specs=[ANY] * n,
        out_shape=[jax.ShapeDtypeStruct((N_CHIPS,) + w.shape, w.dtype) for w in ws],
        scratch_shapes=[pltpu.SemaphoreType.DMA((n, 6)), pltpu.SemaphoreType.DMA((n, 6))],
    )(*ws)
    own = 2 * lax.axis_index("x") + lax.axis_index("y")
    return [lax.dynamic_update_slice(g, w[None], (own, 0, 0, 0)) for g, w in zip(res, ws)]


def _pair_exchange(gs):
    n = len(gs)

    def body(*refs):
        ins, outs = refs[:n], refs[n:2 * n]
        send, recv = refs[2 * n:]
        x, y, c, _ = _place()
        copies = [_remote(_half(ins[t], 1 - c, gs[t].shape[2] // 2, 2), outs[t], send.at[t], recv.at[t], (x, y, 1 - c))
                  for t in range(n)]
        for cp in copies:
            cp.start()
        for cp in copies:
            cp.wait()

    return pl.pallas_call(
        body, name="grad_pair_exchange", in_specs=[ANY] * n, out_specs=[ANY] * n,
        out_shape=[jax.ShapeDtypeStruct(g.shape[:2] + (g.shape[2] // 2, g.shape[3]), F32) for g in gs],
        scratch_shapes=[pltpu.SemaphoreType.DMA((n,)), pltpu.SemaphoreType.DMA((n,))],
    )(*gs)


def _pair_sum(g, theirs, c_idx):
    n4, ly, r, cc = g.shape
    rh = r // 2
    tm = _tile(rh, 256)
    nt = rh // tm

    def body(c_ref, g_ref, t_ref, o_ref):
        o_ref[...] = (g_ref[...] + t_ref[...]).astype(o_ref.dtype)

    blk = (None, tm, cc)
    grid_spec = pltpu.PrefetchScalarGridSpec(
        num_scalar_prefetch=1, grid=(n4 * ly, nt),
        in_specs=[pl.BlockSpec(blk, lambda a, i, cr: (a, cr[0] * nt + i, 0)), pl.BlockSpec(blk, lambda a, i, cr: (a, i, 0))],
        out_specs=pl.BlockSpec(blk, lambda a, i, cr: (a, i, 0)))
    out = pl.pallas_call(
        body, name="grad_pair_sum", grid_spec=grid_spec, out_shape=jax.ShapeDtypeStruct((n4 * ly, rh, cc), BF16),
        compiler_params=_params("parallel", "parallel"))(c_idx, g.reshape(n4 * ly, r, cc), theirs.reshape(n4 * ly, rh, cc))
    return out.reshape(n4, ly, rh, cc)


def _chip_exchange(ps):
    n = len(ps)

    def body(*refs):
        ins, outs = refs[:n], refs[n:2 * n]
        send, recv = refs[2 * n:]
        x, y, c, chips = _place()
        copies = []
        for t in range(n):
            for j, chip in enumerate(chips):
                copies.append(_remote(ins[t].at[2 * chip[0] + chip[1]], outs[t].at[j], send.at[t, j], recv.at[t, j], (*chip, c)))
        for cp in copies:
            cp.start()
        for cp in copies:
            cp.wait()

    return pl.pallas_call(
        body, name="grad_chip_exchange", in_specs=[ANY] * n, out_specs=[ANY] * n,
        out_shape=[jax.ShapeDtypeStruct((3,) + p.shape[1:], p.dtype) for p in ps],
        scratch_shapes=[pltpu.SemaphoreType.DMA((n, 3)), pltpu.SemaphoreType.DMA((n, 3))],
    )(*ps)


def _chip_sum(p, landed, own_idx):
    _, ly, rh, cc = p.shape
    tm = _tile(rh, 256)

    def body(o_ref, p_ref, a_ref, b_ref, c_ref, out_ref):
        up = lambda r: r[...].astype(F32)
        out_ref[...] = ((up(p_ref) + up(a_ref)) + up(b_ref)) + up(c_ref)

    blk = (None, None, tm, cc)
    slot = lambda j: pl.BlockSpec(blk, lambda l, i, o: (j, l, i, 0))
    grid_spec = pltpu.PrefetchScalarGridSpec(
        num_scalar_prefetch=1, grid=(ly, rh // tm),
        in_specs=[pl.BlockSpec(blk, lambda l, i, o: (o[0], l, i, 0)), slot(0), slot(1), slot(2)],
        out_specs=pl.BlockSpec((None, tm, cc), lambda l, i, o: (l, i, 0)))
    return pl.pallas_call(
        body, name="grad_chip_sum", grid_spec=grid_spec, out_shape=jax.ShapeDtypeStruct((ly, rh, cc), F32),
        compiler_params=_params("parallel", "parallel"))(own_idx, p, landed, landed, landed)


def _pair_swap(halves):
    n = len(halves)

    def body(*refs):
        ins, outs = refs[:n], refs[n:2 * n]
        send, recv = refs[2 * n:]
        x, y, c, _ = _place()
        copies = [_remote(ins[t], outs[t], send.at[t], recv.at[t], (x, y, 1 - c)) for t in range(n)]
        for cp in copies:
            cp.start()
        for cp in copies:
            cp.wait()

    return pl.pallas_call(
        body, name="grad_pair_swap", in_specs=[ANY] * n, out_specs=[ANY] * n,
        out_shape=[jax.ShapeDtypeStruct(h.shape, F32) for h in halves],
        scratch_shapes=[pltpu.SemaphoreType.DMA((n,)), pltpu.SemaphoreType.DMA((n,))],
    )(*halves)


def _adamw_big(w, m, v, mine, theirs, c_idx):
    ly, r, cc = w.shape
    rh = r // 2
    tm = _tile(rh, 256)
    nt = rh // tm

    def body(c_ref, w_ref, m_ref, v_ref, a_ref, b_ref, g_out, d_out, m_out, v_out):
        gr = jnp.where(pl.program_id(1) == c_ref[0], a_ref[...], b_ref[...])
        delta, mn, vn = _adamw_update(w_ref[...], gr, m_ref[...], v_ref[...])
        g_out[...] = gr
        d_out[...] = delta
        m_out[...] = mn
        v_out[...] = vn

    blk = (None, tm, cc)
    full = pl.BlockSpec(blk, lambda l, hc, i, cr: (l, hc * nt + i, 0))
    half = pl.BlockSpec(blk, lambda l, hc, i, cr: (l, i, 0))
    grid_spec = pltpu.PrefetchScalarGridSpec(
        num_scalar_prefetch=1, grid=(ly, 2, nt), in_specs=[full, full, full, half, half], out_specs=[full] * 4)
    sd = jax.ShapeDtypeStruct(w.shape, F32)
    return pl.pallas_call(
        body, name="adamw", grid_spec=grid_spec, out_shape=[sd] * 4,
        compiler_params=_params("parallel", "parallel", "parallel"))(c_idx, w, m, v, mine, theirs)


def _allreduce_small(v):
    rows, w = v.shape

    def body(x_ref, sum_ref, all_ref, send, recv, local):
        x, y, c, chips = _place()
        me, sibling = (x, y, c), (x, y, 1 - c)

        def slot(px, py, pc):
            return all_ref.at[4 * px + 2 * py + pc]

        def copy(k, block, to, src=None):
            return _remote(slot(*block) if src is None else src, slot(*block), send.at[k], recv.at[k], to)

        mine = pltpu.make_async_copy(x_ref, slot(*me), local)
        mine.start()
        first = [copy(0, me, sibling, src=x_ref)]
        first += [copy(1 + j, me, (*chip, c), src=x_ref) for j, chip in enumerate(chips)]
        for cp in first:
            cp.start()
        passed = [copy(4 + j, (*chip, c), sibling) for j, chip in enumerate(chips)]
        for j, chip in enumerate(chips):
            copy(1 + j, (*chip, c), me).wait_recv()
            passed[j].start()
        copy(0, sibling, me).wait_recv()
        for j, chip in enumerate(chips):
            copy(4 + j, (*chip, 1 - c), me).wait_recv()
        for cp in first + passed:
            cp.wait_send()
        mine.wait()
        tot = all_ref[0]
        for k in range(1, N_DEV):
            tot = tot + all_ref[k]
        sum_ref[...] = tot

    vm = pl.BlockSpec(memory_space=pltpu.VMEM)
    return pl.pallas_call(
        body, name="allreduce_small", in_specs=[vm], out_specs=[vm, vm],
        out_shape=[jax.ShapeDtypeStruct((rows, w), F32), jax.ShapeDtypeStruct((N_DEV, rows, w), F32)],
        scratch_shapes=[pltpu.SemaphoreType.DMA((7,)), pltpu.SemaphoreType.DMA((7,)), pltpu.SemaphoreType.DMA],
        compiler_params=pltpu.CompilerParams(vmem_limit_bytes=VMEM_LIMIT),
    )(v)[0]


BIG = ["ffn1_w1", "ffn1_w3", "ffn1_w2", "ffn2_w1", "ffn2_w3", "ffn2_w2", "ple_proj", "ple_gate",
       "s5_w_in", "s5_w_glu", "sb_w_qkv", "sb_w_o"]
SMALL = ["ffn1_norm", "mix_norm", "ffn2_norm", "ple_norm", "s5_a_re", "s5_a_im", "s5_log_dt", "s5_b_re", "s5_b_im",
         "s5_c_re", "s5_c_im", "s5_d", "final_norm"]
ORDER = ["ffn1_norm", "ffn1_w1", "ffn1_w3", "ffn1_w2", "mix_norm", "ffn2_norm", "ffn2_w1", "ffn2_w3", "ffn2_w2",
         "ple_norm", "ple_proj", "ple_gate", "s5_w_in", "s5_a_re", "s5_a_im", "s5_log_dt", "s5_b_re", "s5_b_im",
         "s5_c_re", "s5_c_im", "s5_d", "s5_w_glu", "sb_w_qkv", "sb_w_o", "final_norm"]


def _pack(arrays):
    flat = jnp.concatenate([a.reshape(-1) for a in arrays])
    pad = (-flat.shape[0]) % 1024
    return jnp.pad(flat, (0, pad)).reshape(-1, 128)


def _unpack(packed, like):
    flat = packed.reshape(-1)
    out, off = [], 0
    for a in like:
        out.append(flat[off:off + a.size].reshape(a.shape))
        off += a.size
    return out


def _fwd_bwd(x, p, target, w, gathered):
    bl, l, d = x.shape
    t = bl * l
    depth = w["ffn1_norm"].shape[0]
    s5_ops, s5_vjp = jax.vjp(_s5_prep, w["s5_a_re"][0], w["s5_a_im"][0], w["s5_log_dt"][0], w["s5_b_re"][0],
                             w["s5_b_im"][0], w["s5_c_re"][0], w["s5_c_im"][0])

    h = x.reshape(t, d)
    p2 = [p[i].reshape(t, p.shape[-1]).astype(BF16) for i in range(depth)]
    saved = []
    for i in range(depth):
        norm = lambda name: w[name][i:i + 1]
        h, s1 = _ffn_fwd(h, norm("ffn1_norm"), gathered["ffn1_w1"], gathered["ffn1_w3"], gathered["ffn1_w2"], i)
        if i % 2 == 0:
            h, s2 = _s5_fwd(h, norm("mix_norm"), s5_ops, w["s5_d"][i // 2:i // 2 + 1], gathered["s5_w_in"], gathered["s5_w_glu"], bl)
        else:
            h, s2 = _sb_fwd(h, norm("mix_norm"), gathered["sb_w_qkv"], gathered["sb_w_o"], bl)
        h, s3 = _ffn_fwd(h, norm("ffn2_norm"), gathered["ffn2_w1"], gathered["ffn2_w3"], gathered["ffn2_w2"], i)
        h, s4 = _ple_fwd(h, norm("ple_norm"), p2[i], gathered["ple_proj"], gathered["ple_gate"], i)
        saved.append((s1, s2, s3, s4))

    loss, dh, dfinal = _head(h, w["final_norm"].reshape(1, d), target.reshape(t, d))

    big = {k: None for k in BIG}
    small = {k: [None] * w[k].shape[0] if w[k].ndim > 1 else None for k in SMALL}
    small["final_norm"] = dfinal.reshape(d)
    for i in reversed(range(depth)):
        norm = lambda name: w[name][i:i + 1]
        slots = lambda *names: [(big[k], i, depth) for k in names]
        s1, s2, s3, s4 = saved[i]
        dh, dg, big["ple_proj"], big["ple_gate"] = _ple_bwd(
            dh, s4, norm("ple_norm"), p2[i], gathered["ple_proj"], gathered["ple_gate"], i, slots("ple_proj", "ple_gate"))
        small["ple_norm"][i] = dg[0]
        dh, dg, big["ffn2_w1"], big["ffn2_w3"], big["ffn2_w2"] = _ffn_bwd(
            dh, s3, norm("ffn2_norm"), gathered["ffn2_w1"], gathered["ffn2_w3"], gathered["ffn2_w2"], i,
            slots("ffn2_w1", "ffn2_w3", "ffn2_w2"))
        small["ffn2_norm"][i] = dg[0]
        if i % 2 == 0:
            dh, dg, big["s5_w_in"], big["s5_w_glu"], dd, dops = _s5_bwd(
                dh, s2, norm("mix_norm"), s5_ops, w["s5_d"][i // 2:i // 2 + 1], gathered["s5_w_in"], gathered["s5_w_glu"], bl)
            small["s5_d"][0] = dd[0]
            raw = s5_vjp(dops)
            for name, gr in zip(["s5_a_re", "s5_a_im", "s5_log_dt", "s5_b_re", "s5_b_im", "s5_c_re", "s5_c_im"], raw):
                small[name][0] = gr
        else:
            dh, dg, big["sb_w_qkv"], big["sb_w_o"] = _sb_bwd(dh, s2, norm("mix_norm"), gathered["sb_w_qkv"], gathered["sb_w_o"], bl)
        small["mix_norm"][i] = dg[0]
        dh, dg, big["ffn1_w1"], big["ffn1_w3"], big["ffn1_w2"] = _ffn_bwd(
            dh, s1, norm("ffn1_norm"), gathered["ffn1_w1"], gathered["ffn1_w3"], gathered["ffn1_w2"], i,
            slots("ffn1_w1", "ffn1_w3", "ffn1_w2"))
        small["ffn1_norm"][i] = dg[0]
    small_list = [jnp.stack(small[k]) if isinstance(small[k], list) else small[k] for k in SMALL]
    return loss, dh.reshape(bl, l, d), big, small_list


def _step(x, p, target, w, m, v):
    gathered = dict(zip(BIG, _allgather_weights([_to_bf16(w[k]) for k in BIG])))
    loss, grad_x, big, small_list = _fwd_bwd(x, p, target, w, gathered)

    c_idx = lax.axis_index("c").astype(jnp.int32).reshape(1)
    own_idx = (2 * lax.axis_index("x") + lax.axis_index("y")).astype(jnp.int32).reshape(1)
    partial = [big[k] for k in BIG]
    pair = [_pair_sum(g, t, c_idx) for g, t in zip(partial, _pair_exchange(partial))]
    mine = [_chip_sum(pr, ld, own_idx) for pr, ld in zip(pair, _chip_exchange(pair))]
    theirs = _pair_swap(mine)
    out_g, out_d, out_m, out_v = {}, {}, {}, {}
    for k, a, b in zip(BIG, mine, theirs):
        out_g[k], out_d[k], out_m[k], out_v[k] = _adamw_big(w[k], m[k], v[k], a, b, c_idx)

    like = [w[k] for k in SMALL]
    pad = [jnp.zeros((1,), F32)]
    g_small = _allreduce_small(_pack(small_list + [loss.reshape(1)]))
    packed = (g_small,) + tuple(_adamw_small(_pack(like + pad), g_small, _pack([m[k] for k in SMALL] + pad),
                                             _pack([v[k] for k in SMALL] + pad)))
    for dst, pk in zip((out_g, out_d, out_m, out_v), packed):
        dst.update(dict(zip(SMALL, _unpack(pk, like))))
    loss = g_small.reshape(-1)[sum(a.size for a in like)]
    return (loss, grad_x, *[out_g[k] for k in ORDER], *[out_d[k] for k in ORDER],
            *[out_m[k] for k in ORDER], *[out_v[k] for k in ORDER])


def kernel(x, p, ffn1_norm, ffn1_w1, ffn1_w3, ffn1_w2, mix_norm, ffn2_norm, ffn2_w1, ffn2_w3, ffn2_w2, ple_norm, ple_proj, ple_gate, s5_w_in, s5_a_re, s5_a_im, s5_log_dt, s5_b_re, s5_b_im, s5_c_re, s5_c_im, s5_d, s5_w_glu, sb_w_qkv, sb_w_o, final_norm, loss_target, m_ffn1_norm, m_ffn1_w1, m_ffn1_w3, m_ffn1_w2, m_mix_norm, m_ffn2_norm, m_ffn2_w1, m_ffn2_w3, m_ffn2_w2, m_ple_norm, m_ple_proj, m_ple_gate, m_s5_w_in, m_s5_a_re, m_s5_a_im, m_s5_log_dt, m_s5_b_re, m_s5_b_im, m_s5_c_re, m_s5_c_im, m_s5_d, m_s5_w_glu, m_sb_w_qkv, m_sb_w_o, m_final_norm, v_ffn1_norm, v_ffn1_w1, v_ffn1_w3, v_ffn1_w2, v_mix_norm, v_ffn2_norm, v_ffn2_w1, v_ffn2_w3, v_ffn2_w2, v_ple_norm, v_ple_proj, v_ple_gate, v_s5_w_in, v_s5_a_re, v_s5_a_im, v_s5_log_dt, v_s5_b_re, v_s5_b_im, v_s5_c_re, v_s5_c_im, v_s5_d, v_s5_w_glu, v_sb_w_qkv, v_sb_w_o, v_final_norm):
    args = dict(locals())
    w = {k: args[k] for k in ORDER}
    m = {k: args["m_" + k] for k in ORDER}
    v = {k: args["v_" + k] for k in ORDER}
    return _step(x, p, loss_target, w, m, v)
```

```python
import functools
import math

import jax
import jax.numpy as jnp
from jax import lax
from jax.experimental import pallas as pl
from jax.experimental.pallas import tpu as pltpu

F32 = jnp.float32
BF16 = jnp.bfloat16
MESH = pl.DeviceIdType.MESH

N_CHIPS = 4
N_DEV = 8
RMS_EPS = 1e-6
S5_GROUP = 16
S5_STATE = 64
S5_CHUNK = 16
SB_HEAD_DIM = 64
SB_BLOCK = 128
SB_CUT = -104.0
SB_UNROLL = 3
ADAM_LR, ADAM_B1, ADAM_B2, ADAM_EPS, ADAM_WD, ADAM_STEP = 0.001, 0.9, 0.999, 1e-08, 0.01, 10
VMEM_LIMIT = 48 * 1024 * 1024

NN = (((1,), (0,)), ((), ()))
NT = (((1,), (1,)), ((), ()))
TN = (((0,), (0,)), ((), ()))

ANY = pl.BlockSpec(memory_space=pl.ANY)


def _tile(n, target):
    if n <= target:
        return n
    for t in range(target - target % 8, 7, -8):
        if n % t == 0:
            return t
    raise ValueError(f"no row tile for {n}")


def _params(*semantics):
    return pltpu.CompilerParams(dimension_semantics=semantics, vmem_limit_bytes=VMEM_LIMIT)


def _sigmoid(v):
    return 1.0 / (1.0 + jnp.exp(-v))


def _gemm(name, grid, operands, in_specs, groups, acc_shapes, out_shapes, out_specs, epilogue, reduce_axis=None, aliases=None):
    n_in, n_out = len(operands), len(out_shapes)
    n_red = None if reduce_axis is None else grid[reduce_axis]

    def body(*refs):
        ins, outs, accs = refs[:n_in], refs[n_in:n_in + n_out], refs[n_in + n_out:]

        def products():
            res = []
            for terms in groups:
                tot = None
                for ia, ib, dims in terms:
                    d = lax.dot_general(ins[ia][...], ins[ib][...], dims, preferred_element_type=F32)
                    tot = d if tot is None else tot + d
                res.append(tot)
            return res

        def finish(vals):
            for o, v in zip(outs, epilogue(vals, ins)):
                o[...] = v.astype(o.dtype)

        if reduce_axis is None:
            finish(products())
        else:
            k = pl.program_id(reduce_axis)

            @pl.when(k == 0)
            def _():
                for a in accs:
                    a[...] = jnp.zeros_like(a)

            for a, d in zip(accs, products()):
                a[...] += d

            @pl.when(k == n_red - 1)
            def _():
                finish([a[...] for a in accs])

    scratch = [] if reduce_axis is None else [pltpu.VMEM(s, F32) for s in acc_shapes]
    sem = tuple("arbitrary" if i == reduce_axis else "parallel" for i in range(len(grid)))
    return pl.pallas_call(
        body, name=name, grid=grid, in_specs=in_specs, out_specs=out_specs, out_shape=out_shapes,
        scratch_shapes=scratch, input_output_aliases=aliases or {}, compiler_params=_params(*sem))(*operands)


def _ident(vals, ins):
    return vals


def _act_spec(layout, tm, cs, pos):
    if layout == "sm":
        return pl.BlockSpec((None, tm, cs), lambda *g: (pos(*g)[1], pos(*g)[0], 0))
    return pl.BlockSpec((tm, cs), lambda *g: pos(*g))


def _act_shape(layout, t, cs, dtype):
    return jax.ShapeDtypeStruct((N_CHIPS, t, cs) if layout == "sm" else (t, N_CHIPS * cs), dtype)


def _w_spec(w, layer, pos_k):
    _, _, r, c = w.shape
    return pl.BlockSpec((None, None, r, c), lambda *g: (pos_k(*g), layer, 0, 0))


def _mm_cs(name, x, w, layer, out_layout, out_dtype, tm=512):
    t, kd = x.shape
    cs = w.shape[3]
    tm = _tile(t, tm)
    return _gemm(
        name, (N_CHIPS, t // tm), [x, w],
        [pl.BlockSpec((tm, kd), lambda k, i: (i, 0)), _w_spec(w, layer, lambda k, i: k)],
        [[(0, 1, NN)]], None, [_act_shape(out_layout, t, cs, out_dtype)],
        [_act_spec(out_layout, tm, cs, lambda k, i: (i, k))], _ident)[0]


def _mm_rs(name, xs, layout, w, layer, res=None, alpha=1.0, out_dtype=F32, tm=1024):
    ks, n = w.shape[2], w.shape[3]
    t = xs.shape[1] if layout == "sm" else xs.shape[0]
    tm = _tile(t, tm)
    operands = [xs, w] + ([] if res is None else [res])
    specs = [_act_spec(layout, tm, ks, lambda i, k: (i, k)), _w_spec(w, layer, lambda i, k: k)]
    if res is not None:
        specs.append(pl.BlockSpec((tm, n), lambda i, k: (i, 0)))

    def epilogue(vals, ins):
        y = alpha * vals[0]
        return [y if res is None else ins[2][...] + y]

    return _gemm(
        name, (t // tm, N_CHIPS), operands, specs, [[(0, 1, NN)]], [(tm, n)],
        [jax.ShapeDtypeStruct((t, n), out_dtype)], [pl.BlockSpec((tm, n), lambda i, k: (i, 0))],
        epilogue, reduce_axis=1)[0]


def _mm_cs_dx(name, pairs, layout, layer, tm=1024, transposed=False):
    w0 = pairs[0][1]
    kd, cs = (w0.shape[3], w0.shape[2]) if transposed else (w0.shape[2], w0.shape[3])
    dy0 = pairs[0][0]
    t = dy0.shape[1] if layout == "sm" else dy0.shape[0]
    tm = _tile(t, tm)
    operands, specs, terms = [], [], []
    for dy, w in pairs:
        terms.append((len(operands), len(operands) + 1, NN if transposed else NT))
        operands += [dy, w]
        specs += [_act_spec(layout, tm, cs, lambda i, k: (i, k)), _w_spec(w, layer, lambda i, k: k)]
    return _gemm(
        name, (t // tm, N_CHIPS), operands, specs, [terms], [(tm, kd)],
        [jax.ShapeDtypeStruct((t, kd), F32)], [pl.BlockSpec((tm, kd), lambda i, k: (i, 0))],
        _ident, reduce_axis=1)[0]


def _mm_rs_dx(name, dy, w, layer, out_layout, out_dtype, tm=512):
    t, n = dy.shape
    ks = w.shape[2]
    tm = _tile(t, tm)
    return _gemm(
        name, (N_CHIPS, t // tm), [dy, w],
        [pl.BlockSpec((tm, n), lambda k, i: (i, 0)), _w_spec(w, layer, lambda k, i: k)],
        [[(0, 1, NT)]], None, [_act_shape(out_layout, t, ks, out_dtype)],
        [_act_spec(out_layout, tm, ks, lambda k, i: (i, k))], _ident)[0]


def _mm_dw(name, x, x_layout, dy, dy_layout, slot, alpha=1.0, tk=2048):
    stack, layer, layers = slot
    if x_layout is None:
        t, rows = x.shape
        cols = dy.shape[2] if dy_layout == "sm" else dy.shape[1] // N_CHIPS
        tk = _tile(t, tk)
        xspec = pl.BlockSpec((tk, rows), lambda k, j: (j, 0))
        yspec = _act_spec(dy_layout, tk, cols, lambda k, j: (j, k))
    else:
        t, cols = dy.shape
        rows = x.shape[2] if x_layout == "sm" else x.shape[1] // N_CHIPS
        tk = _tile(t, tk)
        xspec = _act_spec(x_layout, tk, rows, lambda k, j: (j, k))
        yspec = pl.BlockSpec((tk, cols), lambda k, j: (j, 0))
    operands, specs = [x, dy], [xspec, yspec]
    if stack is not None:
        operands.append(stack)
        specs.append(ANY)
    return _gemm(
        name, (N_CHIPS, t // tk), operands, specs, [[(0, 1, TN)]], [(rows, cols)],
        [jax.ShapeDtypeStruct((N_CHIPS, layers, rows, cols), F32)],
        [pl.BlockSpec((None, None, rows, cols), lambda k, j: (k, layer, 0, 0))],
        lambda vals, ins: [alpha * vals[0]], reduce_axis=1, aliases=None if stack is None else {2: 0})[0]


def _rows(name, fn, ins, outs, accs=(), tm=256):
    t = ins[0].shape[0]
    tm = _tile(t, tm)
    n_in, n_out, n_acc = len(ins), len(outs), len(accs)
    in_specs = []
    for a in ins:
        if a.shape[0] == t:
            in_specs.append(pl.BlockSpec((tm, a.shape[1]), lambda i: (i, 0)))
        else:
            in_specs.append(pl.BlockSpec(a.shape, lambda i: (0, 0)))
    out_shape = [jax.ShapeDtypeStruct((t, c), d) for c, d in outs] + [jax.ShapeDtypeStruct(s, F32) for s in accs]
    out_specs = [pl.BlockSpec((tm, c), lambda i: (i, 0)) for c, _ in outs] + [pl.BlockSpec(s, lambda i: (0, 0)) for s in accs]

    def body(*refs):
        i = pl.program_id(0)
        row_vals, acc_vals = fn(*[r[...] for r in refs[:n_in]])
        for o, v in zip(refs[n_in:n_in + n_out], row_vals):
            o[...] = v.astype(o.dtype)
        acc_refs = refs[n_in + n_out:]
        if n_acc:
            @pl.when(i == 0)
            def _():
                for a in acc_refs:
                    a[...] = jnp.zeros_like(a)

            for a, v in zip(acc_refs, acc_vals):
                a[...] += v

    res = pl.pallas_call(
        body, name=name, grid=(t // tm,), in_specs=in_specs, out_specs=out_specs, out_shape=out_shape,
        compiler_params=_params("arbitrary" if n_acc else "parallel"))(*ins)
    return res[:n_out], res[n_out:]


def _to_bf16(a):
    def fn(x):
        return [x], []
    return _rows("weights_bf16", fn, [a.reshape(-1, a.shape[-1])], [(a.shape[-1], BF16)], tm=512)[0][0].reshape(a.shape)


def _rms_stats(x):
    return lax.rsqrt(jnp.mean(x * x, axis=-1, keepdims=True) + RMS_EPS)


def _rmsnorm(name, h, g):
    def fn(x, gg):
        return [x * _rms_stats(x) * gg], []
    return _rows(name, fn, [h, g], [(h.shape[1], BF16)])[0][0]


def _rms_bwd_math(dn, x, g):
    r = _rms_stats(x)
    xhat = x * r
    dxh = dn * g
    dx = r * (dxh - xhat * jnp.mean(dxh * xhat, axis=-1, keepdims=True))
    return dx, jnp.sum(dn * xhat, axis=0, keepdims=True)


def _rmsnorm_bwd(name, dres, dn, h, g):
    def fn(dr, d, x, gg):
        dx, dg = _rms_bwd_math(d, x, gg)
        return [dr + dx], [dg]
    (dh,), (dg,) = _rows(name, fn, [dres, dn, h, g], [(h.shape[1], F32)], [(1, h.shape[1])])
    return dh, dg


def _ffn_fwd(h, g, w1, w3, w2, layer, tm=512):
    t, d = h.shape
    fs = w1.shape[2]
    n = _rmsnorm("ffn_norm", h, g)
    tm = _tile(t, tm)

    def up(vals, ins):
        a, b = vals
        return [a, b, a * _sigmoid(a) * b]

    sm = _act_shape("sm", t, fs, BF16)
    osp = _act_spec("sm", tm, fs, lambda k, i: (i, k))
    a, b, s = _gemm(
        "ffn_up", (N_CHIPS, t // tm), [n, w1, w3],
        [pl.BlockSpec((tm, d), lambda k, i: (i, 0)), _w_spec(w1, layer, lambda k, i: k), _w_spec(w3, layer, lambda k, i: k)],
        [[(0, 1, NT)], [(0, 2, NT)]], None, [sm, sm, sm], [osp, osp, osp], up)
    out = _mm_rs("ffn_down", s, "sm", w2, layer, res=h, alpha=0.5)
    return out, (h, n, a, b, s)


def _ffn_bwd(dout, saved, g, w1, w3, w2, layer, slots, tm=512):
    h, n, a, b, s = saved
    t, d = h.shape
    fs = w1.shape[2]
    tm = _tile(t, tm)
    dob = dout.astype(BF16)

    def down(vals, ins):
        ds = 0.5 * vals[0]
        av, bv = ins[2][...].astype(F32), ins[3][...].astype(F32)
        sg = _sigmoid(av)
        return [ds * bv * sg * (1.0 + av * (1.0 - sg)), ds * av * sg]

    sm = _act_shape("sm", t, fs, BF16)
    asp = _act_spec("sm", tm, fs, lambda k, i: (i, k))
    da, db = _gemm(
        "ffn_down_dx", (N_CHIPS, t // tm), [dob, w2, a, b],
        [pl.BlockSpec((tm, d), lambda k, i: (i, 0)), _w_spec(w2, layer, lambda k, i: k), asp, asp],
        [[(0, 1, NT)]], None, [sm, sm], [asp, asp], down)
    dw2 = _mm_dw("ffn_dw2", s, "sm", dob, None, slots[2], alpha=0.5)
    dw1 = _mm_dw("ffn_dw1", da, "sm", n, None, slots[0])
    dw3 = _mm_dw("ffn_dw3", db, "sm", n, None, slots[1])
    dn = _mm_cs_dx("ffn_up_dx", [(da, w1), (db, w3)], "sm", layer, transposed=True)
    dh, dg = _rmsnorm_bwd("ffn_norm_bwd", dout, dn, h, g)
    return dh, dg, dw1, dw3, dw2


def _ple_fwd(h, g, p2, wproj, wgate, layer):
    n = _rmsnorm("ple_norm", h, g)
    gl = _mm_rs("ple_gate", n, "flat", wgate, layer)
    pp = _mm_cs("ple_proj", p2, wproj, layer, "flat", F32)

    def fn(hh, gg, q):
        return [hh + q * _sigmoid(gg)], []
    out = _rows("ple_mix", fn, [h, gl, pp], [(h.shape[1], F32)])[0][0]
    return out, (h, n, gl, pp)


def _ple_bwd(dout, saved, g, p2, wproj, wgate, layer, slots):
    h, n, gl, pp = saved
    d = h.shape[1]

    def fn(do, gg, q):
        sg = _sigmoid(gg)
        return [do * sg, do * q * sg * (1.0 - sg)], []
    (dpp, dgl), _ = _rows("ple_mix_bwd", fn, [dout, gl, pp], [(d, BF16), (d, BF16)])
    dwproj = _mm_dw("ple_dwproj", p2, None, dpp, "flat", slots[0])
    dwgate = _mm_dw("ple_dwgate", n, "flat", dgl, None, slots[1])
    dn = _mm_rs_dx("ple_gate_dx", dgl, wgate, layer, "flat", F32)
    dh, dg = _rmsnorm_bwd("ple_norm_bwd", dout, dn, h, g)
    return dh, dg, dwproj, dwgate


def _head(h, g, target):
    d = h.shape[1]

    def fn(x, gg, tg):
        y = x * _rms_stats(x) * gg
        err = y - tg
        dy = err * (1.0 / d)
        dx, dg = _rms_bwd_math(dy, x, gg)
        loss = 0.5 * jnp.sum(jnp.sum(err * err, axis=-1, keepdims=True) * (1.0 / d), axis=0, keepdims=True)
        return [dx], [dg, jnp.broadcast_to(loss, (1, 128))]
    (dh,), (dg, loss) = _rows("loss_head", fn, [h, g, target], [(d, F32)], [(1, d), (1, 128)])
    return loss[0, 0], dh, dg


S5_LANES = 2 * S5_STATE
S5_GB = 128 // S5_GROUP


def _s5_prep(a_re, a_im, log_dt, b_re, b_im, c_re, c_im):
    c, gb = S5_CHUNK, S5_GB
    g = a_re.shape[0]
    nb = g // gb
    lam_re = jnp.minimum(a_re, -1e-4)
    lam_im = a_im
    dt = jnp.exp(log_dt)[:, None, None]
    ks = jnp.arange(c + 1, dtype=F32)
    mag = jnp.exp(lam_re[..., None] * dt * ks)
    ph = lam_im[..., None] * dt * ks
    pw_re, pw_im = mag * jnp.cos(ph), mag * jnp.sin(ph)
    den = lam_re * lam_re + lam_im * lam_im
    nr, ni = pw_re[..., 1] - 1.0, pw_im[..., 1]
    fr = (nr * lam_re + ni * lam_im) / den
    fi = (ni * lam_re - nr * lam_im) / den
    bb_re = fr[..., None] * b_re - fi[..., None] * b_im
    bb_im = fr[..., None] * b_im + fi[..., None] * b_re
    ct_re, ct_im = c_re.transpose(0, 2, 1), c_im.transpose(0, 2, 1)
    ca_re = ct_re[:, :, None, :] * pw_re[..., None] - ct_im[:, :, None, :] * pw_im[..., None]
    ca_im = ct_re[:, :, None, :] * pw_im[..., None] + ct_im[:, :, None, :] * pw_re[..., None]
    hp = lax.Precision.HIGHEST
    kern = (jnp.einsum("gpj,gpkh->gkjh", bb_re, ca_re[:, :, :c], precision=hp)
            - jnp.einsum("gpj,gpkh->gkjh", bb_im, ca_im[:, :, :c], precision=hp))
    rev_re = pw_re[:, :, :c][:, :, ::-1].transpose(0, 2, 1)
    rev_im = pw_im[:, :, :c][:, :, ::-1].transpose(0, 2, 1)
    bt_re, bt_im = bb_re.transpose(0, 2, 1), bb_im.transpose(0, 2, 1)
    wn_re = rev_re[:, :, None, :] * bt_re[:, None] - rev_im[:, :, None, :] * bt_im[:, None]
    wn_im = rev_re[:, :, None, :] * bt_im[:, None] + rev_im[:, :, None, :] * bt_re[:, None]
    wn = jnp.concatenate([wn_re, wn_im], axis=-1)
    wo = jnp.concatenate([ca_re[:, :, 1:].transpose(0, 2, 3, 1), -ca_im[:, :, 1:].transpose(0, 2, 3, 1)], axis=-1)

    def blocks(x):
        return x.reshape(nb, gb, c, S5_GROUP, x.shape[3]).transpose(0, 2, 1, 3, 4).reshape(nb, c, gb * S5_GROUP, x.shape[3])

    ar, ai = pw_re[..., c], pw_im[..., c]
    return (jnp.tile(blocks(kern), (1, 1, 1, gb)), blocks(wn), blocks(wo),
            jnp.concatenate([ar, ar], axis=1), jnp.concatenate([-ai, ai], axis=1))


def _step_rows(ref, tau, n):
    return ref[pl.ds(tau, n, stride=S5_CHUNK), :].astype(BF16)


def _cat_groups(ref, dtype):
    return jnp.concatenate([ref[:, j, :] for j in range(S5_GB)], axis=1).astype(dtype)


def _cat_steps(ref, n):
    return jnp.concatenate([_step_rows(ref, tau, n) for tau in range(S5_CHUNK)], axis=1)


def _stack_steps(ref, n):
    return jnp.concatenate([_step_rows(ref, tau, n) for tau in range(S5_CHUNK)], axis=0)


def _cat_ops(ref, axis, reverse=False):
    order = range(S5_CHUNK - 1, -1, -1) if reverse else range(S5_CHUNK)
    return jnp.concatenate([ref[k] for k in order], axis=axis)


def _row_group(rows, lanes):
    row = (lax.broadcasted_iota(jnp.int32, (rows, lanes), 0) // S5_GROUP) % S5_GB
    lane = (lax.broadcasted_iota(jnp.int32, (rows, lanes), 1) // S5_GROUP) % S5_GB
    return row, lane


def _own_group(x):
    row, lane = _row_group(*x.shape)
    return jnp.where(row == lane, x, jnp.zeros_like(x))


def _spread(x):
    row, _ = _row_group(*x.shape)
    return jnp.concatenate([jnp.where(row == j, x, jnp.zeros_like(x)) for j in range(S5_GB)], axis=1)


def _gather_own(x):
    row, _ = _row_group(x.shape[0], S5_LANES)
    out = jnp.zeros((x.shape[0], S5_LANES), x.dtype)
    for j in range(S5_GB):
        out = out + jnp.where(row == j, x[:, j * S5_LANES:(j + 1) * S5_LANES], 0.0)
    return out


def _s5_specs(t, d):
    nct, g = t // S5_CHUNK, d // S5_GROUP
    tok = pl.BlockSpec((t, 128), lambda i: (0, i))
    st = pl.BlockSpec((nct, S5_GB, S5_LANES), lambda i: (0, i, 0))
    op = lambda w: pl.BlockSpec((None,) + w.shape[1:], lambda i: (i, 0, 0, 0))
    return nct, g, tok, st, op


def _s5_chunk_fwd(u, bd, bn):
    t, d = u.shape
    nct, g, tok, st, op = _s5_specs(t, d)
    c = S5_CHUNK

    def body(u_ref, bd_ref, bn_ref, y_ref, s_ref):
        ucat = _cat_steps(u_ref, nct)
        sloc = jnp.dot(ucat, _spread(_cat_ops(bn_ref, 0)), preferred_element_type=F32)
        for j in range(S5_GB):
            s_ref[:, j, :] = sloc[:, j * S5_LANES:(j + 1) * S5_LANES]
        lags = _own_group(_cat_ops(bd_ref, 0, reverse=True))
        for tt in range(c):
            y_ref[pl.ds(tt, nct, stride=c), :] = jnp.dot(ucat[:, :(tt + 1) * 128], lags[(c - 1 - tt) * 128:, :],
                                                         preferred_element_type=F32)

    return pl.pallas_call(
        body, name="s5_chunk", grid=(d // 128,), in_specs=[tok, op(bd), op(bn)], out_specs=[tok, st],
        out_shape=[jax.ShapeDtypeStruct((t, d), F32), jax.ShapeDtypeStruct((nct, g, S5_LANES), F32)],
        compiler_params=_params("parallel"))(u, bd, bn)


def _s5_state_out(sprev, co, yin):
    t, d = yin.shape
    nct, g, tok, st, op = _s5_specs(t, d)
    c = S5_CHUNK

    def body(s_ref, co_ref, yi_ref, y_ref):
        ys = lax.dot_general(_cat_groups(s_ref, BF16), _spread(_cat_ops(co_ref, 0)), NT,
                             preferred_element_type=F32)
        for tt in range(c):
            rows = pl.ds(tt, nct, stride=c)
            y_ref[rows, :] = yi_ref[rows, :] + ys[:, tt * 128:(tt + 1) * 128]

    return pl.pallas_call(
        body, name="s5_state_out", grid=(d // 128,), in_specs=[st, op(co), tok], out_specs=tok,
        out_shape=jax.ShapeDtypeStruct((t, d), F32), compiler_params=_params("parallel"))(sprev, co, yin)


def _s5_state_out_dx(dyb, co):
    t, d = dyb.shape
    nct, g, tok, st, op = _s5_specs(t, d)
    c = S5_CHUNK

    def body(dy_ref, co_ref, ds_ref):
        acc = jnp.dot(_cat_steps(dy_ref, nct), _spread(_cat_ops(co_ref, 0)), preferred_element_type=F32)
        for j in range(S5_GB):
            ds_ref[:, j, :] = acc[:, j * S5_LANES:(j + 1) * S5_LANES]

    return pl.pallas_call(
        body, name="s5_state_out_dx", grid=(d // 128,), in_specs=[tok, op(co)], out_specs=st,
        out_shape=jax.ShapeDtypeStruct((nct, g, S5_LANES), F32), compiler_params=_params("parallel"))(dyb, co)


def _s5_chunk_dx(dyb, dsloc, bd, bn, skip):
    t, d = dyb.shape
    nct, g, tok, st, op = _s5_specs(t, d)
    c = S5_CHUNK

    def body(dy_ref, ds_ref, bd_ref, bn_ref, sk_ref, du_ref):
        dus = lax.dot_general(_cat_groups(ds_ref, BF16), _spread(_cat_ops(bn_ref, 0)), NT, preferred_element_type=F32)
        dycat = _cat_steps(dy_ref, nct)
        lags = _own_group(_cat_ops(bd_ref, 1))
        for tau in range(c):
            rows = pl.ds(tau, nct, stride=c)
            du_ref[rows, :] = (sk_ref[rows, :] + dus[:, tau * 128:(tau + 1) * 128]
                               + lax.dot_general(dycat[:, tau * 128:], lags[:, :(c - tau) * 128], NT,
                                                 preferred_element_type=F32))

    return pl.pallas_call(
        body, name="s5_chunk_dx", grid=(d // 128,), in_specs=[tok, st, op(bd), op(bn), tok], out_specs=tok,
        out_shape=jax.ShapeDtypeStruct((t, d), F32), compiler_params=_params("parallel"))(dyb, dsloc, bd, bn, skip)


def _s5_chunk_dw(u, dyb, dsloc, bd, bn):
    t, d = u.shape
    nct, g, tok, st, op = _s5_specs(t, d)
    c = S5_CHUNK

    def body(u_ref, dy_ref, ds_ref, dbd_ref, dbn_ref):
        dbn = _gather_own(lax.dot_general(_cat_steps(u_ref, nct), _cat_groups(ds_ref, BF16), TN,
                                          preferred_element_type=F32))
        for tau in range(c):
            dbn_ref[tau] = dbn[tau * 128:(tau + 1) * 128, :]
        ustk, dystk = _stack_steps(u_ref, nct), _stack_steps(dy_ref, nct)
        for k in range(c):
            dbd_ref[k] = _own_group(lax.dot_general(ustk[:(c - k) * nct], dystk[k * nct:], TN,
                                                    preferred_element_type=F32))

    return pl.pallas_call(
        body, name="s5_chunk_dw", grid=(d // 128,), in_specs=[tok, tok, st], out_specs=[op(bd), op(bn)],
        out_shape=[jax.ShapeDtypeStruct(bd.shape, F32), jax.ShapeDtypeStruct(bn.shape, F32)],
        compiler_params=_params("parallel"))(u, dyb, dsloc)


def _s5_state_out_dw(sprev, dyb, co):
    t, d = dyb.shape
    nct, g, tok, st, op = _s5_specs(t, d)
    c = S5_CHUNK

    def body(s_ref, dy_ref, dco_ref):
        dco = _gather_own(lax.dot_general(_cat_steps(dy_ref, nct), _cat_groups(s_ref, BF16), TN,
                                          preferred_element_type=F32))
        for tt in range(c):
            dco_ref[tt] = dco[tt * 128:(tt + 1) * 128, :]

    return pl.pallas_call(
        body, name="s5_state_out_dw", grid=(d // 128,), in_specs=[st, tok], out_specs=op(co),
        out_shape=jax.ShapeDtypeStruct(co.shape, F32), compiler_params=_params("parallel"))(sprev, dyb)


def _s5_scan_fwd(sloc, m1, m2):
    bl, nc, g, w = sloc.shape

    def body(s_ref, m1_ref, m2_ref, o_ref):
        a1, a2 = m1_ref[...], m2_ref[...]

        def step(c, states):
            new = []
            for b, s in enumerate(states):
                o_ref[b, c] = s
                new.append(a1 * s + a2 * pltpu.roll(s, S5_STATE, 1) + s_ref[b, c])
            return tuple(new)
        lax.fori_loop(0, nc, step, tuple(jnp.zeros((g, w), F32) for _ in range(bl)))

    vm = pl.BlockSpec(memory_space=pltpu.VMEM)
    return pl.pallas_call(
        body, name="s5_scan", in_specs=[vm, vm, vm], out_specs=vm,
        out_shape=jax.ShapeDtypeStruct(sloc.shape, F32),
        compiler_params=pltpu.CompilerParams(vmem_limit_bytes=VMEM_LIMIT))(sloc, m1, m2)


def _s5_scan_bwd(dsprev, sprev, m1, m2):
    bl, nc, g, w = dsprev.shape

    def body(d_ref, s_ref, m1_ref, m2_ref, g_ref, p1_ref, p2_ref):
        a1, a2 = m1_ref[...], m2_ref[...]
        zero = jnp.zeros((g, w), F32)

        def step(i, carry):
            gps, p1, p2 = carry
            c = nc - 2 - i
            new = []
            for b, gp in enumerate(gps):
                g_ref[b, c] = gp
                sp = s_ref[b, c]
                p1 = p1 + gp * sp
                p2 = p2 + gp * pltpu.roll(sp, S5_STATE, 1)
                new.append(d_ref[b, c] + a1 * gp - a2 * pltpu.roll(gp, S5_STATE, 1))
            return tuple(new), p1, p2

        for b in range(bl):
            g_ref[b, nc - 1] = zero
        _, p1, p2 = lax.fori_loop(0, nc - 1, step, (tuple(d_ref[b, nc - 1] for b in range(bl)), zero, zero))
        p1_ref[...] = p1
        p2_ref[...] = p2

    vm = pl.BlockSpec(memory_space=pltpu.VMEM)
    sd = jax.ShapeDtypeStruct
    return pl.pallas_call(
        body, name="s5_scan_bwd", in_specs=[vm, vm, vm, vm], out_specs=[vm, vm, vm],
        out_shape=[sd(dsprev.shape, F32), sd((g, w), F32), sd((g, w), F32)],
        compiler_params=pltpu.CompilerParams(vmem_limit_bytes=VMEM_LIMIT))(dsprev, sprev, m1, m2)


def _gelu_tanh_parts(y):
    c0 = math.sqrt(2.0 / math.pi)
    inner = c0 * (y + 0.044715 * y * y * y)
    th = jnp.tanh(inner)
    return th, c0 * (1.0 + 3 * 0.044715 * y * y)


def _s5_fwd(h, g, ops, d_skip, w_in, w_glu, bl):
    bd, bn, co, m1, m2 = ops
    t, d = h.shape
    nct, groups = t // S5_CHUNK, d // S5_GROUP
    hn = _rmsnorm("mix_norm", h, g)
    u = _mm_rs("s5_in", hn, "flat", w_in, 0)
    yin, sloc = _s5_chunk_fwd(u, bd.astype(BF16), bn.astype(BF16))
    sprev = _s5_scan_fwd(sloc.reshape(bl, nct // bl, groups, S5_LANES), m1, m2).reshape(nct, groups, S5_LANES)
    y = _s5_state_out(sprev, co.astype(BF16), yin)

    def fn(yy, uu, dd):
        y2 = yy + dd * uu
        th, _ = _gelu_tanh_parts(y2)
        return [0.5 * y2 * (1.0 + th)], []
    z = _rows("s5_gelu", fn, [y, u, d_skip], [(d, BF16)])[0][0]
    zz = _mm_cs("s5_glu", z, w_glu, 0, "flat", F32)

    def glu(hh, zv):
        return [hh + zv[:, :d] * _sigmoid(zv[:, d:])], []
    out = _rows("s5_glu_mix", glu, [h, zz], [(d, F32)])[0][0]
    return out, (h, hn, u, sprev, y, z, zz)


def _s5_bwd(dout, saved, g, ops, d_skip, w_in, w_glu, bl):
    h, hn, u, sprev, y, z, zz = saved
    bd, bn, co, m1, m2 = ops
    t, d = h.shape
    nct, groups = t // S5_CHUNK, d // S5_GROUP

    def glu_bwd(do, zv):
        sg = _sigmoid(zv[:, d:])
        return [jnp.concatenate([do * sg, do * zv[:, :d] * sg * (1.0 - sg)], axis=1)], []
    dzz = _rows("s5_glu_bwd", glu_bwd, [dout, zz], [(2 * d, BF16)])[0][0]
    dwglu = _mm_dw("s5_dwglu", z, None, dzz, "flat", (None, 0, 1))
    dz = _mm_cs_dx("s5_glu_dx", [(dzz, w_glu)], "flat", 0)

    def gelu_bwd(dzv, yy, uu, dd):
        y2 = yy + dd * uu
        th, dinner = _gelu_tanh_parts(y2)
        dy2 = dzv * (0.5 * (1.0 + th) + 0.5 * y2 * (1.0 - th * th) * dinner)
        return [dy2, dy2 * dd], [jnp.sum(dy2 * uu, axis=0, keepdims=True)]
    (dyb, du_skip), (dd,) = _rows("s5_gelu_bwd", gelu_bwd, [dz, y, u, d_skip], [(d, F32), (d, F32)], [(1, d)])
    bd_b, bn_b, co_b = bd.astype(BF16), bn.astype(BF16), co.astype(BF16)
    dsprev = _s5_state_out_dx(dyb, co_b)
    shape4 = (bl, nct // bl, groups, S5_LANES)
    dsloc, dm1, dm2 = _s5_scan_bwd(dsprev.reshape(shape4), sprev.reshape(shape4), m1, m2)
    dsloc = dsloc.reshape(nct, groups, S5_LANES)
    du = _s5_chunk_dx(dyb, dsloc, bd_b, bn_b, du_skip).astype(BF16)
    dbd, dbn = _s5_chunk_dw(u, dyb, dsloc, bd, bn)
    dco = _s5_state_out_dw(sprev, dyb, co)
    dwin = _mm_dw("s5_dwin", hn, "flat", du, None, (None, 0, 1))
    dhn = _mm_rs_dx("s5_in_dx", du, w_in, 0, "flat", F32)
    dh, dg = _rmsnorm_bwd("mix_norm_bwd", dout, dhn, h, g)
    return dh, dg, dwin, dwglu, dd, (dbd, dbn, dco, dm1, dm2)


def _sb_block(qi, idx, tb):
    kb = qi - idx
    return pl.multiple_of(jnp.maximum(kb, 0) * tb, tb), idx == 0, kb >= 0


def _sb_scores(q, kblk, diag, exists, row, col):
    z = lax.dot_general(q, kblk, NT, preferred_element_type=F32) * (SB_HEAD_DIM ** -0.5)
    l1 = jnp.log(1.0 + jnp.exp(-jnp.abs(z)))
    ls = jnp.minimum(z, 0.0) - l1
    mask = jnp.logical_and(jnp.logical_or(col < row, jnp.logical_not(diag)), exists)
    lk = jnp.where(mask, ls - z, 0.0)
    return ls, lk, mask


def _sb_more(qi, carry):
    j, cr = carry[0], carry[2]
    return jnp.logical_and(j <= qi, jnp.max(cr) > SB_CUT)


def _split_dot(v, tri):
    hi = v.astype(BF16)
    lo = (v - hi.astype(F32)).astype(BF16)
    return (jnp.dot(hi, tri, preferred_element_type=F32) + jnp.dot(lo, tri, preferred_element_type=F32))


def _sb_attn_fwd(q, k, v):
    bh, l, dh = q.shape
    tb = min(SB_BLOCK, l)
    nq = l // tb

    def body(q_ref, k_ref, v_ref, o_ref):
        qi = pl.program_id(1)
        qv = q_ref[...]
        row = lax.broadcasted_iota(jnp.int32, (tb, tb), 0)
        col = lax.broadcasted_iota(jnp.int32, (tb, tb), 1)
        tri = (row > col).astype(BF16)

        def step(carry):
            j, acc, cr = carry
            where = [_sb_block(qi, j + u, tb) for u in range(SB_UNROLL)]
            scores = [_sb_scores(qv, k_ref[pl.ds(ks, tb), :], diag, exists, row, col) for ks, diag, exists in where]
            laters = [_split_dot(lk, tri) for _, lk, _ in scores]
            for (ks, _, _), (ls, lk, mask), later in zip(where, scores, laters):
                att = jnp.where(mask, jnp.exp(ls + later + cr), 0.0)
                acc = acc + jnp.dot(att.astype(BF16), v_ref[pl.ds(ks, tb), :], preferred_element_type=F32)
                cr = cr + jnp.sum(lk, axis=1, keepdims=True)
            return j + SB_UNROLL, acc, cr

        _, acc, _ = lax.while_loop(functools.partial(_sb_more, qi), step,
                                   (jnp.int32(0), jnp.zeros((tb, dh), F32), jnp.zeros((tb, 1), F32)))
        o_ref[...] = acc

    blk = pl.BlockSpec((None, tb, dh), lambda b, i: (b, i, 0))
    full = pl.BlockSpec((None, l, dh), lambda b, i: (b, 0, 0))
    return pl.pallas_call(
        body, name="sb_attn", grid=(bh, nq), in_specs=[blk, full, full], out_specs=blk,
        out_shape=jax.ShapeDtypeStruct((bh, l, dh), F32), compiler_params=_params("parallel", "parallel"))(q, k, v)


def _sb_attn_bwd(q, k, v, o, do):
    bh, l, dh = q.shape
    tb = min(SB_BLOCK, l)
    nq = l // tb
    scale = SB_HEAD_DIM ** -0.5

    def body(q_ref, k_ref, v_ref, o_ref, do_ref, dq_ref, dk_ref, dv_ref):
        qi = pl.program_id(1)

        @pl.when(qi == 0)
        def _():
            dk_ref[...] = jnp.zeros_like(dk_ref)
            dv_ref[...] = jnp.zeros_like(dv_ref)

        qv = q_ref[...]
        dob = do_ref[...].astype(BF16)
        dsum = jnp.sum(dob.astype(F32) * o_ref[...], axis=1, keepdims=True)
        row = lax.broadcasted_iota(jnp.int32, (tb, tb), 0)
        col = lax.broadcasted_iota(jnp.int32, (tb, tb), 1)
        tri = (row > col).astype(BF16)
        tri_inc = (row >= col).astype(BF16)

        def step(carry):
            j, dq, cr, ce = carry
            n = range(SB_UNROLL)
            where = [_sb_block(qi, j + u, tb) for u in n]
            rows = [pl.ds(ks, tb) for ks, _, _ in where]
            scores = [_sb_scores(qv, k_ref[rows[u], :], where[u][1], where[u][2], row, col) for u in n]
            laters = [_split_dot(lk, tri) for _, lk, _ in scores]
            datts = [lax.dot_general(dob, v_ref[rows[u], :], NT, preferred_element_type=F32) for u in n]
            atts = []
            for (ls, lk, mask), later in zip(scores, laters):
                atts.append(jnp.where(mask, jnp.exp(ls + later + cr), 0.0).astype(BF16))
                cr = cr + jnp.sum(lk, axis=1, keepdims=True)
            es = [atts[u].astype(F32) * datts[u] for u in n]
            sufs = [_split_dot(e, tri_inc) for e in es]
            dzs = []
            for (ls, _, mask), e, suf in zip(scores, es, sufs):
                pre = dsum - ce - suf
                sg = jnp.exp(ls)
                dzs.append((jnp.where(mask, e * (1.0 - sg) - pre * sg, 0.0) * scale).astype(BF16))
                ce = ce + jnp.sum(e, axis=1, keepdims=True)
            for u in n:
                dq = dq + jnp.dot(dzs[u], k_ref[rows[u], :], preferred_element_type=F32)
                dk_ref[rows[u], :] += lax.dot_general(dzs[u], qv, TN, preferred_element_type=F32)
                dv_ref[rows[u], :] += lax.dot_general(atts[u], dob, TN, preferred_element_type=F32)
            return j + SB_UNROLL, dq, cr, ce

        zc = jnp.zeros((tb, 1), F32)
        _, dq, _, _ = lax.while_loop(functools.partial(_sb_more, qi), step,
                                     (jnp.int32(0), jnp.zeros((tb, dh), F32), zc, zc))
        dq_ref[...] = dq

    blk = pl.BlockSpec((None, tb, dh), lambda b, i: (b, i, 0))
    full = pl.BlockSpec((None, l, dh), lambda b, i: (b, 0, 0))
    sd = jax.ShapeDtypeStruct((bh, l, dh), F32)
    return pl.pallas_call(
        body, name="sb_attn_bwd", grid=(bh, nq), in_specs=[blk, full, full, blk, blk], out_specs=[blk, full, full],
        out_shape=[sd, sd, sd], compiler_params=_params("parallel", "arbitrary"))(q, k, v, o, do)


SB_PAIR = 2 * SB_HEAD_DIM


def _pair_masks(tb):
    lane = lax.broadcasted_iota(jnp.int32, (1, SB_PAIR), 1)
    row = lax.broadcasted_iota(jnp.int32, (tb, tb), 0)
    col = lax.broadcasted_iota(jnp.int32, (tb, tb), 1)
    return [lane < SB_HEAD_DIM, lane >= SB_HEAD_DIM], row, col


def _pair_more(qi, carry):
    j, crs = carry[0], carry[2]
    return jnp.logical_and(j <= qi, jnp.maximum(jnp.max(crs[0]), jnp.max(crs[1])) > SB_CUT)


def _pair_specs(bl, l, d, tb):
    nq, off = l // tb, d // SB_PAIR
    qspec = pl.BlockSpec((tb, SB_PAIR), lambda b, p, i: (b * nq + i, p))
    kspec = pl.BlockSpec((l, SB_PAIR), lambda b, p, i: (b, off + p))
    vspec = pl.BlockSpec((l, SB_PAIR), lambda b, p, i: (b, 2 * off + p))
    return qspec, kspec, vspec


def _sb_attn_fwd2(qkv, bl):
    t, d3 = qkv.shape
    d, l = d3 // 3, t // bl
    tb = min(SB_BLOCK, l)
    nq = l // tb

    def body(q_ref, k_ref, v_ref, o_ref, ob_ref):
        qi = pl.program_id(2)
        heads, row, col = _pair_masks(tb)
        qv = q_ref[...]
        qh = [jnp.where(m, qv, jnp.zeros_like(qv)) for m in heads]
        tri = (row > col).astype(BF16)

        def step(carry):
            j, acc, crs = carry
            crs = list(crs)
            where = [_sb_block(qi, j + u, tb) for u in range(SB_UNROLL)]
            kblks = [k_ref[pl.ds(ks, tb), :] for ks, _, _ in where]
            scores = [[_sb_scores(qh[hd], kblks[u], where[u][1], where[u][2], row, col) for hd in range(2)]
                      for u in range(SB_UNROLL)]
            laters = [[_split_dot(sc[1], tri) for sc in su] for su in scores]
            for u in range(SB_UNROLL):
                vblk = v_ref[pl.ds(where[u][0], tb), :]
                outs = []
                for hd in range(2):
                    ls, lk, mask = scores[u][hd]
                    att = jnp.where(mask, jnp.exp(ls + laters[u][hd] + crs[hd]), 0.0)
                    outs.append(jnp.dot(att.astype(BF16), vblk, preferred_element_type=F32))
                    crs[hd] = crs[hd] + jnp.sum(lk, axis=1, keepdims=True)
                acc = acc + jnp.where(heads[0], outs[0], outs[1])
            return j + SB_UNROLL, acc, tuple(crs)

        zc = jnp.zeros((tb, 1), F32)
        _, acc, _ = lax.while_loop(functools.partial(_pair_more, qi), step,
                                   (jnp.int32(0), jnp.zeros((tb, SB_PAIR), F32), (zc, zc)))
        o_ref[...] = acc
        ob_ref[...] = acc.astype(BF16)

    qspec, kspec, vspec = _pair_specs(bl, l, d, tb)
    return pl.pallas_call(
        body, name="sb_attn", grid=(bl, d // SB_PAIR, nq), in_specs=[qspec, kspec, vspec], out_specs=[qspec, qspec],
        out_shape=[jax.ShapeDtypeStruct((t, d), F32), jax.ShapeDtypeStruct((t, d), BF16)],
        compiler_params=_params("parallel", "parallel", "parallel"))(qkv, qkv, qkv)


def _sb_attn_bwd2(qkv, o, do, bl):
    t, d3 = qkv.shape
    d, l = d3 // 3, t // bl
    tb = min(SB_BLOCK, l)
    nq = l // tb
    scale = SB_HEAD_DIM ** -0.5

    def body(q_ref, k_ref, v_ref, o_ref, do_ref, dq_ref, dk_ref, dv_ref, dk_acc, dv_acc):
        qi = pl.program_id(2)

        @pl.when(qi == 0)
        def _():
            dk_acc[...] = jnp.zeros_like(dk_acc)
            dv_acc[...] = jnp.zeros_like(dv_acc)

        heads, row, col = _pair_masks(tb)
        qv = q_ref[...]
        dov = do_ref[...].astype(BF16)
        qh = [jnp.where(m, qv, jnp.zeros_like(qv)) for m in heads]
        doh = [jnp.where(m, dov, jnp.zeros_like(dov)) for m in heads]
        ov = o_ref[...]
        dsum = [jnp.sum(dh.astype(F32) * ov, axis=1, keepdims=True) for dh in doh]
        tri = (row > col).astype(BF16)
        tri_inc = (row >= col).astype(BF16)

        def step(carry):
            j, dq, crs, ces = carry
            crs, ces = list(crs), list(ces)
            n = range(SB_UNROLL)
            where = [_sb_block(qi, j + u, tb) for u in n]
            rows = [pl.ds(ks, tb) for ks, _, _ in where]
            kblks = [k_ref[rows[u], :] for u in n]
            vblks = [v_ref[rows[u], :] for u in n]
            scores = [[_sb_scores(qh[hd], kblks[u], where[u][1], where[u][2], row, col) for hd in range(2)] for u in n]
            laters = [[_split_dot(sc[1], tri) for sc in su] for su in scores]
            datts = [[lax.dot_general(doh[hd], vblks[u], NT, preferred_element_type=F32) for hd in range(2)] for u in n]
            atts = [[None, None] for _ in n]
            for u in n:
                for hd in range(2):
                    ls, lk, mask = scores[u][hd]
                    atts[u][hd] = jnp.where(mask, jnp.exp(ls + laters[u][hd] + crs[hd]), 0.0).astype(BF16)
                    crs[hd] = crs[hd] + jnp.sum(lk, axis=1, keepdims=True)
            es = [[atts[u][hd].astype(F32) * datts[u][hd] for hd in range(2)] for u in n]
            sufs = [[_split_dot(e, tri_inc) for e in eu] for eu in es]
            dzs = [[None, None] for _ in n]
            for u in n:
                for hd in range(2):
                    ls, _, mask = scores[u][hd]
                    pre = dsum[hd] - ces[hd] - sufs[u][hd]
                    sg = jnp.exp(ls)
                    dzs[u][hd] = (jnp.where(mask, es[u][hd] * (1.0 - sg) - pre * sg, 0.0) * scale).astype(BF16)
                    ces[hd] = ces[hd] + jnp.sum(es[u][hd], axis=1, keepdims=True)
            for u in n:
                dq = dq + jnp.where(heads[0], jnp.dot(dzs[u][0], kblks[u], preferred_element_type=F32),
                                    jnp.dot(dzs[u][1], kblks[u], preferred_element_type=F32))
                dk_acc[rows[u], :] += (lax.dot_general(dzs[u][0], qh[0], TN, preferred_element_type=F32)
                                       + lax.dot_general(dzs[u][1], qh[1], TN, preferred_element_type=F32))
                dv_acc[rows[u], :] += (lax.dot_general(atts[u][0], doh[0], TN, preferred_element_type=F32)
                                       + lax.dot_general(atts[u][1], doh[1], TN, preferred_element_type=F32))
            return j + SB_UNROLL, dq, tuple(crs), tuple(ces)

        zc = jnp.zeros((tb, 1), F32)
        _, dq, _, _ = lax.while_loop(functools.partial(_pair_more, qi), step,
                                     (jnp.int32(0), jnp.zeros((tb, SB_PAIR), F32), (zc, zc), (zc, zc)))
        dq_ref[...] = dq.astype(BF16)

        @pl.when(qi == nq - 1)
        def _():
            dk_ref[...] = dk_acc[...].astype(BF16)
            dv_ref[...] = dv_acc[...].astype(BF16)

    qspec, kspec, vspec = _pair_specs(bl, l, d, tb)
    blk = pl.BlockSpec((tb, SB_PAIR), lambda b, p, i: (b * nq + i, p))
    full = pl.BlockSpec((l, SB_PAIR), lambda b, p, i: (b, p))
    sd = jax.ShapeDtypeStruct((t, d), BF16)
    dq, dk, dv = pl.pallas_call(
        body, name="sb_attn_bwd", grid=(bl, d // SB_PAIR, nq), in_specs=[qspec, kspec, vspec, blk, blk],
        out_specs=[blk, full, full], out_shape=[sd, sd, sd],
        scratch_shapes=[pltpu.VMEM((l, SB_PAIR), F32), pltpu.VMEM((l, SB_PAIR), F32)],
        compiler_params=_params("parallel", "parallel", "arbitrary"))(qkv, qkv, qkv, o, do)
    return jnp.concatenate([dq, dk, dv], axis=1)


def _to_heads(x, bl):
    t, w = x.shape
    heads = w // SB_HEAD_DIM
    l = t // bl
    return x.reshape(bl, l, heads, SB_HEAD_DIM).transpose(0, 2, 1, 3).reshape(bl * heads, l, SB_HEAD_DIM)


def _from_heads(x, bl):
    bh, l, dh = x.shape
    heads = bh // bl
    return x.reshape(bl, heads, l, dh).transpose(0, 2, 1, 3).reshape(bl * l, heads * dh)


def _sb_fwd(h, g, w_qkv, w_o, bl):
    t, d = h.shape
    hn = _rmsnorm("mix_norm", h, g)
    qkv = _mm_cs("sb_qkv", hn, w_qkv, 0, "flat", BF16)
    o, ob = _sb_attn_fwd2(qkv, bl)
    out = _mm_rs("sb_out", ob, "flat", w_o, 0, res=h)
    return out, (h, hn, qkv, o, ob)


def _sb_bwd(dout, saved, g, w_qkv, w_o, bl):
    h, hn, qkv, o, ob = saved
    dob = dout.astype(BF16)
    dwo = _mm_dw("sb_dwo", ob, "flat", dob, None, (None, 0, 1))
    do = _mm_rs_dx("sb_out_dx", dob, w_o, 0, "flat", F32)
    dqkv = _sb_attn_bwd2(qkv, o, do, bl)
    dwqkv = _mm_dw("sb_dwqkv", hn, None, dqkv, "flat", (None, 0, 1))
    dhn = _mm_cs_dx("sb_qkv_dx", [(dqkv, w_qkv)], "flat", 0)
    dh, dg = _rmsnorm_bwd("mix_norm_bwd", dout, dhn, h, g)
    return dh, dg, dwqkv, dwo


def _adamw_update(wv, gr, mv, vv):
    c1 = 1.0 / (1.0 - ADAM_B1 ** ADAM_STEP)
    c2 = 1.0 / (1.0 - ADAM_B2 ** ADAM_STEP)
    mn = ADAM_B1 * mv + (1.0 - ADAM_B1) * gr
    vn = ADAM_B2 * vv + (1.0 - ADAM_B2) * gr * gr
    delta = -ADAM_LR * ((mn * c1) / (jnp.sqrt(vn * c2) + ADAM_EPS) + ADAM_WD * wv)
    return delta, mn, vn


def _adamw_small(w, gr, m, v):
    def fn(wv, gv, mv, vv):
        return list(_adamw_update(wv, gv, mv, vv)), []
    return _rows("adamw_small", fn, [w, gr, m, v], [(w.shape[1], F32)] * 3)[0]


def _place():
    x, y, c = lax.axis_index("x"), lax.axis_index("y"), lax.axis_index("c")
    chips = [(1 - x, y), (x, 1 - y), (1 - x, 1 - y)]
    return x, y, c, chips


def _remote(src, dst, send_sem, recv_sem, to):
    return pltpu.make_async_remote_copy(src_ref=src, dst_ref=dst, send_sem=send_sem, recv_sem=recv_sem,
                                        device_id=to, device_id_type=MESH)


def _half(ref, c, rh, lead):
    return ref.at[(slice(None),) * lead + (pl.ds(c * rh, rh),)]


def _allgather_weights(ws):
    n = len(ws)

    def body(*refs):
        ins, outs = refs[:n], refs[n:2 * n]
        send, recv = refs[2 * n:]
        x, y, c, _ = _place()
        chip_x, chip_y, chip_d = (1 - x, y), (x, 1 - y), (1 - x, 1 - y)
        sibling = (x, y, 1 - c)
        index = lambda chip: 2 * chip[0] + chip[1]
        sent = []

        def quarter(ref, half, q, rq):
            return ref.at[:, pl.ds((2 * half + q) * rq, rq)]

        def copy(t, kind, src, dst, to):
            return _remote(src, dst, send.at[t, kind], recv.at[t, kind], to)

        def start(cp):
            cp.start()
            sent.append(cp)

        for t in range(n):
            rq = ws[t].shape[1] // 4
            for q in range(2):
                for base, chip in ((0, chip_x), (2, chip_y)):
                    start(copy(t, base + q, quarter(ins[t], c, q, rq), quarter(outs[t].at[index((x, y))], c, q, rq), (*chip, c)))
        for t in range(n):
            rq = ws[t].shape[1] // 4
            landings = [(chip_x, 0, 0, chip_x, ((4, chip_y), (6, None))), (chip_y, 1, 3, chip_y, ((5, chip_x), (9, None))),
                        (chip_x, 1, 1, chip_x, ((7, None),)), (chip_y, 0, 2, chip_y, ((8, None),)),
                        (chip_d, 0, 4, chip_y, ((10, None),)), (chip_d, 1, 5, chip_x, ((11, None),))]
            for origin, q, kind, sender, onward in landings:
                piece = quarter(outs[t].at[index(origin)], c, q, rq)
                copy(t, kind, piece, piece, (*sender, c)).wait_recv()
                for kind2, chip in onward:
                    start(copy(t, kind2, piece, piece, sibling if chip is None else (*chip, c)))
        for t in range(n):
            rq = ws[t].shape[1] // 4
            for kind, (origin, q) in zip(range(6, 12), ((chip_x, 0), (chip_x, 1), (chip_y, 0), (chip_y, 1), (chip_d, 0), (chip_d, 1))):
                piece = quarter(outs[t].at[index(origin)], 1 - c, q, rq)
                copy(t, kind, piece, piece, sibling).wait_recv()
        for cp in sent:
            cp.wait_send()

    res = pl.pallas_call(
        body, name="allgather_weights", in_specs=[ANY] * n, out_specs=[ANY] * n,
        out_shape=[jax.ShapeDtypeStruct((N_CHIPS,) + w.shape, w.dtype) for w in ws],
        scratch_shapes=[pltpu.SemaphoreType.DMA((n, 12)), pltpu.SemaphoreType.DMA((n, 12))],
    )(*ws)
    own = 2 * lax.axis_index("x") + lax.axis_index("y")
    return [lax.dynamic_update_slice(g, w[None], (own, 0, 0, 0)) for g, w in zip(res, ws)]


def _pair_exchange(gs):
    n = len(gs)

    def body(*refs):
        ins, outs = refs[:n], refs[n:2 * n]
        send, recv = refs[2 * n:]
        x, y, c, _ = _place()
        copies = [_remote(_half(ins[t], 1 - c, gs[t].shape[2] // 2, 2), outs[t], send.at[t], recv.at[t], (x, y, 1 - c))
                  for t in range(n)]
        for cp in copies:
            cp.start()
        for cp in copies:
            cp.wait()

    return pl.pallas_call(
        body, name="grad_pair_exchange", in_specs=[ANY] * n, out_specs=[ANY] * n,
        out_shape=[jax.ShapeDtypeStruct(g.shape[:2] + (g.shape[2] // 2, g.shape[3]), F32) for g in gs],
        scratch_shapes=[pltpu.SemaphoreType.DMA((n,)), pltpu.SemaphoreType.DMA((n,))],
    )(*gs)


def _pair_sum(g, theirs, c_idx):
    n4, ly, r, cc = g.shape
    rh = r // 2
    tm = _tile(rh, 256)
    nt = rh // tm

    def body(c_ref, g_ref, t_ref, o_ref):
        o_ref[...] = (g_ref[...] + t_ref[...]).astype(o_ref.dtype)

    blk = (None, tm, cc)
    grid_spec = pltpu.PrefetchScalarGridSpec(
        num_scalar_prefetch=1, grid=(n4 * ly, nt),
        in_specs=[pl.BlockSpec(blk, lambda a, i, cr: (a, cr[0] * nt + i, 0)), pl.BlockSpec(blk, lambda a, i, cr: (a, i, 0))],
        out_specs=pl.BlockSpec(blk, lambda a, i, cr: (a, i, 0)))
    out = pl.pallas_call(
        body, name="grad_pair_sum", grid_spec=grid_spec, out_shape=jax.ShapeDtypeStruct((n4 * ly, rh, cc), BF16),
        compiler_params=_params("parallel", "parallel"))(c_idx, g.reshape(n4 * ly, r, cc), theirs.reshape(n4 * ly, rh, cc))
    return out.reshape(n4, ly, rh, cc)


def _chip_exchange(ps):
    n = len(ps)

    def body(*refs):
        ins, outs = refs[:n], refs[n:2 * n]
        send, recv = refs[2 * n:]
        x, y, c, chips = _place()
        copies = []
        for t in range(n):
            for j, chip in enumerate(chips):
                copies.append(_remote(ins[t].at[2 * chip[0] + chip[1]], outs[t].at[j], send.at[t, j], recv.at[t, j], (*chip, c)))
        for cp in copies:
            cp.start()
        for cp in copies:
            cp.wait()

    return pl.pallas_call(
        body, name="grad_chip_exchange", in_specs=[ANY] * n, out_specs=[ANY] * n,
        out_shape=[jax.ShapeDtypeStruct((3,) + p.shape[1:], p.dtype) for p in ps],
        scratch_shapes=[pltpu.SemaphoreType.DMA((n, 3)), pltpu.SemaphoreType.DMA((n, 3))],
    )(*ps)


def _chip_sum(p, landed, own_idx):
    _, ly, rh, cc = p.shape
    tm = _tile(rh, 256)

    def body(o_ref, p_ref, a_ref, b_ref, c_ref, out_ref):
        up = lambda r: r[...].astype(F32)
        out_ref[...] = ((up(p_ref) + up(a_ref)) + up(b_ref)) + up(c_ref)

    blk = (None, None, tm, cc)
    slot = lambda j: pl.BlockSpec(blk, lambda l, i, o: (j, l, i, 0))
    grid_spec = pltpu.PrefetchScalarGridSpec(
        num_scalar_prefetch=1, grid=(ly, rh // tm),
        in_specs=[pl.BlockSpec(blk, lambda l, i, o: (o[0], l, i, 0)), slot(0), slot(1), slot(2)],
        out_specs=pl.BlockSpec((None, tm, cc), lambda l, i, o: (l, i, 0)))
    return pl.pallas_call(
        body, name="grad_chip_sum", grid_spec=grid_spec, out_shape=jax.ShapeDtypeStruct((ly, rh, cc), F32),
        compiler_params=_params("parallel", "parallel"))(own_idx, p, landed, landed, landed)


def _pair_swap(halves):
    n = len(halves)

    def body(*refs):
        ins, outs = refs[:n], refs[n:2 * n]
        send, recv = refs[2 * n:]
        x, y, c, _ = _place()
        copies = [_remote(ins[t], outs[t], send.at[t], recv.at[t], (x, y, 1 - c)) for t in range(n)]
        for cp in copies:
            cp.start()
        for cp in copies:
            cp.wait()

    return pl.pallas_call(
        body, name="grad_pair_swap", in_specs=[ANY] * n, out_specs=[ANY] * n,
        out_shape=[jax.ShapeDtypeStruct(h.shape, F32) for h in halves],
        scratch_shapes=[pltpu.SemaphoreType.DMA((n,)), pltpu.SemaphoreType.DMA((n,))],
    )(*halves)


def _adamw_big(w, m, v, mine, theirs, c_idx):
    ly, r, cc = w.shape
    rh = r // 2
    tm = _tile(rh, 256)
    nt = rh // tm

    def body(c_ref, w_ref, m_ref, v_ref, a_ref, b_ref, g_out, d_out, m_out, v_out):
        gr = jnp.where(pl.program_id(1) == c_ref[0], a_ref[...], b_ref[...])
        delta, mn, vn = _adamw_update(w_ref[...], gr, m_ref[...], v_ref[...])
        g_out[...] = gr
        d_out[...] = delta
        m_out[...] = mn
        v_out[...] = vn

    blk = (None, tm, cc)
    full = pl.BlockSpec(blk, lambda l, hc, i, cr: (l, hc * nt + i, 0))
    half = pl.BlockSpec(blk, lambda l, hc, i, cr: (l, i, 0))
    grid_spec = pltpu.PrefetchScalarGridSpec(
        num_scalar_prefetch=1, grid=(ly, 2, nt), in_specs=[full, full, full, half, half], out_specs=[full] * 4)
    sd = jax.ShapeDtypeStruct(w.shape, F32)
    return pl.pallas_call(
        body, name="adamw", grid_spec=grid_spec, out_shape=[sd] * 4,
        compiler_params=_params("parallel", "parallel", "parallel"))(c_idx, w, m, v, mine, theirs)


def _allreduce_small(v):
    rows, w = v.shape

    def body(x_ref, sum_ref, all_ref, send, recv, local):
        x, y, c, chips = _place()
        me, sibling = (x, y, c), (x, y, 1 - c)

        def slot(px, py, pc):
            return all_ref.at[4 * px + 2 * py + pc]

        def copy(k, block, to, src=None):
            return _remote(slot(*block) if src is None else src, slot(*block), send.at[k], recv.at[k], to)

        mine = pltpu.make_async_copy(x_ref, slot(*me), local)
        mine.start()
        first = [copy(0, me, sibling, src=x_ref)]
        first += [copy(1 + j, me, (*chip, c), src=x_ref) for j, chip in enumerate(chips)]
        for cp in first:
            cp.start()
        passed = [copy(4 + j, (*chip, c), sibling) for j, chip in enumerate(chips)]
        for j, chip in enumerate(chips):
            copy(1 + j, (*chip, c), me).wait_recv()
            passed[j].start()
        copy(0, sibling, me).wait_recv()
        for j, chip in enumerate(chips):
            copy(4 + j, (*chip, 1 - c), me).wait_recv()
        for cp in first + passed:
            cp.wait_send()
        mine.wait()
        tot = all_ref[0]
        for k in range(1, N_DEV):
            tot = tot + all_ref[k]
        sum_ref[...] = tot

    vm = pl.BlockSpec(memory_space=pltpu.VMEM)
    return pl.pallas_call(
        body, name="allreduce_small", in_specs=[vm], out_specs=[vm, vm],
        out_shape=[jax.ShapeDtypeStruct((rows, w), F32), jax.ShapeDtypeStruct((N_DEV, rows, w), F32)],
        scratch_shapes=[pltpu.SemaphoreType.DMA((7,)), pltpu.SemaphoreType.DMA((7,)), pltpu.SemaphoreType.DMA],
        compiler_params=pltpu.CompilerParams(vmem_limit_bytes=VMEM_LIMIT),
    )(v)[0]


BIG = ["ffn1_w1", "ffn1_w3", "ffn1_w2", "ffn2_w1", "ffn2_w3", "ffn2_w2", "ple_proj", "ple_gate",
       "s5_w_in", "s5_w_glu", "sb_w_qkv", "sb_w_o"]
TRANSPOSED = ("ffn1_w1", "ffn1_w3", "ffn2_w1", "ffn2_w3")
SMALL = ["ffn1_norm", "mix_norm", "ffn2_norm", "ple_norm", "s5_a_re", "s5_a_im", "s5_log_dt", "s5_b_re", "s5_b_im",
         "s5_c_re", "s5_c_im", "s5_d", "final_norm"]
ORDER = ["ffn1_norm", "ffn1_w1", "ffn1_w3", "ffn1_w2", "mix_norm", "ffn2_norm", "ffn2_w1", "ffn2_w3", "ffn2_w2",
         "ple_norm", "ple_proj", "ple_gate", "s5_w_in", "s5_a_re", "s5_a_im", "s5_log_dt", "s5_b_re", "s5_b_im",
         "s5_c_re", "s5_c_im", "s5_d", "s5_w_glu", "sb_w_qkv", "sb_w_o", "final_norm"]


def _pack(arrays):
    flat = jnp.concatenate([a.reshape(-1) for a in arrays])
    pad = (-flat.shape[0]) % 1024
    return jnp.pad(flat, (0, pad)).reshape(-1, 128)


def _unpack(packed, like):
    flat = packed.reshape(-1)
    out, off = [], 0
    for a in like:
        out.append(flat[off:off + a.size].reshape(a.shape))
        off += a.size
    return out


def _fwd_bwd(x, p, target, w, gathered):
    bl, l, d = x.shape
    t = bl * l
    depth = w["ffn1_norm"].shape[0]
    s5_ops, s5_vjp = jax.vjp(_s5_prep, w["s5_a_re"][0], w["s5_a_im"][0], w["s5_log_dt"][0], w["s5_b_re"][0],
                             w["s5_b_im"][0], w["s5_c_re"][0], w["s5_c_im"][0])

    h = x.reshape(t, d)
    p2 = [p[i].reshape(t, p.shape[-1]).astype(BF16) for i in range(depth)]
    saved = []
    for i in range(depth):
        norm = lambda name: w[name][i:i + 1]
        h, s1 = _ffn_fwd(h, norm("ffn1_norm"), gathered["ffn1_w1"], gathered["ffn1_w3"], gathered["ffn1_w2"], i)
        if i % 2 == 0:
            h, s2 = _s5_fwd(h, norm("mix_norm"), s5_ops, w["s5_d"][i // 2:i // 2 + 1], gathered["s5_w_in"], gathered["s5_w_glu"], bl)
        else:
            h, s2 = _sb_fwd(h, norm("mix_norm"), gathered["sb_w_qkv"], gathered["sb_w_o"], bl)
        h, s3 = _ffn_fwd(h, norm("ffn2_norm"), gathered["ffn2_w1"], gathered["ffn2_w3"], gathered["ffn2_w2"], i)
        h, s4 = _ple_fwd(h, norm("ple_norm"), p2[i], gathered["ple_proj"], gathered["ple_gate"], i)
        saved.append((s1, s2, s3, s4))

    loss, dh, dfinal = _head(h, w["final_norm"].reshape(1, d), target.reshape(t, d))

    big = {k: None for k in BIG}
    small = {k: [None] * w[k].shape[0] if w[k].ndim > 1 else None for k in SMALL}
    small["final_norm"] = dfinal.reshape(d)
    for i in reversed(range(depth)):
        norm = lambda name: w[name][i:i + 1]
        slots = lambda *names: [(big[k], i, depth) for k in names]
        s1, s2, s3, s4 = saved[i]
        dh, dg, big["ple_proj"], big["ple_gate"] = _ple_bwd(
            dh, s4, norm("ple_norm"), p2[i], gathered["ple_proj"], gathered["ple_gate"], i, slots("ple_proj", "ple_gate"))
        small["ple_norm"][i] = dg[0]
        dh, dg, big["ffn2_w1"], big["ffn2_w3"], big["ffn2_w2"] = _ffn_bwd(
            dh, s3, norm("ffn2_norm"), gathered["ffn2_w1"], gathered["ffn2_w3"], gathered["ffn2_w2"], i,
            slots("ffn2_w1", "ffn2_w3", "ffn2_w2"))
        small["ffn2_norm"][i] = dg[0]
        if i % 2 == 0:
            dh, dg, big["s5_w_in"], big["s5_w_glu"], dd, dops = _s5_bwd(
                dh, s2, norm("mix_norm"), s5_ops, w["s5_d"][i // 2:i // 2 + 1], gathered["s5_w_in"], gathered["s5_w_glu"], bl)
            small["s5_d"][0] = dd[0]
            raw = s5_vjp(dops)
            for name, gr in zip(["s5_a_re", "s5_a_im", "s5_log_dt", "s5_b_re", "s5_b_im", "s5_c_re", "s5_c_im"], raw):
                small[name][0] = gr
        else:
            dh, dg, big["sb_w_qkv"], big["sb_w_o"] = _sb_bwd(dh, s2, norm("mix_norm"), gathered["sb_w_qkv"], gathered["sb_w_o"], bl)
        small["mix_norm"][i] = dg[0]
        dh, dg, big["ffn1_w1"], big["ffn1_w3"], big["ffn1_w2"] = _ffn_bwd(
            dh, s1, norm("ffn1_norm"), gathered["ffn1_w1"], gathered["ffn1_w3"], gathered["ffn1_w2"], i,
            slots("ffn1_w1", "ffn1_w3", "ffn1_w2"))
        small["ffn1_norm"][i] = dg[0]
    small_list = [jnp.stack(small[k]) if isinstance(small[k], list) else small[k] for k in SMALL]
    return loss, dh.reshape(bl, l, d), big, small_list


def _step(x, p, target, w, m, v):
    flip = lambda tree: {k: jnp.swapaxes(a, 1, 2) if k in TRANSPOSED else a for k, a in tree.items()}
    w, m, v = flip(w), flip(m), flip(v)
    gathered = dict(zip(BIG, _allgather_weights([_to_bf16(w[k]) for k in BIG])))
    loss, grad_x, big, small_list = _fwd_bwd(x, p, target, w, gathered)

    c_idx = lax.axis_index("c").astype(jnp.int32).reshape(1)
    own_idx = (2 * lax.axis_index("x") + lax.axis_index("y")).astype(jnp.int32).reshape(1)
    partial = [big[k] for k in BIG]
    pair = [_pair_sum(g, t, c_idx) for g, t in zip(partial, _pair_exchange(partial))]
    mine = [_chip_sum(pr, ld, own_idx) for pr, ld in zip(pair, _chip_exchange(pair))]
    theirs = _pair_swap(mine)
    out_g, out_d, out_m, out_v = {}, {}, {}, {}
    for k, a, b in zip(BIG, mine, theirs):
        out_g[k], out_d[k], out_m[k], out_v[k] = _adamw_big(w[k], m[k], v[k], a, b, c_idx)

    like = [w[k] for k in SMALL]
    pad = [jnp.zeros((1,), F32)]
    g_small = _allreduce_small(_pack(small_list + [loss.reshape(1)]))
    packed = (g_small,) + tuple(_adamw_small(_pack(like + pad), g_small, _pack([m[k] for k in SMALL] + pad),
                                             _pack([v[k] for k in SMALL] + pad)))
    for dst, pk in zip((out_g, out_d, out_m, out_v), packed):
        dst.update(dict(zip(SMALL, _unpack(pk, like))))
    loss = g_small.reshape(-1)[sum(a.size for a in like)]
    out_g, out_d, out_m, out_v = flip(out_g), flip(out_d), flip(out_m), flip(out_v)
    return (loss, grad_x, *[out_g[k] for k in ORDER], *[out_d[k] for k in ORDER],
            *[out_m[k] for k in ORDER], *[out_v[k] for k in ORDER])


def kernel(x, p, ffn1_norm, ffn1_w1, ffn1_w3, ffn1_w2, mix_norm, ffn2_norm, ffn2_w1, ffn2_w3, ffn2_w2, ple_norm, ple_proj, ple_gate, s5_w_in, s5_a_re, s5_a_im, s5_log_dt, s5_b_re, s5_b_im, s5_c_re, s5_c_im, s5_d, s5_w_glu, sb_w_qkv, sb_w_o, final_norm, loss_target, m_ffn1_norm, m_ffn1_w1, m_ffn1_w3, m_ffn1_w2, m_mix_norm, m_ffn2_norm, m_ffn2_w1, m_ffn2_w3, m_ffn2_w2, m_ple_norm, m_ple_proj, m_ple_gate, m_s5_w_in, m_s5_a_re, m_s5_a_im, m_s5_log_dt, m_s5_b_re, m_s5_b_im, m_s5_c_re, m_s5_c_im, m_s5_d, m_s5_w_glu, m_sb_w_qkv, m_sb_w_o, m_final_norm, v_ffn1_norm, v_ffn1_w1, v_ffn1_w3, v_ffn1_w2, v_mix_norm, v_ffn2_norm, v_ffn2_w1, v_ffn2_w3, v_ffn2_w2, v_ple_norm, v_ple_proj, v_ple_gate, v_s5_w_in, v_s5_a_re, v_s5_a_im, v_s5_log_dt, v_s5_b_re, v_s5_b_im, v_s5_c_re, v_s5_c_im, v_s5_d, v_s5_w_glu, v_sb_w_qkv, v_sb_w_o, v_final_norm):
    args = dict(locals())
    w = {k: args[k] for k in ORDER}
    m = {k: args["m_" + k] for k in ORDER}
    v = {k: args["v_" + k] for k in ORDER}
    return _step(x, p, loss_target, w, m, v)
```

```python
import functools
import math

import jax
import jax.numpy as jnp
from jax import lax
from jax.experimental import pallas as pl
from jax.experimental.pallas import tpu as pltpu

F32 = jnp.float32
BF16 = jnp.bfloat16
MESH = pl.DeviceIdType.MESH

N_CHIPS = 4
N_DEV = 8
RMS_EPS = 1e-6
S5_GROUP = 16
S5_STATE = 64
S5_CHUNK = 16
SB_HEAD_DIM = 64
SB_BLOCK = 128
SB_CUT = -104.0
SB_UNROLL = 3
ADAM_LR, ADAM_B1, ADAM_B2, ADAM_EPS, ADAM_WD, ADAM_STEP = 0.001, 0.9, 0.999, 1e-08, 0.01, 10
VMEM_LIMIT = 48 * 1024 * 1024

NN = (((1,), (0,)), ((), ()))
NT = (((1,), (1,)), ((), ()))
TN = (((0,), (0,)), ((), ()))

ANY = pl.BlockSpec(memory_space=pl.ANY)


def _tile(n, target):
    if n <= target:
        return n
    for t in range(target - target % 8, 7, -8):
        if n % t == 0:
            return t
    raise ValueError(f"no row tile for {n}")


def _params(*semantics):
    return pltpu.CompilerParams(dimension_semantics=semantics, vmem_limit_bytes=VMEM_LIMIT)


def _sigmoid(v):
    return 1.0 / (1.0 + jnp.exp(-v))


def _gemm(name, grid, operands, in_specs, groups, acc_shapes, out_shapes, out_specs, epilogue, reduce_axis=None, aliases=None):
    n_in, n_out = len(operands), len(out_shapes)
    n_red = None if reduce_axis is None else grid[reduce_axis]

    def body(*refs):
        ins, outs, accs = refs[:n_in], refs[n_in:n_in + n_out], refs[n_in + n_out:]

        def products():
            res = []
            for terms in groups:
                tot = None
                for ia, ib, dims in terms:
                    d = lax.dot_general(ins[ia][...], ins[ib][...], dims, preferred_element_type=F32)
                    tot = d if tot is None else tot + d
                res.append(tot)
            return res

        def finish(vals):
            for o, v in zip(outs, epilogue(vals, ins)):
                o[...] = v.astype(o.dtype)

        if reduce_axis is None:
            finish(products())
        else:
            k = pl.program_id(reduce_axis)

            @pl.when(k == 0)
            def _():
                for a in accs:
                    a[...] = jnp.zeros_like(a)

            for a, d in zip(accs, products()):
                a[...] += d

            @pl.when(k == n_red - 1)
            def _():
                finish([a[...] for a in accs])

    scratch = [] if reduce_axis is None else [pltpu.VMEM(s, F32) for s in acc_shapes]
    sem = tuple("arbitrary" if i == reduce_axis else "parallel" for i in range(len(grid)))
    return pl.pallas_call(
        body, name=name, grid=grid, in_specs=in_specs, out_specs=out_specs, out_shape=out_shapes,
        scratch_shapes=scratch, input_output_aliases=aliases or {}, compiler_params=_params(*sem))(*operands)


def _ident(vals, ins):
    return vals


def _act_spec(layout, tm, cs, pos):
    if layout == "sm":
        return pl.BlockSpec((None, tm, cs), lambda *g: (pos(*g)[1], pos(*g)[0], 0))
    return pl.BlockSpec((tm, cs), lambda *g: pos(*g))


def _act_shape(layout, t, cs, dtype):
    return jax.ShapeDtypeStruct((N_CHIPS, t, cs) if layout == "sm" else (t, N_CHIPS * cs), dtype)


def _w_spec(w, layer, pos_k):
    _, _, r, c = w.shape
    return pl.BlockSpec((None, None, r, c), lambda *g: (pos_k(*g), layer, 0, 0))


def _mm_cs(name, x, w, layer, out_layout, out_dtype, tm=512):
    t, kd = x.shape
    cs = w.shape[3]
    tm = _tile(t, tm)
    return _gemm(
        name, (N_CHIPS, t // tm), [x, w],
        [pl.BlockSpec((tm, kd), lambda k, i: (i, 0)), _w_spec(w, layer, lambda k, i: k)],
        [[(0, 1, NN)]], None, [_act_shape(out_layout, t, cs, out_dtype)],
        [_act_spec(out_layout, tm, cs, lambda k, i: (i, k))], _ident)[0]


def _mm_rs(name, xs, layout, w, layer, res=None, alpha=1.0, out_dtype=F32, tm=1024):
    ks, n = w.shape[2], w.shape[3]
    t = xs.shape[1] if layout == "sm" else xs.shape[0]
    tm = _tile(t, tm)
    operands = [xs, w] + ([] if res is None else [res])
    specs = [_act_spec(layout, tm, ks, lambda i, k: (i, k)), _w_spec(w, layer, lambda i, k: k)]
    if res is not None:
        specs.append(pl.BlockSpec((tm, n), lambda i, k: (i, 0)))

    def epilogue(vals, ins):
        y = alpha * vals[0]
        return [y if res is None else ins[2][...] + y]

    return _gemm(
        name, (t // tm, N_CHIPS), operands, specs, [[(0, 1, NN)]], [(tm, n)],
        [jax.ShapeDtypeStruct((t, n), out_dtype)], [pl.BlockSpec((tm, n), lambda i, k: (i, 0))],
        epilogue, reduce_axis=1)[0]


def _mm_cs_dx(name, pairs, layout, layer, tm=1024, transposed=False):
    w0 = pairs[0][1]
    kd, cs = (w0.shape[3], w0.shape[2]) if transposed else (w0.shape[2], w0.shape[3])
    dy0 = pairs[0][0]
    t = dy0.shape[1] if layout == "sm" else dy0.shape[0]
    tm = _tile(t, tm)
    operands, specs, terms = [], [], []
    for dy, w in pairs:
        terms.append((len(operands), len(operands) + 1, NN if transposed else NT))
        operands += [dy, w]
        specs += [_act_spec(layout, tm, cs, lambda i, k: (i, k)), _w_spec(w, layer, lambda i, k: k)]
    return _gemm(
        name, (t // tm, N_CHIPS), operands, specs, [terms], [(tm, kd)],
        [jax.ShapeDtypeStruct((t, kd), F32)], [pl.BlockSpec((tm, kd), lambda i, k: (i, 0))],
        _ident, reduce_axis=1)[0]


def _mm_rs_dx(name, dy, w, layer, out_layout, out_dtype, tm=512):
    t, n = dy.shape
    ks = w.shape[2]
    tm = _tile(t, tm)
    return _gemm(
        name, (N_CHIPS, t // tm), [dy, w],
        [pl.BlockSpec((tm, n), lambda k, i: (i, 0)), _w_spec(w, layer, lambda k, i: k)],
        [[(0, 1, NT)]], None, [_act_shape(out_layout, t, ks, out_dtype)],
        [_act_spec(out_layout, tm, ks, lambda k, i: (i, k))], _ident)[0]


def _mm_dw(name, x, x_layout, dy, dy_layout, slot, alpha=1.0, tk=2048):
    stack, layer, layers = slot
    if x_layout is None:
        t, rows = x.shape
        cols = dy.shape[2] if dy_layout == "sm" else dy.shape[1] // N_CHIPS
        tk = _tile(t, tk)
        xspec = pl.BlockSpec((tk, rows), lambda k, j: (j, 0))
        yspec = _act_spec(dy_layout, tk, cols, lambda k, j: (j, k))
    else:
        t, cols = dy.shape
        rows = x.shape[2] if x_layout == "sm" else x.shape[1] // N_CHIPS
        tk = _tile(t, tk)
        xspec = _act_spec(x_layout, tk, rows, lambda k, j: (j, k))
        yspec = pl.BlockSpec((tk, cols), lambda k, j: (j, 0))
    operands, specs = [x, dy], [xspec, yspec]
    if stack is not None:
        operands.append(stack)
        specs.append(ANY)
    return _gemm(
        name, (N_CHIPS, t // tk), operands, specs, [[(0, 1, TN)]], [(rows, cols)],
        [jax.ShapeDtypeStruct((N_CHIPS, layers, rows, cols), F32)],
        [pl.BlockSpec((None, None, rows, cols), lambda k, j: (k, layer, 0, 0))],
        lambda vals, ins: [alpha * vals[0]], reduce_axis=1, aliases=None if stack is None else {2: 0})[0]


def _rows(name, fn, ins, outs, accs=(), tm=256):
    t = ins[0].shape[0]
    tm = _tile(t, tm)
    n_in, n_out, n_acc = len(ins), len(outs), len(accs)
    in_specs = []
    for a in ins:
        if a.shape[0] == t:
            in_specs.append(pl.BlockSpec((tm, a.shape[1]), lambda i: (i, 0)))
        else:
            in_specs.append(pl.BlockSpec(a.shape, lambda i: (0, 0)))
    out_shape = [jax.ShapeDtypeStruct((t, c), d) for c, d in outs] + [jax.ShapeDtypeStruct(s, F32) for s in accs]
    out_specs = [pl.BlockSpec((tm, c), lambda i: (i, 0)) for c, _ in outs] + [pl.BlockSpec(s, lambda i: (0, 0)) for s in accs]

    def body(*refs):
        i = pl.program_id(0)
        row_vals, acc_vals = fn(*[r[...] for r in refs[:n_in]])
        for o, v in zip(refs[n_in:n_in + n_out], row_vals):
            o[...] = v.astype(o.dtype)
        acc_refs = refs[n_in + n_out:]
        if n_acc:
            @pl.when(i == 0)
            def _():
                for a in acc_refs:
                    a[...] = jnp.zeros_like(a)

            for a, v in zip(acc_refs, acc_vals):
                a[...] += v

    res = pl.pallas_call(
        body, name=name, grid=(t // tm,), in_specs=in_specs, out_specs=out_specs, out_shape=out_shape,
        compiler_params=_params("arbitrary" if n_acc else "parallel"))(*ins)
    return res[:n_out], res[n_out:]


def _to_bf16(a):
    def fn(x):
        return [x], []
    return _rows("weights_bf16", fn, [a.reshape(-1, a.shape[-1])], [(a.shape[-1], BF16)], tm=512)[0][0].reshape(a.shape)


def _rms_stats(x):
    return lax.rsqrt(jnp.mean(x * x, axis=-1, keepdims=True) + RMS_EPS)


def _rmsnorm(name, h, g):
    def fn(x, gg):
        return [x * _rms_stats(x) * gg], []
    return _rows(name, fn, [h, g], [(h.shape[1], BF16)])[0][0]


def _rms_bwd_math(dn, x, g):
    r = _rms_stats(x)
    xhat = x * r
    dxh = dn * g
    dx = r * (dxh - xhat * jnp.mean(dxh * xhat, axis=-1, keepdims=True))
    return dx, jnp.sum(dn * xhat, axis=0, keepdims=True)


def _rmsnorm_bwd(name, dres, dn, h, g):
    def fn(dr, d, x, gg):
        dx, dg = _rms_bwd_math(d, x, gg)
        return [dr + dx], [dg]
    (dh,), (dg,) = _rows(name, fn, [dres, dn, h, g], [(h.shape[1], F32)], [(1, h.shape[1])])
    return dh, dg


def _ffn_fwd(h, g, w1, w3, w2, layer, tm=512):
    t, d = h.shape
    fs = w1.shape[2]
    n = _rmsnorm("ffn_norm", h, g)
    tm = _tile(t, tm)

    def up(vals, ins):
        a, b = vals
        sg = _sigmoid(a)
        silu = a * sg
        return [b * sg * (1.0 + a * (1.0 - sg)), silu, silu * b]

    sm = _act_shape("sm", t, fs, BF16)
    osp = _act_spec("sm", tm, fs, lambda k, i: (i, k))
    ga, gb, s = _gemm(
        "ffn_up", (N_CHIPS, t // tm), [n, w1, w3],
        [pl.BlockSpec((tm, d), lambda k, i: (i, 0)), _w_spec(w1, layer, lambda k, i: k), _w_spec(w3, layer, lambda k, i: k)],
        [[(0, 1, NT)], [(0, 2, NT)]], None, [sm, sm, sm], [osp, osp, osp], up)
    out = _mm_rs("ffn_down", s, "sm", w2, layer, res=h, alpha=0.5)
    return out, (h, n, ga, gb, s)


def _ffn_bwd(dout, saved, g, w1, w3, w2, layer, slots, tm=512):
    h, n, ga, gb, s = saved
    t, d = h.shape
    fs = w1.shape[2]
    tm = _tile(t, tm)
    dob = dout.astype(BF16)

    def down(vals, ins):
        ds = 0.5 * vals[0]
        return [ds * ins[2][...].astype(F32), ds * ins[3][...].astype(F32)]

    sm = _act_shape("sm", t, fs, BF16)
    asp = _act_spec("sm", tm, fs, lambda k, i: (i, k))
    da, db = _gemm(
        "ffn_down_dx", (N_CHIPS, t // tm), [dob, w2, ga, gb],
        [pl.BlockSpec((tm, d), lambda k, i: (i, 0)), _w_spec(w2, layer, lambda k, i: k), asp, asp],
        [[(0, 1, NT)]], None, [sm, sm], [asp, asp], down)
    dw2 = _mm_dw("ffn_dw2", s, "sm", dob, None, slots[2], alpha=0.5)
    dw1 = _mm_dw("ffn_dw1", da, "sm", n, None, slots[0])
    dw3 = _mm_dw("ffn_dw3", db, "sm", n, None, slots[1])
    dn = _mm_cs_dx("ffn_up_dx", [(da, w1), (db, w3)], "sm", layer, transposed=True)
    dh, dg = _rmsnorm_bwd("ffn_norm_bwd", dout, dn, h, g)
    return dh, dg, dw1, dw3, dw2


def _ple_fwd(h, g, p2, wproj, wgate, layer):
    n = _rmsnorm("ple_norm", h, g)
    gl = _mm_rs("ple_gate", n, "flat", wgate, layer)
    pp = _mm_cs("ple_proj", p2, wproj, layer, "flat", F32)

    def fn(hh, gg, q):
        return [hh + q * _sigmoid(gg)], []
    out = _rows("ple_mix", fn, [h, gl, pp], [(h.shape[1], F32)])[0][0]
    return out, (h, n, gl, pp)


def _ple_bwd(dout, saved, g, p2, wproj, wgate, layer, slots):
    h, n, gl, pp = saved
    d = h.shape[1]

    def fn(do, gg, q):
        sg = _sigmoid(gg)
        return [do * sg, do * q * sg * (1.0 - sg)], []
    (dpp, dgl), _ = _rows("ple_mix_bwd", fn, [dout, gl, pp], [(d, BF16), (d, BF16)])
    dwproj = _mm_dw("ple_dwproj", p2, None, dpp, "flat", slots[0])
    dwgate = _mm_dw("ple_dwgate", n, "flat", dgl, None, slots[1])
    dn = _mm_rs_dx("ple_gate_dx", dgl, wgate, layer, "flat", F32)
    dh, dg = _rmsnorm_bwd("ple_norm_bwd", dout, dn, h, g)
    return dh, dg, dwproj, dwgate


def _head(h, g, target):
    d = h.shape[1]

    def fn(x, gg, tg):
        y = x * _rms_stats(x) * gg
        err = y - tg
        dy = err * (1.0 / d)
        dx, dg = _rms_bwd_math(dy, x, gg)
        loss = 0.5 * jnp.sum(jnp.sum(err * err, axis=-1, keepdims=True) * (1.0 / d), axis=0, keepdims=True)
        return [dx], [dg, jnp.broadcast_to(loss, (1, 128))]
    (dh,), (dg, loss) = _rows("loss_head", fn, [h, g, target], [(d, F32)], [(1, d), (1, 128)])
    return loss[0, 0], dh, dg


S5_LANES = 2 * S5_STATE
S5_GB = 128 // S5_GROUP


def _s5_prep(a_re, a_im, log_dt, b_re, b_im, c_re, c_im):
    c, gb = S5_CHUNK, S5_GB
    g = a_re.shape[0]
    nb = g // gb
    lam_re = jnp.minimum(a_re, -1e-4)
    lam_im = a_im
    dt = jnp.exp(log_dt)[:, None, None]
    ks = jnp.arange(c + 1, dtype=F32)
    mag = jnp.exp(lam_re[..., None] * dt * ks)
    ph = lam_im[..., None] * dt * ks
    pw_re, pw_im = mag * jnp.cos(ph), mag * jnp.sin(ph)
    den = lam_re * lam_re + lam_im * lam_im
    nr, ni = pw_re[..., 1] - 1.0, pw_im[..., 1]
    fr = (nr * lam_re + ni * lam_im) / den
    fi = (ni * lam_re - nr * lam_im) / den
    bb_re = fr[..., None] * b_re - fi[..., None] * b_im
    bb_im = fr[..., None] * b_im + fi[..., None] * b_re
    ct_re, ct_im = c_re.transpose(0, 2, 1), c_im.transpose(0, 2, 1)
    ca_re = ct_re[:, :, None, :] * pw_re[..., None] - ct_im[:, :, None, :] * pw_im[..., None]
    ca_im = ct_re[:, :, None, :] * pw_im[..., None] + ct_im[:, :, None, :] * pw_re[..., None]
    hp = lax.Precision.HIGHEST
    kern = (jnp.einsum("gpj,gpkh->gkjh", bb_re, ca_re[:, :, :c], precision=hp)
            - jnp.einsum("gpj,gpkh->gkjh", bb_im, ca_im[:, :, :c], precision=hp))
    rev_re = pw_re[:, :, :c][:, :, ::-1].transpose(0, 2, 1)
    rev_im = pw_im[:, :, :c][:, :, ::-1].transpose(0, 2, 1)
    bt_re, bt_im = bb_re.transpose(0, 2, 1), bb_im.transpose(0, 2, 1)
    wn_re = rev_re[:, :, None, :] * bt_re[:, None] - rev_im[:, :, None, :] * bt_im[:, None]
    wn_im = rev_re[:, :, None, :] * bt_im[:, None] + rev_im[:, :, None, :] * bt_re[:, None]
    wn = jnp.concatenate([wn_re, wn_im], axis=-1)
    wo = jnp.concatenate([ca_re[:, :, 1:].transpose(0, 2, 3, 1), -ca_im[:, :, 1:].transpose(0, 2, 3, 1)], axis=-1)

    def blocks(x):
        return x.reshape(nb, gb, c, S5_GROUP, x.shape[3]).transpose(0, 2, 1, 3, 4).reshape(nb, c, gb * S5_GROUP, x.shape[3])

    ar, ai = pw_re[..., c], pw_im[..., c]
    return (jnp.tile(blocks(kern), (1, 1, 1, gb)), blocks(wn), blocks(wo),
            jnp.concatenate([ar, ar], axis=1), jnp.concatenate([-ai, ai], axis=1))


def _step_rows(ref, tau, n):
    return ref[pl.ds(tau, n, stride=S5_CHUNK), :].astype(BF16)


def _cat_groups(ref, dtype):
    return jnp.concatenate([ref[:, j, :] for j in range(S5_GB)], axis=1).astype(dtype)


def _cat_steps(ref, n):
    return jnp.concatenate([_step_rows(ref, tau, n) for tau in range(S5_CHUNK)], axis=1)


def _stack_steps(ref, n):
    return jnp.concatenate([_step_rows(ref, tau, n) for tau in range(S5_CHUNK)], axis=0)


def _cat_ops(ref, axis, reverse=False):
    order = range(S5_CHUNK - 1, -1, -1) if reverse else range(S5_CHUNK)
    return jnp.concatenate([ref[k] for k in order], axis=axis)


def _row_group(rows, lanes):
    row = (lax.broadcasted_iota(jnp.int32, (rows, lanes), 0) // S5_GROUP) % S5_GB
    lane = (lax.broadcasted_iota(jnp.int32, (rows, lanes), 1) // S5_GROUP) % S5_GB
    return row, lane


def _own_group(x):
    row, lane = _row_group(*x.shape)
    return jnp.where(row == lane, x, jnp.zeros_like(x))


def _spread(x):
    row, _ = _row_group(*x.shape)
    return jnp.concatenate([jnp.where(row == j, x, jnp.zeros_like(x)) for j in range(S5_GB)], axis=1)


def _gather_own(x):
    row, _ = _row_group(x.shape[0], S5_LANES)
    out = jnp.zeros((x.shape[0], S5_LANES), x.dtype)
    for j in range(S5_GB):
        out = out + jnp.where(row == j, x[:, j * S5_LANES:(j + 1) * S5_LANES], 0.0)
    return out


def _s5_specs(t, d):
    nct, g = t // S5_CHUNK, d // S5_GROUP
    tok = pl.BlockSpec((t, 128), lambda i: (0, i))
    st = pl.BlockSpec((nct, S5_GB, S5_LANES), lambda i: (0, i, 0))
    op = lambda w: pl.BlockSpec((None,) + w.shape[1:], lambda i: (i, 0, 0, 0))
    return nct, g, tok, st, op


def _s5_chunk_fwd(u, bd, bn):
    t, d = u.shape
    nct, g, tok, st, op = _s5_specs(t, d)
    c = S5_CHUNK

    def body(u_ref, bd_ref, bn_ref, y_ref, s_ref):
        ucat = _cat_steps(u_ref, nct)
        sloc = jnp.dot(ucat, _spread(_cat_ops(bn_ref, 0)), preferred_element_type=F32)
        for j in range(S5_GB):
            s_ref[:, j, :] = sloc[:, j * S5_LANES:(j + 1) * S5_LANES]
        lags = _own_group(_cat_ops(bd_ref, 0, reverse=True))
        for tt in range(c):
            y_ref[pl.ds(tt, nct, stride=c), :] = jnp.dot(ucat[:, :(tt + 1) * 128], lags[(c - 1 - tt) * 128:, :],
                                                         preferred_element_type=F32)

    return pl.pallas_call(
        body, name="s5_chunk", grid=(d // 128,), in_specs=[tok, op(bd), op(bn)], out_specs=[tok, st],
        out_shape=[jax.ShapeDtypeStruct((t, d), F32), jax.ShapeDtypeStruct((nct, g, S5_LANES), F32)],
        compiler_params=_params("parallel"))(u, bd, bn)


def _s5_state_out(sprev, co, yin):
    t, d = yin.shape
    nct, g, tok, st, op = _s5_specs(t, d)
    c = S5_CHUNK

    def body(s_ref, co_ref, yi_ref, y_ref):
        ys = lax.dot_general(_cat_groups(s_ref, BF16), _spread(_cat_ops(co_ref, 0)), NT,
                             preferred_element_type=F32)
        for tt in range(c):
            rows = pl.ds(tt, nct, stride=c)
            y_ref[rows, :] = yi_ref[rows, :] + ys[:, tt * 128:(tt + 1) * 128]

    return pl.pallas_call(
        body, name="s5_state_out", grid=(d // 128,), in_specs=[st, op(co), tok], out_specs=tok,
        out_shape=jax.ShapeDtypeStruct((t, d), F32), compiler_params=_params("parallel"))(sprev, co, yin)


def _s5_state_out_dx(dyb, co):
    t, d = dyb.shape
    nct, g, tok, st, op = _s5_specs(t, d)
    c = S5_CHUNK

    def body(dy_ref, co_ref, ds_ref):
        acc = jnp.dot(_cat_steps(dy_ref, nct), _spread(_cat_ops(co_ref, 0)), preferred_element_type=F32)
        for j in range(S5_GB):
            ds_ref[:, j, :] = acc[:, j * S5_LANES:(j + 1) * S5_LANES]

    return pl.pallas_call(
        body, name="s5_state_out_dx", grid=(d // 128,), in_specs=[tok, op(co)], out_specs=st,
        out_shape=jax.ShapeDtypeStruct((nct, g, S5_LANES), F32), compiler_params=_params("parallel"))(dyb, co)


def _s5_chunk_dx(dyb, dsloc, bd, bn, skip):
    t, d = dyb.shape
    nct, g, tok, st, op = _s5_specs(t, d)
    c = S5_CHUNK

    def body(dy_ref, ds_ref, bd_ref, bn_ref, sk_ref, du_ref):
        dus = lax.dot_general(_cat_groups(ds_ref, BF16), _spread(_cat_ops(bn_ref, 0)), NT, preferred_element_type=F32)
        dycat = _cat_steps(dy_ref, nct)
        lags = _own_group(_cat_ops(bd_ref, 1))
        for tau in range(c):
            rows = pl.ds(tau, nct, stride=c)
            du_ref[rows, :] = (sk_ref[rows, :] + dus[:, tau * 128:(tau + 1) * 128]
                               + lax.dot_general(dycat[:, tau * 128:], lags[:, :(c - tau) * 128], NT,
                                                 preferred_element_type=F32))

    return pl.pallas_call(
        body, name="s5_chunk_dx", grid=(d // 128,), in_specs=[tok, st, op(bd), op(bn), tok], out_specs=tok,
        out_shape=jax.ShapeDtypeStruct((t, d), F32), compiler_params=_params("parallel"))(dyb, dsloc, bd, bn, skip)


def _s5_chunk_dw(u, dyb, dsloc, bd, bn):
    t, d = u.shape
    nct, g, tok, st, op = _s5_specs(t, d)
    c = S5_CHUNK

    def body(u_ref, dy_ref, ds_ref, dbd_ref, dbn_ref):
        dbn = _gather_own(lax.dot_general(_cat_steps(u_ref, nct), _cat_groups(ds_ref, BF16), TN,
                                          preferred_element_type=F32))
        for tau in range(c):
            dbn_ref[tau] = dbn[tau * 128:(tau + 1) * 128, :]
        ustk, dystk = _stack_steps(u_ref, nct), _stack_steps(dy_ref, nct)
        for k in range(c):
            dbd_ref[k] = _own_group(lax.dot_general(ustk[:(c - k) * nct], dystk[k * nct:], TN,
                                                    preferred_element_type=F32))

    return pl.pallas_call(
        body, name="s5_chunk_dw", grid=(d // 128,), in_specs=[tok, tok, st], out_specs=[op(bd), op(bn)],
        out_shape=[jax.ShapeDtypeStruct(bd.shape, F32), jax.ShapeDtypeStruct(bn.shape, F32)],
        compiler_params=_params("parallel"))(u, dyb, dsloc)


def _s5_state_out_dw(sprev, dyb, co):
    t, d = dyb.shape
    nct, g, tok, st, op = _s5_specs(t, d)
    c = S5_CHUNK

    def body(s_ref, dy_ref, dco_ref):
        dco = _gather_own(lax.dot_general(_cat_steps(dy_ref, nct), _cat_groups(s_ref, BF16), TN,
                                          preferred_element_type=F32))
        for tt in range(c):
            dco_ref[tt] = dco[tt * 128:(tt + 1) * 128, :]

    return pl.pallas_call(
        body, name="s5_state_out_dw", grid=(d // 128,), in_specs=[st, tok], out_specs=op(co),
        out_shape=jax.ShapeDtypeStruct(co.shape, F32), compiler_params=_params("parallel"))(sprev, dyb)


def _s5_scan_fwd(sloc, m1, m2):
    bl, nc, g, w = sloc.shape

    def body(s_ref, m1_ref, m2_ref, o_ref):
        a1, a2 = m1_ref[...], m2_ref[...]

        def step(c, states):
            new = []
            for b, s in enumerate(states):
                o_ref[b, c] = s
                new.append(a1 * s + a2 * pltpu.roll(s, S5_STATE, 1) + s_ref[b, c])
            return tuple(new)
        lax.fori_loop(0, nc, step, tuple(jnp.zeros((g, w), F32) for _ in range(bl)))

    vm = pl.BlockSpec(memory_space=pltpu.VMEM)
    return pl.pallas_call(
        body, name="s5_scan", in_specs=[vm, vm, vm], out_specs=vm,
        out_shape=jax.ShapeDtypeStruct(sloc.shape, F32),
        compiler_params=pltpu.CompilerParams(vmem_limit_bytes=VMEM_LIMIT))(sloc, m1, m2)


def _s5_scan_bwd(dsprev, sprev, m1, m2):
    bl, nc, g, w = dsprev.shape

    def body(d_ref, s_ref, m1_ref, m2_ref, g_ref, p1_ref, p2_ref):
        a1, a2 = m1_ref[...], m2_ref[...]
        zero = jnp.zeros((g, w), F32)

        def step(i, carry):
            gps, p1, p2 = carry
            c = nc - 2 - i
            new = []
            for b, gp in enumerate(gps):
                g_ref[b, c] = gp
                sp = s_ref[b, c]
                p1 = p1 + gp * sp
                p2 = p2 + gp * pltpu.roll(sp, S5_STATE, 1)
                new.append(d_ref[b, c] + a1 * gp - a2 * pltpu.roll(gp, S5_STATE, 1))
            return tuple(new), p1, p2

        for b in range(bl):
            g_ref[b, nc - 1] = zero
        _, p1, p2 = lax.fori_loop(0, nc - 1, step, (tuple(d_ref[b, nc - 1] for b in range(bl)), zero, zero))
        p1_ref[...] = p1
        p2_ref[...] = p2

    vm = pl.BlockSpec(memory_space=pltpu.VMEM)
    sd = jax.ShapeDtypeStruct
    return pl.pallas_call(
        body, name="s5_scan_bwd", in_specs=[vm, vm, vm, vm], out_specs=[vm, vm, vm],
        out_shape=[sd(dsprev.shape, F32), sd((g, w), F32), sd((g, w), F32)],
        compiler_params=pltpu.CompilerParams(vmem_limit_bytes=VMEM_LIMIT))(dsprev, sprev, m1, m2)


def _gelu_tanh_parts(y):
    c0 = math.sqrt(2.0 / math.pi)
    inner = c0 * (y + 0.044715 * y * y * y)
    th = jnp.tanh(inner)
    return th, c0 * (1.0 + 3 * 0.044715 * y * y)


def _s5_fwd(h, g, ops, d_skip, w_in, w_glu, bl):
    bd, bn, co, m1, m2 = ops
    t, d = h.shape
    nct, groups = t // S5_CHUNK, d // S5_GROUP
    hn = _rmsnorm("mix_norm", h, g)
    u = _mm_rs("s5_in", hn, "flat", w_in, 0)
    yin, sloc = _s5_chunk_fwd(u, bd.astype(BF16), bn.astype(BF16))
    sprev = _s5_scan_fwd(sloc.reshape(bl, nct // bl, groups, S5_LANES), m1, m2).reshape(nct, groups, S5_LANES)
    y = _s5_state_out(sprev, co.astype(BF16), yin)

    def fn(yy, uu, dd):
        y2 = yy + dd * uu
        th, _ = _gelu_tanh_parts(y2)
        return [0.5 * y2 * (1.0 + th)], []
    z = _rows("s5_gelu", fn, [y, u, d_skip], [(d, BF16)])[0][0]
    zz = _mm_cs("s5_glu", z, w_glu, 0, "flat", F32)

    def glu(hh, zv):
        return [hh + zv[:, :d] * _sigmoid(zv[:, d:])], []
    out = _rows("s5_glu_mix", glu, [h, zz], [(d, F32)])[0][0]
    return out, (h, hn, u, sprev, y, z, zz)


def _s5_bwd(dout, saved, g, ops, d_skip, w_in, w_glu, bl):
    h, hn, u, sprev, y, z, zz = saved
    bd, bn, co, m1, m2 = ops
    t, d = h.shape
    nct, groups = t // S5_CHUNK, d // S5_GROUP

    def glu_bwd(do, zv):
        sg = _sigmoid(zv[:, d:])
        return [jnp.concatenate([do * sg, do * zv[:, :d] * sg * (1.0 - sg)], axis=1)], []
    dzz = _rows("s5_glu_bwd", glu_bwd, [dout, zz], [(2 * d, BF16)])[0][0]
    dwglu = _mm_dw("s5_dwglu", z, None, dzz, "flat", (None, 0, 1))
    dz = _mm_cs_dx("s5_glu_dx", [(dzz, w_glu)], "flat", 0)

    def gelu_bwd(dzv, yy, uu, dd):
        y2 = yy + dd * uu
        th, dinner = _gelu_tanh_parts(y2)
        dy2 = dzv * (0.5 * (1.0 + th) + 0.5 * y2 * (1.0 - th * th) * dinner)
        return [dy2, dy2 * dd], [jnp.sum(dy2 * uu, axis=0, keepdims=True)]
    (dyb, du_skip), (dd,) = _rows("s5_gelu_bwd", gelu_bwd, [dz, y, u, d_skip], [(d, F32), (d, F32)], [(1, d)])
    bd_b, bn_b, co_b = bd.astype(BF16), bn.astype(BF16), co.astype(BF16)
    dsprev = _s5_state_out_dx(dyb, co_b)
    shape4 = (bl, nct // bl, groups, S5_LANES)
    dsloc, dm1, dm2 = _s5_scan_bwd(dsprev.reshape(shape4), sprev.reshape(shape4), m1, m2)
    dsloc = dsloc.reshape(nct, groups, S5_LANES)
    du = _s5_chunk_dx(dyb, dsloc, bd_b, bn_b, du_skip).astype(BF16)
    dbd, dbn = _s5_chunk_dw(u, dyb, dsloc, bd, bn)
    dco = _s5_state_out_dw(sprev, dyb, co)
    dwin = _mm_dw("s5_dwin", hn, "flat", du, None, (None, 0, 1))
    dhn = _mm_rs_dx("s5_in_dx", du, w_in, 0, "flat", F32)
    dh, dg = _rmsnorm_bwd("mix_norm_bwd", dout, dhn, h, g)
    return dh, dg, dwin, dwglu, dd, (dbd, dbn, dco, dm1, dm2)


def _sb_block(qi, idx, tb):
    kb = qi - idx
    return pl.multiple_of(jnp.maximum(kb, 0) * tb, tb), idx == 0, kb >= 0


def _sb_scores(q, kblk, diag, exists, row, col):
    z = lax.dot_general(q, kblk, NT, preferred_element_type=F32) * (SB_HEAD_DIM ** -0.5)
    l1 = jnp.log(1.0 + jnp.exp(-jnp.abs(z)))
    ls = jnp.minimum(z, 0.0) - l1
    mask = jnp.logical_and(jnp.logical_or(col < row, jnp.logical_not(diag)), exists)
    lk = jnp.where(mask, ls - z, 0.0)
    return ls, lk, mask


def _split_dot(v, tri):
    hi = v.astype(BF16)
    lo = (v - hi.astype(F32)).astype(BF16)
    return (jnp.dot(hi, tri, preferred_element_type=F32) + jnp.dot(lo, tri, preferred_element_type=F32))


SB_PAIR =2 * SB_HEAD_DIM


def _pair_masks(tb):
    lane = lax.broadcasted_iota(jnp.int32, (1, SB_PAIR), 1)
    row = lax.broadcasted_iota(jnp.int32, (tb, tb), 0)
    col = lax.broadcasted_iota(jnp.int32, (tb, tb), 1)
    return [lane < SB_HEAD_DIM, lane >= SB_HEAD_DIM], row, col


def _pair_more(qi, carry):
    j, crs = carry[0], carry[2]
    return jnp.logical_and(j <= qi, jnp.maximum(jnp.max(crs[0]), jnp.max(crs[1])) > SB_CUT)


def _pair_specs(bl, l, d, tb):
    nq, off = l // tb, d // SB_PAIR
    qspec = pl.BlockSpec((tb, SB_PAIR), lambda b, p, i: (b * nq + i, p))
    kspec = pl.BlockSpec((l, SB_PAIR), lambda b, p, i: (b, off + p))
    vspec = pl.BlockSpec((l, SB_PAIR), lambda b, p, i: (b, 2 * off + p))
    return qspec, kspec, vspec


def _sb_attn_fwd2(qkv, bl):
    t, d3 = qkv.shape
    d, l = d3 // 3, t // bl
    tb = min(SB_BLOCK, l)
    nq = l // tb

    def body(q_ref, k_ref, v_ref, o_ref, ob_ref):
        qi = pl.program_id(2)
        heads, row, col = _pair_masks(tb)
        qv = q_ref[...]
        qh = [jnp.where(m, qv, jnp.zeros_like(qv)) for m in heads]
        tri = (row > col).astype(BF16)

        def step(carry):
            j, acc, crs = carry
            crs = list(crs)
            where = [_sb_block(qi, j + u, tb) for u in range(SB_UNROLL)]
            kblks = [k_ref[pl.ds(ks, tb), :] for ks, _, _ in where]
            scores = [[_sb_scores(qh[hd], kblks[u], where[u][1], where[u][2], row, col) for hd in range(2)]
                      for u in range(SB_UNROLL)]
            laters = [[_split_dot(sc[1], tri) for sc in su] for su in scores]
            for u in range(SB_UNROLL):
                vblk = v_ref[pl.ds(where[u][0], tb), :]
                outs = []
                for hd in range(2):
                    ls, lk, mask = scores[u][hd]
                    att = jnp.where(mask, jnp.exp(ls + laters[u][hd] + crs[hd]), 0.0)
                    outs.append(jnp.dot(att.astype(BF16), vblk, preferred_element_type=F32))
                    crs[hd] = crs[hd] + jnp.sum(lk, axis=1, keepdims=True)
                acc = acc + jnp.where(heads[0], outs[0], outs[1])
            return j + SB_UNROLL, acc, tuple(crs)

        zc = jnp.zeros((tb, 1), F32)
        _, acc, _ = lax.while_loop(functools.partial(_pair_more, qi), step,
                                   (jnp.int32(0), jnp.zeros((tb, SB_PAIR), F32), (zc, zc)))
        o_ref[...] = acc
        ob_ref[...] = acc.astype(BF16)

    qspec, kspec, vspec = _pair_specs(bl, l, d, tb)
    return pl.pallas_call(
        body, name="sb_attn", grid=(bl, d // SB_PAIR, nq), in_specs=[qspec, kspec, vspec], out_specs=[qspec, qspec],
        out_shape=[jax.ShapeDtypeStruct((t, d), F32), jax.ShapeDtypeStruct((t, d), BF16)],
        compiler_params=_params("parallel", "parallel", "parallel"))(qkv, qkv, qkv)


def _sb_attn_bwd2(qkv, o, do, bl):
    t, d3 = qkv.shape
    d, l = d3 // 3, t // bl
    tb = min(SB_BLOCK, l)
    nq = l // tb
    scale = SB_HEAD_DIM ** -0.5

    def body(q_ref, k_ref, v_ref, o_ref, do_ref, dq_ref, dk_ref, dv_ref, dk_acc, dv_acc):
        qi = pl.program_id(2)

        @pl.when(qi == 0)
        def _():
            dk_acc[...] = jnp.zeros_like(dk_acc)
            dv_acc[...] = jnp.zeros_like(dv_acc)

        heads, row, col = _pair_masks(tb)
        qv = q_ref[...]
        dov = do_ref[...].astype(BF16)
        qh = [jnp.where(m, qv, jnp.zeros_like(qv)) for m in heads]
        doh = [jnp.where(m, dov, jnp.zeros_like(dov)) for m in heads]
        ov = o_ref[...]
        dsum = [jnp.sum(dh.astype(F32) * ov, axis=1, keepdims=True) for dh in doh]
        tri = (row > col).astype(BF16)
        tri_inc = (row >= col).astype(BF16)

        def step(carry):
            j, dq, crs, ces = carry
            crs, ces = list(crs), list(ces)
            n = range(SB_UNROLL)
            where = [_sb_block(qi, j + u, tb) for u in n]
            rows = [pl.ds(ks, tb) for ks, _, _ in where]
            kblks = [k_ref[rows[u], :] for u in n]
            vblks = [v_ref[rows[u], :] for u in n]
            scores = [[_sb_scores(qh[hd], kblks[u], where[u][1], where[u][2], row, col) for hd in range(2)] for u in n]
            laters = [[_split_dot(sc[1], tri) for sc in su] for su in scores]
            datts = [[lax.dot_general(doh[hd], vblks[u], NT, preferred_element_type=F32) for hd in range(2)] for u in n]
            atts = [[None, None] for _ in n]
            for u in n:
                for hd in range(2):
                    ls, lk, mask = scores[u][hd]
                    atts[u][hd] = jnp.where(mask, jnp.exp(ls + laters[u][hd] + crs[hd]), 0.0).astype(BF16)
                    crs[hd] = crs[hd] + jnp.sum(lk, axis=1, keepdims=True)
            es = [[atts[u][hd].astype(F32) * datts[u][hd] for hd in range(2)] for u in n]
            sufs = [[_split_dot(e, tri_inc) for e in eu] for eu in es]
            dzs = [[None, None] for _ in n]
            for u in n:
                for hd in range(2):
                    ls, _, mask = scores[u][hd]
                    pre = dsum[hd] - ces[hd] - sufs[u][hd]
                    sg = jnp.exp(ls)
                    dzs[u][hd] = (jnp.where(mask, es[u][hd] * (1.0 - sg) - pre * sg, 0.0) * scale).astype(BF16)
                    ces[hd] = ces[hd] + jnp.sum(es[u][hd], axis=1, keepdims=True)
            for u in n:
                dq = dq + jnp.where(heads[0], jnp.dot(dzs[u][0], kblks[u], preferred_element_type=F32),
                                    jnp.dot(dzs[u][1], kblks[u], preferred_element_type=F32))
                dk_acc[rows[u], :] += (lax.dot_general(dzs[u][0], qh[0], TN, preferred_element_type=F32)
                                       + lax.dot_general(dzs[u][1], qh[1], TN, preferred_element_type=F32))
                dv_acc[rows[u], :] += (lax.dot_general(atts[u][0], doh[0], TN, preferred_element_type=F32)
                                       + lax.dot_general(atts[u][1], doh[1], TN, preferred_element_type=F32))
            return j + SB_UNROLL, dq, tuple(crs), tuple(ces)

        zc = jnp.zeros((tb, 1), F32)
        _, dq, _, _ = lax.while_loop(functools.partial(_pair_more, qi), step,
                                     (jnp.int32(0), jnp.zeros((tb, SB_PAIR), F32), (zc, zc), (zc, zc)))
        dq_ref[...] = dq.astype(BF16)

        @pl.when(qi == nq - 1)
        def _():
            dk_ref[...] = dk_acc[...].astype(BF16)
            dv_ref[...] = dv_acc[...].astype(BF16)

    qspec, kspec, vspec = _pair_specs(bl, l, d, tb)
    blk = pl.BlockSpec((tb, SB_PAIR), lambda b, p, i: (b * nq + i, p))
    full = pl.BlockSpec((l, SB_PAIR), lambda b, p, i: (b, p))
    sd = jax.ShapeDtypeStruct((t, d), BF16)
    dq, dk, dv = pl.pallas_call(
        body, name="sb_attn_bwd", grid=(bl, d // SB_PAIR, nq), in_specs=[qspec, kspec, vspec, blk, blk],
        out_specs=[blk, full, full], out_shape=[sd, sd, sd],
        scratch_shapes=[pltpu.VMEM((l, SB_PAIR), F32), pltpu.VMEM((l, SB_PAIR), F32)],
        compiler_params=_params("parallel", "parallel", "arbitrary"))(qkv, qkv, qkv, o, do)
    return jnp.concatenate([dq, dk, dv], axis=1)


def _sb_fwd(h, g, w_qkv, w_o, bl):
    t, d = h.shape
    hn = _rmsnorm("mix_norm", h, g)
    qkv = _mm_cs("sb_qkv", hn, w_qkv, 0, "flat", BF16)
    o, ob = _sb_attn_fwd2(qkv, bl)
    out = _mm_rs("sb_out", ob, "flat", w_o, 0, res=h)
    return out, (h, hn, qkv, o, ob)


def _sb_bwd(dout, saved, g, w_qkv, w_o, bl):
    h, hn, qkv, o, ob = saved
    dob = dout.astype(BF16)
    dwo = _mm_dw("sb_dwo", ob, "flat", dob, None, (None, 0, 1))
    do = _mm_rs_dx("sb_out_dx", dob, w_o, 0, "flat", F32)
    dqkv = _sb_attn_bwd2(qkv, o, do, bl)
    dwqkv = _mm_dw("sb_dwqkv", hn, None, dqkv, "flat", (None, 0, 1))
    dhn = _mm_cs_dx("sb_qkv_dx", [(dqkv, w_qkv)], "flat", 0)
    dh, dg = _rmsnorm_bwd("mix_norm_bwd", dout, dhn, h, g)
    return dh, dg, dwqkv, dwo


def _adamw_update(wv, gr, mv, vv):
    c1 = 1.0 / (1.0 - ADAM_B1 ** ADAM_STEP)
    c2 = 1.0 / (1.0 - ADAM_B2 ** ADAM_STEP)
    mn = ADAM_B1 * mv + (1.0 - ADAM_B1) * gr
    vn = ADAM_B2 * vv + (1.0 - ADAM_B2) * gr * gr
    delta = -ADAM_LR * ((mn * c1) / (jnp.sqrt(vn * c2) + ADAM_EPS) + ADAM_WD * wv)
    return delta, mn, vn


def _adamw_small(w, gr, m, v):
    def fn(wv, gv, mv, vv):
        return list(_adamw_update(wv, gv, mv, vv)), []
    return _rows("adamw_small", fn, [w, gr, m, v], [(w.shape[1], F32)] * 3)[0]


def _place():
    x, y, c = lax.axis_index("x"), lax.axis_index("y"), lax.axis_index("c")
    chips = [(1 - x, y), (x, 1 - y), (1 - x, 1 - y)]
    return x, y, c, chips


def _remote(src, dst, send_sem, recv_sem, to):
    return pltpu.make_async_remote_copy(src_ref=src, dst_ref=dst, send_sem=send_sem, recv_sem=recv_sem,
                                        device_id=to, device_id_type=MESH)


def _half(ref, c, rh, lead):
    return ref.at[(slice(None),) * lead + (pl.ds(c * rh, rh),)]


def _allgather_weights(ws):
    n = len(ws)

    def body(*refs):
        ins, outs = refs[:n], refs[n:2 * n]
        send, recv = refs[2 * n:]
        x, y, c, _ = _place()
        chip_x, chip_y, chip_d = (1 - x, y), (x, 1 - y), (1 - x, 1 - y)
        sibling = (x, y, 1 - c)
        index = lambda chip: 2 * chip[0] + chip[1]
        sent = []

        def quarter(ref, half, q, rq):
            return ref.at[:, pl.ds((2 * half + q) * rq, rq)]

        def copy(t, kind, src, dst, to):
            return _remote(src, dst, send.at[t, kind], recv.at[t, kind], to)

        def start(cp):
            cp.start()
            sent.append(cp)

        for t in range(n):
            rq = ws[t].shape[1] // 4
            for q in range(2):
                for base, chip in ((0, chip_x), (2, chip_y)):
                    start(copy(t, base + q, quarter(ins[t], c, q, rq), quarter(outs[t].at[index((x, y))], c, q, rq), (*chip, c)))
        for t in range(n):
            rq = ws[t].shape[1] // 4
            landings = [(chip_x, 0, 0, chip_x, ((4, chip_y), (6, None))), (chip_y, 1, 3, chip_y, ((5, chip_x), (9, None))),
                        (chip_x, 1, 1, chip_x, ((7, None),)), (chip_y, 0, 2, chip_y, ((8, None),)),
                        (chip_d, 0, 4, chip_y, ((10, None),)), (chip_d, 1, 5, chip_x, ((11, None),))]
            for origin, q, kind, sender, onward in landings:
                piece = quarter(outs[t].at[index(origin)], c, q, rq)
                copy(t, kind, piece, piece, (*sender, c)).wait_recv()
                for kind2, chip in onward:
                    start(copy(t, kind2, piece, piece, sibling if chip is None else (*chip, c)))
        for t in range(n):
            rq = ws[t].shape[1] // 4
            for kind, (origin, q) in zip(range(6, 12), ((chip_x, 0), (chip_x, 1), (chip_y, 0), (chip_y, 1), (chip_d, 0), (chip_d, 1))):
                piece = quarter(outs[t].at[index(origin)], 1 - c, q, rq)
                copy(t, kind, piece, piece, sibling).wait_recv()
        for cp in sent:
            cp.wait_send()

    res = pl.pallas_call(
        body, name="allgather_weights", in_specs=[ANY] * n, out_specs=[ANY] * n,
        out_shape=[jax.ShapeDtypeStruct((N_CHIPS,) + w.shape, w.dtype) for w in ws],
        scratch_shapes=[pltpu.SemaphoreType.DMA((n, 12)), pltpu.SemaphoreType.DMA((n, 12))],
    )(*ws)
    own = 2 * lax.axis_index("x") + lax.axis_index("y")
    return [lax.dynamic_update_slice(g, w[None], (own, 0, 0, 0)) for g, w in zip(res, ws)]


def _pair_exchange(gs):
    n = len(gs)

    def body(*refs):
        ins, outs = refs[:n], refs[n:2 * n]
        send, recv = refs[2 * n:]
        x, y, c, _ = _place()
        copies = [_remote(_half(ins[t], 1 - c, gs[t].shape[2] // 2, 2), outs[t], send.at[t], recv.at[t], (x, y, 1 - c))
                  for t in range(n)]
        for cp in copies:
            cp.start()
        for cp in copies:
            cp.wait()

    return pl.pallas_call(
        body, name="grad_pair_exchange", in_specs=[ANY] * n, out_specs=[ANY] * n,
        out_shape=[jax.ShapeDtypeStruct(g.shape[:2] + (g.shape[2] // 2, g.shape[3]), F32) for g in gs],
        scratch_shapes=[pltpu.SemaphoreType.DMA((n,)), pltpu.SemaphoreType.DMA((n,))],
    )(*gs)


def _pair_sum(g, theirs, c_idx):
    n4, ly, r, cc = g.shape
    rh = r // 2
    tm = _tile(rh, 256)
    nt = rh // tm

    def body(c_ref, g_ref, t_ref, o_ref):
        o_ref[...] = (g_ref[...] + t_ref[...]).astype(o_ref.dtype)

    blk = (None, tm, cc)
    grid_spec = pltpu.PrefetchScalarGridSpec(
        num_scalar_prefetch=1, grid=(n4 * ly, nt),
        in_specs=[pl.BlockSpec(blk, lambda a, i, cr: (a, cr[0] * nt + i, 0)), pl.BlockSpec(blk, lambda a, i, cr: (a, i, 0))],
        out_specs=pl.BlockSpec(blk, lambda a, i, cr: (a, i, 0)))
    out = pl.pallas_call(
        body, name="grad_pair_sum", grid_spec=grid_spec, out_shape=jax.ShapeDtypeStruct((n4 * ly, rh, cc), BF16),
        compiler_params=_params("parallel", "parallel"))(c_idx, g.reshape(n4 * ly, r, cc), theirs.reshape(n4 * ly, rh, cc))
    return out.reshape(n4, ly, rh, cc)


def _chip_exchange(ps):
    n = len(ps)

    def body(*refs):
        ins, outs = refs[:n], refs[n:2 * n]
        send, recv = refs[2 * n:]
        x, y, c, chips = _place()
        copies = []
        for t in range(n):
            for j, chip in enumerate(chips):
                copies.append(_remote(ins[t].at[2 * chip[0] + chip[1]], outs[t].at[j], send.at[t, j], recv.at[t, j], (*chip, c)))
        for cp in copies:
            cp.start()
        for cp in copies:
            cp.wait()

    return pl.pallas_call(
        body, name="grad_chip_exchange", in_specs=[ANY] * n, out_specs=[ANY] * n,
        out_shape=[jax.ShapeDtypeStruct((3,) + p.shape[1:], p.dtype) for p in ps],
        scratch_shapes=[pltpu.SemaphoreType.DMA((n, 3)), pltpu.SemaphoreType.DMA((n, 3))],
    )(*ps)


def _chip_sum(p, landed, own_idx):
    _, ly, rh, cc = p.shape
    tm = _tile(rh, 256)

    def body(o_ref, p_ref, a_ref, b_ref, c_ref, out_ref):
        up = lambda r: r[...].astype(F32)
        out_ref[...] = ((up(p_ref) + up(a_ref)) + up(b_ref)) + up(c_ref)

    blk = (None, None, tm, cc)
    slot = lambda j: pl.BlockSpec(blk, lambda l, i, o: (j, l, i, 0))
    grid_spec = pltpu.PrefetchScalarGridSpec(
        num_scalar_prefetch=1, grid=(ly, rh // tm),
        in_specs=[pl.BlockSpec(blk, lambda l, i, o: (o[0], l, i, 0)), slot(0), slot(1), slot(2)],
        out_specs=pl.BlockSpec((None, tm, cc), lambda l, i, o: (l, i, 0)))
    return pl.pallas_call(
        body, name="grad_chip_sum", grid_spec=grid_spec, out_shape=jax.ShapeDtypeStruct((ly, rh, cc), F32),
        compiler_params=_params("parallel", "parallel"))(own_idx, p, landed, landed, landed)


def _pair_swap(halves):
    n = len(halves)

    def body(*refs):
        ins, outs = refs[:n], refs[n:2 * n]
        send, recv = refs[2 * n:]
        x, y, c, _ = _place()
        copies = [_remote(ins[t], outs[t], send.at[t], recv.at[t], (x, y, 1 - c)) for t in range(n)]
        for cp in copies:
            cp.start()
        for cp in copies:
            cp.wait()

    return pl.pallas_call(
        body, name="grad_pair_swap", in_specs=[ANY] * n, out_specs=[ANY] * n,
        out_shape=[jax.ShapeDtypeStruct(h.shape, F32) for h in halves],
        scratch_shapes=[pltpu.SemaphoreType.DMA((n,)), pltpu.SemaphoreType.DMA((n,))],
    )(*halves)


def _adamw_big(w, m, v, mine, theirs, c_idx):
    ly, r, cc = w.shape
    rh = r // 2
    tm = _tile(rh, 256)
    nt = rh // tm

    def body(c_ref, w_ref, m_ref, v_ref, a_ref, b_ref, g_out, d_out, m_out, v_out):
        gr = jnp.where(pl.program_id(1) == c_ref[0], a_ref[...], b_ref[...])
        delta, mn, vn = _adamw_update(w_ref[...], gr, m_ref[...], v_ref[...])
        g_out[...] = gr
        d_out[...] = delta
        m_out[...] = mn
        v_out[...] = vn

    blk = (None, tm, cc)
    full = pl.BlockSpec(blk, lambda l, hc, i, cr: (l, hc * nt + i, 0))
    half = pl.BlockSpec(blk, lambda l, hc, i, cr: (l, i, 0))
    grid_spec = pltpu.PrefetchScalarGridSpec(
        num_scalar_prefetch=1, grid=(ly, 2, nt), in_specs=[full, full, full, half, half], out_specs=[full] * 4)
    sd = jax.ShapeDtypeStruct(w.shape, F32)
    return pl.pallas_call(
        body, name="adamw", grid_spec=grid_spec, out_shape=[sd] * 4,
        compiler_params=_params("parallel", "parallel", "parallel"))(c_idx, w, m, v, mine, theirs)


def _allreduce_small(v):
    rows, w = v.shape

    def body(x_ref, sum_ref, all_ref, send, recv, local):
        x, y, c, chips = _place()
        me, sibling = (x, y, c), (x, y, 1 - c)

        def slot(px, py, pc):
            return all_ref.at[4 * px + 2 * py + pc]

        def copy(k, block, to, src=None):
            return _remote(slot(*block) if src is None else src, slot(*block), send.at[k], recv.at[k], to)

        mine = pltpu.make_async_copy(x_ref, slot(*me), local)
        mine.start()
        first = [copy(0, me, sibling, src=x_ref)]
        first += [copy(1 + j, me, (*chip, c), src=x_ref) for j, chip in enumerate(chips)]
        for cp in first:
            cp.start()
        passed = [copy(4 + j, (*chip, c), sibling) for j, chip in enumerate(chips)]
        for j, chip in enumerate(chips):
            copy(1 + j, (*chip, c), me).wait_recv()
            passed[j].start()
        copy(0, sibling, me).wait_recv()
        for j, chip in enumerate(chips):
            copy(4 + j, (*chip, 1 - c), me).wait_recv()
        for cp in first + passed:
            cp.wait_send()
        mine.wait()
        tot = all_ref[0]
        for k in range(1, N_DEV):
            tot = tot + all_ref[k]
        sum_ref[...] = tot

    vm = pl.BlockSpec(memory_space=pltpu.VMEM)
    return pl.pallas_call(
        body, name="allreduce_small", in_specs=[vm], out_specs=[vm, vm],
        out_shape=[jax.ShapeDtypeStruct((rows, w), F32), jax.ShapeDtypeStruct((N_DEV, rows, w), F32)],
        scratch_shapes=[pltpu.SemaphoreType.DMA((7,)), pltpu.SemaphoreType.DMA((7,)), pltpu.SemaphoreType.DMA],
        compiler_params=pltpu.CompilerParams(vmem_limit_bytes=VMEM_LIMIT),
    )(v)[0]


BIG = ["ffn1_w1", "ffn1_w3", "ffn1_w2", "ffn2_w1", "ffn2_w3", "ffn2_w2", "ple_proj", "ple_gate",
       "s5_w_in", "s5_w_glu", "sb_w_qkv", "sb_w_o"]
TRANSPOSED = ("ffn1_w1", "ffn1_w3", "ffn2_w1", "ffn2_w3")
SMALL = ["ffn1_norm", "mix_norm", "ffn2_norm", "ple_norm", "s5_a_re", "s5_a_im", "s5_log_dt", "s5_b_re", "s5_b_im",
         "s5_c_re", "s5_c_im", "s5_d", "final_norm"]
ORDER = ["ffn1_norm", "ffn1_w1", "ffn1_w3", "ffn1_w2", "mix_norm", "ffn2_norm", "ffn2_w1", "ffn2_w3", "ffn2_w2",
         "ple_norm", "ple_proj", "ple_gate", "s5_w_in", "s5_a_re", "s5_a_im", "s5_log_dt", "s5_b_re", "s5_b_im",
         "s5_c_re", "s5_c_im", "s5_d", "s5_w_glu", "sb_w_qkv", "sb_w_o", "final_norm"]


def _pack(arrays):
    flat = jnp.concatenate([a.reshape(-1) for a in arrays])
    pad = (-flat.shape[0]) % 1024
    return jnp.pad(flat, (0, pad)).reshape(-1, 128)


def _unpack(packed, like):
    flat = packed.reshape(-1)
    out, off = [], 0
    for a in like:
        out.append(flat[off:off + a.size].reshape(a.shape))
        off += a.size
    return out


def _fwd_bwd(x, p, target, w, gathered):
    bl, l, d = x.shape
    t = bl * l
    depth = w["ffn1_norm"].shape[0]
    s5_ops, s5_vjp = jax.vjp(_s5_prep, w["s5_a_re"][0], w["s5_a_im"][0], w["s5_log_dt"][0], w["s5_b_re"][0],
                             w["s5_b_im"][0], w["s5_c_re"][0], w["s5_c_im"][0])

    h = x.reshape(t, d)
    p2 = [p[i].reshape(t, p.shape[-1]).astype(BF16) for i in range(depth)]
    saved = []
    for i in range(depth):
        norm = lambda name: w[name][i:i + 1]
        h, s1 = _ffn_fwd(h, norm("ffn1_norm"), gathered["ffn1_w1"], gathered["ffn1_w3"], gathered["ffn1_w2"], i)
        if i % 2 == 0:
            h, s2 = _s5_fwd(h, norm("mix_norm"), s5_ops, w["s5_d"][i // 2:i // 2 + 1], gathered["s5_w_in"], gathered["s5_w_glu"], bl)
        else:
            h, s2 = _sb_fwd(h, norm("mix_norm"), gathered["sb_w_qkv"], gathered["sb_w_o"], bl)
        h, s3 = _ffn_fwd(h, norm("ffn2_norm"), gathered["ffn2_w1"], gathered["ffn2_w3"], gathered["ffn2_w2"], i)
        h, s4 = _ple_fwd(h, norm("ple_norm"), p2[i], gathered["ple_proj"], gathered["ple_gate"], i)
        saved.append((s1, s2, s3, s4))

    loss, dh, dfinal = _head(h, w["final_norm"].reshape(1, d), target.reshape(t, d))

    big = {k: None for k in BIG}
    small = {k: [None] * w[k].shape[0] if w[k].ndim > 1 else None for k in SMALL}
    small["final_norm"] = dfinal.reshape(d)
    for i in reversed(range(depth)):
        norm = lambda name: w[name][i:i + 1]
        slots = lambda *names: [(big[k], i, depth) for k in names]
        s1, s2, s3, s4 = saved[i]
        dh, dg, big["ple_proj"], big["ple_gate"] = _ple_bwd(
            dh, s4, norm("ple_norm"), p2[i], gathered["ple_proj"], gathered["ple_gate"], i, slots("ple_proj", "ple_gate"))
        small["ple_norm"][i] = dg[0]
        dh, dg, big["ffn2_w1"], big["ffn2_w3"], big["ffn2_w2"] = _ffn_bwd(
            dh, s3, norm("ffn2_norm"), gathered["ffn2_w1"], gathered["ffn2_w3"], gathered["ffn2_w2"], i,
            slots("ffn2_w1", "ffn2_w3", "ffn2_w2"))
        small["ffn2_norm"][i] = dg[0]
        if i % 2 == 0:
            dh, dg, big["s5_w_in"], big["s5_w_glu"], dd, dops = _s5_bwd(
                dh, s2, norm("mix_norm"), s5_ops, w["s5_d"][i // 2:i // 2 + 1], gathered["s5_w_in"], gathered["s5_w_glu"], bl)
            small["s5_d"][0] = dd[0]
            raw = s5_vjp(dops)
            for name, gr in zip(["s5_a_re", "s5_a_im", "s5_log_dt", "s5_b_re", "s5_b_im", "s5_c_re", "s5_c_im"], raw):
                small[name][0] = gr
        else:
            dh, dg, big["sb_w_qkv"], big["sb_w_o"] = _sb_bwd(dh, s2, norm("mix_norm"), gathered["sb_w_qkv"], gathered["sb_w_o"], bl)
        small["mix_norm"][i] = dg[0]
        dh, dg, big["ffn1_w1"], big["ffn1_w3"], big["ffn1_w2"] = _ffn_bwd(
            dh, s1, norm("ffn1_norm"), gathered["ffn1_w1"], gathered["ffn1_w3"], gathered["ffn1_w2"], i,
            slots("ffn1_w1", "ffn1_w3", "ffn1_w2"))
        small["ffn1_norm"][i] = dg[0]
    small_list = [jnp.stack(small[k]) if isinstance(small[k], list) else small[k] for k in SMALL]
    return loss, dh.reshape(bl, l, d), big, small_list


def _step(x, p, target, w, m, v):
    flip = lambda tree: {k: jnp.swapaxes(a, 1, 2) if k in TRANSPOSED else a for k, a in tree.items()}
    w, m, v = flip(w), flip(m), flip(v)
    gathered = dict(zip(BIG, _allgather_weights([_to_bf16(w[k]) for k in BIG])))
    loss, grad_x, big, small_list = _fwd_bwd(x, p, target, w, gathered)

    c_idx = lax.axis_index("c").astype(jnp.int32).reshape(1)
    own_idx = (2 * lax.axis_index("x") + lax.axis_index("y")).astype(jnp.int32).reshape(1)
    partial = [big[k] for k in BIG]
    pair = [_pair_sum(g, t, c_idx) for g, t in zip(partial, _pair_exchange(partial))]
    mine = [_chip_sum(pr, ld, own_idx) for pr, ld in zip(pair, _chip_exchange(pair))]
    theirs = _pair_swap(mine)
    out_g, out_d, out_m, out_v = {}, {}, {}, {}
    for k, a, b in zip(BIG, mine, theirs):
        out_g[k], out_d[k], out_m[k], out_v[k] = _adamw_big(w[k], m[k], v[k], a, b, c_idx)

    like = [w[k] for k in SMALL]
    pad = [jnp.zeros((1,), F32)]
    g_small = _allreduce_small(_pack(small_list + [loss.reshape(1)]))
    packed = (g_small,) + tuple(_adamw_small(_pack(like + pad), g_small, _pack([m[k] for k in SMALL] + pad),
                                             _pack([v[k] for k in SMALL] + pad)))
    for dst, pk in zip((out_g, out_d, out_m, out_v), packed):
        dst.update(dict(zip(SMALL, _unpack(pk, like))))
    loss = g_small.reshape(-1)[sum(a.size for a in like)]
    out_g, out_d, out_m, out_v = flip(out_g), flip(out_d), flip(out_m), flip(out_v)
    return (loss, grad_x, *[out_g[k] for k in ORDER], *[out_d[k] for k in ORDER],
            *[out_m[k] for k in ORDER], *[out_v[k] for k in ORDER])


def kernel(x, p, ffn1_norm, ffn1_w1, ffn1_w3, ffn1_w2, mix_norm, ffn2_norm, ffn2_w1, ffn2_w3, ffn2_w2, ple_norm, ple_proj, ple_gate, s5_w_in, s5_a_re, s5_a_im, s5_log_dt, s5_b_re, s5_b_im, s5_c_re, s5_c_im, s5_d, s5_w_glu, sb_w_qkv, sb_w_o, final_norm, loss_target, m_ffn1_norm, m_ffn1_w1, m_ffn1_w3, m_ffn1_w2, m_mix_norm, m_ffn2_norm, m_ffn2_w1, m_ffn2_w3, m_ffn2_w2, m_ple_norm, m_ple_proj, m_ple_gate, m_s5_w_in, m_s5_a_re, m_s5_a_im, m_s5_log_dt, m_s5_b_re, m_s5_b_im, m_s5_c_re, m_s5_c_im, m_s5_d, m_s5_w_glu, m_sb_w_qkv, m_sb_w_o, m_final_norm, v_ffn1_norm, v_ffn1_w1, v_ffn1_w3, v_ffn1_w2, v_mix_norm, v_ffn2_norm, v_ffn2_w1, v_ffn2_w3, v_ffn2_w2, v_ple_norm, v_ple_proj, v_ple_gate, v_s5_w_in, v_s5_a_re, v_s5_a_im, v_s5_log_dt, v_s5_b_re, v_s5_b_im, v_s5_c_re, v_s5_c_im, v_s5_d, v_s5_w_glu, v_sb_w_qkv, v_sb_w_o, v_final_norm):
    args = dict(locals())
    w = {k: args[k] for k in ORDER}
    m = {k: args["m_" + k] for k in ORDER}
    v = {k: args["v_" + k] for k in ORDER}
    return _step(x, p, loss_target, w, m, v)
```

```python
import functools
import math

import jax
import jax.numpy as jnp
from jax import lax
from jax.experimental import pallas as pl
from jax.experimental.pallas import tpu as pltpu

F32 = jnp.float32
BF16 = jnp.bfloat16
MESH = pl.DeviceIdType.MESH

N_CHIPS = 4
N_DEV = 8
RMS_EPS = 1e-6
S5_GROUP = 16
S5_STATE = 64
S5_CHUNK = 16
SB_HEAD_DIM = 64
SB_BLOCK = 128
SB_CUT = -104.0
SB_UNROLL = 3
ADAM_LR, ADAM_B1, ADAM_B2, ADAM_EPS, ADAM_WD, ADAM_STEP = 0.001, 0.9, 0.999, 1e-08, 0.01, 10
VMEM_LIMIT = 48 * 1024 * 1024

NN = (((1,), (0,)), ((), ()))
NT = (((1,), (1,)), ((), ()))
TN = (((0,), (0,)), ((), ()))

ANY = pl.BlockSpec(memory_space=pl.ANY)


def _tile(n, target):
    if n <= target:
        return n
    for t in range(target - target % 8, 7, -8):
        if n % t == 0:
            return t
    raise ValueError(f"no row tile for {n}")


def _params(*semantics):
    return pltpu.CompilerParams(dimension_semantics=semantics, vmem_limit_bytes=VMEM_LIMIT)


def _sigmoid(v):
    return 1.0 / (1.0 + jnp.exp(-v))


def _gemm(name, grid, operands, in_specs, groups, acc_shapes, out_shapes, out_specs, epilogue, reduce_axis=None, aliases=None):
    n_in, n_out = len(operands), len(out_shapes)
    n_red = None if reduce_axis is None else grid[reduce_axis]

    def body(*refs):
        ins, outs, accs = refs[:n_in], refs[n_in:n_in + n_out], refs[n_in + n_out:]

        def products():
            res = []
            for terms in groups:
                tot = None
                for ia, ib, dims in terms:
                    d = lax.dot_general(ins[ia][...], ins[ib][...], dims, preferred_element_type=F32)
                    tot = d if tot is None else tot + d
                res.append(tot)
            return res

        def finish(vals):
            for o, v in zip(outs, epilogue(vals, ins)):
                o[...] = v.astype(o.dtype)

        if reduce_axis is None:
            finish(products())
        else:
            k = pl.program_id(reduce_axis)

            @pl.when(k == 0)
            def _():
                for a in accs:
                    a[...] = jnp.zeros_like(a)

            for a, d in zip(accs, products()):
                a[...] += d

            @pl.when(k == n_red - 1)
            def _():
                finish([a[...] for a in accs])

    scratch = [] if reduce_axis is None else [pltpu.VMEM(s, F32) for s in acc_shapes]
    sem = tuple("arbitrary" if i == reduce_axis else "parallel" for i in range(len(grid)))
    return pl.pallas_call(
        body, name=name, grid=grid, in_specs=in_specs, out_specs=out_specs, out_shape=out_shapes,
        scratch_shapes=scratch, input_output_aliases=aliases or {}, compiler_params=_params(*sem))(*operands)


def _ident(vals, ins):
    return vals


def _act_spec(layout, tm, cs, pos):
    if layout == "sm":
        return pl.BlockSpec((None, tm, cs), lambda *g: (pos(*g)[1], pos(*g)[0], 0))
    return pl.BlockSpec((tm, cs), lambda *g: pos(*g))


def _act_shape(layout, t, cs, dtype):
    return jax.ShapeDtypeStruct((N_CHIPS, t, cs) if layout == "sm" else (t, N_CHIPS * cs), dtype)


def _w_spec(w, layer, pos_k):
    _, _, r, c = w.shape
    return pl.BlockSpec((None, None, r, c), lambda *g: (pos_k(*g), layer, 0, 0))


def _mm_cs(name, x, w, layer, out_layout, out_dtype, tm=1024):
    t, kd = x.shape
    cs = w.shape[3]
    tm = _tile(t, tm)
    return _gemm(
        name, (N_CHIPS, t // tm), [x, w],
        [pl.BlockSpec((tm, kd), lambda k, i: (i, 0)), _w_spec(w, layer, lambda k, i: k)],
        [[(0, 1, NN)]], None, [_act_shape(out_layout, t, cs, out_dtype)],
        [_act_spec(out_layout, tm, cs, lambda k, i: (i, k))], _ident)[0]


def _mm_rs(name, xs, layout, w, layer, res=None, alpha=1.0, out_dtype=F32, tm=1024):
    ks, n = w.shape[2], w.shape[3]
    t = xs.shape[1] if layout == "sm" else xs.shape[0]
    tm = _tile(t, tm)
    operands = [xs, w] + ([] if res is None else [res])
    specs = [_act_spec(layout, tm, ks, lambda i, k: (i, k)), _w_spec(w, layer, lambda i, k: k)]
    if res is not None:
        specs.append(pl.BlockSpec((tm, n), lambda i, k: (i, 0)))

    def epilogue(vals, ins):
        y = alpha * vals[0]
        return [y if res is None else ins[2][...] + y]

    return _gemm(
        name, (t // tm, N_CHIPS), operands, specs, [[(0, 1, NN)]], [(tm, n)],
        [jax.ShapeDtypeStruct((t, n), out_dtype)], [pl.BlockSpec((tm, n), lambda i, k: (i, 0))],
        epilogue, reduce_axis=1)[0]


def _mm_cs_dx(name, pairs, layout, layer, tm=1024, transposed=False):
    w0 = pairs[0][1]
    kd, cs = (w0.shape[3], w0.shape[2]) if transposed else (w0.shape[2], w0.shape[3])
    dy0 = pairs[0][0]
    t = dy0.shape[1] if layout == "sm" else dy0.shape[0]
    tm = _tile(t, tm)
    operands, specs, terms = [], [], []
    for dy, w in pairs:
        terms.append((len(operands), len(operands) + 1, NN if transposed else NT))
        operands += [dy, w]
        specs += [_act_spec(layout, tm, cs, lambda i, k: (i, k)), _w_spec(w, layer, lambda i, k: k)]
    return _gemm(
        name, (t // tm, N_CHIPS), operands, specs, [terms], [(tm, kd)],
        [jax.ShapeDtypeStruct((t, kd), F32)], [pl.BlockSpec((tm, kd), lambda i, k: (i, 0))],
        _ident, reduce_axis=1)[0]


def _mm_rs_dx(name, dy, w, layer, out_layout, out_dtype, tm=1024):
    t, n = dy.shape
    ks = w.shape[2]
    tm = _tile(t, tm)
    return _gemm(
        name, (N_CHIPS, t // tm), [dy, w],
        [pl.BlockSpec((tm, n), lambda k, i: (i, 0)), _w_spec(w, layer, lambda k, i: k)],
        [[(0, 1, NT)]], None, [_act_shape(out_layout, t, ks, out_dtype)],
        [_act_spec(out_layout, tm, ks, lambda k, i: (i, k))], _ident)[0]


def _mm_dw(name, x, x_layout, dy, dy_layout, slot, alpha=1.0, tk=2048):
    stack, layer, layers = slot
    if x_layout is None:
        t, rows = x.shape
        cols = dy.shape[2] if dy_layout == "sm" else dy.shape[1] // N_CHIPS
        tk = _tile(t, tk)
        xspec = pl.BlockSpec((tk, rows), lambda k, j: (j, 0))
        yspec = _act_spec(dy_layout, tk, cols, lambda k, j: (j, k))
    else:
        t, cols = dy.shape
        rows = x.shape[2] if x_layout == "sm" else x.shape[1] // N_CHIPS
        tk = _tile(t, tk)
        xspec = _act_spec(x_layout, tk, rows, lambda k, j: (j, k))
        yspec = pl.BlockSpec((tk, cols), lambda k, j: (j, 0))
    operands, specs = [x, dy], [xspec, yspec]
    if stack is not None:
        operands.append(stack)
        specs.append(ANY)
    return _gemm(
        name, (N_CHIPS, t // tk), operands, specs, [[(0, 1, TN)]], [(rows, cols)],
        [jax.ShapeDtypeStruct((N_CHIPS, layers, rows, cols), F32)],
        [pl.BlockSpec((None, None, rows, cols), lambda k, j: (k, layer, 0, 0))],
        lambda vals, ins: [alpha * vals[0]], reduce_axis=1, aliases=None if stack is None else {2: 0})[0]


def _rows(name, fn, ins, outs, accs=(), tm=512):
    t = ins[0].shape[0]
    tm = _tile(t, tm)
    n_in, n_out, n_acc = len(ins), len(outs), len(accs)
    in_specs = []
    for a in ins:
        if a.shape[0] == t:
            in_specs.append(pl.BlockSpec((tm, a.shape[1]), lambda i: (i, 0)))
        else:
            in_specs.append(pl.BlockSpec(a.shape, lambda i: (0, 0)))
    out_shape = [jax.ShapeDtypeStruct((t, c), d) for c, d in outs] + [jax.ShapeDtypeStruct(s, F32) for s in accs]
    out_specs = [pl.BlockSpec((tm, c), lambda i: (i, 0)) for c, _ in outs] + [pl.BlockSpec(s, lambda i: (0, 0)) for s in accs]

    def body(*refs):
        i = pl.program_id(0)
        row_vals, acc_vals = fn(*[r[...] for r in refs[:n_in]])
        for o, v in zip(refs[n_in:n_in + n_out], row_vals):
            o[...] = v.astype(o.dtype)
        acc_refs = refs[n_in + n_out:]
        if n_acc:
            @pl.when(i == 0)
            def _():
                for a in acc_refs:
                    a[...] = jnp.zeros_like(a)

            for a, v in zip(acc_refs, acc_vals):
                a[...] += v

    res = pl.pallas_call(
        body, name=name, grid=(t // tm,), in_specs=in_specs, out_specs=out_specs, out_shape=out_shape,
        compiler_params=_params("arbitrary" if n_acc else "parallel"))(*ins)
    return res[:n_out], res[n_out:]


def _to_bf16(a):
    def fn(x):
        return [x], []
    return _rows("weights_bf16", fn, [a.reshape(-1, a.shape[-1])], [(a.shape[-1], BF16)], tm=512)[0][0].reshape(a.shape)


def _rms_stats(x):
    return lax.rsqrt(jnp.mean(x * x, axis=-1, keepdims=True) + RMS_EPS)


def _rmsnorm(name, h, g):
    def fn(x, gg):
        return [x * _rms_stats(x) * gg], []
    return _rows(name, fn, [h, g], [(h.shape[1], BF16)])[0][0]


def _rms_bwd_math(dn, x, g):
    r = _rms_stats(x)
    xhat = x * r
    dxh = dn * g
    dx = r * (dxh - xhat * jnp.mean(dxh * xhat, axis=-1, keepdims=True))
    return dx, jnp.sum(dn * xhat, axis=0, keepdims=True)


def _rmsnorm_bwd(name, dres, dn, h, g):
    def fn(dr, d, x, gg):
        dx, dg = _rms_bwd_math(d, x, gg)
        return [dr + dx], [dg]
    (dh,), (dg,) = _rows(name, fn, [dres, dn, h, g], [(h.shape[1], F32)], [(1, h.shape[1])])
    return dh, dg


def _ffn_fwd(h, g, w1, w3, w2, layer, tm=1024):
    t, d = h.shape
    fs = w1.shape[2]
    n = _rmsnorm("ffn_norm", h, g)
    tm = _tile(t, tm)

    def up(vals, ins):
        a, b = vals
        sg = _sigmoid(a)
        silu = a * sg
        return [b * sg * (1.0 + a * (1.0 - sg)), silu, silu * b]

    sm = _act_shape("sm", t, fs, BF16)
    osp = _act_spec("sm", tm, fs, lambda k, i: (i, k))
    ga, gb, s = _gemm(
        "ffn_up", (N_CHIPS, t // tm), [n, w1, w3],
        [pl.BlockSpec((tm, d), lambda k, i: (i, 0)), _w_spec(w1, layer, lambda k, i: k), _w_spec(w3, layer, lambda k, i: k)],
        [[(0, 1, NT)], [(0, 2, NT)]], None, [sm, sm, sm], [osp, osp, osp], up)
    out = _mm_rs("ffn_down", s, "sm", w2, layer, res=h, alpha=0.5)
    return out, (h, n, ga, gb, s)


def _ffn_bwd(dout, saved, g, w1, w3, w2, layer, slots, tm=1024):
    h, n, ga, gb, s = saved
    t, d = h.shape
    fs = w1.shape[2]
    tm = _tile(t, tm)
    dob = dout.astype(BF16)

    def down(vals, ins):
        ds = 0.5 * vals[0]
        return [ds * ins[2][...].astype(F32), ds * ins[3][...].astype(F32)]

    sm = _act_shape("sm", t, fs, BF16)
    asp = _act_spec("sm", tm, fs, lambda k, i: (i, k))
    da, db = _gemm(
        "ffn_down_dx", (N_CHIPS, t // tm), [dob, w2, ga, gb],
        [pl.BlockSpec((tm, d), lambda k, i: (i, 0)), _w_spec(w2, layer, lambda k, i: k), asp, asp],
        [[(0, 1, NT)]], None, [sm, sm], [asp, asp], down)
    dw2 = _mm_dw("ffn_dw2", s, "sm", dob, None, slots[2], alpha=0.5)
    dw1 = _mm_dw("ffn_dw1", da, "sm", n, None, slots[0])
    dw3 = _mm_dw("ffn_dw3", db, "sm", n, None, slots[1])
    dn = _mm_cs_dx("ffn_up_dx", [(da, w1), (db, w3)], "sm", layer, transposed=True)
    dh, dg = _rmsnorm_bwd("ffn_norm_bwd", dout, dn, h, g)
    return dh, dg, dw1, dw3, dw2


def _ple_fwd(h, g, p2, wproj, wgate, layer):
    n = _rmsnorm("ple_norm", h, g)
    gl = _mm_rs("ple_gate", n, "flat", wgate, layer)
    pp = _mm_cs("ple_proj", p2, wproj, layer, "flat", F32)

    def fn(hh, gg, q):
        return [hh + q * _sigmoid(gg)], []
    out = _rows("ple_mix", fn, [h, gl, pp], [(h.shape[1], F32)])[0][0]
    return out, (h, n, gl, pp)


def _ple_bwd(dout, saved, g, p2, wproj, wgate, layer, slots):
    h, n, gl, pp = saved
    d = h.shape[1]

    def fn(do, gg, q):
        sg = _sigmoid(gg)
        return [do * sg, do * q * sg * (1.0 - sg)], []
    (dpp, dgl), _ = _rows("ple_mix_bwd", fn, [dout, gl, pp], [(d, BF16), (d, BF16)])
    dwproj = _mm_dw("ple_dwproj", p2, None, dpp, "flat", slots[0])
    dwgate = _mm_dw("ple_dwgate", n, "flat", dgl, None, slots[1])
    dn = _mm_rs_dx("ple_gate_dx", dgl, wgate, layer, "flat", F32)
    dh, dg = _rmsnorm_bwd("ple_norm_bwd", dout, dn, h, g)
    return dh, dg, dwproj, dwgate


def _head(h, g, target):
    d = h.shape[1]

    def fn(x, gg, tg):
        y = x * _rms_stats(x) * gg
        err = y - tg
        dy = err * (1.0 / d)
        dx, dg = _rms_bwd_math(dy, x, gg)
        loss = 0.5 * jnp.sum(jnp.sum(err * err, axis=-1, keepdims=True) * (1.0 / d), axis=0, keepdims=True)
        return [dx], [dg, jnp.broadcast_to(loss, (1, 128))]
    (dh,), (dg, loss) = _rows("loss_head", fn, [h, g, target], [(d, F32)], [(1, d), (1, 128)])
    return loss[0, 0], dh, dg


S5_LANES = 2 * S5_STATE
S5_GB = 128 // S5_GROUP


def _s5_prep(a_re, a_im, log_dt, b_re, b_im, c_re, c_im):
    c, gb = S5_CHUNK, S5_GB
    g = a_re.shape[0]
    nb = g // gb
    lam_re = jnp.minimum(a_re, -1e-4)
    lam_im = a_im
    dt = jnp.exp(log_dt)[:, None, None]
    ks = jnp.arange(c + 1, dtype=F32)
    mag = jnp.exp(lam_re[..., None] * dt * ks)
    ph = lam_im[..., None] * dt * ks
    pw_re, pw_im = mag * jnp.cos(ph), mag * jnp.sin(ph)
    den = lam_re * lam_re + lam_im * lam_im
    nr, ni = pw_re[..., 1] - 1.0, pw_im[..., 1]
    fr = (nr * lam_re + ni * lam_im) / den
    fi = (ni * lam_re - nr * lam_im) / den
    bb_re = fr[..., None] * b_re - fi[..., None] * b_im
    bb_im = fr[..., None] * b_im + fi[..., None] * b_re
    ct_re, ct_im = c_re.transpose(0, 2, 1), c_im.transpose(0, 2, 1)
    ca_re = ct_re[:, :, None, :] * pw_re[..., None] - ct_im[:, :, None, :] * pw_im[..., None]
    ca_im = ct_re[:, :, None, :] * pw_im[..., None] + ct_im[:, :, None, :] * pw_re[..., None]
    hp = lax.Precision.HIGHEST
    kern = (jnp.einsum("gpj,gpkh->gkjh", bb_re, ca_re[:, :, :c], precision=hp)
            - jnp.einsum("gpj,gpkh->gkjh", bb_im, ca_im[:, :, :c], precision=hp))
    rev_re = pw_re[:, :, :c][:, :, ::-1].transpose(0, 2, 1)
    rev_im = pw_im[:, :, :c][:, :, ::-1].transpose(0, 2, 1)
    bt_re, bt_im = bb_re.transpose(0, 2, 1), bb_im.transpose(0, 2, 1)
    wn_re = rev_re[:, :, None, :] * bt_re[:, None] - rev_im[:, :, None, :] * bt_im[:, None]
    wn_im = rev_re[:, :, None, :] * bt_im[:, None] + rev_im[:, :, None, :] * bt_re[:, None]
    wn = jnp.concatenate([wn_re, wn_im], axis=-1)
    wo = jnp.concatenate([ca_re[:, :, 1:].transpose(0, 2, 3, 1), -ca_im[:, :, 1:].transpose(0, 2, 3, 1)], axis=-1)

    def blocks(x):
        return x.reshape(nb, gb, c, S5_GROUP, x.shape[3]).transpose(0, 2, 1, 3, 4).reshape(nb, c, gb * S5_GROUP, x.shape[3])

    ar, ai = pw_re[..., c], pw_im[..., c]
    return (jnp.tile(blocks(kern), (1, 1, 1, gb)), blocks(wn), blocks(wo),
            jnp.concatenate([ar, ar], axis=1), jnp.concatenate([-ai, ai], axis=1))


def _step_rows(ref, tau, n):
    return ref[pl.ds(tau, n, stride=S5_CHUNK), :].astype(BF16)


def _cat_groups(ref, dtype):
    return jnp.concatenate([ref[:, j, :] for j in range(S5_GB)], axis=1).astype(dtype)


def _cat_steps(ref, n):
    return jnp.concatenate([_step_rows(ref, tau, n) for tau in range(S5_CHUNK)], axis=1)


def _stack_steps(ref, n):
    return jnp.concatenate([_step_rows(ref, tau, n) for tau in range(S5_CHUNK)], axis=0)


def _cat_ops(ref, axis, reverse=False):
    order = range(S5_CHUNK - 1, -1, -1) if reverse else range(S5_CHUNK)
    return jnp.concatenate([ref[k] for k in order], axis=axis)


def _row_group(rows, lanes):
    row = (lax.broadcasted_iota(jnp.int32, (rows, lanes), 0) // S5_GROUP) % S5_GB
    lane = (lax.broadcasted_iota(jnp.int32, (rows, lanes), 1) // S5_GROUP) % S5_GB
    return row, lane


def _own_group(x):
    row, lane = _row_group(*x.shape)
    return jnp.where(row == lane, x, jnp.zeros_like(x))


def _spread(x):
    row, _ = _row_group(*x.shape)
    return jnp.concatenate([jnp.where(row == j, x, jnp.zeros_like(x)) for j in range(S5_GB)], axis=1)


def _gather_own(x):
    row, _ = _row_group(x.shape[0], S5_LANES)
    out = jnp.zeros((x.shape[0], S5_LANES), x.dtype)
    for j in range(S5_GB):
        out = out + jnp.where(row == j, x[:, j * S5_LANES:(j + 1) * S5_LANES], 0.0)
    return out


def _s5_specs(t, d):
    nct, g = t // S5_CHUNK, d // S5_GROUP
    tok = pl.BlockSpec((t, 128), lambda i: (0, i))
    st = pl.BlockSpec((nct, S5_GB, S5_LANES), lambda i: (0, i, 0))
    op = lambda w: pl.BlockSpec((None,) + w.shape[1:], lambda i: (i, 0, 0, 0))
    return nct, g, tok, st, op


def _s5_chunk_fwd(u, bd, bn):
    t, d = u.shape
    nct, g, tok, st, op = _s5_specs(t, d)
    c = S5_CHUNK

    def body(u_ref, bd_ref, bn_ref, y_ref, s_ref):
        ucat = _cat_steps(u_ref, nct)
        sloc = jnp.dot(ucat, _spread(_cat_ops(bn_ref, 0)), preferred_element_type=F32)
        for j in range(S5_GB):
            s_ref[:, j, :] = sloc[:, j * S5_LANES:(j + 1) * S5_LANES]
        lags = _own_group(_cat_ops(bd_ref, 0, reverse=True))
        for tt in range(c):
            y_ref[pl.ds(tt, nct, stride=c), :] = jnp.dot(ucat[:, :(tt + 1) * 128], lags[(c - 1 - tt) * 128:, :],
                                                         preferred_element_type=F32)

    return pl.pallas_call(
        body, name="s5_chunk", grid=(d // 128,), in_specs=[tok, op(bd), op(bn)], out_specs=[tok, st],
        out_shape=[jax.ShapeDtypeStruct((t, d), F32), jax.ShapeDtypeStruct((nct, g, S5_LANES), F32)],
        compiler_params=_params("parallel"))(u, bd, bn)


def _s5_state_out(sprev, co, yin):
    t, d = yin.shape
    nct, g, tok, st, op = _s5_specs(t, d)
    c = S5_CHUNK

    def body(s_ref, co_ref, yi_ref, y_ref):
        ys = lax.dot_general(_cat_groups(s_ref, BF16), _spread(_cat_ops(co_ref, 0)), NT,
                             preferred_element_type=F32)
        for tt in range(c):
            rows = pl.ds(tt, nct, stride=c)
            y_ref[rows, :] = yi_ref[rows, :] + ys[:, tt * 128:(tt + 1) * 128]

    return pl.pallas_call(
        body, name="s5_state_out", grid=(d // 128,), in_specs=[st, op(co), tok], out_specs=tok,
        out_shape=jax.ShapeDtypeStruct((t, d), F32), compiler_params=_params("parallel"))(sprev, co, yin)


def _s5_state_out_dx(dyb, co):
    t, d = dyb.shape
    nct, g, tok, st, op = _s5_specs(t, d)
    c = S5_CHUNK

    def body(dy_ref, co_ref, ds_ref):
        acc = jnp.dot(_cat_steps(dy_ref, nct), _spread(_cat_ops(co_ref, 0)), preferred_element_type=F32)
        for j in range(S5_GB):
            ds_ref[:, j, :] = acc[:, j * S5_LANES:(j + 1) * S5_LANES]

    return pl.pallas_call(
        body, name="s5_state_out_dx", grid=(d // 128,), in_specs=[tok, op(co)], out_specs=st,
        out_shape=jax.ShapeDtypeStruct((nct, g, S5_LANES), F32), compiler_params=_params("parallel"))(dyb, co)


def _s5_chunk_dx(dyb, dsloc, bd, bn, skip):
    t, d = dyb.shape
    nct, g, tok, st, op = _s5_specs(t, d)
    c = S5_CHUNK

    def body(dy_ref, ds_ref, bd_ref, bn_ref, sk_ref, du_ref):
        dus = lax.dot_general(_cat_groups(ds_ref, BF16), _spread(_cat_ops(bn_ref, 0)), NT, preferred_element_type=F32)
        dycat = _cat_steps(dy_ref, nct)
        lags = _own_group(_cat_ops(bd_ref, 1))
        for tau in range(c):
            rows = pl.ds(tau, nct, stride=c)
            du_ref[rows, :] = (sk_ref[rows, :] + dus[:, tau * 128:(tau + 1) * 128]
                               + lax.dot_general(dycat[:, tau * 128:], lags[:, :(c - tau) * 128], NT,
                                                 preferred_element_type=F32))

    return pl.pallas_call(
        body, name="s5_chunk_dx", grid=(d // 128,), in_specs=[tok, st, op(bd), op(bn), tok], out_specs=tok,
        out_shape=jax.ShapeDtypeStruct((t, d), F32), compiler_params=_params("parallel"))(dyb, dsloc, bd, bn, skip)


def _s5_chunk_dw(u, dyb, dsloc, bd, bn):
    t, d = u.shape
    nct, g, tok, st, op = _s5_specs(t, d)
    c = S5_CHUNK

    def body(u_ref, dy_ref, ds_ref, dbd_ref, dbn_ref):
        dbn = _gather_own(lax.dot_general(_cat_steps(u_ref, nct), _cat_groups(ds_ref, BF16), TN,
                                          preferred_element_type=F32))
        for tau in range(c):
            dbn_ref[tau] = dbn[tau * 128:(tau + 1) * 128, :]
        ustk, dystk = _stack_steps(u_ref, nct), _stack_steps(dy_ref, nct)
        for k in range(c):
            dbd_ref[k] = _own_group(lax.dot_general(ustk[:(c - k) * nct], dystk[k * nct:], TN,
                                                    preferred_element_type=F32))

    return pl.pallas_call(
        body, name="s5_chunk_dw", grid=(d // 128,), in_specs=[tok, tok, st], out_specs=[op(bd), op(bn)],
        out_shape=[jax.ShapeDtypeStruct(bd.shape, F32), jax.ShapeDtypeStruct(bn.shape, F32)],
        compiler_params=_params("parallel"))(u, dyb, dsloc)


def _s5_state_out_dw(sprev, dyb, co):
    t, d = dyb.shape
    nct, g, tok, st, op = _s5_specs(t, d)
    c = S5_CHUNK

    def body(s_ref, dy_ref, dco_ref):
        dco = _gather_own(lax.dot_general(_cat_steps(dy_ref, nct), _cat_groups(s_ref, BF16), TN,
                                          preferred_element_type=F32))
        for tt in range(c):
            dco_ref[tt] = dco[tt * 128:(tt + 1) * 128, :]

    return pl.pallas_call(
        body, name="s5_state_out_dw", grid=(d // 128,), in_specs=[st, tok], out_specs=op(co),
        out_shape=jax.ShapeDtypeStruct(co.shape, F32), compiler_params=_params("parallel"))(sprev, dyb)


def _s5_scan_fwd(sloc, m1, m2):
    bl, nc, g, w = sloc.shape

    def body(s_ref, m1_ref, m2_ref, o_ref):
        a1, a2 = m1_ref[...], m2_ref[...]

        def step(c, states):
            new = []
            for b, s in enumerate(states):
                o_ref[b, c] = s
                new.append(a1 * s + a2 * pltpu.roll(s, S5_STATE, 1) + s_ref[b, c])
            return tuple(new)
        lax.fori_loop(0, nc, step, tuple(jnp.zeros((g, w), F32) for _ in range(bl)))

    vm = pl.BlockSpec(memory_space=pltpu.VMEM)
    return pl.pallas_call(
        body, name="s5_scan", in_specs=[vm, vm, vm], out_specs=vm,
        out_shape=jax.ShapeDtypeStruct(sloc.shape, F32),
        compiler_params=pltpu.CompilerParams(vmem_limit_bytes=VMEM_LIMIT))(sloc, m1, m2)


def _s5_scan_bwd(dsprev, sprev, m1, m2):
    bl, nc, g, w = dsprev.shape

    def body(d_ref, s_ref, m1_ref, m2_ref, g_ref, p1_ref, p2_ref):
        a1, a2 = m1_ref[...], m2_ref[...]
        zero = jnp.zeros((g, w), F32)

        def step(i, carry):
            gps, p1, p2 = carry
            c = nc - 2 - i
            new = []
            for b, gp in enumerate(gps):
                g_ref[b, c] = gp
                sp = s_ref[b, c]
                p1 = p1 + gp * sp
                p2 = p2 + gp * pltpu.roll(sp, S5_STATE, 1)
                new.append(d_ref[b, c] + a1 * gp - a2 * pltpu.roll(gp, S5_STATE, 1))
            return tuple(new), p1, p2

        for b in range(bl):
            g_ref[b, nc - 1] = zero
        _, p1, p2 = lax.fori_loop(0, nc - 1, step, (tuple(d_ref[b, nc - 1] for b in range(bl)), zero, zero))
        p1_ref[...] = p1
        p2_ref[...] = p2

    vm = pl.BlockSpec(memory_space=pltpu.VMEM)
    sd = jax.ShapeDtypeStruct
    return pl.pallas_call(
        body, name="s5_scan_bwd", in_specs=[vm, vm, vm, vm], out_specs=[vm, vm, vm],
        out_shape=[sd(dsprev.shape, F32), sd((g, w), F32), sd((g, w), F32)],
        compiler_params=pltpu.CompilerParams(vmem_limit_bytes=VMEM_LIMIT))(dsprev, sprev, m1, m2)


def _gelu_tanh_parts(y):
    c0 = math.sqrt(2.0 / math.pi)
    inner = c0 * (y + 0.044715 * y * y * y)
    th = jnp.tanh(inner)
    return th, c0 * (1.0 + 3 * 0.044715 * y * y)


def _s5_fwd(h, g, ops, d_skip, w_in, w_glu, bl):
    bd, bn, co, m1, m2 = ops
    t, d = h.shape
    nct, groups = t // S5_CHUNK, d // S5_GROUP
    hn = _rmsnorm("mix_norm", h, g)
    u = _mm_rs("s5_in", hn, "flat", w_in, 0)
    yin, sloc = _s5_chunk_fwd(u, bd.astype(BF16), bn.astype(BF16))
    sprev = _s5_scan_fwd(sloc.reshape(bl, nct // bl, groups, S5_LANES), m1, m2).reshape(nct, groups, S5_LANES)
    y = _s5_state_out(sprev, co.astype(BF16), yin)

    def fn(yy, uu, dd):
        y2 = yy + dd * uu
        th, _ = _gelu_tanh_parts(y2)
        return [0.5 * y2 * (1.0 + th)], []
    z = _rows("s5_gelu", fn, [y, u, d_skip], [(d, BF16)])[0][0]
    zz = _mm_cs("s5_glu", z, w_glu, 0, "flat", F32)

    def glu(hh, zv):
        return [hh + zv[:, :d] * _sigmoid(zv[:, d:])], []
    out = _rows("s5_glu_mix", glu, [h, zz], [(d, F32)])[0][0]
    return out, (h, hn, u, sprev, y, z, zz)


def _s5_bwd(dout, saved, g, ops, d_skip, w_in, w_glu, bl):
    h, hn, u, sprev, y, z, zz = saved
    bd, bn, co, m1, m2 = ops
    t, d = h.shape
    nct, groups = t // S5_CHUNK, d // S5_GROUP

    def glu_bwd(do, zv):
        sg = _sigmoid(zv[:, d:])
        return [jnp.concatenate([do * sg, do * zv[:, :d] * sg * (1.0 - sg)], axis=1)], []
    dzz = _rows("s5_glu_bwd", glu_bwd, [dout, zz], [(2 * d, BF16)])[0][0]
    dwglu = _mm_dw("s5_dwglu", z, None, dzz, "flat", (None, 0, 1))
    dz = _mm_cs_dx("s5_glu_dx", [(dzz, w_glu)], "flat", 0)

    def gelu_bwd(dzv, yy, uu, dd):
        y2 = yy + dd * uu
        th, dinner = _gelu_tanh_parts(y2)
        dy2 = dzv * (0.5 * (1.0 + th) + 0.5 * y2 * (1.0 - th * th) * dinner)
        return [dy2, dy2 * dd], [jnp.sum(dy2 * uu, axis=0, keepdims=True)]
    (dyb, du_skip), (dd,) = _rows("s5_gelu_bwd", gelu_bwd, [dz, y, u, d_skip], [(d, F32), (d, F32)], [(1, d)])
    bd_b, bn_b, co_b = bd.astype(BF16), bn.astype(BF16), co.astype(BF16)
    dsprev = _s5_state_out_dx(dyb, co_b)
    shape4 = (bl, nct // bl, groups, S5_LANES)
    dsloc, dm1, dm2 = _s5_scan_bwd(dsprev.reshape(shape4), sprev.reshape(shape4), m1, m2)
    dsloc = dsloc.reshape(nct, groups, S5_LANES)
    du = _s5_chunk_dx(dyb, dsloc, bd_b, bn_b, du_skip).astype(BF16)
    dbd, dbn = _s5_chunk_dw(u, dyb, dsloc, bd, bn)
    dco = _s5_state_out_dw(sprev, dyb, co)
    dwin = _mm_dw("s5_dwin", hn, "flat", du, None, (None, 0, 1))
    dhn = _mm_rs_dx("s5_in_dx", du, w_in, 0, "flat", F32)
    dh, dg = _rmsnorm_bwd("mix_norm_bwd", dout, dhn, h, g)
    return dh, dg, dwin, dwglu, dd, (dbd, dbn, dco, dm1, dm2)


def _sb_block(qi, idx, tb):
    kb = qi - idx
    return pl.multiple_of(jnp.maximum(kb, 0) * tb, tb), idx == 0, kb >= 0


def _sb_scores(q, kblk, diag, exists, row, col):
    z = lax.dot_general(q, kblk, NT, preferred_element_type=F32) * (SB_HEAD_DIM ** -0.5)
    l1 = jnp.log(1.0 + jnp.exp(-jnp.abs(z)))
    ls = jnp.minimum(z, 0.0) - l1
    mask = jnp.logical_and(jnp.logical_or(col < row, jnp.logical_not(diag)), exists)
    lk = jnp.where(mask, ls - z, 0.0)
    return ls, lk, mask


def _split_dot(v, tri):
    hi = v.astype(BF16)
    lo = (v - hi.astype(F32)).astype(BF16)
    return (jnp.dot(hi, tri, preferred_element_type=F32) + jnp.dot(lo, tri, preferred_element_type=F32))


SB_PAIR =2 * SB_HEAD_DIM


def _pair_masks(tb):
    lane = lax.broadcasted_iota(jnp.int32, (1, SB_PAIR), 1)
    row = lax.broadcasted_iota(jnp.int32, (tb, tb), 0)
    col = lax.broadcasted_iota(jnp.int32, (tb, tb), 1)
    return [lane < SB_HEAD_DIM, lane >= SB_HEAD_DIM], row, col


def _pair_more(qi, carry):
    j, crs = carry[0], carry[2]
    return jnp.logical_and(j <= qi, jnp.maximum(jnp.max(crs[0]), jnp.max(crs[1])) > SB_CUT)


def _pair_specs(bl, l, d, tb):
    nq, off = l // tb, d // SB_PAIR
    qspec = pl.BlockSpec((tb, SB_PAIR), lambda b, p, i: (b * nq + i, p))
    kspec = pl.BlockSpec((l, SB_PAIR), lambda b, p, i: (b, off + p))
    vspec = pl.BlockSpec((l, SB_PAIR), lambda b, p, i: (b, 2 * off + p))
    return qspec, kspec, vspec


def _sb_attn_fwd2(qkv, bl):
    t, d3 = qkv.shape
    d, l = d3 // 3, t // bl
    tb = min(SB_BLOCK, l)
    nq = l // tb

    def body(q_ref, k_ref, v_ref, o_ref, ob_ref):
        qi = pl.program_id(2)
        heads, row, col = _pair_masks(tb)
        qv = q_ref[...]
        qh = [jnp.where(m, qv, jnp.zeros_like(qv)) for m in heads]
        tri = (row > col).astype(BF16)

        def step(carry):
            j, acc, crs = carry
            crs = list(crs)
            where = [_sb_block(qi, j + u, tb) for u in range(SB_UNROLL)]
            kblks = [k_ref[pl.ds(ks, tb), :] for ks, _, _ in where]
            scores = [[_sb_scores(qh[hd], kblks[u], where[u][1], where[u][2], row, col) for hd in range(2)]
                      for u in range(SB_UNROLL)]
            laters = [[_split_dot(sc[1], tri) for sc in su] for su in scores]
            for u in range(SB_UNROLL):
                vblk = v_ref[pl.ds(where[u][0], tb), :]
                outs = []
                for hd in range(2):
                    ls, lk, mask = scores[u][hd]
                    att = jnp.where(mask, jnp.exp(ls + laters[u][hd] + crs[hd]), 0.0)
                    outs.append(jnp.dot(att.astype(BF16), vblk, preferred_element_type=F32))
                    crs[hd] = crs[hd] + jnp.sum(lk, axis=1, keepdims=True)
                acc = acc + jnp.where(heads[0], outs[0], outs[1])
            return j + SB_UNROLL, acc, tuple(crs)

        zc = jnp.zeros((tb, 1), F32)
        _, acc, _ = lax.while_loop(functools.partial(_pair_more, qi), step,
                                   (jnp.int32(0), jnp.zeros((tb, SB_PAIR), F32), (zc, zc)))
        o_ref[...] = acc
        ob_ref[...] = acc.astype(BF16)

    qspec, kspec, vspec = _pair_specs(bl, l, d, tb)
    return pl.pallas_call(
        body, name="sb_attn", grid=(bl, d // SB_PAIR, nq), in_specs=[qspec, kspec, vspec], out_specs=[qspec, qspec],
        out_shape=[jax.ShapeDtypeStruct((t, d), F32), jax.ShapeDtypeStruct((t, d), BF16)],
        compiler_params=_params("parallel", "parallel", "parallel"))(qkv, qkv, qkv)


def _sb_attn_bwd2(qkv, o, do, bl):
    t, d3 = qkv.shape
    d, l = d3 // 3, t // bl
    tb = min(SB_BLOCK, l)
    nq = l // tb
    scale = SB_HEAD_DIM ** -0.5

    def body(q_ref, k_ref, v_ref, o_ref, do_ref, dq_ref, dk_ref, dv_ref, dk_acc, dv_acc):
        qi = pl.program_id(2)

        @pl.when(qi == 0)
        def _():
            dk_acc[...] = jnp.zeros_like(dk_acc)
            dv_acc[...] = jnp.zeros_like(dv_acc)

        heads, row, col = _pair_masks(tb)
        qv = q_ref[...]
        dov = do_ref[...].astype(BF16)
        qh = [jnp.where(m, qv, jnp.zeros_like(qv)) for m in heads]
        doh = [jnp.where(m, dov, jnp.zeros_like(dov)) for m in heads]
        ov = o_ref[...]
        dsum = [jnp.sum(dh.astype(F32) * ov, axis=1, keepdims=True) for dh in doh]
        tri = (row > col).astype(BF16)
        tri_inc = (row >= col).astype(BF16)

        def step(carry):
            j, dq, crs, ces = carry
            crs, ces = list(crs), list(ces)
            n = range(SB_UNROLL)
            where = [_sb_block(qi, j + u, tb) for u in n]
            rows = [pl.ds(ks, tb) for ks, _, _ in where]
            kblks = [k_ref[rows[u], :] for u in n]
            vblks = [v_ref[rows[u], :] for u in n]
            scores = [[_sb_scores(qh[hd], kblks[u], where[u][1], where[u][2], row, col) for hd in range(2)] for u in n]
            laters = [[_split_dot(sc[1], tri) for sc in su] for su in scores]
            datts = [[lax.dot_general(doh[hd], vblks[u], NT, preferred_element_type=F32) for hd in range(2)] for u in n]
            atts = [[None, None] for _ in n]
            for u in n:
                for hd in range(2):
                    ls, lk, mask = scores[u][hd]
                    atts[u][hd] = jnp.where(mask, jnp.exp(ls + laters[u][hd] + crs[hd]), 0.0).astype(BF16)
                    crs[hd] = crs[hd] + jnp.sum(lk, axis=1, keepdims=True)
            es = [[atts[u][hd].astype(F32) * datts[u][hd] for hd in range(2)] for u in n]
            sufs = [[_split_dot(e, tri_inc) for e in eu] for eu in es]
            dzs = [[None, None] for _ in n]
            for u in n:
                for hd in range(2):
                    ls, _, mask = scores[u][hd]
                    pre = dsum[hd] - ces[hd] - sufs[u][hd]
                    sg = jnp.exp(ls)
                    dzs[u][hd] = (jnp.where(mask, es[u][hd] * (1.0 - sg) - pre * sg, 0.0) * scale).astype(BF16)
                    ces[hd] = ces[hd] + jnp.sum(es[u][hd], axis=1, keepdims=True)
            for u in n:
                dq = dq + jnp.where(heads[0], jnp.dot(dzs[u][0], kblks[u], preferred_element_type=F32),
                                    jnp.dot(dzs[u][1], kblks[u], preferred_element_type=F32))
                dk_acc[rows[u], :] += (lax.dot_general(dzs[u][0], qh[0], TN, preferred_element_type=F32)
                                       + lax.dot_general(dzs[u][1], qh[1], TN, preferred_element_type=F32))
                dv_acc[rows[u], :] += (lax.dot_general(atts[u][0], doh[0], TN, preferred_element_type=F32)
                                       + lax.dot_general(atts[u][1], doh[1], TN, preferred_element_type=F32))
            return j + SB_UNROLL, dq, tuple(crs), tuple(ces)

        zc = jnp.zeros((tb, 1), F32)
        _, dq, _, _ = lax.while_loop(functools.partial(_pair_more, qi), step,
                                     (jnp.int32(0), jnp.zeros((tb, SB_PAIR), F32), (zc, zc), (zc, zc)))
        dq_ref[...] = dq.astype(BF16)

        @pl.when(qi == nq - 1)
        def _():
            dk_ref[...] = dk_acc[...].astype(BF16)
            dv_ref[...] = dv_acc[...].astype(BF16)

    qspec, kspec, vspec = _pair_specs(bl, l, d, tb)
    blk = pl.BlockSpec((tb, SB_PAIR), lambda b, p, i: (b * nq + i, p))
    full = pl.BlockSpec((l, SB_PAIR), lambda b, p, i: (b, p))
    sd = jax.ShapeDtypeStruct((t, d), BF16)
    dq, dk, dv = pl.pallas_call(
        body, name="sb_attn_bwd", grid=(bl, d // SB_PAIR, nq), in_specs=[qspec, kspec, vspec, blk, blk],
        out_specs=[blk, full, full], out_shape=[sd, sd, sd],
        scratch_shapes=[pltpu.VMEM((l, SB_PAIR), F32), pltpu.VMEM((l, SB_PAIR), F32)],
        compiler_params=_params("parallel", "parallel", "arbitrary"))(qkv, qkv, qkv, o, do)
    return jnp.concatenate([dq, dk, dv], axis=1)


def _sb_fwd(h, g, w_qkv, w_o, bl):
    t, d = h.shape
    hn = _rmsnorm("mix_norm", h, g)
    qkv = _mm_cs("sb_qkv", hn, w_qkv, 0, "flat", BF16)
    o, ob = _sb_attn_fwd2(qkv, bl)
    out = _mm_rs("sb_out", ob, "flat", w_o, 0, res=h)
    return out, (h, hn, qkv, o, ob)


def _sb_bwd(dout, saved, g, w_qkv, w_o, bl):
    h, hn, qkv, o, ob = saved
    dob = dout.astype(BF16)
    dwo = _mm_dw("sb_dwo", ob, "flat", dob, None, (None, 0, 1))
    do = _mm_rs_dx("sb_out_dx", dob, w_o, 0, "flat", F32)
    dqkv = _sb_attn_bwd2(qkv, o, do, bl)
    dwqkv = _mm_dw("sb_dwqkv", hn, None, dqkv, "flat", (None, 0, 1))
    dhn = _mm_cs_dx("sb_qkv_dx", [(dqkv, w_qkv)], "flat", 0)
    dh, dg = _rmsnorm_bwd("mix_norm_bwd", dout, dhn, h, g)
    return dh, dg, dwqkv, dwo


def _adamw_update(wv, gr, mv, vv):
    c1 = 1.0 / (1.0 - ADAM_B1 ** ADAM_STEP)
    c2 = 1.0 / (1.0 - ADAM_B2 ** ADAM_STEP)
    mn = ADAM_B1 * mv + (1.0 - ADAM_B1) * gr
    vn = ADAM_B2 * vv + (1.0 - ADAM_B2) * gr * gr
    delta = -ADAM_LR * ((mn * c1) / (jnp.sqrt(vn * c2) + ADAM_EPS) + ADAM_WD * wv)
    return delta, mn, vn


def _adamw_small(w, gr, m, v):
    def fn(wv, gv, mv, vv):
        return list(_adamw_update(wv, gv, mv, vv)), []
    return _rows("adamw_small", fn, [w, gr, m, v], [(w.shape[1], F32)] * 3)[0]


def _place():
    x, y, c = lax.axis_index("x"), lax.axis_index("y"), lax.axis_index("c")
    chips = [(1 - x, y), (x, 1 - y), (1 - x, 1 - y)]
    return x, y, c, chips


def _remote(src, dst, send_sem, recv_sem, to):
    return pltpu.make_async_remote_copy(src_ref=src, dst_ref=dst, send_sem=send_sem, recv_sem=recv_sem,
                                        device_id=to, device_id_type=MESH)


def _half(ref, c, rh, lead):
    return ref.at[(slice(None),) * lead + (pl.ds(c * rh, rh),)]


def _allgather_weights(ws):
    n = len(ws)

    def body(*refs):
        ins, outs = refs[:n], refs[n:2 * n]
        send, recv = refs[2 * n:]
        x, y, c, _ = _place()
        chip_x, chip_y, chip_d = (1 - x, y), (x, 1 - y), (1 - x, 1 - y)
        sibling = (x, y, 1 - c)
        index = lambda chip: 2 * chip[0] + chip[1]
        sent = []

        def quarter(ref, half, q, rq):
            return ref.at[:, pl.ds((2 * half + q) * rq, rq)]

        def copy(t, kind, src, dst, to):
            return _remote(src, dst, send.at[t, kind], recv.at[t, kind], to)

        def start(cp):
            cp.start()
            sent.append(cp)

        for t in range(n):
            rq = ws[t].shape[1] // 4
            for q in range(2):
                for base, chip in ((0, chip_x), (2, chip_y)):
                    start(copy(t, base + q, quarter(ins[t], c, q, rq), quarter(outs[t].at[index((x, y))], c, q, rq), (*chip, c)))
        for t in range(n):
            rq = ws[t].shape[1] // 4
            landings = [(chip_x, 0, 0, chip_x, ((4, chip_y), (6, None))), (chip_y, 1, 3, chip_y, ((5, chip_x), (9, None))),
                        (chip_x, 1, 1, chip_x, ((7, None),)), (chip_y, 0, 2, chip_y, ((8, None),)),
                        (chip_d, 0, 4, chip_y, ((10, None),)), (chip_d, 1, 5, chip_x, ((11, None),))]
            for origin, q, kind, sender, onward in landings:
                piece = quarter(outs[t].at[index(origin)], c, q, rq)
                copy(t, kind, piece, piece, (*sender, c)).wait_recv()
                for kind2, chip in onward:
                    start(copy(t, kind2, piece, piece, sibling if chip is None else (*chip, c)))
        for t in range(n):
            rq = ws[t].shape[1] // 4
            for kind, (origin, q) in zip(range(6, 12), ((chip_x, 0), (chip_x, 1), (chip_y, 0), (chip_y, 1), (chip_d, 0), (chip_d, 1))):
                piece = quarter(outs[t].at[index(origin)], 1 - c, q, rq)
                copy(t, kind, piece, piece, sibling).wait_recv()
        for cp in sent:
            cp.wait_send()

    res = pl.pallas_call(
        body, name="allgather_weights", in_specs=[ANY] * n, out_specs=[ANY] * n,
        out_shape=[jax.ShapeDtypeStruct((N_CHIPS,) + w.shape, w.dtype) for w in ws],
        scratch_shapes=[pltpu.SemaphoreType.DMA((n, 12)), pltpu.SemaphoreType.DMA((n, 12))],
    )(*ws)
    own = 2 * lax.axis_index("x") + lax.axis_index("y")
    return [lax.dynamic_update_slice(g, w[None], (own, 0, 0, 0)) for g, w in zip(res, ws)]


def _pair_exchange(gs):
    n = len(gs)

    def body(*refs):
        ins, outs = refs[:n], refs[n:2 * n]
        send, recv = refs[2 * n:]
        x, y, c, _ = _place()
        copies = [_remote(_half(ins[t], 1 - c, gs[t].shape[2] // 2, 2), outs[t], send.at[t], recv.at[t], (x, y, 1 - c))
                  for t in range(n)]
        for cp in copies:
            cp.start()
        for cp in copies:
            cp.wait()

    return pl.pallas_call(
        body, name="grad_pair_exchange", in_specs=[ANY] * n, out_specs=[ANY] * n,
        out_shape=[jax.ShapeDtypeStruct(g.shape[:2] + (g.shape[2] // 2, g.shape[3]), F32) for g in gs],
        scratch_shapes=[pltpu.SemaphoreType.DMA((n,)), pltpu.SemaphoreType.DMA((n,))],
    )(*gs)


def _pair_sum(g, theirs, c_idx):
    n4, ly, r, cc = g.shape
    rh = r // 2
    tm = _tile(rh, 512)
    nt = rh // tm

    def body(c_ref, g_ref, t_ref, o_ref):
        o_ref[...] = (g_ref[...] + t_ref[...]).astype(o_ref.dtype)

    blk = (None, tm, cc)
    grid_spec = pltpu.PrefetchScalarGridSpec(
        num_scalar_prefetch=1, grid=(n4 * ly, nt),
        in_specs=[pl.BlockSpec(blk, lambda a, i, cr: (a, cr[0] * nt + i, 0)), pl.BlockSpec(blk, lambda a, i, cr: (a, i, 0))],
        out_specs=pl.BlockSpec(blk, lambda a, i, cr: (a, i, 0)))
    out = pl.pallas_call(
        body, name="grad_pair_sum", grid_spec=grid_spec, out_shape=jax.ShapeDtypeStruct((n4 * ly, rh, cc), BF16),
        compiler_params=_params("parallel", "parallel"))(c_idx, g.reshape(n4 * ly, r, cc), theirs.reshape(n4 * ly, rh, cc))
    return out.reshape(n4, ly, rh, cc)


def _quarter(ref, q, rq):
    return ref.at[:, pl.ds(q * rq, rq)]


def _chip_exchange_first(ps):
    n = len(ps)

    def body(*refs):
        ins, outs = refs[:n], refs[n:2 * n]
        send, recv = refs[2 * n:]
        x, y, c, _ = _place()
        index = lambda cx, cy: 2 * cx + cy
        copies = []
        for t in range(n):
            rq = ps[t].shape[2] // 2
            for base, q, chip in ((0, 0, (1 - x, y)), (2, 1, (x, 1 - y))):
                for j, slice_of in enumerate((chip, (1 - x, 1 - y))):
                    copies.append(_remote(_quarter(ins[t].at[index(*slice_of)], q, rq), outs[t].at[base + j],
                                          send.at[t, base + j], recv.at[t, base + j], (*chip, c)))
        for cp in copies:
            cp.start()
        for cp in copies:
            cp.wait()

    return pl.pallas_call(
        body, name="grad_chip_exchange", in_specs=[ANY] * n, out_specs=[ANY] * n,
        out_shape=[jax.ShapeDtypeStruct((4, p.shape[1], p.shape[2] // 2, p.shape[3]), p.dtype) for p in ps],
        scratch_shapes=[pltpu.SemaphoreType.DMA((n, 4)), pltpu.SemaphoreType.DMA((n, 4))],
    )(*ps)


def _chip_relay_sum(p, first, where):
    _, ly, rh, cc = p.shape
    rq = rh // 2
    tm = _tile(rq, 512)
    nt = rq // tm

    def body(w_ref, p_ref, f_ref, out_ref):
        out_ref[...] = (p_ref[...].astype(F32) + f_ref[...].astype(F32)).astype(out_ref.dtype)

    blk = (None, None, tm, cc)
    grid_spec = pltpu.PrefetchScalarGridSpec(
        num_scalar_prefetch=1, grid=(2, ly, nt),
        in_specs=[pl.BlockSpec(blk, lambda s, l, i, w: (w[2 - s], l, s * nt + i, 0)),
                  pl.BlockSpec(blk, lambda s, l, i, w: (1 + 2 * s, l, i, 0))],
        out_specs=pl.BlockSpec(blk, lambda s, l, i, w: (s, l, i, 0)))
    return pl.pallas_call(
        body, name="grad_relay_sum", grid_spec=grid_spec, out_shape=jax.ShapeDtypeStruct((2, ly, rq, cc), p.dtype),
        compiler_params=_params("parallel", "parallel", "parallel"))(where, p, first)


def _chip_exchange_second(ss):
    n = len(ss)

    def body(*refs):
        ins, outs = refs[:n], refs[n:2 * n]
        send, recv = refs[2 * n:]
        x, y, c, _ = _place()
        copies = []
        for t in range(n):
            for j, chip in enumerate(((x, 1 - y), (1 - x, y))):
                copies.append(_remote(ins[t].at[j], outs[t].at[j], send.at[t, j], recv.at[t, j], (*chip, c)))
        for cp in copies:
            cp.start()
        for cp in copies:
            cp.wait()

    return pl.pallas_call(
        body, name="grad_chip_exchange_2", in_specs=[ANY] * n, out_specs=[ANY] * n,
        out_shape=[jax.ShapeDtypeStruct(s.shape, s.dtype) for s in ss],
        scratch_shapes=[pltpu.SemaphoreType.DMA((n, 2)), pltpu.SemaphoreType.DMA((n, 2))],
    )(*ss)


def _chip_sum(p, first, second, where):
    _, ly, rh, cc = p.shape
    rq = rh // 2
    tm = _tile(rq, 512)
    nt = rq // tm

    def body(w_ref, p_ref, f_ref, s_ref, out_ref):
        out_ref[...] = (p_ref[...].astype(F32) + f_ref[...].astype(F32)) + s_ref[...].astype(F32)

    blk = (None, None, tm, cc)
    grid_spec = pltpu.PrefetchScalarGridSpec(
        num_scalar_prefetch=1, grid=(ly, 2, nt),
        in_specs=[pl.BlockSpec(blk, lambda l, q, i, w: (w[0], l, q * nt + i, 0)),
                  pl.BlockSpec(blk, lambda l, q, i, w: (2 * q, l, i, 0)),
                  pl.BlockSpec(blk, lambda l, q, i, w: (q, l, i, 0))],
        out_specs=pl.BlockSpec((None, tm, cc), lambda l, q, i, w: (l, q * nt + i, 0)))
    return pl.pallas_call(
        body, name="grad_chip_sum", grid_spec=grid_spec, out_shape=jax.ShapeDtypeStruct((ly, rh, cc), F32),
        compiler_params=_params("parallel", "parallel", "parallel"))(where, p, first, second)


def _pair_swap(halves):
    n = len(halves)

    def body(*refs):
        ins, outs = refs[:n], refs[n:2 * n]
        send, recv = refs[2 * n:]
        x, y, c, _ = _place()
        copies = [_remote(ins[t], outs[t], send.at[t], recv.at[t], (x, y, 1 - c)) for t in range(n)]
        for cp in copies:
            cp.start()
        for cp in copies:
            cp.wait()

    return pl.pallas_call(
        body, name="grad_pair_swap", in_specs=[ANY] * n, out_specs=[ANY] * n,
        out_shape=[jax.ShapeDtypeStruct(h.shape, F32) for h in halves],
        scratch_shapes=[pltpu.SemaphoreType.DMA((n,)), pltpu.SemaphoreType.DMA((n,))],
    )(*halves)


def _adamw_big(w, m, v, mine, theirs, c_idx):
    ly, r, cc = w.shape
    rh = r // 2
    tm = _tile(rh, 512)
    nt = rh // tm

    def body(c_ref, w_ref, m_ref, v_ref, a_ref, b_ref, g_out, d_out, m_out, v_out):
        gr = jnp.where(pl.program_id(1) == c_ref[0], a_ref[...], b_ref[...])
        delta, mn, vn = _adamw_update(w_ref[...], gr, m_ref[...], v_ref[...])
        g_out[...] = gr
        d_out[...] = delta
        m_out[...] = mn
        v_out[...] = vn

    blk = (None, tm, cc)
    full = pl.BlockSpec(blk, lambda l, hc, i, cr: (l, hc * nt + i, 0))
    half = pl.BlockSpec(blk, lambda l, hc, i, cr: (l, i, 0))
    grid_spec = pltpu.PrefetchScalarGridSpec(
        num_scalar_prefetch=1, grid=(ly, 2, nt), in_specs=[full, full, full, half, half], out_specs=[full] * 4)
    sd = jax.ShapeDtypeStruct(w.shape, F32)
    return pl.pallas_call(
        body, name="adamw", grid_spec=grid_spec, out_shape=[sd] * 4,
        compiler_params=_params("parallel", "parallel", "parallel"))(c_idx, w, m, v, mine, theirs)


def _allreduce_small(v):
    rows, w = v.shape

    def body(x_ref, sum_ref, all_ref, send, recv, local):
        x, y, c, chips = _place()
        me, sibling = (x, y, c), (x, y, 1 - c)

        def slot(px, py, pc):
            return all_ref.at[4 * px + 2 * py + pc]

        def copy(k, block, to, src=None):
            return _remote(slot(*block) if src is None else src, slot(*block), send.at[k], recv.at[k], to)

        mine = pltpu.make_async_copy(x_ref, slot(*me), local)
        mine.start()
        first = [copy(0, me, sibling, src=x_ref)]
        first += [copy(1 + j, me, (*chip, c), src=x_ref) for j, chip in enumerate(chips)]
        for cp in first:
            cp.start()
        passed = [copy(4 + j, (*chip, c), sibling) for j, chip in enumerate(chips)]
        for j, chip in enumerate(chips):
            copy(1 + j, (*chip, c), me).wait_recv()
            passed[j].start()
        copy(0, sibling, me).wait_recv()
        for j, chip in enumerate(chips):
            copy(4 + j, (*chip, 1 - c), me).wait_recv()
        for cp in first + passed:
            cp.wait_send()
        mine.wait()
        tot = all_ref[0]
        for k in range(1, N_DEV):
            tot = tot + all_ref[k]
        sum_ref[...] = tot

    vm = pl.BlockSpec(memory_space=pltpu.VMEM)
    return pl.pallas_call(
        body, name="allreduce_small", in_specs=[vm], out_specs=[vm, vm],
        out_shape=[jax.ShapeDtypeStruct((rows, w), F32), jax.ShapeDtypeStruct((N_DEV, rows, w), F32)],
        scratch_shapes=[pltpu.SemaphoreType.DMA((7,)), pltpu.SemaphoreType.DMA((7,)), pltpu.SemaphoreType.DMA],
        compiler_params=pltpu.CompilerParams(vmem_limit_bytes=VMEM_LIMIT),
    )(v)[0]


BIG = ["ffn1_w1", "ffn1_w3", "ffn1_w2", "ffn2_w1", "ffn2_w3", "ffn2_w2", "ple_proj", "ple_gate",
       "s5_w_in", "s5_w_glu", "sb_w_qkv", "sb_w_o"]
TRANSPOSED = ("ffn1_w1", "ffn1_w3", "ffn2_w1", "ffn2_w3")
SMALL = ["ffn1_norm", "mix_norm", "ffn2_norm", "ple_norm", "s5_a_re", "s5_a_im", "s5_log_dt", "s5_b_re", "s5_b_im",
         "s5_c_re", "s5_c_im", "s5_d", "final_norm"]
ORDER = ["ffn1_norm", "ffn1_w1", "ffn1_w3", "ffn1_w2", "mix_norm", "ffn2_norm", "ffn2_w1", "ffn2_w3", "ffn2_w2",
         "ple_norm", "ple_proj", "ple_gate", "s5_w_in", "s5_a_re", "s5_a_im", "s5_log_dt", "s5_b_re", "s5_b_im",
         "s5_c_re", "s5_c_im", "s5_d", "s5_w_glu", "sb_w_qkv", "sb_w_o", "final_norm"]


def _pack(arrays):
    flat = jnp.concatenate([a.reshape(-1) for a in arrays])
    pad = (-flat.shape[0]) % 1024
    return jnp.pad(flat, (0, pad)).reshape(-1, 128)


def _unpack(packed, like):
    flat = packed.reshape(-1)
    out, off = [], 0
    for a in like:
        out.append(flat[off:off + a.size].reshape(a.shape))
        off += a.size
    return out


def _fwd_bwd(x, p, target, w, gathered):
    bl, l, d = x.shape
    t = bl * l
    depth = w["ffn1_norm"].shape[0]
    s5_ops, s5_vjp = jax.vjp(_s5_prep, w["s5_a_re"][0], w["s5_a_im"][0], w["s5_log_dt"][0], w["s5_b_re"][0],
                             w["s5_b_im"][0], w["s5_c_re"][0], w["s5_c_im"][0])

    h = x.reshape(t, d)
    p2 = [p[i].reshape(t, p.shape[-1]).astype(BF16) for i in range(depth)]
    saved = []
    for i in range(depth):
        norm = lambda name: w[name][i:i + 1]
        h, s1 = _ffn_fwd(h, norm("ffn1_norm"), gathered["ffn1_w1"], gathered["ffn1_w3"], gathered["ffn1_w2"], i)
        if i % 2 == 0:
            h, s2 = _s5_fwd(h, norm("mix_norm"), s5_ops, w["s5_d"][i // 2:i // 2 + 1], gathered["s5_w_in"], gathered["s5_w_glu"], bl)
        else:
            h, s2 = _sb_fwd(h, norm("mix_norm"), gathered["sb_w_qkv"], gathered["sb_w_o"], bl)
        h, s3 = _ffn_fwd(h, norm("ffn2_norm"), gathered["ffn2_w1"], gathered["ffn2_w3"], gathered["ffn2_w2"], i)
        h, s4 = _ple_fwd(h, norm("ple_norm"), p2[i], gathered["ple_proj"], gathered["ple_gate"], i)
        saved.append((s1, s2, s3, s4))

    loss, dh, dfinal = _head(h, w["final_norm"].reshape(1, d), target.reshape(t, d))

    big = {k: None for k in BIG}
    small = {k: [None] * w[k].shape[0] if w[k].ndim > 1 else None for k in SMALL}
    small["final_norm"] = dfinal.reshape(d)
    for i in reversed(range(depth)):
        norm = lambda name: w[name][i:i + 1]
        slots = lambda *names: [(big[k], i, depth) for k in names]
        s1, s2, s3, s4 = saved[i]
        dh, dg, big["ple_proj"], big["ple_gate"] = _ple_bwd(
            dh, s4, norm("ple_norm"), p2[i], gathered["ple_proj"], gathered["ple_gate"], i, slots("ple_proj", "ple_gate"))
        small["ple_norm"][i] = dg[0]
        dh, dg, big["ffn2_w1"], big["ffn2_w3"], big["ffn2_w2"] = _ffn_bwd(
            dh, s3, norm("ffn2_norm"), gathered["ffn2_w1"], gathered["ffn2_w3"], gathered["ffn2_w2"], i,
            slots("ffn2_w1", "ffn2_w3", "ffn2_w2"))
        small["ffn2_norm"][i] = dg[0]
        if i % 2 == 0:
            dh, dg, big["s5_w_in"], big["s5_w_glu"], dd, dops = _s5_bwd(
                dh, s2, norm("mix_norm"), s5_ops, w["s5_d"][i // 2:i // 2 + 1], gathered["s5_w_in"], gathered["s5_w_glu"], bl)
            small["s5_d"][0] = dd[0]
            raw = s5_vjp(dops)
            for name, gr in zip(["s5_a_re", "s5_a_im", "s5_log_dt", "s5_b_re", "s5_b_im", "s5_c_re", "s5_c_im"], raw):
                small[name][0] = gr
        else:
            dh, dg, big["sb_w_qkv"], big["sb_w_o"] = _sb_bwd(dh, s2, norm("mix_norm"), gathered["sb_w_qkv"], gathered["sb_w_o"], bl)
        small["mix_norm"][i] = dg[0]
        dh, dg, big["ffn1_w1"], big["ffn1_w3"], big["ffn1_w2"] = _ffn_bwd(
            dh, s1, norm("ffn1_norm"), gathered["ffn1_w1"], gathered["ffn1_w3"], gathered["ffn1_w2"], i,
            slots("ffn1_w1", "ffn1_w3", "ffn1_w2"))
        small["ffn1_norm"][i] = dg[0]
    small_list = [jnp.stack(small[k]) if isinstance(small[k], list) else small[k] for k in SMALL]
    return loss, dh.reshape(bl, l, d), big, small_list


def _step(x, p, target, w, m, v):
    flip = lambda tree: {k: jnp.swapaxes(a, 1, 2) if k in TRANSPOSED else a for k, a in tree.items()}
    w, m, v = flip(w), flip(m), flip(v)
    gathered = dict(zip(BIG, _allgather_weights([_to_bf16(w[k]) for k in BIG])))
    loss, grad_x, big, small_list = _fwd_bwd(x, p, target, w, gathered)

    c_idx = lax.axis_index("c").astype(jnp.int32).reshape(1)
    cx, cy = lax.axis_index("x"), lax.axis_index("y")
    where = jnp.stack([2 * cx + cy, 2 * (1 - cx) + cy, 2 * cx + (1 - cy)]).astype(jnp.int32)
    partial = [big[k] for k in BIG]
    pair = [_pair_sum(g, t, c_idx) for g, t in zip(partial, _pair_exchange(partial))]
    first = _chip_exchange_first(pair)
    second = _chip_exchange_second([_chip_relay_sum(pr, f, where) for pr, f in zip(pair, first)])
    mine = [_chip_sum(pr, f, s, where) for pr, f, s in zip(pair, first, second)]
    theirs = _pair_swap(mine)
    out_g, out_d, out_m, out_v = {}, {}, {}, {}
    for k, a, b in zip(BIG, mine, theirs):
        out_g[k], out_d[k], out_m[k], out_v[k] = _adamw_big(w[k], m[k], v[k], a, b, c_idx)

    like = [w[k] for k in SMALL]
    pad = [jnp.zeros((1,), F32)]
    g_small = _allreduce_small(_pack(small_list + [loss.reshape(1)]))
    packed = (g_small,) + tuple(_adamw_small(_pack(like + pad), g_small, _pack([m[k] for k in SMALL] + pad),
                                             _pack([v[k] for k in SMALL] + pad)))
    for dst, pk in zip((out_g, out_d, out_m, out_v), packed):
        dst.update(dict(zip(SMALL, _unpack(pk, like))))
    loss = g_small.reshape(-1)[sum(a.size for a in like)]
    out_g, out_d, out_m, out_v = flip(out_g), flip(out_d), flip(out_m), flip(out_v)
    return (loss, grad_x, *[out_g[k] for k in ORDER], *[out_d[k] for k in ORDER],
            *[out_m[k] for k in ORDER], *[out_v[k] for k in ORDER])


def kernel(x, p, ffn1_norm, ffn1_w1, ffn1_w3, ffn1_w2, mix_norm, ffn2_norm, ffn2_w1, ffn2_w3, ffn2_w2, ple_norm, ple_proj, ple_gate, s5_w_in, s5_a_re, s5_a_im, s5_log_dt, s5_b_re, s5_b_im, s5_c_re, s5_c_im, s5_d, s5_w_glu, sb_w_qkv, sb_w_o, final_norm, loss_target, m_ffn1_norm, m_ffn1_w1, m_ffn1_w3, m_ffn1_w2, m_mix_norm, m_ffn2_norm, m_ffn2_w1, m_ffn2_w3, m_ffn2_w2, m_ple_norm, m_ple_proj, m_ple_gate, m_s5_w_in, m_s5_a_re, m_s5_a_im, m_s5_log_dt, m_s5_b_re, m_s5_b_im, m_s5_c_re, m_s5_c_im, m_s5_d, m_s5_w_glu, m_sb_w_qkv, m_sb_w_o, m_final_norm, v_ffn1_norm, v_ffn1_w1, v_ffn1_w3, v_ffn1_w2, v_mix_norm, v_ffn2_norm, v_ffn2_w1, v_ffn2_w3, v_ffn2_w2, v_ple_norm, v_ple_proj, v_ple_gate, v_s5_w_in, v_s5_a_re, v_s5_a_im, v_s5_log_dt, v_s5_b_re, v_s5_b_im, v_s5_c_re, v_s5_c_im, v_s5_d, v_s5_w_glu, v_sb_w_qkv, v_sb_w_o, v_final_norm):
    args = dict(locals())
    w = {k: args[k] for k in ORDER}
    m = {k: args["m_" + k] for k in ORDER}
    v = {k: args["v_" + k] for k in ORDER}
    return _step(x, p, loss_target, w, m, v)
```

```python
import functools
import math

import jax
import jax.numpy as jnp
from jax import lax
from jax.experimental import pallas as pl
from jax.experimental.pallas import tpu as pltpu

F32 = jnp.float32
BF16 = jnp.bfloat16
MESH = pl.DeviceIdType.MESH

N_CHIPS = 4
N_DEV = 8
RMS_EPS = 1e-6
S5_GROUP = 16
S5_STATE = 64
S5_CHUNK = 16
SB_HEAD_DIM = 64
SB_BLOCK = 128
SB_CUT = -104.0
SB_UNROLL = 3
ADAM_LR, ADAM_B1, ADAM_B2, ADAM_EPS, ADAM_WD, ADAM_STEP = 0.001, 0.9, 0.999, 1e-08, 0.01, 10
VMEM_LIMIT = 48 * 1024 * 1024

NN = (((1,), (0,)), ((), ()))
NT = (((1,), (1,)), ((), ()))
TN = (((0,), (0,)), ((), ()))

ANY = pl.BlockSpec(memory_space=pl.ANY)


def _tile(n, target):
    if n <= target:
        return n
    for t in range(target - target % 8, 7, -8):
        if n % t == 0:
            return t
    raise ValueError(f"no row tile for {n}")


def _params(*semantics):
    return pltpu.CompilerParams(dimension_semantics=semantics, vmem_limit_bytes=VMEM_LIMIT)


def _sigmoid(v):
    return 1.0 / (1.0 + jnp.exp(-v))


def _gemm(name, grid, operands, in_specs, groups, acc_shapes, out_shapes, out_specs, epilogue, reduce_axis=None, aliases=None):
    n_in, n_out = len(operands), len(out_shapes)
    n_red = None if reduce_axis is None else grid[reduce_axis]

    def body(*refs):
        ins, outs, accs = refs[:n_in], refs[n_in:n_in + n_out], refs[n_in + n_out:]

        def products():
            res = []
            for terms in groups:
                tot = None
                for ia, ib, dims in terms:
                    d = lax.dot_general(ins[ia][...], ins[ib][...], dims, preferred_element_type=F32)
                    tot = d if tot is None else tot + d
                res.append(tot)
            return res

        def finish(vals):
            for o, v in zip(outs, epilogue(vals, ins)):
                o[...] = v.astype(o.dtype)

        if reduce_axis is None:
            finish(products())
        else:
            k = pl.program_id(reduce_axis)

            @pl.when(k == 0)
            def _():
                for a in accs:
                    a[...] = jnp.zeros_like(a)

            for a, d in zip(accs, products()):
                a[...] += d

            @pl.when(k == n_red - 1)
            def _():
                finish([a[...] for a in accs])

    scratch = [] if reduce_axis is None else [pltpu.VMEM(s, F32) for s in acc_shapes]
    sem = tuple("arbitrary" if i == reduce_axis else "parallel" for i in range(len(grid)))
    return pl.pallas_call(
        body, name=name, grid=grid, in_specs=in_specs, out_specs=out_specs, out_shape=out_shapes,
        scratch_shapes=scratch, input_output_aliases=aliases or {}, compiler_params=_params(*sem))(*operands)


def _ident(vals, ins):
    return vals


def _act_spec(layout, tm, cs, pos):
    if layout == "sm":
        return pl.BlockSpec((None, tm, cs), lambda *g: (pos(*g)[1], pos(*g)[0], 0))
    return pl.BlockSpec((tm, cs), lambda *g: pos(*g))


def _act_shape(layout, t, cs, dtype):
    return jax.ShapeDtypeStruct((N_CHIPS, t, cs) if layout == "sm" else (t, N_CHIPS * cs), dtype)


def _w_spec(w, layer, pos_k):
    _, _, r, c = w.shape
    return pl.BlockSpec((None, None, r, c), lambda *g: (pos_k(*g), layer, 0, 0))


def _mm_cs(name, x, w, layer, out_layout, out_dtype, tm=1024):
    t, kd = x.shape
    cs = w.shape[3]
    tm = _tile(t, tm)
    return _gemm(
        name, (N_CHIPS, t // tm), [x, w],
        [pl.BlockSpec((tm, kd), lambda k, i: (i, 0)), _w_spec(w, layer, lambda k, i: k)],
        [[(0, 1, NN)]], None, [_act_shape(out_layout, t, cs, out_dtype)],
        [_act_spec(out_layout, tm, cs, lambda k, i: (i, k))], _ident)[0]


def _mm_rs(name, xs, layout, w, layer, res=None, alpha=1.0, out_dtype=F32, tm=1024):
    ks, n = w.shape[2], w.shape[3]
    t = xs.shape[1] if layout == "sm" else xs.shape[0]
    tm = _tile(t, tm)
    operands = [xs, w] + ([] if res is None else [res])
    specs = [_act_spec(layout, tm, ks, lambda i, k: (i, k)), _w_spec(w, layer, lambda i, k: k)]
    if res is not None:
        specs.append(pl.BlockSpec((tm, n), lambda i, k: (i, 0)))

    def epilogue(vals, ins):
        y = alpha * vals[0]
        return [y if res is None else ins[2][...] + y]

    return _gemm(
        name, (t // tm, N_CHIPS), operands, specs, [[(0, 1, NN)]], [(tm, n)],
        [jax.ShapeDtypeStruct((t, n), out_dtype)], [pl.BlockSpec((tm, n), lambda i, k: (i, 0))],
        epilogue, reduce_axis=1)[0]


def _mm_cs_dx(name, pairs, layout, layer, tm=1024, transposed=False, norm=None):
    w0 = pairs[0][1]
    kd, cs = (w0.shape[3], w0.shape[2]) if transposed else (w0.shape[2], w0.shape[3])
    dy0 = pairs[0][0]
    t = dy0.shape[1] if layout == "sm" else dy0.shape[0]
    tm = _tile(t, tm if norm is None else tm // 2)
    operands, specs, terms = [], [], []
    for dy, w in pairs:
        terms.append((len(operands), len(operands) + 1, NN if transposed else NT))
        operands += [dy, w]
        specs += [_act_spec(layout, tm, cs, lambda i, k: (i, k)), _w_spec(w, layer, lambda i, k: k)]
    row = pl.BlockSpec((tm, kd), lambda i, k: (i, 0))
    if norm is None:
        return _gemm(name, (t // tm, N_CHIPS), operands, specs, [terms], [(tm, kd)],
                     [jax.ShapeDtypeStruct((t, kd), F32)], [row], _ident, reduce_axis=1)[0]
    base = len(operands)
    operands += list(norm)
    specs += [row, row, pl.BlockSpec(norm[2].shape, lambda i, k: (0, 0))]

    def epilogue(vals, ins):
        dx, dg = _rms_bwd_math(vals[0], ins[base + 1][...], ins[base + 2][...])
        dh = ins[base][...] + dx
        return [dh, dh, dg]

    dh, dhb, dg = _gemm(
        name, (t // tm, N_CHIPS), operands, specs, [terms], [(tm, kd)],
        [jax.ShapeDtypeStruct((t, kd), F32), jax.ShapeDtypeStruct((t, kd), BF16), jax.ShapeDtypeStruct((t // tm, 1, kd), F32)],
        [row, row, pl.BlockSpec((None, 1, kd), lambda i, k: (i, 0, 0))], epilogue, reduce_axis=1)
    return dh, dhb, dg.sum(axis=0)


def _mm_rs_dx(name, dy, w, layer, out_layout, out_dtype, tm=1024):
    t, n = dy.shape
    ks = w.shape[2]
    tm = _tile(t, tm)
    return _gemm(
        name, (N_CHIPS, t // tm), [dy, w],
        [pl.BlockSpec((tm, n), lambda k, i: (i, 0)), _w_spec(w, layer, lambda k, i: k)],
        [[(0, 1, NT)]], None, [_act_shape(out_layout, t, ks, out_dtype)],
        [_act_spec(out_layout, tm, ks, lambda k, i: (i, k))], _ident)[0]


def _mm_dw(name, x, x_layout, dy, dy_layout, slot, alpha=1.0, tk=2048):
    stack, layer, layers = slot
    if x_layout is None:
        t, rows = x.shape
        cols = dy.shape[2] if dy_layout == "sm" else dy.shape[1] // N_CHIPS
        tk = _tile(t, tk)
        xspec = pl.BlockSpec((tk, rows), lambda k, j: (j, 0))
        yspec = _act_spec(dy_layout, tk, cols, lambda k, j: (j, k))
    else:
        t, cols = dy.shape
        rows = x.shape[2] if x_layout == "sm" else x.shape[1] // N_CHIPS
        tk = _tile(t, tk)
        xspec = _act_spec(x_layout, tk, rows, lambda k, j: (j, k))
        yspec = pl.BlockSpec((tk, cols), lambda k, j: (j, 0))
    operands, specs = [x, dy], [xspec, yspec]
    if stack is not None:
        operands.append(stack)
        specs.append(ANY)
    return _gemm(
        name, (N_CHIPS, t // tk), operands, specs, [[(0, 1, TN)]], [(rows, cols)],
        [jax.ShapeDtypeStruct((N_CHIPS, layers, rows, cols), F32)],
        [pl.BlockSpec((None, None, rows, cols), lambda k, j: (k, layer, 0, 0))],
        lambda vals, ins: [alpha * vals[0]], reduce_axis=1, aliases=None if stack is None else {2: 0})[0]


def _rows(name, fn, ins, outs, accs=(), tm=512):
    t = ins[0].shape[0]
    tm = _tile(t, tm)
    n_in, n_out, n_acc = len(ins), len(outs), len(accs)
    in_specs = []
    for a in ins:
        if a.shape[0] == t:
            in_specs.append(pl.BlockSpec((tm, a.shape[1]), lambda i: (i, 0)))
        else:
            in_specs.append(pl.BlockSpec(a.shape, lambda i: (0, 0)))
    out_shape = [jax.ShapeDtypeStruct((t, c), d) for c, d in outs] + [jax.ShapeDtypeStruct(s, F32) for s in accs]
    out_specs = [pl.BlockSpec((tm, c), lambda i: (i, 0)) for c, _ in outs] + [pl.BlockSpec(s, lambda i: (0, 0)) for s in accs]

    def body(*refs):
        i = pl.program_id(0)
        row_vals, acc_vals = fn(*[r[...] for r in refs[:n_in]])
        for o, v in zip(refs[n_in:n_in + n_out], row_vals):
            o[...] = v.astype(o.dtype)
        acc_refs = refs[n_in + n_out:]
        if n_acc:
            @pl.when(i == 0)
            def _():
                for a in acc_refs:
                    a[...] = jnp.zeros_like(a)

            for a, v in zip(acc_refs, acc_vals):
                a[...] += v

    res = pl.pallas_call(
        body, name=name, grid=(t // tm,), in_specs=in_specs, out_specs=out_specs, out_shape=out_shape,
        compiler_params=_params("arbitrary" if n_acc else "parallel"))(*ins)
    return res[:n_out], res[n_out:]


def _to_bf16(a):
    def fn(x):
        return [x], []
    return _rows("weights_bf16", fn, [a.reshape(-1, a.shape[-1])], [(a.shape[-1], BF16)], tm=512)[0][0].reshape(a.shape)


def _rms_stats(x):
    return lax.rsqrt(jnp.mean(x * x, axis=-1, keepdims=True) + RMS_EPS)


def _rmsnorm(name, h, g):
    def fn(x, gg):
        return [x * _rms_stats(x) * gg], []
    return _rows(name, fn, [h, g], [(h.shape[1], BF16)])[0][0]


def _rms_bwd_math(dn, x, g):
    r = _rms_stats(x)
    xhat = x * r
    dxh = dn * g
    dx = r * (dxh - xhat * jnp.mean(dxh * xhat, axis=-1, keepdims=True))
    return dx, jnp.sum(dn * xhat, axis=0, keepdims=True)


def _rmsnorm_bwd(name, dres, dn, h, g):
    def fn(dr, d, x, gg):
        dx, dg = _rms_bwd_math(d, x, gg)
        return [dr + dx, dr + dx], [dg]
    (dh, dhb), (dg,) = _rows(name, fn, [dres, dn, h, g], [(h.shape[1], F32), (h.shape[1], BF16)], [(1, h.shape[1])])
    return dh, dhb, dg


def _ffn_fwd(h, g, w1, w3, w2, layer, tm=1024):
    t, d = h.shape
    fs = w1.shape[2]
    n = _rmsnorm("ffn_norm", h, g)
    tm = _tile(t, tm)

    def up(vals, ins):
        a, b = vals
        sg = _sigmoid(a)
        silu = a * sg
        return [b * sg * (1.0 + a * (1.0 - sg)), silu, silu * b]

    sm = _act_shape("sm", t, fs, BF16)
    osp = _act_spec("sm", tm, fs, lambda k, i: (i, k))
    ga, gb, s = _gemm(
        "ffn_up", (N_CHIPS, t // tm), [n, w1, w3],
        [pl.BlockSpec((tm, d), lambda k, i: (i, 0)), _w_spec(w1, layer, lambda k, i: k), _w_spec(w3, layer, lambda k, i: k)],
        [[(0, 1, NT)], [(0, 2, NT)]], None, [sm, sm, sm], [osp, osp, osp], up)
    out = _mm_rs("ffn_down", s, "sm", w2, layer, res=h, alpha=0.5)
    return out, (h, n, ga, gb, s)


def _ffn_bwd(dout, dob, saved, g, w1, w3, w2, layer, slots, tm=1024):
    h, n, ga, gb, s = saved
    t, d = h.shape
    fs = w1.shape[2]
    tm = _tile(t, tm)

    def down(vals, ins):
        ds = 0.5 * vals[0]
        return [ds * ins[2][...].astype(F32), ds * ins[3][...].astype(F32)]

    sm = _act_shape("sm", t, fs, BF16)
    asp = _act_spec("sm", tm, fs, lambda k, i: (i, k))
    da, db = _gemm(
        "ffn_down_dx", (N_CHIPS, t // tm), [dob, w2, ga, gb],
        [pl.BlockSpec((tm, d), lambda k, i: (i, 0)), _w_spec(w2, layer, lambda k, i: k), asp, asp],
        [[(0, 1, NT)]], None, [sm, sm], [asp, asp], down)
    dw2 = _mm_dw("ffn_dw2", s, "sm", dob, None, slots[2], alpha=0.5)
    dw1 = _mm_dw("ffn_dw1", da, "sm", n, None, slots[0])
    dw3 = _mm_dw("ffn_dw3", db, "sm", n, None, slots[1])
    dh, dhb, dg = _mm_cs_dx("ffn_up_dx", [(da, w1), (db, w3)], "sm", layer, transposed=True, norm=(dout, h, g))
    return dh, dhb, dg, dw1, dw3, dw2


def _ple_fwd(h, g, p2, wproj, wgate, layer):
    n = _rmsnorm("ple_norm", h, g)
    gl = _mm_rs("ple_gate", n, "flat", wgate, layer)
    pp = _mm_cs("ple_proj", p2, wproj, layer, "flat", F32)

    def fn(hh, gg, q):
        return [hh + q * _sigmoid(gg)], []
    out = _rows("ple_mix", fn, [h, gl, pp], [(h.shape[1], F32)])[0][0]
    return out, (h, n, gl, pp)


def _ple_bwd(dout, saved, g, p2, wproj, wgate, layer, slots):
    h, n, gl, pp = saved
    d = h.shape[1]

    def fn(do, gg, q):
        sg = _sigmoid(gg)
        return [do * sg, do * q * sg * (1.0 - sg)], []
    (dpp, dgl), _ = _rows("ple_mix_bwd", fn, [dout, gl, pp], [(d, BF16), (d, BF16)])
    dwproj = _mm_dw("ple_dwproj", p2, None, dpp, "flat", slots[0])
    dwgate = _mm_dw("ple_dwgate", n, "flat", dgl, None, slots[1])
    dn = _mm_rs_dx("ple_gate_dx", dgl, wgate, layer, "flat", F32)
    dh, dhb, dg = _rmsnorm_bwd("ple_norm_bwd", dout, dn, h, g)
    return dh, dhb, dg, dwproj, dwgate


def _head(h, g, target):
    d = h.shape[1]

    def fn(x, gg, tg):
        y = x * _rms_stats(x) * gg
        err = y - tg
        dy = err * (1.0 / d)
        dx, dg = _rms_bwd_math(dy, x, gg)
        loss = 0.5 * jnp.sum(jnp.sum(err * err, axis=-1, keepdims=True) * (1.0 / d), axis=0, keepdims=True)
        return [dx], [dg, jnp.broadcast_to(loss, (1, 128))]
    (dh,), (dg, loss) = _rows("loss_head", fn, [h, g, target], [(d, F32)], [(1, d), (1, 128)])
    return loss[0, 0], dh, dg


S5_LANES = 2 * S5_STATE
S5_GB = 128 // S5_GROUP


def _s5_prep(a_re, a_im, log_dt, b_re, b_im, c_re, c_im):
    c, gb = S5_CHUNK, S5_GB
    g = a_re.shape[0]
    nb = g // gb
    lam_re = jnp.minimum(a_re, -1e-4)
    lam_im = a_im
    dt = jnp.exp(log_dt)[:, None, None]
    ks = jnp.arange(c + 1, dtype=F32)
    mag = jnp.exp(lam_re[..., None] * dt * ks)
    ph = lam_im[..., None] * dt * ks
    pw_re, pw_im = mag * jnp.cos(ph), mag * jnp.sin(ph)
    den = lam_re * lam_re + lam_im * lam_im
    nr, ni = pw_re[..., 1] - 1.0, pw_im[..., 1]
    fr = (nr * lam_re + ni * lam_im) / den
    fi = (ni * lam_re - nr * lam_im) / den
    bb_re = fr[..., None] * b_re - fi[..., None] * b_im
    bb_im = fr[..., None] * b_im + fi[..., None] * b_re
    ct_re, ct_im = c_re.transpose(0, 2, 1), c_im.transpose(0, 2, 1)
    ca_re = ct_re[:, :, None, :] * pw_re[..., None] - ct_im[:, :, None, :] * pw_im[..., None]
    ca_im = ct_re[:, :, None, :] * pw_im[..., None] + ct_im[:, :, None, :] * pw_re[..., None]
    hp = lax.Precision.HIGHEST
    kern = (jnp.einsum("gpj,gpkh->gkjh", bb_re, ca_re[:, :, :c], precision=hp)
            - jnp.einsum("gpj,gpkh->gkjh", bb_im, ca_im[:, :, :c], precision=hp))
    rev_re = pw_re[:, :, :c][:, :, ::-1].transpose(0, 2, 1)
    rev_im = pw_im[:, :, :c][:, :, ::-1].transpose(0, 2, 1)
    bt_re, bt_im = bb_re.transpose(0, 2, 1), bb_im.transpose(0, 2, 1)
    wn_re = rev_re[:, :, None, :] * bt_re[:, None] - rev_im[:, :, None, :] * bt_im[:, None]
    wn_im = rev_re[:, :, None, :] * bt_im[:, None] + rev_im[:, :, None, :] * bt_re[:, None]
    wn = jnp.concatenate([wn_re, wn_im], axis=-1)
    wo = jnp.concatenate([ca_re[:, :, 1:].transpose(0, 2, 3, 1), -ca_im[:, :, 1:].transpose(0, 2, 3, 1)], axis=-1)

    def blocks(x):
        return x.reshape(nb, gb, c, S5_GROUP, x.shape[3]).transpose(0, 2, 1, 3, 4).reshape(nb, c, gb * S5_GROUP, x.shape[3])

    ar, ai = pw_re[..., c], pw_im[..., c]
    return (jnp.tile(blocks(kern), (1, 1, 1, gb)), blocks(wn), blocks(wo),
            jnp.concatenate([ar, ar], axis=1), jnp.concatenate([-ai, ai], axis=1))


def _step_rows(ref, tau, n):
    return ref[pl.ds(tau, n, stride=S5_CHUNK), :].astype(BF16)


def _cat_groups(ref, dtype):
    return jnp.concatenate([ref[:, j, :] for j in range(S5_GB)], axis=1).astype(dtype)


def _cat_steps(ref, n):
    return jnp.concatenate([_step_rows(ref, tau, n) for tau in range(S5_CHUNK)], axis=1)


def _stack_steps(ref, n):
    return jnp.concatenate([_step_rows(ref, tau, n) for tau in range(S5_CHUNK)], axis=0)


def _cat_ops(ref, axis, reverse=False):
    order = range(S5_CHUNK - 1, -1, -1) if reverse else range(S5_CHUNK)
    return jnp.concatenate([ref[k] for k in order], axis=axis)


def _row_group(rows, lanes):
    row = (lax.broadcasted_iota(jnp.int32, (rows, lanes), 0) // S5_GROUP) % S5_GB
    lane = (lax.broadcasted_iota(jnp.int32, (rows, lanes), 1) // S5_GROUP) % S5_GB
    return row, lane


def _own_group(x):
    row, lane = _row_group(*x.shape)
    return jnp.where(row == lane, x, jnp.zeros_like(x))


def _spread(x):
    row, _ = _row_group(*x.shape)
    return jnp.concatenate([jnp.where(row == j, x, jnp.zeros_like(x)) for j in range(S5_GB)], axis=1)


def _gather_own(x):
    row, _ = _row_group(x.shape[0], S5_LANES)
    out = jnp.zeros((x.shape[0], S5_LANES), x.dtype)
    for j in range(S5_GB):
        out = out + jnp.where(row == j, x[:, j * S5_LANES:(j + 1) * S5_LANES], 0.0)
    return out


def _s5_specs(t, d):
    nct, g = t // S5_CHUNK, d // S5_GROUP
    tok = pl.BlockSpec((t, 128), lambda i: (0, i))
    st = pl.BlockSpec((nct, S5_GB, S5_LANES), lambda i: (0, i, 0))
    op = lambda w: pl.BlockSpec((None,) + w.shape[1:], lambda i: (i, 0, 0, 0))
    return nct, g, tok, st, op


def _s5_chunk_fwd(u, bd, bn):
    t, d = u.shape
    nct, g, tok, st, op = _s5_specs(t, d)
    c = S5_CHUNK

    def body(u_ref, bd_ref, bn_ref, y_ref, s_ref):
        ucat = _cat_steps(u_ref, nct)
        sloc = jnp.dot(ucat, _spread(_cat_ops(bn_ref, 0)), preferred_element_type=F32)
        for j in range(S5_GB):
            s_ref[:, j, :] = sloc[:, j * S5_LANES:(j + 1) * S5_LANES]
        lags = _own_group(_cat_ops(bd_ref, 0, reverse=True))
        for tt in range(c):
            y_ref[pl.ds(tt, nct, stride=c), :] = jnp.dot(ucat[:, :(tt + 1) * 128], lags[(c - 1 - tt) * 128:, :],
                                                         preferred_element_type=F32)

    return pl.pallas_call(
        body, name="s5_chunk", grid=(d // 128,), in_specs=[tok, op(bd), op(bn)], out_specs=[tok, st],
        out_shape=[jax.ShapeDtypeStruct((t, d), F32), jax.ShapeDtypeStruct((nct, g, S5_LANES), F32)],
        compiler_params=_params("parallel"))(u, bd, bn)


def _s5_state_out(sprev, co, yin):
    t, d = yin.shape
    nct, g, tok, st, op = _s5_specs(t, d)
    c = S5_CHUNK

    def body(s_ref, co_ref, yi_ref, y_ref):
        ys = lax.dot_general(_cat_groups(s_ref, BF16), _spread(_cat_ops(co_ref, 0)), NT,
                             preferred_element_type=F32)
        for tt in range(c):
            rows = pl.ds(tt, nct, stride=c)
            y_ref[rows, :] = yi_ref[rows, :] + ys[:, tt * 128:(tt + 1) * 128]

    return pl.pallas_call(
        body, name="s5_state_out", grid=(d // 128,), in_specs=[st, op(co), tok], out_specs=tok,
        out_shape=jax.ShapeDtypeStruct((t, d), F32), compiler_params=_params("parallel"))(sprev, co, yin)


def _s5_state_out_dx(dyb, co):
    t, d = dyb.shape
    nct, g, tok, st, op = _s5_specs(t, d)
    c = S5_CHUNK

    def body(dy_ref, co_ref, ds_ref):
        acc = jnp.dot(_cat_steps(dy_ref, nct), _spread(_cat_ops(co_ref, 0)), preferred_element_type=F32)
        for j in range(S5_GB):
            ds_ref[:, j, :] = acc[:, j * S5_LANES:(j + 1) * S5_LANES]

    return pl.pallas_call(
        body, name="s5_state_out_dx", grid=(d // 128,), in_specs=[tok, op(co)], out_specs=st,
        out_shape=jax.ShapeDtypeStruct((nct, g, S5_LANES), F32), compiler_params=_params("parallel"))(dyb, co)


def _s5_chunk_dx(dyb, dsloc, bd, bn, skip):
    t, d = dyb.shape
    nct, g, tok, st, op = _s5_specs(t, d)
    c = S5_CHUNK

    def body(dy_ref, ds_ref, bd_ref, bn_ref, sk_ref, du_ref):
        dus = lax.dot_general(_cat_groups(ds_ref, BF16), _spread(_cat_ops(bn_ref, 0)), NT, preferred_element_type=F32)
        dycat = _cat_steps(dy_ref, nct)
        lags = _own_group(_cat_ops(bd_ref, 1))
        for tau in range(c):
            rows = pl.ds(tau, nct, stride=c)
            du_ref[rows, :] = (sk_ref[rows, :] + dus[:, tau * 128:(tau + 1) * 128]
                               + lax.dot_general(dycat[:, tau * 128:], lags[:, :(c - tau) * 128], NT,
                                                 preferred_element_type=F32))

    return pl.pallas_call(
        body, name="s5_chunk_dx", grid=(d // 128,), in_specs=[tok, st, op(bd), op(bn), tok], out_specs=tok,
        out_shape=jax.ShapeDtypeStruct((t, d), F32), compiler_params=_params("parallel"))(dyb, dsloc, bd, bn, skip)


def _s5_chunk_dw(u, dyb, dsloc, bd, bn):
    t, d = u.shape
    nct, g, tok, st, op = _s5_specs(t, d)
    c = S5_CHUNK

    def body(u_ref, dy_ref, ds_ref, dbd_ref, dbn_ref):
        dbn = _gather_own(lax.dot_general(_cat_steps(u_ref, nct), _cat_groups(ds_ref, BF16), TN,
                                          preferred_element_type=F32))
        for tau in range(c):
            dbn_ref[tau] = dbn[tau * 128:(tau + 1) * 128, :]
        ustk, dystk = _stack_steps(u_ref, nct), _stack_steps(dy_ref, nct)
        for k in range(c):
            dbd_ref[k] = _own_group(lax.dot_general(ustk[:(c - k) * nct], dystk[k * nct:], TN,
                                                    preferred_element_type=F32))

    return pl.pallas_call(
        body, name="s5_chunk_dw", grid=(d // 128,), in_specs=[tok, tok, st], out_specs=[op(bd), op(bn)],
        out_shape=[jax.ShapeDtypeStruct(bd.shape, F32), jax.ShapeDtypeStruct(bn.shape, F32)],
        compiler_params=_params("parallel"))(u, dyb, dsloc)


def _s5_state_out_dw(sprev, dyb, co):
    t, d = dyb.shape
    nct, g, tok, st, op = _s5_specs(t, d)
    c = S5_CHUNK

    def body(s_ref, dy_ref, dco_ref):
        dco = _gather_own(lax.dot_general(_cat_steps(dy_ref, nct), _cat_groups(s_ref, BF16), TN,
                                          preferred_element_type=F32))
        for tt in range(c):
            dco_ref[tt] = dco[tt * 128:(tt + 1) * 128, :]

    return pl.pallas_call(
        body, name="s5_state_out_dw", grid=(d // 128,), in_specs=[st, tok], out_specs=op(co),
        out_shape=jax.ShapeDtypeStruct(co.shape, F32), compiler_params=_params("parallel"))(sprev, dyb)


def _s5_scan_fwd(sloc, m1, m2):
    bl, nc, g, w = sloc.shape

    def body(s_ref, m1_ref, m2_ref, o_ref):
        a1, a2 = m1_ref[...], m2_ref[...]

        def step(c, states):
            new = []
            for b, s in enumerate(states):
                o_ref[b, c] = s
                new.append(a1 * s + a2 * pltpu.roll(s, S5_STATE, 1) + s_ref[b, c])
            return tuple(new)
        lax.fori_loop(0, nc, step, tuple(jnp.zeros((g, w), F32) for _ in range(bl)))

    vm = pl.BlockSpec(memory_space=pltpu.VMEM)
    return pl.pallas_call(
        body, name="s5_scan", in_specs=[vm, vm, vm], out_specs=vm,
        out_shape=jax.ShapeDtypeStruct(sloc.shape, F32),
        compiler_params=pltpu.CompilerParams(vmem_limit_bytes=VMEM_LIMIT))(sloc, m1, m2)


def _s5_scan_bwd(dsprev, sprev, m1, m2):
    bl, nc, g, w = dsprev.shape

    def body(d_ref, s_ref, m1_ref, m2_ref, g_ref, p1_ref, p2_ref):
        a1, a2 = m1_ref[...], m2_ref[...]
        zero = jnp.zeros((g, w), F32)

        def step(i, carry):
            gps, p1, p2 = carry
            c = nc - 2 - i
            new = []
            for b, gp in enumerate(gps):
                g_ref[b, c] = gp
                sp = s_ref[b, c]
                p1 = p1 + gp * sp
                p2 = p2 + gp * pltpu.roll(sp, S5_STATE, 1)
                new.append(d_ref[b, c] + a1 * gp - a2 * pltpu.roll(gp, S5_STATE, 1))
            return tuple(new), p1, p2

        for b in range(bl):
            g_ref[b, nc - 1] = zero
        _, p1, p2 = lax.fori_loop(0, nc - 1, step, (tuple(d_ref[b, nc - 1] for b in range(bl)), zero, zero))
        p1_ref[...] = p1
        p2_ref[...] = p2

    vm = pl.BlockSpec(memory_space=pltpu.VMEM)
    sd = jax.ShapeDtypeStruct
    return pl.pallas_call(
        body, name="s5_scan_bwd", in_specs=[vm, vm, vm, vm], out_specs=[vm, vm, vm],
        out_shape=[sd(dsprev.shape, F32), sd((g, w), F32), sd((g, w), F32)],
        compiler_params=pltpu.CompilerParams(vmem_limit_bytes=VMEM_LIMIT))(dsprev, sprev, m1, m2)


def _gelu_tanh_parts(y):
    c0 = math.sqrt(2.0 / math.pi)
    inner = c0 * (y + 0.044715 * y * y * y)
    th = jnp.tanh(inner)
    return th, c0 * (1.0 + 3 * 0.044715 * y * y)


def _s5_fwd(h, g, ops, d_skip, w_in, w_glu, bl):
    bd, bn, co, m1, m2 = ops
    t, d = h.shape
    nct, groups = t // S5_CHUNK, d // S5_GROUP
    hn = _rmsnorm("mix_norm", h, g)
    u = _mm_rs("s5_in", hn, "flat", w_in, 0)
    yin, sloc = _s5_chunk_fwd(u, bd.astype(BF16), bn.astype(BF16))
    sprev = _s5_scan_fwd(sloc.reshape(bl, nct // bl, groups, S5_LANES), m1, m2).reshape(nct, groups, S5_LANES)
    y = _s5_state_out(sprev, co.astype(BF16), yin)

    def fn(yy, uu, dd):
        y2 = yy + dd * uu
        th, _ = _gelu_tanh_parts(y2)
        return [0.5 * y2 * (1.0 + th)], []
    z = _rows("s5_gelu", fn, [y, u, d_skip], [(d, BF16)])[0][0]
    zz = _mm_cs("s5_glu", z, w_glu, 0, "flat", F32)

    def glu(hh, zv):
        return [hh + zv[:, :d] * _sigmoid(zv[:, d:])], []
    out = _rows("s5_glu_mix", glu, [h, zz], [(d, F32)])[0][0]
    return out, (h, hn, u, sprev, y, z, zz)


def _s5_bwd(dout, saved, g, ops, d_skip, w_in, w_glu, bl):
    h, hn, u, sprev, y, z, zz = saved
    bd, bn, co, m1, m2 = ops
    t, d = h.shape
    nct, groups = t // S5_CHUNK, d // S5_GROUP

    def glu_bwd(do, zv):
        sg = _sigmoid(zv[:, d:])
        return [jnp.concatenate([do * sg, do * zv[:, :d] * sg * (1.0 - sg)], axis=1)], []
    dzz = _rows("s5_glu_bwd", glu_bwd, [dout, zz], [(2 * d, BF16)])[0][0]
    dwglu = _mm_dw("s5_dwglu", z, None, dzz, "flat", (None, 0, 1))
    dz = _mm_cs_dx("s5_glu_dx", [(dzz, w_glu)], "flat", 0)

    def gelu_bwd(dzv, yy, uu, dd):
        y2 = yy + dd * uu
        th, dinner = _gelu_tanh_parts(y2)
        dy2 = dzv * (0.5 * (1.0 + th) + 0.5 * y2 * (1.0 - th * th) * dinner)
        return [dy2, dy2 * dd], [jnp.sum(dy2 * uu, axis=0, keepdims=True)]
    (dyb, du_skip), (dd,) = _rows("s5_gelu_bwd", gelu_bwd, [dz, y, u, d_skip], [(d, F32), (d, F32)], [(1, d)])
    bd_b, bn_b, co_b = bd.astype(BF16), bn.astype(BF16), co.astype(BF16)
    dsprev = _s5_state_out_dx(dyb, co_b)
    shape4 = (bl, nct // bl, groups, S5_LANES)
    dsloc, dm1, dm2 = _s5_scan_bwd(dsprev.reshape(shape4), sprev.reshape(shape4), m1, m2)
    dsloc = dsloc.reshape(nct, groups, S5_LANES)
    du = _s5_chunk_dx(dyb, dsloc, bd_b, bn_b, du_skip).astype(BF16)
    dbd, dbn = _s5_chunk_dw(u, dyb, dsloc, bd, bn)
    dco = _s5_state_out_dw(sprev, dyb, co)
    dwin = _mm_dw("s5_dwin", hn, "flat", du, None, (None, 0, 1))
    dhn = _mm_rs_dx("s5_in_dx", du, w_in, 0, "flat", F32)
    dh, dhb, dg = _rmsnorm_bwd("mix_norm_bwd", dout, dhn, h, g)
    return dh, dhb, dg, dwin, dwglu, dd, (dbd, dbn, dco, dm1, dm2)


def _sb_block(qi, idx, tb):
    kb = qi - idx
    return pl.multiple_of(jnp.maximum(kb, 0) * tb, tb), idx == 0, kb >= 0


def _sb_scores(q, kblk, diag, exists, row, col):
    z = lax.dot_general(q, kblk, NT, preferred_element_type=F32) * (SB_HEAD_DIM ** -0.5)
    l1 = jnp.log(1.0 + jnp.exp(-jnp.abs(z)))
    ls = jnp.minimum(z, 0.0) - l1
    mask = jnp.logical_and(jnp.logical_or(col < row, jnp.logical_not(diag)), exists)
    lk = jnp.where(mask, ls - z, 0.0)
    return ls, lk, mask


def _split_dot(v, tri):
    hi = v.astype(BF16)
    lo = (v - hi.astype(F32)).astype(BF16)
    return (jnp.dot(hi, tri, preferred_element_type=F32) + jnp.dot(lo, tri, preferred_element_type=F32))


SB_PAIR =2 * SB_HEAD_DIM


def _pair_masks(tb):
    lane = lax.broadcasted_iota(jnp.int32, (1, SB_PAIR), 1)
    row = lax.broadcasted_iota(jnp.int32, (tb, tb), 0)
    col = lax.broadcasted_iota(jnp.int32, (tb, tb), 1)
    return [lane < SB_HEAD_DIM, lane >= SB_HEAD_DIM], row, col


def _pair_more(qi, carry):
    j, crs = carry[0], carry[2]
    return jnp.logical_and(j <= qi, jnp.maximum(jnp.max(crs[0]), jnp.max(crs[1])) > SB_CUT)


def _pair_specs(bl, l, d, tb):
    nq, off = l // tb, d // SB_PAIR
    qspec = pl.BlockSpec((tb, SB_PAIR), lambda b, p, i: (b * nq + i, p))
    kspec = pl.BlockSpec((l, SB_PAIR), lambda b, p, i: (b, off + p))
    vspec = pl.BlockSpec((l, SB_PAIR), lambda b, p, i: (b, 2 * off + p))
    return qspec, kspec, vspec


def _sb_attn_fwd2(qkv, bl):
    t, d3 = qkv.shape
    d, l = d3 // 3, t // bl
    tb = min(SB_BLOCK, l)
    nq = l // tb

    def body(q_ref, k_ref, v_ref, o_ref, ob_ref):
        qi = pl.program_id(2)
        heads, row, col = _pair_masks(tb)
        qv = q_ref[...]
        qh = [jnp.where(m, qv, jnp.zeros_like(qv)) for m in heads]
        tri = (row > col).astype(BF16)

        def step(carry):
            j, acc, crs = carry
            crs = list(crs)
            where = [_sb_block(qi, j + u, tb) for u in range(SB_UNROLL)]
            kblks = [k_ref[pl.ds(ks, tb), :] for ks, _, _ in where]
            scores = [[_sb_scores(qh[hd], kblks[u], where[u][1], where[u][2], row, col) for hd in range(2)]
                      for u in range(SB_UNROLL)]
            laters = [[_split_dot(sc[1], tri) for sc in su] for su in scores]
            for u in range(SB_UNROLL):
                vblk = v_ref[pl.ds(where[u][0], tb), :]
                outs = []
                for hd in range(2):
                    ls, lk, mask = scores[u][hd]
                    att = jnp.where(mask, jnp.exp(ls + laters[u][hd] + crs[hd]), 0.0)
                    outs.append(jnp.dot(att.astype(BF16), vblk, preferred_element_type=F32))
                    crs[hd] = crs[hd] + jnp.sum(lk, axis=1, keepdims=True)
                acc = acc + jnp.where(heads[0], outs[0], outs[1])
            return j + SB_UNROLL, acc, tuple(crs)

        zc = jnp.zeros((tb, 1), F32)
        _, acc, _ = lax.while_loop(functools.partial(_pair_more, qi), step,
                                   (jnp.int32(0), jnp.zeros((tb, SB_PAIR), F32), (zc, zc)))
        o_ref[...] = acc
        ob_ref[...] = acc.astype(BF16)

    qspec, kspec, vspec = _pair_specs(bl, l, d, tb)
    return pl.pallas_call(
        body, name="sb_attn", grid=(bl, d // SB_PAIR, nq), in_specs=[qspec, kspec, vspec], out_specs=[qspec, qspec],
        out_shape=[jax.ShapeDtypeStruct((t, d), F32), jax.ShapeDtypeStruct((t, d), BF16)],
        compiler_params=_params("parallel", "parallel", "parallel"))(qkv, qkv, qkv)


def _sb_attn_bwd2(qkv, o, do, bl):
    t, d3 = qkv.shape
    d, l = d3 // 3, t // bl
    tb = min(SB_BLOCK, l)
    nq = l // tb
    scale = SB_HEAD_DIM ** -0.5

    def body(q_ref, k_ref, v_ref, o_ref, do_ref, dq_ref, dk_ref, dv_ref, dk_acc, dv_acc):
        qi = pl.program_id(2)

        @pl.when(qi == 0)
        def _():
            dk_acc[...] = jnp.zeros_like(dk_acc)
            dv_acc[...] = jnp.zeros_like(dv_acc)

        heads, row, col = _pair_masks(tb)
        qv = q_ref[...]
        dov = do_ref[...].astype(BF16)
        qh = [jnp.where(m, qv, jnp.zeros_like(qv)) for m in heads]
        doh = [jnp.where(m, dov, jnp.zeros_like(dov)) for m in heads]
        ov = o_ref[...]
        dsum = [jnp.sum(dh.astype(F32) * ov, axis=1, keepdims=True) for dh in doh]
        tri = (row > col).astype(BF16)
        tri_inc = (row >= col).astype(BF16)

        def step(carry):
            j, dq, crs, ces = carry
            crs, ces = list(crs), list(ces)
            n = range(SB_UNROLL)
            where = [_sb_block(qi, j + u, tb) for u in n]
            rows = [pl.ds(ks, tb) for ks, _, _ in where]
            kblks = [k_ref[rows[u], :] for u in n]
            vblks = [v_ref[rows[u], :] for u in n]
            scores = [[_sb_scores(qh[hd], kblks[u], where[u][1], where[u][2], row, col) for hd in range(2)] for u in n]
            laters = [[_split_dot(sc[1], tri) for sc in su] for su in scores]
            datts = [[lax.dot_general(doh[hd], vblks[u], NT, preferred_element_type=F32) for hd in range(2)] for u in n]
            atts = [[None, None] for _ in n]
            for u in n:
                for hd in range(2):
                    ls, lk, mask = scores[u][hd]
                    atts[u][hd] = jnp.where(mask, jnp.exp(ls + laters[u][hd] + crs[hd]), 0.0).astype(BF16)
                    crs[hd] = crs[hd] + jnp.sum(lk, axis=1, keepdims=True)
            es = [[atts[u][hd].astype(F32) * datts[u][hd] for hd in range(2)] for u in n]
            sufs = [[_split_dot(e, tri_inc) for e in eu] for eu in es]
            dzs = [[None, None] for _ in n]
            for u in n:
                for hd in range(2):
                    ls, _, mask = scores[u][hd]
                    pre = dsum[hd] - ces[hd] - sufs[u][hd]
                    sg = jnp.exp(ls)
                    dzs[u][hd] = (jnp.where(mask, es[u][hd] * (1.0 - sg) - pre * sg, 0.0) * scale).astype(BF16)
                    ces[hd] = ces[hd] + jnp.sum(es[u][hd], axis=1, keepdims=True)
            for u in n:
                dq = dq + jnp.where(heads[0], jnp.dot(dzs[u][0], kblks[u], preferred_element_type=F32),
                                    jnp.dot(dzs[u][1], kblks[u], preferred_element_type=F32))
                dk_acc[rows[u], :] += (lax.dot_general(dzs[u][0], qh[0], TN, preferred_element_type=F32)
                                       + lax.dot_general(dzs[u][1], qh[1], TN, preferred_element_type=F32))
                dv_acc[rows[u], :] += (lax.dot_general(atts[u][0], doh[0], TN, preferred_element_type=F32)
                                       + lax.dot_general(atts[u][1], doh[1], TN, preferred_element_type=F32))
            return j + SB_UNROLL, dq, tuple(crs), tuple(ces)

        zc = jnp.zeros((tb, 1), F32)
        _, dq, _, _ = lax.while_loop(functools.partial(_pair_more, qi), step,
                                     (jnp.int32(0), jnp.zeros((tb, SB_PAIR), F32), (zc, zc), (zc, zc)))
        dq_ref[...] = dq.astype(BF16)

        @pl.when(qi == nq - 1)
        def _():
            dk_ref[...] = dk_acc[...].astype(BF16)
            dv_ref[...] = dv_acc[...].astype(BF16)

    qspec, kspec, vspec = _pair_specs(bl, l, d, tb)
    blk = pl.BlockSpec((tb, SB_PAIR), lambda b, p, i: (b * nq + i, p))
    full = pl.BlockSpec((l, SB_PAIR), lambda b, p, i: (b, p))
    sd = jax.ShapeDtypeStruct((t, d), BF16)
    dq, dk, dv = pl.pallas_call(
        body, name="sb_attn_bwd", grid=(bl, d // SB_PAIR, nq), in_specs=[qspec, kspec, vspec, blk, blk],
        out_specs=[blk, full, full], out_shape=[sd, sd, sd],
        scratch_shapes=[pltpu.VMEM((l, SB_PAIR), F32), pltpu.VMEM((l, SB_PAIR), F32)],
        compiler_params=_params("parallel", "parallel", "arbitrary"))(qkv, qkv, qkv, o, do)
    return jnp.concatenate([dq, dk, dv], axis=1)


def _sb_fwd(h, g, w_qkv, w_o, bl):
    t, d = h.shape
    hn = _rmsnorm("mix_norm", h, g)
    qkv = _mm_cs("sb_qkv", hn, w_qkv, 0, "flat", BF16)
    o, ob = _sb_attn_fwd2(qkv, bl)
    out = _mm_rs("sb_out", ob, "flat", w_o, 0, res=h)
    return out, (h, hn, qkv, o, ob)


def _sb_bwd(dout, dob, saved, g, w_qkv, w_o, bl):
    h, hn, qkv, o, ob = saved
    dwo = _mm_dw("sb_dwo", ob, "flat", dob, None, (None, 0, 1))
    do = _mm_rs_dx("sb_out_dx", dob, w_o, 0, "flat", F32)
    dqkv = _sb_attn_bwd2(qkv, o, do, bl)
    dwqkv = _mm_dw("sb_dwqkv", hn, None, dqkv, "flat", (None, 0, 1))
    dh, dhb, dg = _mm_cs_dx("sb_qkv_dx", [(dqkv, w_qkv)], "flat", 0, norm=(dout, h, g))
    return dh, dhb, dg, dwqkv, dwo


def _adamw_update(wv, gr, mv, vv):
    c1 = 1.0 / (1.0 - ADAM_B1 ** ADAM_STEP)
    c2 = 1.0 / (1.0 - ADAM_B2 ** ADAM_STEP)
    mn = ADAM_B1 * mv + (1.0 - ADAM_B1) * gr
    vn = ADAM_B2 * vv + (1.0 - ADAM_B2) * gr * gr
    delta = -ADAM_LR * ((mn * c1) / (jnp.sqrt(vn * c2) + ADAM_EPS) + ADAM_WD * wv)
    return delta, mn, vn


def _adamw_small(w, gr, m, v):
    def fn(wv, gv, mv, vv):
        return list(_adamw_update(wv, gv, mv, vv)), []
    return _rows("adamw_small", fn, [w, gr, m, v], [(w.shape[1], F32)] * 3)[0]


def _place():
    x, y, c = lax.axis_index("x"), lax.axis_index("y"), lax.axis_index("c")
    chips = [(1 - x, y), (x, 1 - y), (1 - x, 1 - y)]
    return x, y, c, chips


def _remote(src, dst, send_sem, recv_sem, to):
    return pltpu.make_async_remote_copy(src_ref=src, dst_ref=dst, send_sem=send_sem, recv_sem=recv_sem,
                                        device_id=to, device_id_type=MESH)


def _half(ref, c, rh, lead):
    return ref.at[(slice(None),) * lead + (pl.ds(c * rh, rh),)]


def _allgather_weights(ws):
    n = len(ws)

    def body(*refs):
        ins, outs = refs[:n], refs[n:2 * n]
        send, recv = refs[2 * n:]
        x, y, c, _ = _place()
        chip_x, chip_y, chip_d = (1 - x, y), (x, 1 - y), (1 - x, 1 - y)
        sibling = (x, y, 1 - c)
        index = lambda chip: 2 * chip[0] + chip[1]
        sent = []

        def quarter(ref, half, q, rq):
            return ref.at[:, pl.ds((2 * half + q) * rq, rq)]

        def copy(t, kind, src, dst, to):
            return _remote(src, dst, send.at[t, kind], recv.at[t, kind], to)

        def start(cp):
            cp.start()
            sent.append(cp)

        for t in range(n):
            rq = ws[t].shape[1] // 4
            for q in range(2):
                for base, chip in ((0, chip_x), (2, chip_y)):
                    start(copy(t, base + q, quarter(ins[t], c, q, rq), quarter(outs[t].at[index((x, y))], c, q, rq), (*chip, c)))
        for t in range(n):
            rq = ws[t].shape[1] // 4
            landings = [(chip_x, 0, 0, chip_x, ((4, chip_y), (6, None))), (chip_y, 1, 3, chip_y, ((5, chip_x), (9, None))),
                        (chip_x, 1, 1, chip_x, ((7, None),)), (chip_y, 0, 2, chip_y, ((8, None),)),
                        (chip_d, 0, 4, chip_y, ((10, None),)), (chip_d, 1, 5, chip_x, ((11, None),))]
            for origin, q, kind, sender, onward in landings:
                piece = quarter(outs[t].at[index(origin)], c, q, rq)
                copy(t, kind, piece, piece, (*sender, c)).wait_recv()
                for kind2, chip in onward:
                    start(copy(t, kind2, piece, piece, sibling if chip is None else (*chip, c)))
        for t in range(n):
            rq = ws[t].shape[1] // 4
            for kind, (origin, q) in zip(range(6, 12), ((chip_x, 0), (chip_x, 1), (chip_y, 0), (chip_y, 1), (chip_d, 0), (chip_d, 1))):
                piece = quarter(outs[t].at[index(origin)], 1 - c, q, rq)
                copy(t, kind, piece, piece, sibling).wait_recv()
        for cp in sent:
            cp.wait_send()

    res = pl.pallas_call(
        body, name="allgather_weights", in_specs=[ANY] * n, out_specs=[ANY] * n,
        out_shape=[jax.ShapeDtypeStruct((N_CHIPS,) + w.shape, w.dtype) for w in ws],
        scratch_shapes=[pltpu.SemaphoreType.DMA((n, 12)), pltpu.SemaphoreType.DMA((n, 12))],
    )(*ws)
    own = 2 * lax.axis_index("x") + lax.axis_index("y")
    return [lax.dynamic_update_slice(g, w[None], (own, 0, 0, 0)) for g, w in zip(res, ws)]


def _pair_exchange(gs):
    n = len(gs)

    def body(*refs):
        ins, outs = refs[:n], refs[n:2 * n]
        send, recv = refs[2 * n:]
        x, y, c, _ = _place()
        copies = [_remote(_half(ins[t], 1 - c, gs[t].shape[2] // 2, 2), outs[t], send.at[t], recv.at[t], (x, y, 1 - c))
                  for t in range(n)]
        for cp in copies:
            cp.start()
        for cp in copies:
            cp.wait()

    return pl.pallas_call(
        body, name="grad_pair_exchange", in_specs=[ANY] * n, out_specs=[ANY] * n,
        out_shape=[jax.ShapeDtypeStruct(g.shape[:2] + (g.shape[2] // 2, g.shape[3]), F32) for g in gs],
        scratch_shapes=[pltpu.SemaphoreType.DMA((n,)), pltpu.SemaphoreType.DMA((n,))],
    )(*gs)


def _pair_sum(g, theirs, c_idx):
    n4, ly, r, cc = g.shape
    rh = r // 2
    tm = _tile(rh, 512)
    nt = rh // tm

    def body(c_ref, g_ref, t_ref, o_ref):
        o_ref[...] = (g_ref[...] + t_ref[...]).astype(o_ref.dtype)

    blk = (None, tm, cc)
    grid_spec = pltpu.PrefetchScalarGridSpec(
        num_scalar_prefetch=1, grid=(n4 * ly, nt),
        in_specs=[pl.BlockSpec(blk, lambda a, i, cr: (a, cr[0] * nt + i, 0)), pl.BlockSpec(blk, lambda a, i, cr: (a, i, 0))],
        out_specs=pl.BlockSpec(blk, lambda a, i, cr: (a, i, 0)))
    out = pl.pallas_call(
        body, name="grad_pair_sum", grid_spec=grid_spec, out_shape=jax.ShapeDtypeStruct((n4 * ly, rh, cc), BF16),
        compiler_params=_params("parallel", "parallel"))(c_idx, g.reshape(n4 * ly, r, cc), theirs.reshape(n4 * ly, rh, cc))
    return out.reshape(n4, ly, rh, cc)


def _quarter(ref, q, rq):
    return ref.at[:, pl.ds(q * rq, rq)]


def _chip_exchange_first(ps):
    n = len(ps)

    def body(*refs):
        ins, outs = refs[:n], refs[n:2 * n]
        send, recv = refs[2 * n:]
        x, y, c, _ = _place()
        index = lambda cx, cy: 2 * cx + cy
        copies = []
        for t in range(n):
            rq = ps[t].shape[2] // 2
            for base, q, chip in ((0, 0, (1 - x, y)), (2, 1, (x, 1 - y))):
                for j, slice_of in enumerate((chip, (1 - x, 1 - y))):
                    copies.append(_remote(_quarter(ins[t].at[index(*slice_of)], q, rq), outs[t].at[base + j],
                                          send.at[t, base + j], recv.at[t, base + j], (*chip, c)))
        for cp in copies:
            cp.start()
        for cp in copies:
            cp.wait()

    return pl.pallas_call(
        body, name="grad_chip_exchange", in_specs=[ANY] * n, out_specs=[ANY] * n,
        out_shape=[jax.ShapeDtypeStruct((4, p.shape[1], p.shape[2] // 2, p.shape[3]), p.dtype) for p in ps],
        scratch_shapes=[pltpu.SemaphoreType.DMA((n, 4)), pltpu.SemaphoreType.DMA((n, 4))],
    )(*ps)


def _chip_relay_sum(p, first, where):
    _, ly, rh, cc = p.shape
    rq = rh // 2
    tm = _tile(rq, 512)
    nt = rq // tm

    def body(w_ref, p_ref, f_ref, out_ref):
        out_ref[...] = (p_ref[...].astype(F32) + f_ref[...].astype(F32)).astype(out_ref.dtype)

    blk = (None, None, tm, cc)
    grid_spec = pltpu.PrefetchScalarGridSpec(
        num_scalar_prefetch=1, grid=(2, ly, nt),
        in_specs=[pl.BlockSpec(blk, lambda s, l, i, w: (w[2 - s], l, s * nt + i, 0)),
                  pl.BlockSpec(blk, lambda s, l, i, w: (1 + 2 * s, l, i, 0))],
        out_specs=pl.BlockSpec(blk, lambda s, l, i, w: (s, l, i, 0)))
    return pl.pallas_call(
        body, name="grad_relay_sum", grid_spec=grid_spec, out_shape=jax.ShapeDtypeStruct((2, ly, rq, cc), p.dtype),
        compiler_params=_params("parallel", "parallel", "parallel"))(where, p, first)


def _chip_exchange_second(ss):
    n = len(ss)

    def body(*refs):
        ins, outs = refs[:n], refs[n:2 * n]
        send, recv = refs[2 * n:]
        x, y, c, _ = _place()
        copies = []
        for t in range(n):
            for j, chip in enumerate(((x, 1 - y), (1 - x, y))):
                copies.append(_remote(ins[t].at[j], outs[t].at[j], send.at[t, j], recv.at[t, j], (*chip, c)))
        for cp in copies:
            cp.start()
        for cp in copies:
            cp.wait()

    return pl.pallas_call(
        body, name="grad_chip_exchange_2", in_specs=[ANY] * n, out_specs=[ANY] * n,
        out_shape=[jax.ShapeDtypeStruct(s.shape, s.dtype) for s in ss],
        scratch_shapes=[pltpu.SemaphoreType.DMA((n, 2)), pltpu.SemaphoreType.DMA((n, 2))],
    )(*ss)


def _chip_sum(p, first, second, where):
    _, ly, rh, cc = p.shape
    rq = rh // 2
    tm = _tile(rq, 512)
    nt = rq // tm

    def body(w_ref, p_ref, f_ref, s_ref, out_ref):
        out_ref[...] = (p_ref[...].astype(F32) + f_ref[...].astype(F32)) + s_ref[...].astype(F32)

    blk = (None, None, tm, cc)
    grid_spec = pltpu.PrefetchScalarGridSpec(
        num_scalar_prefetch=1, grid=(ly, 2, nt),
        in_specs=[pl.BlockSpec(blk, lambda l, q, i, w: (w[0], l, q * nt + i, 0)),
                  pl.BlockSpec(blk, lambda l, q, i, w: (2 * q, l, i, 0)),
                  pl.BlockSpec(blk, lambda l, q, i, w: (q, l, i, 0))],
        out_specs=pl.BlockSpec((None, tm, cc), lambda l, q, i, w: (l, q * nt + i, 0)))
    return pl.pallas_call(
        body, name="grad_chip_sum", grid_spec=grid_spec, out_shape=jax.ShapeDtypeStruct((ly, rh, cc), F32),
        compiler_params=_params("parallel", "parallel", "parallel"))(where, p, first, second)


def _pair_swap(halves):
    n = len(halves)

    def body(*refs):
        ins, outs = refs[:n], refs[n:2 * n]
        send, recv = refs[2 * n:]
        x, y, c, _ = _place()
        copies = [_remote(ins[t], outs[t], send.at[t], recv.at[t], (x, y, 1 - c)) for t in range(n)]
        for cp in copies:
            cp.start()
        for cp in copies:
            cp.wait()

    return pl.pallas_call(
        body, name="grad_pair_swap", in_specs=[ANY] * n, out_specs=[ANY] * n,
        out_shape=[jax.ShapeDtypeStruct(h.shape, F32) for h in halves],
        scratch_shapes=[pltpu.SemaphoreType.DMA((n,)), pltpu.SemaphoreType.DMA((n,))],
    )(*halves)


def _adamw_big(w, m, v, mine, theirs, c_idx):
    ly, r, cc = w.shape
    rh = r // 2
    tm = _tile(rh, 512)
    nt = rh // tm

    def body(c_ref, w_ref, m_ref, v_ref, a_ref, b_ref, g_out, d_out, m_out, v_out):
        gr = jnp.where(pl.program_id(1) == c_ref[0], a_ref[...], b_ref[...])
        delta, mn, vn = _adamw_update(w_ref[...], gr, m_ref[...], v_ref[...])
        g_out[...] = gr
        d_out[...] = delta
        m_out[...] = mn
        v_out[...] = vn

    blk = (None, tm, cc)
    full = pl.BlockSpec(blk, lambda l, hc, i, cr: (l, hc * nt + i, 0))
    half = pl.BlockSpec(blk, lambda l, hc, i, cr: (l, i, 0))
    grid_spec = pltpu.PrefetchScalarGridSpec(
        num_scalar_prefetch=1, grid=(ly, 2, nt), in_specs=[full, full, full, half, half], out_specs=[full] * 4)
    sd = jax.ShapeDtypeStruct(w.shape, F32)
    return pl.pallas_call(
        body, name="adamw", grid_spec=grid_spec, out_shape=[sd] * 4,
        compiler_params=_params("parallel", "parallel", "parallel"))(c_idx, w, m, v, mine, theirs)


def _allreduce_small(v):
    rows, w = v.shape

    def body(x_ref, sum_ref, all_ref, send, recv, local):
        x, y, c, chips = _place()
        me, sibling = (x, y, c), (x, y, 1 - c)

        def slot(px, py, pc):
            return all_ref.at[4 * px + 2 * py + pc]

        def copy(k, block, to, src=None):
            return _remote(slot(*block) if src is None else src, slot(*block), send.at[k], recv.at[k], to)

        mine = pltpu.make_async_copy(x_ref, slot(*me), local)
        mine.start()
        first = [copy(0, me, sibling, src=x_ref)]
        first += [copy(1 + j, me, (*chip, c), src=x_ref) for j, chip in enumerate(chips)]
        for cp in first:
            cp.start()
        passed = [copy(4 + j, (*chip, c), sibling) for j, chip in enumerate(chips)]
        for j, chip in enumerate(chips):
            copy(1 + j, (*chip, c), me).wait_recv()
            passed[j].start()
        copy(0, sibling, me).wait_recv()
        for j, chip in enumerate(chips):
            copy(4 + j, (*chip, 1 - c), me).wait_recv()
        for cp in first + passed:
            cp.wait_send()
        mine.wait()
        tot = all_ref[0]
        for k in range(1, N_DEV):
            tot = tot + all_ref[k]
        sum_ref[...] = tot

    vm = pl.BlockSpec(memory_space=pltpu.VMEM)
    return pl.pallas_call(
        body, name="allreduce_small", in_specs=[vm], out_specs=[vm, vm],
        out_shape=[jax.ShapeDtypeStruct((rows, w), F32), jax.ShapeDtypeStruct((N_DEV, rows, w), F32)],
        scratch_shapes=[pltpu.SemaphoreType.DMA((7,)), pltpu.SemaphoreType.DMA((7,)), pltpu.SemaphoreType.DMA],
        compiler_params=pltpu.CompilerParams(vmem_limit_bytes=VMEM_LIMIT),
    )(v)[0]


BIG = ["ffn1_w1", "ffn1_w3", "ffn1_w2", "ffn2_w1", "ffn2_w3", "ffn2_w2", "ple_proj", "ple_gate",
       "s5_w_in", "s5_w_glu", "sb_w_qkv", "sb_w_o"]
TRANSPOSED = ("ffn1_w1", "ffn1_w3", "ffn2_w1", "ffn2_w3")
SMALL = ["ffn1_norm", "mix_norm", "ffn2_norm", "ple_norm", "s5_a_re", "s5_a_im", "s5_log_dt", "s5_b_re", "s5_b_im",
         "s5_c_re", "s5_c_im", "s5_d", "final_norm"]
ORDER = ["ffn1_norm", "ffn1_w1", "ffn1_w3", "ffn1_w2", "mix_norm", "ffn2_norm", "ffn2_w1", "ffn2_w3", "ffn2_w2",
         "ple_norm", "ple_proj", "ple_gate", "s5_w_in", "s5_a_re", "s5_a_im", "s5_log_dt", "s5_b_re", "s5_b_im",
         "s5_c_re", "s5_c_im", "s5_d", "s5_w_glu", "sb_w_qkv", "sb_w_o", "final_norm"]


def _pack(arrays):
    flat = jnp.concatenate([a.reshape(-1) for a in arrays])
    pad = (-flat.shape[0]) % 1024
    return jnp.pad(flat, (0, pad)).reshape(-1, 128)


def _unpack(packed, like):
    flat = packed.reshape(-1)
    out, off = [], 0
    for a in like:
        out.append(flat[off:off + a.size].reshape(a.shape))
        off += a.size
    return out


def _fwd_bwd(x, p, target, w, gathered):
    bl, l, d = x.shape
    t = bl * l
    depth = w["ffn1_norm"].shape[0]
    s5_ops, s5_vjp = jax.vjp(_s5_prep, w["s5_a_re"][0], w["s5_a_im"][0], w["s5_log_dt"][0], w["s5_b_re"][0],
                             w["s5_b_im"][0], w["s5_c_re"][0], w["s5_c_im"][0])

    h = x.reshape(t, d)
    p2 = [p[i].reshape(t, p.shape[-1]).astype(BF16) for i in range(depth)]
    saved = []
    for i in range(depth):
        norm = lambda name: w[name][i:i + 1]
        h, s1 = _ffn_fwd(h, norm("ffn1_norm"), gathered["ffn1_w1"], gathered["ffn1_w3"], gathered["ffn1_w2"], i)
        if i % 2 == 0:
            h, s2 = _s5_fwd(h, norm("mix_norm"), s5_ops, w["s5_d"][i // 2:i // 2 + 1], gathered["s5_w_in"], gathered["s5_w_glu"], bl)
        else:
            h, s2 = _sb_fwd(h, norm("mix_norm"), gathered["sb_w_qkv"], gathered["sb_w_o"], bl)
        h, s3 = _ffn_fwd(h, norm("ffn2_norm"), gathered["ffn2_w1"], gathered["ffn2_w3"], gathered["ffn2_w2"], i)
        h, s4 = _ple_fwd(h, norm("ple_norm"), p2[i], gathered["ple_proj"], gathered["ple_gate"], i)
        saved.append((s1, s2, s3, s4))

    loss, dh, dfinal = _head(h, w["final_norm"].reshape(1, d), target.reshape(t, d))

    big = {k: None for k in BIG}
    small = {k: [None] * w[k].shape[0] if w[k].ndim > 1 else None for k in SMALL}
    small["final_norm"] = dfinal.reshape(d)
    for i in reversed(range(depth)):
        norm = lambda name: w[name][i:i + 1]
        slots = lambda *names: [(big[k], i, depth) for k in names]
        s1, s2, s3, s4 = saved[i]
        dh, dhb, dg, big["ple_proj"], big["ple_gate"] = _ple_bwd(
            dh, s4, norm("ple_norm"), p2[i], gathered["ple_proj"], gathered["ple_gate"], i, slots("ple_proj", "ple_gate"))
        small["ple_norm"][i] = dg[0]
        dh, dhb, dg, big["ffn2_w1"], big["ffn2_w3"], big["ffn2_w2"] = _ffn_bwd(
            dh, dhb, s3, norm("ffn2_norm"), gathered["ffn2_w1"], gathered["ffn2_w3"], gathered["ffn2_w2"], i,
            slots("ffn2_w1", "ffn2_w3", "ffn2_w2"))
        small["ffn2_norm"][i] = dg[0]
        if i % 2 == 0:
            dh, dhb, dg, big["s5_w_in"], big["s5_w_glu"], dd, dops = _s5_bwd(
                dh, s2, norm("mix_norm"), s5_ops, w["s5_d"][i // 2:i // 2 + 1], gathered["s5_w_in"], gathered["s5_w_glu"], bl)
            small["s5_d"][0] = dd[0]
            raw = s5_vjp(dops)
            for name, gr in zip(["s5_a_re", "s5_a_im", "s5_log_dt", "s5_b_re", "s5_b_im", "s5_c_re", "s5_c_im"], raw):
                small[name][0] = gr
        else:
            dh, dhb, dg, big["sb_w_qkv"], big["sb_w_o"] = _sb_bwd(
                dh, dhb, s2, norm("mix_norm"), gathered["sb_w_qkv"], gathered["sb_w_o"], bl)
        small["mix_norm"][i] = dg[0]
        dh, dhb, dg, big["ffn1_w1"], big["ffn1_w3"], big["ffn1_w2"] = _ffn_bwd(
            dh, dhb, s1, norm("ffn1_norm"), gathered["ffn1_w1"], gathered["ffn1_w3"], gathered["ffn1_w2"], i,
            slots("ffn1_w1", "ffn1_w3", "ffn1_w2"))
        small["ffn1_norm"][i] = dg[0]
    small_list = [jnp.stack(small[k]) if isinstance(small[k], list) else small[k] for k in SMALL]
    return loss, dh.reshape(bl, l, d), big, small_list


def _step(x, p, target, w, m, v):
    flip = lambda tree: {k: jnp.swapaxes(a, 1, 2) if k in TRANSPOSED else a for k, a in tree.items()}
    w, m, v = flip(w), flip(m), flip(v)
    gathered = dict(zip(BIG, _allgather_weights([_to_bf16(w[k]) for k in BIG])))
    loss, grad_x, big, small_list = _fwd_bwd(x, p, target, w, gathered)

    c_idx = lax.axis_index("c").astype(jnp.int32).reshape(1)
    cx, cy = lax.axis_index("x"), lax.axis_index("y")
    where = jnp.stack([2 * cx + cy, 2 * (1 - cx) + cy, 2 * cx + (1 - cy)]).astype(jnp.int32)
    partial = [big[k] for k in BIG]
    pair = [_pair_sum(g, t, c_idx) for g, t in zip(partial, _pair_exchange(partial))]
    first = _chip_exchange_first(pair)
    second = _chip_exchange_second([_chip_relay_sum(pr, f, where) for pr, f in zip(pair, first)])
    mine = [_chip_sum(pr, f, s, where) for pr, f, s in zip(pair, first, second)]
    theirs = _pair_swap(mine)
    out_g, out_d, out_m, out_v = {}, {}, {}, {}
    for k, a, b in zip(BIG, mine, theirs):
        out_g[k], out_d[k], out_m[k], out_v[k] = _adamw_big(w[k], m[k], v[k], a, b, c_idx)

    like = [w[k] for k in SMALL]
    pad = [jnp.zeros((1,), F32)]
    g_small = _allreduce_small(_pack(small_list + [loss.reshape(1)]))
    packed = (g_small,) + tuple(_adamw_small(_pack(like + pad), g_small, _pack([m[k] for k in SMALL] + pad),
                                             _pack([v[k] for k in SMALL] + pad)))
    for dst, pk in zip((out_g, out_d, out_m, out_v), packed):
        dst.update(dict(zip(SMALL, _unpack(pk, like))))
    loss = g_small.reshape(-1)[sum(a.size for a in like)]
    out_g, out_d, out_m, out_v = flip(out_g), flip(out_d), flip(out_m), flip(out_v)
    return (loss, grad_x, *[out_g[k] for k in ORDER], *[out_d[k] for k in ORDER],
            *[out_m[k] for k in ORDER], *[out_v[k] for k in ORDER])


def kernel(x, p, ffn1_norm, ffn1_w1, ffn1_w3, ffn1_w2, mix_norm, ffn2_norm, ffn2_w1, ffn2_w3, ffn2_w2, ple_norm, ple_proj, ple_gate, s5_w_in, s5_a_re, s5_a_im, s5_log_dt, s5_b_re, s5_b_im, s5_c_re, s5_c_im, s5_d, s5_w_glu, sb_w_qkv, sb_w_o, final_norm, loss_target, m_ffn1_norm, m_ffn1_w1, m_ffn1_w3, m_ffn1_w2, m_mix_norm, m_ffn2_norm, m_ffn2_w1, m_ffn2_w3, m_ffn2_w2, m_ple_norm, m_ple_proj, m_ple_gate, m_s5_w_in, m_s5_a_re, m_s5_a_im, m_s5_log_dt, m_s5_b_re, m_s5_b_im, m_s5_c_re, m_s5_c_im, m_s5_d, m_s5_w_glu, m_sb_w_qkv, m_sb_w_o, m_final_norm, v_ffn1_norm, v_ffn1_w1, v_ffn1_w3, v_ffn1_w2, v_mix_norm, v_ffn2_norm, v_ffn2_w1, v_ffn2_w3, v_ffn2_w2, v_ple_norm, v_ple_proj, v_ple_gate, v_s5_w_in, v_s5_a_re, v_s5_a_im, v_s5_log_dt, v_s5_b_re, v_s5_b_im, v_s5_c_re, v_s5_c_im, v_s5_d, v_s5_w_glu, v_sb_w_qkv, v_sb_w_o, v_final_norm):
    args = dict(locals())
    w = {k: args[k] for k in ORDER}
    m = {k: args["m_" + k] for k in ORDER}
    v = {k: args["v_" + k] for k in ORDER}
    return _step(x, p, loss_target, w, m, v)
```

```python
import functools
import math

import jax
import jax.numpy as jnp
from jax import lax
from jax.experimental import pallas as pl
from jax.experimental.pallas import tpu as pltpu

F32 = jnp.float32
BF16 = jnp.bfloat16
MESH = pl.DeviceIdType.MESH

N_CHIPS = 4
N_DEV = 8
RMS_EPS = 1e-6
S5_GROUP = 16
S5_STATE = 64
S5_CHUNK = 16
SB_HEAD_DIM = 64
SB_BLOCK = 128
SB_CUT = -104.0
SB_UNROLL = 3
ADAM_LR, ADAM_B1, ADAM_B2, ADAM_EPS, ADAM_WD, ADAM_STEP = 0.001, 0.9, 0.999, 1e-08, 0.01, 10
VMEM_LIMIT = 56 * 1024 * 1024

NN = (((1,), (0,)), ((), ()))
NT = (((1,), (1,)), ((), ()))
TN = (((0,), (0,)), ((), ()))

ANY = pl.BlockSpec(memory_space=pl.ANY)


def _tile(n, target):
    if n <= target:
        return n
    for t in range(target - target % 8, 7, -8):
        if n % t == 0:
            return t
    raise ValueError(f"no row tile for {n}")


def _params(*semantics):
    return pltpu.CompilerParams(dimension_semantics=semantics, vmem_limit_bytes=VMEM_LIMIT)


def _sigmoid(v):
    return 1.0 / (1.0 + jnp.exp(-v))


def _gemm(name, grid, operands, in_specs, groups, acc_shapes, out_shapes, out_specs, epilogue, reduce_axis=None, aliases=None):
    n_in, n_out = len(operands), len(out_shapes)
    n_red = None if reduce_axis is None else grid[reduce_axis]

    def body(*refs):
        ins, outs, accs = refs[:n_in], refs[n_in:n_in + n_out], refs[n_in + n_out:]

        def products():
            res = []
            for terms in groups:
                tot = None
                for ia, ib, dims in terms:
                    d = lax.dot_general(ins[ia][...], ins[ib][...], dims, preferred_element_type=F32)
                    tot = d if tot is None else tot + d
                res.append(tot)
            return res

        def finish(vals):
            for o, v in zip(outs, epilogue(vals, ins)):
                o[...] = v.astype(o.dtype)

        if reduce_axis is None:
            finish(products())
        else:
            k = pl.program_id(reduce_axis)

            @pl.when(k == 0)
            def _():
                for a in accs:
                    a[...] = jnp.zeros_like(a)

            for a, d in zip(accs, products()):
                a[...] += d

            @pl.when(k == n_red - 1)
            def _():
                finish([a[...] for a in accs])

    scratch = [] if reduce_axis is None else [pltpu.VMEM(s, F32) for s in acc_shapes]
    sem = tuple("arbitrary" if i == reduce_axis else "parallel" for i in range(len(grid)))
    return pl.pallas_call(
        body, name=name, grid=grid, in_specs=in_specs, out_specs=out_specs, out_shape=out_shapes,
        scratch_shapes=scratch, input_output_aliases=aliases or {}, compiler_params=_params(*sem))(*operands)


def _ident(vals, ins):
    return vals


def _act_spec(layout, tm, cs, pos):
    if layout == "sm":
        return pl.BlockSpec((None, tm, cs), lambda *g: (pos(*g)[1], pos(*g)[0], 0))
    return pl.BlockSpec((tm, cs), lambda *g: pos(*g))


def _act_shape(layout, t, cs, dtype):
    return jax.ShapeDtypeStruct((N_CHIPS, t, cs) if layout == "sm" else (t, N_CHIPS * cs), dtype)


def _w_spec(w, layer, pos_k):
    _, _, r, c = w.shape
    return pl.BlockSpec((None, None, r, c), lambda *g: (pos_k(*g), layer, 0, 0))


def _mm_cs(name, x, w, layer, out_layout, out_dtype, tm=1024):
    t, kd = x.shape
    cs = w.shape[3]
    tm = _tile(t, tm)
    return _gemm(
        name, (N_CHIPS, t // tm), [x, w],
        [pl.BlockSpec((tm, kd), lambda k, i: (i, 0)), _w_spec(w, layer, lambda k, i: k)],
        [[(0, 1, NN)]], None, [_act_shape(out_layout, t, cs, out_dtype)],
        [_act_spec(out_layout, tm, cs, lambda k, i: (i, k))], _ident)[0]


def _mm_rs(name, xs, layout, w, layer, res=None, alpha=1.0, out_dtype=F32, tm=1024, gated=None):
    ks, n = w.shape[2], w.shape[3]
    t = xs.shape[1] if layout == "sm" else xs.shape[0]
    tm = _tile(t, tm)
    row = pl.BlockSpec((tm, n), lambda i, k: (i, 0))
    operands = [xs, w] + [a for a in (res, gated) if a is not None]
    specs = [_act_spec(layout, tm, ks, lambda i, k: (i, k)), _w_spec(w, layer, lambda i, k: k)] + [row] * (len(operands) - 2)

    def epilogue(vals, ins):
        y = alpha * vals[0]
        if gated is not None:
            return [y, ins[2][...] + ins[3][...] * _sigmoid(y)]
        return [y if res is None else ins[2][...] + y]

    outs = _gemm(
        name, (t // tm, N_CHIPS), operands, specs, [[(0, 1, NN)]], [(tm, n)],
        [jax.ShapeDtypeStruct((t, n), out_dtype)] * (1 if gated is None else 2), [row] * (1 if gated is None else 2),
        epilogue, reduce_axis=1)
    return outs[0] if gated is None else outs


def _mm_cs_dx(name, pairs, layout, layer, tm=1024, transposed=False, norm=None):
    w0 = pairs[0][1]
    kd, cs = (w0.shape[3], w0.shape[2]) if transposed else (w0.shape[2], w0.shape[3])
    dy0 = pairs[0][0]
    t = dy0.shape[1] if layout == "sm" else dy0.shape[0]
    tm = _tile(t, tm)
    operands, specs, terms = [], [], []
    for dy, w in pairs:
        terms.append((len(operands), len(operands) + 1, NN if transposed else NT))
        operands += [dy, w]
        specs += [_act_spec(layout, tm, cs, lambda i, k: (i, k)), _w_spec(w, layer, lambda i, k: k)]
    row = pl.BlockSpec((tm, kd), lambda i, k: (i, 0))
    if norm is None:
        return _gemm(name, (t // tm, N_CHIPS), operands, specs, [terms], [(tm, kd)],
                     [jax.ShapeDtypeStruct((t, kd), F32)], [row], _ident, reduce_axis=1)[0]
    base = len(operands)
    operands += list(norm)
    specs += [row, row, pl.BlockSpec(norm[2].shape, lambda i, k: (0, 0))]

    def epilogue(vals, ins):
        dx, dg = _rms_bwd_math(vals[0], ins[base + 1][...], ins[base + 2][...])
        dh = ins[base][...] + dx
        return [dh, dh, dg]

    dh, dhb, dg = _gemm(
        name, (t // tm, N_CHIPS), operands, specs, [terms], [(tm, kd)],
        [jax.ShapeDtypeStruct((t, kd), F32), jax.ShapeDtypeStruct((t, kd), BF16), jax.ShapeDtypeStruct((t // tm, 1, kd), F32)],
        [row, row, pl.BlockSpec((None, 1, kd), lambda i, k: (i, 0, 0))], epilogue, reduce_axis=1)
    return dh, dhb, dg.sum(axis=0)


def _mm_rs_dx(name, dy, w, layer, out_layout, out_dtype, tm=1024):
    t, n = dy.shape
    ks = w.shape[2]
    tm = _tile(t, tm)
    return _gemm(
        name, (N_CHIPS, t // tm), [dy, w],
        [pl.BlockSpec((tm, n), lambda k, i: (i, 0)), _w_spec(w, layer, lambda k, i: k)],
        [[(0, 1, NT)]], None, [_act_shape(out_layout, t, ks, out_dtype)],
        [_act_spec(out_layout, tm, ks, lambda k, i: (i, k))], _ident)[0]


def _mm_dw(name, x, x_layout, dy, dy_layout, slot, alpha=1.0, tk=2048):
    stack, layer, layers = slot
    if x_layout is None:
        t, rows = x.shape
        cols = dy.shape[2] if dy_layout == "sm" else dy.shape[1] // N_CHIPS
        tk = _tile(t, tk)
        xspec = pl.BlockSpec((tk, rows), lambda k, j: (j, 0))
        yspec = _act_spec(dy_layout, tk, cols, lambda k, j: (j, k))
    else:
        t, cols = dy.shape
        rows = x.shape[2] if x_layout == "sm" else x.shape[1] // N_CHIPS
        tk = _tile(t, tk)
        xspec = _act_spec(x_layout, tk, rows, lambda k, j: (j, k))
        yspec = pl.BlockSpec((tk, cols), lambda k, j: (j, 0))
    operands, specs = [x, dy], [xspec, yspec]
    if stack is not None:
        operands.append(stack)
        specs.append(ANY)
    return _gemm(
        name, (N_CHIPS, t // tk), operands, specs, [[(0, 1, TN)]], [(rows, cols)],
        [jax.ShapeDtypeStruct((N_CHIPS, layers, rows, cols), F32)],
        [pl.BlockSpec((None, None, rows, cols), lambda k, j: (k, layer, 0, 0))],
        lambda vals, ins: [alpha * vals[0]], reduce_axis=1, aliases=None if stack is None else {2: 0})[0]


def _rows(name, fn, ins, outs, accs=(), tm=512):
    t = ins[0].shape[0]
    tm = _tile(t, tm)
    n_in, n_out, n_acc = len(ins), len(outs), len(accs)
    in_specs = []
    for a in ins:
        if a.shape[0] == t:
            in_specs.append(pl.BlockSpec((tm, a.shape[1]), lambda i: (i, 0)))
        else:
            in_specs.append(pl.BlockSpec(a.shape, lambda i: (0, 0)))
    out_shape = [jax.ShapeDtypeStruct((t, c), d) for c, d in outs] + [jax.ShapeDtypeStruct(s, F32) for s in accs]
    out_specs = [pl.BlockSpec((tm, c), lambda i: (i, 0)) for c, _ in outs] + [pl.BlockSpec(s, lambda i: (0, 0)) for s in accs]

    def body(*refs):
        i = pl.program_id(0)
        row_vals, acc_vals = fn(*[r[...] for r in refs[:n_in]])
        for o, v in zip(refs[n_in:n_in + n_out], row_vals):
            o[...] = v.astype(o.dtype)
        acc_refs = refs[n_in + n_out:]
        if n_acc:
            @pl.when(i == 0)
            def _():
                for a in acc_refs:
                    a[...] = jnp.zeros_like(a)

            for a, v in zip(acc_refs, acc_vals):
                a[...] += v

    res = pl.pallas_call(
        body, name=name, grid=(t // tm,), in_specs=in_specs, out_specs=out_specs, out_shape=out_shape,
        compiler_params=_params("arbitrary" if n_acc else "parallel"))(*ins)
    return res[:n_out], res[n_out:]


def _to_bf16(a):
    def fn(x):
        return [x], []
    return _rows("weights_bf16", fn, [a.reshape(-1, a.shape[-1])], [(a.shape[-1], BF16)], tm=512)[0][0].reshape(a.shape)


def _rms_stats(x):
    return lax.rsqrt(jnp.mean(x * x, axis=-1, keepdims=True) + RMS_EPS)


def _rmsnorm(name, h, g):
    def fn(x, gg):
        return [x * _rms_stats(x) * gg], []
    return _rows(name, fn, [h, g], [(h.shape[1], BF16)])[0][0]


def _rms_bwd_math(dn, x, g):
    r = _rms_stats(x)
    xhat = x * r
    dxh = dn * g
    dx = r * (dxh - xhat * jnp.mean(dxh * xhat, axis=-1, keepdims=True))
    return dx, jnp.sum(dn * xhat, axis=0, keepdims=True)


def _rmsnorm_bwd(name, dres, dn, h, g):
    def fn(dr, d, x, gg):
        dx, dg = _rms_bwd_math(d, x, gg)
        return [dr + dx, dr + dx], [dg]
    (dh, dhb), (dg,) = _rows(name, fn, [dres, dn, h, g], [(h.shape[1], F32), (h.shape[1], BF16)], [(1, h.shape[1])])
    return dh, dhb, dg


def _ffn_fwd(h, g, w1, w3, w2, layer, tm=1024):
    t, d = h.shape
    fs = w1.shape[2]
    n = _rmsnorm("ffn_norm", h, g)
    tm = _tile(t, tm)

    def up(vals, ins):
        a, b = vals
        sg = _sigmoid(a)
        silu = a * sg
        return [b * sg * (1.0 + a * (1.0 - sg)), silu, silu * b]

    sm = _act_shape("sm", t, fs, BF16)
    osp = _act_spec("sm", tm, fs, lambda k, i: (i, k))
    ga, gb, s = _gemm(
        "ffn_up", (N_CHIPS, t // tm), [n, w1, w3],
        [pl.BlockSpec((tm, d), lambda k, i: (i, 0)), _w_spec(w1, layer, lambda k, i: k), _w_spec(w3, layer, lambda k, i: k)],
        [[(0, 1, NT)], [(0, 2, NT)]], None, [sm, sm, sm], [osp, osp, osp], up)
    out = _mm_rs("ffn_down", s, "sm", w2, layer, res=h, alpha=0.5)
    return out, (h, n, ga, gb, s)


def _ffn_bwd(dout, dob, saved, g, w1, w3, w2, layer, slots, tm=1024):
    h, n, ga, gb, s = saved
    t, d = h.shape
    fs = w1.shape[2]
    tm = _tile(t, tm)

    def down(vals, ins):
        ds = 0.5 * vals[0]
        return [ds * ins[2][...].astype(F32), ds * ins[3][...].astype(F32)]

    sm = _act_shape("sm", t, fs, BF16)
    asp = _act_spec("sm", tm, fs, lambda k, i: (i, k))
    da, db = _gemm(
        "ffn_down_dx", (N_CHIPS, t // tm), [dob, w2, ga, gb],
        [pl.BlockSpec((tm, d), lambda k, i: (i, 0)), _w_spec(w2, layer, lambda k, i: k), asp, asp],
        [[(0, 1, NT)]], None, [sm, sm], [asp, asp], down)
    dw2 = _mm_dw("ffn_dw2", s, "sm", dob, None, slots[2], alpha=0.5)
    dw1 = _mm_dw("ffn_dw1", da, "sm", n, None, slots[0])
    dw3 = _mm_dw("ffn_dw3", db, "sm", n, None, slots[1])
    dh, dhb, dg = _mm_cs_dx("ffn_up_dx", [(da, w1), (db, w3)], "sm", layer, transposed=True, norm=(dout, h, g))
    return dh, dhb, dg, dw1, dw3, dw2


def _ple_fwd(h, g, p2, wproj, wgate, layer):
    n = _rmsnorm("ple_norm", h, g)
    pp = _mm_cs("ple_proj", p2, wproj, layer, "flat", F32)
    gl, out = _mm_rs("ple_gate", n, "flat", wgate, layer, res=h, gated=pp)
    return out, (h, n, gl, pp)


def _ple_bwd(dout, saved, g, p2, wproj, wgate, layer, slots):
    h, n, gl, pp = saved
    d = h.shape[1]

    def fn(do, gg, q):
        sg = _sigmoid(gg)
        return [do * sg, do * q * sg * (1.0 - sg)], []
    (dpp, dgl), _ = _rows("ple_mix_bwd", fn, [dout, gl, pp], [(d, BF16), (d, BF16)])
    dwproj = _mm_dw("ple_dwproj", p2, None, dpp, "flat", slots[0])
    dwgate = _mm_dw("ple_dwgate", n, "flat", dgl, None, slots[1])
    dn = _mm_rs_dx("ple_gate_dx", dgl, wgate, layer, "flat", F32)
    dh, dhb, dg = _rmsnorm_bwd("ple_norm_bwd", dout, dn, h, g)
    return dh, dhb, dg, dwproj, dwgate


def _head(h, g, target):
    d = h.shape[1]

    def fn(x, gg, tg):
        y = x * _rms_stats(x) * gg
        err = y - tg
        dy = err * (1.0 / d)
        dx, dg = _rms_bwd_math(dy, x, gg)
        loss = 0.5 * jnp.sum(jnp.sum(err * err, axis=-1, keepdims=True) * (1.0 / d), axis=0, keepdims=True)
        return [dx], [dg, jnp.broadcast_to(loss, (1, 128))]
    (dh,), (dg, loss) = _rows("loss_head", fn, [h, g, target], [(d, F32)], [(1, d), (1, 128)])
    return loss[0, 0], dh, dg


S5_LANES = 2 * S5_STATE
S5_GB = 128 // S5_GROUP


def _s5_prep(a_re, a_im, log_dt, b_re, b_im, c_re, c_im):
    c, gb = S5_CHUNK, S5_GB
    g = a_re.shape[0]
    nb = g // gb
    lam_re = jnp.minimum(a_re, -1e-4)
    lam_im = a_im
    dt = jnp.exp(log_dt)[:, None, None]
    ks = jnp.arange(c + 1, dtype=F32)
    mag = jnp.exp(lam_re[..., None] * dt * ks)
    ph = lam_im[..., None] * dt * ks
    pw_re, pw_im = mag * jnp.cos(ph), mag * jnp.sin(ph)
    den = lam_re * lam_re + lam_im * lam_im
    nr, ni = pw_re[..., 1] - 1.0, pw_im[..., 1]
    fr = (nr * lam_re + ni * lam_im) / den
    fi = (ni * lam_re - nr * lam_im) / den
    bb_re = fr[..., None] * b_re - fi[..., None] * b_im
    bb_im = fr[..., None] * b_im + fi[..., None] * b_re
    ct_re, ct_im = c_re.transpose(0, 2, 1), c_im.transpose(0, 2, 1)
    ca_re = ct_re[:, :, None, :] * pw_re[..., None] - ct_im[:, :, None, :] * pw_im[..., None]
    ca_im = ct_re[:, :, None, :] * pw_im[..., None] + ct_im[:, :, None, :] * pw_re[..., None]
    hp = lax.Precision.HIGHEST
    kern = (jnp.einsum("gpj,gpkh->gkjh", bb_re, ca_re[:, :, :c], precision=hp)
            - jnp.einsum("gpj,gpkh->gkjh", bb_im, ca_im[:, :, :c], precision=hp))
    rev_re = pw_re[:, :, :c][:, :, ::-1].transpose(0, 2, 1)
    rev_im = pw_im[:, :, :c][:, :, ::-1].transpose(0, 2, 1)
    bt_re, bt_im = bb_re.transpose(0, 2, 1), bb_im.transpose(0, 2, 1)
    wn_re = rev_re[:, :, None, :] * bt_re[:, None] - rev_im[:, :, None, :] * bt_im[:, None]
    wn_im = rev_re[:, :, None, :] * bt_im[:, None] + rev_im[:, :, None, :] * bt_re[:, None]
    wn = jnp.concatenate([wn_re, wn_im], axis=-1)
    wo = jnp.concatenate([ca_re[:, :, 1:].transpose(0, 2, 3, 1), -ca_im[:, :, 1:].transpose(0, 2, 3, 1)], axis=-1)

    def blocks(x):
        return x.reshape(nb, gb, c, S5_GROUP, x.shape[3]).transpose(0, 2, 1, 3, 4).reshape(nb, c, gb * S5_GROUP, x.shape[3])

    ar, ai = pw_re[..., c], pw_im[..., c]
    return (jnp.tile(blocks(kern), (1, 1, 1, gb)), blocks(wn), blocks(wo),
            jnp.concatenate([ar, ar], axis=1), jnp.concatenate([-ai, ai], axis=1))


def _step_rows(ref, tau, n):
    return ref[pl.ds(tau, n, stride=S5_CHUNK), :].astype(BF16)


def _cat_groups(ref, dtype):
    return jnp.concatenate([ref[:, j, :] for j in range(S5_GB)], axis=1).astype(dtype)


def _cat_steps(ref, n):
    return jnp.concatenate([_step_rows(ref, tau, n) for tau in range(S5_CHUNK)], axis=1)


def _stack_steps(ref, n):
    return jnp.concatenate([_step_rows(ref, tau, n) for tau in range(S5_CHUNK)], axis=0)


def _cat_ops(ref, axis, reverse=False):
    order = range(S5_CHUNK - 1, -1, -1) if reverse else range(S5_CHUNK)
    return jnp.concatenate([ref[k] for k in order], axis=axis)


def _row_group(rows, lanes):
    row = (lax.broadcasted_iota(jnp.int32, (rows, lanes), 0) // S5_GROUP) % S5_GB
    lane = (lax.broadcasted_iota(jnp.int32, (rows, lanes), 1) // S5_GROUP) % S5_GB
    return row, lane


def _own_group(x):
    row, lane = _row_group(*x.shape)
    return jnp.where(row == lane, x, jnp.zeros_like(x))


def _spread(x):
    row, _ = _row_group(*x.shape)
    return jnp.concatenate([jnp.where(row == j, x, jnp.zeros_like(x)) for j in range(S5_GB)], axis=1)


def _gather_own(x):
    row, _ = _row_group(x.shape[0], S5_LANES)
    out = jnp.zeros((x.shape[0], S5_LANES), x.dtype)
    for j in range(S5_GB):
        out = out + jnp.where(row == j, x[:, j * S5_LANES:(j + 1) * S5_LANES], 0.0)
    return out


def _s5_specs(t, d):
    nct, g = t // S5_CHUNK, d // S5_GROUP
    tok = pl.BlockSpec((t, 128), lambda i: (0, i))
    st = pl.BlockSpec((nct, S5_GB, S5_LANES), lambda i: (0, i, 0))
    op = lambda w: pl.BlockSpec((None,) + w.shape[1:], lambda i: (i, 0, 0, 0))
    return nct, g, tok, st, op


def _s5_chunk_fwd(u, bd, bn):
    t, d = u.shape
    nct, g, tok, st, op = _s5_specs(t, d)
    c = S5_CHUNK

    def body(u_ref, bd_ref, bn_ref, y_ref, s_ref):
        ucat = _cat_steps(u_ref, nct)
        sloc = jnp.dot(ucat, _spread(_cat_ops(bn_ref, 0)), preferred_element_type=F32)
        for j in range(S5_GB):
            s_ref[:, j, :] = sloc[:, j * S5_LANES:(j + 1) * S5_LANES]
        lags = _own_group(_cat_ops(bd_ref, 0, reverse=True))
        for tt in range(c):
            y_ref[pl.ds(tt, nct, stride=c), :] = jnp.dot(ucat[:, :(tt + 1) * 128], lags[(c - 1 - tt) * 128:, :],
                                                         preferred_element_type=F32)

    return pl.pallas_call(
        body, name="s5_chunk", grid=(d // 128,), in_specs=[tok, op(bd), op(bn)], out_specs=[tok, st],
        out_shape=[jax.ShapeDtypeStruct((t, d), F32), jax.ShapeDtypeStruct((nct, g, S5_LANES), F32)],
        compiler_params=_params("parallel"))(u, bd, bn)


def _s5_state_out(sprev, co, yin):
    t, d = yin.shape
    nct, g, tok, st, op = _s5_specs(t, d)
    c = S5_CHUNK

    def body(s_ref, co_ref, yi_ref, y_ref):
        ys = lax.dot_general(_cat_groups(s_ref, BF16), _spread(_cat_ops(co_ref, 0)), NT,
                             preferred_element_type=F32)
        for tt in range(c):
            rows = pl.ds(tt, nct, stride=c)
            y_ref[rows, :] = yi_ref[rows, :] + ys[:, tt * 128:(tt + 1) * 128]

    return pl.pallas_call(
        body, name="s5_state_out", grid=(d // 128,), in_specs=[st, op(co), tok], out_specs=tok,
        out_shape=jax.ShapeDtypeStruct((t, d), F32), compiler_params=_params("parallel"))(sprev, co, yin)


def _s5_state_out_dx(dyb, co):
    t, d = dyb.shape
    nct, g, tok, st, op = _s5_specs(t, d)
    c = S5_CHUNK

    def body(dy_ref, co_ref, ds_ref):
        acc = jnp.dot(_cat_steps(dy_ref, nct), _spread(_cat_ops(co_ref, 0)), preferred_element_type=F32)
        for j in range(S5_GB):
            ds_ref[:, j, :] = acc[:, j * S5_LANES:(j + 1) * S5_LANES]

    return pl.pallas_call(
        body, name="s5_state_out_dx", grid=(d // 128,), in_specs=[tok, op(co)], out_specs=st,
        out_shape=jax.ShapeDtypeStruct((nct, g, S5_LANES), F32), compiler_params=_params("parallel"))(dyb, co)


def _s5_chunk_dx(dyb, dsloc, bd, bn, skip):
    t, d = dyb.shape
    nct, g, tok, st, op = _s5_specs(t, d)
    c = S5_CHUNK

    def body(dy_ref, ds_ref, bd_ref, bn_ref, sk_ref, du_ref):
        dus = lax.dot_general(_cat_groups(ds_ref, BF16), _spread(_cat_ops(bn_ref, 0)), NT, preferred_element_type=F32)
        dycat = _cat_steps(dy_ref, nct)
        lags = _own_group(_cat_ops(bd_ref, 1))
        for tau in range(c):
            rows = pl.ds(tau, nct, stride=c)
            du_ref[rows, :] = (sk_ref[rows, :] + dus[:, tau * 128:(tau + 1) * 128]
                               + lax.dot_general(dycat[:, tau * 128:], lags[:, :(c - tau) * 128], NT,
                                                 preferred_element_type=F32))

    return pl.pallas_call(
        body, name="s5_chunk_dx", grid=(d // 128,), in_specs=[tok, st, op(bd), op(bn), tok], out_specs=tok,
        out_shape=jax.ShapeDtypeStruct((t, d), F32), compiler_params=_params("parallel"))(dyb, dsloc, bd, bn, skip)


def _s5_chunk_dw(u, dyb, dsloc, bd, bn):
    t, d = u.shape
    nct, g, tok, st, op = _s5_specs(t, d)
    c = S5_CHUNK

    def body(u_ref, dy_ref, ds_ref, dbd_ref, dbn_ref):
        dbn = _gather_own(lax.dot_general(_cat_steps(u_ref, nct), _cat_groups(ds_ref, BF16), TN,
                                          preferred_element_type=F32))
        for tau in range(c):
            dbn_ref[tau] = dbn[tau * 128:(tau + 1) * 128, :]
        ustk, dystk = _stack_steps(u_ref, nct), _stack_steps(dy_ref, nct)
        for k in range(c):
            dbd_ref[k] = _own_group(lax.dot_general(ustk[:(c - k) * nct], dystk[k * nct:], TN,
                                                    preferred_element_type=F32))

    return pl.pallas_call(
        body, name="s5_chunk_dw", grid=(d // 128,), in_specs=[tok, tok, st], out_specs=[op(bd), op(bn)],
        out_shape=[jax.ShapeDtypeStruct(bd.shape, F32), jax.ShapeDtypeStruct(bn.shape, F32)],
        compiler_params=_params("parallel"))(u, dyb, dsloc)


def _s5_state_out_dw(sprev, dyb, co):
    t, d = dyb.shape
    nct, g, tok, st, op = _s5_specs(t, d)
    c = S5_CHUNK

    def body(s_ref, dy_ref, dco_ref):
        dco = _gather_own(lax.dot_general(_cat_steps(dy_ref, nct), _cat_groups(s_ref, BF16), TN,
                                          preferred_element_type=F32))
        for tt in range(c):
            dco_ref[tt] = dco[tt * 128:(tt + 1) * 128, :]

    return pl.pallas_call(
        body, name="s5_state_out_dw", grid=(d // 128,), in_specs=[st, tok], out_specs=op(co),
        out_shape=jax.ShapeDtypeStruct(co.shape, F32), compiler_params=_params("parallel"))(sprev, dyb)


def _s5_scan_fwd(sloc, m1, m2):
    bl, nc, g, w = sloc.shape

    def body(s_ref, m1_ref, m2_ref, o_ref):
        a1, a2 = m1_ref[...], m2_ref[...]

        def step(c, states):
            new = []
            for b, s in enumerate(states):
                o_ref[b, c] = s
                new.append(a1 * s + a2 * pltpu.roll(s, S5_STATE, 1) + s_ref[b, c])
            return tuple(new)
        lax.fori_loop(0, nc, step, tuple(jnp.zeros((g, w), F32) for _ in range(bl)))

    vm = pl.BlockSpec(memory_space=pltpu.VMEM)
    return pl.pallas_call(
        body, name="s5_scan", in_specs=[vm, vm, vm], out_specs=vm,
        out_shape=jax.ShapeDtypeStruct(sloc.shape, F32),
        compiler_params=pltpu.CompilerParams(vmem_limit_bytes=VMEM_LIMIT))(sloc, m1, m2)


def _s5_scan_bwd(dsprev, sprev, m1, m2):
    bl, nc, g, w = dsprev.shape

    def body(d_ref, s_ref, m1_ref, m2_ref, g_ref, p1_ref, p2_ref):
        a1, a2 = m1_ref[...], m2_ref[...]
        zero = jnp.zeros((g, w), F32)

        def step(i, carry):
            gps, p1, p2 = carry
            c = nc - 2 - i
            new = []
            for b, gp in enumerate(gps):
                g_ref[b, c] = gp
                sp = s_ref[b, c]
                p1 = p1 + gp * sp
                p2 = p2 + gp * pltpu.roll(sp, S5_STATE, 1)
                new.append(d_ref[b, c] + a1 * gp - a2 * pltpu.roll(gp, S5_STATE, 1))
            return tuple(new), p1, p2

        for b in range(bl):
            g_ref[b, nc - 1] = zero
        _, p1, p2 = lax.fori_loop(0, nc - 1, step, (tuple(d_ref[b, nc - 1] for b in range(bl)), zero, zero))
        p1_ref[...] = p1
        p2_ref[...] = p2

    vm = pl.BlockSpec(memory_space=pltpu.VMEM)
    sd = jax.ShapeDtypeStruct
    return pl.pallas_call(
        body, name="s5_scan_bwd", in_specs=[vm, vm, vm, vm], out_specs=[vm, vm, vm],
        out_shape=[sd(dsprev.shape, F32), sd((g, w), F32), sd((g, w), F32)],
        compiler_params=pltpu.CompilerParams(vmem_limit_bytes=VMEM_LIMIT))(dsprev, sprev, m1, m2)


def _gelu_tanh_parts(y):
    c0 = math.sqrt(2.0 / math.pi)
    inner = c0 * (y + 0.044715 * y * y * y)
    th = jnp.tanh(inner)
    return th, c0 * (1.0 + 3 * 0.044715 * y * y)


def _s5_fwd(h, g, ops, d_skip, w_in, w_glu, bl):
    bd, bn, co, m1, m2 = ops
    t, d = h.shape
    nct, groups = t // S5_CHUNK, d // S5_GROUP
    hn = _rmsnorm("mix_norm", h, g)
    u = _mm_rs("s5_in", hn, "flat", w_in, 0)
    yin, sloc = _s5_chunk_fwd(u, bd.astype(BF16), bn.astype(BF16))
    sprev = _s5_scan_fwd(sloc.reshape(bl, nct // bl, groups, S5_LANES), m1, m2).reshape(nct, groups, S5_LANES)
    y = _s5_state_out(sprev, co.astype(BF16), yin)

    def fn(yy, uu, dd):
        y2 = yy + dd * uu
        th, _ = _gelu_tanh_parts(y2)
        return [0.5 * y2 * (1.0 + th)], []
    z = _rows("s5_gelu", fn, [y, u, d_skip], [(d, BF16)])[0][0]
    zz = _mm_cs("s5_glu", z, w_glu, 0, "flat", F32)

    def glu(hh, zv):
        return [hh + zv[:, :d] * _sigmoid(zv[:, d:])], []
    out = _rows("s5_glu_mix", glu, [h, zz], [(d, F32)])[0][0]
    return out, (h, hn, u, sprev, y, z, zz)


def _s5_bwd(dout, saved, g, ops, d_skip, w_in, w_glu, bl):
    h, hn, u, sprev, y, z, zz = saved
    bd, bn, co, m1, m2 = ops
    t, d = h.shape
    nct, groups = t // S5_CHUNK, d // S5_GROUP

    def glu_bwd(do, zv):
        sg = _sigmoid(zv[:, d:])
        return [jnp.concatenate([do * sg, do * zv[:, :d] * sg * (1.0 - sg)], axis=1)], []
    dzz = _rows("s5_glu_bwd", glu_bwd, [dout, zz], [(2 * d, BF16)])[0][0]
    dwglu = _mm_dw("s5_dwglu", z, None, dzz, "flat", (None, 0, 1))
    dz = _mm_cs_dx("s5_glu_dx", [(dzz, w_glu)], "flat", 0)

    def gelu_bwd(dzv, yy, uu, dd):
        y2 = yy + dd * uu
        th, dinner = _gelu_tanh_parts(y2)
        dy2 = dzv * (0.5 * (1.0 + th) + 0.5 * y2 * (1.0 - th * th) * dinner)
        return [dy2, dy2 * dd], [jnp.sum(dy2 * uu, axis=0, keepdims=True)]
    (dyb, du_skip), (dd,) = _rows("s5_gelu_bwd", gelu_bwd, [dz, y, u, d_skip], [(d, F32), (d, F32)], [(1, d)])
    bd_b, bn_b, co_b = bd.astype(BF16), bn.astype(BF16), co.astype(BF16)
    dsprev = _s5_state_out_dx(dyb, co_b)
    shape4 = (bl, nct // bl, groups, S5_LANES)
    dsloc, dm1, dm2 = _s5_scan_bwd(dsprev.reshape(shape4), sprev.reshape(shape4), m1, m2)
    dsloc = dsloc.reshape(nct, groups, S5_LANES)
    du = _s5_chunk_dx(dyb, dsloc, bd_b, bn_b, du_skip).astype(BF16)
    dbd, dbn = _s5_chunk_dw(u, dyb, dsloc, bd, bn)
    dco = _s5_state_out_dw(sprev, dyb, co)
    dwin = _mm_dw("s5_dwin", hn, "flat", du, None, (None, 0, 1))
    dhn = _mm_rs_dx("s5_in_dx", du, w_in, 0, "flat", F32)
    dh, dhb, dg = _rmsnorm_bwd("mix_norm_bwd", dout, dhn, h, g)
    return dh, dhb, dg, dwin, dwglu, dd, (dbd, dbn, dco, dm1, dm2)


def _sb_block(qi, idx, tb):
    kb = qi - idx
    return pl.multiple_of(jnp.maximum(kb, 0) * tb, tb), idx == 0, kb >= 0


def _sb_scores(q, kblk, diag, exists, row, col):
    z = lax.dot_general(q, kblk, NT, preferred_element_type=F32) * (SB_HEAD_DIM ** -0.5)
    l1 = jnp.log(1.0 + jnp.exp(-jnp.abs(z)))
    ls = jnp.minimum(z, 0.0) - l1
    mask = jnp.logical_and(jnp.logical_or(col < row, jnp.logical_not(diag)), exists)
    lk = jnp.where(mask, ls - z, 0.0)
    return ls, lk, mask


def _split_dot(v, tri):
    hi = v.astype(BF16)
    lo = (v - hi.astype(F32)).astype(BF16)
    return (jnp.dot(hi, tri, preferred_element_type=F32) + jnp.dot(lo, tri, preferred_element_type=F32))


SB_PAIR =2 * SB_HEAD_DIM


def _pair_masks(tb):
    lane = lax.broadcasted_iota(jnp.int32, (1, SB_PAIR), 1)
    row = lax.broadcasted_iota(jnp.int32, (tb, tb), 0)
    col = lax.broadcasted_iota(jnp.int32, (tb, tb), 1)
    return [lane < SB_HEAD_DIM, lane >= SB_HEAD_DIM], row, col


def _pair_more(qi, carry):
    j, crs = carry[0], carry[2]
    return jnp.logical_and(j <= qi, jnp.maximum(jnp.max(crs[0]), jnp.max(crs[1])) > SB_CUT)


def _pair_specs(bl, l, d, tb):
    nq, off = l // tb, d // SB_PAIR
    qspec = pl.BlockSpec((tb, SB_PAIR), lambda b, p, i: (b * nq + i, p))
    kspec = pl.BlockSpec((l, SB_PAIR), lambda b, p, i: (b, off + p))
    vspec = pl.BlockSpec((l, SB_PAIR), lambda b, p, i: (b, 2 * off + p))
    return qspec, kspec, vspec


def _sb_attn_fwd2(qkv, bl):
    t, d3 = qkv.shape
    d, l = d3 // 3, t // bl
    tb = min(SB_BLOCK, l)
    nq = l // tb

    def body(q_ref, k_ref, v_ref, o_ref, ob_ref):
        qi = pl.program_id(2)
        heads, row, col = _pair_masks(tb)
        qv = q_ref[...]
        qh = [jnp.where(m, qv, jnp.zeros_like(qv)) for m in heads]
        tri = (row > col).astype(BF16)

        def step(carry):
            j, acc, crs = carry
            crs = list(crs)
            where = [_sb_block(qi, j + u, tb) for u in range(SB_UNROLL)]
            kblks = [k_ref[pl.ds(ks, tb), :] for ks, _, _ in where]
            scores = [[_sb_scores(qh[hd], kblks[u], where[u][1], where[u][2], row, col) for hd in range(2)]
                      for u in range(SB_UNROLL)]
            laters = [[_split_dot(sc[1], tri) for sc in su] for su in scores]
            for u in range(SB_UNROLL):
                vblk = v_ref[pl.ds(where[u][0], tb), :]
                outs = []
                for hd in range(2):
                    ls, lk, mask = scores[u][hd]
                    att = jnp.where(mask, jnp.exp(ls + laters[u][hd] + crs[hd]), 0.0)
                    outs.append(jnp.dot(att.astype(BF16), vblk, preferred_element_type=F32))
                    crs[hd] = crs[hd] + jnp.sum(lk, axis=1, keepdims=True)
                acc = acc + jnp.where(heads[0], outs[0], outs[1])
            return j + SB_UNROLL, acc, tuple(crs)

        zc = jnp.zeros((tb, 1), F32)
        _, acc, _ = lax.while_loop(functools.partial(_pair_more, qi), step,
                                   (jnp.int32(0), jnp.zeros((tb, SB_PAIR), F32), (zc, zc)))
        o_ref[...] = acc
        ob_ref[...] = acc.astype(BF16)

    qspec, kspec, vspec = _pair_specs(bl, l, d, tb)
    return pl.pallas_call(
        body, name="sb_attn", grid=(bl, d // SB_PAIR, nq), in_specs=[qspec, kspec, vspec], out_specs=[qspec, qspec],
        out_shape=[jax.ShapeDtypeStruct((t, d), F32), jax.ShapeDtypeStruct((t, d), BF16)],
        compiler_params=_params("parallel", "parallel", "parallel"))(qkv, qkv, qkv)


def _sb_attn_bwd2(qkv, o, do, bl):
    t, d3 = qkv.shape
    d, l = d3 // 3, t // bl
    tb = min(SB_BLOCK, l)
    nq = l // tb
    scale = SB_HEAD_DIM ** -0.5

    def body(q_ref, k_ref, v_ref, o_ref, do_ref, dq_ref, dk_ref, dv_ref, dk_acc, dv_acc):
        qi = pl.program_id(2)

        @pl.when(qi == 0)
        def _():
            dk_acc[...] = jnp.zeros_like(dk_acc)
            dv_acc[...] = jnp.zeros_like(dv_acc)

        heads, row, col = _pair_masks(tb)
        qv = q_ref[...]
        dov = do_ref[...].astype(BF16)
        qh = [jnp.where(m, qv, jnp.zeros_like(qv)) for m in heads]
        doh = [jnp.where(m, dov, jnp.zeros_like(dov)) for m in heads]
        ov = o_ref[...]
        dsum = [jnp.sum(dh.astype(F32) * ov, axis=1, keepdims=True) for dh in doh]
        tri = (row > col).astype(BF16)
        tri_inc = (row >= col).astype(BF16)

        def step(carry):
            j, dq, crs, ces = carry
            crs, ces = list(crs), list(ces)
            n = range(SB_UNROLL)
            where = [_sb_block(qi, j + u, tb) for u in n]
            rows = [pl.ds(ks, tb) for ks, _, _ in where]
            kblks = [k_ref[rows[u], :] for u in n]
            vblks = [v_ref[rows[u], :] for u in n]
            scores = [[_sb_scores(qh[hd], kblks[u], where[u][1], where[u][2], row, col) for hd in range(2)] for u in n]
            laters = [[_split_dot(sc[1], tri) for sc in su] for su in scores]
            datts = [[lax.dot_general(doh[hd], vblks[u], NT, preferred_element_type=F32) for hd in range(2)] for u in n]
            atts = [[None, None] for _ in n]
            for u in n:
                for hd in range(2):
                    ls, lk, mask = scores[u][hd]
                    atts[u][hd] = jnp.where(mask, jnp.exp(ls + laters[u][hd] + crs[hd]), 0.0).astype(BF16)
                    crs[hd] = crs[hd] + jnp.sum(lk, axis=1, keepdims=True)
            es = [[atts[u][hd].astype(F32) * datts[u][hd] for hd in range(2)] for u in n]
            sufs = [[_split_dot(e, tri_inc) for e in eu] for eu in es]
            dzs = [[None, None] for _ in n]
            for u in n:
                for hd in range(2):
                    ls, _, mask = scores[u][hd]
                    pre = dsum[hd] - ces[hd] - sufs[u][hd]
                    sg = jnp.exp(ls)
                    dzs[u][hd] = (jnp.where(mask, es[u][hd] * (1.0 - sg) - pre * sg, 0.0) * scale).astype(BF16)
                    ces[hd] = ces[hd] + jnp.sum(es[u][hd], axis=1, keepdims=True)
            for u in n:
                dq = dq + jnp.where(heads[0], jnp.dot(dzs[u][0], kblks[u], preferred_element_type=F32),
                                    jnp.dot(dzs[u][1], kblks[u], preferred_element_type=F32))
                dk_acc[rows[u], :] += (lax.dot_general(dzs[u][0], qh[0], TN, preferred_element_type=F32)
                                       + lax.dot_general(dzs[u][1], qh[1], TN, preferred_element_type=F32))
                dv_acc[rows[u], :] += (lax.dot_general(atts[u][0], doh[0], TN, preferred_element_type=F32)
                                       + lax.dot_general(atts[u][1], doh[1], TN, preferred_element_type=F32))
            return j + SB_UNROLL, dq, tuple(crs), tuple(ces)

        zc = jnp.zeros((tb, 1), F32)
        _, dq, _, _ = lax.while_loop(functools.partial(_pair_more, qi), step,
                                     (jnp.int32(0), jnp.zeros((tb, SB_PAIR), F32), (zc, zc), (zc, zc)))
        dq_ref[...] = dq.astype(BF16)

        @pl.when(qi == nq - 1)
        def _():
            dk_ref[...] = dk_acc[...].astype(BF16)
            dv_ref[...] = dv_acc[...].astype(BF16)

    qspec, kspec, vspec = _pair_specs(bl, l, d, tb)
    blk = pl.BlockSpec((tb, SB_PAIR), lambda b, p, i: (b * nq + i, p))
    full = pl.BlockSpec((l, SB_PAIR), lambda b, p, i: (b, p))
    sd = jax.ShapeDtypeStruct((t, d), BF16)
    dq, dk, dv = pl.pallas_call(
        body, name="sb_attn_bwd", grid=(bl, d // SB_PAIR, nq), in_specs=[qspec, kspec, vspec, blk, blk],
        out_specs=[blk, full, full], out_shape=[sd, sd, sd],
        scratch_shapes=[pltpu.VMEM((l, SB_PAIR), F32), pltpu.VMEM((l, SB_PAIR), F32)],
        compiler_params=_params("parallel", "parallel", "arbitrary"))(qkv, qkv, qkv, o, do)
    return jnp.concatenate([dq, dk, dv], axis=1)


def _sb_fwd(h, g, w_qkv, w_o, bl):
    t, d = h.shape
    hn = _rmsnorm("mix_norm", h, g)
    qkv = _mm_cs("sb_qkv", hn, w_qkv, 0, "flat", BF16)
    o, ob = _sb_attn_fwd2(qkv, bl)
    out = _mm_rs("sb_out", ob, "flat", w_o, 0, res=h)
    return out, (h, hn, qkv, o, ob)


def _sb_bwd(dout, dob, saved, g, w_qkv, w_o, bl):
    h, hn, qkv, o, ob = saved
    dwo = _mm_dw("sb_dwo", ob, "flat", dob, None, (None, 0, 1))
    do = _mm_rs_dx("sb_out_dx", dob, w_o, 0, "flat", F32)
    dqkv = _sb_attn_bwd2(qkv, o, do, bl)
    dwqkv = _mm_dw("sb_dwqkv", hn, None, dqkv, "flat", (None, 0, 1))
    dh, dhb, dg = _mm_cs_dx("sb_qkv_dx", [(dqkv, w_qkv)], "flat", 0, norm=(dout, h, g))
    return dh, dhb, dg, dwqkv, dwo


def _adamw_update(wv, gr, mv, vv):
    c1 = 1.0 / (1.0 - ADAM_B1 ** ADAM_STEP)
    c2 = 1.0 / (1.0 - ADAM_B2 ** ADAM_STEP)
    mn = ADAM_B1 * mv + (1.0 - ADAM_B1) * gr
    vn = ADAM_B2 * vv + (1.0 - ADAM_B2) * gr * gr
    delta = -ADAM_LR * ((mn * c1) / (jnp.sqrt(vn * c2) + ADAM_EPS) + ADAM_WD * wv)
    return delta, mn, vn


def _adamw_small(w, gr, m, v):
    def fn(wv, gv, mv, vv):
        return list(_adamw_update(wv, gv, mv, vv)), []
    return _rows("adamw_small", fn, [w, gr, m, v], [(w.shape[1], F32)] * 3)[0]


def _place():
    x, y, c = lax.axis_index("x"), lax.axis_index("y"), lax.axis_index("c")
    chips = [(1 - x, y), (x, 1 - y), (1 - x, 1 - y)]
    return x, y, c, chips


def _remote(src, dst, send_sem, recv_sem, to):
    return pltpu.make_async_remote_copy(src_ref=src, dst_ref=dst, send_sem=send_sem, recv_sem=recv_sem,
                                        device_id=to, device_id_type=MESH)


def _half(ref, c, rh, lead):
    return ref.at[(slice(None),) * lead + (pl.ds(c * rh, rh),)]


def _allgather_weights(ws):
    n = len(ws)

    def body(*refs):
        ins, outs = refs[:n], refs[n:2 * n]
        send, recv = refs[2 * n:]
        x, y, c, _ = _place()
        chip_x, chip_y, chip_d = (1 - x, y), (x, 1 - y), (1 - x, 1 - y)
        sibling = (x, y, 1 - c)
        index = lambda chip: 2 * chip[0] + chip[1]
        sent = []

        def quarter(ref, half, q, rq):
            return ref.at[:, pl.ds((2 * half + q) * rq, rq)]

        def copy(t, kind, src, dst, to):
            return _remote(src, dst, send.at[t, kind], recv.at[t, kind], to)

        def start(cp):
            cp.start()
            sent.append(cp)

        for t in range(n):
            rq = ws[t].shape[1] // 4
            for q in range(2):
                for base, chip in ((0, chip_x), (2, chip_y)):
                    start(copy(t, base + q, quarter(ins[t], c, q, rq), quarter(outs[t].at[index((x, y))], c, q, rq), (*chip, c)))
        for t in range(n):
            rq = ws[t].shape[1] // 4
            landings = [(chip_x, 0, 0, chip_x, ((4, chip_y), (6, None))), (chip_y, 1, 3, chip_y, ((5, chip_x), (9, None))),
                        (chip_x, 1, 1, chip_x, ((7, None),)), (chip_y, 0, 2, chip_y, ((8, None),)),
                        (chip_d, 0, 4, chip_y, ((10, None),)), (chip_d, 1, 5, chip_x, ((11, None),))]
            for origin, q, kind, sender, onward in landings:
                piece = quarter(outs[t].at[index(origin)], c, q, rq)
                copy(t, kind, piece, piece, (*sender, c)).wait_recv()
                for kind2, chip in onward:
                    start(copy(t, kind2, piece, piece, sibling if chip is None else (*chip, c)))
        for t in range(n):
            rq = ws[t].shape[1] // 4
            for kind, (origin, q) in zip(range(6, 12), ((chip_x, 0), (chip_x, 1), (chip_y, 0), (chip_y, 1), (chip_d, 0), (chip_d, 1))):
                piece = quarter(outs[t].at[index(origin)], 1 - c, q, rq)
                copy(t, kind, piece, piece, sibling).wait_recv()
        for cp in sent:
            cp.wait_send()

    res = pl.pallas_call(
        body, name="allgather_weights", in_specs=[ANY] * n, out_specs=[ANY] * n,
        out_shape=[jax.ShapeDtypeStruct((N_CHIPS,) + w.shape, w.dtype) for w in ws],
        scratch_shapes=[pltpu.SemaphoreType.DMA((n, 12)), pltpu.SemaphoreType.DMA((n, 12))],
    )(*ws)
    own = 2 * lax.axis_index("x") + lax.axis_index("y")
    return [lax.dynamic_update_slice(g, w[None], (own, 0, 0, 0)) for g, w in zip(res, ws)]


def _pair_exchange(gs):
    n = len(gs)

    def body(*refs):
        ins, outs = refs[:n], refs[n:2 * n]
        send, recv = refs[2 * n:]
        x, y, c, _ = _place()
        copies = [_remote(_half(ins[t], 1 - c, gs[t].shape[2] // 2, 2), outs[t], send.at[t], recv.at[t], (x, y, 1 - c))
                  for t in range(n)]
        for cp in copies:
            cp.start()
        for cp in copies:
            cp.wait()

    return pl.pallas_call(
        body, name="grad_pair_exchange", in_specs=[ANY] * n, out_specs=[ANY] * n,
        out_shape=[jax.ShapeDtypeStruct(g.shape[:2] + (g.shape[2] // 2, g.shape[3]), F32) for g in gs],
        scratch_shapes=[pltpu.SemaphoreType.DMA((n,)), pltpu.SemaphoreType.DMA((n,))],
    )(*gs)


def _pair_sum(g, theirs, c_idx):
    n4, ly, r, cc = g.shape
    rh = r // 2
    tm = _tile(rh, 512)
    nt = rh // tm

    def body(c_ref, g_ref, t_ref, o_ref):
        o_ref[...] = (g_ref[...] + t_ref[...]).astype(o_ref.dtype)

    blk = (None, tm, cc)
    grid_spec = pltpu.PrefetchScalarGridSpec(
        num_scalar_prefetch=1, grid=(n4 * ly, nt),
        in_specs=[pl.BlockSpec(blk, lambda a, i, cr: (a, cr[0] * nt + i, 0)), pl.BlockSpec(blk, lambda a, i, cr: (a, i, 0))],
        out_specs=pl.BlockSpec(blk, lambda a, i, cr: (a, i, 0)))
    out = pl.pallas_call(
        body, name="grad_pair_sum", grid_spec=grid_spec, out_shape=jax.ShapeDtypeStruct((n4 * ly, rh, cc), BF16),
        compiler_params=_params("parallel", "parallel"))(c_idx, g.reshape(n4 * ly, r, cc), theirs.reshape(n4 * ly, rh, cc))
    return out.reshape(n4, ly, rh, cc)


def _quarter(ref, q, rq):
    return ref.at[:, pl.ds(q * rq, rq)]


def _chip_exchange_first(ps):
    n = len(ps)

    def body(*refs):
        ins, outs = refs[:n], refs[n:2 * n]
        send, recv = refs[2 * n:]
        x, y, c, _ = _place()
        index = lambda cx, cy: 2 * cx + cy
        copies = []
        for t in range(n):
            rq = ps[t].shape[2] // 2
            for base, q, chip in ((0, 0, (1 - x, y)), (2, 1, (x, 1 - y))):
                for j, slice_of in enumerate((chip, (1 - x, 1 - y))):
                    copies.append(_remote(_quarter(ins[t].at[index(*slice_of)], q, rq), outs[t].at[base + j],
                                          send.at[t, base + j], recv.at[t, base + j], (*chip, c)))
        for cp in copies:
            cp.start()
        for cp in copies:
            cp.wait()

    return pl.pallas_call(
        body, name="grad_chip_exchange", in_specs=[ANY] * n, out_specs=[ANY] * n,
        out_shape=[jax.ShapeDtypeStruct((4, p.shape[1], p.shape[2] // 2, p.shape[3]), p.dtype) for p in ps],
        scratch_shapes=[pltpu.SemaphoreType.DMA((n, 4)), pltpu.SemaphoreType.DMA((n, 4))],
    )(*ps)


def _chip_relay_sum(p, first, where):
    _, ly, rh, cc = p.shape
    rq = rh // 2
    tm = _tile(rq, 512)
    nt = rq // tm

    def body(w_ref, p_ref, f_ref, out_ref):
        out_ref[...] = (p_ref[...].astype(F32) + f_ref[...].astype(F32)).astype(out_ref.dtype)

    blk = (None, None, tm, cc)
    grid_spec = pltpu.PrefetchScalarGridSpec(
        num_scalar_prefetch=1, grid=(2, ly, nt),
        in_specs=[pl.BlockSpec(blk, lambda s, l, i, w: (w[2 - s], l, s * nt + i, 0)),
                  pl.BlockSpec(blk, lambda s, l, i, w: (1 + 2 * s, l, i, 0))],
        out_specs=pl.BlockSpec(blk, lambda s, l, i, w: (s, l, i, 0)))
    return pl.pallas_call(
        body, name="grad_relay_sum", grid_spec=grid_spec, out_shape=jax.ShapeDtypeStruct((2, ly, rq, cc), p.dtype),
        compiler_params=_params("parallel", "parallel", "parallel"))(where, p, first)


def _chip_exchange_second(ss):
    n = len(ss)

    def body(*refs):
        ins, outs = refs[:n], refs[n:2 * n]
        send, recv = refs[2 * n:]
        x, y, c, _ = _place()
        copies = []
        for t in range(n):
            for j, chip in enumerate(((x, 1 - y), (1 - x, y))):
                copies.append(_remote(ins[t].at[j], outs[t].at[j], send.at[t, j], recv.at[t, j], (*chip, c)))
        for cp in copies:
            cp.start()
        for cp in copies:
            cp.wait()

    return pl.pallas_call(
        body, name="grad_chip_exchange_2", in_specs=[ANY] * n, out_specs=[ANY] * n,
        out_shape=[jax.ShapeDtypeStruct(s.shape, s.dtype) for s in ss],
        scratch_shapes=[pltpu.SemaphoreType.DMA((n, 2)), pltpu.SemaphoreType.DMA((n, 2))],
    )(*ss)


def _chip_sum(p, first, second, where):
    _, ly, rh, cc = p.shape
    rq = rh // 2
    tm = _tile(rq, 512)
    nt = rq // tm

    def body(w_ref, p_ref, f_ref, s_ref, out_ref):
        out_ref[...] = (p_ref[...].astype(F32) + f_ref[...].astype(F32)) + s_ref[...].astype(F32)

    blk = (None, None, tm, cc)
    grid_spec = pltpu.PrefetchScalarGridSpec(
        num_scalar_prefetch=1, grid=(ly, 2, nt),
        in_specs=[pl.BlockSpec(blk, lambda l, q, i, w: (w[0], l, q * nt + i, 0)),
                  pl.BlockSpec(blk, lambda l, q, i, w: (2 * q, l, i, 0)),
                  pl.BlockSpec(blk, lambda l, q, i, w: (q, l, i, 0))],
        out_specs=pl.BlockSpec((None, tm, cc), lambda l, q, i, w: (l, q * nt + i, 0)))
    return pl.pallas_call(
        body, name="grad_chip_sum", grid_spec=grid_spec, out_shape=jax.ShapeDtypeStruct((ly, rh, cc), F32),
        compiler_params=_params("parallel", "parallel", "parallel"))(where, p, first, second)


def _pair_swap(halves):
    n = len(halves)

    def body(*refs):
        ins, outs = refs[:n], refs[n:2 * n]
        send, recv = refs[2 * n:]
        x, y, c, _ = _place()
        copies = [_remote(ins[t], outs[t], send.at[t], recv.at[t], (x, y, 1 - c)) for t in range(n)]
        for cp in copies:
            cp.start()
        for cp in copies:
            cp.wait()

    return pl.pallas_call(
        body, name="grad_pair_swap", in_specs=[ANY] * n, out_specs=[ANY] * n,
        out_shape=[jax.ShapeDtypeStruct(h.shape, F32) for h in halves],
        scratch_shapes=[pltpu.SemaphoreType.DMA((n,)), pltpu.SemaphoreType.DMA((n,))],
    )(*halves)


def _adamw_big(w, m, v, mine, theirs, c_idx):
    ly, r, cc = w.shape
    rh = r // 2
    tm = _tile(rh, 512)
    nt = rh // tm

    def body(c_ref, w_ref, m_ref, v_ref, a_ref, b_ref, g_out, d_out, m_out, v_out):
        gr = jnp.where(pl.program_id(1) == c_ref[0], a_ref[...], b_ref[...])
        delta, mn, vn = _adamw_update(w_ref[...], gr, m_ref[...], v_ref[...])
        g_out[...] = gr
        d_out[...] = delta
        m_out[...] = mn
        v_out[...] = vn

    blk = (None, tm, cc)
    full = pl.BlockSpec(blk, lambda l, hc, i, cr: (l, hc * nt + i, 0))
    half = pl.BlockSpec(blk, lambda l, hc, i, cr: (l, i, 0))
    grid_spec = pltpu.PrefetchScalarGridSpec(
        num_scalar_prefetch=1, grid=(ly, 2, nt), in_specs=[full, full, full, half, half], out_specs=[full] * 4)
    sd = jax.ShapeDtypeStruct(w.shape, F32)
    return pl.pallas_call(
        body, name="adamw", grid_spec=grid_spec, out_shape=[sd] * 4,
        compiler_params=_params("parallel", "parallel", "parallel"))(c_idx, w, m, v, mine, theirs)


def _allreduce_small(v):
    rows, w = v.shape

    def body(x_ref, sum_ref, all_ref, send, recv, local):
        x, y, c, chips = _place()
        me, sibling = (x, y, c), (x, y, 1 - c)

        def slot(px, py, pc):
            return all_ref.at[4 * px + 2 * py + pc]

        def copy(k, block, to, src=None):
            return _remote(slot(*block) if src is None else src, slot(*block), send.at[k], recv.at[k], to)

        mine = pltpu.make_async_copy(x_ref, slot(*me), local)
        mine.start()
        first = [copy(0, me, sibling, src=x_ref)]
        first += [copy(1 + j, me, (*chip, c), src=x_ref) for j, chip in enumerate(chips)]
        for cp in first:
            cp.start()
        passed = [copy(4 + j, (*chip, c), sibling) for j, chip in enumerate(chips)]
        for j, chip in enumerate(chips):
            copy(1 + j, (*chip, c), me).wait_recv()
            passed[j].start()
        copy(0, sibling, me).wait_recv()
        for j, chip in enumerate(chips):
            copy(4 + j, (*chip, 1 - c), me).wait_recv()
        for cp in first + passed:
            cp.wait_send()
        mine.wait()
        tot = all_ref[0]
        for k in range(1, N_DEV):
            tot = tot + all_ref[k]
        sum_ref[...] = tot

    vm = pl.BlockSpec(memory_space=pltpu.VMEM)
    return pl.pallas_call(
        body, name="allreduce_small", in_specs=[vm], out_specs=[vm, vm],
        out_shape=[jax.ShapeDtypeStruct((rows, w), F32), jax.ShapeDtypeStruct((N_DEV, rows, w), F32)],
        scratch_shapes=[pltpu.SemaphoreType.DMA((7,)), pltpu.SemaphoreType.DMA((7,)), pltpu.SemaphoreType.DMA],
        compiler_params=pltpu.CompilerParams(vmem_limit_bytes=VMEM_LIMIT),
    )(v)[0]


BIG = ["ffn1_w1", "ffn1_w3", "ffn1_w2", "ffn2_w1", "ffn2_w3", "ffn2_w2", "ple_proj", "ple_gate",
       "s5_w_in", "s5_w_glu", "sb_w_qkv", "sb_w_o"]
TRANSPOSED = ("ffn1_w1", "ffn1_w3", "ffn2_w1", "ffn2_w3")
SMALL = ["ffn1_norm", "mix_norm", "ffn2_norm", "ple_norm", "s5_a_re", "s5_a_im", "s5_log_dt", "s5_b_re", "s5_b_im",
         "s5_c_re", "s5_c_im", "s5_d", "final_norm"]
ORDER = ["ffn1_norm", "ffn1_w1", "ffn1_w3", "ffn1_w2", "mix_norm", "ffn2_norm", "ffn2_w1", "ffn2_w3", "ffn2_w2",
         "ple_norm", "ple_proj", "ple_gate", "s5_w_in", "s5_a_re", "s5_a_im", "s5_log_dt", "s5_b_re", "s5_b_im",
         "s5_c_re", "s5_c_im", "s5_d", "s5_w_glu", "sb_w_qkv", "sb_w_o", "final_norm"]


def _pack(arrays):
    flat = jnp.concatenate([a.reshape(-1) for a in arrays])
    pad = (-flat.shape[0]) % 1024
    return jnp.pad(flat, (0, pad)).reshape(-1, 128)


def _unpack(packed, like):
    flat = packed.reshape(-1)
    out, off = [], 0
    for a in like:
        out.append(flat[off:off + a.size].reshape(a.shape))
        off += a.size
    return out


def _fwd_bwd(x, p, target, w, gathered):
    bl, l, d = x.shape
    t = bl * l
    depth = w["ffn1_norm"].shape[0]
    s5_ops, s5_vjp = jax.vjp(_s5_prep, w["s5_a_re"][0], w["s5_a_im"][0], w["s5_log_dt"][0], w["s5_b_re"][0],
                             w["s5_b_im"][0], w["s5_c_re"][0], w["s5_c_im"][0])

    h = x.reshape(t, d)
    p2 = [p[i].reshape(t, p.shape[-1]).astype(BF16) for i in range(depth)]
    saved = []
    for i in range(depth):
        norm = lambda name: w[name][i:i + 1]
        h, s1 = _ffn_fwd(h, norm("ffn1_norm"), gathered["ffn1_w1"], gathered["ffn1_w3"], gathered["ffn1_w2"], i)
        if i % 2 == 0:
            h, s2 = _s5_fwd(h, norm("mix_norm"), s5_ops, w["s5_d"][i // 2:i // 2 + 1], gathered["s5_w_in"], gathered["s5_w_glu"], bl)
        else:
            h, s2 = _sb_fwd(h, norm("mix_norm"), gathered["sb_w_qkv"], gathered["sb_w_o"], bl)
        h, s3 = _ffn_fwd(h, norm("ffn2_norm"), gathered["ffn2_w1"], gathered["ffn2_w3"], gathered["ffn2_w2"], i)
        h, s4 = _ple_fwd(h, norm("ple_norm"), p2[i], gathered["ple_proj"], gathered["ple_gate"], i)
        saved.append((s1, s2, s3, s4))

    loss, dh, dfinal = _head(h, w["final_norm"].reshape(1, d), target.reshape(t, d))

    big = {k: None for k in BIG}
    small = {k: [None] * w[k].shape[0] if w[k].ndim > 1 else None for k in SMALL}
    small["final_norm"] = dfinal.reshape(d)
    for i in reversed(range(depth)):
        norm = lambda name: w[name][i:i + 1]
        slots = lambda *names: [(big[k], i, depth) for k in names]
        s1, s2, s3, s4 = saved[i]
        dh, dhb, dg, big["ple_proj"], big["ple_gate"] = _ple_bwd(
            dh, s4, norm("ple_norm"), p2[i], gathered["ple_proj"], gathered["ple_gate"], i, slots("ple_proj", "ple_gate"))
        small["ple_norm"][i] = dg[0]
        dh, dhb, dg, big["ffn2_w1"], big["ffn2_w3"], big["ffn2_w2"] = _ffn_bwd(
            dh, dhb, s3, norm("ffn2_norm"), gathered["ffn2_w1"], gathered["ffn2_w3"], gathered["ffn2_w2"], i,
            slots("ffn2_w1", "ffn2_w3", "ffn2_w2"))
        small["ffn2_norm"][i] = dg[0]
        if i % 2 == 0:
            dh, dhb, dg, big["s5_w_in"], big["s5_w_glu"], dd, dops = _s5_bwd(
                dh, s2, norm("mix_norm"), s5_ops, w["s5_d"][i // 2:i // 2 + 1], gathered["s5_w_in"], gathered["s5_w_glu"], bl)
            small["s5_d"][0] = dd[0]
            raw = s5_vjp(dops)
            for name, gr in zip(["s5_a_re", "s5_a_im", "s5_log_dt", "s5_b_re", "s5_b_im", "s5_c_re", "s5_c_im"], raw):
                small[name][0] = gr
        else:
            dh, dhb, dg, big["sb_w_qkv"], big["sb_w_o"] = _sb_bwd(
                dh, dhb, s2, norm("mix_norm"), gathered["sb_w_qkv"], gathered["sb_w_o"], bl)
        small["mix_norm"][i] = dg[0]
        dh, dhb, dg, big["ffn1_w1"], big["ffn1_w3"], big["ffn1_w2"] = _ffn_bwd(
            dh, dhb, s1, norm("ffn1_norm"), gathered["ffn1_w1"], gathered["ffn1_w3"], gathered["ffn1_w2"], i,
            slots("ffn1_w1", "ffn1_w3", "ffn1_w2"))
        small["ffn1_norm"][i] = dg[0]
    small_list = [jnp.stack(small[k]) if isinstance(small[k], list) else small[k] for k in SMALL]
    return loss, dh.reshape(bl, l, d), big, small_list


def _step(x, p, target, w, m, v):
    flip = lambda tree: {k: jnp.swapaxes(a, 1, 2) if k in TRANSPOSED else a for k, a in tree.items()}
    w, m, v = flip(w), flip(m), flip(v)
    gathered = dict(zip(BIG, _allgather_weights([_to_bf16(w[k]) for k in BIG])))
    loss, grad_x, big, small_list = _fwd_bwd(x, p, target, w, gathered)

    c_idx = lax.axis_index("c").astype(jnp.int32).reshape(1)
    cx, cy = lax.axis_index("x"), lax.axis_index("y")
    where = jnp.stack([2 * cx + cy, 2 * (1 - cx) + cy, 2 * cx + (1 - cy)]).astype(jnp.int32)
    partial = [big[k] for k in BIG]
    pair = [_pair_sum(g, t, c_idx) for g, t in zip(partial, _pair_exchange(partial))]
    first = _chip_exchange_first(pair)
    second = _chip_exchange_second([_chip_relay_sum(pr, f, where) for pr, f in zip(pair, first)])
    mine = [_chip_sum(pr, f, s, where) for pr, f, s in zip(pair, first, second)]
    theirs = _pair_swap(mine)
    out_g, out_d, out_m, out_v = {}, {}, {}, {}
    for k, a, b in zip(BIG, mine, theirs):
        out_g[k], out_d[k], out_m[k], out_v[k] = _adamw_big(w[k], m[k], v[k], a, b, c_idx)

    like = [w[k] for k in SMALL]
    pad = [jnp.zeros((1,), F32)]
    g_small = _allreduce_small(_pack(small_list + [loss.reshape(1)]))
    packed = (g_small,) + tuple(_adamw_small(_pack(like + pad), g_small, _pack([m[k] for k in SMALL] + pad),
                                             _pack([v[k] for k in SMALL] + pad)))
    for dst, pk in zip((out_g, out_d, out_m, out_v), packed):
        dst.update(dict(zip(SMALL, _unpack(pk, like))))
    loss = g_small.reshape(-1)[sum(a.size for a in like)]
    out_g, out_d, out_m, out_v = flip(out_g), flip(out_d), flip(out_m), flip(out_v)
    return (loss, grad_x, *[out_g[k] for k in ORDER], *[out_d[k] for k in ORDER],
            *[out_m[k] for k in ORDER], *[out_v[k] for k in ORDER])


def kernel(x, p, ffn1_norm, ffn1_w1, ffn1_w3, ffn1_w2, mix_norm, ffn2_norm, ffn2_w1, ffn2_w3, ffn2_w2, ple_norm, ple_proj, ple_gate, s5_w_in, s5_a_re, s5_a_im, s5_log_dt, s5_b_re, s5_b_im, s5_c_re, s5_c_im, s5_d, s5_w_glu, sb_w_qkv, sb_w_o, final_norm, loss_target, m_ffn1_norm, m_ffn1_w1, m_ffn1_w3, m_ffn1_w2, m_mix_norm, m_ffn2_norm, m_ffn2_w1, m_ffn2_w3, m_ffn2_w2, m_ple_norm, m_ple_proj, m_ple_gate, m_s5_w_in, m_s5_a_re, m_s5_a_im, m_s5_log_dt, m_s5_b_re, m_s5_b_im, m_s5_c_re, m_s5_c_im, m_s5_d, m_s5_w_glu, m_sb_w_qkv, m_sb_w_o, m_final_norm, v_ffn1_norm, v_ffn1_w1, v_ffn1_w3, v_ffn1_w2, v_mix_norm, v_ffn2_norm, v_ffn2_w1, v_ffn2_w3, v_ffn2_w2, v_ple_norm, v_ple_proj, v_ple_gate, v_s5_w_in, v_s5_a_re, v_s5_a_im, v_s5_log_dt, v_s5_b_re, v_s5_b_im, v_s5_c_re, v_s5_c_im, v_s5_d, v_s5_w_glu, v_sb_w_qkv, v_sb_w_o, v_final_norm):
    args = dict(locals())
    w = {k: args[k] for k in ORDER}
    m = {k: args["m_" + k] for k in ORDER}
    v = {k: args["v_" + k] for k in ORDER}
    return _step(x, p, loss_target, w, m, v)
```

```python
import functools
import math

import jax
import jax.numpy as jnp
from jax import lax
from jax.experimental import pallas as pl
from jax.experimental.pallas import tpu as pltpu

F32 = jnp.float32
BF16 = jnp.bfloat16
MESH = pl.DeviceIdType.MESH

N_CHIPS = 4
N_DEV = 8
RMS_EPS = 1e-6
S5_GROUP = 16
S5_STATE = 64
S5_CHUNK = 16
SB_HEAD_DIM = 64
SB_BLOCK = 128
SB_CUT = -104.0
SB_UNROLL = 3
ADAM_LR, ADAM_B1, ADAM_B2, ADAM_EPS, ADAM_WD, ADAM_STEP = 0.001, 0.9, 0.999, 1e-08, 0.01, 10
VMEM_LIMIT = 56 * 1024 * 1024

NN = (((1,), (0,)), ((), ()))
NT = (((1,), (1,)), ((), ()))
TN = (((0,), (0,)), ((), ()))

ANY = pl.BlockSpec(memory_space=pl.ANY)


def _tile(n, target):
    if n <= target:
        return n
    for t in range(target - target % 8, 7, -8):
        if n % t == 0:
            return t
    raise ValueError(f"no row tile for {n}")


def _params(*semantics):
    return pltpu.CompilerParams(dimension_semantics=semantics, vmem_limit_bytes=VMEM_LIMIT)


def _sigmoid(v):
    return 1.0 / (1.0 + jnp.exp(-v))


def _gemm(name, grid, operands, in_specs, groups, acc_shapes, out_shapes, out_specs, epilogue, reduce_axis=None, aliases=None):
    n_in, n_out = len(operands), len(out_shapes)
    n_red = None if reduce_axis is None else grid[reduce_axis]

    def body(*refs):
        ins, outs, accs = refs[:n_in], refs[n_in:n_in + n_out], refs[n_in + n_out:]

        def products():
            res = []
            for terms in groups:
                tot = None
                for ia, ib, dims in terms:
                    d = lax.dot_general(ins[ia][...], ins[ib][...], dims, preferred_element_type=F32)
                    tot = d if tot is None else tot + d
                res.append(tot)
            return res

        def finish(vals):
            for o, v in zip(outs, epilogue(vals, ins)):
                o[...] = v.astype(o.dtype)

        if reduce_axis is None:
            finish(products())
        else:
            k = pl.program_id(reduce_axis)

            @pl.when(k == 0)
            def _():
                for a in accs:
                    a[...] = jnp.zeros_like(a)

            for a, d in zip(accs, products()):
                a[...] += d

            @pl.when(k == n_red - 1)
            def _():
                finish([a[...] for a in accs])

    scratch = [] if reduce_axis is None else [pltpu.VMEM(s, F32) for s in acc_shapes]
    sem = tuple("arbitrary" if i == reduce_axis else "parallel" for i in range(len(grid)))
    return pl.pallas_call(
        body, name=name, grid=grid, in_specs=in_specs, out_specs=out_specs, out_shape=out_shapes,
        scratch_shapes=scratch, input_output_aliases=aliases or {}, compiler_params=_params(*sem))(*operands)


def _ident(vals, ins):
    return vals


def _act_spec(layout, tm, cs, pos):
    if layout == "sm":
        return pl.BlockSpec((None, tm, cs), lambda *g: (pos(*g)[1], pos(*g)[0], 0))
    return pl.BlockSpec((tm, cs), lambda *g: pos(*g))


def _act_shape(layout, t, cs, dtype):
    return jax.ShapeDtypeStruct((N_CHIPS, t, cs) if layout == "sm" else (t, N_CHIPS * cs), dtype)


def _w_spec(w, layer, pos_k):
    _, _, r, c = w.shape
    return pl.BlockSpec((None, None, r, c), lambda *g: (pos_k(*g), layer, 0, 0))


def _mm_cs(name, x, w, layer, out_layout, out_dtype, tm=1024):
    t, kd = x.shape
    cs = w.shape[3]
    tm = _tile(t, tm)
    return _gemm(
        name, (N_CHIPS, t // tm), [x, w],
        [pl.BlockSpec((tm, kd), lambda k, i: (i, 0)), _w_spec(w, layer, lambda k, i: k)],
        [[(0, 1, NN)]], None, [_act_shape(out_layout, t, cs, out_dtype)],
        [_act_spec(out_layout, tm, cs, lambda k, i: (i, k))], _ident)[0]


def _mm_rs(name, xs, layout, w, layer, res=None, alpha=1.0, out_dtype=F32, tm=1024, gated=None):
    ks, n = w.shape[2], w.shape[3]
    t = xs.shape[1] if layout == "sm" else xs.shape[0]
    tm = _tile(t, tm)
    row = pl.BlockSpec((tm, n), lambda i, k: (i, 0))
    operands = [xs, w] + [a for a in (res, gated) if a is not None]
    specs = [_act_spec(layout, tm, ks, lambda i, k: (i, k)), _w_spec(w, layer, lambda i, k: k)] + [row] * (len(operands) - 2)

    def epilogue(vals, ins):
        y = alpha * vals[0]
        if gated is not None:
            return [y, ins[2][...] + ins[3][...] * _sigmoid(y)]
        return [y if res is None else ins[2][...] + y]

    outs = _gemm(
        name, (t // tm, N_CHIPS), operands, specs, [[(0, 1, NN)]], [(tm, n)],
        [jax.ShapeDtypeStruct((t, n), out_dtype)] * (1 if gated is None else 2), [row] * (1 if gated is None else 2),
        epilogue, reduce_axis=1)
    return outs[0] if gated is None else outs


def _mm_cs_dx(name, pairs, layout, layer, tm=1024, transposed=False, norm=None):
    w0 = pairs[0][1]
    kd, cs = (w0.shape[3], w0.shape[2]) if transposed else (w0.shape[2], w0.shape[3])
    dy0 = pairs[0][0]
    t = dy0.shape[1] if layout == "sm" else dy0.shape[0]
    tm = _tile(t, tm)
    operands, specs, terms = [], [], []
    for dy, w in pairs:
        terms.append((len(operands), len(operands) + 1, NN if transposed else NT))
        operands += [dy, w]
        specs += [_act_spec(layout, tm, cs, lambda i, k: (i, k)), _w_spec(w, layer, lambda i, k: k)]
    row = pl.BlockSpec((tm, kd), lambda i, k: (i, 0))
    if norm is None:
        return _gemm(name, (t // tm, N_CHIPS), operands, specs, [terms], [(tm, kd)],
                     [jax.ShapeDtypeStruct((t, kd), F32)], [row], _ident, reduce_axis=1)[0]
    base = len(operands)
    operands += list(norm)
    specs += [row, row, pl.BlockSpec(norm[2].shape, lambda i, k: (0, 0))]

    def epilogue(vals, ins):
        dx, dg = _rms_bwd_math(vals[0], ins[base + 1][...], ins[base + 2][...])
        dh = ins[base][...] + dx
        return [dh, dh, dg]

    dh, dhb, dg = _gemm(
        name, (t // tm, N_CHIPS), operands, specs, [terms], [(tm, kd)],
        [jax.ShapeDtypeStruct((t, kd), F32), jax.ShapeDtypeStruct((t, kd), BF16), jax.ShapeDtypeStruct((t // tm, 1, kd), F32)],
        [row, row, pl.BlockSpec((None, 1, kd), lambda i, k: (i, 0, 0))], epilogue, reduce_axis=1)
    return dh, dhb, dg.sum(axis=0)


def _mm_rs_dx(name, dy, w, layer, out_layout, out_dtype, tm=1024):
    t, n = dy.shape
    ks = w.shape[2]
    tm = _tile(t, tm)
    return _gemm(
        name, (N_CHIPS, t // tm), [dy, w],
        [pl.BlockSpec((tm, n), lambda k, i: (i, 0)), _w_spec(w, layer, lambda k, i: k)],
        [[(0, 1, NT)]], None, [_act_shape(out_layout, t, ks, out_dtype)],
        [_act_spec(out_layout, tm, ks, lambda k, i: (i, k))], _ident)[0]


def _mm_dw(name, x, x_layout, dy, dy_layout, slot, alpha=1.0, tk=4096):
    stack, layer, layers = slot
    if x_layout is None:
        t, rows = x.shape
        cols = dy.shape[2] if dy_layout == "sm" else dy.shape[1] // N_CHIPS
        tk = _tile(t, tk)
        xspec = pl.BlockSpec((tk, rows), lambda k, j: (j, 0))
        yspec = _act_spec(dy_layout, tk, cols, lambda k, j: (j, k))
    else:
        t, cols = dy.shape
        rows = x.shape[2] if x_layout == "sm" else x.shape[1] // N_CHIPS
        tk = _tile(t, tk)
        xspec = _act_spec(x_layout, tk, rows, lambda k, j: (j, k))
        yspec = pl.BlockSpec((tk, cols), lambda k, j: (j, 0))
    operands, specs = [x, dy], [xspec, yspec]
    if stack is not None:
        operands.append(stack)
        specs.append(ANY)
    return _gemm(
        name, (N_CHIPS, t // tk), operands, specs, [[(0, 1, TN)]], [(rows, cols)],
        [jax.ShapeDtypeStruct((N_CHIPS, layers, rows, cols), F32)],
        [pl.BlockSpec((None, None, rows, cols), lambda k, j: (k, layer, 0, 0))],
        lambda vals, ins: [alpha * vals[0]], reduce_axis=1 if t // tk > 1 else None,
        aliases=None if stack is None else {2: 0})[0]


def _rows(name, fn, ins, outs, accs=(), tm=512):
    t = ins[0].shape[0]
    tm = _tile(t, tm)
    n_in, n_out, n_acc = len(ins), len(outs), len(accs)
    in_specs = []
    for a in ins:
        if a.shape[0] == t:
            in_specs.append(pl.BlockSpec((tm, a.shape[1]), lambda i: (i, 0)))
        else:
            in_specs.append(pl.BlockSpec(a.shape, lambda i: (0, 0)))
    out_shape = [jax.ShapeDtypeStruct((t, c), d) for c, d in outs] + [jax.ShapeDtypeStruct(s, F32) for s in accs]
    out_specs = [pl.BlockSpec((tm, c), lambda i: (i, 0)) for c, _ in outs] + [pl.BlockSpec(s, lambda i: (0, 0)) for s in accs]

    def body(*refs):
        i = pl.program_id(0)
        row_vals, acc_vals = fn(*[r[...] for r in refs[:n_in]])
        for o, v in zip(refs[n_in:n_in + n_out], row_vals):
            o[...] = v.astype(o.dtype)
        acc_refs = refs[n_in + n_out:]
        if n_acc:
            @pl.when(i == 0)
            def _():
                for a in acc_refs:
                    a[...] = jnp.zeros_like(a)

            for a, v in zip(acc_refs, acc_vals):
                a[...] += v

    res = pl.pallas_call(
        body, name=name, grid=(t // tm,), in_specs=in_specs, out_specs=out_specs, out_shape=out_shape,
        compiler_params=_params("arbitrary" if n_acc else "parallel"))(*ins)
    return res[:n_out], res[n_out:]


def _to_bf16(a):
    def fn(x):
        return [x], []
    return _rows("weights_bf16", fn, [a.reshape(-1, a.shape[-1])], [(a.shape[-1], BF16)], tm=512)[0][0].reshape(a.shape)


def _rms_stats(x):
    return lax.rsqrt(jnp.mean(x * x, axis=-1, keepdims=True) + RMS_EPS)


def _rmsnorm(name, h, g):
    def fn(x, gg):
        return [x * _rms_stats(x) * gg], []
    return _rows(name, fn, [h, g], [(h.shape[1], BF16)])[0][0]


def _rms_bwd_math(dn, x, g):
    r = _rms_stats(x)
    xhat = x * r
    dxh = dn * g
    dx = r * (dxh - xhat * jnp.mean(dxh * xhat, axis=-1, keepdims=True))
    return dx, jnp.sum(dn * xhat, axis=0, keepdims=True)


def _rmsnorm_bwd(name, dres, dn, h, g):
    def fn(dr, d, x, gg):
        dx, dg = _rms_bwd_math(d, x, gg)
        return [dr + dx, dr + dx], [dg]
    (dh, dhb), (dg,) = _rows(name, fn, [dres, dn, h, g], [(h.shape[1], F32), (h.shape[1], BF16)], [(1, h.shape[1])])
    return dh, dhb, dg


def _ffn_fwd(h, g, w1, w3, w2, layer, tm=1024):
    t, d = h.shape
    fs = w1.shape[2]
    n = _rmsnorm("ffn_norm", h, g)
    tm = _tile(t, tm)

    def up(vals, ins):
        a, b = vals
        sg = _sigmoid(a)
        silu = a * sg
        return [b * sg * (1.0 + a * (1.0 - sg)), silu, silu * b]

    sm = _act_shape("sm", t, fs, BF16)
    osp = _act_spec("sm", tm, fs, lambda k, i: (i, k))
    ga, gb, s = _gemm(
        "ffn_up", (N_CHIPS, t // tm), [n, w1, w3],
        [pl.BlockSpec((tm, d), lambda k, i: (i, 0)), _w_spec(w1, layer, lambda k, i: k), _w_spec(w3, layer, lambda k, i: k)],
        [[(0, 1, NT)], [(0, 2, NT)]], None, [sm, sm, sm], [osp, osp, osp], up)
    out = _mm_rs("ffn_down", s, "sm", w2, layer, res=h, alpha=0.5)
    return out, (h, n, ga, gb, s)


def _ffn_bwd(dout, dob, saved, g, w1, w3, w2, layer, slots, tm=1024):
    h, n, ga, gb, s = saved
    t, d = h.shape
    fs = w1.shape[2]
    tm = _tile(t, tm)

    def down(vals, ins):
        ds = 0.5 * vals[0]
        return [ds * ins[2][...].astype(F32), ds * ins[3][...].astype(F32)]

    sm = _act_shape("sm", t, fs, BF16)
    asp = _act_spec("sm", tm, fs, lambda k, i: (i, k))
    da, db = _gemm(
        "ffn_down_dx", (N_CHIPS, t // tm), [dob, w2, ga, gb],
        [pl.BlockSpec((tm, d), lambda k, i: (i, 0)), _w_spec(w2, layer, lambda k, i: k), asp, asp],
        [[(0, 1, NT)]], None, [sm, sm], [asp, asp], down)
    dw2 = _mm_dw("ffn_dw2", s, "sm", dob, None, slots[2], alpha=0.5)
    dw1 = _mm_dw("ffn_dw1", da, "sm", n, None, slots[0])
    dw3 = _mm_dw("ffn_dw3", db, "sm", n, None, slots[1])
    dh, dhb, dg = _mm_cs_dx("ffn_up_dx", [(da, w1), (db, w3)], "sm", layer, transposed=True, norm=(dout, h, g))
    return dh, dhb, dg, dw1, dw3, dw2


def _ple_fwd(h, g, p2, wproj, wgate, layer):
    n = _rmsnorm("ple_norm", h, g)
    pp = _mm_cs("ple_proj", p2, wproj, layer, "flat", F32)
    gl, out = _mm_rs("ple_gate", n, "flat", wgate, layer, res=h, gated=pp)
    return out, (h, n, gl, pp)


def _ple_bwd(dout, saved, g, p2, wproj, wgate, layer, slots):
    h, n, gl, pp = saved
    d = h.shape[1]

    def fn(do, gg, q):
        sg = _sigmoid(gg)
        return [do * sg, do * q * sg * (1.0 - sg)], []
    (dpp, dgl), _ = _rows("ple_mix_bwd", fn, [dout, gl, pp], [(d, BF16), (d, BF16)])
    dwproj = _mm_dw("ple_dwproj", p2, None, dpp, "flat", slots[0])
    dwgate = _mm_dw("ple_dwgate", n, "flat", dgl, None, slots[1])
    dn = _mm_rs_dx("ple_gate_dx", dgl, wgate, layer, "flat", F32)
    dh, dhb, dg = _rmsnorm_bwd("ple_norm_bwd", dout, dn, h, g)
    return dh, dhb, dg, dwproj, dwgate


def _head(h, g, target):
    d = h.shape[1]

    def fn(x, gg, tg):
        y = x * _rms_stats(x) * gg
        err = y - tg
        dy = err * (1.0 / d)
        dx, dg = _rms_bwd_math(dy, x, gg)
        loss = 0.5 * jnp.sum(jnp.sum(err * err, axis=-1, keepdims=True) * (1.0 / d), axis=0, keepdims=True)
        return [dx], [dg, jnp.broadcast_to(loss, (1, 128))]
    (dh,), (dg, loss) = _rows("loss_head", fn, [h, g, target], [(d, F32)], [(1, d), (1, 128)])
    return loss[0, 0], dh, dg


S5_LANES = 2 * S5_STATE
S5_GB = 128 // S5_GROUP


def _s5_prep(a_re, a_im, log_dt, b_re, b_im, c_re, c_im):
    c, gb = S5_CHUNK, S5_GB
    g = a_re.shape[0]
    nb = g // gb
    lam_re = jnp.minimum(a_re, -1e-4)
    lam_im = a_im
    dt = jnp.exp(log_dt)[:, None, None]
    ks = jnp.arange(c + 1, dtype=F32)
    mag = jnp.exp(lam_re[..., None] * dt * ks)
    ph = lam_im[..., None] * dt * ks
    pw_re, pw_im = mag * jnp.cos(ph), mag * jnp.sin(ph)
    den = lam_re * lam_re + lam_im * lam_im
    nr, ni = pw_re[..., 1] - 1.0, pw_im[..., 1]
    fr = (nr * lam_re + ni * lam_im) / den
    fi = (ni * lam_re - nr * lam_im) / den
    bb_re = fr[..., None] * b_re - fi[..., None] * b_im
    bb_im = fr[..., None] * b_im + fi[..., None] * b_re
    ct_re, ct_im = c_re.transpose(0, 2, 1), c_im.transpose(0, 2, 1)
    ca_re = ct_re[:, :, None, :] * pw_re[..., None] - ct_im[:, :, None, :] * pw_im[..., None]
    ca_im = ct_re[:, :, None, :] * pw_im[..., None] + ct_im[:, :, None, :] * pw_re[..., None]
    hp = lax.Precision.HIGHEST
    kern = (jnp.einsum("gpj,gpkh->gkjh", bb_re, ca_re[:, :, :c], precision=hp)
            - jnp.einsum("gpj,gpkh->gkjh", bb_im, ca_im[:, :, :c], precision=hp))
    rev_re = pw_re[:, :, :c][:, :, ::-1].transpose(0, 2, 1)
    rev_im = pw_im[:, :, :c][:, :, ::-1].transpose(0, 2, 1)
    bt_re, bt_im = bb_re.transpose(0, 2, 1), bb_im.transpose(0, 2, 1)
    wn_re = rev_re[:, :, None, :] * bt_re[:, None] - rev_im[:, :, None, :] * bt_im[:, None]
    wn_im = rev_re[:, :, None, :] * bt_im[:, None] + rev_im[:, :, None, :] * bt_re[:, None]
    wn = jnp.concatenate([wn_re, wn_im], axis=-1)
    wo = jnp.concatenate([ca_re[:, :, 1:].transpose(0, 2, 3, 1), -ca_im[:, :, 1:].transpose(0, 2, 3, 1)], axis=-1)

    def blocks(x):
        return x.reshape(nb, gb, c, S5_GROUP, x.shape[3]).transpose(0, 2, 1, 3, 4).reshape(nb, c, gb * S5_GROUP, x.shape[3])

    ar, ai = pw_re[..., c], pw_im[..., c]
    return (jnp.tile(blocks(kern), (1, 1, 1, gb)), blocks(wn), blocks(wo),
            jnp.concatenate([ar, ar], axis=1), jnp.concatenate([-ai, ai], axis=1))


def _step_rows(ref, tau, n):
    return ref[pl.ds(tau, n, stride=S5_CHUNK), :].astype(BF16)


def _cat_groups(ref, dtype):
    return jnp.concatenate([ref[:, j, :] for j in range(S5_GB)], axis=1).astype(dtype)


def _cat_steps(ref, n):
    return jnp.concatenate([_step_rows(ref, tau, n) for tau in range(S5_CHUNK)], axis=1)


def _stack_steps(ref, n):
    return jnp.concatenate([_step_rows(ref, tau, n) for tau in range(S5_CHUNK)], axis=0)


def _cat_ops(ref, axis, reverse=False):
    order = range(S5_CHUNK - 1, -1, -1) if reverse else range(S5_CHUNK)
    return jnp.concatenate([ref[k] for k in order], axis=axis)


def _row_group(rows, lanes):
    row = (lax.broadcasted_iota(jnp.int32, (rows, lanes), 0) // S5_GROUP) % S5_GB
    lane = (lax.broadcasted_iota(jnp.int32, (rows, lanes), 1) // S5_GROUP) % S5_GB
    return row, lane


def _own_group(x):
    row, lane = _row_group(*x.shape)
    return jnp.where(row == lane, x, jnp.zeros_like(x))


def _spread(x):
    row, _ = _row_group(*x.shape)
    return jnp.concatenate([jnp.where(row == j, x, jnp.zeros_like(x)) for j in range(S5_GB)], axis=1)


def _gather_own(x):
    row, _ = _row_group(x.shape[0], S5_LANES)
    out = jnp.zeros((x.shape[0], S5_LANES), x.dtype)
    for j in range(S5_GB):
        out = out + jnp.where(row == j, x[:, j * S5_LANES:(j + 1) * S5_LANES], 0.0)
    return out


def _s5_specs(t, d):
    nct, g = t // S5_CHUNK, d // S5_GROUP
    tok = pl.BlockSpec((t, 128), lambda i: (0, i))
    st = pl.BlockSpec((nct, S5_GB, S5_LANES), lambda i: (0, i, 0))
    op = lambda w: pl.BlockSpec((None,) + w.shape[1:], lambda i: (i, 0, 0, 0))
    return nct, g, tok, st, op


def _s5_chunk_fwd(u, bd, bn):
    t, d = u.shape
    nct, g, tok, st, op = _s5_specs(t, d)
    c = S5_CHUNK

    def body(u_ref, bd_ref, bn_ref, y_ref, s_ref):
        ucat = _cat_steps(u_ref, nct)
        sloc = jnp.dot(ucat, _spread(_cat_ops(bn_ref, 0)), preferred_element_type=F32)
        for j in range(S5_GB):
            s_ref[:, j, :] = sloc[:, j * S5_LANES:(j + 1) * S5_LANES]
        lags = _own_group(_cat_ops(bd_ref, 0, reverse=True))
        for tt in range(c):
            y_ref[pl.ds(tt, nct, stride=c), :] = jnp.dot(ucat[:, :(tt + 1) * 128], lags[(c - 1 - tt) * 128:, :],
                                                         preferred_element_type=F32)

    return pl.pallas_call(
        body, name="s5_chunk", grid=(d // 128,), in_specs=[tok, op(bd), op(bn)], out_specs=[tok, st],
        out_shape=[jax.ShapeDtypeStruct((t, d), F32), jax.ShapeDtypeStruct((nct, g, S5_LANES), F32)],
        compiler_params=_params("parallel"))(u, bd, bn)


def _s5_state_out(sprev, co, yin):
    t, d = yin.shape
    nct, g, tok, st, op = _s5_specs(t, d)
    c = S5_CHUNK

    def body(s_ref, co_ref, yi_ref, y_ref):
        ys = lax.dot_general(_cat_groups(s_ref, BF16), _spread(_cat_ops(co_ref, 0)), NT,
                             preferred_element_type=F32)
        for tt in range(c):
            rows = pl.ds(tt, nct, stride=c)
            y_ref[rows, :] = yi_ref[rows, :] + ys[:, tt * 128:(tt + 1) * 128]

    return pl.pallas_call(
        body, name="s5_state_out", grid=(d // 128,), in_specs=[st, op(co), tok], out_specs=tok,
        out_shape=jax.ShapeDtypeStruct((t, d), F32), compiler_params=_params("parallel"))(sprev, co, yin)


def _s5_state_out_dx(dyb, co):
    t, d = dyb.shape
    nct, g, tok, st, op = _s5_specs(t, d)
    c = S5_CHUNK

    def body(dy_ref, co_ref, ds_ref):
        acc = jnp.dot(_cat_steps(dy_ref, nct), _spread(_cat_ops(co_ref, 0)), preferred_element_type=F32)
        for j in range(S5_GB):
            ds_ref[:, j, :] = acc[:, j * S5_LANES:(j + 1) * S5_LANES]

    return pl.pallas_call(
        body, name="s5_state_out_dx", grid=(d // 128,), in_specs=[tok, op(co)], out_specs=st,
        out_shape=jax.ShapeDtypeStruct((nct, g, S5_LANES), F32), compiler_params=_params("parallel"))(dyb, co)


def _s5_chunk_dx(dyb, dsloc, bd, bn, skip):
    t, d = dyb.shape
    nct, g, tok, st, op = _s5_specs(t, d)
    c = S5_CHUNK

    def body(dy_ref, ds_ref, bd_ref, bn_ref, sk_ref, du_ref):
        dus = lax.dot_general(_cat_groups(ds_ref, BF16), _spread(_cat_ops(bn_ref, 0)), NT, preferred_element_type=F32)
        dycat = _cat_steps(dy_ref, nct)
        lags = _own_group(_cat_ops(bd_ref, 1))
        for tau in range(c):
            rows = pl.ds(tau, nct, stride=c)
            du_ref[rows, :] = (sk_ref[rows, :] + dus[:, tau * 128:(tau + 1) * 128]
                               + lax.dot_general(dycat[:, tau * 128:], lags[:, :(c - tau) * 128], NT,
                                                 preferred_element_type=F32))

    return pl.pallas_call(
        body, name="s5_chunk_dx", grid=(d // 128,), in_specs=[tok, st, op(bd), op(bn), tok], out_specs=tok,
        out_shape=jax.ShapeDtypeStruct((t, d), F32), compiler_params=_params("parallel"))(dyb, dsloc, bd, bn, skip)


def _s5_chunk_dw(u, dyb, dsloc, bd, bn):
    t, d = u.shape
    nct, g, tok, st, op = _s5_specs(t, d)
    c = S5_CHUNK

    def body(u_ref, dy_ref, ds_ref, dbd_ref, dbn_ref):
        dbn = _gather_own(lax.dot_general(_cat_steps(u_ref, nct), _cat_groups(ds_ref, BF16), TN,
                                          preferred_element_type=F32))
        for tau in range(c):
            dbn_ref[tau] = dbn[tau * 128:(tau + 1) * 128, :]
        ustk, dystk = _stack_steps(u_ref, nct), _stack_steps(dy_ref, nct)
        for k in range(c):
            dbd_ref[k] = _own_group(lax.dot_general(ustk[:(c - k) * nct], dystk[k * nct:], TN,
                                                    preferred_element_type=F32))

    return pl.pallas_call(
        body, name="s5_chunk_dw", grid=(d // 128,), in_specs=[tok, tok, st], out_specs=[op(bd), op(bn)],
        out_shape=[jax.ShapeDtypeStruct(bd.shape, F32), jax.ShapeDtypeStruct(bn.shape, F32)],
        compiler_params=_params("parallel"))(u, dyb, dsloc)


def _s5_state_out_dw(sprev, dyb, co):
    t, d = dyb.shape
    nct, g, tok, st, op = _s5_specs(t, d)
    c = S5_CHUNK

    def body(s_ref, dy_ref, dco_ref):
        dco = _gather_own(lax.dot_general(_cat_steps(dy_ref, nct), _cat_groups(s_ref, BF16), TN,
                                          preferred_element_type=F32))
        for tt in range(c):
            dco_ref[tt] = dco[tt * 128:(tt + 1) * 128, :]

    return pl.pallas_call(
        body, name="s5_state_out_dw", grid=(d // 128,), in_specs=[st, tok], out_specs=op(co),
        out_shape=jax.ShapeDtypeStruct(co.shape, F32), compiler_params=_params("parallel"))(sprev, dyb)


def _s5_scan_fwd(sloc, m1, m2):
    bl, nc, g, w = sloc.shape

    def body(s_ref, m1_ref, m2_ref, o_ref):
        a1, a2 = m1_ref[...], m2_ref[...]

        def step(c, states):
            new = []
            for b, s in enumerate(states):
                o_ref[b, c] = s
                new.append(a1 * s + a2 * pltpu.roll(s, S5_STATE, 1) + s_ref[b, c])
            return tuple(new)
        lax.fori_loop(0, nc, step, tuple(jnp.zeros((g, w), F32) for _ in range(bl)))

    vm = pl.BlockSpec(memory_space=pltpu.VMEM)
    return pl.pallas_call(
        body, name="s5_scan", in_specs=[vm, vm, vm], out_specs=vm,
        out_shape=jax.ShapeDtypeStruct(sloc.shape, F32),
        compiler_params=pltpu.CompilerParams(vmem_limit_bytes=VMEM_LIMIT))(sloc, m1, m2)


def _s5_scan_bwd(dsprev, sprev, m1, m2):
    bl, nc, g, w = dsprev.shape

    def body(d_ref, s_ref, m1_ref, m2_ref, g_ref, p1_ref, p2_ref):
        a1, a2 = m1_ref[...], m2_ref[...]
        zero = jnp.zeros((g, w), F32)

        def step(i, carry):
            gps, p1, p2 = carry
            c = nc - 2 - i
            new = []
            for b, gp in enumerate(gps):
                g_ref[b, c] = gp
                sp = s_ref[b, c]
                p1 = p1 + gp * sp
                p2 = p2 + gp * pltpu.roll(sp, S5_STATE, 1)
                new.append(d_ref[b, c] + a1 * gp - a2 * pltpu.roll(gp, S5_STATE, 1))
            return tuple(new), p1, p2

        for b in range(bl):
            g_ref[b, nc - 1] = zero
        _, p1, p2 = lax.fori_loop(0, nc - 1, step, (tuple(d_ref[b, nc - 1] for b in range(bl)), zero, zero))
        p1_ref[...] = p1
        p2_ref[...] = p2

    vm = pl.BlockSpec(memory_space=pltpu.VMEM)
    sd = jax.ShapeDtypeStruct
    return pl.pallas_call(
        body, name="s5_scan_bwd", in_specs=[vm, vm, vm, vm], out_specs=[vm, vm, vm],
        out_shape=[sd(dsprev.shape, F32), sd((g, w), F32), sd((g, w), F32)],
        compiler_params=pltpu.CompilerParams(vmem_limit_bytes=VMEM_LIMIT))(dsprev, sprev, m1, m2)


def _gelu_tanh_parts(y):
    c0 = math.sqrt(2.0 / math.pi)
    inner = c0 * (y + 0.044715 * y * y * y)
    th = jnp.tanh(inner)
    return th, c0 * (1.0 + 3 * 0.044715 * y * y)


def _s5_fwd(h, g, ops, d_skip, w_in, w_glu, bl):
    bd, bn, co, m1, m2 = ops
    t, d = h.shape
    nct, groups = t // S5_CHUNK, d // S5_GROUP
    hn = _rmsnorm("mix_norm", h, g)
    u = _mm_rs("s5_in", hn, "flat", w_in, 0)
    yin, sloc = _s5_chunk_fwd(u, bd.astype(BF16), bn.astype(BF16))
    sprev = _s5_scan_fwd(sloc.reshape(bl, nct // bl, groups, S5_LANES), m1, m2).reshape(nct, groups, S5_LANES)
    y = _s5_state_out(sprev, co.astype(BF16), yin)

    def fn(yy, uu, dd):
        y2 = yy + dd * uu
        th, _ = _gelu_tanh_parts(y2)
        return [0.5 * y2 * (1.0 + th)], []
    z = _rows("s5_gelu", fn, [y, u, d_skip], [(d, BF16)])[0][0]
    zz = _mm_cs("s5_glu", z, w_glu, 0, "flat", F32)

    def glu(hh, zv):
        return [hh + zv[:, :d] * _sigmoid(zv[:, d:])], []
    out = _rows("s5_glu_mix", glu, [h, zz], [(d, F32)])[0][0]
    return out, (h, hn, u, sprev, y, z, zz)


def _s5_bwd(dout, saved, g, ops, d_skip, w_in, w_glu, bl):
    h, hn, u, sprev, y, z, zz = saved
    bd, bn, co, m1, m2 = ops
    t, d = h.shape
    nct, groups = t // S5_CHUNK, d // S5_GROUP

    def glu_bwd(do, zv):
        sg = _sigmoid(zv[:, d:])
        return [jnp.concatenate([do * sg, do * zv[:, :d] * sg * (1.0 - sg)], axis=1)], []
    dzz = _rows("s5_glu_bwd", glu_bwd, [dout, zz], [(2 * d, BF16)])[0][0]
    dwglu = _mm_dw("s5_dwglu", z, None, dzz, "flat", (None, 0, 1))
    dz = _mm_cs_dx("s5_glu_dx", [(dzz, w_glu)], "flat", 0)

    def gelu_bwd(dzv, yy, uu, dd):
        y2 = yy + dd * uu
        th, dinner = _gelu_tanh_parts(y2)
        dy2 = dzv * (0.5 * (1.0 + th) + 0.5 * y2 * (1.0 - th * th) * dinner)
        return [dy2, dy2 * dd], [jnp.sum(dy2 * uu, axis=0, keepdims=True)]
    (dyb, du_skip), (dd,) = _rows("s5_gelu_bwd", gelu_bwd, [dz, y, u, d_skip], [(d, F32), (d, F32)], [(1, d)])
    bd_b, bn_b, co_b = bd.astype(BF16), bn.astype(BF16), co.astype(BF16)
    dsprev = _s5_state_out_dx(dyb, co_b)
    shape4 = (bl, nct // bl, groups, S5_LANES)
    dsloc, dm1, dm2 = _s5_scan_bwd(dsprev.reshape(shape4), sprev.reshape(shape4), m1, m2)
    dsloc = dsloc.reshape(nct, groups, S5_LANES)
    du = _s5_chunk_dx(dyb, dsloc, bd_b, bn_b, du_skip).astype(BF16)
    dbd, dbn = _s5_chunk_dw(u, dyb, dsloc, bd, bn)
    dco = _s5_state_out_dw(sprev, dyb, co)
    dwin = _mm_dw("s5_dwin", hn, "flat", du, None, (None, 0, 1))
    dhn = _mm_rs_dx("s5_in_dx", du, w_in, 0, "flat", F32)
    dh, dhb, dg = _rmsnorm_bwd("mix_norm_bwd", dout, dhn, h, g)
    return dh, dhb, dg, dwin, dwglu, dd, (dbd, dbn, dco, dm1, dm2)


def _sb_block(qi, idx, tb):
    kb = qi - idx
    return pl.multiple_of(jnp.maximum(kb, 0) * tb, tb), idx == 0, kb >= 0


def _sb_scores(q, kblk, diag, exists, row, col):
    z = lax.dot_general(q, kblk, NT, preferred_element_type=F32) * (SB_HEAD_DIM ** -0.5)
    l1 = jnp.log(1.0 + jnp.exp(-jnp.abs(z)))
    ls = jnp.minimum(z, 0.0) - l1
    mask = jnp.logical_and(jnp.logical_or(col < row, jnp.logical_not(diag)), exists)
    lk = jnp.where(mask, ls - z, 0.0)
    return ls, lk, mask


def _split_dot(v, tri):
    hi = v.astype(BF16)
    lo = (v - hi.astype(F32)).astype(BF16)
    return (jnp.dot(hi, tri, preferred_element_type=F32) + jnp.dot(lo, tri, preferred_element_type=F32))


SB_PAIR =2 * SB_HEAD_DIM


def _pair_masks(tb):
    lane = lax.broadcasted_iota(jnp.int32, (1, SB_PAIR), 1)
    row = lax.broadcasted_iota(jnp.int32, (tb, tb), 0)
    col = lax.broadcasted_iota(jnp.int32, (tb, tb), 1)
    return [lane < SB_HEAD_DIM, lane >= SB_HEAD_DIM], row, col


def _pair_more(qi, carry):
    j, crs = carry[0], carry[2]
    return jnp.logical_and(j <= qi, jnp.maximum(jnp.max(crs[0]), jnp.max(crs[1])) > SB_CUT)


def _pair_specs(bl, l, d, tb):
    nq, off = l // tb, d // SB_PAIR
    qspec = pl.BlockSpec((tb, SB_PAIR), lambda b, p, i: (b * nq + i, p))
    kspec = pl.BlockSpec((l, SB_PAIR), lambda b, p, i: (b, off + p))
    vspec = pl.BlockSpec((l, SB_PAIR), lambda b, p, i: (b, 2 * off + p))
    return qspec, kspec, vspec


def _sb_attn_fwd2(qkv, bl):
    t, d3 = qkv.shape
    d, l = d3 // 3, t // bl
    tb = min(SB_BLOCK, l)
    nq = l // tb

    def body(q_ref, k_ref, v_ref, o_ref, ob_ref):
        qi = pl.program_id(2)
        heads, row, col = _pair_masks(tb)
        qv = q_ref[...]
        qh = [jnp.where(m, qv, jnp.zeros_like(qv)) for m in heads]
        tri = (row > col).astype(BF16)

        def step(carry):
            j, acc, crs = carry
            crs = list(crs)
            where = [_sb_block(qi, j + u, tb) for u in range(SB_UNROLL)]
            kblks = [k_ref[pl.ds(ks, tb), :] for ks, _, _ in where]
            scores = [[_sb_scores(qh[hd], kblks[u], where[u][1], where[u][2], row, col) for hd in range(2)]
                      for u in range(SB_UNROLL)]
            laters = [[_split_dot(sc[1], tri) for sc in su] for su in scores]
            for u in range(SB_UNROLL):
                vblk = v_ref[pl.ds(where[u][0], tb), :]
                outs = []
                for hd in range(2):
                    ls, lk, mask = scores[u][hd]
                    att = jnp.where(mask, jnp.exp(ls + laters[u][hd] + crs[hd]), 0.0)
                    outs.append(jnp.dot(att.astype(BF16), vblk, preferred_element_type=F32))
                    crs[hd] = crs[hd] + jnp.sum(lk, axis=1, keepdims=True)
                acc = acc + jnp.where(heads[0], outs[0], outs[1])
            return j + SB_UNROLL, acc, tuple(crs)

        zc = jnp.zeros((tb, 1), F32)
        _, acc, _ = lax.while_loop(functools.partial(_pair_more, qi), step,
                                   (jnp.int32(0), jnp.zeros((tb, SB_PAIR), F32), (zc, zc)))
        o_ref[...] = acc
        ob_ref[...] = acc.astype(BF16)

    qspec, kspec, vspec = _pair_specs(bl, l, d, tb)
    return pl.pallas_call(
        body, name="sb_attn", grid=(bl, d // SB_PAIR, nq), in_specs=[qspec, kspec, vspec], out_specs=[qspec, qspec],
        out_shape=[jax.ShapeDtypeStruct((t, d), F32), jax.ShapeDtypeStruct((t, d), BF16)],
        compiler_params=_params("parallel", "parallel", "parallel"))(qkv, qkv, qkv)


def _sb_attn_bwd2(qkv, o, do, bl):
    t, d3 = qkv.shape
    d, l = d3 // 3, t // bl
    tb = min(SB_BLOCK, l)
    nq = l // tb
    scale = SB_HEAD_DIM ** -0.5

    def body(q_ref, k_ref, v_ref, o_ref, do_ref, dq_ref, dk_ref, dv_ref, dk_acc, dv_acc):
        qi = pl.program_id(2)

        @pl.when(qi == 0)
        def _():
            dk_acc[...] = jnp.zeros_like(dk_acc)
            dv_acc[...] = jnp.zeros_like(dv_acc)

        heads, row, col = _pair_masks(tb)
        qv = q_ref[...]
        dov = do_ref[...].astype(BF16)
        qh = [jnp.where(m, qv, jnp.zeros_like(qv)) for m in heads]
        doh = [jnp.where(m, dov, jnp.zeros_like(dov)) for m in heads]
        ov = o_ref[...]
        dsum = [jnp.sum(dh.astype(F32) * ov, axis=1, keepdims=True) for dh in doh]
        tri = (row > col).astype(BF16)
        tri_inc = (row >= col).astype(BF16)

        def step(carry):
            j, dq, crs, ces = carry
            crs, ces = list(crs), list(ces)
            n = range(SB_UNROLL)
            where = [_sb_block(qi, j + u, tb) for u in n]
            rows = [pl.ds(ks, tb) for ks, _, _ in where]
            kblks = [k_ref[rows[u], :] for u in n]
            vblks = [v_ref[rows[u], :] for u in n]
            scores = [[_sb_scores(qh[hd], kblks[u], where[u][1], where[u][2], row, col) for hd in range(2)] for u in n]
            laters = [[_split_dot(sc[1], tri) for sc in su] for su in scores]
            datts = [[lax.dot_general(doh[hd], vblks[u], NT, preferred_element_type=F32) for hd in range(2)] for u in n]
            atts = [[None, None] for _ in n]
            for u in n:
                for hd in range(2):
                    ls, lk, mask = scores[u][hd]
                    atts[u][hd] = jnp.where(mask, jnp.exp(ls + laters[u][hd] + crs[hd]), 0.0).astype(BF16)
                    crs[hd] = crs[hd] + jnp.sum(lk, axis=1, keepdims=True)
            es = [[atts[u][hd].astype(F32) * datts[u][hd] for hd in range(2)] for u in n]
            sufs = [[_split_dot(e, tri_inc) for e in eu] for eu in es]
            dzs = [[None, None] for _ in n]
            for u in n:
                for hd in range(2):
                    ls, _, mask = scores[u][hd]
                    pre = dsum[hd] - ces[hd] - sufs[u][hd]
                    sg = jnp.exp(ls)
                    dzs[u][hd] = (jnp.where(mask, es[u][hd] * (1.0 - sg) - pre * sg, 0.0) * scale).astype(BF16)
                    ces[hd] = ces[hd] + jnp.sum(es[u][hd], axis=1, keepdims=True)
            for u in n:
                dq = dq + jnp.where(heads[0], jnp.dot(dzs[u][0], kblks[u], preferred_element_type=F32),
                                    jnp.dot(dzs[u][1], kblks[u], preferred_element_type=F32))
                dk_acc[rows[u], :] += (lax.dot_general(dzs[u][0], qh[0], TN, preferred_element_type=F32)
                                       + lax.dot_general(dzs[u][1], qh[1], TN, preferred_element_type=F32))
                dv_acc[rows[u], :] += (lax.dot_general(atts[u][0], doh[0], TN, preferred_element_type=F32)
                                       + lax.dot_general(atts[u][1], doh[1], TN, preferred_element_type=F32))
            return j + SB_UNROLL, dq, tuple(crs), tuple(ces)

        zc = jnp.zeros((tb, 1), F32)
        _, dq, _, _ = lax.while_loop(functools.partial(_pair_more, qi), step,
                                     (jnp.int32(0), jnp.zeros((tb, SB_PAIR), F32), (zc, zc), (zc, zc)))
        dq_ref[...] = dq.astype(BF16)

        @pl.when(qi == nq - 1)
        def _():
            dk_ref[...] = dk_acc[...].astype(BF16)
            dv_ref[...] = dv_acc[...].astype(BF16)

    qspec, kspec, vspec = _pair_specs(bl, l, d, tb)
    blk = pl.BlockSpec((tb, SB_PAIR), lambda b, p, i: (b * nq + i, p))
    full = pl.BlockSpec((l, SB_PAIR), lambda b, p, i: (b, p))
    sd = jax.ShapeDtypeStruct((t, d), BF16)
    dq, dk, dv = pl.pallas_call(
        body, name="sb_attn_bwd", grid=(bl, d // SB_PAIR, nq), in_specs=[qspec, kspec, vspec, blk, blk],
        out_specs=[blk, full, full], out_shape=[sd, sd, sd],
        scratch_shapes=[pltpu.VMEM((l, SB_PAIR), F32), pltpu.VMEM((l, SB_PAIR), F32)],
        compiler_params=_params("parallel", "parallel", "arbitrary"))(qkv, qkv, qkv, o, do)
    return jnp.concatenate([dq, dk, dv], axis=1)


def _sb_fwd(h, g, w_qkv, w_o, bl):
    t, d = h.shape
    hn = _rmsnorm("mix_norm", h, g)
    qkv = _mm_cs("sb_qkv", hn, w_qkv, 0, "flat", BF16)
    o, ob = _sb_attn_fwd2(qkv, bl)
    out = _mm_rs("sb_out", ob, "flat", w_o, 0, res=h)
    return out, (h, hn, qkv, o, ob)


def _sb_bwd(dout, dob, saved, g, w_qkv, w_o, bl):
    h, hn, qkv, o, ob = saved
    dwo = _mm_dw("sb_dwo", ob, "flat", dob, None, (None, 0, 1))
    do = _mm_rs_dx("sb_out_dx", dob, w_o, 0, "flat", F32)
    dqkv = _sb_attn_bwd2(qkv, o, do, bl)
    dwqkv = _mm_dw("sb_dwqkv", hn, None, dqkv, "flat", (None, 0, 1))
    dh, dhb, dg = _mm_cs_dx("sb_qkv_dx", [(dqkv, w_qkv)], "flat", 0, norm=(dout, h, g))
    return dh, dhb, dg, dwqkv, dwo


def _adamw_update(wv, gr, mv, vv):
    c1 = 1.0 / (1.0 - ADAM_B1 ** ADAM_STEP)
    c2 = 1.0 / (1.0 - ADAM_B2 ** ADAM_STEP)
    mn = ADAM_B1 * mv + (1.0 - ADAM_B1) * gr
    vn = ADAM_B2 * vv + (1.0 - ADAM_B2) * gr * gr
    delta = -ADAM_LR * ((mn * c1) / (jnp.sqrt(vn * c2) + ADAM_EPS) + ADAM_WD * wv)
    return delta, mn, vn


def _adamw_small(w, gr, m, v):
    def fn(wv, gv, mv, vv):
        return list(_adamw_update(wv, gv, mv, vv)), []
    return _rows("adamw_small", fn, [w, gr, m, v], [(w.shape[1], F32)] * 3)[0]


def _place():
    x, y, c = lax.axis_index("x"), lax.axis_index("y"), lax.axis_index("c")
    chips = [(1 - x, y), (x, 1 - y), (1 - x, 1 - y)]
    return x, y, c, chips


def _remote(src, dst, send_sem, recv_sem, to):
    return pltpu.make_async_remote_copy(src_ref=src, dst_ref=dst, send_sem=send_sem, recv_sem=recv_sem,
                                        device_id=to, device_id_type=MESH)


def _half(ref, c, rh, lead):
    return ref.at[(slice(None),) * lead + (pl.ds(c * rh, rh),)]


def _allgather_weights(ws):
    n = len(ws)

    def body(*refs):
        ins, outs = refs[:n], refs[n:2 * n]
        send, recv = refs[2 * n:]
        x, y, c, _ = _place()
        chip_x, chip_y, chip_d = (1 - x, y), (x, 1 - y), (1 - x, 1 - y)
        sibling = (x, y, 1 - c)
        index = lambda chip: 2 * chip[0] + chip[1]
        sent = []

        def quarter(ref, half, q, rq):
            return ref.at[:, pl.ds((2 * half + q) * rq, rq)]

        def copy(t, kind, src, dst, to):
            return _remote(src, dst, send.at[t, kind], recv.at[t, kind], to)

        def start(cp):
            cp.start()
            sent.append(cp)

        for t in range(n):
            rq = ws[t].shape[1] // 4
            for q in range(2):
                for base, chip in ((0, chip_x), (2, chip_y)):
                    start(copy(t, base + q, quarter(ins[t], c, q, rq), quarter(outs[t].at[index((x, y))], c, q, rq), (*chip, c)))
        for t in range(n):
            rq = ws[t].shape[1] // 4
            landings = [(chip_x, 0, 0, chip_x, ((4, chip_y), (6, None))), (chip_y, 1, 3, chip_y, ((5, chip_x), (9, None))),
                        (chip_x, 1, 1, chip_x, ((7, None),)), (chip_y, 0, 2, chip_y, ((8, None),)),
                        (chip_d, 0, 4, chip_y, ((10, None),)), (chip_d, 1, 5, chip_x, ((11, None),))]
            for origin, q, kind, sender, onward in landings:
                piece = quarter(outs[t].at[index(origin)], c, q, rq)
                copy(t, kind, piece, piece, (*sender, c)).wait_recv()
                for kind2, chip in onward:
                    start(copy(t, kind2, piece, piece, sibling if chip is None else (*chip, c)))
        for t in range(n):
            rq = ws[t].shape[1] // 4
            for kind, (origin, q) in zip(range(6, 12), ((chip_x, 0), (chip_x, 1), (chip_y, 0), (chip_y, 1), (chip_d, 0), (chip_d, 1))):
                piece = quarter(outs[t].at[index(origin)], 1 - c, q, rq)
                copy(t, kind, piece, piece, sibling).wait_recv()
        for cp in sent:
            cp.wait_send()

    res = pl.pallas_call(
        body, name="allgather_weights", in_specs=[ANY] * n, out_specs=[ANY] * n,
        out_shape=[jax.ShapeDtypeStruct((N_CHIPS,) + w.shape, w.dtype) for w in ws],
        scratch_shapes=[pltpu.SemaphoreType.DMA((n, 12)), pltpu.SemaphoreType.DMA((n, 12))],
    )(*ws)
    own = 2 * lax.axis_index("x") + lax.axis_index("y")
    return [lax.dynamic_update_slice(g, w[None], (own, 0, 0, 0)) for g, w in zip(res, ws)]


def _pair_exchange(gs):
    n = len(gs)

    def body(*refs):
        ins, outs = refs[:n], refs[n:2 * n]
        send, recv = refs[2 * n:]
        x, y, c, _ = _place()
        copies = [_remote(_half(ins[t], 1 - c, gs[t].shape[2] // 2, 2), outs[t], send.at[t], recv.at[t], (x, y, 1 - c))
                  for t in range(n)]
        for cp in copies:
            cp.start()
        for cp in copies:
            cp.wait()

    return pl.pallas_call(
        body, name="grad_pair_exchange", in_specs=[ANY] * n, out_specs=[ANY] * n,
        out_shape=[jax.ShapeDtypeStruct(g.shape[:2] + (g.shape[2] // 2, g.shape[3]), F32) for g in gs],
        scratch_shapes=[pltpu.SemaphoreType.DMA((n,)), pltpu.SemaphoreType.DMA((n,))],
    )(*gs)


def _pair_sum(g, theirs, c_idx):
    n4, ly, r, cc = g.shape
    rh = r // 2
    tm = _tile(rh, 512)
    nt = rh // tm

    def body(c_ref, g_ref, t_ref, o_ref):
        o_ref[...] = (g_ref[...] + t_ref[...]).astype(o_ref.dtype)

    blk = (None, tm, cc)
    grid_spec = pltpu.PrefetchScalarGridSpec(
        num_scalar_prefetch=1, grid=(n4 * ly, nt),
        in_specs=[pl.BlockSpec(blk, lambda a, i, cr: (a, cr[0] * nt + i, 0)), pl.BlockSpec(blk, lambda a, i, cr: (a, i, 0))],
        out_specs=pl.BlockSpec(blk, lambda a, i, cr: (a, i, 0)))
    out = pl.pallas_call(
        body, name="grad_pair_sum", grid_spec=grid_spec, out_shape=jax.ShapeDtypeStruct((n4 * ly, rh, cc), BF16),
        compiler_params=_params("parallel", "parallel"))(c_idx, g.reshape(n4 * ly, r, cc), theirs.reshape(n4 * ly, rh, cc))
    return out.reshape(n4, ly, rh, cc)


def _quarter(ref, q, rq):
    return ref.at[:, pl.ds(q * rq, rq)]


def _chip_exchange_first(ps):
    n = len(ps)

    def body(*refs):
        ins, outs = refs[:n], refs[n:2 * n]
        send, recv = refs[2 * n:]
        x, y, c, _ = _place()
        index = lambda cx, cy: 2 * cx + cy
        copies = []
        for t in range(n):
            rq = ps[t].shape[2] // 2
            for base, q, chip in ((0, 0, (1 - x, y)), (2, 1, (x, 1 - y))):
                for j, slice_of in enumerate((chip, (1 - x, 1 - y))):
                    copies.append(_remote(_quarter(ins[t].at[index(*slice_of)], q, rq), outs[t].at[base + j],
                                          send.at[t, base + j], recv.at[t, base + j], (*chip, c)))
        for cp in copies:
            cp.start()
        for cp in copies:
            cp.wait()

    return pl.pallas_call(
        body, name="grad_chip_exchange", in_specs=[ANY] * n, out_specs=[ANY] * n,
        out_shape=[jax.ShapeDtypeStruct((4, p.shape[1], p.shape[2] // 2, p.shape[3]), p.dtype) for p in ps],
        scratch_shapes=[pltpu.SemaphoreType.DMA((n, 4)), pltpu.SemaphoreType.DMA((n, 4))],
    )(*ps)


def _chip_relay_sum(p, first, where):
    _, ly, rh, cc = p.shape
    rq = rh // 2
    tm = _tile(rq, 512)
    nt = rq // tm

    def body(w_ref, p_ref, f_ref, out_ref):
        out_ref[...] = (p_ref[...].astype(F32) + f_ref[...].astype(F32)).astype(out_ref.dtype)

    blk = (None, None, tm, cc)
    grid_spec = pltpu.PrefetchScalarGridSpec(
        num_scalar_prefetch=1, grid=(2, ly, nt),
        in_specs=[pl.BlockSpec(blk, lambda s, l, i, w: (w[2 - s], l, s * nt + i, 0)),
                  pl.BlockSpec(blk, lambda s, l, i, w: (1 + 2 * s, l, i, 0))],
        out_specs=pl.BlockSpec(blk, lambda s, l, i, w: (s, l, i, 0)))
    return pl.pallas_call(
        body, name="grad_relay_sum", grid_spec=grid_spec, out_shape=jax.ShapeDtypeStruct((2, ly, rq, cc), p.dtype),
        compiler_params=_params("parallel", "parallel", "parallel"))(where, p, first)


def _chip_exchange_second(ss):
    n = len(ss)

    def body(*refs):
        ins, outs = refs[:n], refs[n:2 * n]
        send, recv = refs[2 * n:]
        x, y, c, _ = _place()
        copies = []
        for t in range(n):
            for j, chip in enumerate(((x, 1 - y), (1 - x, y))):
                copies.append(_remote(ins[t].at[j], outs[t].at[j], send.at[t, j], recv.at[t, j], (*chip, c)))
        for cp in copies:
            cp.start()
        for cp in copies:
            cp.wait()

    return pl.pallas_call(
        body, name="grad_chip_exchange_2", in_specs=[ANY] * n, out_specs=[ANY] * n,
        out_shape=[jax.ShapeDtypeStruct(s.shape, s.dtype) for s in ss],
        scratch_shapes=[pltpu.SemaphoreType.DMA((n, 2)), pltpu.SemaphoreType.DMA((n, 2))],
    )(*ss)


def _chip_sum(p, first, second, where):
    _, ly, rh, cc = p.shape
    rq = rh // 2
    tm = _tile(rq, 512)
    nt = rq // tm

    def body(w_ref, p_ref, f_ref, s_ref, out_ref):
        out_ref[...] = (p_ref[...].astype(F32) + f_ref[...].astype(F32)) + s_ref[...].astype(F32)

    blk = (None, None, tm, cc)
    grid_spec = pltpu.PrefetchScalarGridSpec(
        num_scalar_prefetch=1, grid=(ly, 2, nt),
        in_specs=[pl.BlockSpec(blk, lambda l, q, i, w: (w[0], l, q * nt + i, 0)),
                  pl.BlockSpec(blk, lambda l, q, i, w: (2 * q, l, i, 0)),
                  pl.BlockSpec(blk, lambda l, q, i, w: (q, l, i, 0))],
        out_specs=pl.BlockSpec((None, tm, cc), lambda l, q, i, w: (l, q * nt + i, 0)))
    return pl.pallas_call(
        body, name="grad_chip_sum", grid_spec=grid_spec, out_shape=jax.ShapeDtypeStruct((ly, rh, cc), F32),
        compiler_params=_params("parallel", "parallel", "parallel"))(where, p, first, second)


def _pair_swap(halves):
    n = len(halves)

    def body(*refs):
        ins, outs = refs[:n], refs[n:2 * n]
        send, recv = refs[2 * n:]
        x, y, c, _ = _place()
        copies = [_remote(ins[t], outs[t], send.at[t], recv.at[t], (x, y, 1 - c)) for t in range(n)]
        for cp in copies:
            cp.start()
        for cp in copies:
            cp.wait()

    return pl.pallas_call(
        body, name="grad_pair_swap", in_specs=[ANY] * n, out_specs=[ANY] * n,
        out_shape=[jax.ShapeDtypeStruct(h.shape, F32) for h in halves],
        scratch_shapes=[pltpu.SemaphoreType.DMA((n,)), pltpu.SemaphoreType.DMA((n,))],
    )(*halves)


def _adamw_big(w, m, v, mine, theirs, c_idx):
    ly, r, cc = w.shape
    rh = r // 2
    tm = _tile(rh, 512)
    nt = rh // tm

    def body(c_ref, w_ref, m_ref, v_ref, a_ref, b_ref, g_out, d_out, m_out, v_out):
        gr = jnp.where(pl.program_id(1) == c_ref[0], a_ref[...], b_ref[...])
        delta, mn, vn = _adamw_update(w_ref[...], gr, m_ref[...], v_ref[...])
        g_out[...] = gr
        d_out[...] = delta
        m_out[...] = mn
        v_out[...] = vn

    blk = (None, tm, cc)
    full = pl.BlockSpec(blk, lambda l, hc, i, cr: (l, hc * nt + i, 0))
    half = pl.BlockSpec(blk, lambda l, hc, i, cr: (l, i, 0))
    grid_spec = pltpu.PrefetchScalarGridSpec(
        num_scalar_prefetch=1, grid=(ly, 2, nt), in_specs=[full, full, full, half, half], out_specs=[full] * 4)
    sd = jax.ShapeDtypeStruct(w.shape, F32)
    return pl.pallas_call(
        body, name="adamw", grid_spec=grid_spec, out_shape=[sd] * 4,
        compiler_params=_params("parallel", "parallel", "parallel"))(c_idx, w, m, v, mine, theirs)


def _allreduce_small(v):
    rows, w = v.shape

    def body(x_ref, sum_ref, all_ref, send, recv, local):
        x, y, c, chips = _place()
        me, sibling = (x, y, c), (x, y, 1 - c)

        def slot(px, py, pc):
            return all_ref.at[4 * px + 2 * py + pc]

        def copy(k, block, to, src=None):
            return _remote(slot(*block) if src is None else src, slot(*block), send.at[k], recv.at[k], to)

        mine = pltpu.make_async_copy(x_ref, slot(*me), local)
        mine.start()
        first = [copy(0, me, sibling, src=x_ref)]
        first += [copy(1 + j, me, (*chip, c), src=x_ref) for j, chip in enumerate(chips)]
        for cp in first:
            cp.start()
        passed = [copy(4 + j, (*chip, c), sibling) for j, chip in enumerate(chips)]
        for j, chip in enumerate(chips):
            copy(1 + j, (*chip, c), me).wait_recv()
            passed[j].start()
        copy(0, sibling, me).wait_recv()
        for j, chip in enumerate(chips):
            copy(4 + j, (*chip, 1 - c), me).wait_recv()
        for cp in first + passed:
            cp.wait_send()
        mine.wait()
        tot = all_ref[0]
        for k in range(1, N_DEV):
            tot = tot + all_ref[k]
        sum_ref[...] = tot

    vm = pl.BlockSpec(memory_space=pltpu.VMEM)
    return pl.pallas_call(
        body, name="allreduce_small", in_specs=[vm], out_specs=[vm, vm],
        out_shape=[jax.ShapeDtypeStruct((rows, w), F32), jax.ShapeDtypeStruct((N_DEV, rows, w), F32)],
        scratch_shapes=[pltpu.SemaphoreType.DMA((7,)), pltpu.SemaphoreType.DMA((7,)), pltpu.SemaphoreType.DMA],
        compiler_params=pltpu.CompilerParams(vmem_limit_bytes=VMEM_LIMIT),
    )(v)[0]


BIG = ["ffn1_w1", "ffn1_w3", "ffn1_w2", "ffn2_w1", "ffn2_w3", "ffn2_w2", "ple_proj", "ple_gate",
       "s5_w_in", "s5_w_glu", "sb_w_qkv", "sb_w_o"]
TRANSPOSED = ("ffn1_w1", "ffn1_w3", "ffn2_w1", "ffn2_w3")
SMALL = ["ffn1_norm", "mix_norm", "ffn2_norm", "ple_norm", "s5_a_re", "s5_a_im", "s5_log_dt", "s5_b_re", "s5_b_im",
         "s5_c_re", "s5_c_im", "s5_d", "final_norm"]
ORDER = ["ffn1_norm", "ffn1_w1", "ffn1_w3", "ffn1_w2", "mix_norm", "ffn2_norm", "ffn2_w1", "ffn2_w3", "ffn2_w2",
         "ple_norm", "ple_proj", "ple_gate", "s5_w_in", "s5_a_re", "s5_a_im", "s5_log_dt", "s5_b_re", "s5_b_im",
         "s5_c_re", "s5_c_im", "s5_d", "s5_w_glu", "sb_w_qkv", "sb_w_o", "final_norm"]


def _pack(arrays):
    flat = jnp.concatenate([a.reshape(-1) for a in arrays])
    pad = (-flat.shape[0]) % 1024
    return jnp.pad(flat, (0, pad)).reshape(-1, 128)


def _unpack(packed, like):
    flat = packed.reshape(-1)
    out, off = [], 0
    for a in like:
        out.append(flat[off:off + a.size].reshape(a.shape))
        off += a.size
    return out


def _fwd_bwd(x, p, target, w, gathered):
    bl, l, d = x.shape
    t = bl * l
    depth = w["ffn1_norm"].shape[0]
    s5_ops, s5_vjp = jax.vjp(_s5_prep, w["s5_a_re"][0], w["s5_a_im"][0], w["s5_log_dt"][0], w["s5_b_re"][0],
                             w["s5_b_im"][0], w["s5_c_re"][0], w["s5_c_im"][0])

    h = x.reshape(t, d)
    p2 = [p[i].reshape(t, p.shape[-1]).astype(BF16) for i in range(depth)]
    saved = []
    for i in range(depth):
        norm = lambda name: w[name][i:i + 1]
        h, s1 = _ffn_fwd(h, norm("ffn1_norm"), gathered["ffn1_w1"], gathered["ffn1_w3"], gathered["ffn1_w2"], i)
        if i % 2 == 0:
            h, s2 = _s5_fwd(h, norm("mix_norm"), s5_ops, w["s5_d"][i // 2:i // 2 + 1], gathered["s5_w_in"], gathered["s5_w_glu"], bl)
        else:
            h, s2 = _sb_fwd(h, norm("mix_norm"), gathered["sb_w_qkv"], gathered["sb_w_o"], bl)
        h, s3 = _ffn_fwd(h, norm("ffn2_norm"), gathered["ffn2_w1"], gathered["ffn2_w3"], gathered["ffn2_w2"], i)
        h, s4 = _ple_fwd(h, norm("ple_norm"), p2[i], gathered["ple_proj"], gathered["ple_gate"], i)
        saved.append((s1, s2, s3, s4))

    loss, dh, dfinal = _head(h, w["final_norm"].reshape(1, d), target.reshape(t, d))

    big = {k: None for k in BIG}
    small = {k: [None] * w[k].shape[0] if w[k].ndim > 1 else None for k in SMALL}
    small["final_norm"] = dfinal.reshape(d)
    for i in reversed(range(depth)):
        norm = lambda name: w[name][i:i + 1]
        slots = lambda *names: [(big[k], i, depth) for k in names]
        s1, s2, s3, s4 = saved[i]
        dh, dhb, dg, big["ple_proj"], big["ple_gate"] = _ple_bwd(
            dh, s4, norm("ple_norm"), p2[i], gathered["ple_proj"], gathered["ple_gate"], i, slots("ple_proj", "ple_gate"))
        small["ple_norm"][i] = dg[0]
        dh, dhb, dg, big["ffn2_w1"], big["ffn2_w3"], big["ffn2_w2"] = _ffn_bwd(
            dh, dhb, s3, norm("ffn2_norm"), gathered["ffn2_w1"], gathered["ffn2_w3"], gathered["ffn2_w2"], i,
            slots("ffn2_w1", "ffn2_w3", "ffn2_w2"))
        small["ffn2_norm"][i] = dg[0]
        if i % 2 == 0:
            dh, dhb, dg, big["s5_w_in"], big["s5_w_glu"], dd, dops = _s5_bwd(
                dh, s2, norm("mix_norm"), s5_ops, w["s5_d"][i // 2:i // 2 + 1], gathered["s5_w_in"], gathered["s5_w_glu"], bl)
            small["s5_d"][0] = dd[0]
            raw = s5_vjp(dops)
            for name, gr in zip(["s5_a_re", "s5_a_im", "s5_log_dt", "s5_b_re", "s5_b_im", "s5_c_re", "s5_c_im"], raw):
                small[name][0] = gr
        else:
            dh, dhb, dg, big["sb_w_qkv"], big["sb_w_o"] = _sb_bwd(
                dh, dhb, s2, norm("mix_norm"), gathered["sb_w_qkv"], gathered["sb_w_o"], bl)
        small["mix_norm"][i] = dg[0]
        dh, dhb, dg, big["ffn1_w1"], big["ffn1_w3"], big["ffn1_w2"] = _ffn_bwd(
            dh, dhb, s1, norm("ffn1_norm"), gathered["ffn1_w1"], gathered["ffn1_w3"], gathered["ffn1_w2"], i,
            slots("ffn1_w1", "ffn1_w3", "ffn1_w2"))
        small["ffn1_norm"][i] = dg[0]
    small_list = [jnp.stack(small[k]) if isinstance(small[k], list) else small[k] for k in SMALL]
    return loss, dh.reshape(bl, l, d), big, small_list


def _step(x, p, target, w, m, v):
    flip = lambda tree: {k: jnp.swapaxes(a, 1, 2) if k in TRANSPOSED else a for k, a in tree.items()}
    w, m, v = flip(w), flip(m), flip(v)
    gathered = dict(zip(BIG, _allgather_weights([_to_bf16(w[k]) for k in BIG])))
    loss, grad_x, big, small_list = _fwd_bwd(x, p, target, w, gathered)

    c_idx = lax.axis_index("c").astype(jnp.int32).reshape(1)
    cx, cy = lax.axis_index("x"), lax.axis_index("y")
    where = jnp.stack([2 * cx + cy, 2 * (1 - cx) + cy, 2 * cx + (1 - cy)]).astype(jnp.int32)
    partial = [big[k] for k in BIG]
    pair = [_pair_sum(g, t, c_idx) for g, t in zip(partial, _pair_exchange(partial))]
    first = _chip_exchange_first(pair)
    second = _chip_exchange_second([_chip_relay_sum(pr, f, where) for pr, f in zip(pair, first)])
    mine = [_chip_sum(pr, f, s, where) for pr, f, s in zip(pair, first, second)]
    theirs = _pair_swap(mine)
    out_g, out_d, out_m, out_v = {}, {}, {}, {}
    for k, a, b in zip(BIG, mine, theirs):
        out_g[k], out_d[k], out_m[k], out_v[k] = _adamw_big(w[k], m[k], v[k], a, b, c_idx)

    like = [w[k] for k in SMALL]
    pad = [jnp.zeros((1,), F32)]
    g_small = _allreduce_small(_pack(small_list + [loss.reshape(1)]))
    packed = (g_small,) + tuple(_adamw_small(_pack(like + pad), g_small, _pack([m[k] for k in SMALL] + pad),
                                             _pack([v[k] for k in SMALL] + pad)))
    for dst, pk in zip((out_g, out_d, out_m, out_v), packed):
        dst.update(dict(zip(SMALL, _unpack(pk, like))))
    loss = g_small.reshape(-1)[sum(a.size for a in like)]
    out_g, out_d, out_m, out_v = flip(out_g), flip(out_d), flip(out_m), flip(out_v)
    return (loss, grad_x, *[out_g[k] for k in ORDER], *[out_d[k] for k in ORDER],
            *[out_m[k] for k in ORDER], *[out_v[k] for k in ORDER])


def kernel(x, p, ffn1_norm, ffn1_w1, ffn1_w3, ffn1_w2, mix_norm, ffn2_norm, ffn2_w1, ffn2_w3, ffn2_w2, ple_norm, ple_proj, ple_gate, s5_w_in, s5_a_re, s5_a_im, s5_log_dt, s5_b_re, s5_b_im, s5_c_re, s5_c_im, s5_d, s5_w_glu, sb_w_qkv, sb_w_o, final_norm, loss_target, m_ffn1_norm, m_ffn1_w1, m_ffn1_w3, m_ffn1_w2, m_mix_norm, m_ffn2_norm, m_ffn2_w1, m_ffn2_w3, m_ffn2_w2, m_ple_norm, m_ple_proj, m_ple_gate, m_s5_w_in, m_s5_a_re, m_s5_a_im, m_s5_log_dt, m_s5_b_re, m_s5_b_im, m_s5_c_re, m_s5_c_im, m_s5_d, m_s5_w_glu, m_sb_w_qkv, m_sb_w_o, m_final_norm, v_ffn1_norm, v_ffn1_w1, v_ffn1_w3, v_ffn1_w2, v_mix_norm, v_ffn2_norm, v_ffn2_w1, v_ffn2_w3, v_ffn2_w2, v_ple_norm, v_ple_proj, v_ple_gate, v_s5_w_in, v_s5_a_re, v_s5_a_im, v_s5_log_dt, v_s5_b_re, v_s5_b_im, v_s5_c_re, v_s5_c_im, v_s5_d, v_s5_w_glu, v_sb_w_qkv, v_sb_w_o, v_final_norm):
    args = dict(locals())
    w = {k: args[k] for k in ORDER}
    m = {k: args["m_" + k] for k in ORDER}
    v = {k: args["v_" + k] for k in ORDER}
    return _step(x, p, loss_target, w, m, v)
```

```python
import functools
import math

import jax
import jax.numpy as jnp
from jax import lax
from jax.experimental import pallas as pl
from jax.experimental.pallas import tpu as pltpu

F32 = jnp.float32
BF16 = jnp.bfloat16
MESH = pl.DeviceIdType.MESH

N_CHIPS = 4
N_DEV = 8
RMS_EPS = 1e-6
S5_GROUP = 16
S5_STATE = 64
S5_CHUNK = 16
SB_HEAD_DIM = 64
SB_BLOCK = 128
SB_CUT = -104.0
SB_UNROLL = 3
ADAM_LR, ADAM_B1, ADAM_B2, ADAM_EPS, ADAM_WD, ADAM_STEP = 0.001, 0.9, 0.999, 1e-08, 0.01, 10
VMEM_LIMIT = 56 * 1024 * 1024

NN = (((1,), (0,)), ((), ()))
NT = (((1,), (1,)), ((), ()))
TN = (((0,), (0,)), ((), ()))

ANY = pl.BlockSpec(memory_space=pl.ANY)


def _tile(n, target):
    if n <= target:
        return n
    for t in range(target - target % 8, 7, -8):
        if n % t == 0:
            return t
    raise ValueError(f"no row tile for {n}")


def _params(*semantics):
    return pltpu.CompilerParams(dimension_semantics=semantics, vmem_limit_bytes=VMEM_LIMIT)


def _sigmoid(v):
    return 1.0 / (1.0 + jnp.exp(-v))


def _gemm(name, grid, operands, in_specs, groups, acc_shapes, out_shapes, out_specs, epilogue, reduce_axis=None, aliases=None):
    n_in, n_out = len(operands), len(out_shapes)
    n_red = None if reduce_axis is None else grid[reduce_axis]

    def body(*refs):
        ins, outs, accs = refs[:n_in], refs[n_in:n_in + n_out], refs[n_in + n_out:]

        def products():
            res = []
            for terms in groups:
                tot = None
                for ia, ib, dims in terms:
                    d = lax.dot_general(ins[ia][...], ins[ib][...], dims, preferred_element_type=F32)
                    tot = d if tot is None else tot + d
                res.append(tot)
            return res

        def finish(vals):
            for o, v in zip(outs, epilogue(vals, ins)):
                o[...] = v.astype(o.dtype)

        if reduce_axis is None:
            finish(products())
        else:
            k = pl.program_id(reduce_axis)

            @pl.when(k == 0)
            def _():
                for a in accs:
                    a[...] = jnp.zeros_like(a)

            for a, d in zip(accs, products()):
                a[...] += d

            @pl.when(k == n_red - 1)
            def _():
                finish([a[...] for a in accs])

    scratch = [] if reduce_axis is None else [pltpu.VMEM(s, F32) for s in acc_shapes]
    sem = tuple("arbitrary" if i == reduce_axis else "parallel" for i in range(len(grid)))
    return pl.pallas_call(
        body, name=name, grid=grid, in_specs=in_specs, out_specs=out_specs, out_shape=out_shapes,
        scratch_shapes=scratch, input_output_aliases=aliases or {}, compiler_params=_params(*sem))(*operands)


def _ident(vals, ins):
    return vals


def _act_spec(layout, tm, cs, pos):
    if layout == "sm":
        return pl.BlockSpec((None, tm, cs), lambda *g: (pos(*g)[1], pos(*g)[0], 0))
    return pl.BlockSpec((tm, cs), lambda *g: pos(*g))


def _act_shape(layout, t, cs, dtype):
    return jax.ShapeDtypeStruct((N_CHIPS, t, cs) if layout == "sm" else (t, N_CHIPS * cs), dtype)


def _w_spec(w, layer, pos_k):
    _, _, r, c = w.shape
    return pl.BlockSpec((None, None, r, c), lambda *g: (pos_k(*g), layer, 0, 0))


def _mm_cs(name, x, w, layer, out_layout, out_dtype, tm=1024):
    t, kd = x.shape
    cs = w.shape[3]
    tm = _tile(t, tm)
    return _gemm(
        name, (N_CHIPS, t // tm), [x, w],
        [pl.BlockSpec((tm, kd), lambda k, i: (i, 0)), _w_spec(w, layer, lambda k, i: k)],
        [[(0, 1, NN)]], None, [_act_shape(out_layout, t, cs, out_dtype)],
        [_act_spec(out_layout, tm, cs, lambda k, i: (i, k))], _ident)[0]


def _mm_rs(name, xs, layout, w, layer, res=None, alpha=1.0, out_dtype=F32, tm=1024, gated=None):
    ks, n = w.shape[2], w.shape[3]
    t = xs.shape[1] if layout == "sm" else xs.shape[0]
    tm = _tile(t, tm)
    row = pl.BlockSpec((tm, n), lambda i: (i, 0))
    extras = [a for a in (res, gated) if a is not None]
    shards = range(N_CHIPS)
    operands = [xs] * N_CHIPS + [w] * N_CHIPS + extras
    specs = ([_act_spec(layout, tm, ks, lambda i, k=k: (i, k)) for k in shards]
             + [_w_spec(w, layer, lambda i, k=k: k) for k in shards] + [row] * len(extras))
    base = 2 * N_CHIPS

    def epilogue(vals, ins):
        y = alpha * vals[0]
        if gated is not None:
            return [y, ins[base][...] + ins[base + 1][...] * _sigmoid(y)]
        return [y if res is None else ins[base][...] + y]

    outs = _gemm(
        name, (t // tm,), operands, specs, [[(k, N_CHIPS + k, NN) for k in shards]], None,
        [jax.ShapeDtypeStruct((t, n), out_dtype)] * (1 if gated is None else 2), [row] * (1 if gated is None else 2),
        epilogue)
    return outs[0] if gated is None else outs


def _mm_cs_dx(name, pairs, layout, layer, tm=1024, transposed=False, norm=None):
    w0 = pairs[0][1]
    kd, cs = (w0.shape[3], w0.shape[2]) if transposed else (w0.shape[2], w0.shape[3])
    dy0 = pairs[0][0]
    t = dy0.shape[1] if layout == "sm" else dy0.shape[0]
    tm = _tile(t, tm)
    operands, specs, terms = [], [], []
    for dy, w in pairs:
        terms.append((len(operands), len(operands) + 1, NN if transposed else NT))
        operands += [dy, w]
        specs += [_act_spec(layout, tm, cs, lambda i, k: (i, k)), _w_spec(w, layer, lambda i, k: k)]
    row = pl.BlockSpec((tm, kd), lambda i, k: (i, 0))
    if norm is None:
        return _gemm(name, (t // tm, N_CHIPS), operands, specs, [terms], [(tm, kd)],
                     [jax.ShapeDtypeStruct((t, kd), F32)], [row], _ident, reduce_axis=1)[0]
    base = len(operands)
    operands += list(norm)
    specs += [row, row, pl.BlockSpec(norm[2].shape, lambda i, k: (0, 0))]

    def epilogue(vals, ins):
        dx, dg = _rms_bwd_math(vals[0], ins[base + 1][...], ins[base + 2][...])
        dh = ins[base][...] + dx
        return [dh, dh, dg]

    dh, dhb, dg = _gemm(
        name, (t // tm, N_CHIPS), operands, specs, [terms], [(tm, kd)],
        [jax.ShapeDtypeStruct((t, kd), F32), jax.ShapeDtypeStruct((t, kd), BF16), jax.ShapeDtypeStruct((t // tm, 1, kd), F32)],
        [row, row, pl.BlockSpec((None, 1, kd), lambda i, k: (i, 0, 0))], epilogue, reduce_axis=1)
    return dh, dhb, dg.sum(axis=0)


def _mm_rs_dx(name, dy, w, layer, out_layout, out_dtype, tm=1024):
    t, n = dy.shape
    ks = w.shape[2]
    tm = _tile(t, tm)
    return _gemm(
        name, (N_CHIPS, t // tm), [dy, w],
        [pl.BlockSpec((tm, n), lambda k, i: (i, 0)), _w_spec(w, layer, lambda k, i: k)],
        [[(0, 1, NT)]], None, [_act_shape(out_layout, t, ks, out_dtype)],
        [_act_spec(out_layout, tm, ks, lambda k, i: (i, k))], _ident)[0]


def _mm_dw(name, x, x_layout, dy, dy_layout, slot, alpha=1.0, tk=4096):
    stack, layer, layers = slot
    if x_layout is None:
        t, rows = x.shape
        cols = dy.shape[2] if dy_layout == "sm" else dy.shape[1] // N_CHIPS
        tk = _tile(t, tk)
        xspec = pl.BlockSpec((tk, rows), lambda k, j: (j, 0))
        yspec = _act_spec(dy_layout, tk, cols, lambda k, j: (j, k))
    else:
        t, cols = dy.shape
        rows = x.shape[2] if x_layout == "sm" else x.shape[1] // N_CHIPS
        tk = _tile(t, tk)
        xspec = _act_spec(x_layout, tk, rows, lambda k, j: (j, k))
        yspec = pl.BlockSpec((tk, cols), lambda k, j: (j, 0))
    operands, specs = [x, dy], [xspec, yspec]
    if stack is not None:
        operands.append(stack)
        specs.append(ANY)
    return _gemm(
        name, (N_CHIPS, t // tk), operands, specs, [[(0, 1, TN)]], [(rows, cols)],
        [jax.ShapeDtypeStruct((N_CHIPS, layers, rows, cols), F32)],
        [pl.BlockSpec((None, None, rows, cols), lambda k, j: (k, layer, 0, 0))],
        lambda vals, ins: [alpha * vals[0]], reduce_axis=1 if t // tk > 1 else None,
        aliases=None if stack is None else {2: 0})[0]


def _rows(name, fn, ins, outs, accs=(), tm=512):
    t = ins[0].shape[0]
    tm = _tile(t, tm)
    n_in, n_out, n_acc = len(ins), len(outs), len(accs)
    in_specs = []
    for a in ins:
        if a.shape[0] == t:
            in_specs.append(pl.BlockSpec((tm, a.shape[1]), lambda i: (i, 0)))
        else:
            in_specs.append(pl.BlockSpec(a.shape, lambda i: (0, 0)))
    out_shape = [jax.ShapeDtypeStruct((t, c), d) for c, d in outs] + [jax.ShapeDtypeStruct(s, F32) for s in accs]
    out_specs = [pl.BlockSpec((tm, c), lambda i: (i, 0)) for c, _ in outs] + [pl.BlockSpec(s, lambda i: (0, 0)) for s in accs]

    def body(*refs):
        i = pl.program_id(0)
        row_vals, acc_vals = fn(*[r[...] for r in refs[:n_in]])
        for o, v in zip(refs[n_in:n_in + n_out], row_vals):
            o[...] = v.astype(o.dtype)
        acc_refs = refs[n_in + n_out:]
        if n_acc:
            @pl.when(i == 0)
            def _():
                for a in acc_refs:
                    a[...] = jnp.zeros_like(a)

            for a, v in zip(acc_refs, acc_vals):
                a[...] += v

    res = pl.pallas_call(
        body, name=name, grid=(t // tm,), in_specs=in_specs, out_specs=out_specs, out_shape=out_shape,
        compiler_params=_params("arbitrary" if n_acc else "parallel"))(*ins)
    return res[:n_out], res[n_out:]


def _to_bf16(a):
    def fn(x):
        return [x], []
    return _rows("weights_bf16", fn, [a.reshape(-1, a.shape[-1])], [(a.shape[-1], BF16)], tm=512)[0][0].reshape(a.shape)


def _rms_stats(x):
    return lax.rsqrt(jnp.mean(x * x, axis=-1, keepdims=True) + RMS_EPS)


def _rmsnorm(name, h, g):
    def fn(x, gg):
        return [x * _rms_stats(x) * gg], []
    return _rows(name, fn, [h, g], [(h.shape[1], BF16)])[0][0]


def _rms_bwd_math(dn, x, g):
    r = _rms_stats(x)
    xhat = x * r
    dxh = dn * g
    dx = r * (dxh - xhat * jnp.mean(dxh * xhat, axis=-1, keepdims=True))
    return dx, jnp.sum(dn * xhat, axis=0, keepdims=True)


def _rmsnorm_bwd(name, dres, dn, h, g):
    def fn(dr, d, x, gg):
        dx, dg = _rms_bwd_math(d, x, gg)
        return [dr + dx, dr + dx], [dg]
    (dh, dhb), (dg,) = _rows(name, fn, [dres, dn, h, g], [(h.shape[1], F32), (h.shape[1], BF16)], [(1, h.shape[1])])
    return dh, dhb, dg


def _ffn_fwd(h, g, w1, w3, w2, layer, tm=1024):
    t, d = h.shape
    fs = w1.shape[2]
    n = _rmsnorm("ffn_norm", h, g)
    tm = _tile(t, tm)

    def up(vals, ins):
        a, b = vals
        sg = _sigmoid(a)
        silu = a * sg
        return [b * sg * (1.0 + a * (1.0 - sg)), silu, silu * b]

    sm = _act_shape("sm", t, fs, BF16)
    osp = _act_spec("sm", tm, fs, lambda k, i: (i, k))
    ga, gb, s = _gemm(
        "ffn_up", (N_CHIPS, t // tm), [n, w1, w3],
        [pl.BlockSpec((tm, d), lambda k, i: (i, 0)), _w_spec(w1, layer, lambda k, i: k), _w_spec(w3, layer, lambda k, i: k)],
        [[(0, 1, NT)], [(0, 2, NT)]], None, [sm, sm, sm], [osp, osp, osp], up)
    out = _mm_rs("ffn_down", s, "sm", w2, layer, res=h, alpha=0.5)
    return out, (h, n, ga, gb, s)


def _ffn_bwd(dout, dob, saved, g, w1, w3, w2, layer, slots, tm=1024):
    h, n, ga, gb, s = saved
    t, d = h.shape
    fs = w1.shape[2]
    tm = _tile(t, tm)

    def down(vals, ins):
        ds = 0.5 * vals[0]
        return [ds * ins[2][...].astype(F32), ds * ins[3][...].astype(F32)]

    sm = _act_shape("sm", t, fs, BF16)
    asp = _act_spec("sm", tm, fs, lambda k, i: (i, k))
    da, db = _gemm(
        "ffn_down_dx", (N_CHIPS, t // tm), [dob, w2, ga, gb],
        [pl.BlockSpec((tm, d), lambda k, i: (i, 0)), _w_spec(w2, layer, lambda k, i: k), asp, asp],
        [[(0, 1, NT)]], None, [sm, sm], [asp, asp], down)
    dw2 = _mm_dw("ffn_dw2", s, "sm", dob, None, slots[2], alpha=0.5)
    dw1 = _mm_dw("ffn_dw1", da, "sm", n, None, slots[0])
    dw3 = _mm_dw("ffn_dw3", db, "sm", n, None, slots[1])
    dh, dhb, dg = _mm_cs_dx("ffn_up_dx", [(da, w1), (db, w3)], "sm", layer, transposed=True, norm=(dout, h, g))
    return dh, dhb, dg, dw1, dw3, dw2


def _ple_fwd(h, g, p2, wproj, wgate, layer):
    n = _rmsnorm("ple_norm", h, g)
    pp = _mm_cs("ple_proj", p2, wproj, layer, "flat", F32)
    gl, out = _mm_rs("ple_gate", n, "flat", wgate, layer, res=h, gated=pp)
    return out, (h, n, gl, pp)


def _ple_bwd(dout, saved, g, p2, wproj, wgate, layer, slots):
    h, n, gl, pp = saved
    d = h.shape[1]

    def fn(do, gg, q):
        sg = _sigmoid(gg)
        return [do * sg, do * q * sg * (1.0 - sg)], []
    (dpp, dgl), _ = _rows("ple_mix_bwd", fn, [dout, gl, pp], [(d, BF16), (d, BF16)])
    dwproj = _mm_dw("ple_dwproj", p2, None, dpp, "flat", slots[0])
    dwgate = _mm_dw("ple_dwgate", n, "flat", dgl, None, slots[1])
    dn = _mm_rs_dx("ple_gate_dx", dgl, wgate, layer, "flat", F32)
    dh, dhb, dg = _rmsnorm_bwd("ple_norm_bwd", dout, dn, h, g)
    return dh, dhb, dg, dwproj, dwgate


def _head(h, g, target):
    d = h.shape[1]

    def fn(x, gg, tg):
        y = x * _rms_stats(x) * gg
        err = y - tg
        dy = err * (1.0 / d)
        dx, dg = _rms_bwd_math(dy, x, gg)
        loss = 0.5 * jnp.sum(jnp.sum(err * err, axis=-1, keepdims=True) * (1.0 / d), axis=0, keepdims=True)
        return [dx], [dg, jnp.broadcast_to(loss, (1, 128))]
    (dh,), (dg, loss) = _rows("loss_head", fn, [h, g, target], [(d, F32)], [(1, d), (1, 128)])
    return loss[0, 0], dh, dg


S5_LANES = 2 * S5_STATE
S5_GB = 128 // S5_GROUP


def _s5_prep(a_re, a_im, log_dt, b_re, b_im, c_re, c_im):
    c, gb = S5_CHUNK, S5_GB
    g = a_re.shape[0]
    nb = g // gb
    lam_re = jnp.minimum(a_re, -1e-4)
    lam_im = a_im
    dt = jnp.exp(log_dt)[:, None, None]
    ks = jnp.arange(c + 1, dtype=F32)
    mag = jnp.exp(lam_re[..., None] * dt * ks)
    ph = lam_im[..., None] * dt * ks
    pw_re, pw_im = mag * jnp.cos(ph), mag * jnp.sin(ph)
    den = lam_re * lam_re + lam_im * lam_im
    nr, ni = pw_re[..., 1] - 1.0, pw_im[..., 1]
    fr = (nr * lam_re + ni * lam_im) / den
    fi = (ni * lam_re - nr * lam_im) / den
    bb_re = fr[..., None] * b_re - fi[..., None] * b_im
    bb_im = fr[..., None] * b_im + fi[..., None] * b_re
    ct_re, ct_im = c_re.transpose(0, 2, 1), c_im.transpose(0, 2, 1)
    ca_re = ct_re[:, :, None, :] * pw_re[..., None] - ct_im[:, :, None, :] * pw_im[..., None]
    ca_im = ct_re[:, :, None, :] * pw_im[..., None] + ct_im[:, :, None, :] * pw_re[..., None]
    hp = lax.Precision.HIGHEST
    kern = (jnp.einsum("gpj,gpkh->gkjh", bb_re, ca_re[:, :, :c], precision=hp)
            - jnp.einsum("gpj,gpkh->gkjh", bb_im, ca_im[:, :, :c], precision=hp))
    rev_re = pw_re[:, :, :c][:, :, ::-1].transpose(0, 2, 1)
    rev_im = pw_im[:, :, :c][:, :, ::-1].transpose(0, 2, 1)
    bt_re, bt_im = bb_re.transpose(0, 2, 1), bb_im.transpose(0, 2, 1)
    wn_re = rev_re[:, :, None, :] * bt_re[:, None] - rev_im[:, :, None, :] * bt_im[:, None]
    wn_im = rev_re[:, :, None, :] * bt_im[:, None] + rev_im[:, :, None, :] * bt_re[:, None]
    wn = jnp.concatenate([wn_re, wn_im], axis=-1)
    wo = jnp.concatenate([ca_re[:, :, 1:].transpose(0, 2, 3, 1), -ca_im[:, :, 1:].transpose(0, 2, 3, 1)], axis=-1)

    def blocks(x):
        return x.reshape(nb, gb, c, S5_GROUP, x.shape[3]).transpose(0, 2, 1, 3, 4).reshape(nb, c, gb * S5_GROUP, x.shape[3])

    ar, ai = pw_re[..., c], pw_im[..., c]
    return (jnp.tile(blocks(kern), (1, 1, 1, gb)), blocks(wn), blocks(wo),
            jnp.concatenate([ar, ar], axis=1), jnp.concatenate([-ai, ai], axis=1))


def _step_rows(ref, tau, n):
    return ref[pl.ds(tau, n, stride=S5_CHUNK), :].astype(BF16)


def _cat_groups(ref, dtype):
    return jnp.concatenate([ref[:, j, :] for j in range(S5_GB)], axis=1).astype(dtype)


def _cat_steps(ref, n):
    return jnp.concatenate([_step_rows(ref, tau, n) for tau in range(S5_CHUNK)], axis=1)


def _stack_steps(ref, n):
    return jnp.concatenate([_step_rows(ref, tau, n) for tau in range(S5_CHUNK)], axis=0)


def _cat_ops(ref, axis, reverse=False):
    order = range(S5_CHUNK - 1, -1, -1) if reverse else range(S5_CHUNK)
    return jnp.concatenate([ref[k] for k in order], axis=axis)


def _row_group(rows, lanes):
    row = (lax.broadcasted_iota(jnp.int32, (rows, lanes), 0) // S5_GROUP) % S5_GB
    lane = (lax.broadcasted_iota(jnp.int32, (rows, lanes), 1) // S5_GROUP) % S5_GB
    return row, lane


def _own_group(x):
    row, lane = _row_group(*x.shape)
    return jnp.where(row == lane, x, jnp.zeros_like(x))


def _spread(x):
    row, _ = _row_group(*x.shape)
    return jnp.concatenate([jnp.where(row == j, x, jnp.zeros_like(x)) for j in range(S5_GB)], axis=1)


def _gather_own(x):
    row, _ = _row_group(x.shape[0], S5_LANES)
    out = jnp.zeros((x.shape[0], S5_LANES), x.dtype)
    for j in range(S5_GB):
        out = out + jnp.where(row == j, x[:, j * S5_LANES:(j + 1) * S5_LANES], 0.0)
    return out


def _s5_specs(t, d):
    nct, g = t // S5_CHUNK, d // S5_GROUP
    tok = pl.BlockSpec((t, 128), lambda i: (0, i))
    st = pl.BlockSpec((nct, S5_GB, S5_LANES), lambda i: (0, i, 0))
    op = lambda w: pl.BlockSpec((None,) + w.shape[1:], lambda i: (i, 0, 0, 0))
    return nct, g, tok, st, op


def _s5_chunk_fwd(u, bd, bn):
    t, d = u.shape
    nct, g, tok, st, op = _s5_specs(t, d)
    c = S5_CHUNK

    def body(u_ref, bd_ref, bn_ref, y_ref, s_ref):
        ucat = _cat_steps(u_ref, nct)
        sloc = jnp.dot(ucat, _spread(_cat_ops(bn_ref, 0)), preferred_element_type=F32)
        for j in range(S5_GB):
            s_ref[:, j, :] = sloc[:, j * S5_LANES:(j + 1) * S5_LANES]
        lags = _own_group(_cat_ops(bd_ref, 0, reverse=True))
        for tt in range(c):
            y_ref[pl.ds(tt, nct, stride=c), :] = jnp.dot(ucat[:, :(tt + 1) * 128], lags[(c - 1 - tt) * 128:, :],
                                                         preferred_element_type=F32)

    return pl.pallas_call(
        body, name="s5_chunk", grid=(d // 128,), in_specs=[tok, op(bd), op(bn)], out_specs=[tok, st],
        out_shape=[jax.ShapeDtypeStruct((t, d), F32), jax.ShapeDtypeStruct((nct, g, S5_LANES), F32)],
        compiler_params=_params("parallel"))(u, bd, bn)


def _s5_state_out(sprev, co, yin):
    t, d = yin.shape
    nct, g, tok, st, op = _s5_specs(t, d)
    c = S5_CHUNK

    def body(s_ref, co_ref, yi_ref, y_ref):
        ys = lax.dot_general(_cat_groups(s_ref, BF16), _spread(_cat_ops(co_ref, 0)), NT,
                             preferred_element_type=F32)
        for tt in range(c):
            rows = pl.ds(tt, nct, stride=c)
            y_ref[rows, :] = yi_ref[rows, :] + ys[:, tt * 128:(tt + 1) * 128]

    return pl.pallas_call(
        body, name="s5_state_out", grid=(d // 128,), in_specs=[st, op(co), tok], out_specs=tok,
        out_shape=jax.ShapeDtypeStruct((t, d), F32), compiler_params=_params("parallel"))(sprev, co, yin)


def _s5_state_out_dx(dyb, co):
    t, d = dyb.shape
    nct, g, tok, st, op = _s5_specs(t, d)
    c = S5_CHUNK

    def body(dy_ref, co_ref, ds_ref):
        acc = jnp.dot(_cat_steps(dy_ref, nct), _spread(_cat_ops(co_ref, 0)), preferred_element_type=F32)
        for j in range(S5_GB):
            ds_ref[:, j, :] = acc[:, j * S5_LANES:(j + 1) * S5_LANES]

    return pl.pallas_call(
        body, name="s5_state_out_dx", grid=(d // 128,), in_specs=[tok, op(co)], out_specs=st,
        out_shape=jax.ShapeDtypeStruct((nct, g, S5_LANES), F32), compiler_params=_params("parallel"))(dyb, co)


def _s5_chunk_dx(dyb, dsloc, bd, bn, skip):
    t, d = dyb.shape
    nct, g, tok, st, op = _s5_specs(t, d)
    c = S5_CHUNK

    def body(dy_ref, ds_ref, bd_ref, bn_ref, sk_ref, du_ref):
        dus = lax.dot_general(_cat_groups(ds_ref, BF16), _spread(_cat_ops(bn_ref, 0)), NT, preferred_element_type=F32)
        dycat = _cat_steps(dy_ref, nct)
        lags = _own_group(_cat_ops(bd_ref, 1))
        for tau in range(c):
            rows = pl.ds(tau, nct, stride=c)
            du_ref[rows, :] = (sk_ref[rows, :] + dus[:, tau * 128:(tau + 1) * 128]
                               + lax.dot_general(dycat[:, tau * 128:], lags[:, :(c - tau) * 128], NT,
                                                 preferred_element_type=F32))

    return pl.pallas_call(
        body, name="s5_chunk_dx", grid=(d // 128,), in_specs=[tok, st, op(bd), op(bn), tok], out_specs=tok,
        out_shape=jax.ShapeDtypeStruct((t, d), F32), compiler_params=_params("parallel"))(dyb, dsloc, bd, bn, skip)


def _s5_chunk_dw(u, dyb, dsloc, bd, bn):
    t, d = u.shape
    nct, g, tok, st, op = _s5_specs(t, d)
    c = S5_CHUNK

    def body(u_ref, dy_ref, ds_ref, dbd_ref, dbn_ref):
        dbn = _gather_own(lax.dot_general(_cat_steps(u_ref, nct), _cat_groups(ds_ref, BF16), TN,
                                          preferred_element_type=F32))
        for tau in range(c):
            dbn_ref[tau] = dbn[tau * 128:(tau + 1) * 128, :]
        ustk, dystk = _stack_steps(u_ref, nct), _stack_steps(dy_ref, nct)
        for k in range(c):
            dbd_ref[k] = _own_group(lax.dot_general(ustk[:(c - k) * nct], dystk[k * nct:], TN,
                                                    preferred_element_type=F32))

    return pl.pallas_call(
        body, name="s5_chunk_dw", grid=(d // 128,), in_specs=[tok, tok, st], out_specs=[op(bd), op(bn)],
        out_shape=[jax.ShapeDtypeStruct(bd.shape, F32), jax.ShapeDtypeStruct(bn.shape, F32)],
        compiler_params=_params("parallel"))(u, dyb, dsloc)


def _s5_state_out_dw(sprev, dyb, co):
    t, d = dyb.shape
    nct, g, tok, st, op = _s5_specs(t, d)
    c = S5_CHUNK

    def body(s_ref, dy_ref, dco_ref):
        dco = _gather_own(lax.dot_general(_cat_steps(dy_ref, nct), _cat_groups(s_ref, BF16), TN,
                                          preferred_element_type=F32))
        for tt in range(c):
            dco_ref[tt] = dco[tt * 128:(tt + 1) * 128, :]

    return pl.pallas_call(
        body, name="s5_state_out_dw", grid=(d // 128,), in_specs=[st, tok], out_specs=op(co),
        out_shape=jax.ShapeDtypeStruct(co.shape, F32), compiler_params=_params("parallel"))(sprev, dyb)


def _s5_scan_fwd(sloc, m1, m2):
    bl, nc, g, w = sloc.shape

    def body(s_ref, m1_ref, m2_ref, o_ref):
        a1, a2 = m1_ref[...], m2_ref[...]

        def step(c, states):
            new = []
            for b, s in enumerate(states):
                o_ref[b, c] = s
                new.append(a1 * s + a2 * pltpu.roll(s, S5_STATE, 1) + s_ref[b, c])
            return tuple(new)
        lax.fori_loop(0, nc, step, tuple(jnp.zeros((g, w), F32) for _ in range(bl)))

    vm = pl.BlockSpec(memory_space=pltpu.VMEM)
    return pl.pallas_call(
        body, name="s5_scan", in_specs=[vm, vm, vm], out_specs=vm,
        out_shape=jax.ShapeDtypeStruct(sloc.shape, F32),
        compiler_params=pltpu.CompilerParams(vmem_limit_bytes=VMEM_LIMIT))(sloc, m1, m2)


def _s5_scan_bwd(dsprev, sprev, m1, m2):
    bl, nc, g, w = dsprev.shape

    def body(d_ref, s_ref, m1_ref, m2_ref, g_ref, p1_ref, p2_ref):
        a1, a2 = m1_ref[...], m2_ref[...]
        zero = jnp.zeros((g, w), F32)

        def step(i, carry):
            gps, p1, p2 = carry
            c = nc - 2 - i
            new = []
            for b, gp in enumerate(gps):
                g_ref[b, c] = gp
                sp = s_ref[b, c]
                p1 = p1 + gp * sp
                p2 = p2 + gp * pltpu.roll(sp, S5_STATE, 1)
                new.append(d_ref[b, c] + a1 * gp - a2 * pltpu.roll(gp, S5_STATE, 1))
            return tuple(new), p1, p2

        for b in range(bl):
            g_ref[b, nc - 1] = zero
        _, p1, p2 = lax.fori_loop(0, nc - 1, step, (tuple(d_ref[b, nc - 1] for b in range(bl)), zero, zero))
        p1_ref[...] = p1
        p2_ref[...] = p2

    vm = pl.BlockSpec(memory_space=pltpu.VMEM)
    sd = jax.ShapeDtypeStruct
    return pl.pallas_call(
        body, name="s5_scan_bwd", in_specs=[vm, vm, vm, vm], out_specs=[vm, vm, vm],
        out_shape=[sd(dsprev.shape, F32), sd((g, w), F32), sd((g, w), F32)],
        compiler_params=pltpu.CompilerParams(vmem_limit_bytes=VMEM_LIMIT))(dsprev, sprev, m1, m2)


def _gelu_tanh_parts(y):
    c0 = math.sqrt(2.0 / math.pi)
    inner = c0 * (y + 0.044715 * y * y * y)
    th = jnp.tanh(inner)
    return th, c0 * (1.0 + 3 * 0.044715 * y * y)


def _s5_fwd(h, g, ops, d_skip, w_in, w_glu, bl):
    bd, bn, co, m1, m2 = ops
    t, d = h.shape
    nct, groups = t // S5_CHUNK, d // S5_GROUP
    hn = _rmsnorm("mix_norm", h, g)
    u = _mm_rs("s5_in", hn, "flat", w_in, 0)
    yin, sloc = _s5_chunk_fwd(u, bd.astype(BF16), bn.astype(BF16))
    sprev = _s5_scan_fwd(sloc.reshape(bl, nct // bl, groups, S5_LANES), m1, m2).reshape(nct, groups, S5_LANES)
    y = _s5_state_out(sprev, co.astype(BF16), yin)

    def fn(yy, uu, dd):
        y2 = yy + dd * uu
        th, _ = _gelu_tanh_parts(y2)
        return [0.5 * y2 * (1.0 + th)], []
    z = _rows("s5_gelu", fn, [y, u, d_skip], [(d, BF16)])[0][0]
    zz = _mm_cs("s5_glu", z, w_glu, 0, "flat", F32)

    def glu(hh, zv):
        return [hh + zv[:, :d] * _sigmoid(zv[:, d:])], []
    out = _rows("s5_glu_mix", glu, [h, zz], [(d, F32)])[0][0]
    return out, (h, hn, u, sprev, y, z, zz)


def _s5_bwd(dout, saved, g, ops, d_skip, w_in, w_glu, bl):
    h, hn, u, sprev, y, z, zz = saved
    bd, bn, co, m1, m2 = ops
    t, d = h.shape
    nct, groups = t // S5_CHUNK, d // S5_GROUP

    def glu_bwd(do, zv):
        sg = _sigmoid(zv[:, d:])
        return [jnp.concatenate([do * sg, do * zv[:, :d] * sg * (1.0 - sg)], axis=1)], []
    dzz = _rows("s5_glu_bwd", glu_bwd, [dout, zz], [(2 * d, BF16)])[0][0]
    dwglu = _mm_dw("s5_dwglu", z, None, dzz, "flat", (None, 0, 1))
    dz = _mm_cs_dx("s5_glu_dx", [(dzz, w_glu)], "flat", 0)

    def gelu_bwd(dzv, yy, uu, dd):
        y2 = yy + dd * uu
        th, dinner = _gelu_tanh_parts(y2)
        dy2 = dzv * (0.5 * (1.0 + th) + 0.5 * y2 * (1.0 - th * th) * dinner)
        return [dy2, dy2 * dd], [jnp.sum(dy2 * uu, axis=0, keepdims=True)]
    (dyb, du_skip), (dd,) = _rows("s5_gelu_bwd", gelu_bwd, [dz, y, u, d_skip], [(d, F32), (d, F32)], [(1, d)])
    bd_b, bn_b, co_b = bd.astype(BF16), bn.astype(BF16), co.astype(BF16)
    dsprev = _s5_state_out_dx(dyb, co_b)
    shape4 = (bl, nct // bl, groups, S5_LANES)
    dsloc, dm1, dm2 = _s5_scan_bwd(dsprev.reshape(shape4), sprev.reshape(shape4), m1, m2)
    dsloc = dsloc.reshape(nct, groups, S5_LANES)
    du = _s5_chunk_dx(dyb, dsloc, bd_b, bn_b, du_skip).astype(BF16)
    dbd, dbn = _s5_chunk_dw(u, dyb, dsloc, bd, bn)
    dco = _s5_state_out_dw(sprev, dyb, co)
    dwin = _mm_dw("s5_dwin", hn, "flat", du, None, (None, 0, 1))
    dhn = _mm_rs_dx("s5_in_dx", du, w_in, 0, "flat", F32)
    dh, dhb, dg = _rmsnorm_bwd("mix_norm_bwd", dout, dhn, h, g)
    return dh, dhb, dg, dwin, dwglu, dd, (dbd, dbn, dco, dm1, dm2)


def _sb_block(qi, idx, tb):
    kb = qi - idx
    return pl.multiple_of(jnp.maximum(kb, 0) * tb, tb), idx == 0, kb >= 0


def _sb_scores(q, kblk, diag, exists, row, col):
    z = lax.dot_general(q, kblk, NT, preferred_element_type=F32) * (SB_HEAD_DIM ** -0.5)
    l1 = jnp.log(1.0 + jnp.exp(-jnp.abs(z)))
    ls = jnp.minimum(z, 0.0) - l1
    mask = jnp.logical_and(jnp.logical_or(col < row, jnp.logical_not(diag)), exists)
    lk = jnp.where(mask, ls - z, 0.0)
    return ls, lk, mask


def _split_dot(v, tri):
    hi = v.astype(BF16)
    lo = (v - hi.astype(F32)).astype(BF16)
    return (jnp.dot(hi, tri, preferred_element_type=F32) + jnp.dot(lo, tri, preferred_element_type=F32))


SB_PAIR =2 * SB_HEAD_DIM


def _pair_masks(tb):
    lane = lax.broadcasted_iota(jnp.int32, (1, SB_PAIR), 1)
    row = lax.broadcasted_iota(jnp.int32, (tb, tb), 0)
    col = lax.broadcasted_iota(jnp.int32, (tb, tb), 1)
    return [lane < SB_HEAD_DIM, lane >= SB_HEAD_DIM], row, col


def _pair_more(qi, carry):
    j, crs = carry[0], carry[2]
    return jnp.logical_and(j <= qi, jnp.maximum(jnp.max(crs[0]), jnp.max(crs[1])) > SB_CUT)


def _pair_specs(bl, l, d, tb):
    nq, off = l // tb, d // SB_PAIR
    qspec = pl.BlockSpec((tb, SB_PAIR), lambda b, p, i: (b * nq + i, p))
    kspec = pl.BlockSpec((l, SB_PAIR), lambda b, p, i: (b, off + p))
    vspec = pl.BlockSpec((l, SB_PAIR), lambda b, p, i: (b, 2 * off + p))
    return qspec, kspec, vspec


def _sb_attn_fwd2(qkv, bl):
    t, d3 = qkv.shape
    d, l = d3 // 3, t // bl
    tb = min(SB_BLOCK, l)
    nq = l // tb

    def body(q_ref, k_ref, v_ref, o_ref, ob_ref):
        qi = pl.program_id(2)
        heads, row, col = _pair_masks(tb)
        qv = q_ref[...]
        qh = [jnp.where(m, qv, jnp.zeros_like(qv)) for m in heads]
        tri = (row > col).astype(BF16)

        def step(carry):
            j, acc, crs = carry
            crs = list(crs)
            where = [_sb_block(qi, j + u, tb) for u in range(SB_UNROLL)]
            kblks = [k_ref[pl.ds(ks, tb), :] for ks, _, _ in where]
            scores = [[_sb_scores(qh[hd], kblks[u], where[u][1], where[u][2], row, col) for hd in range(2)]
                      for u in range(SB_UNROLL)]
            laters = [[_split_dot(sc[1], tri) for sc in su] for su in scores]
            for u in range(SB_UNROLL):
                vblk = v_ref[pl.ds(where[u][0], tb), :]
                outs = []
                for hd in range(2):
                    ls, lk, mask = scores[u][hd]
                    att = jnp.where(mask, jnp.exp(ls + laters[u][hd] + crs[hd]), 0.0)
                    outs.append(jnp.dot(att.astype(BF16), vblk, preferred_element_type=F32))
                    crs[hd] = crs[hd] + jnp.sum(lk, axis=1, keepdims=True)
                acc = acc + jnp.where(heads[0], outs[0], outs[1])
            return j + SB_UNROLL, acc, tuple(crs)

        zc = jnp.zeros((tb, 1), F32)
        _, acc, _ = lax.while_loop(functools.partial(_pair_more, qi), step,
                                   (jnp.int32(0), jnp.zeros((tb, SB_PAIR), F32), (zc, zc)))
        o_ref[...] = acc
        ob_ref[...] = acc.astype(BF16)

    qspec, kspec, vspec = _pair_specs(bl, l, d, tb)
    return pl.pallas_call(
        body, name="sb_attn", grid=(bl, d // SB_PAIR, nq), in_specs=[qspec, kspec, vspec], out_specs=[qspec, qspec],
        out_shape=[jax.ShapeDtypeStruct((t, d), F32), jax.ShapeDtypeStruct((t, d), BF16)],
        compiler_params=_params("parallel", "parallel", "parallel"))(qkv, qkv, qkv)


def _sb_attn_bwd2(qkv, o, do, bl):
    t, d3 = qkv.shape
    d, l = d3 // 3, t // bl
    tb = min(SB_BLOCK, l)
    nq = l // tb
    scale = SB_HEAD_DIM ** -0.5

    def body(q_ref, k_ref, v_ref, o_ref, do_ref, dq_ref, dk_ref, dv_ref, dk_acc, dv_acc):
        qi = pl.program_id(2)

        @pl.when(qi == 0)
        def _():
            dk_acc[...] = jnp.zeros_like(dk_acc)
            dv_acc[...] = jnp.zeros_like(dv_acc)

        heads, row, col = _pair_masks(tb)
        qv = q_ref[...]
        dov = do_ref[...].astype(BF16)
        qh = [jnp.where(m, qv, jnp.zeros_like(qv)) for m in heads]
        doh = [jnp.where(m, dov, jnp.zeros_like(dov)) for m in heads]
        ov = o_ref[...]
        dsum = [jnp.sum(dh.astype(F32) * ov, axis=1, keepdims=True) for dh in doh]
        tri = (row > col).astype(BF16)
        tri_inc = (row >= col).astype(BF16)

        def step(carry):
            j, dq, crs, ces = carry
            crs, ces = list(crs), list(ces)
            n = range(SB_UNROLL)
            where = [_sb_block(qi, j + u, tb) for u in n]
            rows = [pl.ds(ks, tb) for ks, _, _ in where]
            kblks = [k_ref[rows[u], :] for u in n]
            vblks = [v_ref[rows[u], :] for u in n]
            scores = [[_sb_scores(qh[hd], kblks[u], where[u][1], where[u][2], row, col) for hd in range(2)] for u in n]
            laters = [[_split_dot(sc[1], tri) for sc in su] for su in scores]
            datts = [[lax.dot_general(doh[hd], vblks[u], NT, preferred_element_type=F32) for hd in range(2)] for u in n]
            atts = [[None, None] for _ in n]
            for u in n:
                for hd in range(2):
                    ls, lk, mask = scores[u][hd]
                    atts[u][hd] = jnp.where(mask, jnp.exp(ls + laters[u][hd] + crs[hd]), 0.0).astype(BF16)
                    crs[hd] = crs[hd] + jnp.sum(lk, axis=1, keepdims=True)
            es = [[atts[u][hd].astype(F32) * datts[u][hd] for hd in range(2)] for u in n]
            sufs = [[_split_dot(e, tri_inc) for e in eu] for eu in es]
            dzs = [[None, None] for _ in n]
            for u in n:
                for hd in range(2):
                    ls, _, mask = scores[u][hd]
                    pre = dsum[hd] - ces[hd] - sufs[u][hd]
                    sg = jnp.exp(ls)
                    dzs[u][hd] = (jnp.where(mask, es[u][hd] * (1.0 - sg) - pre * sg, 0.0) * scale).astype(BF16)
                    ces[hd] = ces[hd] + jnp.sum(es[u][hd], axis=1, keepdims=True)
            for u in n:
                dq = dq + jnp.where(heads[0], jnp.dot(dzs[u][0], kblks[u], preferred_element_type=F32),
                                    jnp.dot(dzs[u][1], kblks[u], preferred_element_type=F32))
                dk_acc[rows[u], :] += (lax.dot_general(dzs[u][0], qh[0], TN, preferred_element_type=F32)
                                       + lax.dot_general(dzs[u][1], qh[1], TN, preferred_element_type=F32))
                dv_acc[rows[u], :] += (lax.dot_general(atts[u][0], doh[0], TN, preferred_element_type=F32)
                                       + lax.dot_general(atts[u][1], doh[1], TN, preferred_element_type=F32))
            return j + SB_UNROLL, dq, tuple(crs), tuple(ces)

        zc = jnp.zeros((tb, 1), F32)
        _, dq, _, _ = lax.while_loop(functools.partial(_pair_more, qi), step,
                                     (jnp.int32(0), jnp.zeros((tb, SB_PAIR), F32), (zc, zc), (zc, zc)))
        dq_ref[...] = dq.astype(BF16)

        @pl.when(qi == nq - 1)
        def _():
            dk_ref[...] = dk_acc[...].astype(BF16)
            dv_ref[...] = dv_acc[...].astype(BF16)

    qspec, kspec, vspec = _pair_specs(bl, l, d, tb)
    blk = pl.BlockSpec((tb, SB_PAIR), lambda b, p, i: (b * nq + i, p))
    full = pl.BlockSpec((l, SB_PAIR), lambda b, p, i: (b, p))
    sd = jax.ShapeDtypeStruct((t, d), BF16)
    dq, dk, dv = pl.pallas_call(
        body, name="sb_attn_bwd", grid=(bl, d // SB_PAIR, nq), in_specs=[qspec, kspec, vspec, blk, blk],
        out_specs=[blk, full, full], out_shape=[sd, sd, sd],
        scratch_shapes=[pltpu.VMEM((l, SB_PAIR), F32), pltpu.VMEM((l, SB_PAIR), F32)],
        compiler_params=_params("parallel", "parallel", "arbitrary"))(qkv, qkv, qkv, o, do)
    return jnp.concatenate([dq, dk, dv], axis=1)


def _sb_fwd(h, g, w_qkv, w_o, bl):
    t, d = h.shape
    hn = _rmsnorm("mix_norm", h, g)
    qkv = _mm_cs("sb_qkv", hn, w_qkv, 0, "flat", BF16)
    o, ob = _sb_attn_fwd2(qkv, bl)
    out = _mm_rs("sb_out", ob, "flat", w_o, 0, res=h)
    return out, (h, hn, qkv, o, ob)


def _sb_bwd(dout, dob, saved, g, w_qkv, w_o, bl):
    h, hn, qkv, o, ob = saved
    dwo = _mm_dw("sb_dwo", ob, "flat", dob, None, (None, 0, 1))
    do = _mm_rs_dx("sb_out_dx", dob, w_o, 0, "flat", F32)
    dqkv = _sb_attn_bwd2(qkv, o, do, bl)
    dwqkv = _mm_dw("sb_dwqkv", hn, None, dqkv, "flat", (None, 0, 1))
    dh, dhb, dg = _mm_cs_dx("sb_qkv_dx", [(dqkv, w_qkv)], "flat", 0, norm=(dout, h, g))
    return dh, dhb, dg, dwqkv, dwo


def _adamw_update(wv, gr, mv, vv):
    c1 = 1.0 / (1.0 - ADAM_B1 ** ADAM_STEP)
    c2 = 1.0 / (1.0 - ADAM_B2 ** ADAM_STEP)
    mn = ADAM_B1 * mv + (1.0 - ADAM_B1) * gr
    vn = ADAM_B2 * vv + (1.0 - ADAM_B2) * gr * gr
    delta = -ADAM_LR * ((mn * c1) / (jnp.sqrt(vn * c2) + ADAM_EPS) + ADAM_WD * wv)
    return delta, mn, vn


def _adamw_small(w, gr, m, v):
    def fn(wv, gv, mv, vv):
        return list(_adamw_update(wv, gv, mv, vv)), []
    return _rows("adamw_small", fn, [w, gr, m, v], [(w.shape[1], F32)] * 3)[0]


def _place():
    x, y, c = lax.axis_index("x"), lax.axis_index("y"), lax.axis_index("c")
    chips = [(1 - x, y), (x, 1 - y), (1 - x, 1 - y)]
    return x, y, c, chips


def _remote(src, dst, send_sem, recv_sem, to):
    return pltpu.make_async_remote_copy(src_ref=src, dst_ref=dst, send_sem=send_sem, recv_sem=recv_sem,
                                        device_id=to, device_id_type=MESH)


def _half(ref, c, rh, lead):
    return ref.at[(slice(None),) * lead + (pl.ds(c * rh, rh),)]


def _allgather_weights(ws):
    n = len(ws)

    def body(*refs):
        ins, outs = refs[:n], refs[n:2 * n]
        send, recv = refs[2 * n:]
        x, y, c, _ = _place()
        chip_x, chip_y, chip_d = (1 - x, y), (x, 1 - y), (1 - x, 1 - y)
        sibling = (x, y, 1 - c)
        index = lambda chip: 2 * chip[0] + chip[1]
        sent = []

        def quarter(ref, half, q, rq):
            return ref.at[:, pl.ds((2 * half + q) * rq, rq)]

        def copy(t, kind, src, dst, to):
            return _remote(src, dst, send.at[t, kind], recv.at[t, kind], to)

        def start(cp):
            cp.start()
            sent.append(cp)

        for t in range(n):
            rq = ws[t].shape[1] // 4
            for q in range(2):
                for base, chip in ((0, chip_x), (2, chip_y)):
                    start(copy(t, base + q, quarter(ins[t], c, q, rq), quarter(outs[t].at[index((x, y))], c, q, rq), (*chip, c)))
        for t in range(n):
            rq = ws[t].shape[1] // 4
            landings = [(chip_x, 0, 0, chip_x, ((4, chip_y), (6, None))), (chip_y, 1, 3, chip_y, ((5, chip_x), (9, None))),
                        (chip_x, 1, 1, chip_x, ((7, None),)), (chip_y, 0, 2, chip_y, ((8, None),)),
                        (chip_d, 0, 4, chip_y, ((10, None),)), (chip_d, 1, 5, chip_x, ((11, None),))]
            for origin, q, kind, sender, onward in landings:
                piece = quarter(outs[t].at[index(origin)], c, q, rq)
                copy(t, kind, piece, piece, (*sender, c)).wait_recv()
                for kind2, chip in onward:
                    start(copy(t, kind2, piece, piece, sibling if chip is None else (*chip, c)))
        for t in range(n):
            rq = ws[t].shape[1] // 4
            for kind, (origin, q) in zip(range(6, 12), ((chip_x, 0), (chip_x, 1), (chip_y, 0), (chip_y, 1), (chip_d, 0), (chip_d, 1))):
                piece = quarter(outs[t].at[index(origin)], 1 - c, q, rq)
                copy(t, kind, piece, piece, sibling).wait_recv()
        for cp in sent:
            cp.wait_send()

    res = pl.pallas_call(
        body, name="allgather_weights", in_specs=[ANY] * n, out_specs=[ANY] * n,
        out_shape=[jax.ShapeDtypeStruct((N_CHIPS,) + w.shape, w.dtype) for w in ws],
        scratch_shapes=[pltpu.SemaphoreType.DMA((n, 12)), pltpu.SemaphoreType.DMA((n, 12))],
    )(*ws)
    own = 2 * lax.axis_index("x") + lax.axis_index("y")
    return [lax.dynamic_update_slice(g, w[None], (own, 0, 0, 0)) for g, w in zip(res, ws)]


def _pair_exchange(gs):
    n = len(gs)

    def body(*refs):
        ins, outs = refs[:n], refs[n:2 * n]
        send, recv = refs[2 * n:]
        x, y, c, _ = _place()
        copies = [_remote(_half(ins[t], 1 - c, gs[t].shape[2] // 2, 2), outs[t], send.at[t], recv.at[t], (x, y, 1 - c))
                  for t in range(n)]
        for cp in copies:
            cp.start()
        for cp in copies:
            cp.wait()

    return pl.pallas_call(
        body, name="grad_pair_exchange", in_specs=[ANY] * n, out_specs=[ANY] * n,
        out_shape=[jax.ShapeDtypeStruct(g.shape[:2] + (g.shape[2] // 2, g.shape[3]), F32) for g in gs],
        scratch_shapes=[pltpu.SemaphoreType.DMA((n,)), pltpu.SemaphoreType.DMA((n,))],
    )(*gs)


def _pair_sum(g, theirs, c_idx):
    n4, ly, r, cc = g.shape
    rh = r // 2
    tm = _tile(rh, 512)
    nt = rh // tm

    def body(c_ref, g_ref, t_ref, o_ref):
        o_ref[...] = (g_ref[...] + t_ref[...]).astype(o_ref.dtype)

    blk = (None, tm, cc)
    grid_spec = pltpu.PrefetchScalarGridSpec(
        num_scalar_prefetch=1, grid=(n4 * ly, nt),
        in_specs=[pl.BlockSpec(blk, lambda a, i, cr: (a, cr[0] * nt + i, 0)), pl.BlockSpec(blk, lambda a, i, cr: (a, i, 0))],
        out_specs=pl.BlockSpec(blk, lambda a, i, cr: (a, i, 0)))
    out = pl.pallas_call(
        body, name="grad_pair_sum", grid_spec=grid_spec, out_shape=jax.ShapeDtypeStruct((n4 * ly, rh, cc), BF16),
        compiler_params=_params("parallel", "parallel"))(c_idx, g.reshape(n4 * ly, r, cc), theirs.reshape(n4 * ly, rh, cc))
    return out.reshape(n4, ly, rh, cc)


def _quarter(ref, q, rq):
    return ref.at[:, pl.ds(q * rq, rq)]


def _chip_exchange_first(ps):
    n = len(ps)

    def body(*refs):
        ins, outs = refs[:n], refs[n:2 * n]
        send, recv = refs[2 * n:]
        x, y, c, _ = _place()
        index = lambda cx, cy: 2 * cx + cy
        copies = []
        for t in range(n):
            rq = ps[t].shape[2] // 2
            for base, q, chip in ((0, 0, (1 - x, y)), (2, 1, (x, 1 - y))):
                for j, slice_of in enumerate((chip, (1 - x, 1 - y))):
                    copies.append(_remote(_quarter(ins[t].at[index(*slice_of)], q, rq), outs[t].at[base + j],
                                          send.at[t, base + j], recv.at[t, base + j], (*chip, c)))
        for cp in copies:
            cp.start()
        for cp in copies:
            cp.wait()

    return pl.pallas_call(
        body, name="grad_chip_exchange", in_specs=[ANY] * n, out_specs=[ANY] * n,
        out_shape=[jax.ShapeDtypeStruct((4, p.shape[1], p.shape[2] // 2, p.shape[3]), p.dtype) for p in ps],
        scratch_shapes=[pltpu.SemaphoreType.DMA((n, 4)), pltpu.SemaphoreType.DMA((n, 4))],
    )(*ps)


def _chip_relay_sum(p, first, where):
    _, ly, rh, cc = p.shape
    rq = rh // 2
    tm = _tile(rq, 512)
    nt = rq // tm

    def body(w_ref, p_ref, f_ref, out_ref):
        out_ref[...] = (p_ref[...].astype(F32) + f_ref[...].astype(F32)).astype(out_ref.dtype)

    blk = (None, None, tm, cc)
    grid_spec = pltpu.PrefetchScalarGridSpec(
        num_scalar_prefetch=1, grid=(2, ly, nt),
        in_specs=[pl.BlockSpec(blk, lambda s, l, i, w: (w[2 - s], l, s * nt + i, 0)),
                  pl.BlockSpec(blk, lambda s, l, i, w: (1 + 2 * s, l, i, 0))],
        out_specs=pl.BlockSpec(blk, lambda s, l, i, w: (s, l, i, 0)))
    return pl.pallas_call(
        body, name="grad_relay_sum", grid_spec=grid_spec, out_shape=jax.ShapeDtypeStruct((2, ly, rq, cc), p.dtype),
        compiler_params=_params("parallel", "parallel", "parallel"))(where, p, first)


def _chip_exchange_second(ss):
    n = len(ss)

    def body(*refs):
        ins, outs = refs[:n], refs[n:2 * n]
        send, recv = refs[2 * n:]
        x, y, c, _ = _place()
        copies = []
        for t in range(n):
            for j, chip in enumerate(((x, 1 - y), (1 - x, y))):
                copies.append(_remote(ins[t].at[j], outs[t].at[j], send.at[t, j], recv.at[t, j], (*chip, c)))
        for cp in copies:
            cp.start()
        for cp in copies:
            cp.wait()

    return pl.pallas_call(
        body, name="grad_chip_exchange_2", in_specs=[ANY] * n, out_specs=[ANY] * n,
        out_shape=[jax.ShapeDtypeStruct(s.shape, s.dtype) for s in ss],
        scratch_shapes=[pltpu.SemaphoreType.DMA((n, 2)), pltpu.SemaphoreType.DMA((n, 2))],
    )(*ss)


def _chip_sum(p, first, second, where):
    _, ly, rh, cc = p.shape
    rq = rh // 2
    tm = _tile(rq, 512)
    nt = rq // tm

    def body(w_ref, p_ref, f_ref, s_ref, out_ref):
        out_ref[...] = (p_ref[...].astype(F32) + f_ref[...].astype(F32)) + s_ref[...].astype(F32)

    blk = (None, None, tm, cc)
    grid_spec = pltpu.PrefetchScalarGridSpec(
        num_scalar_prefetch=1, grid=(ly, 2, nt),
        in_specs=[pl.BlockSpec(blk, lambda l, q, i, w: (w[0], l, q * nt + i, 0)),
                  pl.BlockSpec(blk, lambda l, q, i, w: (2 * q, l, i, 0)),
                  pl.BlockSpec(blk, lambda l, q, i, w: (q, l, i, 0))],
        out_specs=pl.BlockSpec((None, tm, cc), lambda l, q, i, w: (l, q * nt + i, 0)))
    return pl.pallas_call(
        body, name="grad_chip_sum", grid_spec=grid_spec, out_shape=jax.ShapeDtypeStruct((ly, rh, cc), F32),
        compiler_params=_params("parallel", "parallel", "parallel"))(where, p, first, second)


def _pair_swap(halves):
    n = len(halves)

    def body(*refs):
        ins, outs = refs[:n], refs[n:2 * n]
        send, recv = refs[2 * n:]
        x, y, c, _ = _place()
        copies = [_remote(ins[t], outs[t], send.at[t], recv.at[t], (x, y, 1 - c)) for t in range(n)]
        for cp in copies:
            cp.start()
        for cp in copies:
            cp.wait()

    return pl.pallas_call(
        body, name="grad_pair_swap", in_specs=[ANY] * n, out_specs=[ANY] * n,
        out_shape=[jax.ShapeDtypeStruct(h.shape, F32) for h in halves],
        scratch_shapes=[pltpu.SemaphoreType.DMA((n,)), pltpu.SemaphoreType.DMA((n,))],
    )(*halves)


def _adamw_big(w, m, v, mine, theirs, c_idx):
    ly, r, cc = w.shape
    rh = r // 2
    tm = _tile(rh, 512)
    nt = rh // tm

    def body(c_ref, w_ref, m_ref, v_ref, a_ref, b_ref, g_out, d_out, m_out, v_out):
        gr = jnp.where(pl.program_id(1) == c_ref[0], a_ref[...], b_ref[...])
        delta, mn, vn = _adamw_update(w_ref[...], gr, m_ref[...], v_ref[...])
        g_out[...] = gr
        d_out[...] = delta
        m_out[...] = mn
        v_out[...] = vn

    blk = (None, tm, cc)
    full = pl.BlockSpec(blk, lambda l, hc, i, cr: (l, hc * nt + i, 0))
    half = pl.BlockSpec(blk, lambda l, hc, i, cr: (l, i, 0))
    grid_spec = pltpu.PrefetchScalarGridSpec(
        num_scalar_prefetch=1, grid=(ly, 2, nt), in_specs=[full, full, full, half, half], out_specs=[full] * 4)
    sd = jax.ShapeDtypeStruct(w.shape, F32)
    return pl.pallas_call(
        body, name="adamw", grid_spec=grid_spec, out_shape=[sd] * 4,
        compiler_params=_params("parallel", "parallel", "parallel"))(c_idx, w, m, v, mine, theirs)


def _allreduce_small(v):
    rows, w = v.shape

    def body(x_ref, sum_ref, all_ref, send, recv, local):
        x, y, c, chips = _place()
        me, sibling = (x, y, c), (x, y, 1 - c)

        def slot(px, py, pc):
            return all_ref.at[4 * px + 2 * py + pc]

        def copy(k, block, to, src=None):
            return _remote(slot(*block) if src is None else src, slot(*block), send.at[k], recv.at[k], to)

        mine = pltpu.make_async_copy(x_ref, slot(*me), local)
        mine.start()
        first = [copy(0, me, sibling, src=x_ref)]
        first += [copy(1 + j, me, (*chip, c), src=x_ref) for j, chip in enumerate(chips)]
        for cp in first:
            cp.start()
        passed = [copy(4 + j, (*chip, c), sibling) for j, chip in enumerate(chips)]
        for j, chip in enumerate(chips):
            copy(1 + j, (*chip, c), me).wait_recv()
            passed[j].start()
        copy(0, sibling, me).wait_recv()
        for j, chip in enumerate(chips):
            copy(4 + j, (*chip, 1 - c), me).wait_recv()
        for cp in first + passed:
            cp.wait_send()
        mine.wait()
        tot = all_ref[0]
        for k in range(1, N_DEV):
            tot = tot + all_ref[k]
        sum_ref[...] = tot

    vm = pl.BlockSpec(memory_space=pltpu.VMEM)
    return pl.pallas_call(
        body, name="allreduce_small", in_specs=[vm], out_specs=[vm, vm],
        out_shape=[jax.ShapeDtypeStruct((rows, w), F32), jax.ShapeDtypeStruct((N_DEV, rows, w), F32)],
        scratch_shapes=[pltpu.SemaphoreType.DMA((7,)), pltpu.SemaphoreType.DMA((7,)), pltpu.SemaphoreType.DMA],
        compiler_params=pltpu.CompilerParams(vmem_limit_bytes=VMEM_LIMIT),
    )(v)[0]


BIG = ["ffn1_w1", "ffn1_w3", "ffn1_w2", "ffn2_w1", "ffn2_w3", "ffn2_w2", "ple_proj", "ple_gate",
       "s5_w_in", "s5_w_glu", "sb_w_qkv", "sb_w_o"]
TRANSPOSED = ("ffn1_w1", "ffn1_w3", "ffn2_w1", "ffn2_w3")
SMALL = ["ffn1_norm", "mix_norm", "ffn2_norm", "ple_norm", "s5_a_re", "s5_a_im", "s5_log_dt", "s5_b_re", "s5_b_im",
         "s5_c_re", "s5_c_im", "s5_d", "final_norm"]
ORDER = ["ffn1_norm", "ffn1_w1", "ffn1_w3", "ffn1_w2", "mix_norm", "ffn2_norm", "ffn2_w1", "ffn2_w3", "ffn2_w2",
         "ple_norm", "ple_proj", "ple_gate", "s5_w_in", "s5_a_re", "s5_a_im", "s5_log_dt", "s5_b_re", "s5_b_im",
         "s5_c_re", "s5_c_im", "s5_d", "s5_w_glu", "sb_w_qkv", "sb_w_o", "final_norm"]


def _pack(arrays):
    flat = jnp.concatenate([a.reshape(-1) for a in arrays])
    pad = (-flat.shape[0]) % 1024
    return jnp.pad(flat, (0, pad)).reshape(-1, 128)


def _unpack(packed, like):
    flat = packed.reshape(-1)
    out, off = [], 0
    for a in like:
        out.append(flat[off:off + a.size].reshape(a.shape))
        off += a.size
    return out


def _fwd_bwd(x, p, target, w, gathered):
    bl, l, d = x.shape
    t = bl * l
    depth = w["ffn1_norm"].shape[0]
    s5_ops, s5_vjp = jax.vjp(_s5_prep, w["s5_a_re"][0], w["s5_a_im"][0], w["s5_log_dt"][0], w["s5_b_re"][0],
                             w["s5_b_im"][0], w["s5_c_re"][0], w["s5_c_im"][0])

    h = x.reshape(t, d)
    p2 = [p[i].reshape(t, p.shape[-1]).astype(BF16) for i in range(depth)]
    saved = []
    for i in range(depth):
        norm = lambda name: w[name][i:i + 1]
        h, s1 = _ffn_fwd(h, norm("ffn1_norm"), gathered["ffn1_w1"], gathered["ffn1_w3"], gathered["ffn1_w2"], i)
        if i % 2 == 0:
            h, s2 = _s5_fwd(h, norm("mix_norm"), s5_ops, w["s5_d"][i // 2:i // 2 + 1], gathered["s5_w_in"], gathered["s5_w_glu"], bl)
        else:
            h, s2 = _sb_fwd(h, norm("mix_norm"), gathered["sb_w_qkv"], gathered["sb_w_o"], bl)
        h, s3 = _ffn_fwd(h, norm("ffn2_norm"), gathered["ffn2_w1"], gathered["ffn2_w3"], gathered["ffn2_w2"], i)
        h, s4 = _ple_fwd(h, norm("ple_norm"), p2[i], gathered["ple_proj"], gathered["ple_gate"], i)
        saved.append((s1, s2, s3, s4))

    loss, dh, dfinal = _head(h, w["final_norm"].reshape(1, d), target.reshape(t, d))

    big = {k: None for k in BIG}
    small = {k: [None] * w[k].shape[0] if w[k].ndim > 1 else None for k in SMALL}
    small["final_norm"] = dfinal.reshape(d)
    for i in reversed(range(depth)):
        norm = lambda name: w[name][i:i + 1]
        slots = lambda *names: [(big[k], i, depth) for k in names]
        s1, s2, s3, s4 = saved[i]
        dh, dhb, dg, big["ple_proj"], big["ple_gate"] = _ple_bwd(
            dh, s4, norm("ple_norm"), p2[i], gathered["ple_proj"], gathered["ple_gate"], i, slots("ple_proj", "ple_gate"))
        small["ple_norm"][i] = dg[0]
        dh, dhb, dg, big["ffn2_w1"], big["ffn2_w3"], big["ffn2_w2"] = _ffn_bwd(
            dh, dhb, s3, norm("ffn2_norm"), gathered["ffn2_w1"], gathered["ffn2_w3"], gathered["ffn2_w2"], i,
            slots("ffn2_w1", "ffn2_w3", "ffn2_w2"))
        small["ffn2_norm"][i] = dg[0]
        if i % 2 == 0:
            dh, dhb, dg, big["s5_w_in"], big["s5_w_glu"], dd, dops = _s5_bwd(
                dh, s2, norm("mix_norm"), s5_ops, w["s5_d"][i // 2:i // 2 + 1], gathered["s5_w_in"], gathered["s5_w_glu"], bl)
            small["s5_d"][0] = dd[0]
            raw = s5_vjp(dops)
            for name, gr in zip(["s5_a_re", "s5_a_im", "s5_log_dt", "s5_b_re", "s5_b_im", "s5_c_re", "s5_c_im"], raw):
                small[name][0] = gr
        else:
            dh, dhb, dg, big["sb_w_qkv"], big["sb_w_o"] = _sb_bwd(
                dh, dhb, s2, norm("mix_norm"), gathered["sb_w_qkv"], gathered["sb_w_o"], bl)
        small["mix_norm"][i] = dg[0]
        dh, dhb, dg, big["ffn1_w1"], big["ffn1_w3"], big["ffn1_w2"] = _ffn_bwd(
            dh, dhb, s1, norm("ffn1_norm"), gathered["ffn1_w1"], gathered["ffn1_w3"], gathered["ffn1_w2"], i,
            slots("ffn1_w1", "ffn1_w3", "ffn1_w2"))
        small["ffn1_norm"][i] = dg[0]
    small_list = [jnp.stack(small[k]) if isinstance(small[k], list) else small[k] for k in SMALL]
    return loss, dh.reshape(bl, l, d), big, small_list


def _step(x, p, target, w, m, v):
    flip = lambda tree: {k: jnp.swapaxes(a, 1, 2) if k in TRANSPOSED else a for k, a in tree.items()}
    w, m, v = flip(w), flip(m), flip(v)
    gathered = dict(zip(BIG, _allgather_weights([_to_bf16(w[k]) for k in BIG])))
    loss, grad_x, big, small_list = _fwd_bwd(x, p, target, w, gathered)

    c_idx = lax.axis_index("c").astype(jnp.int32).reshape(1)
    cx, cy = lax.axis_index("x"), lax.axis_index("y")
    where = jnp.stack([2 * cx + cy, 2 * (1 - cx) + cy, 2 * cx + (1 - cy)]).astype(jnp.int32)
    partial = [big[k] for k in BIG]
    pair = [_pair_sum(g, t, c_idx) for g, t in zip(partial, _pair_exchange(partial))]
    first = _chip_exchange_first(pair)
    second = _chip_exchange_second([_chip_relay_sum(pr, f, where) for pr, f in zip(pair, first)])
    mine = [_chip_sum(pr, f, s, where) for pr, f, s in zip(pair, first, second)]
    theirs = _pair_swap(mine)
    out_g, out_d, out_m, out_v = {}, {}, {}, {}
    for k, a, b in zip(BIG, mine, theirs):
        out_g[k], out_d[k], out_m[k], out_v[k] = _adamw_big(w[k], m[k], v[k], a, b, c_idx)

    like = [w[k] for k in SMALL]
    pad = [jnp.zeros((1,), F32)]
    g_small = _allreduce_small(_pack(small_list + [loss.reshape(1)]))
    packed = (g_small,) + tuple(_adamw_small(_pack(like + pad), g_small, _pack([m[k] for k in SMALL] + pad),
                                             _pack([v[k] for k in SMALL] + pad)))
    for dst, pk in zip((out_g, out_d, out_m, out_v), packed):
        dst.update(dict(zip(SMALL, _unpack(pk, like))))
    loss = g_small.reshape(-1)[sum(a.size for a in like)]
    out_g, out_d, out_m, out_v = flip(out_g), flip(out_d), flip(out_m), flip(out_v)
    return (loss, grad_x, *[out_g[k] for k in ORDER], *[out_d[k] for k in ORDER],
            *[out_m[k] for k in ORDER], *[out_v[k] for k in ORDER])


def kernel(x, p, ffn1_norm, ffn1_w1, ffn1_w3, ffn1_w2, mix_norm, ffn2_norm, ffn2_w1, ffn2_w3, ffn2_w2, ple_norm, ple_proj, ple_gate, s5_w_in, s5_a_re, s5_a_im, s5_log_dt, s5_b_re, s5_b_im, s5_c_re, s5_c_im, s5_d, s5_w_glu, sb_w_qkv, sb_w_o, final_norm, loss_target, m_ffn1_norm, m_ffn1_w1, m_ffn1_w3, m_ffn1_w2, m_mix_norm, m_ffn2_norm, m_ffn2_w1, m_ffn2_w3, m_ffn2_w2, m_ple_norm, m_ple_proj, m_ple_gate, m_s5_w_in, m_s5_a_re, m_s5_a_im, m_s5_log_dt, m_s5_b_re, m_s5_b_im, m_s5_c_re, m_s5_c_im, m_s5_d, m_s5_w_glu, m_sb_w_qkv, m_sb_w_o, m_final_norm, v_ffn1_norm, v_ffn1_w1, v_ffn1_w3, v_ffn1_w2, v_mix_norm, v_ffn2_norm, v_ffn2_w1, v_ffn2_w3, v_ffn2_w2, v_ple_norm, v_ple_proj, v_ple_gate, v_s5_w_in, v_s5_a_re, v_s5_a_im, v_s5_log_dt, v_s5_b_re, v_s5_b_im, v_s5_c_re, v_s5_c_im, v_s5_d, v_s5_w_glu, v_sb_w_qkv, v_sb_w_o, v_final_norm):
    args = dict(locals())
    w = {k: args[k] for k in ORDER}
    m = {k: args["m_" + k] for k in ORDER}
    v = {k: args["v_" + k] for k in ORDER}
    return _step(x, p, loss_target, w, m, v)
```

```python
import functools
import math

import jax
import jax.numpy as jnp
from jax import lax
from jax.experimental import pallas as pl
from jax.experimental.pallas import tpu as pltpu

F32 = jnp.float32
BF16 = jnp.bfloat16
MESH = pl.DeviceIdType.MESH

N_CHIPS = 4
N_DEV = 8
RMS_EPS = 1e-6
S5_GROUP = 16
S5_STATE = 64
S5_CHUNK = 16
SB_HEAD_DIM = 64
SB_BLOCK = 128
SB_CUT = -104.0
SB_UNROLL = 3
ADAM_LR, ADAM_B1, ADAM_B2, ADAM_EPS, ADAM_WD, ADAM_STEP = 0.001, 0.9, 0.999, 1e-08, 0.01, 10
VMEM_LIMIT = 56 * 1024 * 1024

NN = (((1,), (0,)), ((), ()))
NT = (((1,), (1,)), ((), ()))
TN = (((0,), (0,)), ((), ()))

ANY = pl.BlockSpec(memory_space=pl.ANY)


def _tile(n, target):
    if n <= target:
        return n
    for t in range(target - target % 8, 7, -8):
        if n % t == 0:
            return t
    raise ValueError(f"no row tile for {n}")


def _params(*semantics):
    return pltpu.CompilerParams(dimension_semantics=semantics, vmem_limit_bytes=VMEM_LIMIT)


def _sigmoid(v):
    return 1.0 / (1.0 + jnp.exp(-v))


def _gemm(name, grid, operands, in_specs, groups, acc_shapes, out_shapes, out_specs, epilogue, reduce_axis=None, aliases=None):
    n_in, n_out = len(operands), len(out_shapes)
    n_red = None if reduce_axis is None else grid[reduce_axis]

    def body(*refs):
        ins, outs, accs = refs[:n_in], refs[n_in:n_in + n_out], refs[n_in + n_out:]

        def products():
            res = []
            for terms in groups:
                tot = None
                for ia, ib, dims in terms:
                    d = lax.dot_general(ins[ia][...], ins[ib][...], dims, preferred_element_type=F32)
                    tot = d if tot is None else tot + d
                res.append(tot)
            return res

        def finish(vals):
            for o, v in zip(outs, epilogue(vals, ins)):
                o[...] = v.astype(o.dtype)

        if reduce_axis is None:
            finish(products())
        else:
            k = pl.program_id(reduce_axis)

            @pl.when(k == 0)
            def _():
                for a in accs:
                    a[...] = jnp.zeros_like(a)

            for a, d in zip(accs, products()):
                a[...] += d

            @pl.when(k == n_red - 1)
            def _():
                finish([a[...] for a in accs])

    scratch = [] if reduce_axis is None else [pltpu.VMEM(s, F32) for s in acc_shapes]
    sem = tuple("arbitrary" if i == reduce_axis else "parallel" for i in range(len(grid)))
    return pl.pallas_call(
        body, name=name, grid=grid, in_specs=in_specs, out_specs=out_specs, out_shape=out_shapes,
        scratch_shapes=scratch, input_output_aliases=aliases or {}, compiler_params=_params(*sem))(*operands)


def _ident(vals, ins):
    return vals


def _act_spec(layout, tm, cs, pos):
    if layout == "sm":
        return pl.BlockSpec((None, tm, cs), lambda *g: (pos(*g)[1], pos(*g)[0], 0))
    return pl.BlockSpec((tm, cs), lambda *g: pos(*g))


def _act_shape(layout, t, cs, dtype):
    return jax.ShapeDtypeStruct((N_CHIPS, t, cs) if layout == "sm" else (t, N_CHIPS * cs), dtype)


def _w_spec(w, layer, pos_k):
    _, _, r, c = w.shape
    return pl.BlockSpec((None, None, r, c), lambda *g: (pos_k(*g), layer, 0, 0))


def _mm_cs(name, x, w, layer, out_layout, out_dtype, tm=1024):
    t, kd = x.shape
    cs = w.shape[3]
    tm = _tile(t, tm)
    return _gemm(
        name, (N_CHIPS, t // tm), [x, w],
        [pl.BlockSpec((tm, kd), lambda k, i: (i, 0)), _w_spec(w, layer, lambda k, i: k)],
        [[(0, 1, NN)]], None, [_act_shape(out_layout, t, cs, out_dtype)],
        [_act_spec(out_layout, tm, cs, lambda k, i: (i, k))], _ident)[0]


def _mm_rs(name, xs, layout, w, layer, res=None, alpha=1.0, out_dtype=F32, tm=1024, gated=None):
    ks, n = w.shape[2], w.shape[3]
    t = xs.shape[1] if layout == "sm" else xs.shape[0]
    tm = _tile(t, tm)
    row = pl.BlockSpec((tm, n), lambda i: (i, 0))
    extras = [a for a in (res, gated) if a is not None]
    shards = range(N_CHIPS)
    operands = [xs] * N_CHIPS + [w] * N_CHIPS + extras
    specs = ([_act_spec(layout, tm, ks, lambda i, k=k: (i, k)) for k in shards]
             + [_w_spec(w, layer, lambda i, k=k: k) for k in shards] + [row] * len(extras))
    base = 2 * N_CHIPS

    def epilogue(vals, ins):
        y = alpha * vals[0]
        if gated is not None:
            return [y, ins[base][...] + ins[base + 1][...] * _sigmoid(y)]
        return [y if res is None else ins[base][...] + y]

    outs = _gemm(
        name, (t // tm,), operands, specs, [[(k, N_CHIPS + k, NN) for k in shards]], None,
        [jax.ShapeDtypeStruct((t, n), out_dtype)] * (1 if gated is None else 2), [row] * (1 if gated is None else 2),
        epilogue)
    return outs[0] if gated is None else outs


def _mm_cs_dx(name, pairs, layout, layer, tm=1024, transposed=False, norm=None):
    w0 = pairs[0][1]
    kd, cs = (w0.shape[3], w0.shape[2]) if transposed else (w0.shape[2], w0.shape[3])
    dy0 = pairs[0][0]
    t = dy0.shape[1] if layout == "sm" else dy0.shape[0]
    tm = _tile(t, tm // 2)
    operands, specs, terms = [], [], []
    for dy, w in pairs:
        for k in range(N_CHIPS):
            terms.append((len(operands), len(operands) + 1, NN if transposed else NT))
            operands += [dy, w]
            specs += [_act_spec(layout, tm, cs, lambda i, k=k: (i, k)), _w_spec(w, layer, lambda i, k=k: k)]
    row = pl.BlockSpec((tm, kd), lambda i: (i, 0))
    if norm is None:
        return _gemm(name, (t // tm,), operands, specs, [terms], None,
                     [jax.ShapeDtypeStruct((t, kd), F32)], [row], _ident)[0]
    base = len(operands)
    operands += list(norm)
    specs += [row, row, pl.BlockSpec(norm[2].shape, lambda i: (0, 0))]

    def epilogue(vals, ins):
        dx, dg = _rms_bwd_math(vals[0], ins[base + 1][...], ins[base + 2][...])
        dh = ins[base][...] + dx
        return [dh, dh, dg]

    dh, dhb, dg = _gemm(
        name, (t // tm,), operands, specs, [terms], None,
        [jax.ShapeDtypeStruct((t, kd), F32), jax.ShapeDtypeStruct((t, kd), BF16), jax.ShapeDtypeStruct((t // tm, 1, kd), F32)],
        [row, row, pl.BlockSpec((None, 1, kd), lambda i: (i, 0, 0))], epilogue)
    return dh, dhb, dg.sum(axis=0)


def _mm_rs_dx(name, dy, w, layer, out_layout, out_dtype, tm=1024):
    t, n = dy.shape
    ks = w.shape[2]
    tm = _tile(t, tm)
    return _gemm(
        name, (N_CHIPS, t // tm), [dy, w],
        [pl.BlockSpec((tm, n), lambda k, i: (i, 0)), _w_spec(w, layer, lambda k, i: k)],
        [[(0, 1, NT)]], None, [_act_shape(out_layout, t, ks, out_dtype)],
        [_act_spec(out_layout, tm, ks, lambda k, i: (i, k))], _ident)[0]


def _mm_dw(name, x, x_layout, dy, dy_layout, slot, alpha=1.0, tk=4096):
    stack, layer, layers = slot
    if x_layout is None:
        t, rows = x.shape
        cols = dy.shape[2] if dy_layout == "sm" else dy.shape[1] // N_CHIPS
        tk = _tile(t, tk)
        xspec = pl.BlockSpec((tk, rows), lambda k, j: (j, 0))
        yspec = _act_spec(dy_layout, tk, cols, lambda k, j: (j, k))
    else:
        t, cols = dy.shape
        rows = x.shape[2] if x_layout == "sm" else x.shape[1] // N_CHIPS
        tk = _tile(t, tk)
        xspec = _act_spec(x_layout, tk, rows, lambda k, j: (j, k))
        yspec = pl.BlockSpec((tk, cols), lambda k, j: (j, 0))
    operands, specs = [x, dy], [xspec, yspec]
    if stack is not None:
        operands.append(stack)
        specs.append(ANY)
    return _gemm(
        name, (N_CHIPS, t // tk), operands, specs, [[(0, 1, TN)]], [(rows, cols)],
        [jax.ShapeDtypeStruct((N_CHIPS, layers, rows, cols), F32)],
        [pl.BlockSpec((None, None, rows, cols), lambda k, j: (k, layer, 0, 0))],
        lambda vals, ins: [alpha * vals[0]], reduce_axis=1 if t // tk > 1 else None,
        aliases=None if stack is None else {2: 0})[0]


def _rows(name, fn, ins, outs, accs=(), tm=512):
    t = ins[0].shape[0]
    tm = _tile(t, tm)
    n_in, n_out, n_acc = len(ins), len(outs), len(accs)
    in_specs = []
    for a in ins:
        if a.shape[0] == t:
            in_specs.append(pl.BlockSpec((tm, a.shape[1]), lambda i: (i, 0)))
        else:
            in_specs.append(pl.BlockSpec(a.shape, lambda i: (0, 0)))
    out_shape = [jax.ShapeDtypeStruct((t, c), d) for c, d in outs] + [jax.ShapeDtypeStruct(s, F32) for s in accs]
    out_specs = [pl.BlockSpec((tm, c), lambda i: (i, 0)) for c, _ in outs] + [pl.BlockSpec(s, lambda i: (0, 0)) for s in accs]

    def body(*refs):
        i = pl.program_id(0)
        row_vals, acc_vals = fn(*[r[...] for r in refs[:n_in]])
        for o, v in zip(refs[n_in:n_in + n_out], row_vals):
            o[...] = v.astype(o.dtype)
        acc_refs = refs[n_in + n_out:]
        if n_acc:
            @pl.when(i == 0)
            def _():
                for a in acc_refs:
                    a[...] = jnp.zeros_like(a)

            for a, v in zip(acc_refs, acc_vals):
                a[...] += v

    res = pl.pallas_call(
        body, name=name, grid=(t // tm,), in_specs=in_specs, out_specs=out_specs, out_shape=out_shape,
        compiler_params=_params("arbitrary" if n_acc else "parallel"))(*ins)
    return res[:n_out], res[n_out:]


def _to_bf16(a):
    def fn(x):
        return [x], []
    return _rows("weights_bf16", fn, [a.reshape(-1, a.shape[-1])], [(a.shape[-1], BF16)], tm=512)[0][0].reshape(a.shape)


def _rms_stats(x):
    return lax.rsqrt(jnp.mean(x * x, axis=-1, keepdims=True) + RMS_EPS)


def _rmsnorm(name, h, g):
    def fn(x, gg):
        return [x * _rms_stats(x) * gg], []
    return _rows(name, fn, [h, g], [(h.shape[1], BF16)])[0][0]


def _rms_bwd_math(dn, x, g):
    r = _rms_stats(x)
    xhat = x * r
    dxh = dn * g
    dx = r * (dxh - xhat * jnp.mean(dxh * xhat, axis=-1, keepdims=True))
    return dx, jnp.sum(dn * xhat, axis=0, keepdims=True)


def _rmsnorm_bwd(name, dres, dn, h, g):
    def fn(dr, d, x, gg):
        dx, dg = _rms_bwd_math(d, x, gg)
        return [dr + dx, dr + dx], [dg]
    (dh, dhb), (dg,) = _rows(name, fn, [dres, dn, h, g], [(h.shape[1], F32), (h.shape[1], BF16)], [(1, h.shape[1])])
    return dh, dhb, dg


def _ffn_fwd(h, g, w1, w3, w2, layer, tm=1024):
    t, d = h.shape
    fs = w1.shape[2]
    n = _rmsnorm("ffn_norm", h, g)
    tm = _tile(t, tm)

    def up(vals, ins):
        a, b = vals
        sg = _sigmoid(a)
        silu = a * sg
        return [b * sg * (1.0 + a * (1.0 - sg)), silu, silu * b]

    sm = _act_shape("sm", t, fs, BF16)
    osp = _act_spec("sm", tm, fs, lambda k, i: (i, k))
    ga, gb, s = _gemm(
        "ffn_up", (N_CHIPS, t // tm), [n, w1, w3],
        [pl.BlockSpec((tm, d), lambda k, i: (i, 0)), _w_spec(w1, layer, lambda k, i: k), _w_spec(w3, layer, lambda k, i: k)],
        [[(0, 1, NT)], [(0, 2, NT)]], None, [sm, sm, sm], [osp, osp, osp], up)
    out = _mm_rs("ffn_down", s, "sm", w2, layer, res=h, alpha=0.5)
    return out, (h, n, ga, gb, s)


def _ffn_bwd(dout, dob, saved, g, w1, w3, w2, layer, slots, tm=1024):
    h, n, ga, gb, s = saved
    t, d = h.shape
    fs = w1.shape[2]
    tm = _tile(t, tm)

    def down(vals, ins):
        ds = 0.5 * vals[0]
        return [ds * ins[2][...].astype(F32), ds * ins[3][...].astype(F32)]

    sm = _act_shape("sm", t, fs, BF16)
    asp = _act_spec("sm", tm, fs, lambda k, i: (i, k))
    da, db = _gemm(
        "ffn_down_dx", (N_CHIPS, t // tm), [dob, w2, ga, gb],
        [pl.BlockSpec((tm, d), lambda k, i: (i, 0)), _w_spec(w2, layer, lambda k, i: k), asp, asp],
        [[(0, 1, NT)]], None, [sm, sm], [asp, asp], down)
    dw2 = _mm_dw("ffn_dw2", s, "sm", dob, None, slots[2], alpha=0.5)
    dw1 = _mm_dw("ffn_dw1", da, "sm", n, None, slots[0])
    dw3 = _mm_dw("ffn_dw3", db, "sm", n, None, slots[1])
    dh, dhb, dg = _mm_cs_dx("ffn_up_dx", [(da, w1), (db, w3)], "sm", layer, transposed=True, norm=(dout, h, g))
    return dh, dhb, dg, dw1, dw3, dw2


def _ple_fwd(h, g, p2, wproj, wgate, layer):
    n = _rmsnorm("ple_norm", h, g)
    pp = _mm_cs("ple_proj", p2, wproj, layer, "flat", F32)
    gl, out = _mm_rs("ple_gate", n, "flat", wgate, layer, res=h, gated=pp)
    return out, (h, n, gl, pp)


def _ple_bwd(dout, saved, g, p2, wproj, wgate, layer, slots):
    h, n, gl, pp = saved
    d = h.shape[1]

    def fn(do, gg, q):
        sg = _sigmoid(gg)
        return [do * sg, do * q * sg * (1.0 - sg)], []
    (dpp, dgl), _ = _rows("ple_mix_bwd", fn, [dout, gl, pp], [(d, BF16), (d, BF16)])
    dwproj = _mm_dw("ple_dwproj", p2, None, dpp, "flat", slots[0])
    dwgate = _mm_dw("ple_dwgate", n, "flat", dgl, None, slots[1])
    dn = _mm_rs_dx("ple_gate_dx", dgl, wgate, layer, "flat", F32)
    dh, dhb, dg = _rmsnorm_bwd("ple_norm_bwd", dout, dn, h, g)
    return dh, dhb, dg, dwproj, dwgate


def _head(h, g, target):
    d = h.shape[1]

    def fn(x, gg, tg):
        y = x * _rms_stats(x) * gg
        err = y - tg
        dy = err * (1.0 / d)
        dx, dg = _rms_bwd_math(dy, x, gg)
        loss = 0.5 * jnp.sum(jnp.sum(err * err, axis=-1, keepdims=True) * (1.0 / d), axis=0, keepdims=True)
        return [dx], [dg, jnp.broadcast_to(loss, (1, 128))]
    (dh,), (dg, loss) = _rows("loss_head", fn, [h, g, target], [(d, F32)], [(1, d), (1, 128)])
    return loss[0, 0], dh, dg


S5_LANES = 2 * S5_STATE
S5_GB = 128 // S5_GROUP


def _s5_prep(a_re, a_im, log_dt, b_re, b_im, c_re, c_im):
    c, gb = S5_CHUNK, S5_GB
    g = a_re.shape[0]
    nb = g // gb
    lam_re = jnp.minimum(a_re, -1e-4)
    lam_im = a_im
    dt = jnp.exp(log_dt)[:, None, None]
    ks = jnp.arange(c + 1, dtype=F32)
    mag = jnp.exp(lam_re[..., None] * dt * ks)
    ph = lam_im[..., None] * dt * ks
    pw_re, pw_im = mag * jnp.cos(ph), mag * jnp.sin(ph)
    den = lam_re * lam_re + lam_im * lam_im
    nr, ni = pw_re[..., 1] - 1.0, pw_im[..., 1]
    fr = (nr * lam_re + ni * lam_im) / den
    fi = (ni * lam_re - nr * lam_im) / den
    bb_re = fr[..., None] * b_re - fi[..., None] * b_im
    bb_im = fr[..., None] * b_im + fi[..., None] * b_re
    ct_re, ct_im = c_re.transpose(0, 2, 1), c_im.transpose(0, 2, 1)
    ca_re = ct_re[:, :, None, :] * pw_re[..., None] - ct_im[:, :, None, :] * pw_im[..., None]
    ca_im = ct_re[:, :, None, :] * pw_im[..., None] + ct_im[:, :, None, :] * pw_re[..., None]
    hp = lax.Precision.HIGHEST
    kern = (jnp.einsum("gpj,gpkh->gkjh", bb_re, ca_re[:, :, :c], precision=hp)
            - jnp.einsum("gpj,gpkh->gkjh", bb_im, ca_im[:, :, :c], precision=hp))
    rev_re = pw_re[:, :, :c][:, :, ::-1].transpose(0, 2, 1)
    rev_im = pw_im[:, :, :c][:, :, ::-1].transpose(0, 2, 1)
    bt_re, bt_im = bb_re.transpose(0, 2, 1), bb_im.transpose(0, 2, 1)
    wn_re = rev_re[:, :, None, :] * bt_re[:, None] - rev_im[:, :, None, :] * bt_im[:, None]
    wn_im = rev_re[:, :, None, :] * bt_im[:, None] + rev_im[:, :, None, :] * bt_re[:, None]
    wn = jnp.concatenate([wn_re, wn_im], axis=-1)
    wo = jnp.concatenate([ca_re[:, :, 1:].transpose(0, 2, 3, 1), -ca_im[:, :, 1:].transpose(0, 2, 3, 1)], axis=-1)

    def blocks(x):
        return x.reshape(nb, gb, c, S5_GROUP, x.shape[3]).transpose(0, 2, 1, 3, 4).reshape(nb, c, gb * S5_GROUP, x.shape[3])

    ar, ai = pw_re[..., c], pw_im[..., c]
    return (jnp.tile(blocks(kern), (1, 1, 1, gb)), blocks(wn), blocks(wo),
            jnp.concatenate([ar, ar], axis=1), jnp.concatenate([-ai, ai], axis=1))


def _step_rows(ref, tau, n):
    return ref[pl.ds(tau, n, stride=S5_CHUNK), :].astype(BF16)


def _cat_groups(ref, dtype):
    return jnp.concatenate([ref[:, j, :] for j in range(S5_GB)], axis=1).astype(dtype)


def _cat_steps(ref, n):
    return jnp.concatenate([_step_rows(ref, tau, n) for tau in range(S5_CHUNK)], axis=1)


def _stack_steps(ref, n):
    return jnp.concatenate([_step_rows(ref, tau, n) for tau in range(S5_CHUNK)], axis=0)


def _cat_ops(ref, axis, reverse=False):
    order = range(S5_CHUNK - 1, -1, -1) if reverse else range(S5_CHUNK)
    return jnp.concatenate([ref[k] for k in order], axis=axis)


def _row_group(rows, lanes):
    row = (lax.broadcasted_iota(jnp.int32, (rows, lanes), 0) // S5_GROUP) % S5_GB
    lane = (lax.broadcasted_iota(jnp.int32, (rows, lanes), 1) // S5_GROUP) % S5_GB
    return row, lane


def _own_group(x):
    row, lane = _row_group(*x.shape)
    return jnp.where(row == lane, x, jnp.zeros_like(x))


def _spread(x):
    row, _ = _row_group(*x.shape)
    return jnp.concatenate([jnp.where(row == j, x, jnp.zeros_like(x)) for j in range(S5_GB)], axis=1)


def _gather_own(x):
    row, _ = _row_group(x.shape[0], S5_LANES)
    out = jnp.zeros((x.shape[0], S5_LANES), x.dtype)
    for j in range(S5_GB):
        out = out + jnp.where(row == j, x[:, j * S5_LANES:(j + 1) * S5_LANES], 0.0)
    return out


def _s5_specs(t, d):
    nct, g = t // S5_CHUNK, d // S5_GROUP
    tok = pl.BlockSpec((t, 128), lambda i: (0, i))
    st = pl.BlockSpec((nct, S5_GB, S5_LANES), lambda i: (0, i, 0))
    op = lambda w: pl.BlockSpec((None,) + w.shape[1:], lambda i: (i, 0, 0, 0))
    return nct, g, tok, st, op


def _s5_chunk_fwd(u, bd, bn):
    t, d = u.shape
    nct, g, tok, st, op = _s5_specs(t, d)
    c = S5_CHUNK

    def body(u_ref, bd_ref, bn_ref, y_ref, s_ref):
        ucat = _cat_steps(u_ref, nct)
        sloc = jnp.dot(ucat, _spread(_cat_ops(bn_ref, 0)), preferred_element_type=F32)
        for j in range(S5_GB):
            s_ref[:, j, :] = sloc[:, j * S5_LANES:(j + 1) * S5_LANES]
        lags = _own_group(_cat_ops(bd_ref, 0, reverse=True))
        for tt in range(c):
            y_ref[pl.ds(tt, nct, stride=c), :] = jnp.dot(ucat[:, :(tt + 1) * 128], lags[(c - 1 - tt) * 128:, :],
                                                         preferred_element_type=F32)

    return pl.pallas_call(
        body, name="s5_chunk", grid=(d // 128,), in_specs=[tok, op(bd), op(bn)], out_specs=[tok, st],
        out_shape=[jax.ShapeDtypeStruct((t, d), F32), jax.ShapeDtypeStruct((nct, g, S5_LANES), F32)],
        compiler_params=_params("parallel"))(u, bd, bn)


def _s5_state_out(sprev, co, yin):
    t, d = yin.shape
    nct, g, tok, st, op = _s5_specs(t, d)
    c = S5_CHUNK

    def body(s_ref, co_ref, yi_ref, y_ref):
        ys = lax.dot_general(_cat_groups(s_ref, BF16), _spread(_cat_ops(co_ref, 0)), NT,
                             preferred_element_type=F32)
        for tt in range(c):
            rows = pl.ds(tt, nct, stride=c)
            y_ref[rows, :] = yi_ref[rows, :] + ys[:, tt * 128:(tt + 1) * 128]

    return pl.pallas_call(
        body, name="s5_state_out", grid=(d // 128,), in_specs=[st, op(co), tok], out_specs=tok,
        out_shape=jax.ShapeDtypeStruct((t, d), F32), compiler_params=_params("parallel"))(sprev, co, yin)


def _s5_state_out_dx(dyb, co):
    t, d = dyb.shape
    nct, g, tok, st, op = _s5_specs(t, d)
    c = S5_CHUNK

    def body(dy_ref, co_ref, ds_ref):
        acc = jnp.dot(_cat_steps(dy_ref, nct), _spread(_cat_ops(co_ref, 0)), preferred_element_type=F32)
        for j in range(S5_GB):
            ds_ref[:, j, :] = acc[:, j * S5_LANES:(j + 1) * S5_LANES]

    return pl.pallas_call(
        body, name="s5_state_out_dx", grid=(d // 128,), in_specs=[tok, op(co)], out_specs=st,
        out_shape=jax.ShapeDtypeStruct((nct, g, S5_LANES), F32), compiler_params=_params("parallel"))(dyb, co)


def _s5_chunk_dx(dyb, dsloc, bd, bn, skip):
    t, d = dyb.shape
    nct, g, tok, st, op = _s5_specs(t, d)
    c = S5_CHUNK

    def body(dy_ref, ds_ref, bd_ref, bn_ref, sk_ref, du_ref):
        dus = lax.dot_general(_cat_groups(ds_ref, BF16), _spread(_cat_ops(bn_ref, 0)), NT, preferred_element_type=F32)
        dycat = _cat_steps(dy_ref, nct)
        lags = _own_group(_cat_ops(bd_ref, 1))
        for tau in range(c):
            rows = pl.ds(tau, nct, stride=c)
            du_ref[rows, :] = (sk_ref[rows, :] + dus[:, tau * 128:(tau + 1) * 128]
                               + lax.dot_general(dycat[:, tau * 128:], lags[:, :(c - tau) * 128], NT,
                                                 preferred_element_type=F32))

    return pl.pallas_call(
        body, name="s5_chunk_dx", grid=(d // 128,), in_specs=[tok, st, op(bd), op(bn), tok], out_specs=tok,
        out_shape=jax.ShapeDtypeStruct((t, d), F32), compiler_params=_params("parallel"))(dyb, dsloc, bd, bn, skip)


def _s5_chunk_dw(u, dyb, dsloc, bd, bn):
    t, d = u.shape
    nct, g, tok, st, op = _s5_specs(t, d)
    c = S5_CHUNK

    def body(u_ref, dy_ref, ds_ref, dbd_ref, dbn_ref):
        dbn = _gather_own(lax.dot_general(_cat_steps(u_ref, nct), _cat_groups(ds_ref, BF16), TN,
                                          preferred_element_type=F32))
        for tau in range(c):
            dbn_ref[tau] = dbn[tau * 128:(tau + 1) * 128, :]
        ustk, dystk = _stack_steps(u_ref, nct), _stack_steps(dy_ref, nct)
        for k in range(c):
            dbd_ref[k] = _own_group(lax.dot_general(ustk[:(c - k) * nct], dystk[k * nct:], TN,
                                                    preferred_element_type=F32))

    return pl.pallas_call(
        body, name="s5_chunk_dw", grid=(d // 128,), in_specs=[tok, tok, st], out_specs=[op(bd), op(bn)],
        out_shape=[jax.ShapeDtypeStruct(bd.shape, F32), jax.ShapeDtypeStruct(bn.shape, F32)],
        compiler_params=_params("parallel"))(u, dyb, dsloc)


def _s5_state_out_dw(sprev, dyb, co):
    t, d = dyb.shape
    nct, g, tok, st, op = _s5_specs(t, d)
    c = S5_CHUNK

    def body(s_ref, dy_ref, dco_ref):
        dco = _gather_own(lax.dot_general(_cat_steps(dy_ref, nct), _cat_groups(s_ref, BF16), TN,
                                          preferred_element_type=F32))
        for tt in range(c):
            dco_ref[tt] = dco[tt * 128:(tt + 1) * 128, :]

    return pl.pallas_call(
        body, name="s5_state_out_dw", grid=(d // 128,), in_specs=[st, tok], out_specs=op(co),
        out_shape=jax.ShapeDtypeStruct(co.shape, F32), compiler_params=_params("parallel"))(sprev, dyb)


def _s5_scan_fwd(sloc, m1, m2):
    bl, nc, g, w = sloc.shape

    def body(s_ref, m1_ref, m2_ref, o_ref):
        a1, a2 = m1_ref[...], m2_ref[...]

        def step(c, states):
            new = []
            for b, s in enumerate(states):
                o_ref[b, c] = s
                new.append(a1 * s + a2 * pltpu.roll(s, S5_STATE, 1) + s_ref[b, c])
            return tuple(new)
        lax.fori_loop(0, nc, step, tuple(jnp.zeros((g, w), F32) for _ in range(bl)))

    vm = pl.BlockSpec(memory_space=pltpu.VMEM)
    return pl.pallas_call(
        body, name="s5_scan", in_specs=[vm, vm, vm], out_specs=vm,
        out_shape=jax.ShapeDtypeStruct(sloc.shape, F32),
        compiler_params=pltpu.CompilerParams(vmem_limit_bytes=VMEM_LIMIT))(sloc, m1, m2)


def _s5_scan_bwd(dsprev, sprev, m1, m2):
    bl, nc, g, w = dsprev.shape

    def body(d_ref, s_ref, m1_ref, m2_ref, g_ref, p1_ref, p2_ref):
        a1, a2 = m1_ref[...], m2_ref[...]
        zero = jnp.zeros((g, w), F32)

        def step(i, carry):
            gps, p1, p2 = carry
            c = nc - 2 - i
            new = []
            for b, gp in enumerate(gps):
                g_ref[b, c] = gp
                sp = s_ref[b, c]
                p1 = p1 + gp * sp
                p2 = p2 + gp * pltpu.roll(sp, S5_STATE, 1)
                new.append(d_ref[b, c] + a1 * gp - a2 * pltpu.roll(gp, S5_STATE, 1))
            return tuple(new), p1, p2

        for b in range(bl):
            g_ref[b, nc - 1] = zero
        _, p1, p2 = lax.fori_loop(0, nc - 1, step, (tuple(d_ref[b, nc - 1] for b in range(bl)), zero, zero))
        p1_ref[...] = p1
        p2_ref[...] = p2

    vm = pl.BlockSpec(memory_space=pltpu.VMEM)
    sd = jax.ShapeDtypeStruct
    return pl.pallas_call(
        body, name="s5_scan_bwd", in_specs=[vm, vm, vm, vm], out_specs=[vm, vm, vm],
        out_shape=[sd(dsprev.shape, F32), sd((g, w), F32), sd((g, w), F32)],
        compiler_params=pltpu.CompilerParams(vmem_limit_bytes=VMEM_LIMIT))(dsprev, sprev, m1, m2)


def _gelu_tanh_parts(y):
    c0 = math.sqrt(2.0 / math.pi)
    inner = c0 * (y + 0.044715 * y * y * y)
    th = jnp.tanh(inner)
    return th, c0 * (1.0 + 3 * 0.044715 * y * y)


def _s5_fwd(h, g, ops, d_skip, w_in, w_glu, bl):
    bd, bn, co, m1, m2 = ops
    t, d = h.shape
    nct, groups = t // S5_CHUNK, d // S5_GROUP
    hn = _rmsnorm("mix_norm", h, g)
    u = _mm_rs("s5_in", hn, "flat", w_in, 0)
    yin, sloc = _s5_chunk_fwd(u, bd.astype(BF16), bn.astype(BF16))
    sprev = _s5_scan_fwd(sloc.reshape(bl, nct // bl, groups, S5_LANES), m1, m2).reshape(nct, groups, S5_LANES)
    y = _s5_state_out(sprev, co.astype(BF16), yin)

    def fn(yy, uu, dd):
        y2 = yy + dd * uu
        th, _ = _gelu_tanh_parts(y2)
        return [0.5 * y2 * (1.0 + th)], []
    z = _rows("s5_gelu", fn, [y, u, d_skip], [(d, BF16)])[0][0]
    zz = _mm_cs("s5_glu", z, w_glu, 0, "flat", F32)

    def glu(hh, zv):
        return [hh + zv[:, :d] * _sigmoid(zv[:, d:])], []
    out = _rows("s5_glu_mix", glu, [h, zz], [(d, F32)])[0][0]
    return out, (h, hn, u, sprev, y, z, zz)


def _s5_bwd(dout, saved, g, ops, d_skip, w_in, w_glu, bl):
    h, hn, u, sprev, y, z, zz = saved
    bd, bn, co, m1, m2 = ops
    t, d = h.shape
    nct, groups = t // S5_CHUNK, d // S5_GROUP

    def glu_bwd(do, zv):
        sg = _sigmoid(zv[:, d:])
        return [jnp.concatenate([do * sg, do * zv[:, :d] * sg * (1.0 - sg)], axis=1)], []
    dzz = _rows("s5_glu_bwd", glu_bwd, [dout, zz], [(2 * d, BF16)])[0][0]
    dwglu = _mm_dw("s5_dwglu", z, None, dzz, "flat", (None, 0, 1))
    dz = _mm_cs_dx("s5_glu_dx", [(dzz, w_glu)], "flat", 0)

    def gelu_bwd(dzv, yy, uu, dd):
        y2 = yy + dd * uu
        th, dinner = _gelu_tanh_parts(y2)
        dy2 = dzv * (0.5 * (1.0 + th) + 0.5 * y2 * (1.0 - th * th) * dinner)
        return [dy2, dy2 * dd], [jnp.sum(dy2 * uu, axis=0, keepdims=True)]
    (dyb, du_skip), (dd,) = _rows("s5_gelu_bwd", gelu_bwd, [dz, y, u, d_skip], [(d, F32), (d, F32)], [(1, d)])
    bd_b, bn_b, co_b = bd.astype(BF16), bn.astype(BF16), co.astype(BF16)
    dsprev = _s5_state_out_dx(dyb, co_b)
    shape4 = (bl, nct // bl, groups, S5_LANES)
    dsloc, dm1, dm2 = _s5_scan_bwd(dsprev.reshape(shape4), sprev.reshape(shape4), m1, m2)
    dsloc = dsloc.reshape(nct, groups, S5_LANES)
    du = _s5_chunk_dx(dyb, dsloc, bd_b, bn_b, du_skip).astype(BF16)
    dbd, dbn = _s5_chunk_dw(u, dyb, dsloc, bd, bn)
    dco = _s5_state_out_dw(sprev, dyb, co)
    dwin = _mm_dw("s5_dwin", hn, "flat", du, None, (None, 0, 1))
    dhn = _mm_rs_dx("s5_in_dx", du, w_in, 0, "flat", F32)
    dh, dhb, dg = _rmsnorm_bwd("mix_norm_bwd", dout, dhn, h, g)
    return dh, dhb, dg, dwin, dwglu, dd, (dbd, dbn, dco, dm1, dm2)


def _sb_block(qi, idx, tb):
    kb = qi - idx
    return pl.multiple_of(jnp.maximum(kb, 0) * tb, tb), idx == 0, kb >= 0


def _sb_scores(q, kblk, diag, exists, row, col):
    z = lax.dot_general(q, kblk, NT, preferred_element_type=F32) * (SB_HEAD_DIM ** -0.5)
    l1 = jnp.log(1.0 + jnp.exp(-jnp.abs(z)))
    ls = jnp.minimum(z, 0.0) - l1
    mask = jnp.logical_and(jnp.logical_or(col < row, jnp.logical_not(diag)), exists)
    lk = jnp.where(mask, ls - z, 0.0)
    return ls, lk, mask


def _split_dot(v, tri):
    hi = v.astype(BF16)
    lo = (v - hi.astype(F32)).astype(BF16)
    return (jnp.dot(hi, tri, preferred_element_type=F32) + jnp.dot(lo, tri, preferred_element_type=F32))


SB_PAIR =2 * SB_HEAD_DIM


def _pair_masks(tb):
    lane = lax.broadcasted_iota(jnp.int32, (1, SB_PAIR), 1)
    row = lax.broadcasted_iota(jnp.int32, (tb, tb), 0)
    col = lax.broadcasted_iota(jnp.int32, (tb, tb), 1)
    return [lane < SB_HEAD_DIM, lane >= SB_HEAD_DIM], row, col


def _pair_more(qi, carry):
    j, crs = carry[0], carry[2]
    return jnp.logical_and(j <= qi, jnp.maximum(jnp.max(crs[0]), jnp.max(crs[1])) > SB_CUT)


def _pair_specs(bl, l, d, tb):
    nq, off = l // tb, d // SB_PAIR
    qspec = pl.BlockSpec((tb, SB_PAIR), lambda b, p, i: (b * nq + i, p))
    kspec = pl.BlockSpec((l, SB_PAIR), lambda b, p, i: (b, off + p))
    vspec = pl.BlockSpec((l, SB_PAIR), lambda b, p, i: (b, 2 * off + p))
    return qspec, kspec, vspec


def _sb_attn_fwd2(qkv, bl):
    t, d3 = qkv.shape
    d, l = d3 // 3, t // bl
    tb = min(SB_BLOCK, l)
    nq = l // tb

    def body(q_ref, k_ref, v_ref, o_ref, ob_ref):
        qi = pl.program_id(2)
        heads, row, col = _pair_masks(tb)
        qv = q_ref[...]
        qh = [jnp.where(m, qv, jnp.zeros_like(qv)) for m in heads]
        tri = (row > col).astype(BF16)

        def step(carry):
            j, acc, crs = carry
            crs = list(crs)
            where = [_sb_block(qi, j + u, tb) for u in range(SB_UNROLL)]
            kblks = [k_ref[pl.ds(ks, tb), :] for ks, _, _ in where]
            scores = [[_sb_scores(qh[hd], kblks[u], where[u][1], where[u][2], row, col) for hd in range(2)]
                      for u in range(SB_UNROLL)]
            laters = [[_split_dot(sc[1], tri) for sc in su] for su in scores]
            for u in range(SB_UNROLL):
                vblk = v_ref[pl.ds(where[u][0], tb), :]
                outs = []
                for hd in range(2):
                    ls, lk, mask = scores[u][hd]
                    att = jnp.where(mask, jnp.exp(ls + laters[u][hd] + crs[hd]), 0.0)
                    outs.append(jnp.dot(att.astype(BF16), vblk, preferred_element_type=F32))
                    crs[hd] = crs[hd] + jnp.sum(lk, axis=1, keepdims=True)
                acc = acc + jnp.where(heads[0], outs[0], outs[1])
            return j + SB_UNROLL, acc, tuple(crs)

        zc = jnp.zeros((tb, 1), F32)
        _, acc, _ = lax.while_loop(functools.partial(_pair_more, qi), step,
                                   (jnp.int32(0), jnp.zeros((tb, SB_PAIR), F32), (zc, zc)))
        o_ref[...] = acc
        ob_ref[...] = acc.astype(BF16)

    qspec, kspec, vspec = _pair_specs(bl, l, d, tb)
    return pl.pallas_call(
        body, name="sb_attn", grid=(bl, d // SB_PAIR, nq), in_specs=[qspec, kspec, vspec], out_specs=[qspec, qspec],
        out_shape=[jax.ShapeDtypeStruct((t, d), F32), jax.ShapeDtypeStruct((t, d), BF16)],
        compiler_params=_params("parallel", "parallel", "parallel"))(qkv, qkv, qkv)


def _sb_attn_bwd2(qkv, o, do, bl):
    t, d3 = qkv.shape
    d, l = d3 // 3, t // bl
    tb = min(SB_BLOCK, l)
    nq = l // tb
    scale = SB_HEAD_DIM ** -0.5

    def body(q_ref, k_ref, v_ref, o_ref, do_ref, dq_ref, dk_ref, dv_ref, dk_acc, dv_acc):
        qi = pl.program_id(2)

        @pl.when(qi == 0)
        def _():
            dk_acc[...] = jnp.zeros_like(dk_acc)
            dv_acc[...] = jnp.zeros_like(dv_acc)

        heads, row, col = _pair_masks(tb)
        qv = q_ref[...]
        dov = do_ref[...].astype(BF16)
        qh = [jnp.where(m, qv, jnp.zeros_like(qv)) for m in heads]
        doh = [jnp.where(m, dov, jnp.zeros_like(dov)) for m in heads]
        ov = o_ref[...]
        dsum = [jnp.sum(dh.astype(F32) * ov, axis=1, keepdims=True) for dh in doh]
        tri = (row > col).astype(BF16)
        tri_inc = (row >= col).astype(BF16)

        def step(carry):
            j, dq, crs, ces = carry
            crs, ces = list(crs), list(ces)
            n = range(SB_UNROLL)
            where = [_sb_block(qi, j + u, tb) for u in n]
            rows = [pl.ds(ks, tb) for ks, _, _ in where]
            kblks = [k_ref[rows[u], :] for u in n]
            vblks = [v_ref[rows[u], :] for u in n]
            scores = [[_sb_scores(qh[hd], kblks[u], where[u][1], where[u][2], row, col) for hd in range(2)] for u in n]
            laters = [[_split_dot(sc[1], tri) for sc in su] for su in scores]
            datts = [[lax.dot_general(doh[hd], vblks[u], NT, preferred_element_type=F32) for hd in range(2)] for u in n]
            atts = [[None, None] for _ in n]
            for u in n:
                for hd in range(2):
                    ls, lk, mask = scores[u][hd]
                    atts[u][hd] = jnp.where(mask, jnp.exp(ls + laters[u][hd] + crs[hd]), 0.0).astype(BF16)
                    crs[hd] = crs[hd] + jnp.sum(lk, axis=1, keepdims=True)
            es = [[atts[u][hd].astype(F32) * datts[u][hd] for hd in range(2)] for u in n]
            sufs = [[_split_dot(e, tri_inc) for e in eu] for eu in es]
            dzs = [[None, None] for _ in n]
            for u in n:
                for hd in range(2):
                    ls, _, mask = scores[u][hd]
                    pre = dsum[hd] - ces[hd] - sufs[u][hd]
                    sg = jnp.exp(ls)
                    dzs[u][hd] = (jnp.where(mask, es[u][hd] * (1.0 - sg) - pre * sg, 0.0) * scale).astype(BF16)
                    ces[hd] = ces[hd] + jnp.sum(es[u][hd], axis=1, keepdims=True)
            for u in n:
                dq = dq + jnp.where(heads[0], jnp.dot(dzs[u][0], kblks[u], preferred_element_type=F32),
                                    jnp.dot(dzs[u][1], kblks[u], preferred_element_type=F32))
                dk_acc[rows[u], :] += (lax.dot_general(dzs[u][0], qh[0], TN, preferred_element_type=F32)
                                       + lax.dot_general(dzs[u][1], qh[1], TN, preferred_element_type=F32))
                dv_acc[rows[u], :] += (lax.dot_general(atts[u][0], doh[0], TN, preferred_element_type=F32)
                                       + lax.dot_general(atts[u][1], doh[1], TN, preferred_element_type=F32))
            return j + SB_UNROLL, dq, tuple(crs), tuple(ces)

        zc = jnp.zeros((tb, 1), F32)
        _, dq, _, _ = lax.while_loop(functools.partial(_pair_more, qi), step,
                                     (jnp.int32(0), jnp.zeros((tb, SB_PAIR), F32), (zc, zc), (zc, zc)))
        dq_ref[...] = dq.astype(BF16)

        @pl.when(qi == nq - 1)
        def _():
            dk_ref[...] = dk_acc[...].astype(BF16)
            dv_ref[...] = dv_acc[...].astype(BF16)

    qspec, kspec, vspec = _pair_specs(bl, l, d, tb)
    blk = pl.BlockSpec((tb, SB_PAIR), lambda b, p, i: (b * nq + i, p))
    full = pl.BlockSpec((l, SB_PAIR), lambda b, p, i: (b, p))
    sd = jax.ShapeDtypeStruct((t, d), BF16)
    dq, dk, dv = pl.pallas_call(
        body, name="sb_attn_bwd", grid=(bl, d // SB_PAIR, nq), in_specs=[qspec, kspec, vspec, blk, blk],
        out_specs=[blk, full, full], out_shape=[sd, sd, sd],
        scratch_shapes=[pltpu.VMEM((l, SB_PAIR), F32), pltpu.VMEM((l, SB_PAIR), F32)],
        compiler_params=_params("parallel", "parallel", "arbitrary"))(qkv, qkv, qkv, o, do)
    return jnp.concatenate([dq, dk, dv], axis=1)


def _sb_fwd(h, g, w_qkv, w_o, bl):
    t, d = h.shape
    hn = _rmsnorm("mix_norm", h, g)
    qkv = _mm_cs("sb_qkv", hn, w_qkv, 0, "flat", BF16)
    o, ob = _sb_attn_fwd2(qkv, bl)
    out = _mm_rs("sb_out", ob, "flat", w_o, 0, res=h)
    return out, (h, hn, qkv, o, ob)


def _sb_bwd(dout, dob, saved, g, w_qkv, w_o, bl):
    h, hn, qkv, o, ob = saved
    dwo = _mm_dw("sb_dwo", ob, "flat", dob, None, (None, 0, 1))
    do = _mm_rs_dx("sb_out_dx", dob, w_o, 0, "flat", F32)
    dqkv = _sb_attn_bwd2(qkv, o, do, bl)
    dwqkv = _mm_dw("sb_dwqkv", hn, None, dqkv, "flat", (None, 0, 1))
    dh, dhb, dg = _mm_cs_dx("sb_qkv_dx", [(dqkv, w_qkv)], "flat", 0, norm=(dout, h, g))
    return dh, dhb, dg, dwqkv, dwo


def _adamw_update(wv, gr, mv, vv):
    c1 = 1.0 / (1.0 - ADAM_B1 ** ADAM_STEP)
    c2 = 1.0 / (1.0 - ADAM_B2 ** ADAM_STEP)
    mn = ADAM_B1 * mv + (1.0 - ADAM_B1) * gr
    vn = ADAM_B2 * vv + (1.0 - ADAM_B2) * gr * gr
    delta = -ADAM_LR * ((mn * c1) / (jnp.sqrt(vn * c2) + ADAM_EPS) + ADAM_WD * wv)
    return delta, mn, vn


def _adamw_small(w, gr, m, v):
    def fn(wv, gv, mv, vv):
        return list(_adamw_update(wv, gv, mv, vv)), []
    return _rows("adamw_small", fn, [w, gr, m, v], [(w.shape[1], F32)] * 3)[0]


def _place():
    x, y, c = lax.axis_index("x"), lax.axis_index("y"), lax.axis_index("c")
    chips = [(1 - x, y), (x, 1 - y), (1 - x, 1 - y)]
    return x, y, c, chips


def _remote(src, dst, send_sem, recv_sem, to):
    return pltpu.make_async_remote_copy(src_ref=src, dst_ref=dst, send_sem=send_sem, recv_sem=recv_sem,
                                        device_id=to, device_id_type=MESH)


def _half(ref, c, rh, lead):
    return ref.at[(slice(None),) * lead + (pl.ds(c * rh, rh),)]


def _allgather_weights(ws):
    n = len(ws)

    def body(*refs):
        ins, outs = refs[:n], refs[n:2 * n]
        send, recv = refs[2 * n:]
        x, y, c, _ = _place()
        chip_x, chip_y, chip_d = (1 - x, y), (x, 1 - y), (1 - x, 1 - y)
        sibling = (x, y, 1 - c)
        index = lambda chip: 2 * chip[0] + chip[1]
        sent = []

        def quarter(ref, half, q, rq):
            return ref.at[:, pl.ds((2 * half + q) * rq, rq)]

        def copy(t, kind, src, dst, to):
            return _remote(src, dst, send.at[t, kind], recv.at[t, kind], to)

        def start(cp):
            cp.start()
            sent.append(cp)

        for t in range(n):
            rq = ws[t].shape[1] // 4
            for q in range(2):
                for base, chip in ((0, chip_x), (2, chip_y)):
                    start(copy(t, base + q, quarter(ins[t], c, q, rq), quarter(outs[t].at[index((x, y))], c, q, rq), (*chip, c)))
        for t in range(n):
            rq = ws[t].shape[1] // 4
            landings = [(chip_x, 0, 0, chip_x, ((4, chip_y), (6, None))), (chip_y, 1, 3, chip_y, ((5, chip_x), (9, None))),
                        (chip_x, 1, 1, chip_x, ((7, None),)), (chip_y, 0, 2, chip_y, ((8, None),)),
                        (chip_d, 0, 4, chip_y, ((10, None),)), (chip_d, 1, 5, chip_x, ((11, None),))]
            for origin, q, kind, sender, onward in landings:
                piece = quarter(outs[t].at[index(origin)], c, q, rq)
                copy(t, kind, piece, piece, (*sender, c)).wait_recv()
                for kind2, chip in onward:
                    start(copy(t, kind2, piece, piece, sibling if chip is None else (*chip, c)))
        for t in range(n):
            rq = ws[t].shape[1] // 4
            for kind, (origin, q) in zip(range(6, 12), ((chip_x, 0), (chip_x, 1), (chip_y, 0), (chip_y, 1), (chip_d, 0), (chip_d, 1))):
                piece = quarter(outs[t].at[index(origin)], 1 - c, q, rq)
                copy(t, kind, piece, piece, sibling).wait_recv()
        for cp in sent:
            cp.wait_send()

    res = pl.pallas_call(
        body, name="allgather_weights", in_specs=[ANY] * n, out_specs=[ANY] * n,
        out_shape=[jax.ShapeDtypeStruct((N_CHIPS,) + w.shape, w.dtype) for w in ws],
        scratch_shapes=[pltpu.SemaphoreType.DMA((n, 12)), pltpu.SemaphoreType.DMA((n, 12))],
    )(*ws)
    own = 2 * lax.axis_index("x") + lax.axis_index("y")
    return [lax.dynamic_update_slice(g, w[None], (own, 0, 0, 0)) for g, w in zip(res, ws)]


def _pair_exchange(gs):
    n = len(gs)

    def body(*refs):
        ins, outs = refs[:n], refs[n:2 * n]
        send, recv = refs[2 * n:]
        x, y, c, _ = _place()
        copies = [_remote(_half(ins[t], 1 - c, gs[t].shape[2] // 2, 2), outs[t], send.at[t], recv.at[t], (x, y, 1 - c))
                  for t in range(n)]
        for cp in copies:
            cp.start()
        for cp in copies:
            cp.wait()

    return pl.pallas_call(
        body, name="grad_pair_exchange", in_specs=[ANY] * n, out_specs=[ANY] * n,
        out_shape=[jax.ShapeDtypeStruct(g.shape[:2] + (g.shape[2] // 2, g.shape[3]), F32) for g in gs],
        scratch_shapes=[pltpu.SemaphoreType.DMA((n,)), pltpu.SemaphoreType.DMA((n,))],
    )(*gs)


def _pair_sum(g, theirs, c_idx):
    n4, ly, r, cc = g.shape
    rh = r // 2
    tm = _tile(rh, 512)
    nt = rh // tm

    def body(c_ref, g_ref, t_ref, o_ref):
        o_ref[...] = (g_ref[...] + t_ref[...]).astype(o_ref.dtype)

    blk = (None, tm, cc)
    grid_spec = pltpu.PrefetchScalarGridSpec(
        num_scalar_prefetch=1, grid=(n4 * ly, nt),
        in_specs=[pl.BlockSpec(blk, lambda a, i, cr: (a, cr[0] * nt + i, 0)), pl.BlockSpec(blk, lambda a, i, cr: (a, i, 0))],
        out_specs=pl.BlockSpec(blk, lambda a, i, cr: (a, i, 0)))
    out = pl.pallas_call(
        body, name="grad_pair_sum", grid_spec=grid_spec, out_shape=jax.ShapeDtypeStruct((n4 * ly, rh, cc), BF16),
        compiler_params=_params("parallel", "parallel"))(c_idx, g.reshape(n4 * ly, r, cc), theirs.reshape(n4 * ly, rh, cc))
    return out.reshape(n4, ly, rh, cc)


def _quarter(ref, q, rq):
    return ref.at[:, pl.ds(q * rq, rq)]


def _chip_exchange_first(ps):
    n = len(ps)

    def body(*refs):
        ins, outs = refs[:n], refs[n:2 * n]
        send, recv = refs[2 * n:]
        x, y, c, _ = _place()
        index = lambda cx, cy: 2 * cx + cy
        copies = []
        for t in range(n):
            rq = ps[t].shape[2] // 2
            for base, q, chip in ((0, 0, (1 - x, y)), (2, 1, (x, 1 - y))):
                for j, slice_of in enumerate((chip, (1 - x, 1 - y))):
                    copies.append(_remote(_quarter(ins[t].at[index(*slice_of)], q, rq), outs[t].at[base + j],
                                          send.at[t, base + j], recv.at[t, base + j], (*chip, c)))
        for cp in copies:
            cp.start()
        for cp in copies:
            cp.wait()

    return pl.pallas_call(
        body, name="grad_chip_exchange", in_specs=[ANY] * n, out_specs=[ANY] * n,
        out_shape=[jax.ShapeDtypeStruct((4, p.shape[1], p.shape[2] // 2, p.shape[3]), p.dtype) for p in ps],
        scratch_shapes=[pltpu.SemaphoreType.DMA((n, 4)), pltpu.SemaphoreType.DMA((n, 4))],
    )(*ps)


def _chip_relay_sum(p, first, where):
    _, ly, rh, cc = p.shape
    rq = rh // 2
    tm = _tile(rq, 512)
    nt = rq // tm

    def body(w_ref, p_ref, f_ref, out_ref):
        out_ref[...] = (p_ref[...].astype(F32) + f_ref[...].astype(F32)).astype(out_ref.dtype)

    blk = (None, None, tm, cc)
    grid_spec = pltpu.PrefetchScalarGridSpec(
        num_scalar_prefetch=1, grid=(2, ly, nt),
        in_specs=[pl.BlockSpec(blk, lambda s, l, i, w: (w[2 - s], l, s * nt + i, 0)),
                  pl.BlockSpec(blk, lambda s, l, i, w: (1 + 2 * s, l, i, 0))],
        out_specs=pl.BlockSpec(blk, lambda s, l, i, w: (s, l, i, 0)))
    return pl.pallas_call(
        body, name="grad_relay_sum", grid_spec=grid_spec, out_shape=jax.ShapeDtypeStruct((2, ly, rq, cc), p.dtype),
        compiler_params=_params("parallel", "parallel", "parallel"))(where, p, first)


def _chip_exchange_second(ss):
    n = len(ss)

    def body(*refs):
        ins, outs = refs[:n], refs[n:2 * n]
        send, recv = refs[2 * n:]
        x, y, c, _ = _place()
        copies = []
        for t in range(n):
            for j, chip in enumerate(((x, 1 - y), (1 - x, y))):
                copies.append(_remote(ins[t].at[j], outs[t].at[j], send.at[t, j], recv.at[t, j], (*chip, c)))
        for cp in copies:
            cp.start()
        for cp in copies:
            cp.wait()

    return pl.pallas_call(
        body, name="grad_chip_exchange_2", in_specs=[ANY] * n, out_specs=[ANY] * n,
        out_shape=[jax.ShapeDtypeStruct(s.shape, s.dtype) for s in ss],
        scratch_shapes=[pltpu.SemaphoreType.DMA((n, 2)), pltpu.SemaphoreType.DMA((n, 2))],
    )(*ss)


def _chip_sum(p, first, second, where):
    _, ly, rh, cc = p.shape
    rq = rh // 2
    tm = _tile(rq, 512)
    nt = rq // tm

    def body(w_ref, p_ref, f_ref, s_ref, out_ref):
        out_ref[...] = (p_ref[...].astype(F32) + f_ref[...].astype(F32)) + s_ref[...].astype(F32)

    blk = (None, None, tm, cc)
    grid_spec = pltpu.PrefetchScalarGridSpec(
        num_scalar_prefetch=1, grid=(ly, 2, nt),
        in_specs=[pl.BlockSpec(blk, lambda l, q, i, w: (w[0], l, q * nt + i, 0)),
                  pl.BlockSpec(blk, lambda l, q, i, w: (2 * q, l, i, 0)),
                  pl.BlockSpec(blk, lambda l, q, i, w: (q, l, i, 0))],
        out_specs=pl.BlockSpec((None, tm, cc), lambda l, q, i, w: (l, q * nt + i, 0)))
    return pl.pallas_call(
        body, name="grad_chip_sum", grid_spec=grid_spec, out_shape=jax.ShapeDtypeStruct((ly, rh, cc), F32),
        compiler_params=_params("parallel", "parallel", "parallel"))(where, p, first, second)


def _pair_swap(halves):
    n = len(halves)

    def body(*refs):
        ins, outs = refs[:n], refs[n:2 * n]
        send, recv = refs[2 * n:]
        x, y, c, _ = _place()
        copies = [_remote(ins[t], outs[t], send.at[t], recv.at[t], (x, y, 1 - c)) for t in range(n)]
        for cp in copies:
            cp.start()
        for cp in copies:
            cp.wait()

    return pl.pallas_call(
        body, name="grad_pair_swap", in_specs=[ANY] * n, out_specs=[ANY] * n,
        out_shape=[jax.ShapeDtypeStruct(h.shape, F32) for h in halves],
        scratch_shapes=[pltpu.SemaphoreType.DMA((n,)), pltpu.SemaphoreType.DMA((n,))],
    )(*halves)


def _adamw_big(w, m, v, mine, theirs, c_idx):
    ly, r, cc = w.shape
    rh = r // 2
    tm = _tile(rh, 512)
    nt = rh // tm

    def body(c_ref, w_ref, m_ref, v_ref, a_ref, b_ref, g_out, d_out, m_out, v_out):
        gr = jnp.where(pl.program_id(1) == c_ref[0], a_ref[...], b_ref[...])
        delta, mn, vn = _adamw_update(w_ref[...], gr, m_ref[...], v_ref[...])
        g_out[...] = gr
        d_out[...] = delta
        m_out[...] = mn
        v_out[...] = vn

    blk = (None, tm, cc)
    full = pl.BlockSpec(blk, lambda l, hc, i, cr: (l, hc * nt + i, 0))
    half = pl.BlockSpec(blk, lambda l, hc, i, cr: (l, i, 0))
    grid_spec = pltpu.PrefetchScalarGridSpec(
        num_scalar_prefetch=1, grid=(ly, 2, nt), in_specs=[full, full, full, half, half], out_specs=[full] * 4)
    sd = jax.ShapeDtypeStruct(w.shape, F32)
    return pl.pallas_call(
        body, name="adamw", grid_spec=grid_spec, out_shape=[sd] * 4,
        compiler_params=_params("parallel", "parallel", "parallel"))(c_idx, w, m, v, mine, theirs)


def _allreduce_small(v):
    rows, w = v.shape

    def body(x_ref, sum_ref, all_ref, send, recv, local):
        x, y, c, chips = _place()
        me, sibling = (x, y, c), (x, y, 1 - c)

        def slot(px, py, pc):
            return all_ref.at[4 * px + 2 * py + pc]

        def copy(k, block, to, src=None):
            return _remote(slot(*block) if src is None else src, slot(*block), send.at[k], recv.at[k], to)

        mine = pltpu.make_async_copy(x_ref, slot(*me), local)
        mine.start()
        first = [copy(0, me, sibling, src=x_ref)]
        first += [copy(1 + j, me, (*chip, c), src=x_ref) for j, chip in enumerate(chips)]
        for cp in first:
            cp.start()
        passed = [copy(4 + j, (*chip, c), sibling) for j, chip in enumerate(chips)]
        for j, chip in enumerate(chips):
            copy(1 + j, (*chip, c), me).wait_recv()
            passed[j].start()
        copy(0, sibling, me).wait_recv()
        for j, chip in enumerate(chips):
            copy(4 + j, (*chip, 1 - c), me).wait_recv()
        for cp in first + passed:
            cp.wait_send()
        mine.wait()
        tot = all_ref[0]
        for k in range(1, N_DEV):
            tot = tot + all_ref[k]
        sum_ref[...] = tot

    vm = pl.BlockSpec(memory_space=pltpu.VMEM)
    return pl.pallas_call(
        body, name="allreduce_small", in_specs=[vm], out_specs=[vm, vm],
        out_shape=[jax.ShapeDtypeStruct((rows, w), F32), jax.ShapeDtypeStruct((N_DEV, rows, w), F32)],
        scratch_shapes=[pltpu.SemaphoreType.DMA((7,)), pltpu.SemaphoreType.DMA((7,)), pltpu.SemaphoreType.DMA],
        compiler_params=pltpu.CompilerParams(vmem_limit_bytes=VMEM_LIMIT),
    )(v)[0]


BIG = ["ffn1_w1", "ffn1_w3", "ffn1_w2", "ffn2_w1", "ffn2_w3", "ffn2_w2", "ple_proj", "ple_gate",
       "s5_w_in", "s5_w_glu", "sb_w_qkv", "sb_w_o"]
TRANSPOSED = ("ffn1_w1", "ffn1_w3", "ffn2_w1", "ffn2_w3")
SMALL = ["ffn1_norm", "mix_norm", "ffn2_norm", "ple_norm", "s5_a_re", "s5_a_im", "s5_log_dt", "s5_b_re", "s5_b_im",
         "s5_c_re", "s5_c_im", "s5_d", "final_norm"]
ORDER = ["ffn1_norm", "ffn1_w1", "ffn1_w3", "ffn1_w2", "mix_norm", "ffn2_norm", "ffn2_w1", "ffn2_w3", "ffn2_w2",
         "ple_norm", "ple_proj", "ple_gate", "s5_w_in", "s5_a_re", "s5_a_im", "s5_log_dt", "s5_b_re", "s5_b_im",
         "s5_c_re", "s5_c_im", "s5_d", "s5_w_glu", "sb_w_qkv", "sb_w_o", "final_norm"]


def _pack(arrays):
    flat = jnp.concatenate([a.reshape(-1) for a in arrays])
    pad = (-flat.shape[0]) % 1024
    return jnp.pad(flat, (0, pad)).reshape(-1, 128)


def _unpack(packed, like):
    flat = packed.reshape(-1)
    out, off = [], 0
    for a in like:
        out.append(flat[off:off + a.size].reshape(a.shape))
        off += a.size
    return out


def _fwd_bwd(x, p, target, w, gathered):
    bl, l, d = x.shape
    t = bl * l
    depth = w["ffn1_norm"].shape[0]
    s5_ops, s5_vjp = jax.vjp(_s5_prep, w["s5_a_re"][0], w["s5_a_im"][0], w["s5_log_dt"][0], w["s5_b_re"][0],
                             w["s5_b_im"][0], w["s5_c_re"][0], w["s5_c_im"][0])

    h = x.reshape(t, d)
    p2 = [p[i].reshape(t, p.shape[-1]).astype(BF16) for i in range(depth)]
    saved = []
    for i in range(depth):
        norm = lambda name: w[name][i:i + 1]
        h, s1 = _ffn_fwd(h, norm("ffn1_norm"), gathered["ffn1_w1"], gathered["ffn1_w3"], gathered["ffn1_w2"], i)
        if i % 2 == 0:
            h, s2 = _s5_fwd(h, norm("mix_norm"), s5_ops, w["s5_d"][i // 2:i // 2 + 1], gathered["s5_w_in"], gathered["s5_w_glu"], bl)
        else:
            h, s2 = _sb_fwd(h, norm("mix_norm"), gathered["sb_w_qkv"], gathered["sb_w_o"], bl)
        h, s3 = _ffn_fwd(h, norm("ffn2_norm"), gathered["ffn2_w1"], gathered["ffn2_w3"], gathered["ffn2_w2"], i)
        h, s4 = _ple_fwd(h, norm("ple_norm"), p2[i], gathered["ple_proj"], gathered["ple_gate"], i)
        saved.append((s1, s2, s3, s4))

    loss, dh, dfinal = _head(h, w["final_norm"].reshape(1, d), target.reshape(t, d))

    big = {k: None for k in BIG}
    small = {k: [None] * w[k].shape[0] if w[k].ndim > 1 else None for k in SMALL}
    small["final_norm"] = dfinal.reshape(d)
    for i in reversed(range(depth)):
        norm = lambda name: w[name][i:i + 1]
        slots = lambda *names: [(big[k], i, depth) for k in names]
        s1, s2, s3, s4 = saved[i]
        dh, dhb, dg, big["ple_proj"], big["ple_gate"] = _ple_bwd(
            dh, s4, norm("ple_norm"), p2[i], gathered["ple_proj"], gathered["ple_gate"], i, slots("ple_proj", "ple_gate"))
        small["ple_norm"][i] = dg[0]
        dh, dhb, dg, big["ffn2_w1"], big["ffn2_w3"], big["ffn2_w2"] = _ffn_bwd(
            dh, dhb, s3, norm("ffn2_norm"), gathered["ffn2_w1"], gathered["ffn2_w3"], gathered["ffn2_w2"], i,
            slots("ffn2_w1", "ffn2_w3", "ffn2_w2"))
        small["ffn2_norm"][i] = dg[0]
        if i % 2 == 0:
            dh, dhb, dg, big["s5_w_in"], big["s5_w_glu"], dd, dops = _s5_bwd(
                dh, s2, norm("mix_norm"), s5_ops, w["s5_d"][i // 2:i // 2 + 1], gathered["s5_w_in"], gathered["s5_w_glu"], bl)
            small["s5_d"][0] = dd[0]
            raw = s5_vjp(dops)
            for name, gr in zip(["s5_a_re", "s5_a_im", "s5_log_dt", "s5_b_re", "s5_b_im", "s5_c_re", "s5_c_im"], raw):
                small[name][0] = gr
        else:
            dh, dhb, dg, big["sb_w_qkv"], big["sb_w_o"] = _sb_bwd(
                dh, dhb, s2, norm("mix_norm"), gathered["sb_w_qkv"], gathered["sb_w_o"], bl)
        small["mix_norm"][i] = dg[0]
        dh, dhb, dg, big["ffn1_w1"], big["ffn1_w3"], big["ffn1_w2"] = _ffn_bwd(
            dh, dhb, s1, norm("ffn1_norm"), gathered["ffn1_w1"], gathered["ffn1_w3"], gathered["ffn1_w2"], i,
            slots("ffn1_w1", "ffn1_w3", "ffn1_w2"))
        small["ffn1_norm"][i] = dg[0]
    small_list = [jnp.stack(small[k]) if isinstance(small[k], list) else small[k] for k in SMALL]
    return loss, dh.reshape(bl, l, d), big, small_list


def _step(x, p, target, w, m, v):
    flip = lambda tree: {k: jnp.swapaxes(a, 1, 2) if k in TRANSPOSED else a for k, a in tree.items()}
    w, m, v = flip(w), flip(m), flip(v)
    gathered = dict(zip(BIG, _allgather_weights([_to_bf16(w[k]) for k in BIG])))
    loss, grad_x, big, small_list = _fwd_bwd(x, p, target, w, gathered)

    c_idx = lax.axis_index("c").astype(jnp.int32).reshape(1)
    cx, cy = lax.axis_index("x"), lax.axis_index("y")
    where = jnp.stack([2 * cx + cy, 2 * (1 - cx) + cy, 2 * cx + (1 - cy)]).astype(jnp.int32)
    partial = [big[k] for k in BIG]
    pair = [_pair_sum(g, t, c_idx) for g, t in zip(partial, _pair_exchange(partial))]
    first = _chip_exchange_first(pair)
    second = _chip_exchange_second([_chip_relay_sum(pr, f, where) for pr, f in zip(pair, first)])
    mine = [_chip_sum(pr, f, s, where) for pr, f, s in zip(pair, first, second)]
    theirs = _pair_swap(mine)
    out_g, out_d, out_m, out_v = {}, {}, {}, {}
    for k, a, b in zip(BIG, mine, theirs):
        out_g[k], out_d[k], out_m[k], out_v[k] = _adamw_big(w[k], m[k], v[k], a, b, c_idx)

    like = [w[k] for k in SMALL]
    pad = [jnp.zeros((1,), F32)]
    g_small = _allreduce_small(_pack(small_list + [loss.reshape(1)]))
    packed = (g_small,) + tuple(_adamw_small(_pack(like + pad), g_small, _pack([m[k] for k in SMALL] + pad),
                                             _pack([v[k] for k in SMALL] + pad)))
    for dst, pk in zip((out_g, out_d, out_m, out_v), packed):
        dst.update(dict(zip(SMALL, _unpack(pk, like))))
    loss = g_small.reshape(-1)[sum(a.size for a in like)]
    out_g, out_d, out_m, out_v = flip(out_g), flip(out_d), flip(out_m), flip(out_v)
    return (loss, grad_x, *[out_g[k] for k in ORDER], *[out_d[k] for k in ORDER],
            *[out_m[k] for k in ORDER], *[out_v[k] for k in ORDER])


def kernel(x, p, ffn1_norm, ffn1_w1, ffn1_w3, ffn1_w2, mix_norm, ffn2_norm, ffn2_w1, ffn2_w3, ffn2_w2, ple_norm, ple_proj, ple_gate, s5_w_in, s5_a_re, s5_a_im, s5_log_dt, s5_b_re, s5_b_im, s5_c_re, s5_c_im, s5_d, s5_w_glu, sb_w_qkv, sb_w_o, final_norm, loss_target, m_ffn1_norm, m_ffn1_w1, m_ffn1_w3, m_ffn1_w2, m_mix_norm, m_ffn2_norm, m_ffn2_w1, m_ffn2_w3, m_ffn2_w2, m_ple_norm, m_ple_proj, m_ple_gate, m_s5_w_in, m_s5_a_re, m_s5_a_im, m_s5_log_dt, m_s5_b_re, m_s5_b_im, m_s5_c_re, m_s5_c_im, m_s5_d, m_s5_w_glu, m_sb_w_qkv, m_sb_w_o, m_final_norm, v_ffn1_norm, v_ffn1_w1, v_ffn1_w3, v_ffn1_w2, v_mix_norm, v_ffn2_norm, v_ffn2_w1, v_ffn2_w3, v_ffn2_w2, v_ple_norm, v_ple_proj, v_ple_gate, v_s5_w_in, v_s5_a_re, v_s5_a_im, v_s5_log_dt, v_s5_b_re, v_s5_b_im, v_s5_c_re, v_s5_c_im, v_s5_d, v_s5_w_glu, v_sb_w_qkv, v_sb_w_o, v_final_norm):
    args = dict(locals())
    w = {k: args[k] for k in ORDER}
    m = {k: args["m_" + k] for k in ORDER}
    v = {k: args["v_" + k] for k in ORDER}
    return _step(x, p, loss_target, w, m, v)
```

```python
import functools
import math

import jax
import jax.numpy as jnp
from jax import lax
from jax.experimental import pallas as pl
from jax.experimental.pallas import tpu as pltpu

F32 = jnp.float32
BF16 = jnp.bfloat16
MESH = pl.DeviceIdType.MESH

N_CHIPS = 4
N_DEV = 8
RMS_EPS = 1e-6
S5_GROUP = 16
S5_STATE = 64
S5_CHUNK = 16
SB_HEAD_DIM = 64
SB_BLOCK = 128
SB_CUT = -104.0
SB_UNROLL = 3
ADAM_LR, ADAM_B1, ADAM_B2, ADAM_EPS, ADAM_WD, ADAM_STEP = 0.001, 0.9, 0.999, 1e-08, 0.01, 10
VMEM_LIMIT = 56 * 1024 * 1024

NN = (((1,), (0,)), ((), ()))
NT = (((1,), (1,)), ((), ()))
TN = (((0,), (0,)), ((), ()))

ANY = pl.BlockSpec(memory_space=pl.ANY)


def _tile(n, target):
    if n <= target:
        return n
    for t in range(target - target % 8, 7, -8):
        if n % t == 0:
            return t
    raise ValueError(f"no row tile for {n}")


def _params(*semantics):
    return pltpu.CompilerParams(dimension_semantics=semantics, vmem_limit_bytes=VMEM_LIMIT)


def _sigmoid(v):
    return 1.0 / (1.0 + jnp.exp(-v))


def _gemm(name, grid, operands, in_specs, groups, acc_shapes, out_shapes, out_specs, epilogue, reduce_axis=None, aliases=None):
    n_in, n_out = len(operands), len(out_shapes)
    n_red = None if reduce_axis is None else grid[reduce_axis]

    def body(*refs):
        ins, outs, accs = refs[:n_in], refs[n_in:n_in + n_out], refs[n_in + n_out:]

        def products():
            res = []
            for terms in groups:
                tot = None
                for ia, ib, dims in terms:
                    d = lax.dot_general(ins[ia][...], ins[ib][...], dims, preferred_element_type=F32)
                    tot = d if tot is None else tot + d
                res.append(tot)
            return res

        def finish(vals):
            for o, v in zip(outs, epilogue(vals, ins)):
                o[...] = v.astype(o.dtype)

        if reduce_axis is None:
            finish(products())
        else:
            k = pl.program_id(reduce_axis)

            @pl.when(k == 0)
            def _():
                for a in accs:
                    a[...] = jnp.zeros_like(a)

            for a, d in zip(accs, products()):
                a[...] += d

            @pl.when(k == n_red - 1)
            def _():
                finish([a[...] for a in accs])

    scratch = [] if reduce_axis is None else [pltpu.VMEM(s, F32) for s in acc_shapes]
    sem = tuple("arbitrary" if i == reduce_axis else "parallel" for i in range(len(grid)))
    return pl.pallas_call(
        body, name=name, grid=grid, in_specs=in_specs, out_specs=out_specs, out_shape=out_shapes,
        scratch_shapes=scratch, input_output_aliases=aliases or {}, compiler_params=_params(*sem))(*operands)


def _ident(vals, ins):
    return vals


def _act_spec(layout, tm, cs, pos):
    if layout == "sm":
        return pl.BlockSpec((None, tm, cs), lambda *g: (pos(*g)[1], pos(*g)[0], 0))
    return pl.BlockSpec((tm, cs), lambda *g: pos(*g))


def _act_shape(layout, t, cs, dtype):
    return jax.ShapeDtypeStruct((N_CHIPS, t, cs) if layout == "sm" else (t, N_CHIPS * cs), dtype)


def _w_spec(w, layer, pos_k):
    _, _, r, c = w.shape
    return pl.BlockSpec((None, None, r, c), lambda *g: (pos_k(*g), layer, 0, 0))


def _mm_cs(name, x, w, layer, out_layout, out_dtype, tm=1024):
    t, kd = x.shape
    cs = w.shape[3]
    tm = _tile(t, tm)
    return _gemm(
        name, (N_CHIPS, t // tm), [x, w],
        [pl.BlockSpec((tm, kd), lambda k, i: (i, 0)), _w_spec(w, layer, lambda k, i: k)],
        [[(0, 1, NN)]], None, [_act_shape(out_layout, t, cs, out_dtype)],
        [_act_spec(out_layout, tm, cs, lambda k, i: (i, k))], _ident)[0]


def _mm_rs(name, xs, layout, w, layer, res=None, alpha=1.0, out_dtype=F32, tm=1024, gated=None):
    ks, n = w.shape[2], w.shape[3]
    t = xs.shape[1] if layout == "sm" else xs.shape[0]
    tm = _tile(t, tm)
    row = pl.BlockSpec((tm, n), lambda i: (i, 0))
    extras = [a for a in (res, gated) if a is not None]
    shards = range(N_CHIPS)
    operands = [xs] * N_CHIPS + [w] * N_CHIPS + extras
    specs = ([_act_spec(layout, tm, ks, lambda i, k=k: (i, k)) for k in shards]
             + [_w_spec(w, layer, lambda i, k=k: k) for k in shards] + [row] * len(extras))
    base = 2 * N_CHIPS

    def epilogue(vals, ins):
        y = alpha * vals[0]
        if gated is not None:
            return [y, ins[base][...] + ins[base + 1][...] * _sigmoid(y)]
        return [y if res is None else ins[base][...] + y]

    outs = _gemm(
        name, (t // tm,), operands, specs, [[(k, N_CHIPS + k, NN) for k in shards]], None,
        [jax.ShapeDtypeStruct((t, n), out_dtype)] * (1 if gated is None else 2), [row] * (1 if gated is None else 2),
        epilogue)
    return outs[0] if gated is None else outs


def _mm_cs_dx(name, pairs, layout, layer, tm=1024, transposed=False, norm=None):
    w0 = pairs[0][1]
    kd, cs = (w0.shape[3], w0.shape[2]) if transposed else (w0.shape[2], w0.shape[3])
    dy0 = pairs[0][0]
    t = dy0.shape[1] if layout == "sm" else dy0.shape[0]
    tm = _tile(t, tm // 2)
    operands, specs, terms = [], [], []
    for dy, w in pairs:
        for k in range(N_CHIPS):
            terms.append((len(operands), len(operands) + 1, NN if transposed else NT))
            operands += [dy, w]
            specs += [_act_spec(layout, tm, cs, lambda i, k=k: (i, k)), _w_spec(w, layer, lambda i, k=k: k)]
    row = pl.BlockSpec((tm, kd), lambda i: (i, 0))
    if norm is None:
        return _gemm(name, (t // tm,), operands, specs, [terms], None,
                     [jax.ShapeDtypeStruct((t, kd), F32)], [row], _ident)[0]
    base = len(operands)
    operands += list(norm)
    specs += [row, row, pl.BlockSpec(norm[2].shape, lambda i: (0, 0))]

    def epilogue(vals, ins):
        dx, dg = _rms_bwd_math(vals[0], ins[base + 1][...], ins[base + 2][...])
        dh = ins[base][...] + dx
        return [dh, dh, dg]

    dh, dhb, dg = _gemm(
        name, (t // tm,), operands, specs, [terms], None,
        [jax.ShapeDtypeStruct((t, kd), F32), jax.ShapeDtypeStruct((t, kd), BF16), jax.ShapeDtypeStruct((t // tm, 1, kd), F32)],
        [row, row, pl.BlockSpec((None, 1, kd), lambda i: (i, 0, 0))], epilogue)
    return dh, dhb, dg.sum(axis=0)


def _mm_rs_dx(name, dy, w, layer, out_layout, out_dtype, tm=1024):
    t, n = dy.shape
    ks = w.shape[2]
    tm = _tile(t, tm)
    return _gemm(
        name, (N_CHIPS, t // tm), [dy, w],
        [pl.BlockSpec((tm, n), lambda k, i: (i, 0)), _w_spec(w, layer, lambda k, i: k)],
        [[(0, 1, NT)]], None, [_act_shape(out_layout, t, ks, out_dtype)],
        [_act_spec(out_layout, tm, ks, lambda k, i: (i, k))], _ident)[0]


def _mm_dw(name, x, x_layout, dy, dy_layout, slot, alpha=1.0, tk=4096):
    stack, layer, layers = slot
    if x_layout is None:
        t, rows = x.shape
        cols = dy.shape[2] if dy_layout == "sm" else dy.shape[1] // N_CHIPS
        tk = _tile(t, tk)
        xspec = pl.BlockSpec((tk, rows), lambda k, j: (j, 0))
        yspec = _act_spec(dy_layout, tk, cols, lambda k, j: (j, k))
    else:
        t, cols = dy.shape
        rows = x.shape[2] if x_layout == "sm" else x.shape[1] // N_CHIPS
        tk = _tile(t, tk)
        xspec = _act_spec(x_layout, tk, rows, lambda k, j: (j, k))
        yspec = pl.BlockSpec((tk, cols), lambda k, j: (j, 0))
    operands, specs = [x, dy], [xspec, yspec]
    if stack is not None:
        operands.append(stack)
        specs.append(ANY)
    return _gemm(
        name, (N_CHIPS, t // tk), operands, specs, [[(0, 1, TN)]], [(rows, cols)],
        [jax.ShapeDtypeStruct((N_CHIPS, layers, rows, cols), F32)],
        [pl.BlockSpec((None, None, rows, cols), lambda k, j: (k, layer, 0, 0))],
        lambda vals, ins: [alpha * vals[0]], reduce_axis=1 if t // tk > 1 else None,
        aliases=None if stack is None else {2: 0})[0]


def _rows(name, fn, ins, outs, accs=(), tm=1024):
    t = ins[0].shape[0]
    tm = _tile(t, tm)
    n_in, n_out, n_acc = len(ins), len(outs), len(accs)
    in_specs = []
    for a in ins:
        if a.shape[0] == t:
            in_specs.append(pl.BlockSpec((tm, a.shape[1]), lambda i: (i, 0)))
        else:
            in_specs.append(pl.BlockSpec(a.shape, lambda i: (0, 0)))
    out_shape = [jax.ShapeDtypeStruct((t, c), d) for c, d in outs] + [jax.ShapeDtypeStruct(s, F32) for s in accs]
    out_specs = [pl.BlockSpec((tm, c), lambda i: (i, 0)) for c, _ in outs] + [pl.BlockSpec(s, lambda i: (0, 0)) for s in accs]

    def body(*refs):
        i = pl.program_id(0)
        row_vals, acc_vals = fn(*[r[...] for r in refs[:n_in]])
        for o, v in zip(refs[n_in:n_in + n_out], row_vals):
            o[...] = v.astype(o.dtype)
        acc_refs = refs[n_in + n_out:]
        if n_acc:
            @pl.when(i == 0)
            def _():
                for a in acc_refs:
                    a[...] = jnp.zeros_like(a)

            for a, v in zip(acc_refs, acc_vals):
                a[...] += v

    res = pl.pallas_call(
        body, name=name, grid=(t // tm,), in_specs=in_specs, out_specs=out_specs, out_shape=out_shape,
        compiler_params=_params("arbitrary" if n_acc else "parallel"))(*ins)
    return res[:n_out], res[n_out:]


def _to_bf16(a):
    def fn(x):
        return [x], []
    return _rows("weights_bf16", fn, [a.reshape(-1, a.shape[-1])], [(a.shape[-1], BF16)], tm=512)[0][0].reshape(a.shape)


def _rms_stats(x):
    return lax.rsqrt(jnp.mean(x * x, axis=-1, keepdims=True) + RMS_EPS)


def _rmsnorm(name, h, g):
    def fn(x, gg):
        return [x * _rms_stats(x) * gg], []
    return _rows(name, fn, [h, g], [(h.shape[1], BF16)])[0][0]


def _rms_bwd_math(dn, x, g):
    r = _rms_stats(x)
    xhat = x * r
    dxh = dn * g
    dx = r * (dxh - xhat * jnp.mean(dxh * xhat, axis=-1, keepdims=True))
    return dx, jnp.sum(dn * xhat, axis=0, keepdims=True)


def _rmsnorm_bwd(name, dres, dn, h, g):
    def fn(dr, d, x, gg):
        dx, dg = _rms_bwd_math(d, x, gg)
        return [dr + dx, dr + dx], [dg]
    (dh, dhb), (dg,) = _rows(name, fn, [dres, dn, h, g], [(h.shape[1], F32), (h.shape[1], BF16)], [(1, h.shape[1])])
    return dh, dhb, dg


def _ffn_fwd(h, g, w1, w3, w2, layer, tm=1024):
    t, d = h.shape
    fs = w1.shape[2]
    n = _rmsnorm("ffn_norm", h, g)
    tm = _tile(t, tm)

    def up(vals, ins):
        a, b = vals
        sg = _sigmoid(a)
        silu = a * sg
        return [b * sg * (1.0 + a * (1.0 - sg)), silu, silu * b]

    sm = _act_shape("sm", t, fs, BF16)
    osp = _act_spec("sm", tm, fs, lambda k, i: (i, k))
    ga, gb, s = _gemm(
        "ffn_up", (N_CHIPS, t // tm), [n, w1, w3],
        [pl.BlockSpec((tm, d), lambda k, i: (i, 0)), _w_spec(w1, layer, lambda k, i: k), _w_spec(w3, layer, lambda k, i: k)],
        [[(0, 1, NT)], [(0, 2, NT)]], None, [sm, sm, sm], [osp, osp, osp], up)
    out = _mm_rs("ffn_down", s, "sm", w2, layer, res=h, alpha=0.5)
    return out, (h, n, ga, gb, s)


def _ffn_bwd(dout, dob, saved, g, w1, w3, w2, layer, slots, tm=1024):
    h, n, ga, gb, s = saved
    t, d = h.shape
    fs = w1.shape[2]
    tm = _tile(t, tm)

    def down(vals, ins):
        ds = 0.5 * vals[0]
        return [ds * ins[2][...].astype(F32), ds * ins[3][...].astype(F32)]

    sm = _act_shape("sm", t, fs, BF16)
    asp = _act_spec("sm", tm, fs, lambda k, i: (i, k))
    da, db = _gemm(
        "ffn_down_dx", (N_CHIPS, t // tm), [dob, w2, ga, gb],
        [pl.BlockSpec((tm, d), lambda k, i: (i, 0)), _w_spec(w2, layer, lambda k, i: k), asp, asp],
        [[(0, 1, NT)]], None, [sm, sm], [asp, asp], down)
    dw2 = _mm_dw("ffn_dw2", s, "sm", dob, None, slots[2], alpha=0.5)
    dw1 = _mm_dw("ffn_dw1", da, "sm", n, None, slots[0])
    dw3 = _mm_dw("ffn_dw3", db, "sm", n, None, slots[1])
    dh, dhb, dg = _mm_cs_dx("ffn_up_dx", [(da, w1), (db, w3)], "sm", layer, transposed=True, norm=(dout, h, g))
    return dh, dhb, dg, dw1, dw3, dw2


def _ple_fwd(h, g, p2, wproj, wgate, layer):
    n = _rmsnorm("ple_norm", h, g)
    pp = _mm_cs("ple_proj", p2, wproj, layer, "flat", F32)
    gl, out = _mm_rs("ple_gate", n, "flat", wgate, layer, res=h, gated=pp)
    return out, (h, n, gl, pp)


def _ple_bwd(dout, saved, g, p2, wproj, wgate, layer, slots):
    h, n, gl, pp = saved
    d = h.shape[1]

    def fn(do, gg, q):
        sg = _sigmoid(gg)
        return [do * sg, do * q * sg * (1.0 - sg)], []
    (dpp, dgl), _ = _rows("ple_mix_bwd", fn, [dout, gl, pp], [(d, BF16), (d, BF16)])
    dwproj = _mm_dw("ple_dwproj", p2, None, dpp, "flat", slots[0])
    dwgate = _mm_dw("ple_dwgate", n, "flat", dgl, None, slots[1])
    dn = _mm_rs_dx("ple_gate_dx", dgl, wgate, layer, "flat", F32)
    dh, dhb, dg = _rmsnorm_bwd("ple_norm_bwd", dout, dn, h, g)
    return dh, dhb, dg, dwproj, dwgate


def _head(h, g, target):
    d = h.shape[1]

    def fn(x, gg, tg):
        y = x * _rms_stats(x) * gg
        err = y - tg
        dy = err * (1.0 / d)
        dx, dg = _rms_bwd_math(dy, x, gg)
        loss = 0.5 * jnp.sum(jnp.sum(err * err, axis=-1, keepdims=True) * (1.0 / d), axis=0, keepdims=True)
        return [dx], [dg, jnp.broadcast_to(loss, (1, 128))]
    (dh,), (dg, loss) = _rows("loss_head", fn, [h, g, target], [(d, F32)], [(1, d), (1, 128)])
    return loss[0, 0], dh, dg


S5_LANES = 2 * S5_STATE
S5_GB = 128 // S5_GROUP


def _s5_prep(a_re, a_im, log_dt, b_re, b_im, c_re, c_im):
    c, gb = S5_CHUNK, S5_GB
    g = a_re.shape[0]
    nb = g // gb
    lam_re = jnp.minimum(a_re, -1e-4)
    lam_im = a_im
    dt = jnp.exp(log_dt)[:, None, None]
    ks = jnp.arange(c + 1, dtype=F32)
    mag = jnp.exp(lam_re[..., None] * dt * ks)
    ph = lam_im[..., None] * dt * ks
    pw_re, pw_im = mag * jnp.cos(ph), mag * jnp.sin(ph)
    den = lam_re * lam_re + lam_im * lam_im
    nr, ni = pw_re[..., 1] - 1.0, pw_im[..., 1]
    fr = (nr * lam_re + ni * lam_im) / den
    fi = (ni * lam_re - nr * lam_im) / den
    bb_re = fr[..., None] * b_re - fi[..., None] * b_im
    bb_im = fr[..., None] * b_im + fi[..., None] * b_re
    ct_re, ct_im = c_re.transpose(0, 2, 1), c_im.transpose(0, 2, 1)
    ca_re = ct_re[:, :, None, :] * pw_re[..., None] - ct_im[:, :, None, :] * pw_im[..., None]
    ca_im = ct_re[:, :, None, :] * pw_im[..., None] + ct_im[:, :, None, :] * pw_re[..., None]
    hp = lax.Precision.HIGHEST
    kern = (jnp.einsum("gpj,gpkh->gkjh", bb_re, ca_re[:, :, :c], precision=hp)
            - jnp.einsum("gpj,gpkh->gkjh", bb_im, ca_im[:, :, :c], precision=hp))
    rev_re = pw_re[:, :, :c][:, :, ::-1].transpose(0, 2, 1)
    rev_im = pw_im[:, :, :c][:, :, ::-1].transpose(0, 2, 1)
    bt_re, bt_im = bb_re.transpose(0, 2, 1), bb_im.transpose(0, 2, 1)
    wn_re = rev_re[:, :, None, :] * bt_re[:, None] - rev_im[:, :, None, :] * bt_im[:, None]
    wn_im = rev_re[:, :, None, :] * bt_im[:, None] + rev_im[:, :, None, :] * bt_re[:, None]
    wn = jnp.concatenate([wn_re, wn_im], axis=-1)
    wo = jnp.concatenate([ca_re[:, :, 1:].transpose(0, 2, 3, 1), -ca_im[:, :, 1:].transpose(0, 2, 3, 1)], axis=-1)

    def blocks(x):
        return x.reshape(nb, gb, c, S5_GROUP, x.shape[3]).transpose(0, 2, 1, 3, 4).reshape(nb, c, gb * S5_GROUP, x.shape[3])

    ar, ai = pw_re[..., c], pw_im[..., c]
    return (jnp.tile(blocks(kern), (1, 1, 1, gb)), blocks(wn), blocks(wo),
            jnp.concatenate([ar, ar], axis=1), jnp.concatenate([-ai, ai], axis=1))


def _step_rows(ref, tau, n):
    return ref[pl.ds(tau, n, stride=S5_CHUNK), :].astype(BF16)


def _cat_groups(ref, dtype):
    return jnp.concatenate([ref[:, j, :] for j in range(S5_GB)], axis=1).astype(dtype)


def _cat_steps(ref, n):
    return jnp.concatenate([_step_rows(ref, tau, n) for tau in range(S5_CHUNK)], axis=1)


def _stack_steps(ref, n):
    return jnp.concatenate([_step_rows(ref, tau, n) for tau in range(S5_CHUNK)], axis=0)


def _cat_ops(ref, axis, reverse=False):
    order = range(S5_CHUNK - 1, -1, -1) if reverse else range(S5_CHUNK)
    return jnp.concatenate([ref[k] for k in order], axis=axis)


def _row_group(rows, lanes):
    row = (lax.broadcasted_iota(jnp.int32, (rows, lanes), 0) // S5_GROUP) % S5_GB
    lane = (lax.broadcasted_iota(jnp.int32, (rows, lanes), 1) // S5_GROUP) % S5_GB
    return row, lane


def _own_group(x):
    row, lane = _row_group(*x.shape)
    return jnp.where(row == lane, x, jnp.zeros_like(x))


def _spread(x):
    row, _ = _row_group(*x.shape)
    return jnp.concatenate([jnp.where(row == j, x, jnp.zeros_like(x)) for j in range(S5_GB)], axis=1)


def _gather_own(x):
    row, _ = _row_group(x.shape[0], S5_LANES)
    out = jnp.zeros((x.shape[0], S5_LANES), x.dtype)
    for j in range(S5_GB):
        out = out + jnp.where(row == j, x[:, j * S5_LANES:(j + 1) * S5_LANES], 0.0)
    return out


def _s5_specs(t, d):
    nct, g = t // S5_CHUNK, d // S5_GROUP
    tok = pl.BlockSpec((t, 128), lambda i: (0, i))
    st = pl.BlockSpec((nct, S5_GB, S5_LANES), lambda i: (0, i, 0))
    op = lambda w: pl.BlockSpec((None,) + w.shape[1:], lambda i: (i, 0, 0, 0))
    return nct, g, tok, st, op


def _s5_chunk_fwd(u, bd, bn):
    t, d = u.shape
    nct, g, tok, st, op = _s5_specs(t, d)
    c = S5_CHUNK

    def body(u_ref, bd_ref, bn_ref, y_ref, s_ref):
        ucat = _cat_steps(u_ref, nct)
        sloc = jnp.dot(ucat, _spread(_cat_ops(bn_ref, 0)), preferred_element_type=F32)
        for j in range(S5_GB):
            s_ref[:, j, :] = sloc[:, j * S5_LANES:(j + 1) * S5_LANES]
        lags = _own_group(_cat_ops(bd_ref, 0, reverse=True))
        for tt in range(c):
            y_ref[pl.ds(tt, nct, stride=c), :] = jnp.dot(ucat[:, :(tt + 1) * 128], lags[(c - 1 - tt) * 128:, :],
                                                         preferred_element_type=F32)

    return pl.pallas_call(
        body, name="s5_chunk", grid=(d // 128,), in_specs=[tok, op(bd), op(bn)], out_specs=[tok, st],
        out_shape=[jax.ShapeDtypeStruct((t, d), F32), jax.ShapeDtypeStruct((nct, g, S5_LANES), F32)],
        compiler_params=_params("parallel"))(u, bd, bn)


def _s5_state_out(sprev, co, yin):
    t, d = yin.shape
    nct, g, tok, st, op = _s5_specs(t, d)
    c = S5_CHUNK

    def body(s_ref, co_ref, yi_ref, y_ref):
        ys = lax.dot_general(_cat_groups(s_ref, BF16), _spread(_cat_ops(co_ref, 0)), NT,
                             preferred_element_type=F32)
        for tt in range(c):
            rows = pl.ds(tt, nct, stride=c)
            y_ref[rows, :] = yi_ref[rows, :] + ys[:, tt * 128:(tt + 1) * 128]

    return pl.pallas_call(
        body, name="s5_state_out", grid=(d // 128,), in_specs=[st, op(co), tok], out_specs=tok,
        out_shape=jax.ShapeDtypeStruct((t, d), F32), compiler_params=_params("parallel"))(sprev, co, yin)


def _s5_state_out_dx(dyb, co):
    t, d = dyb.shape
    nct, g, tok, st, op = _s5_specs(t, d)
    c = S5_CHUNK

    def body(dy_ref, co_ref, ds_ref):
        acc = jnp.dot(_cat_steps(dy_ref, nct), _spread(_cat_ops(co_ref, 0)), preferred_element_type=F32)
        for j in range(S5_GB):
            ds_ref[:, j, :] = acc[:, j * S5_LANES:(j + 1) * S5_LANES]

    return pl.pallas_call(
        body, name="s5_state_out_dx", grid=(d // 128,), in_specs=[tok, op(co)], out_specs=st,
        out_shape=jax.ShapeDtypeStruct((nct, g, S5_LANES), F32), compiler_params=_params("parallel"))(dyb, co)


def _s5_chunk_dx(dyb, dsloc, bd, bn, skip):
    t, d = dyb.shape
    nct, g, tok, st, op = _s5_specs(t, d)
    c = S5_CHUNK

    def body(dy_ref, ds_ref, bd_ref, bn_ref, sk_ref, du_ref):
        dus = lax.dot_general(_cat_groups(ds_ref, BF16), _spread(_cat_ops(bn_ref, 0)), NT, preferred_element_type=F32)
        dycat = _cat_steps(dy_ref, nct)
        lags = _own_group(_cat_ops(bd_ref, 1))
        for tau in range(c):
            rows = pl.ds(tau, nct, stride=c)
            du_ref[rows, :] = (sk_ref[rows, :] + dus[:, tau * 128:(tau + 1) * 128]
                               + lax.dot_general(dycat[:, tau * 128:], lags[:, :(c - tau) * 128], NT,
                                                 preferred_element_type=F32))

    return pl.pallas_call(
        body, name="s5_chunk_dx", grid=(d // 128,), in_specs=[tok, st, op(bd), op(bn), tok], out_specs=tok,
        out_shape=jax.ShapeDtypeStruct((t, d), F32), compiler_params=_params("parallel"))(dyb, dsloc, bd, bn, skip)


def _s5_chunk_dw(u, dyb, dsloc, bd, bn):
    t, d = u.shape
    nct, g, tok, st, op = _s5_specs(t, d)
    c = S5_CHUNK

    def body(u_ref, dy_ref, ds_ref, dbd_ref, dbn_ref):
        dbn = _gather_own(lax.dot_general(_cat_steps(u_ref, nct), _cat_groups(ds_ref, BF16), TN,
                                          preferred_element_type=F32))
        for tau in range(c):
            dbn_ref[tau] = dbn[tau * 128:(tau + 1) * 128, :]
        ustk, dystk = _stack_steps(u_ref, nct), _stack_steps(dy_ref, nct)
        for k in range(c):
            dbd_ref[k] = _own_group(lax.dot_general(ustk[:(c - k) * nct], dystk[k * nct:], TN,
                                                    preferred_element_type=F32))

    return pl.pallas_call(
        body, name="s5_chunk_dw", grid=(d // 128,), in_specs=[tok, tok, st], out_specs=[op(bd), op(bn)],
        out_shape=[jax.ShapeDtypeStruct(bd.shape, F32), jax.ShapeDtypeStruct(bn.shape, F32)],
        compiler_params=_params("parallel"))(u, dyb, dsloc)


def _s5_state_out_dw(sprev, dyb, co):
    t, d = dyb.shape
    nct, g, tok, st, op = _s5_specs(t, d)
    c = S5_CHUNK

    def body(s_ref, dy_ref, dco_ref):
        dco = _gather_own(lax.dot_general(_cat_steps(dy_ref, nct), _cat_groups(s_ref, BF16), TN,
                                          preferred_element_type=F32))
        for tt in range(c):
            dco_ref[tt] = dco[tt * 128:(tt + 1) * 128, :]

    return pl.pallas_call(
        body, name="s5_state_out_dw", grid=(d // 128,), in_specs=[st, tok], out_specs=op(co),
        out_shape=jax.ShapeDtypeStruct(co.shape, F32), compiler_params=_params("parallel"))(sprev, dyb)


def _s5_scan_fwd(sloc, m1, m2):
    bl, nc, g, w = sloc.shape

    def body(s_ref, m1_ref, m2_ref, o_ref):
        a1, a2 = m1_ref[...], m2_ref[...]

        def step(c, states):
            new = []
            for b, s in enumerate(states):
                o_ref[b, c] = s
                new.append(a1 * s + a2 * pltpu.roll(s, S5_STATE, 1) + s_ref[b, c])
            return tuple(new)
        lax.fori_loop(0, nc, step, tuple(jnp.zeros((g, w), F32) for _ in range(bl)))

    vm = pl.BlockSpec(memory_space=pltpu.VMEM)
    return pl.pallas_call(
        body, name="s5_scan", in_specs=[vm, vm, vm], out_specs=vm,
        out_shape=jax.ShapeDtypeStruct(sloc.shape, F32),
        compiler_params=pltpu.CompilerParams(vmem_limit_bytes=VMEM_LIMIT))(sloc, m1, m2)


def _s5_scan_bwd(dsprev, sprev, m1, m2):
    bl, nc, g, w = dsprev.shape

    def body(d_ref, s_ref, m1_ref, m2_ref, g_ref, p1_ref, p2_ref):
        a1, a2 = m1_ref[...], m2_ref[...]
        zero = jnp.zeros((g, w), F32)

        def step(i, carry):
            gps, p1, p2 = carry
            c = nc - 2 - i
            new = []
            for b, gp in enumerate(gps):
                g_ref[b, c] = gp
                sp = s_ref[b, c]
                p1 = p1 + gp * sp
                p2 = p2 + gp * pltpu.roll(sp, S5_STATE, 1)
                new.append(d_ref[b, c] + a1 * gp - a2 * pltpu.roll(gp, S5_STATE, 1))
            return tuple(new), p1, p2

        for b in range(bl):
            g_ref[b, nc - 1] = zero
        _, p1, p2 = lax.fori_loop(0, nc - 1, step, (tuple(d_ref[b, nc - 1] for b in range(bl)), zero, zero))
        p1_ref[...] = p1
        p2_ref[...] = p2

    vm = pl.BlockSpec(memory_space=pltpu.VMEM)
    sd = jax.ShapeDtypeStruct
    return pl.pallas_call(
        body, name="s5_scan_bwd", in_specs=[vm, vm, vm, vm], out_specs=[vm, vm, vm],
        out_shape=[sd(dsprev.shape, F32), sd((g, w), F32), sd((g, w), F32)],
        compiler_params=pltpu.CompilerParams(vmem_limit_bytes=VMEM_LIMIT))(dsprev, sprev, m1, m2)


def _gelu_tanh_parts(y):
    c0 = math.sqrt(2.0 / math.pi)
    inner = c0 * (y + 0.044715 * y * y * y)
    th = jnp.tanh(inner)
    return th, c0 * (1.0 + 3 * 0.044715 * y * y)


def _s5_fwd(h, g, ops, d_skip, w_in, w_glu, bl):
    bd, bn, co, m1, m2 = ops
    t, d = h.shape
    nct, groups = t // S5_CHUNK, d // S5_GROUP
    hn = _rmsnorm("mix_norm", h, g)
    u = _mm_rs("s5_in", hn, "flat", w_in, 0)
    yin, sloc = _s5_chunk_fwd(u, bd.astype(BF16), bn.astype(BF16))
    sprev = _s5_scan_fwd(sloc.reshape(bl, nct // bl, groups, S5_LANES), m1, m2).reshape(nct, groups, S5_LANES)
    y = _s5_state_out(sprev, co.astype(BF16), yin)

    def fn(yy, uu, dd):
        y2 = yy + dd * uu
        th, _ = _gelu_tanh_parts(y2)
        return [0.5 * y2 * (1.0 + th)], []
    z = _rows("s5_gelu", fn, [y, u, d_skip], [(d, BF16)])[0][0]
    zz = _mm_cs("s5_glu", z, w_glu, 0, "flat", F32)

    def glu(hh, zv):
        return [hh + zv[:, :d] * _sigmoid(zv[:, d:])], []
    out = _rows("s5_glu_mix", glu, [h, zz], [(d, F32)])[0][0]
    return out, (h, hn, u, sprev, y, z, zz)


def _s5_bwd(dout, saved, g, ops, d_skip, w_in, w_glu, bl):
    h, hn, u, sprev, y, z, zz = saved
    bd, bn, co, m1, m2 = ops
    t, d = h.shape
    nct, groups = t // S5_CHUNK, d // S5_GROUP

    def glu_bwd(do, zv):
        sg = _sigmoid(zv[:, d:])
        return [jnp.concatenate([do * sg, do * zv[:, :d] * sg * (1.0 - sg)], axis=1)], []
    dzz = _rows("s5_glu_bwd", glu_bwd, [dout, zz], [(2 * d, BF16)])[0][0]
    dwglu = _mm_dw("s5_dwglu", z, None, dzz, "flat", (None, 0, 1))
    dz = _mm_cs_dx("s5_glu_dx", [(dzz, w_glu)], "flat", 0)

    def gelu_bwd(dzv, yy, uu, dd):
        y2 = yy + dd * uu
        th, dinner = _gelu_tanh_parts(y2)
        dy2 = dzv * (0.5 * (1.0 + th) + 0.5 * y2 * (1.0 - th * th) * dinner)
        return [dy2, dy2 * dd], [jnp.sum(dy2 * uu, axis=0, keepdims=True)]
    (dyb, du_skip), (dd,) = _rows("s5_gelu_bwd", gelu_bwd, [dz, y, u, d_skip], [(d, F32), (d, F32)], [(1, d)])
    bd_b, bn_b, co_b = bd.astype(BF16), bn.astype(BF16), co.astype(BF16)
    dsprev = _s5_state_out_dx(dyb, co_b)
    shape4 = (bl, nct // bl, groups, S5_LANES)
    dsloc, dm1, dm2 = _s5_scan_bwd(dsprev.reshape(shape4), sprev.reshape(shape4), m1, m2)
    dsloc = dsloc.reshape(nct, groups, S5_LANES)
    du = _s5_chunk_dx(dyb, dsloc, bd_b, bn_b, du_skip).astype(BF16)
    dbd, dbn = _s5_chunk_dw(u, dyb, dsloc, bd, bn)
    dco = _s5_state_out_dw(sprev, dyb, co)
    dwin = _mm_dw("s5_dwin", hn, "flat", du, None, (None, 0, 1))
    dhn = _mm_rs_dx("s5_in_dx", du, w_in, 0, "flat", F32)
    dh, dhb, dg = _rmsnorm_bwd("mix_norm_bwd", dout, dhn, h, g)
    return dh, dhb, dg, dwin, dwglu, dd, (dbd, dbn, dco, dm1, dm2)


def _sb_block(qi, idx, tb):
    kb = qi - idx
    return pl.multiple_of(jnp.maximum(kb, 0) * tb, tb), idx == 0, kb >= 0


def _sb_scores(q, kblk, diag, exists, row, col):
    z = lax.dot_general(q, kblk, NT, preferred_element_type=F32) * (SB_HEAD_DIM ** -0.5)
    l1 = jnp.log(1.0 + jnp.exp(-jnp.abs(z)))
    ls = jnp.minimum(z, 0.0) - l1
    mask = jnp.logical_and(jnp.logical_or(col < row, jnp.logical_not(diag)), exists)
    lk = jnp.where(mask, ls - z, 0.0)
    return ls, lk, mask


def _split_dot(v, tri):
    hi = v.astype(BF16)
    lo = (v - hi.astype(F32)).astype(BF16)
    return (jnp.dot(hi, tri, preferred_element_type=F32) + jnp.dot(lo, tri, preferred_element_type=F32))


SB_PAIR =2 * SB_HEAD_DIM


def _pair_masks(tb):
    lane = lax.broadcasted_iota(jnp.int32, (1, SB_PAIR), 1)
    row = lax.broadcasted_iota(jnp.int32, (tb, tb), 0)
    col = lax.broadcasted_iota(jnp.int32, (tb, tb), 1)
    return [lane < SB_HEAD_DIM, lane >= SB_HEAD_DIM], row, col


def _pair_more(qi, carry):
    j, crs = carry[0], carry[2]
    return jnp.logical_and(j <= qi, jnp.maximum(jnp.max(crs[0]), jnp.max(crs[1])) > SB_CUT)


def _pair_specs(bl, l, d, tb):
    nq, off = l // tb, d // SB_PAIR
    qspec = pl.BlockSpec((tb, SB_PAIR), lambda b, p, i: (b * nq + i, p))
    kspec = pl.BlockSpec((l, SB_PAIR), lambda b, p, i: (b, off + p))
    vspec = pl.BlockSpec((l, SB_PAIR), lambda b, p, i: (b, 2 * off + p))
    return qspec, kspec, vspec


def _sb_attn_fwd2(qkv, bl):
    t, d3 = qkv.shape
    d, l = d3 // 3, t // bl
    tb = min(SB_BLOCK, l)
    nq = l // tb

    def body(q_ref, k_ref, v_ref, o_ref, ob_ref):
        qi = pl.program_id(2)
        heads, row, col = _pair_masks(tb)
        qv = q_ref[...]
        qh = [jnp.where(m, qv, jnp.zeros_like(qv)) for m in heads]
        tri = (row > col).astype(BF16)

        def step(carry):
            j, acc, crs = carry
            crs = list(crs)
            where = [_sb_block(qi, j + u, tb) for u in range(SB_UNROLL)]
            kblks = [k_ref[pl.ds(ks, tb), :] for ks, _, _ in where]
            scores = [[_sb_scores(qh[hd], kblks[u], where[u][1], where[u][2], row, col) for hd in range(2)]
                      for u in range(SB_UNROLL)]
            laters = [[_split_dot(sc[1], tri) for sc in su] for su in scores]
            for u in range(SB_UNROLL):
                vblk = v_ref[pl.ds(where[u][0], tb), :]
                outs = []
                for hd in range(2):
                    ls, lk, mask = scores[u][hd]
                    att = jnp.where(mask, jnp.exp(ls + laters[u][hd] + crs[hd]), 0.0)
                    outs.append(jnp.dot(att.astype(BF16), vblk, preferred_element_type=F32))
                    crs[hd] = crs[hd] + jnp.sum(lk, axis=1, keepdims=True)
                acc = acc + jnp.where(heads[0], outs[0], outs[1])
            return j + SB_UNROLL, acc, tuple(crs)

        zc = jnp.zeros((tb, 1), F32)
        _, acc, _ = lax.while_loop(functools.partial(_pair_more, qi), step,
                                   (jnp.int32(0), jnp.zeros((tb, SB_PAIR), F32), (zc, zc)))
        o_ref[...] = acc
        ob_ref[...] = acc.astype(BF16)

    qspec, kspec, vspec = _pair_specs(bl, l, d, tb)
    return pl.pallas_call(
        body, name="sb_attn", grid=(bl, d // SB_PAIR, nq), in_specs=[qspec, kspec, vspec], out_specs=[qspec, qspec],
        out_shape=[jax.ShapeDtypeStruct((t, d), F32), jax.ShapeDtypeStruct((t, d), BF16)],
        compiler_params=_params("parallel", "parallel", "parallel"))(qkv, qkv, qkv)


def _sb_attn_bwd2(qkv, o, do, bl):
    t, d3 = qkv.shape
    d, l = d3 // 3, t // bl
    tb = min(SB_BLOCK, l)
    nq = l // tb
    scale = SB_HEAD_DIM ** -0.5

    def body(q_ref, k_ref, v_ref, o_ref, do_ref, dq_ref, dk_ref, dv_ref, dk_acc, dv_acc):
        qi = pl.program_id(2)

        @pl.when(qi == 0)
        def _():
            dk_acc[...] = jnp.zeros_like(dk_acc)
            dv_acc[...] = jnp.zeros_like(dv_acc)

        heads, row, col = _pair_masks(tb)
        qv = q_ref[...]
        dov = do_ref[...].astype(BF16)
        qh = [jnp.where(m, qv, jnp.zeros_like(qv)) for m in heads]
        doh = [jnp.where(m, dov, jnp.zeros_like(dov)) for m in heads]
        ov = o_ref[...]
        dsum = [jnp.sum(dh.astype(F32) * ov, axis=1, keepdims=True) for dh in doh]
        tri = (row > col).astype(BF16)
        tri_inc = (row >= col).astype(BF16)

        def step(carry):
            j, dq, crs, ces = carry
            crs, ces = list(crs), list(ces)
            n = range(SB_UNROLL)
            where = [_sb_block(qi, j + u, tb) for u in n]
            rows = [pl.ds(ks, tb) for ks, _, _ in where]
            kblks = [k_ref[rows[u], :] for u in n]
            vblks = [v_ref[rows[u], :] for u in n]
            scores = [[_sb_scores(qh[hd], kblks[u], where[u][1], where[u][2], row, col) for hd in range(2)] for u in n]
            laters = [[_split_dot(sc[1], tri) for sc in su] for su in scores]
            datts = [[lax.dot_general(doh[hd], vblks[u], NT, preferred_element_type=F32) for hd in range(2)] for u in n]
            atts = [[None, None] for _ in n]
            for u in n:
                for hd in range(2):
                    ls, lk, mask = scores[u][hd]
                    atts[u][hd] = jnp.where(mask, jnp.exp(ls + laters[u][hd] + crs[hd]), 0.0).astype(BF16)
                    crs[hd] = crs[hd] + jnp.sum(lk, axis=1, keepdims=True)
            es = [[atts[u][hd].astype(F32) * datts[u][hd] for hd in range(2)] for u in n]
            sufs = [[_split_dot(e, tri_inc) for e in eu] for eu in es]
            dzs = [[None, None] for _ in n]
            for u in n:
                for hd in range(2):
                    ls, _, mask = scores[u][hd]
                    pre = dsum[hd] - ces[hd] - sufs[u][hd]
                    sg = jnp.exp(ls)
                    dzs[u][hd] = (jnp.where(mask, es[u][hd] * (1.0 - sg) - pre * sg, 0.0) * scale).astype(BF16)
                    ces[hd] = ces[hd] + jnp.sum(es[u][hd], axis=1, keepdims=True)
            for u in n:
                dq = dq + jnp.where(heads[0], jnp.dot(dzs[u][0], kblks[u], preferred_element_type=F32),
                                    jnp.dot(dzs[u][1], kblks[u], preferred_element_type=F32))
                dk_acc[rows[u], :] += (lax.dot_general(dzs[u][0], qh[0], TN, preferred_element_type=F32)
                                       + lax.dot_general(dzs[u][1], qh[1], TN, preferred_element_type=F32))
                dv_acc[rows[u], :] += (lax.dot_general(atts[u][0], doh[0], TN, preferred_element_type=F32)
                                       + lax.dot_general(atts[u][1], doh[1], TN, preferred_element_type=F32))
            return j + SB_UNROLL, dq, tuple(crs), tuple(ces)

        zc = jnp.zeros((tb, 1), F32)
        _, dq, _, _ = lax.while_loop(functools.partial(_pair_more, qi), step,
                                     (jnp.int32(0), jnp.zeros((tb, SB_PAIR), F32), (zc, zc), (zc, zc)))
        dq_ref[...] = dq.astype(BF16)

        @pl.when(qi == nq - 1)
        def _():
            dk_ref[...] = dk_acc[...].astype(BF16)
            dv_ref[...] = dv_acc[...].astype(BF16)

    qspec, kspec, vspec = _pair_specs(bl, l, d, tb)
    blk = pl.BlockSpec((tb, SB_PAIR), lambda b, p, i: (b * nq + i, p))
    full = pl.BlockSpec((l, SB_PAIR), lambda b, p, i: (b, p))
    sd = jax.ShapeDtypeStruct((t, d), BF16)
    dq, dk, dv = pl.pallas_call(
        body, name="sb_attn_bwd", grid=(bl, d // SB_PAIR, nq), in_specs=[qspec, kspec, vspec, blk, blk],
        out_specs=[blk, full, full], out_shape=[sd, sd, sd],
        scratch_shapes=[pltpu.VMEM((l, SB_PAIR), F32), pltpu.VMEM((l, SB_PAIR), F32)],
        compiler_params=_params("parallel", "parallel", "arbitrary"))(qkv, qkv, qkv, o, do)
    return jnp.concatenate([dq, dk, dv], axis=1)


def _sb_fwd(h, g, w_qkv, w_o, bl):
    t, d = h.shape
    hn = _rmsnorm("mix_norm", h, g)
    qkv = _mm_cs("sb_qkv", hn, w_qkv, 0, "flat", BF16)
    o, ob = _sb_attn_fwd2(qkv, bl)
    out = _mm_rs("sb_out", ob, "flat", w_o, 0, res=h)
    return out, (h, hn, qkv, o, ob)


def _sb_bwd(dout, dob, saved, g, w_qkv, w_o, bl):
    h, hn, qkv, o, ob = saved
    dwo = _mm_dw("sb_dwo", ob, "flat", dob, None, (None, 0, 1))
    do = _mm_rs_dx("sb_out_dx", dob, w_o, 0, "flat", F32)
    dqkv = _sb_attn_bwd2(qkv, o, do, bl)
    dwqkv = _mm_dw("sb_dwqkv", hn, None, dqkv, "flat", (None, 0, 1))
    dh, dhb, dg = _mm_cs_dx("sb_qkv_dx", [(dqkv, w_qkv)], "flat", 0, norm=(dout, h, g))
    return dh, dhb, dg, dwqkv, dwo


def _adamw_update(wv, gr, mv, vv):
    c1 = 1.0 / (1.0 - ADAM_B1 ** ADAM_STEP)
    c2 = 1.0 / (1.0 - ADAM_B2 ** ADAM_STEP)
    mn = ADAM_B1 * mv + (1.0 - ADAM_B1) * gr
    vn = ADAM_B2 * vv + (1.0 - ADAM_B2) * gr * gr
    delta = -ADAM_LR * ((mn * c1) / (jnp.sqrt(vn * c2) + ADAM_EPS) + ADAM_WD * wv)
    return delta, mn, vn


def _adamw_small(w, gr, m, v):
    def fn(wv, gv, mv, vv):
        return list(_adamw_update(wv, gv, mv, vv)), []
    return _rows("adamw_small", fn, [w, gr, m, v], [(w.shape[1], F32)] * 3)[0]


def _place():
    x, y, c = lax.axis_index("x"), lax.axis_index("y"), lax.axis_index("c")
    chips = [(1 - x, y), (x, 1 - y), (1 - x, 1 - y)]
    return x, y, c, chips


def _remote(src, dst, send_sem, recv_sem, to):
    return pltpu.make_async_remote_copy(src_ref=src, dst_ref=dst, send_sem=send_sem, recv_sem=recv_sem,
                                        device_id=to, device_id_type=MESH)


def _half(ref, c, rh, lead):
    return ref.at[(slice(None),) * lead + (pl.ds(c * rh, rh),)]


def _allgather_weights(ws):
    n = len(ws)

    def body(*refs):
        ins, outs = refs[:n], refs[n:2 * n]
        send, recv = refs[2 * n:]
        x, y, c, _ = _place()
        chip_x, chip_y, chip_d = (1 - x, y), (x, 1 - y), (1 - x, 1 - y)
        sibling = (x, y, 1 - c)
        index = lambda chip: 2 * chip[0] + chip[1]
        sent = []

        def quarter(ref, half, q, rq):
            return ref.at[:, pl.ds((2 * half + q) * rq, rq)]

        def copy(t, kind, src, dst, to):
            return _remote(src, dst, send.at[t, kind], recv.at[t, kind], to)

        def start(cp):
            cp.start()
            sent.append(cp)

        for t in range(n):
            rq = ws[t].shape[1] // 4
            for q in range(2):
                for base, chip in ((0, chip_x), (2, chip_y)):
                    start(copy(t, base + q, quarter(ins[t], c, q, rq), quarter(outs[t].at[index((x, y))], c, q, rq), (*chip, c)))
        for t in range(n):
            rq = ws[t].shape[1] // 4
            landings = [(chip_x, 0, 0, chip_x, ((4, chip_y), (6, None))), (chip_y, 1, 3, chip_y, ((5, chip_x), (9, None))),
                        (chip_x, 1, 1, chip_x, ((7, None),)), (chip_y, 0, 2, chip_y, ((8, None),)),
                        (chip_d, 0, 4, chip_y, ((10, None),)), (chip_d, 1, 5, chip_x, ((11, None),))]
            for origin, q, kind, sender, onward in landings:
                piece = quarter(outs[t].at[index(origin)], c, q, rq)
                copy(t, kind, piece, piece, (*sender, c)).wait_recv()
                for kind2, chip in onward:
                    start(copy(t, kind2, piece, piece, sibling if chip is None else (*chip, c)))
        for t in range(n):
            rq = ws[t].shape[1] // 4
            for kind, (origin, q) in zip(range(6, 12), ((chip_x, 0), (chip_x, 1), (chip_y, 0), (chip_y, 1), (chip_d, 0), (chip_d, 1))):
                piece = quarter(outs[t].at[index(origin)], 1 - c, q, rq)
                copy(t, kind, piece, piece, sibling).wait_recv()
        for cp in sent:
            cp.wait_send()

    res = pl.pallas_call(
        body, name="allgather_weights", in_specs=[ANY] * n, out_specs=[ANY] * n,
        out_shape=[jax.ShapeDtypeStruct((N_CHIPS,) + w.shape, w.dtype) for w in ws],
        scratch_shapes=[pltpu.SemaphoreType.DMA((n, 12)), pltpu.SemaphoreType.DMA((n, 12))],
    )(*ws)
    own = 2 * lax.axis_index("x") + lax.axis_index("y")
    return [lax.dynamic_update_slice(g, w[None], (own, 0, 0, 0)) for g, w in zip(res, ws)]


def _pair_exchange(gs):
    n = len(gs)

    def body(*refs):
        ins, outs = refs[:n], refs[n:2 * n]
        send, recv = refs[2 * n:]
        x, y, c, _ = _place()
        copies = [_remote(_half(ins[t], 1 - c, gs[t].shape[2] // 2, 2), outs[t], send.at[t], recv.at[t], (x, y, 1 - c))
                  for t in range(n)]
        for cp in copies:
            cp.start()
        for cp in copies:
            cp.wait()

    return pl.pallas_call(
        body, name="grad_pair_exchange", in_specs=[ANY] * n, out_specs=[ANY] * n,
        out_shape=[jax.ShapeDtypeStruct(g.shape[:2] + (g.shape[2] // 2, g.shape[3]), F32) for g in gs],
        scratch_shapes=[pltpu.SemaphoreType.DMA((n,)), pltpu.SemaphoreType.DMA((n,))],
    )(*gs)


def _pair_sum(g, theirs, c_idx):
    n4, ly, r, cc = g.shape
    rh = r // 2
    tm = _tile(rh, 512)
    nt = rh // tm

    def body(c_ref, g_ref, t_ref, o_ref):
        o_ref[...] = (g_ref[...] + t_ref[...]).astype(o_ref.dtype)

    blk = (None, tm, cc)
    grid_spec = pltpu.PrefetchScalarGridSpec(
        num_scalar_prefetch=1, grid=(n4 * ly, nt),
        in_specs=[pl.BlockSpec(blk, lambda a, i, cr: (a, cr[0] * nt + i, 0)), pl.BlockSpec(blk, lambda a, i, cr: (a, i, 0))],
        out_specs=pl.BlockSpec(blk, lambda a, i, cr: (a, i, 0)))
    out = pl.pallas_call(
        body, name="grad_pair_sum", grid_spec=grid_spec, out_shape=jax.ShapeDtypeStruct((n4 * ly, rh, cc), BF16),
        compiler_params=_params("parallel", "parallel"))(c_idx, g.reshape(n4 * ly, r, cc), theirs.reshape(n4 * ly, rh, cc))
    return out.reshape(n4, ly, rh, cc)


def _quarter(ref, q, rq):
    return ref.at[:, pl.ds(q * rq, rq)]


def _chip_exchange_first(ps):
    n = len(ps)

    def body(*refs):
        ins, outs = refs[:n], refs[n:2 * n]
        send, recv = refs[2 * n:]
        x, y, c, _ = _place()
        index = lambda cx, cy: 2 * cx + cy
        copies = []
        for t in range(n):
            rq = ps[t].shape[2] // 2
            for base, q, chip in ((0, 0, (1 - x, y)), (2, 1, (x, 1 - y))):
                for j, slice_of in enumerate((chip, (1 - x, 1 - y))):
                    copies.append(_remote(_quarter(ins[t].at[index(*slice_of)], q, rq), outs[t].at[base + j],
                                          send.at[t, base + j], recv.at[t, base + j], (*chip, c)))
        for cp in copies:
            cp.start()
        for cp in copies:
            cp.wait()

    return pl.pallas_call(
        body, name="grad_chip_exchange", in_specs=[ANY] * n, out_specs=[ANY] * n,
        out_shape=[jax.ShapeDtypeStruct((4, p.shape[1], p.shape[2] // 2, p.shape[3]), p.dtype) for p in ps],
        scratch_shapes=[pltpu.SemaphoreType.DMA((n, 4)), pltpu.SemaphoreType.DMA((n, 4))],
    )(*ps)


def _chip_relay_sum(p, first, where):
    _, ly, rh, cc = p.shape
    rq = rh // 2
    tm = _tile(rq, 512)
    nt = rq // tm

    def body(w_ref, p_ref, f_ref, out_ref):
        out_ref[...] = (p_ref[...].astype(F32) + f_ref[...].astype(F32)).astype(out_ref.dtype)

    blk = (None, None, tm, cc)
    grid_spec = pltpu.PrefetchScalarGridSpec(
        num_scalar_prefetch=1, grid=(2, ly, nt),
        in_specs=[pl.BlockSpec(blk, lambda s, l, i, w: (w[2 - s], l, s * nt + i, 0)),
                  pl.BlockSpec(blk, lambda s, l, i, w: (1 + 2 * s, l, i, 0))],
        out_specs=pl.BlockSpec(blk, lambda s, l, i, w: (s, l, i, 0)))
    return pl.pallas_call(
        body, name="grad_relay_sum", grid_spec=grid_spec, out_shape=jax.ShapeDtypeStruct((2, ly, rq, cc), p.dtype),
        compiler_params=_params("parallel", "parallel", "parallel"))(where, p, first)


def _chip_exchange_second(ss):
    n = len(ss)

    def body(*refs):
        ins, outs = refs[:n], refs[n:2 * n]
        send, recv = refs[2 * n:]
        x, y, c, _ = _place()
        copies = []
        for t in range(n):
            for j, chip in enumerate(((x, 1 - y), (1 - x, y))):
                copies.append(_remote(ins[t].at[j], outs[t].at[j], send.at[t, j], recv.at[t, j], (*chip, c)))
        for cp in copies:
            cp.start()
        for cp in copies:
            cp.wait()

    return pl.pallas_call(
        body, name="grad_chip_exchange_2", in_specs=[ANY] * n, out_specs=[ANY] * n,
        out_shape=[jax.ShapeDtypeStruct(s.shape, s.dtype) for s in ss],
        scratch_shapes=[pltpu.SemaphoreType.DMA((n, 2)), pltpu.SemaphoreType.DMA((n, 2))],
    )(*ss)


def _chip_sum(p, first, second, where):
    _, ly, rh, cc = p.shape
    rq = rh // 2
    tm = _tile(rq, 512)
    nt = rq // tm

    def body(w_ref, p_ref, f_ref, s_ref, out_ref):
        out_ref[...] = (p_ref[...].astype(F32) + f_ref[...].astype(F32)) + s_ref[...].astype(F32)

    blk = (None, None, tm, cc)
    grid_spec = pltpu.PrefetchScalarGridSpec(
        num_scalar_prefetch=1, grid=(ly, 2, nt),
        in_specs=[pl.BlockSpec(blk, lambda l, q, i, w: (w[0], l, q * nt + i, 0)),
                  pl.BlockSpec(blk, lambda l, q, i, w: (2 * q, l, i, 0)),
                  pl.BlockSpec(blk, lambda l, q, i, w: (q, l, i, 0))],
        out_specs=pl.BlockSpec((None, tm, cc), lambda l, q, i, w: (l, q * nt + i, 0)))
    return pl.pallas_call(
        body, name="grad_chip_sum", grid_spec=grid_spec, out_shape=jax.ShapeDtypeStruct((ly, rh, cc), F32),
        compiler_params=_params("parallel", "parallel", "parallel"))(where, p, first, second)


def _pair_swap(halves):
    n = len(halves)

    def body(*refs):
        ins, outs = refs[:n], refs[n:2 * n]
        send, recv = refs[2 * n:]
        x, y, c, _ = _place()
        copies = [_remote(ins[t], outs[t], send.at[t], recv.at[t], (x, y, 1 - c)) for t in range(n)]
        for cp in copies:
            cp.start()
        for cp in copies:
            cp.wait()

    return pl.pallas_call(
        body, name="grad_pair_swap", in_specs=[ANY] * n, out_specs=[ANY] * n,
        out_shape=[jax.ShapeDtypeStruct(h.shape, F32) for h in halves],
        scratch_shapes=[pltpu.SemaphoreType.DMA((n,)), pltpu.SemaphoreType.DMA((n,))],
    )(*halves)


def _adamw_big(w, m, v, mine, theirs, c_idx):
    ly, r, cc = w.shape
    rh = r // 2
    tm = _tile(rh, 512)
    nt = rh // tm

    def body(c_ref, w_ref, m_ref, v_ref, a_ref, b_ref, g_out, d_out, m_out, v_out):
        gr = jnp.where(pl.program_id(1) == c_ref[0], a_ref[...], b_ref[...])
        delta, mn, vn = _adamw_update(w_ref[...], gr, m_ref[...], v_ref[...])
        g_out[...] = gr
        d_out[...] = delta
        m_out[...] = mn
        v_out[...] = vn

    blk = (None, tm, cc)
    full = pl.BlockSpec(blk, lambda l, hc, i, cr: (l, hc * nt + i, 0))
    half = pl.BlockSpec(blk, lambda l, hc, i, cr: (l, i, 0))
    grid_spec = pltpu.PrefetchScalarGridSpec(
        num_scalar_prefetch=1, grid=(ly, 2, nt), in_specs=[full, full, full, half, half], out_specs=[full] * 4)
    sd = jax.ShapeDtypeStruct(w.shape, F32)
    return pl.pallas_call(
        body, name="adamw", grid_spec=grid_spec, out_shape=[sd] * 4,
        compiler_params=_params("parallel", "parallel", "parallel"))(c_idx, w, m, v, mine, theirs)


def _allreduce_small(v):
    rows, w = v.shape

    def body(x_ref, sum_ref, all_ref, send, recv, local):
        x, y, c, chips = _place()
        me, sibling = (x, y, c), (x, y, 1 - c)

        def slot(px, py, pc):
            return all_ref.at[4 * px + 2 * py + pc]

        def copy(k, block, to, src=None):
            return _remote(slot(*block) if src is None else src, slot(*block), send.at[k], recv.at[k], to)

        mine = pltpu.make_async_copy(x_ref, slot(*me), local)
        mine.start()
        first = [copy(0, me, sibling, src=x_ref)]
        first += [copy(1 + j, me, (*chip, c), src=x_ref) for j, chip in enumerate(chips)]
        for cp in first:
            cp.start()
        passed = [copy(4 + j, (*chip, c), sibling) for j, chip in enumerate(chips)]
        for j, chip in enumerate(chips):
            copy(1 + j, (*chip, c), me).wait_recv()
            passed[j].start()
        copy(0, sibling, me).wait_recv()
        for j, chip in enumerate(chips):
            copy(4 + j, (*chip, 1 - c), me).wait_recv()
        for cp in first + passed:
            cp.wait_send()
        mine.wait()
        tot = all_ref[0]
        for k in range(1, N_DEV):
            tot = tot + all_ref[k]
        sum_ref[...] = tot

    vm = pl.BlockSpec(memory_space=pltpu.VMEM)
    return pl.pallas_call(
        body, name="allreduce_small", in_specs=[vm], out_specs=[vm, vm],
        out_shape=[jax.ShapeDtypeStruct((rows, w), F32), jax.ShapeDtypeStruct((N_DEV, rows, w), F32)],
        scratch_shapes=[pltpu.SemaphoreType.DMA((7,)), pltpu.SemaphoreType.DMA((7,)), pltpu.SemaphoreType.DMA],
        compiler_params=pltpu.CompilerParams(vmem_limit_bytes=VMEM_LIMIT),
    )(v)[0]


BIG = ["ffn1_w1", "ffn1_w3", "ffn1_w2", "ffn2_w1", "ffn2_w3", "ffn2_w2", "ple_proj", "ple_gate",
       "s5_w_in", "s5_w_glu", "sb_w_qkv", "sb_w_o"]
TRANSPOSED = ("ffn1_w1", "ffn1_w3", "ffn2_w1", "ffn2_w3")
SMALL = ["ffn1_norm", "mix_norm", "ffn2_norm", "ple_norm", "s5_a_re", "s5_a_im", "s5_log_dt", "s5_b_re", "s5_b_im",
         "s5_c_re", "s5_c_im", "s5_d", "final_norm"]
ORDER = ["ffn1_norm", "ffn1_w1", "ffn1_w3", "ffn1_w2", "mix_norm", "ffn2_norm", "ffn2_w1", "ffn2_w3", "ffn2_w2",
         "ple_norm", "ple_proj", "ple_gate", "s5_w_in", "s5_a_re", "s5_a_im", "s5_log_dt", "s5_b_re", "s5_b_im",
         "s5_c_re", "s5_c_im", "s5_d", "s5_w_glu", "sb_w_qkv", "sb_w_o", "final_norm"]


def _pack(arrays):
    flat = jnp.concatenate([a.reshape(-1) for a in arrays])
    pad = (-flat.shape[0]) % 1024
    return jnp.pad(flat, (0, pad)).reshape(-1, 128)


def _unpack(packed, like):
    flat = packed.reshape(-1)
    out, off = [], 0
    for a in like:
        out.append(flat[off:off + a.size].reshape(a.shape))
        off += a.size
    return out


def _fwd_bwd(x, p, target, w, gathered):
    bl, l, d = x.shape
    t = bl * l
    depth = w["ffn1_norm"].shape[0]
    s5_ops, s5_vjp = jax.vjp(_s5_prep, w["s5_a_re"][0], w["s5_a_im"][0], w["s5_log_dt"][0], w["s5_b_re"][0],
                             w["s5_b_im"][0], w["s5_c_re"][0], w["s5_c_im"][0])

    h = x.reshape(t, d)
    p2 = [p[i].reshape(t, p.shape[-1]).astype(BF16) for i in range(depth)]
    saved = []
    for i in range(depth):
        norm = lambda name: w[name][i:i + 1]
        h, s1 = _ffn_fwd(h, norm("ffn1_norm"), gathered["ffn1_w1"], gathered["ffn1_w3"], gathered["ffn1_w2"], i)
        if i % 2 == 0:
            h, s2 = _s5_fwd(h, norm("mix_norm"), s5_ops, w["s5_d"][i // 2:i // 2 + 1], gathered["s5_w_in"], gathered["s5_w_glu"], bl)
        else:
            h, s2 = _sb_fwd(h, norm("mix_norm"), gathered["sb_w_qkv"], gathered["sb_w_o"], bl)
        h, s3 = _ffn_fwd(h, norm("ffn2_norm"), gathered["ffn2_w1"], gathered["ffn2_w3"], gathered["ffn2_w2"], i)
        h, s4 = _ple_fwd(h, norm("ple_norm"), p2[i], gathered["ple_proj"], gathered["ple_gate"], i)
        saved.append((s1, s2, s3, s4))

    loss, dh, dfinal = _head(h, w["final_norm"].reshape(1, d), target.reshape(t, d))

    big = {k: None for k in BIG}
    small = {k: [None] * w[k].shape[0] if w[k].ndim > 1 else None for k in SMALL}
    small["final_norm"] = dfinal.reshape(d)
    for i in reversed(range(depth)):
        norm = lambda name: w[name][i:i + 1]
        slots = lambda *names: [(big[k], i, depth) for k in names]
        s1, s2, s3, s4 = saved[i]
        dh, dhb, dg, big["ple_proj"], big["ple_gate"] = _ple_bwd(
            dh, s4, norm("ple_norm"), p2[i], gathered["ple_proj"], gathered["ple_gate"], i, slots("ple_proj", "ple_gate"))
        small["ple_norm"][i] = dg[0]
        dh, dhb, dg, big["ffn2_w1"], big["ffn2_w3"], big["ffn2_w2"] = _ffn_bwd(
            dh, dhb, s3, norm("ffn2_norm"), gathered["ffn2_w1"], gathered["ffn2_w3"], gathered["ffn2_w2"], i,
            slots("ffn2_w1", "ffn2_w3", "ffn2_w2"))
        small["ffn2_norm"][i] = dg[0]
        if i % 2 == 0:
            dh, dhb, dg, big["s5_w_in"], big["s5_w_glu"], dd, dops = _s5_bwd(
                dh, s2, norm("mix_norm"), s5_ops, w["s5_d"][i // 2:i // 2 + 1], gathered["s5_w_in"], gathered["s5_w_glu"], bl)
            small["s5_d"][0] = dd[0]
            raw = s5_vjp(dops)
            for name, gr in zip(["s5_a_re", "s5_a_im", "s5_log_dt", "s5_b_re", "s5_b_im", "s5_c_re", "s5_c_im"], raw):
                small[name][0] = gr
        else:
            dh, dhb, dg, big["sb_w_qkv"], big["sb_w_o"] = _sb_bwd(
                dh, dhb, s2, norm("mix_norm"), gathered["sb_w_qkv"], gathered["sb_w_o"], bl)
        small["mix_norm"][i] = dg[0]
        dh, dhb, dg, big["ffn1_w1"], big["ffn1_w3"], big["ffn1_w2"] = _ffn_bwd(
            dh, dhb, s1, norm("ffn1_norm"), gathered["ffn1_w1"], gathered["ffn1_w3"], gathered["ffn1_w2"], i,
            slots("ffn1_w1", "ffn1_w3", "ffn1_w2"))
        small["ffn1_norm"][i] = dg[0]
    small_list = [jnp.stack(small[k]) if isinstance(small[k], list) else small[k] for k in SMALL]
    return loss, dh.reshape(bl, l, d), big, small_list


def _step(x, p, target, w, m, v):
    flip = lambda tree: {k: jnp.swapaxes(a, 1, 2) if k in TRANSPOSED else a for k, a in tree.items()}
    w, m, v = flip(w), flip(m), flip(v)
    gathered = dict(zip(BIG, _allgather_weights([_to_bf16(w[k]) for k in BIG])))
    loss, grad_x, big, small_list = _fwd_bwd(x, p, target, w, gathered)

    c_idx = lax.axis_index("c").astype(jnp.int32).reshape(1)
    cx, cy = lax.axis_index("x"), lax.axis_index("y")
    where = jnp.stack([2 * cx + cy, 2 * (1 - cx) + cy, 2 * cx + (1 - cy)]).astype(jnp.int32)
    partial = [big[k] for k in BIG]
    pair = [_pair_sum(g, t, c_idx) for g, t in zip(partial, _pair_exchange(partial))]
    first = _chip_exchange_first(pair)
    second = _chip_exchange_second([_chip_relay_sum(pr, f, where) for pr, f in zip(pair, first)])
    mine = [_chip_sum(pr, f, s, where) for pr, f, s in zip(pair, first, second)]
    theirs = _pair_swap(mine)
    out_g, out_d, out_m, out_v = {}, {}, {}, {}
    for k, a, b in zip(BIG, mine, theirs):
        out_g[k], out_d[k], out_m[k], out_v[k] = _adamw_big(w[k], m[k], v[k], a, b, c_idx)

    like = [w[k] for k in SMALL]
    pad = [jnp.zeros((1,), F32)]
    g_small = _allreduce_small(_pack(small_list + [loss.reshape(1)]))
    packed = (g_small,) + tuple(_adamw_small(_pack(like + pad), g_small, _pack([m[k] for k in SMALL] + pad),
                                             _pack([v[k] for k in SMALL] + pad)))
    for dst, pk in zip((out_g, out_d, out_m, out_v), packed):
        dst.update(dict(zip(SMALL, _unpack(pk, like))))
    loss = g_small.reshape(-1)[sum(a.size for a in like)]
    out_g, out_d, out_m, out_v = flip(out_g), flip(out_d), flip(out_m), flip(out_v)
    return (loss, grad_x, *[out_g[k] for k in ORDER], *[out_d[k] for k in ORDER],
            *[out_m[k] for k in ORDER], *[out_v[k] for k in ORDER])


def kernel(x, p, ffn1_norm, ffn1_w1, ffn1_w3, ffn1_w2, mix_norm, ffn2_norm, ffn2_w1, ffn2_w3, ffn2_w2, ple_norm, ple_proj, ple_gate, s5_w_in, s5_a_re, s5_a_im, s5_log_dt, s5_b_re, s5_b_im, s5_c_re, s5_c_im, s5_d, s5_w_glu, sb_w_qkv, sb_w_o, final_norm, loss_target, m_ffn1_norm, m_ffn1_w1, m_ffn1_w3, m_ffn1_w2, m_mix_norm, m_ffn2_norm, m_ffn2_w1, m_ffn2_w3, m_ffn2_w2, m_ple_norm, m_ple_proj, m_ple_gate, m_s5_w_in, m_s5_a_re, m_s5_a_im, m_s5_log_dt, m_s5_b_re, m_s5_b_im, m_s5_c_re, m_s5_c_im, m_s5_d, m_s5_w_glu, m_sb_w_qkv, m_sb_w_o, m_final_norm, v_ffn1_norm, v_ffn1_w1, v_ffn1_w3, v_ffn1_w2, v_mix_norm, v_ffn2_norm, v_ffn2_w1, v_ffn2_w3, v_ffn2_w2, v_ple_norm, v_ple_proj, v_ple_gate, v_s5_w_in, v_s5_a_re, v_s5_a_im, v_s5_log_dt, v_s5_b_re, v_s5_b_im, v_s5_c_re, v_s5_c_im, v_s5_d, v_s5_w_glu, v_sb_w_qkv, v_sb_w_o, v_final_norm):
    args = dict(locals())
    w = {k: args[k] for k in ORDER}
    m = {k: args["m_" + k] for k in ORDER}
    v = {k: args["v_" + k] for k in ORDER}
    return _step(x, p, loss_target, w, m, v)
```

```python
import functools
import math

import jax
import jax.numpy as jnp
from jax import lax
from jax.experimental import pallas as pl
from jax.experimental.pallas import tpu as pltpu

F32 = jnp.float32
BF16 = jnp.bfloat16
MESH = pl.DeviceIdType.MESH

N_CHIPS = 4
N_DEV = 8
RMS_EPS = 1e-6
S5_GROUP = 16
S5_STATE = 64
S5_CHUNK = 16
SB_HEAD_DIM = 64
SB_BLOCK = 128
SB_CUT = -104.0
SB_UNROLL = 3
ADAM_LR, ADAM_B1, ADAM_B2, ADAM_EPS, ADAM_WD, ADAM_STEP = 0.001, 0.9, 0.999, 1e-08, 0.01, 10
VMEM_LIMIT = 56 * 1024 * 1024

NN = (((1,), (0,)), ((), ()))
NT = (((1,), (1,)), ((), ()))
TN = (((0,), (0,)), ((), ()))

ANY = pl.BlockSpec(memory_space=pl.ANY)


def _tile(n, target):
    if n <= target:
        return n
    for t in range(target - target % 8, 7, -8):
        if n % t == 0:
            return t
    raise ValueError(f"no row tile for {n}")


def _params(*semantics):
    return pltpu.CompilerParams(dimension_semantics=semantics, vmem_limit_bytes=VMEM_LIMIT)


def _sigmoid(v):
    return 1.0 / (1.0 + jnp.exp(-v))


def _gemm(name, grid, operands, in_specs, groups, acc_shapes, out_shapes, out_specs, epilogue, reduce_axis=None, aliases=None):
    n_in, n_out = len(operands), len(out_shapes)
    n_red = None if reduce_axis is None else grid[reduce_axis]

    def body(*refs):
        ins, outs, accs = refs[:n_in], refs[n_in:n_in + n_out], refs[n_in + n_out:]

        def products():
            res = []
            for terms in groups:
                tot = None
                for ia, ib, dims in terms:
                    d = lax.dot_general(ins[ia][...], ins[ib][...], dims, preferred_element_type=F32)
                    tot = d if tot is None else tot + d
                res.append(tot)
            return res

        def finish(vals):
            for o, v in zip(outs, epilogue(vals, ins)):
                o[...] = v.astype(o.dtype)

        if reduce_axis is None:
            finish(products())
        else:
            k = pl.program_id(reduce_axis)

            @pl.when(k == 0)
            def _():
                for a in accs:
                    a[...] = jnp.zeros_like(a)

            for a, d in zip(accs, products()):
                a[...] += d

            @pl.when(k == n_red - 1)
            def _():
                finish([a[...] for a in accs])

    scratch = [] if reduce_axis is None else [pltpu.VMEM(s, F32) for s in acc_shapes]
    sem = tuple("arbitrary" if i == reduce_axis else "parallel" for i in range(len(grid)))
    return pl.pallas_call(
        body, name=name, grid=grid, in_specs=in_specs, out_specs=out_specs, out_shape=out_shapes,
        scratch_shapes=scratch, input_output_aliases=aliases or {}, compiler_params=_params(*sem))(*operands)


def _ident(vals, ins):
    return vals


def _act_spec(layout, tm, cs, pos):
    if layout == "sm":
        return pl.BlockSpec((None, tm, cs), lambda *g: (pos(*g)[1], pos(*g)[0], 0))
    return pl.BlockSpec((tm, cs), lambda *g: pos(*g))


def _act_shape(layout, t, cs, dtype):
    return jax.ShapeDtypeStruct((N_CHIPS, t, cs) if layout == "sm" else (t, N_CHIPS * cs), dtype)


def _w_spec(w, layer, pos_k):
    _, _, r, c = w.shape
    return pl.BlockSpec((None, None, r, c), lambda *g: (pos_k(*g), layer, 0, 0))


def _mm_cs(name, x, w, layer, out_layout, out_dtype, tm=1024):
    t, kd = x.shape
    cs = w.shape[3]
    tm = _tile(t, tm)
    return _gemm(
        name, (N_CHIPS, t // tm), [x, w],
        [pl.BlockSpec((tm, kd), lambda k, i: (i, 0)), _w_spec(w, layer, lambda k, i: k)],
        [[(0, 1, NN)]], None, [_act_shape(out_layout, t, cs, out_dtype)],
        [_act_spec(out_layout, tm, cs, lambda k, i: (i, k))], _ident)[0]


def _mm_rs(name, xs, layout, w, layer, res=None, alpha=1.0, out_dtype=F32, tm=1024, gated=None):
    ks, n = w.shape[2], w.shape[3]
    t = xs.shape[1] if layout == "sm" else xs.shape[0]
    tm = _tile(t, tm)
    row = pl.BlockSpec((tm, n), lambda i: (i, 0))
    extras = [a for a in (res, gated) if a is not None]
    shards = range(N_CHIPS)
    operands = [xs] * N_CHIPS + [w] * N_CHIPS + extras
    specs = ([_act_spec(layout, tm, ks, lambda i, k=k: (i, k)) for k in shards]
             + [_w_spec(w, layer, lambda i, k=k: k) for k in shards] + [row] * len(extras))
    base = 2 * N_CHIPS

    def epilogue(vals, ins):
        y = alpha * vals[0]
        if gated is not None:
            return [y, ins[base][...] + ins[base + 1][...] * _sigmoid(y)]
        return [y if res is None else ins[base][...] + y]

    outs = _gemm(
        name, (t // tm,), operands, specs, [[(k, N_CHIPS + k, NN) for k in shards]], None,
        [jax.ShapeDtypeStruct((t, n), out_dtype)] * (1 if gated is None else 2), [row] * (1 if gated is None else 2),
        epilogue)
    return outs[0] if gated is None else outs


def _mm_cs_dx(name, pairs, layout, layer, tm=1024, transposed=False, norm=None):
    w0 = pairs[0][1]
    kd, cs = (w0.shape[3], w0.shape[2]) if transposed else (w0.shape[2], w0.shape[3])
    dy0 = pairs[0][0]
    t = dy0.shape[1] if layout == "sm" else dy0.shape[0]
    tm = _tile(t, tm // 2)
    operands, specs, terms = [], [], []
    for dy, w in pairs:
        for k in range(N_CHIPS):
            terms.append((len(operands), len(operands) + 1, NN if transposed else NT))
            operands += [dy, w]
            specs += [_act_spec(layout, tm, cs, lambda i, k=k: (i, k)), _w_spec(w, layer, lambda i, k=k: k)]
    row = pl.BlockSpec((tm, kd), lambda i: (i, 0))
    if norm is None:
        return _gemm(name, (t // tm,), operands, specs, [terms], None,
                     [jax.ShapeDtypeStruct((t, kd), F32)], [row], _ident)[0]
    base = len(operands)
    operands += list(norm)
    specs += [row, row, pl.BlockSpec(norm[2].shape, lambda i: (0, 0))]

    def epilogue(vals, ins):
        dx, dg = _rms_bwd_math(vals[0], ins[base + 1][...], ins[base + 2][...])
        dh = ins[base][...] + dx
        return [dh, dh, dg]

    dh, dhb, dg = _gemm(
        name, (t // tm,), operands, specs, [terms], None,
        [jax.ShapeDtypeStruct((t, kd), F32), jax.ShapeDtypeStruct((t, kd), BF16), jax.ShapeDtypeStruct((t // tm, 1, kd), F32)],
        [row, row, pl.BlockSpec((None, 1, kd), lambda i: (i, 0, 0))], epilogue)
    return dh, dhb, dg.sum(axis=0)


def _mm_rs_dx(name, dy, w, layer, out_layout, out_dtype, tm=1024):
    t, n = dy.shape
    ks = w.shape[2]
    tm = _tile(t, tm)
    return _gemm(
        name, (N_CHIPS, t // tm), [dy, w],
        [pl.BlockSpec((tm, n), lambda k, i: (i, 0)), _w_spec(w, layer, lambda k, i: k)],
        [[(0, 1, NT)]], None, [_act_shape(out_layout, t, ks, out_dtype)],
        [_act_spec(out_layout, tm, ks, lambda k, i: (i, k))], _ident)[0]


def _mm_dw(name, x, x_layout, dy, dy_layout, slot, alpha=1.0, tk=4096):
    stack, layer, layers = slot
    if x_layout is None:
        t, rows = x.shape
        cols = dy.shape[2] if dy_layout == "sm" else dy.shape[1] // N_CHIPS
        tk = _tile(t, tk)
        xspec = pl.BlockSpec((tk, rows), lambda k, j: (j, 0))
        yspec = _act_spec(dy_layout, tk, cols, lambda k, j: (j, k))
    else:
        t, cols = dy.shape
        rows = x.shape[2] if x_layout == "sm" else x.shape[1] // N_CHIPS
        tk = _tile(t, tk)
        xspec = _act_spec(x_layout, tk, rows, lambda k, j: (j, k))
        yspec = pl.BlockSpec((tk, cols), lambda k, j: (j, 0))
    operands, specs = [x, dy], [xspec, yspec]
    if stack is not None:
        operands.append(stack)
        specs.append(ANY)
    return _gemm(
        name, (N_CHIPS, t // tk), operands, specs, [[(0, 1, TN)]], [(rows, cols)],
        [jax.ShapeDtypeStruct((N_CHIPS, layers, rows, cols), F32)],
        [pl.BlockSpec((None, None, rows, cols), lambda k, j: (k, layer, 0, 0))],
        lambda vals, ins: [alpha * vals[0]], reduce_axis=1 if t // tk > 1 else None,
        aliases=None if stack is None else {2: 0})[0]


def _rows(name, fn, ins, outs, accs=(), tm=1024):
    t = ins[0].shape[0]
    tm = _tile(t, tm)
    n_in, n_out, n_acc = len(ins), len(outs), len(accs)
    in_specs = []
    for a in ins:
        if a.shape[0] == t:
            in_specs.append(pl.BlockSpec((tm, a.shape[1]), lambda i: (i, 0)))
        else:
            in_specs.append(pl.BlockSpec(a.shape, lambda i: (0, 0)))
    out_shape = [jax.ShapeDtypeStruct((t, c), d) for c, d in outs] + [jax.ShapeDtypeStruct(s, F32) for s in accs]
    out_specs = [pl.BlockSpec((tm, c), lambda i: (i, 0)) for c, _ in outs] + [pl.BlockSpec(s, lambda i: (0, 0)) for s in accs]

    def body(*refs):
        i = pl.program_id(0)
        row_vals, acc_vals = fn(*[r[...] for r in refs[:n_in]])
        for o, v in zip(refs[n_in:n_in + n_out], row_vals):
            o[...] = v.astype(o.dtype)
        acc_refs = refs[n_in + n_out:]
        if n_acc:
            @pl.when(i == 0)
            def _():
                for a in acc_refs:
                    a[...] = jnp.zeros_like(a)

            for a, v in zip(acc_refs, acc_vals):
                a[...] += v

    res = pl.pallas_call(
        body, name=name, grid=(t // tm,), in_specs=in_specs, out_specs=out_specs, out_shape=out_shape,
        compiler_params=_params("arbitrary" if n_acc else "parallel"))(*ins)
    return res[:n_out], res[n_out:]


def _to_bf16(a):
    def fn(x):
        return [x], []
    return _rows("weights_bf16", fn, [a.reshape(-1, a.shape[-1])], [(a.shape[-1], BF16)], tm=512)[0][0].reshape(a.shape)


def _rms_stats(x):
    return lax.rsqrt(jnp.mean(x * x, axis=-1, keepdims=True) + RMS_EPS)


def _rmsnorm(name, h, g):
    def fn(x, gg):
        return [x * _rms_stats(x) * gg], []
    return _rows(name, fn, [h, g], [(h.shape[1], BF16)])[0][0]


def _rms_bwd_math(dn, x, g):
    r = _rms_stats(x)
    xhat = x * r
    dxh = dn * g
    dx = r * (dxh - xhat * jnp.mean(dxh * xhat, axis=-1, keepdims=True))
    return dx, jnp.sum(dn * xhat, axis=0, keepdims=True)


def _rmsnorm_bwd(name, dres, dn, h, g):
    def fn(dr, d, x, gg):
        dx, dg = _rms_bwd_math(d, x, gg)
        return [dr + dx, dr + dx], [dg]
    (dh, dhb), (dg,) = _rows(name, fn, [dres, dn, h, g], [(h.shape[1], F32), (h.shape[1], BF16)], [(1, h.shape[1])])
    return dh, dhb, dg


def _ffn_fwd(h, g, w1, w3, w2, layer, tm=1024):
    t, d = h.shape
    fs = w1.shape[2]
    n = _rmsnorm("ffn_norm", h, g)
    tm = _tile(t, tm)

    def up(vals, ins):
        a, b = vals
        sg = _sigmoid(a)
        silu = a * sg
        return [b * sg * (1.0 + a * (1.0 - sg)), silu, silu * b]

    sm = _act_shape("sm", t, fs, BF16)
    osp = _act_spec("sm", tm, fs, lambda k, i: (i, k))
    ga, gb, s = _gemm(
        "ffn_up", (N_CHIPS, t // tm), [n, w1, w3],
        [pl.BlockSpec((tm, d), lambda k, i: (i, 0)), _w_spec(w1, layer, lambda k, i: k), _w_spec(w3, layer, lambda k, i: k)],
        [[(0, 1, NT)], [(0, 2, NT)]], None, [sm, sm, sm], [osp, osp, osp], up)
    out = _mm_rs("ffn_down", s, "sm", w2, layer, res=h, alpha=0.5)
    return out, (h, n, ga, gb, s)


def _ffn_bwd(dout, dob, saved, g, w1, w3, w2, layer, slots, tm=1024):
    h, n, ga, gb, s = saved
    t, d = h.shape
    fs = w1.shape[2]
    tm = _tile(t, tm)

    def down(vals, ins):
        ds = 0.5 * vals[0]
        return [ds * ins[2][...].astype(F32), ds * ins[3][...].astype(F32)]

    sm = _act_shape("sm", t, fs, BF16)
    asp = _act_spec("sm", tm, fs, lambda k, i: (i, k))
    da, db = _gemm(
        "ffn_down_dx", (N_CHIPS, t // tm), [dob, w2, ga, gb],
        [pl.BlockSpec((tm, d), lambda k, i: (i, 0)), _w_spec(w2, layer, lambda k, i: k), asp, asp],
        [[(0, 1, NT)]], None, [sm, sm], [asp, asp], down)
    dw2 = _mm_dw("ffn_dw2", s, "sm", dob, None, slots[2], alpha=0.5)
    dw1 = _mm_dw("ffn_dw1", da, "sm", n, None, slots[0])
    dw3 = _mm_dw("ffn_dw3", db, "sm", n, None, slots[1])
    dh, dhb, dg = _mm_cs_dx("ffn_up_dx", [(da, w1), (db, w3)], "sm", layer, transposed=True, norm=(dout, h, g))
    return dh, dhb, dg, dw1, dw3, dw2


def _ple_fwd(h, g, p2, wproj, wgate, layer):
    n = _rmsnorm("ple_norm", h, g)
    pp = _mm_cs("ple_proj", p2, wproj, layer, "flat", F32)
    gl, out = _mm_rs("ple_gate", n, "flat", wgate, layer, res=h, gated=pp)
    return out, (h, n, gl, pp)


def _ple_bwd(dout, saved, g, p2, wproj, wgate, layer, slots):
    h, n, gl, pp = saved
    d = h.shape[1]

    def fn(do, gg, q):
        sg = _sigmoid(gg)
        return [do * sg, do * q * sg * (1.0 - sg)], []
    (dpp, dgl), _ = _rows("ple_mix_bwd", fn, [dout, gl, pp], [(d, BF16), (d, BF16)])
    dwproj = _mm_dw("ple_dwproj", p2, None, dpp, "flat", slots[0])
    dwgate = _mm_dw("ple_dwgate", n, "flat", dgl, None, slots[1])
    dn = _mm_rs_dx("ple_gate_dx", dgl, wgate, layer, "flat", F32)
    dh, dhb, dg = _rmsnorm_bwd("ple_norm_bwd", dout, dn, h, g)
    return dh, dhb, dg, dwproj, dwgate


def _head(h, g, target):
    d = h.shape[1]

    def fn(x, gg, tg):
        y = x * _rms_stats(x) * gg
        err = y - tg
        dy = err * (1.0 / d)
        dx, dg = _rms_bwd_math(dy, x, gg)
        loss = 0.5 * jnp.sum(jnp.sum(err * err, axis=-1, keepdims=True) * (1.0 / d), axis=0, keepdims=True)
        return [dx], [dg, jnp.broadcast_to(loss, (1, 128))]
    (dh,), (dg, loss) = _rows("loss_head", fn, [h, g, target], [(d, F32)], [(1, d), (1, 128)])
    return loss[0, 0], dh, dg


S5_LANES = 2 * S5_STATE
S5_GB = 128 // S5_GROUP


def _s5_prep(a_re, a_im, log_dt, b_re, b_im, c_re, c_im):
    c, gb = S5_CHUNK, S5_GB
    g = a_re.shape[0]
    nb = g // gb
    lam_re = jnp.minimum(a_re, -1e-4)
    lam_im = a_im
    dt = jnp.exp(log_dt)[:, None, None]
    ks = jnp.arange(c + 1, dtype=F32)
    mag = jnp.exp(lam_re[..., None] * dt * ks)
    ph = lam_im[..., None] * dt * ks
    pw_re, pw_im = mag * jnp.cos(ph), mag * jnp.sin(ph)
    den = lam_re * lam_re + lam_im * lam_im
    nr, ni = pw_re[..., 1] - 1.0, pw_im[..., 1]
    fr = (nr * lam_re + ni * lam_im) / den
    fi = (ni * lam_re - nr * lam_im) / den
    bb_re = fr[..., None] * b_re - fi[..., None] * b_im
    bb_im = fr[..., None] * b_im + fi[..., None] * b_re
    ct_re, ct_im = c_re.transpose(0, 2, 1), c_im.transpose(0, 2, 1)
    ca_re = ct_re[:, :, None, :] * pw_re[..., None] - ct_im[:, :, None, :] * pw_im[..., None]
    ca_im = ct_re[:, :, None, :] * pw_im[..., None] + ct_im[:, :, None, :] * pw_re[..., None]
    hp = lax.Precision.HIGHEST
    kern = (jnp.einsum("gpj,gpkh->gkjh", bb_re, ca_re[:, :, :c], precision=hp)
            - jnp.einsum("gpj,gpkh->gkjh", bb_im, ca_im[:, :, :c], precision=hp))
    rev_re = pw_re[:, :, :c][:, :, ::-1].transpose(0, 2, 1)
    rev_im = pw_im[:, :, :c][:, :, ::-1].transpose(0, 2, 1)
    bt_re, bt_im = bb_re.transpose(0, 2, 1), bb_im.transpose(0, 2, 1)
    wn_re = rev_re[:, :, None, :] * bt_re[:, None] - rev_im[:, :, None, :] * bt_im[:, None]
    wn_im = rev_re[:, :, None, :] * bt_im[:, None] + rev_im[:, :, None, :] * bt_re[:, None]
    wn = jnp.concatenate([wn_re, wn_im], axis=-1)
    wo = jnp.concatenate([ca_re[:, :, 1:].transpose(0, 2, 3, 1), -ca_im[:, :, 1:].transpose(0, 2, 3, 1)], axis=-1)

    def blocks(x):
        return x.reshape(nb, gb, c, S5_GROUP, x.shape[3]).transpose(0, 2, 1, 3, 4).reshape(nb, c, gb * S5_GROUP, x.shape[3])

    ar, ai = pw_re[..., c], pw_im[..., c]
    return (jnp.tile(blocks(kern), (1, 1, 1, gb)), blocks(wn), blocks(wo),
            jnp.concatenate([ar, ar], axis=1), jnp.concatenate([-ai, ai], axis=1))


def _step_rows(ref, tau, n):
    return ref[pl.ds(tau, n, stride=S5_CHUNK), :].astype(BF16)


def _cat_groups(ref, dtype):
    return jnp.concatenate([ref[:, j, :] for j in range(S5_GB)], axis=1).astype(dtype)


def _cat_steps(ref, n):
    return jnp.concatenate([_step_rows(ref, tau, n) for tau in range(S5_CHUNK)], axis=1)


def _stack_steps(ref, n):
    return jnp.concatenate([_step_rows(ref, tau, n) for tau in range(S5_CHUNK)], axis=0)


def _cat_ops(ref, axis, reverse=False):
    order = range(S5_CHUNK - 1, -1, -1) if reverse else range(S5_CHUNK)
    return jnp.concatenate([ref[k] for k in order], axis=axis)


def _row_group(rows, lanes):
    row = (lax.broadcasted_iota(jnp.int32, (rows, lanes), 0) // S5_GROUP) % S5_GB
    lane = (lax.broadcasted_iota(jnp.int32, (rows, lanes), 1) // S5_GROUP) % S5_GB
    return row, lane


def _own_group(x):
    row, lane = _row_group(*x.shape)
    return jnp.where(row == lane, x, jnp.zeros_like(x))


def _spread(x):
    row, _ = _row_group(*x.shape)
    return jnp.concatenate([jnp.where(row == j, x, jnp.zeros_like(x)) for j in range(S5_GB)], axis=1)


def _gather_own(x):
    row, _ = _row_group(x.shape[0], S5_LANES)
    out = jnp.zeros((x.shape[0], S5_LANES), x.dtype)
    for j in range(S5_GB):
        out = out + jnp.where(row == j, x[:, j * S5_LANES:(j + 1) * S5_LANES], 0.0)
    return out


def _s5_specs(t, d):
    nct, g = t // S5_CHUNK, d // S5_GROUP
    tok = pl.BlockSpec((t, 128), lambda i: (0, i))
    st = pl.BlockSpec((nct, S5_GB, S5_LANES), lambda i: (0, i, 0))
    op = lambda w: pl.BlockSpec((None,) + w.shape[1:], lambda i: (i, 0, 0, 0))
    return nct, g, tok, st, op


def _s5_chunk_fwd(u, bd, bn):
    t, d = u.shape
    nct, g, tok, st, op = _s5_specs(t, d)
    c = S5_CHUNK

    def body(u_ref, bd_ref, bn_ref, y_ref, s_ref):
        ucat = _cat_steps(u_ref, nct)
        sloc = jnp.dot(ucat, _spread(_cat_ops(bn_ref, 0)), preferred_element_type=F32)
        for j in range(S5_GB):
            s_ref[:, j, :] = sloc[:, j * S5_LANES:(j + 1) * S5_LANES]
        lags = _own_group(_cat_ops(bd_ref, 0, reverse=True))
        for tt in range(c):
            y_ref[pl.ds(tt, nct, stride=c), :] = jnp.dot(ucat[:, :(tt + 1) * 128], lags[(c - 1 - tt) * 128:, :],
                                                         preferred_element_type=F32)

    return pl.pallas_call(
        body, name="s5_chunk", grid=(d // 128,), in_specs=[tok, op(bd), op(bn)], out_specs=[tok, st],
        out_shape=[jax.ShapeDtypeStruct((t, d), F32), jax.ShapeDtypeStruct((nct, g, S5_LANES), F32)],
        compiler_params=_params("parallel"))(u, bd, bn)


def _s5_state_out(sprev, co, yin):
    t, d = yin.shape
    nct, g, tok, st, op = _s5_specs(t, d)
    c = S5_CHUNK

    def body(s_ref, co_ref, yi_ref, y_ref):
        ys = lax.dot_general(_cat_groups(s_ref, BF16), _spread(_cat_ops(co_ref, 0)), NT,
                             preferred_element_type=F32)
        for tt in range(c):
            rows = pl.ds(tt, nct, stride=c)
            y_ref[rows, :] = yi_ref[rows, :] + ys[:, tt * 128:(tt + 1) * 128]

    return pl.pallas_call(
        body, name="s5_state_out", grid=(d // 128,), in_specs=[st, op(co), tok], out_specs=tok,
        out_shape=jax.ShapeDtypeStruct((t, d), F32), compiler_params=_params("parallel"))(sprev, co, yin)


def _s5_state_out_dx(dyb, co):
    t, d = dyb.shape
    nct, g, tok, st, op = _s5_specs(t, d)
    c = S5_CHUNK

    def body(dy_ref, co_ref, ds_ref):
        acc = jnp.dot(_cat_steps(dy_ref, nct), _spread(_cat_ops(co_ref, 0)), preferred_element_type=F32)
        for j in range(S5_GB):
            ds_ref[:, j, :] = acc[:, j * S5_LANES:(j + 1) * S5_LANES]

    return pl.pallas_call(
        body, name="s5_state_out_dx", grid=(d // 128,), in_specs=[tok, op(co)], out_specs=st,
        out_shape=jax.ShapeDtypeStruct((nct, g, S5_LANES), F32), compiler_params=_params("parallel"))(dyb, co)


def _s5_chunk_dx(dyb, dsloc, bd, bn, skip):
    t, d = dyb.shape
    nct, g, tok, st, op = _s5_specs(t, d)
    c = S5_CHUNK

    def body(dy_ref, ds_ref, bd_ref, bn_ref, sk_ref, du_ref):
        dus = lax.dot_general(_cat_groups(ds_ref, BF16), _spread(_cat_ops(bn_ref, 0)), NT, preferred_element_type=F32)
        dycat = _cat_steps(dy_ref, nct)
        lags = _own_group(_cat_ops(bd_ref, 1))
        for tau in range(c):
            rows = pl.ds(tau, nct, stride=c)
            du_ref[rows, :] = (sk_ref[rows, :] + dus[:, tau * 128:(tau + 1) * 128]
                               + lax.dot_general(dycat[:, tau * 128:], lags[:, :(c - tau) * 128], NT,
                                                 preferred_element_type=F32))

    return pl.pallas_call(
        body, name="s5_chunk_dx", grid=(d // 128,), in_specs=[tok, st, op(bd), op(bn), tok], out_specs=tok,
        out_shape=jax.ShapeDtypeStruct((t, d), F32), compiler_params=_params("parallel"))(dyb, dsloc, bd, bn, skip)


def _s5_chunk_dw(u, dyb, dsloc, bd, bn):
    t, d = u.shape
    nct, g, tok, st, op = _s5_specs(t, d)
    c = S5_CHUNK

    def body(u_ref, dy_ref, ds_ref, dbd_ref, dbn_ref):
        dbn = _gather_own(lax.dot_general(_cat_steps(u_ref, nct), _cat_groups(ds_ref, BF16), TN,
                                          preferred_element_type=F32))
        for tau in range(c):
            dbn_ref[tau] = dbn[tau * 128:(tau + 1) * 128, :]
        ustk, dystk = _stack_steps(u_ref, nct), _stack_steps(dy_ref, nct)
        for k in range(c):
            dbd_ref[k] = _own_group(lax.dot_general(ustk[:(c - k) * nct], dystk[k * nct:], TN,
                                                    preferred_element_type=F32))

    return pl.pallas_call(
        body, name="s5_chunk_dw", grid=(d // 128,), in_specs=[tok, tok, st], out_specs=[op(bd), op(bn)],
        out_shape=[jax.ShapeDtypeStruct(bd.shape, F32), jax.ShapeDtypeStruct(bn.shape, F32)],
        compiler_params=_params("parallel"))(u, dyb, dsloc)


def _s5_state_out_dw(sprev, dyb, co):
    t, d = dyb.shape
    nct, g, tok, st, op = _s5_specs(t, d)
    c = S5_CHUNK

    def body(s_ref, dy_ref, dco_ref):
        dco = _gather_own(lax.dot_general(_cat_steps(dy_ref, nct), _cat_groups(s_ref, BF16), TN,
                                          preferred_element_type=F32))
        for tt in range(c):
            dco_ref[tt] = dco[tt * 128:(tt + 1) * 128, :]

    return pl.pallas_call(
        body, name="s5_state_out_dw", grid=(d // 128,), in_specs=[st, tok], out_specs=op(co),
        out_shape=jax.ShapeDtypeStruct(co.shape, F32), compiler_params=_params("parallel"))(sprev, dyb)


def _s5_scan_fwd(sloc, m1, m2):
    bl, nc, g, w = sloc.shape

    def body(s_ref, m1_ref, m2_ref, o_ref):
        a1, a2 = m1_ref[...], m2_ref[...]

        def step(c, states):
            new = []
            for b, s in enumerate(states):
                o_ref[b, c] = s
                new.append(a1 * s + a2 * pltpu.roll(s, S5_STATE, 1) + s_ref[b, c])
            return tuple(new)
        lax.fori_loop(0, nc, step, tuple(jnp.zeros((g, w), F32) for _ in range(bl)))

    vm = pl.BlockSpec(memory_space=pltpu.VMEM)
    return pl.pallas_call(
        body, name="s5_scan", in_specs=[vm, vm, vm], out_specs=vm,
        out_shape=jax.ShapeDtypeStruct(sloc.shape, F32),
        compiler_params=pltpu.CompilerParams(vmem_limit_bytes=VMEM_LIMIT))(sloc, m1, m2)


def _s5_scan_bwd(dsprev, sprev, m1, m2):
    bl, nc, g, w = dsprev.shape

    def body(d_ref, s_ref, m1_ref, m2_ref, g_ref, p1_ref, p2_ref):
        a1, a2 = m1_ref[...], m2_ref[...]
        zero = jnp.zeros((g, w), F32)

        def step(i, carry):
            gps, p1, p2 = carry
            c = nc - 2 - i
            new = []
            for b, gp in enumerate(gps):
                g_ref[b, c] = gp
                sp = s_ref[b, c]
                p1 = p1 + gp * sp
                p2 = p2 + gp * pltpu.roll(sp, S5_STATE, 1)
                new.append(d_ref[b, c] + a1 * gp - a2 * pltpu.roll(gp, S5_STATE, 1))
            return tuple(new), p1, p2

        for b in range(bl):
            g_ref[b, nc - 1] = zero
        _, p1, p2 = lax.fori_loop(0, nc - 1, step, (tuple(d_ref[b, nc - 1] for b in range(bl)), zero, zero))
        p1_ref[...] = p1
        p2_ref[...] = p2

    vm = pl.BlockSpec(memory_space=pltpu.VMEM)
    sd = jax.ShapeDtypeStruct
    return pl.pallas_call(
        body, name="s5_scan_bwd", in_specs=[vm, vm, vm, vm], out_specs=[vm, vm, vm],
        out_shape=[sd(dsprev.shape, F32), sd((g, w), F32), sd((g, w), F32)],
        compiler_params=pltpu.CompilerParams(vmem_limit_bytes=VMEM_LIMIT))(dsprev, sprev, m1, m2)


def _gelu_tanh_parts(y):
    c0 = math.sqrt(2.0 / math.pi)
    inner = c0 * (y + 0.044715 * y * y * y)
    th = jnp.tanh(inner)
    return th, c0 * (1.0 + 3 * 0.044715 * y * y)


def _s5_fwd(h, g, ops, d_skip, w_in, w_glu, bl):
    bd, bn, co, m1, m2 = ops
    t, d = h.shape
    nct, groups = t // S5_CHUNK, d // S5_GROUP
    hn = _rmsnorm("mix_norm", h, g)
    u = _mm_rs("s5_in", hn, "flat", w_in, 0)
    yin, sloc = _s5_chunk_fwd(u, bd.astype(BF16), bn.astype(BF16))
    sprev = _s5_scan_fwd(sloc.reshape(bl, nct // bl, groups, S5_LANES), m1, m2).reshape(nct, groups, S5_LANES)
    y = _s5_state_out(sprev, co.astype(BF16), yin)

    def fn(yy, uu, dd):
        y2 = yy + dd * uu
        th, _ = _gelu_tanh_parts(y2)
        return [0.5 * y2 * (1.0 + th)], []
    z = _rows("s5_gelu", fn, [y, u, d_skip], [(d, BF16)])[0][0]
    zz = _mm_cs("s5_glu", z, w_glu, 0, "flat", F32)

    def glu(hh, zv):
        return [hh + zv[:, :d] * _sigmoid(zv[:, d:])], []
    out = _rows("s5_glu_mix", glu, [h, zz], [(d, F32)])[0][0]
    return out, (h, hn, u, sprev, y, z, zz)


def _s5_bwd(dout, saved, g, ops, d_skip, w_in, w_glu, bl):
    h, hn, u, sprev, y, z, zz = saved
    bd, bn, co, m1, m2 = ops
    t, d = h.shape
    nct, groups = t // S5_CHUNK, d // S5_GROUP

    def glu_bwd(do, zv):
        sg = _sigmoid(zv[:, d:])
        return [jnp.concatenate([do * sg, do * zv[:, :d] * sg * (1.0 - sg)], axis=1)], []
    dzz = _rows("s5_glu_bwd", glu_bwd, [dout, zz], [(2 * d, BF16)])[0][0]
    dwglu = _mm_dw("s5_dwglu", z, None, dzz, "flat", (None, 0, 1))
    dz = _mm_cs_dx("s5_glu_dx", [(dzz, w_glu)], "flat", 0)

    def gelu_bwd(dzv, yy, uu, dd):
        y2 = yy + dd * uu
        th, dinner = _gelu_tanh_parts(y2)
        dy2 = dzv * (0.5 * (1.0 + th) + 0.5 * y2 * (1.0 - th * th) * dinner)
        return [dy2, dy2 * dd], [jnp.sum(dy2 * uu, axis=0, keepdims=True)]
    (dyb, du_skip), (dd,) = _rows("s5_gelu_bwd", gelu_bwd, [dz, y, u, d_skip], [(d, F32), (d, F32)], [(1, d)])
    bd_b, bn_b, co_b = bd.astype(BF16), bn.astype(BF16), co.astype(BF16)
    dsprev = _s5_state_out_dx(dyb, co_b)
    shape4 = (bl, nct // bl, groups, S5_LANES)
    dsloc, dm1, dm2 = _s5_scan_bwd(dsprev.reshape(shape4), sprev.reshape(shape4), m1, m2)
    dsloc = dsloc.reshape(nct, groups, S5_LANES)
    du = _s5_chunk_dx(dyb, dsloc, bd_b, bn_b, du_skip).astype(BF16)
    dbd, dbn = _s5_chunk_dw(u, dyb, dsloc, bd, bn)
    dco = _s5_state_out_dw(sprev, dyb, co)
    dwin = _mm_dw("s5_dwin", hn, "flat", du, None, (None, 0, 1))
    dhn = _mm_rs_dx("s5_in_dx", du, w_in, 0, "flat", F32)
    dh, dhb, dg = _rmsnorm_bwd("mix_norm_bwd", dout, dhn, h, g)
    return dh, dhb, dg, dwin, dwglu, dd, (dbd, dbn, dco, dm1, dm2)


def _sb_block(qi, idx, tb):
    kb = qi - idx
    return pl.multiple_of(jnp.maximum(kb, 0) * tb, tb), idx == 0, kb >= 0


def _sb_scores(q, kblk, diag, exists, row, col):
    z = lax.dot_general(q, kblk, NT, preferred_element_type=F32) * (SB_HEAD_DIM ** -0.5)
    l1 = jnp.log(1.0 + jnp.exp(-jnp.abs(z)))
    ls = jnp.minimum(z, 0.0) - l1
    mask = jnp.logical_and(jnp.logical_or(col < row, jnp.logical_not(diag)), exists)
    lk = jnp.where(mask, ls - z, 0.0)
    return ls, lk, mask


def _split_dot(v, tri):
    hi = v.astype(BF16)
    lo = (v - hi.astype(F32)).astype(BF16)
    return (jnp.dot(hi, tri, preferred_element_type=F32) + jnp.dot(lo, tri, preferred_element_type=F32))


SB_PAIR =2 * SB_HEAD_DIM


def _pair_masks(tb):
    lane = lax.broadcasted_iota(jnp.int32, (1, SB_PAIR), 1)
    row = lax.broadcasted_iota(jnp.int32, (tb, tb), 0)
    col = lax.broadcasted_iota(jnp.int32, (tb, tb), 1)
    return [lane < SB_HEAD_DIM, lane >= SB_HEAD_DIM], row, col


def _pair_more(qi, carry):
    j, crs = carry[0], carry[2]
    return jnp.logical_and(j <= qi, jnp.maximum(jnp.max(crs[0]), jnp.max(crs[1])) > SB_CUT)


def _pair_specs(bl, l, d, tb):
    nq, off = l // tb, d // SB_PAIR
    qspec = pl.BlockSpec((tb, SB_PAIR), lambda b, p, i: (b * nq + i, p))
    kspec = pl.BlockSpec((l, SB_PAIR), lambda b, p, i: (b, off + p))
    vspec = pl.BlockSpec((l, SB_PAIR), lambda b, p, i: (b, 2 * off + p))
    return qspec, kspec, vspec


def _sb_attn_fwd2(qkv, bl):
    t, d3 = qkv.shape
    d, l = d3 // 3, t // bl
    tb = min(SB_BLOCK, l)
    nq = l // tb

    def body(q_ref, k_ref, v_ref, o_ref, ob_ref):
        qi = pl.program_id(2)
        heads, row, col = _pair_masks(tb)
        qv = q_ref[...]
        qh = [jnp.where(m, qv, jnp.zeros_like(qv)) for m in heads]
        tri = (row > col).astype(BF16)

        def step(carry):
            j, acc, crs = carry
            crs = list(crs)
            where = [_sb_block(qi, j + u, tb) for u in range(SB_UNROLL)]
            kblks = [k_ref[pl.ds(ks, tb), :] for ks, _, _ in where]
            scores = [[_sb_scores(qh[hd], kblks[u], where[u][1], where[u][2], row, col) for hd in range(2)]
                      for u in range(SB_UNROLL)]
            laters = [[_split_dot(sc[1], tri) for sc in su] for su in scores]
            for u in range(SB_UNROLL):
                vblk = v_ref[pl.ds(where[u][0], tb), :]
                outs = []
                for hd in range(2):
                    ls, lk, mask = scores[u][hd]
                    att = jnp.where(mask, jnp.exp(ls + laters[u][hd] + crs[hd]), 0.0)
                    outs.append(jnp.dot(att.astype(BF16), vblk, preferred_element_type=F32))
                    crs[hd] = crs[hd] + jnp.sum(lk, axis=1, keepdims=True)
                acc = acc + jnp.where(heads[0], outs[0], outs[1])
            return j + SB_UNROLL, acc, tuple(crs)

        zc = jnp.zeros((tb, 1), F32)
        _, acc, _ = lax.while_loop(functools.partial(_pair_more, qi), step,
                                   (jnp.int32(0), jnp.zeros((tb, SB_PAIR), F32), (zc, zc)))
        o_ref[...] = acc
        ob_ref[...] = acc.astype(BF16)

    qspec, kspec, vspec = _pair_specs(bl, l, d, tb)
    return pl.pallas_call(
        body, name="sb_attn", grid=(bl, d // SB_PAIR, nq), in_specs=[qspec, kspec, vspec], out_specs=[qspec, qspec],
        out_shape=[jax.ShapeDtypeStruct((t, d), F32), jax.ShapeDtypeStruct((t, d), BF16)],
        compiler_params=_params("parallel", "parallel", "parallel"))(qkv, qkv, qkv)


def _sb_attn_bwd2(qkv, o, do, bl):
    t, d3 = qkv.shape
    d, l = d3 // 3, t // bl
    tb = min(SB_BLOCK, l)
    nq = l // tb
    scale = SB_HEAD_DIM ** -0.5

    def body(q_ref, k_ref, v_ref, o_ref, do_ref, dq_ref, dk_ref, dv_ref, dk_acc, dv_acc):
        qi = pl.program_id(2)

        @pl.when(qi == 0)
        def _():
            dk_acc[...] = jnp.zeros_like(dk_acc)
            dv_acc[...] = jnp.zeros_like(dv_acc)

        heads, row, col = _pair_masks(tb)
        qv = q_ref[...]
        dov = do_ref[...].astype(BF16)
        qh = [jnp.where(m, qv, jnp.zeros_like(qv)) for m in heads]
        doh = [jnp.where(m, dov, jnp.zeros_like(dov)) for m in heads]
        ov = o_ref[...]
        dsum = [jnp.sum(dh.astype(F32) * ov, axis=1, keepdims=True) for dh in doh]
        tri = (row > col).astype(BF16)
        tri_inc = (row >= col).astype(BF16)

        def step(carry):
            j, dq, crs, ces = carry
            crs, ces = list(crs), list(ces)
            n = range(SB_UNROLL)
            where = [_sb_block(qi, j + u, tb) for u in n]
            rows = [pl.ds(ks, tb) for ks, _, _ in where]
            kblks = [k_ref[rows[u], :] for u in n]
            vblks = [v_ref[rows[u], :] for u in n]
            scores = [[_sb_scores(qh[hd], kblks[u], where[u][1], where[u][2], row, col) for hd in range(2)] for u in n]
            laters = [[_split_dot(sc[1], tri) for sc in su] for su in scores]
            datts = [[lax.dot_general(doh[hd], vblks[u], NT, preferred_element_type=F32) for hd in range(2)] for u in n]
            atts = [[None, None] for _ in n]
            for u in n:
                for hd in range(2):
                    ls, lk, mask = scores[u][hd]
                    atts[u][hd] = jnp.where(mask, jnp.exp(ls + laters[u][hd] + crs[hd]), 0.0).astype(BF16)
                    crs[hd] = crs[hd] + jnp.sum(lk, axis=1, keepdims=True)
            es = [[atts[u][hd].astype(F32) * datts[u][hd] for hd in range(2)] for u in n]
            sufs = [[_split_dot(e, tri_inc) for e in eu] for eu in es]
            dzs = [[None, None] for _ in n]
            for u in n:
                for hd in range(2):
                    ls, _, mask = scores[u][hd]
                    pre = dsum[hd] - ces[hd] - sufs[u][hd]
                    sg = jnp.exp(ls)
                    dzs[u][hd] = (jnp.where(mask, es[u][hd] * (1.0 - sg) - pre * sg, 0.0) * scale).astype(BF16)
                    ces[hd] = ces[hd] + jnp.sum(es[u][hd], axis=1, keepdims=True)
            for u in n:
                dq = dq + jnp.where(heads[0], jnp.dot(dzs[u][0], kblks[u], preferred_element_type=F32),
                                    jnp.dot(dzs[u][1], kblks[u], preferred_element_type=F32))
                dk_acc[rows[u], :] += (lax.dot_general(dzs[u][0], qh[0], TN, preferred_element_type=F32)
                                       + lax.dot_general(dzs[u][1], qh[1], TN, preferred_element_type=F32))
                dv_acc[rows[u], :] += (lax.dot_general(atts[u][0], doh[0], TN, preferred_element_type=F32)
                                       + lax.dot_general(atts[u][1], doh[1], TN, preferred_element_type=F32))
            return j + SB_UNROLL, dq, tuple(crs), tuple(ces)

        zc = jnp.zeros((tb, 1), F32)
        _, dq, _, _ = lax.while_loop(functools.partial(_pair_more, qi), step,
                                     (jnp.int32(0), jnp.zeros((tb, SB_PAIR), F32), (zc, zc), (zc, zc)))
        dq_ref[...] = dq.astype(BF16)

        @pl.when(qi == nq - 1)
        def _():
            dk_ref[...] = dk_acc[...].astype(BF16)
            dv_ref[...] = dv_acc[...].astype(BF16)

    qspec, kspec, vspec = _pair_specs(bl, l, d, tb)
    blk = pl.BlockSpec((tb, SB_PAIR), lambda b, p, i: (b * nq + i, p))
    full = pl.BlockSpec((l, SB_PAIR), lambda b, p, i: (b, p))
    sd = jax.ShapeDtypeStruct((t, d), BF16)
    dq, dk, dv = pl.pallas_call(
        body, name="sb_attn_bwd", grid=(bl, d // SB_PAIR, nq), in_specs=[qspec, kspec, vspec, blk, blk],
        out_specs=[blk, full, full], out_shape=[sd, sd, sd],
        scratch_shapes=[pltpu.VMEM((l, SB_PAIR), F32), pltpu.VMEM((l, SB_PAIR), F32)],
        compiler_params=_params("parallel", "parallel", "arbitrary"))(qkv, qkv, qkv, o, do)
    return jnp.concatenate([dq, dk, dv], axis=1)


def _sb_fwd(h, g, w_qkv, w_o, bl):
    t, d = h.shape
    hn = _rmsnorm("mix_norm", h, g)
    qkv = _mm_cs("sb_qkv", hn, w_qkv, 0, "flat", BF16)
    o, ob = _sb_attn_fwd2(qkv, bl)
    out = _mm_rs("sb_out", ob, "flat", w_o, 0, res=h)
    return out, (h, hn, qkv, o, ob)


def _sb_bwd(dout, dob, saved, g, w_qkv, w_o, bl):
    h, hn, qkv, o, ob = saved
    dwo = _mm_dw("sb_dwo", ob, "flat", dob, None, (None, 0, 1))
    do = _mm_rs_dx("sb_out_dx", dob, w_o, 0, "flat", F32)
    dqkv = _sb_attn_bwd2(qkv, o, do, bl)
    dwqkv = _mm_dw("sb_dwqkv", hn, None, dqkv, "flat", (None, 0, 1))
    dh, dhb, dg = _mm_cs_dx("sb_qkv_dx", [(dqkv, w_qkv)], "flat", 0, norm=(dout, h, g))
    return dh, dhb, dg, dwqkv, dwo


def _adamw_update(wv, gr, mv, vv):
    c1 = 1.0 / (1.0 - ADAM_B1 ** ADAM_STEP)
    c2 = 1.0 / (1.0 - ADAM_B2 ** ADAM_STEP)
    mn = ADAM_B1 * mv + (1.0 - ADAM_B1) * gr
    vn = ADAM_B2 * vv + (1.0 - ADAM_B2) * gr * gr
    delta = -ADAM_LR * ((mn * c1) / (jnp.sqrt(vn * c2) + ADAM_EPS) + ADAM_WD * wv)
    return delta, mn, vn


def _adamw_small(w, gr, m, v):
    def fn(wv, gv, mv, vv):
        return list(_adamw_update(wv, gv, mv, vv)), []
    return _rows("adamw_small", fn, [w, gr, m, v], [(w.shape[1], F32)] * 3)[0]


def _place():
    x, y, c = lax.axis_index("x"), lax.axis_index("y"), lax.axis_index("c")
    chips = [(1 - x, y), (x, 1 - y), (1 - x, 1 - y)]
    return x, y, c, chips


def _remote(src, dst, send_sem, recv_sem, to):
    return pltpu.make_async_remote_copy(src_ref=src, dst_ref=dst, send_sem=send_sem, recv_sem=recv_sem,
                                        device_id=to, device_id_type=MESH)


def _half(ref, c, rh, lead):
    return ref.at[(slice(None),) * lead + (pl.ds(c * rh, rh),)]


def _allgather_weights(ws):
    n = len(ws)

    def body(*refs):
        ins, outs = refs[:n], refs[n:2 * n]
        send, recv = refs[2 * n:]
        x, y, c, _ = _place()
        chip_x, chip_y, chip_d = (1 - x, y), (x, 1 - y), (1 - x, 1 - y)
        sibling = (x, y, 1 - c)
        index = lambda chip: 2 * chip[0] + chip[1]
        sent = []

        def quarter(ref, half, q, rq):
            return ref.at[:, pl.ds((2 * half + q) * rq, rq)]

        def copy(t, kind, src, dst, to):
            return _remote(src, dst, send.at[t, kind], recv.at[t, kind], to)

        def start(cp):
            cp.start()
            sent.append(cp)

        for t in range(n):
            rq = ws[t].shape[1] // 4
            for q in range(2):
                for base, chip in ((0, chip_x), (2, chip_y)):
                    start(copy(t, base + q, quarter(ins[t], c, q, rq), quarter(outs[t].at[index((x, y))], c, q, rq), (*chip, c)))
        for t in range(n):
            rq = ws[t].shape[1] // 4
            landings = [(chip_x, 0, 0, chip_x, ((4, chip_y), (6, None))), (chip_y, 1, 3, chip_y, ((5, chip_x), (9, None))),
                        (chip_x, 1, 1, chip_x, ((7, None),)), (chip_y, 0, 2, chip_y, ((8, None),)),
                        (chip_d, 0, 4, chip_y, ((10, None),)), (chip_d, 1, 5, chip_x, ((11, None),))]
            for origin, q, kind, sender, onward in landings:
                piece = quarter(outs[t].at[index(origin)], c, q, rq)
                copy(t, kind, piece, piece, (*sender, c)).wait_recv()
                for kind2, chip in onward:
                    start(copy(t, kind2, piece, piece, sibling if chip is None else (*chip, c)))
        for t in range(n):
            rq = ws[t].shape[1] // 4
            for kind, (origin, q) in zip(range(6, 12), ((chip_x, 0), (chip_x, 1), (chip_y, 0), (chip_y, 1), (chip_d, 0), (chip_d, 1))):
                piece = quarter(outs[t].at[index(origin)], 1 - c, q, rq)
                copy(t, kind, piece, piece, sibling).wait_recv()
        for cp in sent:
            cp.wait_send()

    res = pl.pallas_call(
        body, name="allgather_weights", in_specs=[ANY] * n, out_specs=[ANY] * n,
        out_shape=[jax.ShapeDtypeStruct((N_CHIPS,) + w.shape, w.dtype) for w in ws],
        scratch_shapes=[pltpu.SemaphoreType.DMA((n, 12)), pltpu.SemaphoreType.DMA((n, 12))],
    )(*ws)
    own = 2 * lax.axis_index("x") + lax.axis_index("y")
    return [lax.dynamic_update_slice(g, w[None], (own, 0, 0, 0)) for g, w in zip(res, ws)]


def _sibling_half(g, c_idx):
    n4, ly, r, cc = g.shape
    rh = r // 2
    tm = _tile(rh, 512)
    nt = rh // tm

    def body(c_ref, g_ref, o_ref):
        o_ref[...] = g_ref[...].astype(o_ref.dtype)

    blk = (None, tm, cc)
    grid_spec = pltpu.PrefetchScalarGridSpec(
        num_scalar_prefetch=1, grid=(n4 * ly, nt),
        in_specs=[pl.BlockSpec(blk, lambda a, i, cr: (a, (1 - cr[0]) * nt + i, 0))],
        out_specs=pl.BlockSpec(blk, lambda a, i, cr: (a, i, 0)))
    out = pl.pallas_call(
        body, name="grad_sibling_half", grid_spec=grid_spec, out_shape=jax.ShapeDtypeStruct((n4 * ly, rh, cc), BF16),
        compiler_params=_params("parallel", "parallel"))(c_idx, g.reshape(n4 * ly, r, cc))
    return out.reshape(n4, ly, rh, cc)


def _pair_exchange(hs):
    n = len(hs)

    def body(*refs):
        ins, outs = refs[:n], refs[n:2 * n]
        send, recv = refs[2 * n:]
        x, y, c, _ = _place()
        copies = [_remote(ins[t], outs[t], send.at[t], recv.at[t], (x, y, 1 - c)) for t in range(n)]
        for cp in copies:
            cp.start()
        for cp in copies:
            cp.wait()

    return pl.pallas_call(
        body, name="grad_pair_exchange", in_specs=[ANY] * n, out_specs=[ANY] * n,
        out_shape=[jax.ShapeDtypeStruct(h.shape, h.dtype) for h in hs],
        scratch_shapes=[pltpu.SemaphoreType.DMA((n,)), pltpu.SemaphoreType.DMA((n,))],
    )(*hs)


def _pair_sum(g, theirs, c_idx):
    n4, ly, r, cc = g.shape
    rh = r // 2
    tm = _tile(rh, 512)
    nt = rh // tm

    def body(c_ref, g_ref, t_ref, o_ref):
        o_ref[...] = (g_ref[...] + t_ref[...].astype(F32)).astype(o_ref.dtype)

    blk = (None, tm, cc)
    grid_spec = pltpu.PrefetchScalarGridSpec(
        num_scalar_prefetch=1, grid=(n4 * ly, nt),
        in_specs=[pl.BlockSpec(blk, lambda a, i, cr: (a, cr[0] * nt + i, 0)), pl.BlockSpec(blk, lambda a, i, cr: (a, i, 0))],
        out_specs=pl.BlockSpec(blk, lambda a, i, cr: (a, i, 0)))
    out = pl.pallas_call(
        body, name="grad_pair_sum", grid_spec=grid_spec, out_shape=jax.ShapeDtypeStruct((n4 * ly, rh, cc), BF16),
        compiler_params=_params("parallel", "parallel"))(c_idx, g.reshape(n4 * ly, r, cc), theirs.reshape(n4 * ly, rh, cc))
    return out.reshape(n4, ly, rh, cc)


def _quarter(ref, q, rq):
    return ref.at[:, pl.ds(q * rq, rq)]


def _chip_exchange_first(ps):
    n = len(ps)

    def body(*refs):
        ins, outs = refs[:n], refs[n:2 * n]
        send, recv = refs[2 * n:]
        x, y, c, _ = _place()
        index = lambda cx, cy: 2 * cx + cy
        copies = []
        for t in range(n):
            rq = ps[t].shape[2] // 2
            for base, q, chip in ((0, 0, (1 - x, y)), (2, 1, (x, 1 - y))):
                for j, slice_of in enumerate((chip, (1 - x, 1 - y))):
                    copies.append(_remote(_quarter(ins[t].at[index(*slice_of)], q, rq), outs[t].at[base + j],
                                          send.at[t, base + j], recv.at[t, base + j], (*chip, c)))
        for cp in copies:
            cp.start()
        for cp in copies:
            cp.wait()

    return pl.pallas_call(
        body, name="grad_chip_exchange", in_specs=[ANY] * n, out_specs=[ANY] * n,
        out_shape=[jax.ShapeDtypeStruct((4, p.shape[1], p.shape[2] // 2, p.shape[3]), p.dtype) for p in ps],
        scratch_shapes=[pltpu.SemaphoreType.DMA((n, 4)), pltpu.SemaphoreType.DMA((n, 4))],
    )(*ps)


def _chip_relay_sum(p, first, where):
    _, ly, rh, cc = p.shape
    rq = rh // 2
    tm = _tile(rq, 512)
    nt = rq // tm

    def body(w_ref, p_ref, f_ref, out_ref):
        out_ref[...] = (p_ref[...].astype(F32) + f_ref[...].astype(F32)).astype(out_ref.dtype)

    blk = (None, None, tm, cc)
    grid_spec = pltpu.PrefetchScalarGridSpec(
        num_scalar_prefetch=1, grid=(2, ly, nt),
        in_specs=[pl.BlockSpec(blk, lambda s, l, i, w: (w[2 - s], l, s * nt + i, 0)),
                  pl.BlockSpec(blk, lambda s, l, i, w: (1 + 2 * s, l, i, 0))],
        out_specs=pl.BlockSpec(blk, lambda s, l, i, w: (s, l, i, 0)))
    return pl.pallas_call(
        body, name="grad_relay_sum", grid_spec=grid_spec, out_shape=jax.ShapeDtypeStruct((2, ly, rq, cc), p.dtype),
        compiler_params=_params("parallel", "parallel", "parallel"))(where, p, first)


def _chip_exchange_second(ss):
    n = len(ss)

    def body(*refs):
        ins, outs = refs[:n], refs[n:2 * n]
        send, recv = refs[2 * n:]
        x, y, c, _ = _place()
        copies = []
        for t in range(n):
            for j, chip in enumerate(((x, 1 - y), (1 - x, y))):
                copies.append(_remote(ins[t].at[j], outs[t].at[j], send.at[t, j], recv.at[t, j], (*chip, c)))
        for cp in copies:
            cp.start()
        for cp in copies:
            cp.wait()

    return pl.pallas_call(
        body, name="grad_chip_exchange_2", in_specs=[ANY] * n, out_specs=[ANY] * n,
        out_shape=[jax.ShapeDtypeStruct(s.shape, s.dtype) for s in ss],
        scratch_shapes=[pltpu.SemaphoreType.DMA((n, 2)), pltpu.SemaphoreType.DMA((n, 2))],
    )(*ss)


def _chip_sum(p, first, second, where):
    _, ly, rh, cc = p.shape
    rq = rh // 2
    tm = _tile(rq, 512)
    nt = rq // tm

    def body(w_ref, p_ref, f_ref, s_ref, out_ref):
        out_ref[...] = (p_ref[...].astype(F32) + f_ref[...].astype(F32)) + s_ref[...].astype(F32)

    blk = (None, None, tm, cc)
    grid_spec = pltpu.PrefetchScalarGridSpec(
        num_scalar_prefetch=1, grid=(ly, 2, nt),
        in_specs=[pl.BlockSpec(blk, lambda l, q, i, w: (w[0], l, q * nt + i, 0)),
                  pl.BlockSpec(blk, lambda l, q, i, w: (2 * q, l, i, 0)),
                  pl.BlockSpec(blk, lambda l, q, i, w: (q, l, i, 0))],
        out_specs=pl.BlockSpec((None, tm, cc), lambda l, q, i, w: (l, q * nt + i, 0)))
    return pl.pallas_call(
        body, name="grad_chip_sum", grid_spec=grid_spec, out_shape=jax.ShapeDtypeStruct((ly, rh, cc), F32),
        compiler_params=_params("parallel", "parallel", "parallel"))(where, p, first, second)


def _pair_swap(halves):
    n = len(halves)

    def body(*refs):
        ins, outs = refs[:n], refs[n:2 * n]
        send, recv = refs[2 * n:]
        x, y, c, _ = _place()
        copies = [_remote(ins[t], outs[t], send.at[t], recv.at[t], (x, y, 1 - c)) for t in range(n)]
        for cp in copies:
            cp.start()
        for cp in copies:
            cp.wait()

    return pl.pallas_call(
        body, name="grad_pair_swap", in_specs=[ANY] * n, out_specs=[ANY] * n,
        out_shape=[jax.ShapeDtypeStruct(h.shape, F32) for h in halves],
        scratch_shapes=[pltpu.SemaphoreType.DMA((n,)), pltpu.SemaphoreType.DMA((n,))],
    )(*halves)


def _adamw_big(w, m, v, mine, theirs, c_idx):
    ly, r, cc = w.shape
    rh = r // 2
    tm = _tile(rh, 512)
    nt = rh // tm

    def body(c_ref, w_ref, m_ref, v_ref, a_ref, b_ref, g_out, d_out, m_out, v_out):
        gr = jnp.where(pl.program_id(1) == c_ref[0], a_ref[...], b_ref[...])
        delta, mn, vn = _adamw_update(w_ref[...], gr, m_ref[...], v_ref[...])
        g_out[...] = gr
        d_out[...] = delta
        m_out[...] = mn
        v_out[...] = vn

    blk = (None, tm, cc)
    full = pl.BlockSpec(blk, lambda l, hc, i, cr: (l, hc * nt + i, 0))
    half = pl.BlockSpec(blk, lambda l, hc, i, cr: (l, i, 0))
    grid_spec = pltpu.PrefetchScalarGridSpec(
        num_scalar_prefetch=1, grid=(ly, 2, nt), in_specs=[full, full, full, half, half], out_specs=[full] * 4)
    sd = jax.ShapeDtypeStruct(w.shape, F32)
    return pl.pallas_call(
        body, name="adamw", grid_spec=grid_spec, out_shape=[sd] * 4,
        compiler_params=_params("parallel", "parallel", "parallel"))(c_idx, w, m, v, mine, theirs)


def _allreduce_small(v):
    rows, w = v.shape

    def body(x_ref, sum_ref, all_ref, send, recv, local):
        x, y, c, chips = _place()
        me, sibling = (x, y, c), (x, y, 1 - c)

        def slot(px, py, pc):
            return all_ref.at[4 * px + 2 * py + pc]

        def copy(k, block, to, src=None):
            return _remote(slot(*block) if src is None else src, slot(*block), send.at[k], recv.at[k], to)

        mine = pltpu.make_async_copy(x_ref, slot(*me), local)
        mine.start()
        first = [copy(0, me, sibling, src=x_ref)]
        first += [copy(1 + j, me, (*chip, c), src=x_ref) for j, chip in enumerate(chips)]
        for cp in first:
            cp.start()
        passed = [copy(4 + j, (*chip, c), sibling) for j, chip in enumerate(chips)]
        for j, chip in enumerate(chips):
            copy(1 + j, (*chip, c), me).wait_recv()
            passed[j].start()
        copy(0, sibling, me).wait_recv()
        for j, chip in enumerate(chips):
            copy(4 + j, (*chip, 1 - c), me).wait_recv()
        for cp in first + passed:
            cp.wait_send()
        mine.wait()
        tot = all_ref[0]
        for k in range(1, N_DEV):
            tot = tot + all_ref[k]
        sum_ref[...] = tot

    vm = pl.BlockSpec(memory_space=pltpu.VMEM)
    return pl.pallas_call(
        body, name="allreduce_small", in_specs=[vm], out_specs=[vm, vm],
        out_shape=[jax.ShapeDtypeStruct((rows, w), F32), jax.ShapeDtypeStruct((N_DEV, rows, w), F32)],
        scratch_shapes=[pltpu.SemaphoreType.DMA((7,)), pltpu.SemaphoreType.DMA((7,)), pltpu.SemaphoreType.DMA],
        compiler_params=pltpu.CompilerParams(vmem_limit_bytes=VMEM_LIMIT),
    )(v)[0]


BIG = ["ffn1_w1", "ffn1_w3", "ffn1_w2", "ffn2_w1", "ffn2_w3", "ffn2_w2", "ple_proj", "ple_gate",
       "s5_w_in", "s5_w_glu", "sb_w_qkv", "sb_w_o"]
TRANSPOSED = ("ffn1_w1", "ffn1_w3", "ffn2_w1", "ffn2_w3")
SMALL = ["ffn1_norm", "mix_norm", "ffn2_norm", "ple_norm", "s5_a_re", "s5_a_im", "s5_log_dt", "s5_b_re", "s5_b_im",
         "s5_c_re", "s5_c_im", "s5_d", "final_norm"]
ORDER = ["ffn1_norm", "ffn1_w1", "ffn1_w3", "ffn1_w2", "mix_norm", "ffn2_norm", "ffn2_w1", "ffn2_w3", "ffn2_w2",
         "ple_norm", "ple_proj", "ple_gate", "s5_w_in", "s5_a_re", "s5_a_im", "s5_log_dt", "s5_b_re", "s5_b_im",
         "s5_c_re", "s5_c_im", "s5_d", "s5_w_glu", "sb_w_qkv", "sb_w_o", "final_norm"]


def _pack(arrays):
    flat = jnp.concatenate([a.reshape(-1) for a in arrays])
    pad = (-flat.shape[0]) % 1024
    return jnp.pad(flat, (0, pad)).reshape(-1, 128)


def _unpack(packed, like):
    flat = packed.reshape(-1)
    out, off = [], 0
    for a in like:
        out.append(flat[off:off + a.size].reshape(a.shape))
        off += a.size
    return out


def _fwd_bwd(x, p, target, w, gathered):
    bl, l, d = x.shape
    t = bl * l
    depth = w["ffn1_norm"].shape[0]
    s5_ops, s5_vjp = jax.vjp(_s5_prep, w["s5_a_re"][0], w["s5_a_im"][0], w["s5_log_dt"][0], w["s5_b_re"][0],
                             w["s5_b_im"][0], w["s5_c_re"][0], w["s5_c_im"][0])

    h = x.reshape(t, d)
    p2 = [p[i].reshape(t, p.shape[-1]).astype(BF16) for i in range(depth)]
    saved = []
    for i in range(depth):
        norm = lambda name: w[name][i:i + 1]
        h, s1 = _ffn_fwd(h, norm("ffn1_norm"), gathered["ffn1_w1"], gathered["ffn1_w3"], gathered["ffn1_w2"], i)
        if i % 2 == 0:
            h, s2 = _s5_fwd(h, norm("mix_norm"), s5_ops, w["s5_d"][i // 2:i // 2 + 1], gathered["s5_w_in"], gathered["s5_w_glu"], bl)
        else:
            h, s2 = _sb_fwd(h, norm("mix_norm"), gathered["sb_w_qkv"], gathered["sb_w_o"], bl)
        h, s3 = _ffn_fwd(h, norm("ffn2_norm"), gathered["ffn2_w1"], gathered["ffn2_w3"], gathered["ffn2_w2"], i)
        h, s4 = _ple_fwd(h, norm("ple_norm"), p2[i], gathered["ple_proj"], gathered["ple_gate"], i)
        saved.append((s1, s2, s3, s4))

    loss, dh, dfinal = _head(h, w["final_norm"].reshape(1, d), target.reshape(t, d))

    big = {k: None for k in BIG}
    small = {k: [None] * w[k].shape[0] if w[k].ndim > 1 else None for k in SMALL}
    small["final_norm"] = dfinal.reshape(d)
    for i in reversed(range(depth)):
        norm = lambda name: w[name][i:i + 1]
        slots = lambda *names: [(big[k], i, depth) for k in names]
        s1, s2, s3, s4 = saved[i]
        dh, dhb, dg, big["ple_proj"], big["ple_gate"] = _ple_bwd(
            dh, s4, norm("ple_norm"), p2[i], gathered["ple_proj"], gathered["ple_gate"], i, slots("ple_proj", "ple_gate"))
        small["ple_norm"][i] = dg[0]
        dh, dhb, dg, big["ffn2_w1"], big["ffn2_w3"], big["ffn2_w2"] = _ffn_bwd(
            dh, dhb, s3, norm("ffn2_norm"), gathered["ffn2_w1"], gathered["ffn2_w3"], gathered["ffn2_w2"], i,
            slots("ffn2_w1", "ffn2_w3", "ffn2_w2"))
        small["ffn2_norm"][i] = dg[0]
        if i % 2 == 0:
            dh, dhb, dg, big["s5_w_in"], big["s5_w_glu"], dd, dops = _s5_bwd(
                dh, s2, norm("mix_norm"), s5_ops, w["s5_d"][i // 2:i // 2 + 1], gathered["s5_w_in"], gathered["s5_w_glu"], bl)
            small["s5_d"][0] = dd[0]
            raw = s5_vjp(dops)
            for name, gr in zip(["s5_a_re", "s5_a_im", "s5_log_dt", "s5_b_re", "s5_b_im", "s5_c_re", "s5_c_im"], raw):
                small[name][0] = gr
        else:
            dh, dhb, dg, big["sb_w_qkv"], big["sb_w_o"] = _sb_bwd(
                dh, dhb, s2, norm("mix_norm"), gathered["sb_w_qkv"], gathered["sb_w_o"], bl)
        small["mix_norm"][i] = dg[0]
        dh, dhb, dg, big["ffn1_w1"], big["ffn1_w3"], big["ffn1_w2"] = _ffn_bwd(
            dh, dhb, s1, norm("ffn1_norm"), gathered["ffn1_w1"], gathered["ffn1_w3"], gathered["ffn1_w2"], i,
            slots("ffn1_w1", "ffn1_w3", "ffn1_w2"))
        small["ffn1_norm"][i] = dg[0]
    small_list = [jnp.stack(small[k]) if isinstance(small[k], list) else small[k] for k in SMALL]
    return loss, dh.reshape(bl, l, d), big, small_list


def _step(x, p, target, w, m, v):
    flip = lambda tree: {k: jnp.swapaxes(a, 1, 2) if k in TRANSPOSED else a for k, a in tree.items()}
    w, m, v = flip(w), flip(m), flip(v)
    gathered = dict(zip(BIG, _allgather_weights([_to_bf16(w[k]) for k in BIG])))
    loss, grad_x, big, small_list = _fwd_bwd(x, p, target, w, gathered)

    c_idx = lax.axis_index("c").astype(jnp.int32).reshape(1)
    cx, cy = lax.axis_index("x"), lax.axis_index("y")
    where = jnp.stack([2 * cx + cy, 2 * (1 - cx) + cy, 2 * cx + (1 - cy)]).astype(jnp.int32)
    partial = [big[k] for k in BIG]
    pair = [_pair_sum(g, t, c_idx) for g, t in zip(partial, _pair_exchange([_sibling_half(g, c_idx) for g in partial]))]
    first = _chip_exchange_first(pair)
    second = _chip_exchange_second([_chip_relay_sum(pr, f, where) for pr, f in zip(pair, first)])
    mine = [_chip_sum(pr, f, s, where) for pr, f, s in zip(pair, first, second)]
    theirs = _pair_swap(mine)
    out_g, out_d, out_m, out_v = {}, {}, {}, {}
    for k, a, b in zip(BIG, mine, theirs):
        out_g[k], out_d[k], out_m[k], out_v[k] = _adamw_big(w[k], m[k], v[k], a, b, c_idx)

    like = [w[k] for k in SMALL]
    pad = [jnp.zeros((1,), F32)]
    g_small = _allreduce_small(_pack(small_list + [loss.reshape(1)]))
    packed = (g_small,) + tuple(_adamw_small(_pack(like + pad), g_small, _pack([m[k] for k in SMALL] + pad),
                                             _pack([v[k] for k in SMALL] + pad)))
    for dst, pk in zip((out_g, out_d, out_m, out_v), packed):
        dst.update(dict(zip(SMALL, _unpack(pk, like))))
    loss = g_small.reshape(-1)[sum(a.size for a in like)]
    out_g, out_d, out_m, out_v = flip(out_g), flip(out_d), flip(out_m), flip(out_v)
    return (loss, grad_x, *[out_g[k] for k in ORDER], *[out_d[k] for k in ORDER],
            *[out_m[k] for k in ORDER], *[out_v[k] for k in ORDER])


def kernel(x, p, ffn1_norm, ffn1_w1, ffn1_w3, ffn1_w2, mix_norm, ffn2_norm, ffn2_w1, ffn2_w3, ffn2_w2, ple_norm, ple_proj, ple_gate, s5_w_in, s5_a_re, s5_a_im, s5_log_dt, s5_b_re, s5_b_im, s5_c_re, s5_c_im, s5_d, s5_w_glu, sb_w_qkv, sb_w_o, final_norm, loss_target, m_ffn1_norm, m_ffn1_w1, m_ffn1_w3, m_ffn1_w2, m_mix_norm, m_ffn2_norm, m_ffn2_w1, m_ffn2_w3, m_ffn2_w2, m_ple_norm, m_ple_proj, m_ple_gate, m_s5_w_in, m_s5_a_re, m_s5_a_im, m_s5_log_dt, m_s5_b_re, m_s5_b_im, m_s5_c_re, m_s5_c_im, m_s5_d, m_s5_w_glu, m_sb_w_qkv, m_sb_w_o, m_final_norm, v_ffn1_norm, v_ffn1_w1, v_ffn1_w3, v_ffn1_w2, v_mix_norm, v_ffn2_norm, v_ffn2_w1, v_ffn2_w3, v_ffn2_w2, v_ple_norm, v_ple_proj, v_ple_gate, v_s5_w_in, v_s5_a_re, v_s5_a_im, v_s5_log_dt, v_s5_b_re, v_s5_b_im, v_s5_c_re, v_s5_c_im, v_s5_d, v_s5_w_glu, v_sb_w_qkv, v_sb_w_o, v_final_norm):
    args = dict(locals())
    w = {k: args[k] for k in ORDER}
    m = {k: args["m_" + k] for k in ORDER}
    v = {k: args["v_" + k] for k in ORDER}
    return _step(x, p, loss_target, w, m, v)
```
